```python
import jax, jax.numpy as jnp
from jax import lax
import numpy as np

D_MODEL = 1024
BATCH = 16
SEQ = 4096
DEPTH = 1

D_MIX = D_MODEL
A_HEADS = 8
A_DK = 64
A_DV = 64
A_KWIDTH = A_HEADS * A_DK
A_WIDTH = A_HEADS * A_DV
CHUNK = 64
B_HEADS = 4
B_NOPE = 128
B_ROPE = 64
B_V = 128
B_WIDTH = B_HEADS * B_V
Q_LORA = 384
KV_LORA = 256
ROPE_THETA = 10000.0
Q_BLOCK = 128
D_FF = ((8 * D_MODEL + 3 * 256 - 1) // (3 * 256)) * 256
EPS = 1e-6
IN_SPLITS = (A_KWIDTH, A_WIDTH, A_KWIDTH, A_KWIDTH, A_WIDTH, Q_LORA, KV_LORA, B_ROPE)
D_IN = A_KWIDTH * 3 + A_WIDTH * 2 + Q_LORA + KV_LORA + B_ROPE

kernel_name = 'hybrid_hgrn2_mla_encoder_block'


def _rmsnorm(x, g):
    xf = x.astype(jnp.float32)
    y = xf * lax.rsqrt(jnp.mean(xf * xf, axis=-1, keepdims=True) + EPS)
    return (y * g.astype(jnp.float32)).astype(x.dtype)


def _rope_tables(seq):
    inv = 1.0 / (ROPE_THETA ** (jnp.arange(0, B_ROPE, 2, dtype=jnp.float32) / B_ROPE))
    ang = jnp.arange(seq, dtype=jnp.float32)[:, None] * inv[None, :]
    return jnp.cos(ang), jnp.sin(ang)


def _apply_rope(x, cos, sin):
    xf = x.astype(jnp.float32)
    x1, x2 = jnp.split(xf, 2, axis=-1)
    out = jnp.concatenate([x1 * cos - x2 * sin, x1 * sin + x2 * cos], axis=-1)
    return out.astype(x.dtype)


def _gla_chunkwise(q, k, v, log_f):
    bsz, nh, seq, dk = q.shape
    dv = v.shape[-1]
    n = seq // CHUNK
    q = q.reshape(bsz, nh, n, CHUNK, dk)
    k = k.reshape(bsz, nh, n, CHUNK, dk)
    log_f = log_f.reshape(bsz, nh, n, CHUNK, dk)
    v = v.reshape(bsz, nh, n, CHUNK, dv)
    cum = jnp.cumsum(log_f, axis=3)
    last = cum[:, :, :, -1:, :]
    q_dec = q * jnp.exp(cum)
    k_inv = k * jnp.exp(-cum)
    k_to_end = k * jnp.exp(last - cum)
    mask = jnp.tril(jnp.ones((CHUNK, CHUNK), dtype=bool))
    scores = jnp.einsum('bhnid,bhnjd->bhnij', q_dec, k_inv)
    o_intra = jnp.einsum('bhnij,bhnje->bhnie', jnp.where(mask, scores, 0.0), v)
    u = jnp.einsum('bhnjd,bhnje->bhnde', k_to_end, v)
    decay = jnp.exp(last[:, :, :, 0, :])

    def step(s, xs):
        d, u_n = xs
        return d[..., None] * s + u_n, s

    s0 = jnp.zeros((bsz, nh, dk, dv), dtype=q.dtype)
    _, s_prev = lax.scan(step, s0, (jnp.moveaxis(decay, 2, 0), jnp.moveaxis(u, 2, 0)))
    s_prev = jnp.moveaxis(s_prev, 0, 2)
    o_inter = jnp.einsum('bhnid,bhnde->bhnie', q_dec, s_prev)
    return (o_intra + o_inter).reshape(bsz, nh, seq, dv)


def _hgrn2_group(hq, hi, hf_fwd, hf_bwd, hg, lb, norm_g):
    bsz, seq, _ = hq.shape
    f32 = jnp.float32

    def to_heads(a, d):
        return a.astype(f32).reshape(bsz, seq, A_HEADS, d).transpose(0, 2, 1, 3)

    q = to_heads(jax.nn.silu(hq), A_DK)
    v = to_heads(hi, A_DV)

    def direction(pre, lb_dir, reverse):
        lb_h = lb_dir.astype(f32).reshape(A_HEADS, 1, A_DK)
        z = to_heads(pre, A_DK)
        log_f = jnp.log(lb_h + (1.0 - lb_h) * jax.nn.sigmoid(z))
        k = (1.0 - lb_h) * jax.nn.sigmoid(-z)
        if reverse:
            o = _gla_chunkwise(jnp.flip(q, 2), jnp.flip(k, 2), jnp.flip(v, 2), jnp.flip(log_f, 2))
            return jnp.flip(o, 2)
        return _gla_chunkwise(q, k, v, log_f)

    o = direction(hf_fwd, lb[0], False) + direction(hf_bwd, lb[1], True)
    o = o.transpose(0, 2, 1, 3)
    o = o * lax.rsqrt(jnp.mean(o * o, axis=-1, keepdims=True) + EPS)
    o = o * norm_g.astype(f32).reshape(A_HEADS, A_DV)
    o = o.reshape(bsz, seq, A_WIDTH) * jax.nn.silu(hg.astype(f32))
    return o.astype(hq.dtype)


def _mla_group(c_q, c_kv, k_rope, g_qa, w_qb, g_kva, w_kvb, g_out):
    bsz, seq, _ = c_q.shape
    cos, sin = _rope_tables(seq)
    q = (_rmsnorm(c_q, g_qa) @ w_qb).reshape(bsz, seq, B_HEADS, B_NOPE + B_ROPE)
    q_nope, q_rope = q[..., :B_NOPE], q[..., B_NOPE:]
    kv = (_rmsnorm(c_kv, g_kva) @ w_kvb).reshape(bsz, seq, B_HEADS, B_NOPE + B_V)
    k_nope, v = kv[..., :B_NOPE], kv[..., B_NOPE:]
    q_rope = _apply_rope(q_rope, cos[:, None, :], sin[:, None, :])
    k_rope = _apply_rope(k_rope, cos, sin)
    scale = (B_NOPE + B_ROPE) ** -0.5
    n_blk = seq // Q_BLOCK
    qn_blocks = q_nope.reshape(bsz, n_blk, Q_BLOCK, B_HEADS, B_NOPE).swapaxes(0, 1)
    qr_blocks = q_rope.reshape(bsz, n_blk, Q_BLOCK, B_HEADS, B_ROPE).swapaxes(0, 1)

    def attend(blk):
        qn, qr = blk
        s = (jnp.einsum('bqhd,bkhd->bhqk', qn, k_nope)
             + jnp.einsum('bqhr,bkr->bhqk', qr, k_rope))
        p = jax.nn.softmax(s.astype(jnp.float32) * scale, axis=-1)
        return jnp.einsum('bhqk,bkhe->bqhe', p.astype(v.dtype), v)

    o = lax.map(attend, (qn_blocks, qr_blocks))
    o = o.swapaxes(0, 1).reshape(bsz, seq, B_WIDTH)
    return _rmsnorm(o, g_out)


def _fwd_setup_inputs(seed: int = 0) -> dict:
    key = jax.random.key(seed)
    ks = jax.random.split(key, 20)
    f32 = jnp.float32

    def nrm(k, shape, fan_in):
        return jax.random.normal(k, shape, f32) * (fan_in ** -0.5)

    def gain(k, shape):
        return 1.0 + 0.02 * jax.random.normal(k, shape, f32)

    return {
        'x': jax.random.normal(ks[0], (BATCH, SEQ, D_MODEL), f32),
        'norm1_g': gain(ks[1], (DEPTH, D_MODEL)),
        'w_in': nrm(ks[2], (DEPTH, D_MODEL, D_IN), D_MODEL),
        'lb_logits': 0.1 * jax.random.normal(ks[3], (2, DEPTH + 1, A_KWIDTH), f32),
        'hgrn_norm_g': gain(ks[4], (DEPTH, A_WIDTH)),
        'q_a_norm_g': gain(ks[5], (DEPTH, Q_LORA)),
        'w_q_b': nrm(ks[6], (DEPTH, Q_LORA, B_HEADS * (B_NOPE + B_ROPE)), Q_LORA),
        'kv_a_norm_g': gain(ks[7], (DEPTH, KV_LORA)),
        'w_kv_b': nrm(ks[8], (DEPTH, KV_LORA, B_HEADS * (B_NOPE + B_V)), KV_LORA),
        'mla_norm_g': gain(ks[9], (DEPTH, B_WIDTH)),
        'w_out': nrm(ks[10], (DEPTH, D_MIX, D_MODEL), D_MIX),
        'norm2_g': gain(ks[11], (DEPTH, D_MODEL)),
        'w_gate': nrm(ks[12], (DEPTH, D_MODEL, D_FF), D_MODEL),
        'w_up': nrm(ks[13], (DEPTH, D_MODEL, D_FF), D_MODEL),
        'w_down': nrm(ks[14], (DEPTH, D_FF, D_MODEL), D_FF),
        'final_norm_g': gain(ks[15], (D_MODEL,)),
    }


def _fwd_reference(x, norm1_g, w_in, lb_logits, hgrn_norm_g, q_a_norm_g, w_q_b, kv_a_norm_g,
              w_kv_b, mla_norm_g, w_out, norm2_g, w_gate, w_up, w_down, final_norm_g):
    p = jax.nn.softmax(lb_logits.astype(jnp.float32), axis=1)
    lower_bounds = jnp.cumsum(p, axis=1)[:, :DEPTH]
    split_at = [int(v) for v in np.cumsum(IN_SPLITS)[:-1]]
    for l in range(DEPTH):
        h = _rmsnorm(x, norm1_g[l])
        proj = h @ w_in[l]
        hq, hi, hf_fwd, hf_bwd, hg, c_q, c_kv, k_r = jnp.split(proj, split_at, axis=-1)
        y_a = _hgrn2_group(hq, hi, hf_fwd, hf_bwd, hg, lower_bounds[:, l], hgrn_norm_g[l])
        y_b = _mla_group(c_q, c_kv, k_r, q_a_norm_g[l], w_q_b[l], kv_a_norm_g[l],
                         w_kv_b[l], mla_norm_g[l])
        x = x + jnp.concatenate([y_a, y_b], axis=-1) @ w_out[l]
        h = _rmsnorm(x, norm2_g[l])
        x = x + (jax.nn.silu(h @ w_gate[l]) * (h @ w_up[l])) @ w_down[l]
    return _rmsnorm(x, final_norm_g)


import jax as _jax
import jax.numpy as _jnp

TWIN_FORMAT = 'train_step'
FWD_PARAMS = ['x', 'norm1_g', 'w_in', 'lb_logits', 'hgrn_norm_g', 'q_a_norm_g', 'w_q_b', 'kv_a_norm_g', 'w_kv_b', 'mla_norm_g', 'w_out', 'norm2_g', 'w_gate', 'w_up', 'w_down', 'final_norm_g']
TWIN_WEIGHTS = ['norm1_g', 'w_in', 'lb_logits', 'hgrn_norm_g', 'q_a_norm_g', 'w_q_b', 'kv_a_norm_g', 'w_kv_b', 'mla_norm_g', 'w_out', 'norm2_g', 'w_gate', 'w_up', 'w_down', 'final_norm_g']
TWIN_DIFF_INPUT = 'x'
TWIN_INPUTS = ['x', 'norm1_g', 'w_in', 'lb_logits', 'hgrn_norm_g', 'q_a_norm_g', 'w_q_b', 'kv_a_norm_g', 'w_kv_b', 'mla_norm_g', 'w_out', 'norm2_g', 'w_gate', 'w_up', 'w_down', 'final_norm_g', 'loss_target', 'm_norm1_g', 'm_w_in', 'm_lb_logits', 'm_hgrn_norm_g', 'm_q_a_norm_g', 'm_w_q_b', 'm_kv_a_norm_g', 'm_w_kv_b', 'm_mla_norm_g', 'm_w_out', 'm_norm2_g', 'm_w_gate', 'm_w_up', 'm_w_down', 'm_final_norm_g', 'v_norm1_g', 'v_w_in', 'v_lb_logits', 'v_hgrn_norm_g', 'v_q_a_norm_g', 'v_w_q_b', 'v_kv_a_norm_g', 'v_w_kv_b', 'v_mla_norm_g', 'v_w_out', 'v_norm2_g', 'v_w_gate', 'v_w_up', 'v_w_down', 'v_final_norm_g']
TWIN_OUTPUTS = ['loss', 'grad_x', 'grad_norm1_g', 'grad_w_in', 'grad_lb_logits', 'grad_hgrn_norm_g', 'grad_q_a_norm_g', 'grad_w_q_b', 'grad_kv_a_norm_g', 'grad_w_kv_b', 'grad_mla_norm_g', 'grad_w_out', 'grad_norm2_g', 'grad_w_gate', 'grad_w_up', 'grad_w_down', 'grad_final_norm_g', 'delta_norm1_g', 'delta_w_in', 'delta_lb_logits', 'delta_hgrn_norm_g', 'delta_q_a_norm_g', 'delta_w_q_b', 'delta_kv_a_norm_g', 'delta_w_kv_b', 'delta_mla_norm_g', 'delta_w_out', 'delta_norm2_g', 'delta_w_gate', 'delta_w_up', 'delta_w_down', 'delta_final_norm_g', 'new_m_norm1_g', 'new_m_w_in', 'new_m_lb_logits', 'new_m_hgrn_norm_g', 'new_m_q_a_norm_g', 'new_m_w_q_b', 'new_m_kv_a_norm_g', 'new_m_w_kv_b', 'new_m_mla_norm_g', 'new_m_w_out', 'new_m_norm2_g', 'new_m_w_gate', 'new_m_w_up', 'new_m_w_down', 'new_m_final_norm_g', 'new_v_norm1_g', 'new_v_w_in', 'new_v_lb_logits', 'new_v_hgrn_norm_g', 'new_v_q_a_norm_g', 'new_v_w_q_b', 'new_v_kv_a_norm_g', 'new_v_w_kv_b', 'new_v_mla_norm_g', 'new_v_w_out', 'new_v_norm2_g', 'new_v_w_gate', 'new_v_w_up', 'new_v_w_down', 'new_v_final_norm_g']
TWIN_LEAF_KINDS = {'loss': 'loss', 'grad_x': 'grad_x', 'grad_norm1_g': 'grad_w', 'grad_w_in': 'grad_w', 'grad_lb_logits': 'grad_w', 'grad_hgrn_norm_g': 'grad_w', 'grad_q_a_norm_g': 'grad_w', 'grad_w_q_b': 'grad_w', 'grad_kv_a_norm_g': 'grad_w', 'grad_w_kv_b': 'grad_w', 'grad_mla_norm_g': 'grad_w', 'grad_w_out': 'grad_w', 'grad_norm2_g': 'grad_w', 'grad_w_gate': 'grad_w', 'grad_w_up': 'grad_w', 'grad_w_down': 'grad_w', 'grad_final_norm_g': 'grad_w', 'delta_norm1_g': 'delta_w', 'delta_w_in': 'delta_w', 'delta_lb_logits': 'delta_w', 'delta_hgrn_norm_g': 'delta_w', 'delta_q_a_norm_g': 'delta_w', 'delta_w_q_b': 'delta_w', 'delta_kv_a_norm_g': 'delta_w', 'delta_w_kv_b': 'delta_w', 'delta_mla_norm_g': 'delta_w', 'delta_w_out': 'delta_w', 'delta_norm2_g': 'delta_w', 'delta_w_gate': 'delta_w', 'delta_w_up': 'delta_w', 'delta_w_down': 'delta_w', 'delta_final_norm_g': 'delta_w', 'new_m_norm1_g': 'new_m', 'new_m_w_in': 'new_m', 'new_m_lb_logits': 'new_m', 'new_m_hgrn_norm_g': 'new_m', 'new_m_q_a_norm_g': 'new_m', 'new_m_w_q_b': 'new_m', 'new_m_kv_a_norm_g': 'new_m', 'new_m_w_kv_b': 'new_m', 'new_m_mla_norm_g': 'new_m', 'new_m_w_out': 'new_m', 'new_m_norm2_g': 'new_m', 'new_m_w_gate': 'new_m', 'new_m_w_up': 'new_m', 'new_m_w_down': 'new_m', 'new_m_final_norm_g': 'new_m', 'new_v_norm1_g': 'new_v', 'new_v_w_in': 'new_v', 'new_v_lb_logits': 'new_v', 'new_v_hgrn_norm_g': 'new_v', 'new_v_q_a_norm_g': 'new_v', 'new_v_w_q_b': 'new_v', 'new_v_kv_a_norm_g': 'new_v', 'new_v_w_kv_b': 'new_v', 'new_v_mla_norm_g': 'new_v', 'new_v_w_out': 'new_v', 'new_v_norm2_g': 'new_v', 'new_v_w_gate': 'new_v', 'new_v_w_up': 'new_v', 'new_v_w_down': 'new_v', 'new_v_final_norm_g': 'new_v'}


def _forward(args):
    return _fwd_reference(*[args[k] for k in FWD_PARAMS])


def _output_shape():
    out = _jax.eval_shape(lambda: _forward(_fwd_setup_inputs(0)))
    return out.shape, out.dtype

N_MICROBATCH = 1
ADAM_LR = 0.001
ADAM_B1 = 0.9
ADAM_B2 = 0.999
ADAM_EPS = 1e-08
ADAM_WD = 0.01
ADAM_STEP = 10
PER_EXAMPLE_BATCH_AXIS = {'x': 0, 'loss_target': 0}
SHARED_INPUTS = []
_WEIGHT_DTYPES = {'norm1_g': _jnp.float32, 'w_in': _jnp.float32, 'lb_logits': _jnp.float32, 'hgrn_norm_g': _jnp.float32, 'q_a_norm_g': _jnp.float32, 'w_q_b': _jnp.float32, 'kv_a_norm_g': _jnp.float32, 'w_kv_b': _jnp.float32, 'mla_norm_g': _jnp.float32, 'w_out': _jnp.float32, 'norm2_g': _jnp.float32, 'w_gate': _jnp.float32, 'w_up': _jnp.float32, 'w_down': _jnp.float32, 'final_norm_g': _jnp.float32}
MOMENT_SCALE = {'norm1_g': 2.784504e-01, 'w_in': 1.521679e-01, 'lb_logits': 1.052554e-02, 'hgrn_norm_g': 1.329364e-01, 'q_a_norm_g': 2.404650e-01, 'w_q_b': 1.786647e-01, 'kv_a_norm_g': 8.248711e-01, 'w_kv_b': 2.180221e-01, 'mla_norm_g': 2.165461e-01, 'w_out': 1.751563e-01, 'norm2_g': 1.505375e-01, 'w_gate': 6.542327e-02, 'w_up': 6.387085e-02, 'w_down': 1.054038e-01, 'final_norm_g': 6.386024e+01}


def _to_microbatches(a, axis):
    t = _jnp.moveaxis(a, axis, 0)
    t = t.reshape((N_MICROBATCH, t.shape[0] // N_MICROBATCH) + t.shape[1:])
    return _jnp.moveaxis(t, 1, axis + 1)


def setup_inputs(seed: int = 0) -> dict:
    inp = _fwd_setup_inputs(seed)
    key = _jax.random.fold_in(_jax.random.key(seed), 7919)
    shape, _ = _output_shape()
    out = dict(inp)
    out["loss_target"] = _jax.random.normal(_jax.random.fold_in(key, 0), shape, _jnp.float32)
    for i, name in enumerate(TWIN_WEIGHTS):
        w = inp[name].astype(_jnp.float32)
        if MOMENT_SCALE is None:
            s = _jnp.sqrt(_jnp.mean(_jnp.square(w)) + 1e-30)
        else:
            s = MOMENT_SCALE[name]
        km, kv = _jax.random.split(_jax.random.fold_in(key, i + 1))
        out[name] = w
        out["m_" + name] = s * _jax.random.normal(km, w.shape, _jnp.float32)
        out["v_" + name] = (s * s) * _jax.random.uniform(kv, w.shape, _jnp.float32, 0.5, 1.5)
    if N_MICROBATCH > 1:
        for name, axis in PER_EXAMPLE_BATCH_AXIS.items():
            out[name] = _to_microbatches(out[name], axis)
    return {'x': out['x'], 'norm1_g': out['norm1_g'], 'w_in': out['w_in'], 'lb_logits': out['lb_logits'], 'hgrn_norm_g': out['hgrn_norm_g'], 'q_a_norm_g': out['q_a_norm_g'], 'w_q_b': out['w_q_b'], 'kv_a_norm_g': out['kv_a_norm_g'], 'w_kv_b': out['w_kv_b'], 'mla_norm_g': out['mla_norm_g'], 'w_out': out['w_out'], 'norm2_g': out['norm2_g'], 'w_gate': out['w_gate'], 'w_up': out['w_up'], 'w_down': out['w_down'], 'final_norm_g': out['final_norm_g'], 'loss_target': out['loss_target'], 'm_norm1_g': out['m_norm1_g'], 'm_w_in': out['m_w_in'], 'm_lb_logits': out['m_lb_logits'], 'm_hgrn_norm_g': out['m_hgrn_norm_g'], 'm_q_a_norm_g': out['m_q_a_norm_g'], 'm_w_q_b': out['m_w_q_b'], 'm_kv_a_norm_g': out['m_kv_a_norm_g'], 'm_w_kv_b': out['m_w_kv_b'], 'm_mla_norm_g': out['m_mla_norm_g'], 'm_w_out': out['m_w_out'], 'm_norm2_g': out['m_norm2_g'], 'm_w_gate': out['m_w_gate'], 'm_w_up': out['m_w_up'], 'm_w_down': out['m_w_down'], 'm_final_norm_g': out['m_final_norm_g'], 'v_norm1_g': out['v_norm1_g'], 'v_w_in': out['v_w_in'], 'v_lb_logits': out['v_lb_logits'], 'v_hgrn_norm_g': out['v_hgrn_norm_g'], 'v_q_a_norm_g': out['v_q_a_norm_g'], 'v_w_q_b': out['v_w_q_b'], 'v_kv_a_norm_g': out['v_kv_a_norm_g'], 'v_w_kv_b': out['v_w_kv_b'], 'v_mla_norm_g': out['v_mla_norm_g'], 'v_w_out': out['v_w_out'], 'v_norm2_g': out['v_norm2_g'], 'v_w_gate': out['v_w_gate'], 'v_w_up': out['v_w_up'], 'v_w_down': out['v_w_down'], 'v_final_norm_g': out['v_final_norm_g']}


def _loss(weights, diff, rest, loss_target):
    with _jax.named_scope("forward"):
        args = {**rest, TWIN_DIFF_INPUT: diff, **{k: w.astype(_WEIGHT_DTYPES[k]) for k, w in weights.items()}}
        y = _forward(args)
    with _jax.named_scope("loss_head"):
        err = _jnp.square(y.astype(_jnp.float32) - loss_target)
        return 0.5 * _jnp.sum(_jnp.mean(err, axis=-1)) if err.ndim else 0.5 * err


def _adamw(w, g, m, v):
    m = ADAM_B1 * m + (1.0 - ADAM_B1) * g
    v = ADAM_B2 * v + (1.0 - ADAM_B2) * _jnp.square(g)
    m_hat = m / (1.0 - ADAM_B1 ** ADAM_STEP)
    v_hat = v / (1.0 - ADAM_B2 ** ADAM_STEP)
    delta = -ADAM_LR * (m_hat / (_jnp.sqrt(v_hat) + ADAM_EPS) + ADAM_WD * w)
    return delta, m, v


def reference(x, norm1_g, w_in, lb_logits, hgrn_norm_g, q_a_norm_g, w_q_b, kv_a_norm_g, w_kv_b, mla_norm_g, w_out, norm2_g, w_gate, w_up, w_down, final_norm_g, loss_target, m_norm1_g, m_w_in, m_lb_logits, m_hgrn_norm_g, m_q_a_norm_g, m_w_q_b, m_kv_a_norm_g, m_w_kv_b, m_mla_norm_g, m_w_out, m_norm2_g, m_w_gate, m_w_up, m_w_down, m_final_norm_g, v_norm1_g, v_w_in, v_lb_logits, v_hgrn_norm_g, v_q_a_norm_g, v_w_q_b, v_kv_a_norm_g, v_w_kv_b, v_mla_norm_g, v_w_out, v_norm2_g, v_w_gate, v_w_up, v_w_down, v_final_norm_g):
    given = dict(x=x, norm1_g=norm1_g, w_in=w_in, lb_logits=lb_logits, hgrn_norm_g=hgrn_norm_g, q_a_norm_g=q_a_norm_g, w_q_b=w_q_b, kv_a_norm_g=kv_a_norm_g, w_kv_b=w_kv_b, mla_norm_g=mla_norm_g, w_out=w_out, norm2_g=norm2_g, w_gate=w_gate, w_up=w_up, w_down=w_down, final_norm_g=final_norm_g, loss_target=loss_target, m_norm1_g=m_norm1_g, m_w_in=m_w_in, m_lb_logits=m_lb_logits, m_hgrn_norm_g=m_hgrn_norm_g, m_q_a_norm_g=m_q_a_norm_g, m_w_q_b=m_w_q_b, m_kv_a_norm_g=m_kv_a_norm_g, m_w_kv_b=m_w_kv_b, m_mla_norm_g=m_mla_norm_g, m_w_out=m_w_out, m_norm2_g=m_norm2_g, m_w_gate=m_w_gate, m_w_up=m_w_up, m_w_down=m_w_down, m_final_norm_g=m_final_norm_g, v_norm1_g=v_norm1_g, v_w_in=v_w_in, v_lb_logits=v_lb_logits, v_hgrn_norm_g=v_hgrn_norm_g, v_q_a_norm_g=v_q_a_norm_g, v_w_q_b=v_w_q_b, v_kv_a_norm_g=v_kv_a_norm_g, v_w_kv_b=v_w_kv_b, v_mla_norm_g=v_mla_norm_g, v_w_out=v_w_out, v_norm2_g=v_norm2_g, v_w_gate=v_w_gate, v_w_up=v_w_up, v_w_down=v_w_down, v_final_norm_g=v_final_norm_g)
    weights = {n: given[n] for n in TWIN_WEIGHTS}
    shared = {n: given[n] for n in SHARED_INPUTS}
    per_example = {n: given[n] for n in ['x']}
    grad_fn = _jax.value_and_grad(_loss, argnums=(0, 1))

    def one_microbatch(ex, loss_target):
        ex = dict(ex)
        diff = ex.pop(TWIN_DIFF_INPUT)
        return grad_fn(weights, diff, {**shared, **ex}, loss_target)

    if N_MICROBATCH == 1:
        loss, (grad_w, grad_x) = one_microbatch(per_example, given["loss_target"])
    else:
        def body(carry, xs):
            loss_sum, grad_sum = carry
            l_k, (gw_k, gx_k) = one_microbatch(xs[0], xs[1])
            with _jax.named_scope("update"):
                return (loss_sum + l_k, _jax.tree.map(_jnp.add, grad_sum, gw_k)), gx_k

        init = (_jnp.zeros((), _jnp.float32), _jax.tree.map(_jnp.zeros_like, weights))
        (loss, grad_w), grad_x = _jax.lax.scan(body, init, (per_example, given["loss_target"]))
    with _jax.named_scope("update"):
        delta_w, new_m, new_v = {}, {}, {}
        for n in TWIN_WEIGHTS:
            delta_w[n], new_m[n], new_v[n] = _adamw(weights[n], grad_w[n], given["m_" + n], given["v_" + n])
    return (loss, grad_x, *[grad_w[n] for n in TWIN_WEIGHTS], *[delta_w[n] for n in TWIN_WEIGHTS],
            *[new_m[n] for n in TWIN_WEIGHTS], *[new_v[n] for n in TWIN_WEIGHTS])
```

```python
import functools

import jax
import jax.numpy as jnp
from jax import lax
from jax.experimental import pallas as pl
from jax.experimental.pallas import tpu as pltpu

F32 = jnp.float32
BF16 = jnp.bfloat16

D_MODEL = 1024
A_WIDTH = 512
HEAD_PAIRS = 4
CHUNK = 64
B_HEADS = 4
B_NOPE = 128
B_ROPE = 64
B_V = 128
QK_PAD = 256
Q_LORA = 384
KV_LORA = 256
D_FF = 2816
D_IN = 3264
D_IN_PAD = 3328
IN_WIDTHS = (512, 512, 512, 512, 512, Q_LORA, KV_LORA, 128)
ROPE_THETA = 10000.0
EPS = 1e-6
ATTN_SCALE = (B_NOPE + B_ROPE) ** -0.5

ADAM_LR = 0.001
ADAM_B1 = 0.9
ADAM_B2 = 0.999
ADAM_EPS = 1e-08
ADAM_WD = 0.01
ADAM_STEP = 10

VMEM_LIMIT_BYTES = 60 * 1024 * 1024
N_CHIPS = 4
FLAT_W = 1024
SHARD_ROWS = (("w_in", 816), ("w_q_b", 72), ("w_kv_b", 64), ("w_out", 256),
              ("w_gate", 704), ("w_up", 704), ("w_down", 704))
FLAT_ROWS = 3328
SMALL_ROWS = 56


def _params(n_axes):
    return pltpu.CompilerParams(dimension_semantics=("arbitrary",) * n_axes,
                                vmem_limit_bytes=VMEM_LIMIT_BYTES)


def _dot(a, b):
    return jnp.dot(a, b, preferred_element_type=F32)


def _dot_nt(a, b):
    return lax.dot_general(a, b, (((1,), (1,)), ((), ())), preferred_element_type=F32)


def _dot_tn(a, b):
    return lax.dot_general(a, b, (((0,), (0,)), ((), ())), preferred_element_type=F32)


def _split3(x):
    x1 = x.astype(BF16)
    r = x - x1.astype(F32)
    x2 = r.astype(BF16)
    x3 = (r - x2.astype(F32)).astype(BF16)
    return x1, x2, x3


def _exact_left(m16, x):
    x1, x2, x3 = _split3(x)
    return _dot(m16, x1) + _dot(m16, x2) + _dot(m16, x3)


def _exact_right(x, m16):
    x1, x2, x3 = _split3(x)
    return _dot(x1, m16) + _dot(x2, m16) + _dot(x3, m16)


def _iota2(shape, dim):
    return lax.broadcasted_iota(jnp.int32, shape, dim)


def _sigmoid(x):
    return jax.nn.sigmoid(x)


def _pick(dim, cap, mult=128):
    if dim <= cap:
        return dim
    best = None
    for d in range(mult, cap + 1, mult):
        if dim % d == 0:
            best = d
    assert best is not None, (dim, cap, mult)
    return best


def _row_tile(t, cap=256):
    return _pick(t, cap, 8)


def _full(shape):
    return pl.BlockSpec(shape, lambda *_: (0,) * len(shape))


def _rows(tm, width):
    return pl.BlockSpec((tm, width), lambda i: (i, 0))


def _acc_rows(ref, val, first):
    s = jnp.sum(val, axis=0, keepdims=True)

    @pl.when(first)
    def _():
        ref[...] = s

    @pl.when(jnp.logical_not(first))
    def _():
        ref[...] += s


def _in_fwd(x, g1, w_in_p):
    t = x.shape[0]
    tm = _row_tile(t)

    def body(x_ref, g_ref, w_ref, h_ref, *outs):
        xv = x_ref[...]
        r = lax.rsqrt(jnp.mean(xv * xv, axis=-1, keepdims=True) + EPS)
        h = ((xv * r) * g_ref[...]).astype(BF16)
        h_ref[...] = h
        off = 0
        for o_ref, w in zip(outs, IN_WIDTHS):
            o_ref[...] = _dot(h, w_ref[:, off:off + w])
            off += w

    return pl.pallas_call(
        body, name="in_fwd", grid=(t // tm,),
        in_specs=[_rows(tm, D_MODEL), _full((1, D_MODEL)), _full((D_MODEL, D_IN_PAD))],
        out_specs=[_rows(tm, D_MODEL)] + [_rows(tm, w) for w in IN_WIDTHS],
        out_shape=[jax.ShapeDtypeStruct((t, D_MODEL), BF16)]
        + [jax.ShapeDtypeStruct((t, w), F32) for w in IN_WIDTHS],
        compiler_params=_params(1),
    )(x, g1, w_in_p)


def _in_bwd(dq_f, dq_r, dv_f, dv_r, dz_f, dz_r, dhg, hq, dcq, dckv, dkr, dx2, x, g1, w_in_p):
    t = x.shape[0]
    tm = _row_tile(t)

    def body(dqf_ref, dqr_ref, dvf_ref, dvr_ref, dzf_ref, dzr_ref, dhg_ref, hq_ref, dcq_ref, dckv_ref,
             dkr_ref, dx2_ref, x_ref, g_ref, w_ref, dx_ref, dp_ref, dg_ref):
        hqv = hq_ref[...]
        sg = _sigmoid(hqv)
        dhq = (dqf_ref[...] + dqr_ref[...]) * (sg * (1.0 + hqv * (1.0 - sg)))
        pieces = (dhq, dvf_ref[...] + dvr_ref[...], dzf_ref[...], dzr_ref[...], dhg_ref[...],
                  dcq_ref[...], dckv_ref[...], dkr_ref[...])
        dh = None
        off = 0
        for p, w in zip(pieces, IN_WIDTHS):
            p16 = p.astype(BF16)
            dp_ref[:, off:off + w] = p16
            part = _dot_nt(p16, w_ref[:, off:off + w])
            dh = part if dh is None else dh + part
            off += w
        xv = x_ref[...]
        r = lax.rsqrt(jnp.mean(xv * xv, axis=-1, keepdims=True) + EPS)
        xh = xv * r
        _acc_rows(dg_ref, dh * xh, pl.program_id(0) == 0)
        dxh = dh * g_ref[...]
        dx_ref[...] = dx2_ref[...] + r * (dxh - xh * jnp.mean(dxh * xh, axis=-1, keepdims=True))

    a512 = _rows(tm, A_WIDTH)
    return pl.pallas_call(
        body, name="in_bwd", grid=(t // tm,),
        in_specs=[a512] * 8 + [_rows(tm, Q_LORA), _rows(tm, KV_LORA), _rows(tm, 128), _rows(tm, D_MODEL),
                               _rows(tm, D_MODEL), _full((1, D_MODEL)), _full((D_MODEL, D_IN_PAD))],
        out_specs=[_rows(tm, D_MODEL), _rows(tm, D_IN_PAD), _full((1, D_MODEL))],
        out_shape=[jax.ShapeDtypeStruct((t, D_MODEL), F32), jax.ShapeDtypeStruct((t, D_IN_PAD), BF16),
                   jax.ShapeDtypeStruct((1, D_MODEL), F32)],
        compiler_params=_params(1),
    )(dq_f, dq_r, dv_f, dv_r, dz_f, dz_r, dhg, hq, dcq, dckv, dkr, dx2, x, g1, w_in_p)


def _lower_bound(lbl_ref, direction):
    l0 = lbl_ref[direction, 0:1, :]
    l1 = lbl_ref[direction, 1:2, :]
    m = jnp.maximum(l0, l1)
    e0 = jnp.exp(l0 - m)
    e1 = jnp.exp(l1 - m)
    return e0 / (e0 + e1)


def _hgrn_consts(reverse):
    row = _iota2((CHUNK, CHUNK), 0)
    col = _iota2((CHUNK, CHUNK), 1)
    tri = (col >= row) if reverse else (col <= row)
    r128 = _iota2((128, 128), 0)
    c128 = _iota2((128, 128), 1)
    bd = (r128 < 64) == (c128 < 64)
    lane = _iota2((1, 128), 1)
    m0 = (lane < 64).astype(F32)
    return tri, bd, (m0, 1.0 - m0)


def _hgrn_chunk_fwd(z, hqv, v, lb, tri, tri16, reverse):
    sig = _sigmoid(z)
    sn = _sigmoid(-z)
    q = hqv * _sigmoid(hqv)
    f = lb + (1.0 - lb) * sig
    k = (1.0 - lb) * sn
    cum = _exact_left(tri16, jnp.log(f))
    last = cum[0:1, :] if reverse else cum[CHUNK - 1:CHUNK, :]
    e_neg = jnp.exp(-cum)
    e_end = jnp.exp(last - cum)
    a = jnp.exp(cum)
    return dict(sig=sig, sn=sn, q=q, f=f, k=k, cum=cum, last=last, a=a, e_neg=e_neg, e_end=e_end,
                q_dec=q * a, k_inv=k * e_neg, k_end=k * e_end, d=jnp.exp(last))


def _hgrn_dims(t, n_batch):
    s = t // n_batch
    rb = _pick(s, 256, CHUNK)
    return s, rb, s // rb, rb // CHUNK


def _hgrn_fwd(hq, hi, hf, lbl, *, n_batch, direction):
    t = hq.shape[0]
    reverse = direction == 1
    s, rb, nb, nc = _hgrn_dims(t, n_batch)

    def tmap(b, j):
        return (b * nb + ((nb - 1 - j) if reverse else j), 0)

    def smap(b, j):
        return (b * nb + ((nb - 1 - j) if reverse else j), 0, 0, 0)

    def body(hq_ref, hi_ref, hf_ref, lbl_ref, o_ref, st_ref, st_scr):
        @pl.when(pl.program_id(1) == 0)
        def _():
            st_scr[...] = jnp.zeros_like(st_scr)

        lb_all = _lower_bound(lbl_ref, direction)
        tri, bd, masks = _hgrn_consts(reverse)
        tri16 = tri.astype(BF16)

        def chunk(ci, carry):
            c = (nc - 1 - ci) if reverse else ci
            rows = pl.ds(pl.multiple_of(c * CHUNK, CHUNK), CHUNK)
            for p in range(HEAD_PAIRS):
                ls = slice(p * 128, (p + 1) * 128)
                v = hi_ref[rows, ls]
                w = _hgrn_chunk_fwd(hf_ref[rows, ls], hq_ref[rows, ls], v, lb_all[:, ls], tri, tri16, reverse)
                v16 = v.astype(BF16)
                ki16 = w["k_inv"].astype(BF16)
                st = st_scr[p]
                o = _dot_nt(w["q_dec"].astype(BF16), st.astype(BF16))
                for mh in masks:
                    sc = _dot_nt((w["q_dec"] * mh).astype(BF16), ki16)
                    pm = jnp.where(tri, sc, 0.0).astype(BF16)
                    o = o + _dot(pm, v16) * mh
                o_ref[rows, ls] = o
                st_ref[c, p] = st
                ut = _dot_tn(v16, w["k_end"].astype(BF16))
                st_scr[p] = st * w["d"] + jnp.where(bd, ut, 0.0)
            return carry

        lax.fori_loop(0, nc, chunk, 0)

    blk = pl.BlockSpec((rb, A_WIDTH), tmap)
    return pl.pallas_call(
        body, name=f"hgrn_fwd_{direction}", grid=(n_batch, nb),
        in_specs=[blk, blk, blk, _full((2, 2, A_WIDTH))],
        out_specs=[blk, pl.BlockSpec((nc, HEAD_PAIRS, 128, 128), smap)],
        out_shape=[jax.ShapeDtypeStruct((t, A_WIDTH), F32),
                   jax.ShapeDtypeStruct((t // CHUNK, HEAD_PAIRS, 128, 128), F32)],
        scratch_shapes=[pltpu.VMEM((HEAD_PAIRS, 128, 128), F32)],
        compiler_params=_params(2),
    )(hq, hi, hf, lbl)


def _hgrn_bwd(hq, hi, hf, do, st, lbl, *, n_batch, direction):
    t = hq.shape[0]
    reverse = direction == 1
    s, rb, nb, nc = _hgrn_dims(t, n_batch)

    def tmap(b, j):
        return (b * nb + (j if reverse else (nb - 1 - j)), 0)

    def smap(b, j):
        return (b * nb + (j if reverse else (nb - 1 - j)), 0, 0, 0)

    def body(hq_ref, hi_ref, hf_ref, do_ref, st_ref, lbl_ref, dq_ref, dv_ref, dz_ref, dl_ref, g_scr, dlb_scr):
        b = pl.program_id(0)
        j = pl.program_id(1)

        @pl.when(jnp.logical_and(b == 0, j == 0))
        def _():
            dlb_scr[...] = jnp.zeros_like(dlb_scr)

        @pl.when(j == 0)
        def _():
            g_scr[...] = jnp.zeros_like(g_scr)

        lb_all = _lower_bound(lbl_ref, direction)
        tri, bd, masks = _hgrn_consts(reverse)
        tri16 = tri.astype(BF16)
        tri_t16 = _hgrn_consts(not reverse)[0].astype(BF16)
        last_row = 0 if reverse else CHUNK - 1
        is_last = _iota2((CHUNK, 128), 0) == last_row

        def chunk(ci, carry):
            c = ci if reverse else (nc - 1 - ci)
            rows = pl.ds(pl.multiple_of(c * CHUNK, CHUNK), CHUNK)
            for p in range(HEAD_PAIRS):
                ls = slice(p * 128, (p + 1) * 128)
                lb = lb_all[:, ls]
                v = hi_ref[rows, ls]
                w = _hgrn_chunk_fwd(hf_ref[rows, ls], hq_ref[rows, ls], v, lb, tri, tri16, reverse)
                dov = do_ref[rows, ls]
                st_p = st_ref[c, p]
                g = g_scr[p]
                v16 = v.astype(BF16)
                do16 = dov.astype(BF16)
                qd16 = w["q_dec"].astype(BF16)
                ki16 = w["k_inv"].astype(BF16)
                ke16 = w["k_end"].astype(BF16)
                g16 = g.astype(BF16)
                dq_dec = _dot(do16, st_p.astype(BF16))
                dk_end = _dot(v16, g16)
                dv = _dot_nt(ke16, g16)
                dk_inv = jnp.zeros((CHUNK, 128), F32)
                for mh in masks:
                    sc = _dot_nt((w["q_dec"] * mh).astype(BF16), ki16)
                    pm = jnp.where(tri, sc, 0.0).astype(BF16)
                    dp = jnp.where(tri, _dot_nt((dov * mh).astype(BF16), v16), 0.0).astype(BF16)
                    dv = dv + _dot_tn(pm, do16) * mh
                    dq_dec = dq_dec + _dot(dp, ki16) * mh
                    dk_inv = dk_inv + _dot_tn(dp, qd16) * mh
                dd = jnp.sum(g * st_p, axis=0, keepdims=True)
                g_scr[p] = g * w["d"] + jnp.where(bd, _dot_tn(do16, qd16), 0.0)
                dcum = dq_dec * w["q_dec"] - dk_inv * w["k_inv"] - dk_end * w["k_end"]
                dk = dk_inv * w["e_neg"] + dk_end * w["e_end"]
                dlast = jnp.sum(dk_end * w["k_end"], axis=0, keepdims=True) + dd * w["d"]
                dcum = dcum + jnp.where(is_last, dlast, 0.0)
                dlf = _exact_left(tri_t16, dcum)
                tt = dlf / w["f"] - dk
                dq_ref[rows, ls] = dq_dec * w["a"]
                dv_ref[rows, ls] = dv
                dz_ref[rows, ls] = (1.0 - lb) * w["sig"] * w["sn"] * tt
                dlb_scr[:, ls] += jnp.sum(w["sn"] * tt, axis=0, keepdims=True)
            return carry

        lax.fori_loop(0, nc, chunk, 0)

        @pl.when(jnp.logical_and(b == pl.num_programs(0) - 1, j == pl.num_programs(1) - 1))
        def _():
            d0 = dlb_scr[...] * lb_all * (1.0 - lb_all)
            dl_ref[0:1, :] = d0
            dl_ref[1:2, :] = -d0

    blk = pl.BlockSpec((rb, A_WIDTH), tmap)
    return pl.pallas_call(
        body, name=f"hgrn_bwd_{direction}", grid=(n_batch, nb),
        in_specs=[blk, blk, blk, blk, pl.BlockSpec((nc, HEAD_PAIRS, 128, 128), smap), _full((2, 2, A_WIDTH))],
        out_specs=[blk, blk, blk, _full((2, A_WIDTH))],
        out_shape=[jax.ShapeDtypeStruct((t, A_WIDTH), F32)] * 3 + [jax.ShapeDtypeStruct((2, A_WIDTH), F32)],
        scratch_shapes=[pltpu.VMEM((HEAD_PAIRS, 128, 128), F32), pltpu.VMEM((1, A_WIDTH), F32)],
        compiler_params=_params(2),
    )(hq, hi, hf, do, st, lbl)


def _swap_rope_halves(x):
    lane = _iota2(x.shape, 1)
    return jnp.where(lane < 32, pltpu.roll(x, 96, 1), pltpu.roll(x, 32, 1))


def _rms_fwd(xv, g):
    r = lax.rsqrt(jnp.mean(xv * xv, axis=-1, keepdims=True) + EPS)
    return (xv * r) * g


def _rms_bwd(dy, xv, g):
    r = lax.rsqrt(jnp.mean(xv * xv, axis=-1, keepdims=True) + EPS)
    xh = xv * r
    dxh = dy * g
    return r * (dxh - xh * jnp.mean(dxh * xh, axis=-1, keepdims=True)), dy * xh


def _mla_prep(cq, ckv, kr, g_qa, g_kva, w_q_p, w_kv_p, cosx, sinx, *, n_batch):
    t = cq.shape[0]
    s = t // n_batch
    tm = _row_tile(s)
    nt = s // tm

    def body(cq_ref, ckv_ref, kr_ref, gq_ref, gkv_ref, wq_ref, wkv_ref, cos_ref, sin_ref,
             q_ref, k_ref, v_ref, cqn_ref, ckvn_ref):
        cos, sin = cos_ref[...], sin_ref[...]
        cqn = _rms_fwd(cq_ref[...], gq_ref[...]).astype(BF16)
        ckvn = _rms_fwd(ckv_ref[...], gkv_ref[...]).astype(BF16)
        cqn_ref[...] = cqn
        ckvn_ref[...] = ckvn
        krv = kr_ref[...]
        kr_roped = (krv * cos + _swap_rope_halves(krv) * sin).astype(BF16)
        for h in range(B_HEADS):
            o = h * QK_PAD
            q_ref[:, o:o + 128] = _dot(cqn, wq_ref[:, o:o + 128]).astype(BF16)
            qr = _dot(cqn, wq_ref[:, o + 128:o + 256])
            q_ref[:, o + 128:o + 256] = (qr * cos + _swap_rope_halves(qr) * sin).astype(BF16)
            k_ref[:, o:o + 128] = _dot(ckvn, wkv_ref[:, h * 128:(h + 1) * 128]).astype(BF16)
            k_ref[:, o + 128:o + 256] = kr_roped
        v_ref[...] = _dot(ckvn, wkv_ref[:, 512:1024]).astype(BF16)

    tab = pl.BlockSpec((tm, 128), lambda i: (i % nt, 0))
    return pl.pallas_call(
        body, name="mla_prep", grid=(t // tm,),
        in_specs=[_rows(tm, Q_LORA), _rows(tm, KV_LORA), _rows(tm, 128), _full((1, Q_LORA)), _full((1, KV_LORA)),
                  _full((Q_LORA, 1024)), _full((KV_LORA, 1024)), tab, tab],
        out_specs=[_rows(tm, 1024), _rows(tm, 1024), _rows(tm, 512), _rows(tm, Q_LORA), _rows(tm, KV_LORA)],
        out_shape=[jax.ShapeDtypeStruct((t, 1024), BF16), jax.ShapeDtypeStruct((t, 1024), BF16),
                   jax.ShapeDtypeStruct((t, 512), BF16), jax.ShapeDtypeStruct((t, Q_LORA), BF16),
                   jax.ShapeDtypeStruct((t, KV_LORA), BF16)],
        compiler_params=_params(1),
    )(cq, ckv, kr, g_qa, g_kva, w_q_p, w_kv_p, cosx, sinx)


def _mla_prep_bwd(dq, dk, dv, cq, ckv, g_qa, g_kva, w_q_p, w_kv_p, cosx, sinx, *, n_batch):
    t = cq.shape[0]
    s = t // n_batch
    tm = _row_tile(s)
    nt = s // tm

    def body(dq_ref, dk_ref, dv_ref, cq_ref, ckv_ref, gq_ref, gkv_ref, wq_ref, wkv_ref, cos_ref, sin_ref,
             dcq_ref, dckv_ref, dkr_ref, dqp_ref, dkvp_ref, dgq_ref, dgkv_ref):
        cos, sin = cos_ref[...], sin_ref[...]
        first = pl.program_id(0) == 0

        def unrope(d):
            return d * cos + _swap_rope_halves(d * sin)

        dcqn = None
        dkr = None
        dckvn = None
        for h in range(B_HEADS):
            o = h * QK_PAD
            dqn16 = dq_ref[:, o:o + 128].astype(BF16)
            dqr16 = unrope(dq_ref[:, o + 128:o + 256]).astype(BF16)
            dqp_ref[:, o:o + 128] = dqn16
            dqp_ref[:, o + 128:o + 256] = dqr16
            part = _dot_nt(dqn16, wq_ref[:, o:o + 128]) + _dot_nt(dqr16, wq_ref[:, o + 128:o + 256])
            dcqn = part if dcqn is None else dcqn + part
            dkn16 = dk_ref[:, o:o + 128].astype(BF16)
            dkvp_ref[:, h * 128:(h + 1) * 128] = dkn16
            part = _dot_nt(dkn16, wkv_ref[:, h * 128:(h + 1) * 128])
            dckvn = part if dckvn is None else dckvn + part
            kr_part = dk_ref[:, o + 128:o + 256]
            dkr = kr_part if dkr is None else dkr + kr_part
        dv16 = dv_ref[...].astype(BF16)
        dkvp_ref[:, 512:1024] = dv16
        dckvn = dckvn + _dot_nt(dv16, wkv_ref[:, 512:1024])
        dkr_ref[...] = unrope(dkr)
        dcq, dgq = _rms_bwd(dcqn, cq_ref[...], gq_ref[...])
        dckv, dgkv = _rms_bwd(dckvn, ckv_ref[...], gkv_ref[...])
        dcq_ref[...] = dcq
        dckv_ref[...] = dckv
        _acc_rows(dgq_ref, dgq, first)
        _acc_rows(dgkv_ref, dgkv, first)

    tab = pl.BlockSpec((tm, 128), lambda i: (i % nt, 0))
    return pl.pallas_call(
        body, name="mla_prep_bwd", grid=(t // tm,),
        in_specs=[_rows(tm, 1024), _rows(tm, 1024), _rows(tm, 512), _rows(tm, Q_LORA), _rows(tm, KV_LORA),
                  _full((1, Q_LORA)), _full((1, KV_LORA)), _full((Q_LORA, 1024)), _full((KV_LORA, 1024)), tab, tab],
        out_specs=[_rows(tm, Q_LORA), _rows(tm, KV_LORA), _rows(tm, 128), _rows(tm, 1024), _rows(tm, 1024),
                   _full((1, Q_LORA)), _full((1, KV_LORA))],
        out_shape=[jax.ShapeDtypeStruct((t, Q_LORA), F32), jax.ShapeDtypeStruct((t, KV_LORA), F32),
                   jax.ShapeDtypeStruct((t, 128), F32), jax.ShapeDtypeStruct((t, 1024), BF16),
                   jax.ShapeDtypeStruct((t, 1024), BF16), jax.ShapeDtypeStruct((1, Q_LORA), F32),
                   jax.ShapeDtypeStruct((1, KV_LORA), F32)],
        compiler_params=_params(1),
    )(dq, dk, dv, cq, ckv, g_qa, g_kva, w_q_p, w_kv_p, cosx, sinx)


def _attn_dims(t, n_batch):
    s = t // n_batch
    tq = _pick(s, 512, 128)
    return s, tq, s // tq


def _attn_fwd(q, k, v, *, n_batch):
    t = q.shape[0]
    s, tq, nq = _attn_dims(t, n_batch)
    tk, nk = tq, nq

    def body(q_ref, k_ref, v_ref, o_ref, lse_ref, m_scr, l_scr, acc_scr):
        j = pl.program_id(3)

        @pl.when(j == 0)
        def _():
            m_scr[...] = jnp.full_like(m_scr, -jnp.inf)
            l_scr[...] = jnp.zeros_like(l_scr)
            acc_scr[...] = jnp.zeros_like(acc_scr)

        sc = _dot_nt(q_ref[...], k_ref[...]) * ATTN_SCALE
        m_old = m_scr[...]
        m_new = jnp.maximum(m_old, jnp.max(sc, axis=-1, keepdims=True))
        alpha = jnp.exp(m_old - m_new)
        p = jnp.exp(sc - m_new)
        l_scr[...] = alpha * l_scr[...] + jnp.sum(p, axis=-1, keepdims=True)
        acc_scr[...] = alpha * acc_scr[...] + _dot(p.astype(BF16), v_ref[...])
        m_scr[...] = m_new

        @pl.when(j == nk - 1)
        def _():
            o_ref[...] = acc_scr[...] / l_scr[...]
            lse_ref[...] = m_scr[...] + jnp.log(l_scr[...])

    return pl.pallas_call(
        body, name="attn_fwd", grid=(n_batch, B_HEADS, nq, nk),
        in_specs=[pl.BlockSpec((tq, QK_PAD), lambda b, h, i, j: (b * nq + i, h)),
                  pl.BlockSpec((tk, QK_PAD), lambda b, h, i, j: (b * nk + j, h)),
                  pl.BlockSpec((tk, B_V), lambda b, h, i, j: (b * nk + j, h))],
        out_specs=[pl.BlockSpec((tq, B_V), lambda b, h, i, j: (b * nq + i, h)),
                   pl.BlockSpec((None, tq, 1), lambda b, h, i, j: (h, b * nq + i, 0))],
        out_shape=[jax.ShapeDtypeStruct((t, B_HEADS * B_V), F32), jax.ShapeDtypeStruct((B_HEADS, t, 1), F32)],
        scratch_shapes=[pltpu.VMEM((tq, 1), F32), pltpu.VMEM((tq, 1), F32), pltpu.VMEM((tq, B_V), F32)],
        compiler_params=_params(4),
    )(q, k, v)


def _attn_bwd(q, k, v, o, do, lse, *, n_batch):
    t = q.shape[0]
    s, tq, nq = _attn_dims(t, n_batch)
    tk, nk = tq, nq

    def body(q_ref, k_ref, v_ref, o_ref, do_ref, lse_ref, dq_ref, dk_ref, dv_ref, dk_scr, dv_scr):
        j = pl.program_id(2)
        i = pl.program_id(3)

        @pl.when(jnp.logical_and(j == 0, i == 0))
        def _():
            dq_ref[...] = jnp.zeros_like(dq_ref)

        @pl.when(i == 0)
        def _():
            dk_scr[...] = jnp.zeros_like(dk_scr)
            dv_scr[...] = jnp.zeros_like(dv_scr)

        qv, kv = q_ref[...], k_ref[...]
        dov = do_ref[...]
        delta = jnp.sum(dov * o_ref[...], axis=-1, keepdims=True)
        do16 = dov.astype(BF16)
        p = jnp.exp(_dot_nt(qv, kv) * ATTN_SCALE - lse_ref[...])
        dv_scr[...] += _dot_tn(p.astype(BF16), do16)
        dp = _dot_nt(do16, v_ref[...])
        ds = (p * (dp - delta) * ATTN_SCALE).astype(BF16)
        dk_scr[...] += _dot_tn(ds, qv)
        rows = pl.ds(pl.multiple_of(i * tq, tq), tq)
        dq_ref[rows, :] += _dot(ds, kv)

        @pl.when(i == nq - 1)
        def _():
            dk_ref[...] = dk_scr[...]
            dv_ref[...] = dv_scr[...]

    return pl.pallas_call(
        body, name="attn_bwd", grid=(n_batch, B_HEADS, nk, nq),
        in_specs=[pl.BlockSpec((tq, QK_PAD), lambda b, h, j, i: (b * nq + i, h)),
                  pl.BlockSpec((tk, QK_PAD), lambda b, h, j, i: (b * nk + j, h)),
                  pl.BlockSpec((tk, B_V), lambda b, h, j, i: (b * nk + j, h)),
                  pl.BlockSpec((tq, B_V), lambda b, h, j, i: (b * nq + i, h)),
                  pl.BlockSpec((tq, B_V), lambda b, h, j, i: (b * nq + i, h)),
                  pl.BlockSpec((None, tq, 1), lambda b, h, j, i: (h, b * nq + i, 0))],
        out_specs=[pl.BlockSpec((s, QK_PAD), lambda b, h, j, i: (b, h)),
                   pl.BlockSpec((tk, QK_PAD), lambda b, h, j, i: (b * nk + j, h)),
                   pl.BlockSpec((tk, B_V), lambda b, h, j, i: (b * nk + j, h))],
        out_shape=[jax.ShapeDtypeStruct((t, B_HEADS * QK_PAD), F32), jax.ShapeDtypeStruct((t, B_HEADS * QK_PAD), F32),
                   jax.ShapeDtypeStruct((t, B_HEADS * B_V), F32)],
        scratch_shapes=[pltpu.VMEM((tk, QK_PAD), F32), pltpu.VMEM((tk, B_V), F32)],
        compiler_params=_params(4),
    )(q, k, v, o, do, lse)


def _group_ones16():
    r = _iota2((A_WIDTH, A_WIDTH), 0) // 64
    c = _iota2((A_WIDTH, A_WIDTH), 1) // 64
    return (r == c).astype(BF16)


def _head_rms(o, ones16):
    return lax.rsqrt(_exact_right(o * o, ones16) * (1.0 / 64.0) + EPS)


def _out_fwd(o_f, o_r, hg, o_attn, x, g_hn, g_mla, w_out, g2):
    t = x.shape[0]
    tm = _row_tile(t)

    def body(of_ref, or_ref, hg_ref, oa_ref, x_ref, ghn_ref, gm_ref, w_ref, g2_ref, y_ref, x2_ref, h2_ref):
        ones16 = _group_ones16()
        o = of_ref[...] + or_ref[...]
        hgv = hg_ref[...]
        ya = ((o * _head_rms(o, ones16)) * ghn_ref[...]) * (hgv * _sigmoid(hgv))
        yb = _rms_fwd(oa_ref[...], gm_ref[...])
        ya16, yb16 = ya.astype(BF16), yb.astype(BF16)
        y_ref[:, 0:A_WIDTH] = ya16
        y_ref[:, A_WIDTH:D_MODEL] = yb16
        x2 = x_ref[...] + _dot(ya16, w_ref[0:A_WIDTH, :]) + _dot(yb16, w_ref[A_WIDTH:D_MODEL, :])
        x2_ref[...] = x2
        h2_ref[...] = _rms_fwd(x2, g2_ref[...]).astype(BF16)

    a512 = _rows(tm, A_WIDTH)
    return pl.pallas_call(
        body, name="out_fwd", grid=(t // tm,),
        in_specs=[a512, a512, a512, a512, _rows(tm, D_MODEL), _full((1, A_WIDTH)), _full((1, A_WIDTH)),
                  _full((D_MODEL, D_MODEL)), _full((1, D_MODEL))],
        out_specs=[_rows(tm, D_MODEL)] * 3,
        out_shape=[jax.ShapeDtypeStruct((t, D_MODEL), BF16), jax.ShapeDtypeStruct((t, D_MODEL), F32),
                   jax.ShapeDtypeStruct((t, D_MODEL), BF16)],
        compiler_params=_params(1),
    )(o_f, o_r, hg, o_attn, x, g_hn, g_mla, w_out, g2)


def _out_bwd(dx2_16, o_f, o_r, hg, o_attn, g_hn, g_mla, w_out):
    t = dx2_16.shape[0]
    tm = _row_tile(t)

    def body(dx_ref, of_ref, or_ref, hg_ref, oa_ref, ghn_ref, gm_ref, w_ref,
             do_ref, dhg_ref, doa_ref, dghn_ref, dgm_ref):
        first = pl.program_id(0) == 0
        ones16 = _group_ones16()
        dxv = dx_ref[...]
        dya = _dot_nt(dxv, w_ref[0:A_WIDTH, :])
        dyb = _dot_nt(dxv, w_ref[A_WIDTH:D_MODEL, :])
        o = of_ref[...] + or_ref[...]
        rh = _head_rms(o, ones16)
        oh = o * rh
        hgv = hg_ref[...]
        sg = _sigmoid(hgv)
        sl = hgv * sg
        ghn = ghn_ref[...]
        dhg_ref[...] = (dya * (oh * ghn)) * (sg * (1.0 + hgv * (1.0 - sg)))
        _acc_rows(dghn_ref, dya * sl * oh, first)
        doh = dya * sl * ghn
        do_ref[...] = rh * (doh - oh * (_exact_right(doh * oh, ones16) * (1.0 / 64.0)))
        doa, dgm = _rms_bwd(dyb, oa_ref[...], gm_ref[...])
        doa_ref[...] = doa
        _acc_rows(dgm_ref, dgm, first)

    a512 = _rows(tm, A_WIDTH)
    return pl.pallas_call(
        body, name="out_bwd", grid=(t // tm,),
        in_specs=[_rows(tm, D_MODEL), a512, a512, a512, a512, _full((1, A_WIDTH)), _full((1, A_WIDTH)),
                  _full((D_MODEL, D_MODEL))],
        out_specs=[a512, a512, a512, _full((1, A_WIDTH)), _full((1, A_WIDTH))],
        out_shape=[jax.ShapeDtypeStruct((t, A_WIDTH), F32)] * 3 + [jax.ShapeDtypeStruct((1, A_WIDTH), F32)] * 2,
        compiler_params=_params(1),
    )(dx2_16, o_f, o_r, hg, o_attn, g_hn, g_mla, w_out)


def _ffn_fwd_bwd(h2, x2, target, w_gate, w_up, w_down, g_f, g2):
    t = x2.shape[0]
    tm = _row_tile(t, 128)
    inv_d = 1.0 / D_MODEL

    def body(h2_ref, x2_ref, tg_ref, wg_ref, wu_ref, wd_ref, gf_ref, g2_ref,
             act_ref, dgate_ref, dup_ref, dx3_ref, dx2_ref, dx2h_ref, loss_ref, dgf_ref, dg2_ref):
        first = pl.program_id(0) == 0
        h2v = h2_ref[...]
        gate = _dot(h2v, wg_ref[...])
        up = _dot(h2v, wu_ref[...])
        sg = _sigmoid(gate)
        sl = gate * sg
        act16 = (sl * up).astype(BF16)
        act_ref[...] = act16
        x2v = x2_ref[...]
        x3 = x2v + _dot(act16, wd_ref[...])
        r3 = lax.rsqrt(jnp.mean(x3 * x3, axis=-1, keepdims=True) + EPS)
        x3h = x3 * r3
        gf = gf_ref[...]
        err = x3h * gf - tg_ref[...]
        part = 0.5 * jnp.sum(jnp.mean(err * err, axis=-1, keepdims=True), axis=0, keepdims=True)

        @pl.when(first)
        def _():
            loss_ref[...] = jnp.zeros_like(loss_ref)

        loss_ref[...] += part
        dy = err * inv_d
        _acc_rows(dgf_ref, dy * x3h, first)
        dx3h = dy * gf
        dx3 = r3 * (dx3h - x3h * jnp.mean(dx3h * x3h, axis=-1, keepdims=True))
        dx3_16 = dx3.astype(BF16)
        dx3_ref[...] = dx3_16
        da = _dot_nt(dx3_16, wd_ref[...])
        dup16 = (da * sl).astype(BF16)
        dgate16 = (da * up * (sg * (1.0 + gate * (1.0 - sg)))).astype(BF16)
        dup_ref[...] = dup16
        dgate_ref[...] = dgate16
        dh2 = _dot_nt(dgate16, wg_ref[...]) + _dot_nt(dup16, wu_ref[...])
        dx2n, dg2 = _rms_bwd(dh2, x2v, g2_ref[...])
        _acc_rows(dg2_ref, dg2, first)
        dx2 = dx3 + dx2n
        dx2_ref[...] = dx2
        dx2h_ref[...] = dx2.astype(BF16)

    return pl.pallas_call(
        body, name="ffn_fwd_bwd", grid=(t // tm,),
        in_specs=[_rows(tm, D_MODEL), _rows(tm, D_MODEL), _rows(tm, D_MODEL), _full((D_MODEL, D_FF)),
                  _full((D_MODEL, D_FF)), _full((D_FF, D_MODEL)), _full((1, D_MODEL)), _full((1, D_MODEL))],
        out_specs=[_rows(tm, D_FF), _rows(tm, D_FF), _rows(tm, D_FF), _rows(tm, D_MODEL), _rows(tm, D_MODEL),
                   _rows(tm, D_MODEL), _full((8, 128)), _full((1, D_MODEL)), _full((1, D_MODEL))],
        out_shape=[jax.ShapeDtypeStruct((t, D_FF), BF16)] * 3
        + [jax.ShapeDtypeStruct((t, D_MODEL), BF16), jax.ShapeDtypeStruct((t, D_MODEL), F32),
           jax.ShapeDtypeStruct((t, D_MODEL), BF16), jax.ShapeDtypeStruct((8, 128), F32),
           jax.ShapeDtypeStruct((1, D_MODEL), F32), jax.ShapeDtypeStruct((1, D_MODEL), F32)],
        compiler_params=_params(1),
    )(h2, x2, target, w_gate, w_up, w_down, g_f, g2)


def _wgrad(a, b, name):
    t, m = a.shape
    n = b.shape[1]
    bm = _pick(m, 512)
    bn = _pick(n, 1664)
    tk = _pick(t, 512, 16)
    nk = t // tk

    def body(a_ref, b_ref, o_ref):
        part = _dot_tn(a_ref[...], b_ref[...])

        @pl.when(pl.program_id(2) == 0)
        def _():
            o_ref[...] = part

        @pl.when(pl.program_id(2) != 0)
        def _():
            o_ref[...] += part

    return pl.pallas_call(
        body, name=name, grid=(m // bm, n // bn, nk),
        in_specs=[pl.BlockSpec((tk, bm), lambda i, j, k: (k, i)), pl.BlockSpec((tk, bn), lambda i, j, k: (k, j))],
        out_specs=pl.BlockSpec((bm, bn), lambda i, j, k: (i, j)),
        out_shape=jax.ShapeDtypeStruct((m, n), F32),
        compiler_params=_params(3),
    )(a, b)


def _rope_tables(seq):
    inv = 1.0 / (ROPE_THETA ** (jnp.arange(0, B_ROPE, 2, dtype=F32) / B_ROPE))
    ang = jnp.arange(seq, dtype=F32)[:, None] * inv[None, :]
    cos, sin = jnp.cos(ang), jnp.sin(ang)
    zeros = jnp.zeros((seq, 64), F32)
    return jnp.concatenate([cos, cos, zeros], axis=1), jnp.concatenate([-sin, sin, zeros], axis=1)


def _pad_weights(w_in, w_q_b, w_kv_b):
    w_in_p = jnp.pad(w_in, ((0, 0), (0, D_IN_PAD - D_IN)))
    w_q_p = jnp.pad(w_q_b.reshape(Q_LORA, B_HEADS, B_NOPE + B_ROPE), ((0, 0), (0, 0), (0, 64))).reshape(Q_LORA, 1024)
    kv = w_kv_b.reshape(KV_LORA, B_HEADS, B_NOPE + B_V)
    w_kv_p = jnp.concatenate([kv[:, :, :B_NOPE].reshape(KV_LORA, 512), kv[:, :, B_NOPE:].reshape(KV_LORA, 512)], axis=1)
    return w_in_p, w_q_p, w_kv_p


def _unpad_grads(g_in_p, g_q_p, g_kv_p):
    g_in = g_in_p[:, :D_IN]
    g_q = g_q_p.reshape(Q_LORA, B_HEADS, QK_PAD)[:, :, :B_NOPE + B_ROPE].reshape(Q_LORA, B_HEADS * (B_NOPE + B_ROPE))
    g_kv = jnp.concatenate([g_kv_p[:, :512].reshape(KV_LORA, B_HEADS, B_NOPE),
                            g_kv_p[:, 512:].reshape(KV_LORA, B_HEADS, B_V)], axis=2).reshape(KV_LORA, 1024)
    return g_in, g_q, g_kv


def _local_step(x, target, lbl, g1, g_hn, g_qa, g_kva, g_mla, g2, g_f, w_in, w_q_b, w_kv_b, w_out, w_gate, w_up, w_down):
    n_batch, seq, _ = x.shape
    t = n_batch * seq
    x = x.reshape(t, D_MODEL)
    target = target.reshape(t, D_MODEL)
    w_in_p, w_q_p, w_kv_p = _pad_weights(w_in, w_q_b, w_kv_b)
    cosx, sinx = _rope_tables(seq)

    h1, hq, hi, hff, hfb, hg, cq, ckv, kr = _in_fwd(x, g1, w_in_p)
    o_f, st_f = _hgrn_fwd(hq, hi, hff, lbl, n_batch=n_batch, direction=0)
    o_r, st_r = _hgrn_fwd(hq, hi, hfb, lbl, n_batch=n_batch, direction=1)
    q, k, v, cqn, ckvn = _mla_prep(cq, ckv, kr, g_qa, g_kva, w_q_p, w_kv_p, cosx, sinx, n_batch=n_batch)
    o_attn, lse = _attn_fwd(q, k, v, n_batch=n_batch)
    ycat, x2, h2 = _out_fwd(o_f, o_r, hg, o_attn, x, g_hn, g_mla, w_out, g2)
    act, dgate, dup, dx3_16, dx2, dx2_16, loss, dg_f, dg2 = _ffn_fwd_bwd(h2, x2, target, w_gate, w_up, w_down, g_f, g2)
    do_h, dhg, do_attn, dg_hn, dg_mla = _out_bwd(dx2_16, o_f, o_r, hg, o_attn, g_hn, g_mla, w_out)
    dq, dk, dv = _attn_bwd(q, k, v, o_attn, do_attn, lse, n_batch=n_batch)
    dcq, dckv, dkr, dqp16, dkvp16, dg_qa, dg_kva = _mla_prep_bwd(dq, dk, dv, cq, ckv, g_qa, g_kva, w_q_p, w_kv_p,
                                                                  cosx, sinx, n_batch=n_batch)
    dq_f, dv_f, dz_f, dl_f = _hgrn_bwd(hq, hi, hff, do_h, st_f, lbl, n_batch=n_batch, direction=0)
    dq_r, dv_r, dz_r, dl_r = _hgrn_bwd(hq, hi, hfb, do_h, st_r, lbl, n_batch=n_batch, direction=1)
    dx, dproj16, dg1 = _in_bwd(dq_f, dq_r, dv_f, dv_r, dz_f, dz_r, dhg, hq, dcq, dckv, dkr, dx2, x, g1, w_in_p)

    gw_in, gw_q, gw_kv = _unpad_grads(_wgrad(h1, dproj16, "wgrad_in"), _wgrad(cqn, dqp16, "wgrad_q_b"),
                                      _wgrad(ckvn, dkvp16, "wgrad_kv_b"))
    grads = dict(
        norm1_g=dg1, w_in=gw_in, lb_logits=jnp.stack([dl_f, dl_r]), hgrn_norm_g=dg_hn, q_a_norm_g=dg_qa,
        w_q_b=gw_q, kv_a_norm_g=dg_kva, w_kv_b=gw_kv, mla_norm_g=dg_mla, w_out=_wgrad(ycat, dx2_16, "wgrad_out"),
        norm2_g=dg2, w_gate=_wgrad(h2, dgate, "wgrad_gate"), w_up=_wgrad(h2, dup, "wgrad_up"),
        w_down=_wgrad(act, dx3_16, "wgrad_down"), final_norm_g=dg_f)
    return loss[0, 0], dx.reshape(n_batch, seq, D_MODEL), grads


_HBM = pl.BlockSpec(memory_space=pltpu.HBM)
_MESH = pl.DeviceIdType.MESH


def _place():
    x, y, c = lax.axis_index("x"), lax.axis_index("y"), lax.axis_index("c")
    other_chips = [(1 - x, y), (x, 1 - y), (1 - x, 1 - y)]
    return x, y, c, other_chips


def _gather_weights(wflat16, lb8):
    def body(w_ref, lb_ref, wall_ref, lball_ref, send_sems, recv_sems, local_sems):
        x, y, c, chips = _place()
        mine = 2 * x + y

        def copies(j, chip_index, to):
            def mk(src, dst, n):
                return pltpu.make_async_remote_copy(src_ref=src, dst_ref=dst.at[chip_index], send_sem=send_sems.at[n],
                                                    recv_sem=recv_sems.at[n], device_id=to, device_id_type=_MESH)
            return mk(w_ref, wall_ref, j), mk(lb_ref, lball_ref, 3 + j)

        local = [pltpu.make_async_copy(w_ref, wall_ref.at[mine], local_sems.at[0]),
                 pltpu.make_async_copy(lb_ref, lball_ref.at[mine], local_sems.at[1])]
        for cp in local:
            cp.start()
        sends = []
        for j, chip in enumerate(chips):
            sends += copies(j, mine, (*chip, c))
        for cp in sends:
            cp.start()
        for j, (px, py) in enumerate(chips):
            for cp in copies(j, 2 * px + py, (x, y, c)):
                cp.wait_recv()
        for cp in sends:
            cp.wait_send()
        for cp in local:
            cp.wait()

    return pl.pallas_call(
        body, name="gather_weights", in_specs=[_HBM, _HBM], out_specs=[_HBM, _HBM],
        out_shape=[jax.ShapeDtypeStruct((N_CHIPS,) + wflat16.shape, BF16), jax.ShapeDtypeStruct((N_CHIPS,) + lb8.shape, F32)],
        scratch_shapes=[pltpu.SemaphoreType.DMA((6,)), pltpu.SemaphoreType.DMA((6,)), pltpu.SemaphoreType.DMA((2,))],
    )(wflat16, lb8)


def _scatter_grads(g16, small):
    def body(g_ref, s_ref, recv_ref, sall_ref, send_sems, recv_sems, local_sem):
        x, y, c, chips = _place()
        me = 4 * x + 2 * y + c
        flips = [(fx, fy, fc) for fx in (0, 1) for fy in (0, 1) for fc in (0, 1)][1:]

        def peer(f):
            return tuple((1 - a) if b else a for a, b in zip((x, y, c), f))

        def seg(j, src_index, to):
            return pltpu.make_async_remote_copy(src_ref=g_ref.at[src_index], dst_ref=recv_ref.at[j], send_sem=send_sems.at[j],
                                                recv_sem=recv_sems.at[j], device_id=to, device_id_type=_MESH)

        def sm(r, index, to):
            return pltpu.make_async_remote_copy(src_ref=s_ref, dst_ref=sall_ref.at[index], send_sem=send_sems.at[3 + r],
                                                recv_sem=recv_sems.at[3 + r], device_id=to, device_id_type=_MESH)

        local = pltpu.make_async_copy(s_ref, sall_ref.at[me], local_sem)
        local.start()
        sends = [seg(j, 2 * px + py, (px, py, c)) for j, (px, py) in enumerate(chips)]
        sends += [sm(r, me, peer(f)) for r, f in enumerate(flips)]
        for cp in sends:
            cp.start()
        for j in range(3):
            seg(j, 0, (x, y, c)).wait_recv()
        for r, f in enumerate(flips):
            px, py, pc = peer(f)
            sm(r, 4 * px + 2 * py + pc, (x, y, c)).wait_recv()
        for cp in sends:
            cp.wait_send()
        local.wait()

    return pl.pallas_call(
        body, name="scatter_grads", in_specs=[_HBM, _HBM], out_specs=[_HBM, _HBM],
        out_shape=[jax.ShapeDtypeStruct((3,) + g16.shape[1:], BF16), jax.ShapeDtypeStruct((8,) + small.shape, F32)],
        scratch_shapes=[pltpu.SemaphoreType.DMA((10,)), pltpu.SemaphoreType.DMA((10,)), pltpu.SemaphoreType.DMA(())],
    )(g16, small)


def _swap_sibling(p):
    def body(p_ref, o_ref, send_sem, recv_sem):
        x, y, c, _ = _place()
        cp = pltpu.make_async_remote_copy(src_ref=p_ref, dst_ref=o_ref, send_sem=send_sem, recv_sem=recv_sem,
                                          device_id=(x, y, 1 - c), device_id_type=_MESH)
        cp.start()
        cp.wait()

    return pl.pallas_call(
        body, name="swap_sibling", in_specs=[_HBM], out_specs=_HBM, out_shape=jax.ShapeDtypeStruct(p.shape, p.dtype),
        scratch_shapes=[pltpu.SemaphoreType.DMA(()), pltpu.SemaphoreType.DMA(())],
    )(p)


def _sum_segments(own, recv):
    rows = own.shape[0]
    tm = _row_tile(rows)

    def body(o_ref, r_ref, out_ref):
        acc = o_ref[...]
        for j in range(3):
            acc = acc + r_ref[j].astype(F32)
        out_ref[...] = acc

    return pl.pallas_call(
        body, name="sum_segments", grid=(rows // tm,),
        in_specs=[_rows(tm, FLAT_W), pl.BlockSpec((3, tm, FLAT_W), lambda i: (0, i, 0))],
        out_specs=_rows(tm, FLAT_W), out_shape=jax.ShapeDtypeStruct(own.shape, F32), compiler_params=_params(1),
    )(own, recv)


def _sum_devices(sall):
    def body(s_ref, o_ref):
        acc = s_ref[0]
        for d in range(1, 8):
            acc = acc + s_ref[d]
        o_ref[...] = acc

    return pl.pallas_call(body, name="sum_devices", out_shape=jax.ShapeDtypeStruct(sall.shape[1:], F32))(sall)


def _adamw(w, m, v, ga, gb, name):
    rows, cols = w.shape
    tm = _row_tile(rows)
    two = gb is not None

    def body(*refs):
        w_ref, m_ref, v_ref, ga_ref = refs[:4]
        g_ref, d_ref, m2_ref, v2_ref = refs[-4:]
        g = ga_ref[...] + refs[4][...] if two else ga_ref[...]
        m2 = ADAM_B1 * m_ref[...] + (1.0 - ADAM_B1) * g
        v2 = ADAM_B2 * v_ref[...] + (1.0 - ADAM_B2) * (g * g)
        m_hat = m2 / (1.0 - ADAM_B1 ** ADAM_STEP)
        v_hat = v2 / (1.0 - ADAM_B2 ** ADAM_STEP)
        g_ref[...] = g
        d_ref[...] = -ADAM_LR * (m_hat / (jnp.sqrt(v_hat) + ADAM_EPS) + ADAM_WD * w_ref[...])
        m2_ref[...] = m2
        v2_ref[...] = v2

    blk = _rows(tm, cols)
    args = (w, m, v, ga) + ((gb,) if two else ())
    return pl.pallas_call(
        body, name=name, grid=(rows // tm,), in_specs=[blk] * len(args), out_specs=[blk] * 4,
        out_shape=[jax.ShapeDtypeStruct(w.shape, F32)] * 4, compiler_params=_params(1),
    )(*args)


_COLUMN_SHARDED = ("w_in", "w_q_b", "w_kv_b", "w_gate", "w_up")
_FULL_SHAPES = dict(w_in=(D_MODEL, D_IN), w_q_b=(Q_LORA, 768), w_kv_b=(KV_LORA, 1024), w_out=(D_MODEL, D_MODEL),
                    w_gate=(D_MODEL, D_FF), w_up=(D_MODEL, D_FF), w_down=(D_FF, D_MODEL))
_SMALL = (("norm1_g", 1024), ("lb_logits", 2048), ("hgrn_norm_g", 512), ("q_a_norm_g", 384), ("kv_a_norm_g", 256),
          ("mla_norm_g", 512), ("norm2_g", 1024), ("final_norm_g", 1024))
_UPDATE_ROWS = 48


def _pad_rows(a, rows):
    return jnp.pad(a, ((0, rows - a.shape[0]), (0, 0)))


def _flatten_shards(shards):
    return _pad_rows(jnp.concatenate([shards[n].reshape(r, FLAT_W) for n, r in SHARD_ROWS], axis=0), FLAT_ROWS)


def _split_flat(flat):
    out, off = {}, 0
    for n, r in SHARD_ROWS:
        out[n] = flat[off:off + r]
        off += r
    return out


def _to_segments(name, g):
    rows = dict(SHARD_ROWS)[name]
    if name in _COLUMN_SHARDED:
        r, c = g.shape
        g = g.reshape(r, N_CHIPS, c // N_CHIPS).transpose(1, 0, 2)
    return g.reshape(N_CHIPS, rows, FLAT_W)


def _from_segments(name, seg):
    r, c = _FULL_SHAPES[name]
    if name in _COLUMN_SHARDED:
        return seg.reshape(N_CHIPS, r, c // N_CHIPS).transpose(1, 0, 2).reshape(r, c)
    return seg.reshape(r, c)


def kernel(x, norm1_g, w_in, lb_logits, hgrn_norm_g, q_a_norm_g, w_q_b, kv_a_norm_g, w_kv_b, mla_norm_g, w_out, norm2_g, w_gate, w_up, w_down, final_norm_g, loss_target, m_norm1_g, m_w_in, m_lb_logits, m_hgrn_norm_g, m_q_a_norm_g, m_w_q_b, m_kv_a_norm_g, m_w_kv_b, m_mla_norm_g, m_w_out, m_norm2_g, m_w_gate, m_w_up, m_w_down, m_final_norm_g, v_norm1_g, v_w_in, v_lb_logits, v_hgrn_norm_g, v_q_a_norm_g, v_w_q_b, v_kv_a_norm_g, v_w_kv_b, v_mla_norm_g, v_w_out, v_norm2_g, v_w_gate, v_w_up, v_w_down, v_final_norm_g):
    names = ("norm1_g", "w_in", "lb_logits", "hgrn_norm_g", "q_a_norm_g", "w_q_b", "kv_a_norm_g", "w_kv_b", "mla_norm_g",
             "w_out", "norm2_g", "w_gate", "w_up", "w_down", "final_norm_g")
    w = dict(zip(names, (norm1_g, w_in, lb_logits, hgrn_norm_g, q_a_norm_g, w_q_b, kv_a_norm_g, w_kv_b, mla_norm_g,
                         w_out, norm2_g, w_gate, w_up, w_down, final_norm_g)))
    m = dict(zip(names, (m_norm1_g, m_w_in, m_lb_logits, m_hgrn_norm_g, m_q_a_norm_g, m_w_q_b, m_kv_a_norm_g, m_w_kv_b,
                         m_mla_norm_g, m_w_out, m_norm2_g, m_w_gate, m_w_up, m_w_down, m_final_norm_g)))
    v = dict(zip(names, (v_norm1_g, v_w_in, v_lb_logits, v_hgrn_norm_g, v_q_a_norm_g, v_w_q_b, v_kv_a_norm_g, v_w_kv_b,
                         v_mla_norm_g, v_w_out, v_norm2_g, v_w_gate, v_w_up, v_w_down, v_final_norm_g)))
    matrices = tuple(n for n, _ in SHARD_ROWS)
    chip = 2 * lax.axis_index("x") + lax.axis_index("y")

    wflat16 = _flatten_shards({n: w[n][0].astype(BF16) for n in matrices})
    lb8 = _pad_rows(lb_logits.reshape(4, 128), 8)
    wall, lball = _gather_weights(wflat16, lb8)
    full = {n: _from_segments(n, seg) for n, seg in zip(matrices, (wall[:, o:o + r] for o, r in _offsets()))}
    lbl = lball[:, :4].reshape(N_CHIPS, 2, 2, 128).transpose(1, 2, 0, 3).reshape(2, 2, A_WIDTH)

    loss_part, grad_x, g = _local_step(
        x, loss_target, lbl, norm1_g, hgrn_norm_g, q_a_norm_g, kv_a_norm_g, mla_norm_g, norm2_g, final_norm_g[None, :],
        full["w_in"], full["w_q_b"], full["w_kv_b"], full["w_out"], full["w_gate"], full["w_up"], full["w_down"])
    loss = lax.psum(loss_part, ("x", "y", "c"))

    g32 = jnp.concatenate([_to_segments(n, g[n]) for n in matrices]
                          + [jnp.zeros((N_CHIPS, FLAT_ROWS - 3320, FLAT_W), F32)], axis=1)
    small = _pad_rows(jnp.concatenate([g[n].reshape(-1) for n, _ in _SMALL]).reshape(-1, 128), SMALL_ROWS)
    recv, small_all = _scatter_grads(g32.astype(BF16), small)
    own = lax.dynamic_index_in_dim(g32, chip, axis=0, keepdims=False)
    part = _sum_segments(own, recv)
    part_sib = _swap_sibling(part)
    small_sum = _sum_devices(small_all).reshape(-1)

    out = {}
    pa, pb = _split_flat(part), _split_flat(part_sib)
    for n in matrices:
        shp = w[n].shape[1:]
        res = _adamw(w[n][0], m[n][0], v[n][0], pa[n].reshape(shp), pb[n].reshape(shp), f"adamw_{n}")
        out[n] = tuple(r[None] for r in res)

    small_g, off = {}, 0
    for n, size in _SMALL:
        small_g[n] = small_sum[off:off + size]
        off += size
    small_g["lb_logits"] = lax.dynamic_slice_in_dim(small_g["lb_logits"].reshape(2, 2, A_WIDTH), chip * 128, 128, axis=2)
    small_names = tuple(n for n, _ in _SMALL)

    def pack(d):
        return _pad_rows(jnp.concatenate([d[n].reshape(-1) for n in small_names]).reshape(-1, 128), _UPDATE_ROWS)

    res = _adamw(pack(w), pack(m), pack(v), pack(small_g), None, "adamw_small")
    off = 0
    flat = [r.reshape(-1) for r in res]
    for n in small_names:
        size = w[n].size
        out[n] = tuple(f[off:off + size].reshape(w[n].shape) for f in flat)
        off += size

    return (loss, grad_x) + tuple(out[n][i] for i in range(4) for n in names)


def _offsets():
    off, res = 0, []
    for _, r in SHARD_ROWS:
        res.append((off, r))
        off += r
    return res
```

```python
import functools

import jax
import jax.numpy as jnp
from jax import lax
from jax.experimental import pallas as pl
from jax.experimental.pallas import tpu as pltpu

F32 = jnp.float32
BF16 = jnp.bfloat16

D_MODEL = 1024
A_WIDTH = 512
HEAD_PAIRS = 4
CHUNK = 64
B_HEADS = 4
B_NOPE = 128
B_ROPE = 64
B_V = 128
QK_PAD = 256
Q_LORA = 384
KV_LORA = 256
D_FF = 2816
D_IN = 3264
D_IN_PAD = 3328
IN_WIDTHS = (512, 512, 512, 512, 512, Q_LORA, KV_LORA, 128)
ROPE_THETA = 10000.0
EPS = 1e-6
ATTN_SCALE = (B_NOPE + B_ROPE) ** -0.5

ADAM_LR = 0.001
ADAM_B1 = 0.9
ADAM_B2 = 0.999
ADAM_EPS = 1e-08
ADAM_WD = 0.01
ADAM_STEP = 10

VMEM_LIMIT_BYTES = 60 * 1024 * 1024
N_CHIPS = 4
FLAT_W = 1024
SHARD_ROWS = (("w_in", 816), ("w_q_b", 72), ("w_kv_b", 64), ("w_out", 256),
              ("w_gate", 704), ("w_up", 704), ("w_down", 704))
FLAT_ROWS = 3328
SMALL_ROWS = 56


def _params(n_axes):
    return pltpu.CompilerParams(dimension_semantics=("arbitrary",) * n_axes,
                                vmem_limit_bytes=VMEM_LIMIT_BYTES)


def _dot(a, b):
    return jnp.dot(a, b, preferred_element_type=F32)


def _dot_nt(a, b):
    return lax.dot_general(a, b, (((1,), (1,)), ((), ())), preferred_element_type=F32)


def _dot_tn(a, b):
    return lax.dot_general(a, b, (((0,), (0,)), ((), ())), preferred_element_type=F32)


def _split3(x):
    x1 = x.astype(BF16)
    r = x - x1.astype(F32)
    x2 = r.astype(BF16)
    x3 = (r - x2.astype(F32)).astype(BF16)
    return x1, x2, x3


def _exact_left(m16, x):
    x1, x2, x3 = _split3(x)
    return _dot(m16, x1) + _dot(m16, x2) + _dot(m16, x3)


def _exact_right(x, m16):
    x1, x2, x3 = _split3(x)
    return _dot(x1, m16) + _dot(x2, m16) + _dot(x3, m16)


def _iota2(shape, dim):
    return lax.broadcasted_iota(jnp.int32, shape, dim)


def _sigmoid(x):
    return jax.nn.sigmoid(x)


def _pick(dim, cap, mult=128):
    if dim <= cap:
        return dim
    best = None
    for d in range(mult, cap + 1, mult):
        if dim % d == 0:
            best = d
    assert best is not None, (dim, cap, mult)
    return best


def _row_tile(t, cap=256):
    return _pick(t, cap, 8)


def _full(shape, single=False):
    if single:
        return pl.BlockSpec(shape, lambda *_: (0,) * len(shape), pipeline_mode=pl.Buffered(1))
    return pl.BlockSpec(shape, lambda *_: (0,) * len(shape))


def _rows(tm, width):
    return pl.BlockSpec((tm, width), lambda i: (i, 0))


def _acc_rows(ref, val, first):
    s = jnp.sum(val, axis=0, keepdims=True)

    @pl.when(first)
    def _():
        ref[...] = s

    @pl.when(jnp.logical_not(first))
    def _():
        ref[...] += s


def _in_fwd(x, g1, w_in_p):
    t = x.shape[0]
    tm = _row_tile(t)

    def body(x_ref, g_ref, w_ref, h_ref, *outs):
        xv = x_ref[...]
        r = lax.rsqrt(jnp.mean(xv * xv, axis=-1, keepdims=True) + EPS)
        h = ((xv * r) * g_ref[...]).astype(BF16)
        h_ref[...] = h
        off = 0
        for o_ref, w in zip(outs, IN_WIDTHS):
            o_ref[...] = _dot(h, w_ref[:, off:off + w])
            off += w

    return pl.pallas_call(
        body, name="in_fwd", grid=(t // tm,),
        in_specs=[_rows(tm, D_MODEL), _full((1, D_MODEL)), _full((D_MODEL, D_IN_PAD))],
        out_specs=[_rows(tm, D_MODEL)] + [_rows(tm, w) for w in IN_WIDTHS],
        out_shape=[jax.ShapeDtypeStruct((t, D_MODEL), BF16)]
        + [jax.ShapeDtypeStruct((t, w), F32) for w in IN_WIDTHS],
        compiler_params=_params(1),
    )(x, g1, w_in_p)


def _in_bwd(dq_f, dq_r, dv_f, dv_r, dz_f, dz_r, dhg, hq, dcq, dckv, dkr, dx2, x, g1, w_in_p):
    t = x.shape[0]
    tm = _row_tile(t)

    def body(dqf_ref, dqr_ref, dvf_ref, dvr_ref, dzf_ref, dzr_ref, dhg_ref, hq_ref, dcq_ref, dckv_ref,
             dkr_ref, dx2_ref, x_ref, g_ref, w_ref, dx_ref, dp_ref, dg_ref):
        hqv = hq_ref[...]
        sg = _sigmoid(hqv)
        dhq = (dqf_ref[...] + dqr_ref[...]) * (sg * (1.0 + hqv * (1.0 - sg)))
        pieces = (dhq, dvf_ref[...] + dvr_ref[...], dzf_ref[...], dzr_ref[...], dhg_ref[...],
                  dcq_ref[...], dckv_ref[...], dkr_ref[...])
        dh = None
        off = 0
        for p, w in zip(pieces, IN_WIDTHS):
            p16 = p.astype(BF16)
            dp_ref[:, off:off + w] = p16
            part = _dot_nt(p16, w_ref[:, off:off + w])
            dh = part if dh is None else dh + part
            off += w
        xv = x_ref[...]
        r = lax.rsqrt(jnp.mean(xv * xv, axis=-1, keepdims=True) + EPS)
        xh = xv * r
        _acc_rows(dg_ref, dh * xh, pl.program_id(0) == 0)
        dxh = dh * g_ref[...]
        dx_ref[...] = dx2_ref[...] + r * (dxh - xh * jnp.mean(dxh * xh, axis=-1, keepdims=True))

    a512 = _rows(tm, A_WIDTH)
    return pl.pallas_call(
        body, name="in_bwd", grid=(t // tm,),
        in_specs=[a512] * 8 + [_rows(tm, Q_LORA), _rows(tm, KV_LORA), _rows(tm, 128), _rows(tm, D_MODEL),
                               _rows(tm, D_MODEL), _full((1, D_MODEL)), _full((D_MODEL, D_IN_PAD))],
        out_specs=[_rows(tm, D_MODEL), _rows(tm, D_IN_PAD), _full((1, D_MODEL))],
        out_shape=[jax.ShapeDtypeStruct((t, D_MODEL), F32), jax.ShapeDtypeStruct((t, D_IN_PAD), BF16),
                   jax.ShapeDtypeStruct((1, D_MODEL), F32)],
        compiler_params=_params(1),
    )(dq_f, dq_r, dv_f, dv_r, dz_f, dz_r, dhg, hq, dcq, dckv, dkr, dx2, x, g1, w_in_p)


def _lower_bound(lbl_ref, direction):
    l0 = lbl_ref[direction, 0:1, :]
    l1 = lbl_ref[direction, 1:2, :]
    m = jnp.maximum(l0, l1)
    e0 = jnp.exp(l0 - m)
    e1 = jnp.exp(l1 - m)
    return e0 / (e0 + e1)


def _hgrn_consts(reverse):
    row = _iota2((CHUNK, CHUNK), 0)
    col = _iota2((CHUNK, CHUNK), 1)
    tri = (col >= row) if reverse else (col <= row)
    r128 = _iota2((128, 128), 0)
    c128 = _iota2((128, 128), 1)
    bd = (r128 < 64) == (c128 < 64)
    lane = _iota2((1, 128), 1)
    m0 = (lane < 64).astype(F32)
    return tri, bd, (m0, 1.0 - m0)


def _hgrn_chunk_fwd(z, hqv, v, lb, tri, tri16, reverse):
    sig = _sigmoid(z)
    sn = _sigmoid(-z)
    q = hqv * _sigmoid(hqv)
    f = lb + (1.0 - lb) * sig
    k = (1.0 - lb) * sn
    cum = _exact_left(tri16, jnp.log(f))
    last = cum[0:1, :] if reverse else cum[CHUNK - 1:CHUNK, :]
    e_neg = jnp.exp(-cum)
    e_end = jnp.exp(last - cum)
    a = jnp.exp(cum)
    return dict(sig=sig, sn=sn, q=q, f=f, k=k, cum=cum, last=last, a=a, e_neg=e_neg, e_end=e_end,
                q_dec=q * a, k_inv=k * e_neg, k_end=k * e_end, d=jnp.exp(last))


def _hgrn_dims(t, n_batch):
    s = t // n_batch
    rb = _pick(s, 256, CHUNK)
    return s, rb, s // rb, rb // CHUNK


def _hgrn_fwd(hq, hi, hf, lbl, *, n_batch, direction):
    t = hq.shape[0]
    reverse = direction == 1
    s, rb, nb, nc = _hgrn_dims(t, n_batch)

    def tmap(b, j):
        return (b * nb + ((nb - 1 - j) if reverse else j), 0)

    def smap(b, j):
        return (b * nb + ((nb - 1 - j) if reverse else j), 0, 0, 0)

    def body(hq_ref, hi_ref, hf_ref, lbl_ref, o_ref, st_ref, st_scr):
        @pl.when(pl.program_id(1) == 0)
        def _():
            st_scr[...] = jnp.zeros_like(st_scr)

        lb_all = _lower_bound(lbl_ref, direction)
        tri, bd, masks = _hgrn_consts(reverse)
        tri16 = tri.astype(BF16)

        def chunk(ci, carry):
            c = (nc - 1 - ci) if reverse else ci
            rows = pl.ds(pl.multiple_of(c * CHUNK, CHUNK), CHUNK)
            for p in range(HEAD_PAIRS):
                ls = slice(p * 128, (p + 1) * 128)
                v = hi_ref[rows, ls]
                w = _hgrn_chunk_fwd(hf_ref[rows, ls], hq_ref[rows, ls], v, lb_all[:, ls], tri, tri16, reverse)
                v16 = v.astype(BF16)
                ki16 = w["k_inv"].astype(BF16)
                st = st_scr[p]
                o = _dot_nt(w["q_dec"].astype(BF16), st.astype(BF16))
                for mh in masks:
                    sc = _dot_nt((w["q_dec"] * mh).astype(BF16), ki16)
                    pm = jnp.where(tri, sc, 0.0).astype(BF16)
                    o = o + _dot(pm, v16) * mh
                o_ref[rows, ls] = o
                st_ref[c, p] = st
                ut = _dot_tn(v16, w["k_end"].astype(BF16))
                st_scr[p] = st * w["d"] + jnp.where(bd, ut, 0.0)
            return carry

        lax.fori_loop(0, nc, chunk, 0)

    blk = pl.BlockSpec((rb, A_WIDTH), tmap)
    return pl.pallas_call(
        body, name=f"hgrn_fwd_{direction}", grid=(n_batch, nb),
        in_specs=[blk, blk, blk, _full((2, 2, A_WIDTH))],
        out_specs=[blk, pl.BlockSpec((nc, HEAD_PAIRS, 128, 128), smap)],
        out_shape=[jax.ShapeDtypeStruct((t, A_WIDTH), F32),
                   jax.ShapeDtypeStruct((t // CHUNK, HEAD_PAIRS, 128, 128), F32)],
        scratch_shapes=[pltpu.VMEM((HEAD_PAIRS, 128, 128), F32)],
        compiler_params=_params(2),
    )(hq, hi, hf, lbl)


def _hgrn_bwd(hq, hi, hf, do, st, lbl, *, n_batch, direction):
    t = hq.shape[0]
    reverse = direction == 1
    s, rb, nb, nc = _hgrn_dims(t, n_batch)

    def tmap(b, j):
        return (b * nb + (j if reverse else (nb - 1 - j)), 0)

    def smap(b, j):
        return (b * nb + (j if reverse else (nb - 1 - j)), 0, 0, 0)

    def body(hq_ref, hi_ref, hf_ref, do_ref, st_ref, lbl_ref, dq_ref, dv_ref, dz_ref, dl_ref, g_scr, dlb_scr):
        b = pl.program_id(0)
        j = pl.program_id(1)

        @pl.when(jnp.logical_and(b == 0, j == 0))
        def _():
            dlb_scr[...] = jnp.zeros_like(dlb_scr)

        @pl.when(j == 0)
        def _():
            g_scr[...] = jnp.zeros_like(g_scr)

        lb_all = _lower_bound(lbl_ref, direction)
        tri, bd, masks = _hgrn_consts(reverse)
        tri16 = tri.astype(BF16)
        tri_t16 = _hgrn_consts(not reverse)[0].astype(BF16)
        last_row = 0 if reverse else CHUNK - 1
        is_last = _iota2((CHUNK, 128), 0) == last_row

        def chunk(ci, carry):
            c = ci if reverse else (nc - 1 - ci)
            rows = pl.ds(pl.multiple_of(c * CHUNK, CHUNK), CHUNK)
            for p in range(HEAD_PAIRS):
                ls = slice(p * 128, (p + 1) * 128)
                lb = lb_all[:, ls]
                v = hi_ref[rows, ls]
                w = _hgrn_chunk_fwd(hf_ref[rows, ls], hq_ref[rows, ls], v, lb, tri, tri16, reverse)
                dov = do_ref[rows, ls]
                st_p = st_ref[c, p]
                g = g_scr[p]
                v16 = v.astype(BF16)
                do16 = dov.astype(BF16)
                qd16 = w["q_dec"].astype(BF16)
                ki16 = w["k_inv"].astype(BF16)
                ke16 = w["k_end"].astype(BF16)
                g16 = g.astype(BF16)
                dq_dec = _dot(do16, st_p.astype(BF16))
                dk_end = _dot(v16, g16)
                dv = _dot_nt(ke16, g16)
                dk_inv = jnp.zeros((CHUNK, 128), F32)
                for mh in masks:
                    sc = _dot_nt((w["q_dec"] * mh).astype(BF16), ki16)
                    pm = jnp.where(tri, sc, 0.0).astype(BF16)
                    dp = jnp.where(tri, _dot_nt((dov * mh).astype(BF16), v16), 0.0).astype(BF16)
                    dv = dv + _dot_tn(pm, do16) * mh
                    dq_dec = dq_dec + _dot(dp, ki16) * mh
                    dk_inv = dk_inv + _dot_tn(dp, qd16) * mh
                dd = jnp.sum(g * st_p, axis=0, keepdims=True)
                g_scr[p] = g * w["d"] + jnp.where(bd, _dot_tn(do16, qd16), 0.0)
                dcum = dq_dec * w["q_dec"] - dk_inv * w["k_inv"] - dk_end * w["k_end"]
                dk = dk_inv * w["e_neg"] + dk_end * w["e_end"]
                dlast = jnp.sum(dk_end * w["k_end"], axis=0, keepdims=True) + dd * w["d"]
                dcum = dcum + jnp.where(is_last, dlast, 0.0)
                dlf = _exact_left(tri_t16, dcum)
                tt = dlf / w["f"] - dk
                dq_ref[rows, ls] = dq_dec * w["a"]
                dv_ref[rows, ls] = dv
                dz_ref[rows, ls] = (1.0 - lb) * w["sig"] * w["sn"] * tt
                dlb_scr[:, ls] += jnp.sum(w["sn"] * tt, axis=0, keepdims=True)
            return carry

        lax.fori_loop(0, nc, chunk, 0)

        @pl.when(jnp.logical_and(b == pl.num_programs(0) - 1, j == pl.num_programs(1) - 1))
        def _():
            d0 = dlb_scr[...] * lb_all * (1.0 - lb_all)
            dl_ref[0:1, :] = d0
            dl_ref[1:2, :] = -d0

    blk = pl.BlockSpec((rb, A_WIDTH), tmap)
    return pl.pallas_call(
        body, name=f"hgrn_bwd_{direction}", grid=(n_batch, nb),
        in_specs=[blk, blk, blk, blk, pl.BlockSpec((nc, HEAD_PAIRS, 128, 128), smap), _full((2, 2, A_WIDTH))],
        out_specs=[blk, blk, blk, _full((2, A_WIDTH))],
        out_shape=[jax.ShapeDtypeStruct((t, A_WIDTH), F32)] * 3 + [jax.ShapeDtypeStruct((2, A_WIDTH), F32)],
        scratch_shapes=[pltpu.VMEM((HEAD_PAIRS, 128, 128), F32), pltpu.VMEM((1, A_WIDTH), F32)],
        compiler_params=_params(2),
    )(hq, hi, hf, do, st, lbl)


def _swap_rope_halves(x):
    lane = _iota2(x.shape, 1)
    return jnp.where(lane < 32, pltpu.roll(x, 96, 1), pltpu.roll(x, 32, 1))


def _rms_fwd(xv, g):
    r = lax.rsqrt(jnp.mean(xv * xv, axis=-1, keepdims=True) + EPS)
    return (xv * r) * g


def _rms_bwd(dy, xv, g):
    r = lax.rsqrt(jnp.mean(xv * xv, axis=-1, keepdims=True) + EPS)
    xh = xv * r
    dxh = dy * g
    return r * (dxh - xh * jnp.mean(dxh * xh, axis=-1, keepdims=True)), dy * xh


def _mla_prep(cq, ckv, kr, g_qa, g_kva, w_q_p, w_kv_p, cosx, sinx, *, n_batch):
    t = cq.shape[0]
    s = t // n_batch
    tm = _row_tile(s)
    nt = s // tm

    def body(cq_ref, ckv_ref, kr_ref, gq_ref, gkv_ref, wq_ref, wkv_ref, cos_ref, sin_ref,
             q_ref, k_ref, v_ref, cqn_ref, ckvn_ref):
        cos, sin = cos_ref[...], sin_ref[...]
        cqn = _rms_fwd(cq_ref[...], gq_ref[...]).astype(BF16)
        ckvn = _rms_fwd(ckv_ref[...], gkv_ref[...]).astype(BF16)
        cqn_ref[...] = cqn
        ckvn_ref[...] = ckvn
        krv = kr_ref[...]
        kr_roped = (krv * cos + _swap_rope_halves(krv) * sin).astype(BF16)
        for h in range(B_HEADS):
            o = h * QK_PAD
            q_ref[:, o:o + 128] = _dot(cqn, wq_ref[:, o:o + 128]).astype(BF16)
            qr = _dot(cqn, wq_ref[:, o + 128:o + 256])
            q_ref[:, o + 128:o + 256] = (qr * cos + _swap_rope_halves(qr) * sin).astype(BF16)
            k_ref[:, o:o + 128] = _dot(ckvn, wkv_ref[:, h * 128:(h + 1) * 128]).astype(BF16)
            k_ref[:, o + 128:o + 256] = kr_roped
        v_ref[...] = _dot(ckvn, wkv_ref[:, 512:1024]).astype(BF16)

    tab = pl.BlockSpec((tm, 128), lambda i: (i % nt, 0))
    return pl.pallas_call(
        body, name="mla_prep", grid=(t // tm,),
        in_specs=[_rows(tm, Q_LORA), _rows(tm, KV_LORA), _rows(tm, 128), _full((1, Q_LORA)), _full((1, KV_LORA)),
                  _full((Q_LORA, 1024)), _full((KV_LORA, 1024)), tab, tab],
        out_specs=[_rows(tm, 1024), _rows(tm, 1024), _rows(tm, 512), _rows(tm, Q_LORA), _rows(tm, KV_LORA)],
        out_shape=[jax.ShapeDtypeStruct((t, 1024), BF16), jax.ShapeDtypeStruct((t, 1024), BF16),
                   jax.ShapeDtypeStruct((t, 512), BF16), jax.ShapeDtypeStruct((t, Q_LORA), BF16),
                   jax.ShapeDtypeStruct((t, KV_LORA), BF16)],
        compiler_params=_params(1),
    )(cq, ckv, kr, g_qa, g_kva, w_q_p, w_kv_p, cosx, sinx)


def _mla_prep_bwd(dqt, dk, dv, cq, ckv, g_qa, g_kva, w_q_p, w_kv_p, cosx, sinx, *, n_batch):
    t = cq.shape[0]
    s = t // n_batch
    tm = _row_tile(s)
    nt = s // tm

    def body(dqt_ref, dk_ref, dv_ref, cq_ref, ckv_ref, gq_ref, gkv_ref, wq_ref, wkv_ref, cos_ref, sin_ref,
             dcq_ref, dckv_ref, dkr_ref, dqp_ref, dkvp_ref, dgq_ref, dgkv_ref):
        cos, sin = cos_ref[...], sin_ref[...]
        first = pl.program_id(0) == 0

        def unrope(d):
            return d * cos + _swap_rope_halves(d * sin)

        dcqn = None
        dkr = None
        dckvn = None
        for h in range(B_HEADS):
            o = h * QK_PAD
            dq_h = jnp.transpose(dqt_ref[o:o + QK_PAD, :])
            dqn16 = dq_h[:, 0:128].astype(BF16)
            dqr16 = unrope(dq_h[:, 128:256]).astype(BF16)
            dqp_ref[:, o:o + 128] = dqn16
            dqp_ref[:, o + 128:o + 256] = dqr16
            part = _dot_nt(dqn16, wq_ref[:, o:o + 128]) + _dot_nt(dqr16, wq_ref[:, o + 128:o + 256])
            dcqn = part if dcqn is None else dcqn + part
            dkn16 = dk_ref[:, o:o + 128].astype(BF16)
            dkvp_ref[:, h * 128:(h + 1) * 128] = dkn16
            part = _dot_nt(dkn16, wkv_ref[:, h * 128:(h + 1) * 128])
            dckvn = part if dckvn is None else dckvn + part
            kr_part = dk_ref[:, o + 128:o + 256]
            dkr = kr_part if dkr is None else dkr + kr_part
        dv16 = dv_ref[...].astype(BF16)
        dkvp_ref[:, 512:1024] = dv16
        dckvn = dckvn + _dot_nt(dv16, wkv_ref[:, 512:1024])
        dkr_ref[...] = unrope(dkr)
        dcq, dgq = _rms_bwd(dcqn, cq_ref[...], gq_ref[...])
        dckv, dgkv = _rms_bwd(dckvn, ckv_ref[...], gkv_ref[...])
        dcq_ref[...] = dcq
        dckv_ref[...] = dckv
        _acc_rows(dgq_ref, dgq, first)
        _acc_rows(dgkv_ref, dgkv, first)

    tab = pl.BlockSpec((tm, 128), lambda i: (i % nt, 0))
    return pl.pallas_call(
        body, name="mla_prep_bwd", grid=(t // tm,),
        in_specs=[pl.BlockSpec((1024, tm), lambda i: (0, i)), _rows(tm, 1024), _rows(tm, 512), _rows(tm, Q_LORA),
                  _rows(tm, KV_LORA),
                  _full((1, Q_LORA)), _full((1, KV_LORA)), _full((Q_LORA, 1024)), _full((KV_LORA, 1024)), tab, tab],
        out_specs=[_rows(tm, Q_LORA), _rows(tm, KV_LORA), _rows(tm, 128), _rows(tm, 1024), _rows(tm, 1024),
                   _full((1, Q_LORA)), _full((1, KV_LORA))],
        out_shape=[jax.ShapeDtypeStruct((t, Q_LORA), F32), jax.ShapeDtypeStruct((t, KV_LORA), F32),
                   jax.ShapeDtypeStruct((t, 128), F32), jax.ShapeDtypeStruct((t, 1024), BF16),
                   jax.ShapeDtypeStruct((t, 1024), BF16), jax.ShapeDtypeStruct((1, Q_LORA), F32),
                   jax.ShapeDtypeStruct((1, KV_LORA), F32)],
        compiler_params=_params(1),
    )(dqt, dk, dv, cq, ckv, g_qa, g_kva, w_q_p, w_kv_p, cosx, sinx)


def _attn_dims(t, n_batch):
    s = t // n_batch
    tq = _pick(s, 256, 128)
    return s, tq, s // tq


def _attn_fwd(q, k, v, *, n_batch):
    t = q.shape[0]
    s, tq, nq = _attn_dims(t, n_batch)

    def body(q_ref, k_ref, v_ref, o_ref, lse_ref):
        sc = _dot_nt(q_ref[...], k_ref[...]) * ATTN_SCALE
        m = jnp.max(sc, axis=-1, keepdims=True)
        p = jnp.exp(sc - m)
        l = jnp.sum(p, axis=-1, keepdims=True)
        o_ref[...] = _dot(p.astype(BF16), v_ref[...]) / l
        lse = m + jnp.log(l)
        lse_ref[...] = jnp.transpose(jnp.broadcast_to(lse, (tq, 128)))[0:1, :]

    return pl.pallas_call(
        body, name="attn_fwd", grid=(n_batch, B_HEADS, nq),
        in_specs=[pl.BlockSpec((tq, QK_PAD), lambda b, h, i: (b * nq + i, h)),
                  pl.BlockSpec((s, QK_PAD), lambda b, h, i: (b, h)),
                  pl.BlockSpec((s, B_V), lambda b, h, i: (b, h))],
        out_specs=[pl.BlockSpec((tq, B_V), lambda b, h, i: (b * nq + i, h)),
                   pl.BlockSpec((None, 1, tq), lambda b, h, i: (h, 0, b * nq + i))],
        out_shape=[jax.ShapeDtypeStruct((t, B_HEADS * B_V), F32), jax.ShapeDtypeStruct((B_HEADS, 1, t), F32)],
        compiler_params=_params(3),
    )(q, k, v)


def _attn_bwd(q, k, v, do16, lse, delta, *, n_batch):
    t = q.shape[0]
    s, tk, nk = _attn_dims(t, n_batch)

    def body(q_ref, k_ref, v_ref, do_ref, lse_ref, dl_ref, dqt_ref, dk_ref, dv_ref):
        j = pl.program_id(2)
        qv, kv, dov = q_ref[...], k_ref[...], do_ref[...]
        pt = jnp.exp(_dot_nt(kv, qv) * ATTN_SCALE - lse_ref[...])
        dv_ref[...] = _dot(pt.astype(BF16), dov)
        dpt = _dot_nt(v_ref[...], dov)
        dst = (pt * (dpt - dl_ref[...]) * ATTN_SCALE).astype(BF16)
        dk_ref[...] = _dot(dst, qv)
        part = _dot_tn(kv, dst)

        @pl.when(j == 0)
        def _():
            dqt_ref[...] = part

        @pl.when(j != 0)
        def _():
            dqt_ref[...] += part

    row = pl.BlockSpec((None, 1, s), lambda b, h, j: (h, 0, b))
    return pl.pallas_call(
        body, name="attn_bwd", grid=(n_batch, B_HEADS, nk),
        in_specs=[pl.BlockSpec((s, QK_PAD), lambda b, h, j: (b, h)),
                  pl.BlockSpec((tk, QK_PAD), lambda b, h, j: (b * nk + j, h)),
                  pl.BlockSpec((tk, B_V), lambda b, h, j: (b * nk + j, h)),
                  pl.BlockSpec((s, B_V), lambda b, h, j: (b, h)), row, row],
        out_specs=[pl.BlockSpec((QK_PAD, s), lambda b, h, j: (h, b)),
                   pl.BlockSpec((tk, QK_PAD), lambda b, h, j: (b * nk + j, h)),
                   pl.BlockSpec((tk, B_V), lambda b, h, j: (b * nk + j, h))],
        out_shape=[jax.ShapeDtypeStruct((B_HEADS * QK_PAD, t), F32), jax.ShapeDtypeStruct((t, B_HEADS * QK_PAD), F32),
                   jax.ShapeDtypeStruct((t, B_HEADS * B_V), F32)],
        compiler_params=_params(3),
    )(q, k, v, do16, lse, delta)


def _group_ones16():
    r = _iota2((A_WIDTH, A_WIDTH), 0) // 64
    c = _iota2((A_WIDTH, A_WIDTH), 1) // 64
    return (r == c).astype(BF16)


def _head_rms(o, ones16):
    return lax.rsqrt(_exact_right(o * o, ones16) * (1.0 / 64.0) + EPS)


def _out_fwd(o_f, o_r, hg, o_attn, x, g_hn, g_mla, w_out, g2):
    t = x.shape[0]
    tm = _row_tile(t)

    def body(of_ref, or_ref, hg_ref, oa_ref, x_ref, ghn_ref, gm_ref, w_ref, g2_ref, y_ref, x2_ref, h2_ref):
        ones16 = _group_ones16()
        o = of_ref[...] + or_ref[...]
        hgv = hg_ref[...]
        ya = ((o * _head_rms(o, ones16)) * ghn_ref[...]) * (hgv * _sigmoid(hgv))
        yb = _rms_fwd(oa_ref[...], gm_ref[...])
        ya16, yb16 = ya.astype(BF16), yb.astype(BF16)
        y_ref[:, 0:A_WIDTH] = ya16
        y_ref[:, A_WIDTH:D_MODEL] = yb16
        x2 = x_ref[...] + _dot(ya16, w_ref[0:A_WIDTH, :]) + _dot(yb16, w_ref[A_WIDTH:D_MODEL, :])
        x2_ref[...] = x2
        h2_ref[...] = _rms_fwd(x2, g2_ref[...]).astype(BF16)

    a512 = _rows(tm, A_WIDTH)
    return pl.pallas_call(
        body, name="out_fwd", grid=(t // tm,),
        in_specs=[a512, a512, a512, a512, _rows(tm, D_MODEL), _full((1, A_WIDTH)), _full((1, A_WIDTH)),
                  _full((D_MODEL, D_MODEL)), _full((1, D_MODEL))],
        out_specs=[_rows(tm, D_MODEL)] * 3,
        out_shape=[jax.ShapeDtypeStruct((t, D_MODEL), BF16), jax.ShapeDtypeStruct((t, D_MODEL), F32),
                   jax.ShapeDtypeStruct((t, D_MODEL), BF16)],
        compiler_params=_params(1),
    )(o_f, o_r, hg, o_attn, x, g_hn, g_mla, w_out, g2)


def _out_bwd(dx2_16, o_f, o_r, hg, o_attn, g_hn, g_mla, w_out):
    t = dx2_16.shape[0]
    tm = _row_tile(t)

    def body(dx_ref, of_ref, or_ref, hg_ref, oa_ref, ghn_ref, gm_ref, w_ref,
             do_ref, dhg_ref, doa_ref, dl_ref, dghn_ref, dgm_ref):
        first = pl.program_id(0) == 0
        ones16 = _group_ones16()
        dxv = dx_ref[...]
        dya = _dot_nt(dxv, w_ref[0:A_WIDTH, :])
        dyb = _dot_nt(dxv, w_ref[A_WIDTH:D_MODEL, :])
        o = of_ref[...] + or_ref[...]
        rh = _head_rms(o, ones16)
        oh = o * rh
        hgv = hg_ref[...]
        sg = _sigmoid(hgv)
        sl = hgv * sg
        ghn = ghn_ref[...]
        dhg_ref[...] = (dya * (oh * ghn)) * (sg * (1.0 + hgv * (1.0 - sg)))
        _acc_rows(dghn_ref, dya * sl * oh, first)
        doh = dya * sl * ghn
        do_ref[...] = rh * (doh - oh * (_exact_right(doh * oh, ones16) * (1.0 / 64.0)))
        oav = oa_ref[...]
        doa, dgm = _rms_bwd(dyb, oav, gm_ref[...])
        doa_ref[...] = doa.astype(BF16)
        _acc_rows(dgm_ref, dgm, first)
        sel16 = (_iota2((8, A_WIDTH), 0) == _iota2((8, A_WIDTH), 1) // B_V).astype(BF16)
        x1, x2, x3 = _split3(doa * oav)
        delta = _dot_nt(sel16, x1) + _dot_nt(sel16, x2) + _dot_nt(sel16, x3)
        for h in range(B_HEADS):
            dl_ref[h] = delta[h:h + 1, :]

    a512 = _rows(tm, A_WIDTH)
    return pl.pallas_call(
        body, name="out_bwd", grid=(t // tm,),
        in_specs=[_rows(tm, D_MODEL), a512, a512, a512, a512, _full((1, A_WIDTH)), _full((1, A_WIDTH)),
                  _full((D_MODEL, D_MODEL))],
        out_specs=[a512, a512, a512, pl.BlockSpec((B_HEADS, 1, tm), lambda i: (0, 0, i)),
                   _full((1, A_WIDTH)), _full((1, A_WIDTH))],
        out_shape=[jax.ShapeDtypeStruct((t, A_WIDTH), F32)] * 2
        + [jax.ShapeDtypeStruct((t, A_WIDTH), BF16), jax.ShapeDtypeStruct((B_HEADS, 1, t), F32)]
        + [jax.ShapeDtypeStruct((1, A_WIDTH), F32)] * 2,
        compiler_params=_params(1),
    )(dx2_16, o_f, o_r, hg, o_attn, g_hn, g_mla, w_out)


def _ffn_fwd_bwd(h2, x2, target, w_gate, w_up, w_down, g_f, g2):
    t = x2.shape[0]
    tm = _row_tile(t)
    inv_d = 1.0 / D_MODEL

    def body(h2_ref, x2_ref, tg_ref, wg_ref, wu_ref, wd_ref, gf_ref, g2_ref,
             act_ref, dgate_ref, dup_ref, dx3_ref, dx2_ref, dx2h_ref, loss_ref, dgf_ref, dg2_ref):
        first = pl.program_id(0) == 0
        h2v = h2_ref[...]
        gate = _dot(h2v, wg_ref[...])
        up = _dot(h2v, wu_ref[...])
        sg = _sigmoid(gate)
        sl = gate * sg
        act16 = (sl * up).astype(BF16)
        act_ref[...] = act16
        x2v = x2_ref[...]
        x3 = x2v + _dot(act16, wd_ref[...])
        r3 = lax.rsqrt(jnp.mean(x3 * x3, axis=-1, keepdims=True) + EPS)
        x3h = x3 * r3
        gf = gf_ref[...]
        err = x3h * gf - tg_ref[...]
        part = 0.5 * jnp.sum(jnp.mean(err * err, axis=-1, keepdims=True), axis=0, keepdims=True)

        @pl.when(first)
        def _():
            loss_ref[...] = jnp.zeros_like(loss_ref)

        loss_ref[...] += part
        dy = err * inv_d
        _acc_rows(dgf_ref, dy * x3h, first)
        dx3h = dy * gf
        dx3 = r3 * (dx3h - x3h * jnp.mean(dx3h * x3h, axis=-1, keepdims=True))
        dx3_16 = dx3.astype(BF16)
        dx3_ref[...] = dx3_16
        da = _dot_nt(dx3_16, wd_ref[...])
        dup16 = (da * sl).astype(BF16)
        dgate16 = (da * up * (sg * (1.0 + gate * (1.0 - sg)))).astype(BF16)
        dup_ref[...] = dup16
        dgate_ref[...] = dgate16
        dh2 = _dot_nt(dgate16, wg_ref[...]) + _dot_nt(dup16, wu_ref[...])
        dx2n, dg2 = _rms_bwd(dh2, x2v, g2_ref[...])
        _acc_rows(dg2_ref, dg2, first)
        dx2 = dx3 + dx2n
        dx2_ref[...] = dx2
        dx2h_ref[...] = dx2.astype(BF16)

    return pl.pallas_call(
        body, name="ffn_fwd_bwd", grid=(t // tm,),
        in_specs=[_rows(tm, D_MODEL), _rows(tm, D_MODEL), _rows(tm, D_MODEL), _full((D_MODEL, D_FF), True),
                  _full((D_MODEL, D_FF), True), _full((D_FF, D_MODEL), True), _full((1, D_MODEL)), _full((1, D_MODEL))],
        out_specs=[_rows(tm, D_FF), _rows(tm, D_FF), _rows(tm, D_FF), _rows(tm, D_MODEL), _rows(tm, D_MODEL),
                   _rows(tm, D_MODEL), _full((8, 128)), _full((1, D_MODEL)), _full((1, D_MODEL))],
        out_shape=[jax.ShapeDtypeStruct((t, D_FF), BF16)] * 3
        + [jax.ShapeDtypeStruct((t, D_MODEL), BF16), jax.ShapeDtypeStruct((t, D_MODEL), F32),
           jax.ShapeDtypeStruct((t, D_MODEL), BF16), jax.ShapeDtypeStruct((8, 128), F32),
           jax.ShapeDtypeStruct((1, D_MODEL), F32), jax.ShapeDtypeStruct((1, D_MODEL), F32)],
        compiler_params=_params(1),
    )(h2, x2, target, w_gate, w_up, w_down, g_f, g2)


def _wgrad(a, b, name):
    t, m = a.shape
    n = b.shape[1]
    bm = _pick(m, 1408)
    bn = _pick(n, 1664)
    tk = _pick(t, 1024, 16)
    nk = t // tk

    def body(a_ref, b_ref, o_ref):
        part = _dot_tn(a_ref[...], b_ref[...])

        @pl.when(pl.program_id(2) == 0)
        def _():
            o_ref[...] = part

        @pl.when(pl.program_id(2) != 0)
        def _():
            o_ref[...] += part

    return pl.pallas_call(
        body, name=name, grid=(m // bm, n // bn, nk),
        in_specs=[pl.BlockSpec((tk, bm), lambda i, j, k: (k, i)), pl.BlockSpec((tk, bn), lambda i, j, k: (k, j))],
        out_specs=pl.BlockSpec((bm, bn), lambda i, j, k: (i, j)),
        out_shape=jax.ShapeDtypeStruct((m, n), F32),
        compiler_params=_params(3),
    )(a, b)


def _rope_tables(seq):
    inv = 1.0 / (ROPE_THETA ** (jnp.arange(0, B_ROPE, 2, dtype=F32) / B_ROPE))
    ang = jnp.arange(seq, dtype=F32)[:, None] * inv[None, :]
    cos, sin = jnp.cos(ang), jnp.sin(ang)
    zeros = jnp.zeros((seq, 64), F32)
    return jnp.concatenate([cos, cos, zeros], axis=1), jnp.concatenate([-sin, sin, zeros], axis=1)


def _pad_weights(w_in, w_q_b, w_kv_b):
    w_in_p = jnp.pad(w_in, ((0, 0), (0, D_IN_PAD - D_IN)))
    w_q_p = jnp.pad(w_q_b.reshape(Q_LORA, B_HEADS, B_NOPE + B_ROPE), ((0, 0), (0, 0), (0, 64))).reshape(Q_LORA, 1024)
    kv = w_kv_b.reshape(KV_LORA, B_HEADS, B_NOPE + B_V)
    w_kv_p = jnp.concatenate([kv[:, :, :B_NOPE].reshape(KV_LORA, 512), kv[:, :, B_NOPE:].reshape(KV_LORA, 512)], axis=1)
    return w_in_p, w_q_p, w_kv_p


def _unpad_grads(g_in_p, g_q_p, g_kv_p):
    g_in = g_in_p[:, :D_IN]
    g_q = g_q_p.reshape(Q_LORA, B_HEADS, QK_PAD)[:, :, :B_NOPE + B_ROPE].reshape(Q_LORA, B_HEADS * (B_NOPE + B_ROPE))
    g_kv = jnp.concatenate([g_kv_p[:, :512].reshape(KV_LORA, B_HEADS, B_NOPE),
                            g_kv_p[:, 512:].reshape(KV_LORA, B_HEADS, B_V)], axis=2).reshape(KV_LORA, 1024)
    return g_in, g_q, g_kv


def _local_step(x, target, lbl, g1, g_hn, g_qa, g_kva, g_mla, g2, g_f, w_in, w_q_b, w_kv_b, w_out, w_gate, w_up, w_down):
    n_batch, seq, _ = x.shape
    t = n_batch * seq
    x = x.reshape(t, D_MODEL)
    target = target.reshape(t, D_MODEL)
    w_in_p, w_q_p, w_kv_p = _pad_weights(w_in, w_q_b, w_kv_b)
    cosx, sinx = _rope_tables(seq)

    h1, hq, hi, hff, hfb, hg, cq, ckv, kr = _in_fwd(x, g1, w_in_p)
    o_f, st_f = _hgrn_fwd(hq, hi, hff, lbl, n_batch=n_batch, direction=0)
    o_r, st_r = _hgrn_fwd(hq, hi, hfb, lbl, n_batch=n_batch, direction=1)
    q, k, v, cqn, ckvn = _mla_prep(cq, ckv, kr, g_qa, g_kva, w_q_p, w_kv_p, cosx, sinx, n_batch=n_batch)
    o_attn, lse = _attn_fwd(q, k, v, n_batch=n_batch)
    ycat, x2, h2 = _out_fwd(o_f, o_r, hg, o_attn, x, g_hn, g_mla, w_out, g2)
    act, dgate, dup, dx3_16, dx2, dx2_16, loss, dg_f, dg2 = _ffn_fwd_bwd(h2, x2, target, w_gate, w_up, w_down, g_f, g2)
    do_h, dhg, do_attn16, delta, dg_hn, dg_mla = _out_bwd(dx2_16, o_f, o_r, hg, o_attn, g_hn, g_mla, w_out)
    dqt, dk, dv = _attn_bwd(q, k, v, do_attn16, lse, delta, n_batch=n_batch)
    dcq, dckv, dkr, dqp16, dkvp16, dg_qa, dg_kva = _mla_prep_bwd(dqt, dk, dv, cq, ckv, g_qa, g_kva, w_q_p, w_kv_p,
                                                                  cosx, sinx, n_batch=n_batch)
    dq_f, dv_f, dz_f, dl_f = _hgrn_bwd(hq, hi, hff, do_h, st_f, lbl, n_batch=n_batch, direction=0)
    dq_r, dv_r, dz_r, dl_r = _hgrn_bwd(hq, hi, hfb, do_h, st_r, lbl, n_batch=n_batch, direction=1)
    dx, dproj16, dg1 = _in_bwd(dq_f, dq_r, dv_f, dv_r, dz_f, dz_r, dhg, hq, dcq, dckv, dkr, dx2, x, g1, w_in_p)

    gw_in, gw_q, gw_kv = _unpad_grads(_wgrad(h1, dproj16, "wgrad_in"), _wgrad(cqn, dqp16, "wgrad_q_b"),
                                      _wgrad(ckvn, dkvp16, "wgrad_kv_b"))
    grads = dict(
        norm1_g=dg1, w_in=gw_in, lb_logits=jnp.stack([dl_f, dl_r]), hgrn_norm_g=dg_hn, q_a_norm_g=dg_qa,
        w_q_b=gw_q, kv_a_norm_g=dg_kva, w_kv_b=gw_kv, mla_norm_g=dg_mla, w_out=_wgrad(ycat, dx2_16, "wgrad_out"),
        norm2_g=dg2, w_gate=_wgrad(h2, dgate, "wgrad_gate"), w_up=_wgrad(h2, dup, "wgrad_up"),
        w_down=_wgrad(act, dx3_16, "wgrad_down"), final_norm_g=dg_f)
    return loss[0, 0], dx.reshape(n_batch, seq, D_MODEL), grads


_HBM = pl.BlockSpec(memory_space=pltpu.HBM)
_MESH = pl.DeviceIdType.MESH


def _place():
    x, y, c = lax.axis_index("x"), lax.axis_index("y"), lax.axis_index("c")
    other_chips = [(1 - x, y), (x, 1 - y), (1 - x, 1 - y)]
    return x, y, c, other_chips


def _gather_weights(wflat16, lb8):
    def body(w_ref, lb_ref, wall_ref, lball_ref, send_sems, recv_sems, local_sems):
        x, y, c, chips = _place()
        mine = 2 * x + y

        def copies(j, chip_index, to):
            def mk(src, dst, n):
                return pltpu.make_async_remote_copy(src_ref=src, dst_ref=dst.at[chip_index], send_sem=send_sems.at[n],
                                                    recv_sem=recv_sems.at[n], device_id=to, device_id_type=_MESH)
            return mk(w_ref, wall_ref, j), mk(lb_ref, lball_ref, 3 + j)

        local = [pltpu.make_async_copy(w_ref, wall_ref.at[mine], local_sems.at[0]),
                 pltpu.make_async_copy(lb_ref, lball_ref.at[mine], local_sems.at[1])]
        for cp in local:
            cp.start()
        sends = []
        for j, chip in enumerate(chips):
            sends += copies(j, mine, (*chip, c))
        for cp in sends:
            cp.start()
        for j, (px, py) in enumerate(chips):
            for cp in copies(j, 2 * px + py, (x, y, c)):
                cp.wait_recv()
        for cp in sends:
            cp.wait_send()
        for cp in local:
            cp.wait()

    return pl.pallas_call(
        body, name="gather_weights", in_specs=[_HBM, _HBM], out_specs=[_HBM, _HBM],
        out_shape=[jax.ShapeDtypeStruct((N_CHIPS,) + wflat16.shape, BF16), jax.ShapeDtypeStruct((N_CHIPS,) + lb8.shape, F32)],
        scratch_shapes=[pltpu.SemaphoreType.DMA((6,)), pltpu.SemaphoreType.DMA((6,)), pltpu.SemaphoreType.DMA((2,))],
    )(wflat16, lb8)


def _scatter_grads(g16, small):
    def body(g_ref, s_ref, recv_ref, sall_ref, send_sems, recv_sems, local_sem):
        x, y, c, chips = _place()
        me = 4 * x + 2 * y + c
        flips = [(fx, fy, fc) for fx in (0, 1) for fy in (0, 1) for fc in (0, 1)][1:]

        def peer(f):
            return tuple((1 - a) if b else a for a, b in zip((x, y, c), f))

        def seg(j, src_index, to):
            return pltpu.make_async_remote_copy(src_ref=g_ref.at[src_index], dst_ref=recv_ref.at[j], send_sem=send_sems.at[j],
                                                recv_sem=recv_sems.at[j], device_id=to, device_id_type=_MESH)

        def sm(r, index, to):
            return pltpu.make_async_remote_copy(src_ref=s_ref, dst_ref=sall_ref.at[index], send_sem=send_sems.at[3 + r],
                                                recv_sem=recv_sems.at[3 + r], device_id=to, device_id_type=_MESH)

        local = pltpu.make_async_copy(s_ref, sall_ref.at[me], local_sem)
        local.start()
        sends = [seg(j, 2 * px + py, (px, py, c)) for j, (px, py) in enumerate(chips)]
        sends += [sm(r, me, peer(f)) for r, f in enumerate(flips)]
        for cp in sends:
            cp.start()
        for j in range(3):
            seg(j, 0, (x, y, c)).wait_recv()
        for r, f in enumerate(flips):
            px, py, pc = peer(f)
            sm(r, 4 * px + 2 * py + pc, (x, y, c)).wait_recv()
        for cp in sends:
            cp.wait_send()
        local.wait()

    return pl.pallas_call(
        body, name="scatter_grads", in_specs=[_HBM, _HBM], out_specs=[_HBM, _HBM],
        out_shape=[jax.ShapeDtypeStruct((3,) + g16.shape[1:], BF16), jax.ShapeDtypeStruct((8,) + small.shape, F32)],
        scratch_shapes=[pltpu.SemaphoreType.DMA((10,)), pltpu.SemaphoreType.DMA((10,)), pltpu.SemaphoreType.DMA(())],
    )(g16, small)


def _swap_sibling(p):
    def body(p_ref, o_ref, send_sem, recv_sem):
        x, y, c, _ = _place()
        cp = pltpu.make_async_remote_copy(src_ref=p_ref, dst_ref=o_ref, send_sem=send_sem, recv_sem=recv_sem,
                                          device_id=(x, y, 1 - c), device_id_type=_MESH)
        cp.start()
        cp.wait()

    return pl.pallas_call(
        body, name="swap_sibling", in_specs=[_HBM], out_specs=_HBM, out_shape=jax.ShapeDtypeStruct(p.shape, p.dtype),
        scratch_shapes=[pltpu.SemaphoreType.DMA(()), pltpu.SemaphoreType.DMA(())],
    )(p)


def _sum_segments(own, recv):
    rows = own.shape[0]
    tm = _row_tile(rows)

    def body(o_ref, r_ref, out_ref):
        acc = o_ref[...]
        for j in range(3):
            acc = acc + r_ref[j].astype(F32)
        out_ref[...] = acc

    return pl.pallas_call(
        body, name="sum_segments", grid=(rows // tm,),
        in_specs=[_rows(tm, FLAT_W), pl.BlockSpec((3, tm, FLAT_W), lambda i: (0, i, 0))],
        out_specs=_rows(tm, FLAT_W), out_shape=jax.ShapeDtypeStruct(own.shape, F32), compiler_params=_params(1),
    )(own, recv)


def _sum_devices(sall):
    def body(s_ref, o_ref):
        acc = s_ref[0]
        for d in range(1, 8):
            acc = acc + s_ref[d]
        o_ref[...] = acc

    return pl.pallas_call(body, name="sum_devices", out_shape=jax.ShapeDtypeStruct(sall.shape[1:], F32))(sall)


def _adamw(w, m, v, ga, gb, name):
    rows, cols = w.shape
    tm = _row_tile(rows)
    two = gb is not None

    def body(*refs):
        w_ref, m_ref, v_ref, ga_ref = refs[:4]
        g_ref, d_ref, m2_ref, v2_ref = refs[-4:]
        g = ga_ref[...] + refs[4][...] if two else ga_ref[...]
        m2 = ADAM_B1 * m_ref[...] + (1.0 - ADAM_B1) * g
        v2 = ADAM_B2 * v_ref[...] + (1.0 - ADAM_B2) * (g * g)
        m_hat = m2 / (1.0 - ADAM_B1 ** ADAM_STEP)
        v_hat = v2 / (1.0 - ADAM_B2 ** ADAM_STEP)
        g_ref[...] = g
        d_ref[...] = -ADAM_LR * (m_hat / (jnp.sqrt(v_hat) + ADAM_EPS) + ADAM_WD * w_ref[...])
        m2_ref[...] = m2
        v2_ref[...] = v2

    blk = _rows(tm, cols)
    args = (w, m, v, ga) + ((gb,) if two else ())
    return pl.pallas_call(
        body, name=name, grid=(rows // tm,), in_specs=[blk] * len(args), out_specs=[blk] * 4,
        out_shape=[jax.ShapeDtypeStruct(w.shape, F32)] * 4, compiler_params=_params(1),
    )(*args)


_COLUMN_SHARDED = ("w_in", "w_q_b", "w_kv_b", "w_gate", "w_up")
_FULL_SHAPES = dict(w_in=(D_MODEL, D_IN), w_q_b=(Q_LORA, 768), w_kv_b=(KV_LORA, 1024), w_out=(D_MODEL, D_MODEL),
                    w_gate=(D_MODEL, D_FF), w_up=(D_MODEL, D_FF), w_down=(D_FF, D_MODEL))
_SMALL = (("norm1_g", 1024), ("lb_logits", 2048), ("hgrn_norm_g", 512), ("q_a_norm_g", 384), ("kv_a_norm_g", 256),
          ("mla_norm_g", 512), ("norm2_g", 1024), ("final_norm_g", 1024))
_UPDATE_ROWS = 48


def _pad_rows(a, rows):
    return jnp.pad(a, ((0, rows - a.shape[0]), (0, 0)))


def _flatten_shards(shards):
    return _pad_rows(jnp.concatenate([shards[n].reshape(r, FLAT_W) for n, r in SHARD_ROWS], axis=0), FLAT_ROWS)


def _split_flat(flat):
    out, off = {}, 0
    for n, r in SHARD_ROWS:
        out[n] = flat[off:off + r]
        off += r
    return out


def _to_segments(name, g):
    rows = dict(SHARD_ROWS)[name]
    if name in _COLUMN_SHARDED:
        r, c = g.shape
        g = g.reshape(r, N_CHIPS, c // N_CHIPS).transpose(1, 0, 2)
    return g.reshape(N_CHIPS, rows, FLAT_W)


def _from_segments(name, seg):
    r, c = _FULL_SHAPES[name]
    if name in _COLUMN_SHARDED:
        return seg.reshape(N_CHIPS, r, c // N_CHIPS).transpose(1, 0, 2).reshape(r, c)
    return seg.reshape(r, c)


def kernel(x, norm1_g, w_in, lb_logits, hgrn_norm_g, q_a_norm_g, w_q_b, kv_a_norm_g, w_kv_b, mla_norm_g, w_out, norm2_g, w_gate, w_up, w_down, final_norm_g, loss_target, m_norm1_g, m_w_in, m_lb_logits, m_hgrn_norm_g, m_q_a_norm_g, m_w_q_b, m_kv_a_norm_g, m_w_kv_b, m_mla_norm_g, m_w_out, m_norm2_g, m_w_gate, m_w_up, m_w_down, m_final_norm_g, v_norm1_g, v_w_in, v_lb_logits, v_hgrn_norm_g, v_q_a_norm_g, v_w_q_b, v_kv_a_norm_g, v_w_kv_b, v_mla_norm_g, v_w_out, v_norm2_g, v_w_gate, v_w_up, v_w_down, v_final_norm_g):
    names = ("norm1_g", "w_in", "lb_logits", "hgrn_norm_g", "q_a_norm_g", "w_q_b", "kv_a_norm_g", "w_kv_b", "mla_norm_g",
             "w_out", "norm2_g", "w_gate", "w_up", "w_down", "final_norm_g")
    w = dict(zip(names, (norm1_g, w_in, lb_logits, hgrn_norm_g, q_a_norm_g, w_q_b, kv_a_norm_g, w_kv_b, mla_norm_g,
                         w_out, norm2_g, w_gate, w_up, w_down, final_norm_g)))
    m = dict(zip(names, (m_norm1_g, m_w_in, m_lb_logits, m_hgrn_norm_g, m_q_a_norm_g, m_w_q_b, m_kv_a_norm_g, m_w_kv_b,
                         m_mla_norm_g, m_w_out, m_norm2_g, m_w_gate, m_w_up, m_w_down, m_final_norm_g)))
    v = dict(zip(names, (v_norm1_g, v_w_in, v_lb_logits, v_hgrn_norm_g, v_q_a_norm_g, v_w_q_b, v_kv_a_norm_g, v_w_kv_b,
                         v_mla_norm_g, v_w_out, v_norm2_g, v_w_gate, v_w_up, v_w_down, v_final_norm_g)))
    matrices = tuple(n for n, _ in SHARD_ROWS)
    chip = 2 * lax.axis_index("x") + lax.axis_index("y")

    wflat16 = _flatten_shards({n: w[n][0].astype(BF16) for n in matrices})
    lb8 = _pad_rows(lb_logits.reshape(4, 128), 8)
    wall, lball = _gather_weights(wflat16, lb8)
    full = {n: _from_segments(n, seg) for n, seg in zip(matrices, (wall[:, o:o + r] for o, r in _offsets()))}
    lbl = lball[:, :4].reshape(N_CHIPS, 2, 2, 128).transpose(1, 2, 0, 3).reshape(2, 2, A_WIDTH)

    loss_part, grad_x, g = _local_step(
        x, loss_target, lbl, norm1_g, hgrn_norm_g, q_a_norm_g, kv_a_norm_g, mla_norm_g, norm2_g, final_norm_g[None, :],
        full["w_in"], full["w_q_b"], full["w_kv_b"], full["w_out"], full["w_gate"], full["w_up"], full["w_down"])
    loss = lax.psum(loss_part, ("x", "y", "c"))

    g32 = jnp.concatenate([_to_segments(n, g[n]) for n in matrices]
                          + [jnp.zeros((N_CHIPS, FLAT_ROWS - 3320, FLAT_W), F32)], axis=1)
    small = _pad_rows(jnp.concatenate([g[n].reshape(-1) for n, _ in _SMALL]).reshape(-1, 128), SMALL_ROWS)
    recv, small_all = _scatter_grads(g32.astype(BF16), small)
    own = lax.dynamic_index_in_dim(g32, chip, axis=0, keepdims=False)
    part = _sum_segments(own, recv)
    part_sib = _swap_sibling(part)
    small_sum = _sum_devices(small_all).reshape(-1)

    out = {}
    pa, pb = _split_flat(part), _split_flat(part_sib)
    for n in matrices:
        shp = w[n].shape[1:]
        res = _adamw(w[n][0], m[n][0], v[n][0], pa[n].reshape(shp), pb[n].reshape(shp), f"adamw_{n}")
        out[n] = tuple(r[None] for r in res)

    small_g, off = {}, 0
    for n, size in _SMALL:
        small_g[n] = small_sum[off:off + size]
        off += size
    small_g["lb_logits"] = lax.dynamic_slice_in_dim(small_g["lb_logits"].reshape(2, 2, A_WIDTH), chip * 128, 128, axis=2)
    small_names = tuple(n for n, _ in _SMALL)

    def pack(d):
        return _pad_rows(jnp.concatenate([d[n].reshape(-1) for n in small_names]).reshape(-1, 128), _UPDATE_ROWS)

    res = _adamw(pack(w), pack(m), pack(v), pack(small_g), None, "adamw_small")
    off = 0
    flat = [r.reshape(-1) for r in res]
    for n in small_names:
        size = w[n].size
        out[n] = tuple(f[off:off + size].reshape(w[n].shape) for f in flat)
        off += size

    return (loss, grad_x) + tuple(out[n][i] for i in range(4) for n in names)


def _offsets():
    off, res = 0, []
    for _, r in SHARD_ROWS:
        res.append((off, r))
        off += r
    return res
```

```python
import functools

import jax
import jax.numpy as jnp
from jax import lax
from jax.experimental import pallas as pl
from jax.experimental.pallas import tpu as pltpu

F32 = jnp.float32
BF16 = jnp.bfloat16

D_MODEL = 1024
A_WIDTH = 512
HEAD_PAIRS = 4
CHUNK = 64
B_HEADS = 4
B_NOPE = 128
B_ROPE = 64
B_V = 128
QK_PAD = 256
Q_LORA = 384
KV_LORA = 256
D_FF = 2816
D_IN = 3264
D_IN_PAD = 3328
IN_WIDTHS = (512, 512, 512, 512, 512, Q_LORA, KV_LORA, 128)
ROPE_THETA = 10000.0
EPS = 1e-6
ATTN_SCALE = (B_NOPE + B_ROPE) ** -0.5

ADAM_LR = 0.001
ADAM_B1 = 0.9
ADAM_B2 = 0.999
ADAM_EPS = 1e-08
ADAM_WD = 0.01
ADAM_STEP = 10

VMEM_LIMIT_BYTES = 60 * 1024 * 1024
N_CHIPS = 4
FLAT_W = 1024
SHARD_ROWS = (("w_in", 816), ("w_q_b", 72), ("w_kv_b", 64), ("w_out", 256),
              ("w_gate", 704), ("w_up", 704), ("w_down", 704))
FLAT_ROWS = 3328
SMALL_ROWS = 56


def _params(n_axes):
    return pltpu.CompilerParams(dimension_semantics=("arbitrary",) * n_axes,
                                vmem_limit_bytes=VMEM_LIMIT_BYTES)


def _dot(a, b):
    return jnp.dot(a, b, preferred_element_type=F32)


def _dot_nt(a, b):
    return lax.dot_general(a, b, (((1,), (1,)), ((), ())), preferred_element_type=F32)


def _dot_tn(a, b):
    return lax.dot_general(a, b, (((0,), (0,)), ((), ())), preferred_element_type=F32)


def _split3(x):
    x1 = x.astype(BF16)
    r = x - x1.astype(F32)
    x2 = r.astype(BF16)
    x3 = (r - x2.astype(F32)).astype(BF16)
    return x1, x2, x3


def _exact_left(m16, x):
    x1, x2, x3 = _split3(x)
    return _dot(m16, x1) + _dot(m16, x2) + _dot(m16, x3)


def _exact_right(x, m16):
    x1, x2, x3 = _split3(x)
    return _dot(x1, m16) + _dot(x2, m16) + _dot(x3, m16)


def _iota2(shape, dim):
    return lax.broadcasted_iota(jnp.int32, shape, dim)


def _sigmoid(x):
    return jax.nn.sigmoid(x)


def _pick(dim, cap, mult=128):
    if dim <= cap:
        return dim
    best = None
    for d in range(mult, cap + 1, mult):
        if dim % d == 0:
            best = d
    assert best is not None, (dim, cap, mult)
    return best


def _row_tile(t, cap=256):
    return _pick(t, cap, 8)


def _full(shape, single=False):
    if single:
        return pl.BlockSpec(shape, lambda *_: (0,) * len(shape), pipeline_mode=pl.Buffered(1))
    return pl.BlockSpec(shape, lambda *_: (0,) * len(shape))


def _rows(tm, width):
    return pl.BlockSpec((tm, width), lambda i: (i, 0))


def _acc_rows(ref, val, first):
    s = jnp.sum(val, axis=0, keepdims=True)

    @pl.when(first)
    def _():
        ref[...] = s

    @pl.when(jnp.logical_not(first))
    def _():
        ref[...] += s


def _in_fwd(x, g1, w_in_p):
    t = x.shape[0]
    tm = _row_tile(t)

    def body(x_ref, g_ref, w_ref, h_ref, *outs):
        xv = x_ref[...]
        r = lax.rsqrt(jnp.mean(xv * xv, axis=-1, keepdims=True) + EPS)
        h = ((xv * r) * g_ref[...]).astype(BF16)
        h_ref[...] = h
        off = 0
        for o_ref, w in zip(outs, IN_WIDTHS):
            o_ref[...] = _dot(h, w_ref[:, off:off + w])
            off += w

    return pl.pallas_call(
        body, name="in_fwd", grid=(t // tm,),
        in_specs=[_rows(tm, D_MODEL), _full((1, D_MODEL)), _full((D_MODEL, D_IN_PAD))],
        out_specs=[_rows(tm, D_MODEL)] + [_rows(tm, w) for w in IN_WIDTHS],
        out_shape=[jax.ShapeDtypeStruct((t, D_MODEL), BF16)]
        + [jax.ShapeDtypeStruct((t, w), F32) for w in IN_WIDTHS],
        compiler_params=_params(1),
    )(x, g1, w_in_p)


def _in_bwd(dq_f, dq_r, dv_f, dv_r, dz_f, dz_r, dhg, hq, dcq, dckv, dkr, dx2, x, g1, w_in_p):
    t = x.shape[0]
    tm = _row_tile(t)

    def body(dqf_ref, dqr_ref, dvf_ref, dvr_ref, dzf_ref, dzr_ref, dhg_ref, hq_ref, dcq_ref, dckv_ref,
             dkr_ref, dx2_ref, x_ref, g_ref, w_ref, dx_ref, dp_ref, dg_ref):
        hqv = hq_ref[...]
        sg = _sigmoid(hqv)
        dhq = (dqf_ref[...] + dqr_ref[...]) * (sg * (1.0 + hqv * (1.0 - sg)))
        pieces = (dhq, dvf_ref[...] + dvr_ref[...], dzf_ref[...], dzr_ref[...], dhg_ref[...],
                  dcq_ref[...], dckv_ref[...], dkr_ref[...])
        dh = None
        off = 0
        for p, w in zip(pieces, IN_WIDTHS):
            p16 = p.astype(BF16)
            dp_ref[:, off:off + w] = p16
            part = _dot_nt(p16, w_ref[:, off:off + w])
            dh = part if dh is None else dh + part
            off += w
        xv = x_ref[...]
        r = lax.rsqrt(jnp.mean(xv * xv, axis=-1, keepdims=True) + EPS)
        xh = xv * r
        _acc_rows(dg_ref, dh * xh, pl.program_id(0) == 0)
        dxh = dh * g_ref[...]
        dx_ref[...] = dx2_ref[...] + r * (dxh - xh * jnp.mean(dxh * xh, axis=-1, keepdims=True))

    a512 = _rows(tm, A_WIDTH)
    return pl.pallas_call(
        body, name="in_bwd", grid=(t // tm,),
        in_specs=[a512] * 8 + [_rows(tm, Q_LORA), _rows(tm, KV_LORA), _rows(tm, 128), _rows(tm, D_MODEL),
                               _rows(tm, D_MODEL), _full((1, D_MODEL)), _full((D_MODEL, D_IN_PAD))],
        out_specs=[_rows(tm, D_MODEL), _rows(tm, D_IN_PAD), _full((1, D_MODEL))],
        out_shape=[jax.ShapeDtypeStruct((t, D_MODEL), F32), jax.ShapeDtypeStruct((t, D_IN_PAD), BF16),
                   jax.ShapeDtypeStruct((1, D_MODEL), F32)],
        compiler_params=_params(1),
    )(dq_f, dq_r, dv_f, dv_r, dz_f, dz_r, dhg, hq, dcq, dckv, dkr, dx2, x, g1, w_in_p)


def _lower_bound(lbl_ref, direction):
    l0 = lbl_ref[direction, 0:1, :]
    l1 = lbl_ref[direction, 1:2, :]
    m = jnp.maximum(l0, l1)
    e0 = jnp.exp(l0 - m)
    e1 = jnp.exp(l1 - m)
    return e0 / (e0 + e1)


def _hgrn_consts(rb, reverse):
    row = _iota2((rb, rb), 0)
    col = _iota2((rb, rb), 1)
    same = (row // CHUNK) == (col // CHUNK)
    tri = jnp.logical_and(same, (col >= row) if reverse else (col <= row))
    tri_t = jnp.logical_and(same, (col <= row) if reverse else (col >= row))
    r128 = _iota2((128, 128), 0)
    c128 = _iota2((128, 128), 1)
    bd = (r128 < 64) == (c128 < 64)
    lane = _iota2((1, 128), 1)
    m0 = (lane < 64).astype(F32)
    return tri, tri_t, bd, (m0, 1.0 - m0)


def _per_chunk(x, fn):
    n = x.shape[0] // CHUNK
    return jnp.concatenate([jnp.broadcast_to(fn(x[c * CHUNK:(c + 1) * CHUNK]), (CHUNK, x.shape[1])) for c in range(n)],
                           axis=0)


def _hgrn_block(z, hqv, lb, tri16, reverse):
    sig = _sigmoid(z)
    sn = _sigmoid(-z)
    q = hqv * _sigmoid(hqv)
    f = lb + (1.0 - lb) * sig
    k = (1.0 - lb) * sn
    lf = jnp.log(f)
    cum = _exact_left(tri16, lf)
    last = _per_chunk(cum, (lambda a: a[0:1]) if reverse else (lambda a: a[CHUNK - 1:CHUNK]))
    e_neg = jnp.exp(-cum)
    e_end = jnp.exp(last - cum)
    a = jnp.exp(cum)
    return dict(sig=sig, sn=sn, q=q, f=f, k=k, a=a, e_neg=e_neg, e_end=e_end,
                q_dec=q * a, k_inv=k * e_neg, k_end=k * e_end, d=jnp.exp(last))


def _hgrn_dims(t, n_batch):
    s = t // n_batch
    rb = _pick(s, 256, CHUNK)
    return s, rb, s // rb, rb // CHUNK


def _hgrn_fwd(hq, hi, hf, lbl, *, n_batch, direction):
    t = hq.shape[0]
    reverse = direction == 1
    s, rb, nb, nc = _hgrn_dims(t, n_batch)

    def tmap(b, j):
        return (b * nb + ((nb - 1 - j) if reverse else j), 0)

    def smap(b, j):
        return (b * nb + ((nb - 1 - j) if reverse else j), 0, 0, 0)

    def body(hq_ref, hi_ref, hf_ref, lbl_ref, o_ref, st_ref, st_scr):
        @pl.when(pl.program_id(1) == 0)
        def _():
            st_scr[...] = jnp.zeros_like(st_scr)

        lb_all = _lower_bound(lbl_ref, direction)
        tri, _, bd, masks = _hgrn_consts(rb, reverse)
        tri16 = tri.astype(BF16)
        order = range(nc - 1, -1, -1) if reverse else range(nc)

        for p in range(HEAD_PAIRS):
            ls = slice(p * 128, (p + 1) * 128)
            w = _hgrn_block(hf_ref[:, ls], hq_ref[:, ls], lb_all[:, ls], tri16, reverse)
            v16 = hi_ref[:, ls].astype(BF16)
            qd16 = w["q_dec"].astype(BF16)
            ki16 = w["k_inv"].astype(BF16)
            ke16 = w["k_end"].astype(BF16)
            o_intra = jnp.zeros((rb, 128), F32)
            for mh in masks:
                sc = _dot_nt((w["q_dec"] * mh).astype(BF16), ki16)
                o_intra = o_intra + _dot(jnp.where(tri, sc, 0.0).astype(BF16), v16) * mh
            st = st_scr[p]
            for c in order:
                rs = slice(c * CHUNK, (c + 1) * CHUNK)
                o_ref[rs, ls] = o_intra[rs] + _dot_nt(qd16[rs], st.astype(BF16))
                st_ref[c, p] = st
                st = st * w["d"][c * CHUNK:c * CHUNK + 1] + jnp.where(bd, _dot_tn(v16[rs], ke16[rs]), 0.0)
            st_scr[p] = st

    blk = pl.BlockSpec((rb, A_WIDTH), tmap)
    return pl.pallas_call(
        body, name=f"hgrn_fwd_{direction}", grid=(n_batch, nb),
        in_specs=[blk, blk, blk, _full((2, 2, A_WIDTH))],
        out_specs=[blk, pl.BlockSpec((nc, HEAD_PAIRS, 128, 128), smap)],
        out_shape=[jax.ShapeDtypeStruct((t, A_WIDTH), F32),
                   jax.ShapeDtypeStruct((t // CHUNK, HEAD_PAIRS, 128, 128), F32)],
        scratch_shapes=[pltpu.VMEM((HEAD_PAIRS, 128, 128), F32)],
        compiler_params=_params(2),
    )(hq, hi, hf, lbl)


def _hgrn_bwd(hq, hi, hf, do, st, lbl, *, n_batch, direction):
    t = hq.shape[0]
    reverse = direction == 1
    s, rb, nb, nc = _hgrn_dims(t, n_batch)

    def tmap(b, j):
        return (b * nb + (j if reverse else (nb - 1 - j)), 0)

    def smap(b, j):
        return (b * nb + (j if reverse else (nb - 1 - j)), 0, 0, 0)

    def body(hq_ref, hi_ref, hf_ref, do_ref, st_ref, lbl_ref, dq_ref, dv_ref, dz_ref, dl_ref, g_scr, dlb_scr):
        b = pl.program_id(0)
        j = pl.program_id(1)

        @pl.when(jnp.logical_and(b == 0, j == 0))
        def _():
            dlb_scr[...] = jnp.zeros_like(dlb_scr)

        @pl.when(j == 0)
        def _():
            g_scr[...] = jnp.zeros_like(g_scr)

        lb_all = _lower_bound(lbl_ref, direction)
        tri, tri_t, bd, masks = _hgrn_consts(rb, reverse)
        tri16, tri_t16 = tri.astype(BF16), tri_t.astype(BF16)
        order = range(nc) if reverse else range(nc - 1, -1, -1)

        for p in range(HEAD_PAIRS):
            ls = slice(p * 128, (p + 1) * 128)
            lb = lb_all[:, ls]
            w = _hgrn_block(hf_ref[:, ls], hq_ref[:, ls], lb, tri16, reverse)
            dov = do_ref[:, ls]
            v16 = hi_ref[:, ls].astype(BF16)
            do16 = dov.astype(BF16)
            qd16 = w["q_dec"].astype(BF16)
            ki16 = w["k_inv"].astype(BF16)
            ke16 = w["k_end"].astype(BF16)
            dv = jnp.zeros((rb, 128), F32)
            dq_dec = jnp.zeros((rb, 128), F32)
            dk_inv = jnp.zeros((rb, 128), F32)
            for mh in masks:
                qm16 = (w["q_dec"] * mh).astype(BF16)
                dom16 = (dov * mh).astype(BF16)
                dp = jnp.where(tri, _dot_nt(dom16, v16), 0.0).astype(BF16)
                pm_t = jnp.where(tri_t, _dot_nt(ki16, qm16), 0.0).astype(BF16)
                dp_t = jnp.where(tri_t, _dot_nt(v16, dom16), 0.0).astype(BF16)
                dv = dv + _dot(pm_t, do16) * mh
                dq_dec = dq_dec + _dot(dp, ki16) * mh
                dk_inv = dk_inv + _dot(dp_t, qd16) * mh
            g = g_scr[p]
            dq_x, dk_end, dv_x, dd = [None] * nc, [None] * nc, [None] * nc, [None] * nc
            for c in order:
                rs = slice(c * CHUNK, (c + 1) * CHUNK)
                st_c = st_ref[c, p]
                g16 = g.astype(BF16)
                dq_x[c] = _dot(do16[rs], st_c.astype(BF16))
                dk_end[c] = _dot(v16[rs], g16)
                dv_x[c] = _dot_nt(ke16[rs], g16)
                dd[c] = jnp.broadcast_to(jnp.sum(g * st_c, axis=0, keepdims=True), (CHUNK, 128))
                g = g * w["d"][c * CHUNK:c * CHUNK + 1] + jnp.where(bd, _dot_tn(do16[rs], qd16[rs]), 0.0)
            g_scr[p] = g
            dq_dec = dq_dec + jnp.concatenate(dq_x, axis=0)
            dk_end = jnp.concatenate(dk_end, axis=0)
            dv = dv + jnp.concatenate(dv_x, axis=0)
            dd = jnp.concatenate(dd, axis=0)
            dke = dk_end * w["k_end"]
            dcum = dq_dec * w["q_dec"] - dk_inv * w["k_inv"] - dke
            dk = dk_inv * w["e_neg"] + dk_end * w["e_end"]
            dlast = _per_chunk(dke, lambda a: jnp.sum(a, axis=0, keepdims=True)) + dd * w["d"]
            dlf = _exact_left(tri_t16, dcum) + dlast
            tt = dlf / w["f"] - dk
            dq_ref[:, ls] = dq_dec * w["a"]
            dv_ref[:, ls] = dv
            dz_ref[:, ls] = (1.0 - lb) * w["sig"] * w["sn"] * tt
            dlb_scr[:, ls] += jnp.sum(w["sn"] * tt, axis=0, keepdims=True)

        @pl.when(jnp.logical_and(b == pl.num_programs(0) - 1, j == pl.num_programs(1) - 1))
        def _():
            d0 = dlb_scr[...] * lb_all * (1.0 - lb_all)
            dl_ref[0:1, :] = d0
            dl_ref[1:2, :] = -d0

    blk = pl.BlockSpec((rb, A_WIDTH), tmap)
    return pl.pallas_call(
        body, name=f"hgrn_bwd_{direction}", grid=(n_batch, nb),
        in_specs=[blk, blk, blk, blk, pl.BlockSpec((nc, HEAD_PAIRS, 128, 128), smap), _full((2, 2, A_WIDTH))],
        out_specs=[blk, blk, blk, _full((2, A_WIDTH))],
        out_shape=[jax.ShapeDtypeStruct((t, A_WIDTH), F32)] * 3 + [jax.ShapeDtypeStruct((2, A_WIDTH), F32)],
        scratch_shapes=[pltpu.VMEM((HEAD_PAIRS, 128, 128), F32), pltpu.VMEM((1, A_WIDTH), F32)],
        compiler_params=_params(2),
    )(hq, hi, hf, do, st, lbl)


def _swap_rope_halves(x):
    lane = _iota2(x.shape, 1)
    return jnp.where(lane < 32, pltpu.roll(x, 96, 1), pltpu.roll(x, 32, 1))


def _rms_fwd(xv, g):
    r = lax.rsqrt(jnp.mean(xv * xv, axis=-1, keepdims=True) + EPS)
    return (xv * r) * g


def _rms_bwd(dy, xv, g):
    r = lax.rsqrt(jnp.mean(xv * xv, axis=-1, keepdims=True) + EPS)
    xh = xv * r
    dxh = dy * g
    return r * (dxh - xh * jnp.mean(dxh * xh, axis=-1, keepdims=True)), dy * xh


def _mla_prep(cq, ckv, kr, g_qa, g_kva, w_q_p, w_kv_p, cosx, sinx, *, n_batch):
    t = cq.shape[0]
    s = t // n_batch
    tm = _row_tile(s)
    nt = s // tm

    def body(cq_ref, ckv_ref, kr_ref, gq_ref, gkv_ref, wq_ref, wkv_ref, cos_ref, sin_ref,
             q_ref, k_ref, v_ref, cqn_ref, ckvn_ref):
        cos, sin = cos_ref[...], sin_ref[...]
        cqn = _rms_fwd(cq_ref[...], gq_ref[...]).astype(BF16)
        ckvn = _rms_fwd(ckv_ref[...], gkv_ref[...]).astype(BF16)
        cqn_ref[...] = cqn
        ckvn_ref[...] = ckvn
        krv = kr_ref[...]
        kr_roped = (krv * cos + _swap_rope_halves(krv) * sin).astype(BF16)
        for h in range(B_HEADS):
            o = h * QK_PAD
            q_ref[:, o:o + 128] = _dot(cqn, wq_ref[:, o:o + 128]).astype(BF16)
            qr = _dot(cqn, wq_ref[:, o + 128:o + 256])
            q_ref[:, o + 128:o + 256] = (qr * cos + _swap_rope_halves(qr) * sin).astype(BF16)
            k_ref[:, o:o + 128] = _dot(ckvn, wkv_ref[:, h * 128:(h + 1) * 128]).astype(BF16)
            k_ref[:, o + 128:o + 256] = kr_roped
        v_ref[...] = _dot(ckvn, wkv_ref[:, 512:1024]).astype(BF16)

    tab = pl.BlockSpec((tm, 128), lambda i: (i % nt, 0))
    return pl.pallas_call(
        body, name="mla_prep", grid=(t // tm,),
        in_specs=[_rows(tm, Q_LORA), _rows(tm, KV_LORA), _rows(tm, 128), _full((1, Q_LORA)), _full((1, KV_LORA)),
                  _full((Q_LORA, 1024)), _full((KV_LORA, 1024)), tab, tab],
        out_specs=[_rows(tm, 1024), _rows(tm, 1024), _rows(tm, 512), _rows(tm, Q_LORA), _rows(tm, KV_LORA)],
        out_shape=[jax.ShapeDtypeStruct((t, 1024), BF16), jax.ShapeDtypeStruct((t, 1024), BF16),
                   jax.ShapeDtypeStruct((t, 512), BF16), jax.ShapeDtypeStruct((t, Q_LORA), BF16),
                   jax.ShapeDtypeStruct((t, KV_LORA), BF16)],
        compiler_params=_params(1),
    )(cq, ckv, kr, g_qa, g_kva, w_q_p, w_kv_p, cosx, sinx)


def _mla_prep_bwd(dqt, dk, dv, cq, ckv, g_qa, g_kva, w_q_p, w_kv_p, cosx, sinx, *, n_batch):
    t = cq.shape[0]
    s = t // n_batch
    tm = _row_tile(s)
    nt = s // tm

    def body(dqt_ref, dk_ref, dv_ref, cq_ref, ckv_ref, gq_ref, gkv_ref, wq_ref, wkv_ref, cos_ref, sin_ref,
             dcq_ref, dckv_ref, dkr_ref, dqp_ref, dkvp_ref, dgq_ref, dgkv_ref):
        cos, sin = cos_ref[...], sin_ref[...]
        first = pl.program_id(0) == 0

        def unrope(d):
            return d * cos + _swap_rope_halves(d * sin)

        dcqn = None
        dkr = None
        dckvn = None
        for h in range(B_HEADS):
            o = h * QK_PAD
            dq_h = jnp.transpose(dqt_ref[o:o + QK_PAD, :])
            dqn16 = dq_h[:, 0:128].astype(BF16)
            dqr16 = unrope(dq_h[:, 128:256]).astype(BF16)
            dqp_ref[:, o:o + 128] = dqn16
            dqp_ref[:, o + 128:o + 256] = dqr16
            part = _dot_nt(dqn16, wq_ref[:, o:o + 128]) + _dot_nt(dqr16, wq_ref[:, o + 128:o + 256])
            dcqn = part if dcqn is None else dcqn + part
            dkn16 = dk_ref[:, o:o + 128].astype(BF16)
            dkvp_ref[:, h * 128:(h + 1) * 128] = dkn16
            part = _dot_nt(dkn16, wkv_ref[:, h * 128:(h + 1) * 128])
            dckvn = part if dckvn is None else dckvn + part
            kr_part = dk_ref[:, o + 128:o + 256]
            dkr = kr_part if dkr is None else dkr + kr_part
        dv16 = dv_ref[...].astype(BF16)
        dkvp_ref[:, 512:1024] = dv16
        dckvn = dckvn + _dot_nt(dv16, wkv_ref[:, 512:1024])
        dkr_ref[...] = unrope(dkr)
        dcq, dgq = _rms_bwd(dcqn, cq_ref[...], gq_ref[...])
        dckv, dgkv = _rms_bwd(dckvn, ckv_ref[...], gkv_ref[...])
        dcq_ref[...] = dcq
        dckv_ref[...] = dckv
        _acc_rows(dgq_ref, dgq, first)
        _acc_rows(dgkv_ref, dgkv, first)

    tab = pl.BlockSpec((tm, 128), lambda i: (i % nt, 0))
    return pl.pallas_call(
        body, name="mla_prep_bwd", grid=(t // tm,),
        in_specs=[pl.BlockSpec((1024, tm), lambda i: (0, i)), _rows(tm, 1024), _rows(tm, 512), _rows(tm, Q_LORA),
                  _rows(tm, KV_LORA),
                  _full((1, Q_LORA)), _full((1, KV_LORA)), _full((Q_LORA, 1024)), _full((KV_LORA, 1024)), tab, tab],
        out_specs=[_rows(tm, Q_LORA), _rows(tm, KV_LORA), _rows(tm, 128), _rows(tm, 1024), _rows(tm, 1024),
                   _full((1, Q_LORA)), _full((1, KV_LORA))],
        out_shape=[jax.ShapeDtypeStruct((t, Q_LORA), F32), jax.ShapeDtypeStruct((t, KV_LORA), F32),
                   jax.ShapeDtypeStruct((t, 128), F32), jax.ShapeDtypeStruct((t, 1024), BF16),
                   jax.ShapeDtypeStruct((t, 1024), BF16), jax.ShapeDtypeStruct((1, Q_LORA), F32),
                   jax.ShapeDtypeStruct((1, KV_LORA), F32)],
        compiler_params=_params(1),
    )(dqt, dk, dv, cq, ckv, g_qa, g_kva, w_q_p, w_kv_p, cosx, sinx)


def _attn_dims(t, n_batch):
    s = t // n_batch
    tq = _pick(s, 256, 128)
    return s, tq, s // tq


def _grid_ends(n_axes):
    ids = [pl.program_id(a) for a in range(n_axes)]
    first = functools.reduce(jnp.logical_and, [i == 0 for i in ids])
    last = functools.reduce(jnp.logical_and, [i == pl.num_programs(a) - 1 for a, i in enumerate(ids)])
    return first, last


def _attn_fwd(q, k, v, wsrc, *, n_batch):
    t = q.shape[0]
    s, tq, nq = _attn_dims(t, n_batch)

    def body(q_ref, k_ref, v_ref, w_ref, o_ref, lse_ref, wall_ref, send_sems, recv_sems, local_sem):
        first, last = _grid_ends(3)

        @pl.when(first)
        def _():
            _gather_start(w_ref, wall_ref, send_sems, recv_sems, local_sem)

        @pl.when(last)
        def _():
            _gather_wait(w_ref, wall_ref, send_sems, recv_sems, local_sem)

        sc = _dot_nt(q_ref[...], k_ref[...]) * ATTN_SCALE
        m = jnp.max(sc, axis=-1, keepdims=True)
        p = jnp.exp(sc - m)
        l = jnp.sum(p, axis=-1, keepdims=True)
        o_ref[...] = _dot(p.astype(BF16), v_ref[...]) / l
        lse = m + jnp.log(l)
        lse_ref[...] = jnp.transpose(jnp.broadcast_to(lse, (tq, 128)))[0:1, :]

    return pl.pallas_call(
        body, name="attn_fwd", grid=(n_batch, B_HEADS, nq),
        in_specs=[pl.BlockSpec((tq, QK_PAD), lambda b, h, i: (b * nq + i, h)),
                  pl.BlockSpec((s, QK_PAD), lambda b, h, i: (b, h)),
                  pl.BlockSpec((s, B_V), lambda b, h, i: (b, h)), _HBM],
        out_specs=[pl.BlockSpec((tq, B_V), lambda b, h, i: (b * nq + i, h)),
                   pl.BlockSpec((None, 1, tq), lambda b, h, i: (h, 0, b * nq + i)), _HBM],
        out_shape=[jax.ShapeDtypeStruct((t, B_HEADS * B_V), F32), jax.ShapeDtypeStruct((B_HEADS, 1, t), F32),
                   jax.ShapeDtypeStruct((N_CHIPS,) + wsrc.shape, wsrc.dtype)],
        scratch_shapes=[pltpu.SemaphoreType.DMA((3,)), pltpu.SemaphoreType.DMA((3,)), pltpu.SemaphoreType.DMA(())],
        compiler_params=_params(3),
    )(q, k, v, wsrc)


def _attn_bwd(q, k, v, do16, lse, delta, gseg16, *, n_batch):
    t = q.shape[0]
    s, tk, nk = _attn_dims(t, n_batch)

    def body(q_ref, k_ref, v_ref, do_ref, lse_ref, dl_ref, g_ref, dqt_ref, dk_ref, dv_ref, recv_ref, send_sems, recv_sems):
        first, last = _grid_ends(3)

        @pl.when(first)
        def _():
            _scatter_start(g_ref, recv_ref, send_sems, recv_sems)

        @pl.when(last)
        def _():
            _scatter_wait(g_ref, recv_ref, send_sems, recv_sems)

        j = pl.program_id(2)
        qv, kv, dov = q_ref[...], k_ref[...], do_ref[...]
        pt = jnp.exp(_dot_nt(kv, qv) * ATTN_SCALE - lse_ref[...])
        dv_ref[...] = _dot(pt.astype(BF16), dov)
        dpt = _dot_nt(v_ref[...], dov)
        dst = (pt * (dpt - dl_ref[...]) * ATTN_SCALE).astype(BF16)
        dk_ref[...] = _dot(dst, qv)
        part = _dot_tn(kv, dst)

        @pl.when(j == 0)
        def _():
            dqt_ref[...] = part

        @pl.when(j != 0)
        def _():
            dqt_ref[...] += part

    row = pl.BlockSpec((None, 1, s), lambda b, h, j: (h, 0, b))
    return pl.pallas_call(
        body, name="attn_bwd", grid=(n_batch, B_HEADS, nk),
        in_specs=[pl.BlockSpec((s, QK_PAD), lambda b, h, j: (b, h)),
                  pl.BlockSpec((tk, QK_PAD), lambda b, h, j: (b * nk + j, h)),
                  pl.BlockSpec((tk, B_V), lambda b, h, j: (b * nk + j, h)),
                  pl.BlockSpec((s, B_V), lambda b, h, j: (b, h)), row, row, _HBM],
        out_specs=[pl.BlockSpec((QK_PAD, s), lambda b, h, j: (h, b)),
                   pl.BlockSpec((tk, QK_PAD), lambda b, h, j: (b * nk + j, h)),
                   pl.BlockSpec((tk, B_V), lambda b, h, j: (b * nk + j, h)), _HBM],
        out_shape=[jax.ShapeDtypeStruct((B_HEADS * QK_PAD, t), F32), jax.ShapeDtypeStruct((t, B_HEADS * QK_PAD), F32),
                   jax.ShapeDtypeStruct((t, B_HEADS * B_V), F32), jax.ShapeDtypeStruct((3,) + gseg16.shape[1:], gseg16.dtype)],
        scratch_shapes=[pltpu.SemaphoreType.DMA((3,)), pltpu.SemaphoreType.DMA((3,))],
        compiler_params=_params(3),
    )(q, k, v, do16, lse, delta, gseg16)


def _group_ones16():
    r = _iota2((A_WIDTH, A_WIDTH), 0) // 64
    c = _iota2((A_WIDTH, A_WIDTH), 1) // 64
    return (r == c).astype(BF16)


def _head_rms(o, ones16):
    return lax.rsqrt(_exact_right(o * o, ones16) * (1.0 / 64.0) + EPS)


def _out_fwd(o_f, o_r, hg, o_attn, x, g_hn, g_mla, w_out, g2):
    t = x.shape[0]
    tm = _row_tile(t)

    def body(of_ref, or_ref, hg_ref, oa_ref, x_ref, ghn_ref, gm_ref, w_ref, g2_ref, y_ref, x2_ref, h2_ref):
        ones16 = _group_ones16()
        o = of_ref[...] + or_ref[...]
        hgv = hg_ref[...]
        ya = ((o * _head_rms(o, ones16)) * ghn_ref[...]) * (hgv * _sigmoid(hgv))
        yb = _rms_fwd(oa_ref[...], gm_ref[...])
        ya16, yb16 = ya.astype(BF16), yb.astype(BF16)
        y_ref[:, 0:A_WIDTH] = ya16
        y_ref[:, A_WIDTH:D_MODEL] = yb16
        x2 = x_ref[...] + _dot(ya16, w_ref[0:A_WIDTH, :]) + _dot(yb16, w_ref[A_WIDTH:D_MODEL, :])
        x2_ref[...] = x2
        h2_ref[...] = _rms_fwd(x2, g2_ref[...]).astype(BF16)

    a512 = _rows(tm, A_WIDTH)
    return pl.pallas_call(
        body, name="out_fwd", grid=(t // tm,),
        in_specs=[a512, a512, a512, a512, _rows(tm, D_MODEL), _full((1, A_WIDTH)), _full((1, A_WIDTH)),
                  _full((D_MODEL, D_MODEL)), _full((1, D_MODEL))],
        out_specs=[_rows(tm, D_MODEL)] * 3,
        out_shape=[jax.ShapeDtypeStruct((t, D_MODEL), BF16), jax.ShapeDtypeStruct((t, D_MODEL), F32),
                   jax.ShapeDtypeStruct((t, D_MODEL), BF16)],
        compiler_params=_params(1),
    )(o_f, o_r, hg, o_attn, x, g_hn, g_mla, w_out, g2)


def _out_bwd(dx2_16, o_f, o_r, hg, o_attn, g_hn, g_mla, w_out):
    t = dx2_16.shape[0]
    tm = _row_tile(t)

    def body(dx_ref, of_ref, or_ref, hg_ref, oa_ref, ghn_ref, gm_ref, w_ref,
             do_ref, dhg_ref, doa_ref, dl_ref, dghn_ref, dgm_ref):
        first = pl.program_id(0) == 0
        ones16 = _group_ones16()
        dxv = dx_ref[...]
        dya = _dot_nt(dxv, w_ref[0:A_WIDTH, :])
        dyb = _dot_nt(dxv, w_ref[A_WIDTH:D_MODEL, :])
        o = of_ref[...] + or_ref[...]
        rh = _head_rms(o, ones16)
        oh = o * rh
        hgv = hg_ref[...]
        sg = _sigmoid(hgv)
        sl = hgv * sg
        ghn = ghn_ref[...]
        dhg_ref[...] = (dya * (oh * ghn)) * (sg * (1.0 + hgv * (1.0 - sg)))
        _acc_rows(dghn_ref, dya * sl * oh, first)
        doh = dya * sl * ghn
        do_ref[...] = rh * (doh - oh * (_exact_right(doh * oh, ones16) * (1.0 / 64.0)))
        oav = oa_ref[...]
        doa, dgm = _rms_bwd(dyb, oav, gm_ref[...])
        doa_ref[...] = doa.astype(BF16)
        _acc_rows(dgm_ref, dgm, first)
        sel16 = (_iota2((8, A_WIDTH), 0) == _iota2((8, A_WIDTH), 1) // B_V).astype(BF16)
        x1, x2, x3 = _split3(doa * oav)
        delta = _dot_nt(sel16, x1) + _dot_nt(sel16, x2) + _dot_nt(sel16, x3)
        for h in range(B_HEADS):
            dl_ref[h] = delta[h:h + 1, :]

    a512 = _rows(tm, A_WIDTH)
    return pl.pallas_call(
        body, name="out_bwd", grid=(t // tm,),
        in_specs=[_rows(tm, D_MODEL), a512, a512, a512, a512, _full((1, A_WIDTH)), _full((1, A_WIDTH)),
                  _full((D_MODEL, D_MODEL))],
        out_specs=[a512, a512, a512, pl.BlockSpec((B_HEADS, 1, tm), lambda i: (0, 0, i)),
                   _full((1, A_WIDTH)), _full((1, A_WIDTH))],
        out_shape=[jax.ShapeDtypeStruct((t, A_WIDTH), F32)] * 2
        + [jax.ShapeDtypeStruct((t, A_WIDTH), BF16), jax.ShapeDtypeStruct((B_HEADS, 1, t), F32)]
        + [jax.ShapeDtypeStruct((1, A_WIDTH), F32)] * 2,
        compiler_params=_params(1),
    )(dx2_16, o_f, o_r, hg, o_attn, g_hn, g_mla, w_out)


def _ffn_fwd_bwd(h2, x2, target, w_gate, w_up, w_down, g_f, g2):
    t = x2.shape[0]
    tm = _row_tile(t)
    inv_d = 1.0 / D_MODEL

    def body(h2_ref, x2_ref, tg_ref, wg_ref, wu_ref, wd_ref, gf_ref, g2_ref,
             act_ref, dgate_ref, dup_ref, dx3_ref, dx2_ref, dx2h_ref, loss_ref, dgf_ref, dg2_ref):
        first = pl.program_id(0) == 0
        h2v = h2_ref[...]
        gate = _dot(h2v, wg_ref[...])
        up = _dot(h2v, wu_ref[...])
        sg = _sigmoid(gate)
        sl = gate * sg
        act16 = (sl * up).astype(BF16)
        act_ref[...] = act16
        x2v = x2_ref[...]
        x3 = x2v + _dot(act16, wd_ref[...])
        r3 = lax.rsqrt(jnp.mean(x3 * x3, axis=-1, keepdims=True) + EPS)
        x3h = x3 * r3
        gf = gf_ref[...]
        err = x3h * gf - tg_ref[...]
        part = 0.5 * jnp.sum(jnp.mean(err * err, axis=-1, keepdims=True), axis=0, keepdims=True)

        @pl.when(first)
        def _():
            loss_ref[...] = jnp.zeros_like(loss_ref)

        loss_ref[...] += part
        dy = err * inv_d
        _acc_rows(dgf_ref, dy * x3h, first)
        dx3h = dy * gf
        dx3 = r3 * (dx3h - x3h * jnp.mean(dx3h * x3h, axis=-1, keepdims=True))
        dx3_16 = dx3.astype(BF16)
        dx3_ref[...] = dx3_16
        da = _dot_nt(dx3_16, wd_ref[...])
        dup16 = (da * sl).astype(BF16)
        dgate16 = (da * up * (sg * (1.0 + gate * (1.0 - sg)))).astype(BF16)
        dup_ref[...] = dup16
        dgate_ref[...] = dgate16
        dh2 = _dot_nt(dgate16, wg_ref[...]) + _dot_nt(dup16, wu_ref[...])
        dx2n, dg2 = _rms_bwd(dh2, x2v, g2_ref[...])
        _acc_rows(dg2_ref, dg2, first)
        dx2 = dx3 + dx2n
        dx2_ref[...] = dx2
        dx2h_ref[...] = dx2.astype(BF16)

    return pl.pallas_call(
        body, name="ffn_fwd_bwd", grid=(t // tm,),
        in_specs=[_rows(tm, D_MODEL), _rows(tm, D_MODEL), _rows(tm, D_MODEL), _full((D_MODEL, D_FF), True),
                  _full((D_MODEL, D_FF), True), _full((D_FF, D_MODEL), True), _full((1, D_MODEL)), _full((1, D_MODEL))],
        out_specs=[_rows(tm, D_FF), _rows(tm, D_FF), _rows(tm, D_FF), _rows(tm, D_MODEL), _rows(tm, D_MODEL),
                   _rows(tm, D_MODEL), _full((8, 128)), _full((1, D_MODEL)), _full((1, D_MODEL))],
        out_shape=[jax.ShapeDtypeStruct((t, D_FF), BF16)] * 3
        + [jax.ShapeDtypeStruct((t, D_MODEL), BF16), jax.ShapeDtypeStruct((t, D_MODEL), F32),
           jax.ShapeDtypeStruct((t, D_MODEL), BF16), jax.ShapeDtypeStruct((8, 128), F32),
           jax.ShapeDtypeStruct((1, D_MODEL), F32), jax.ShapeDtypeStruct((1, D_MODEL), F32)],
        compiler_params=_params(1),
    )(h2, x2, target, w_gate, w_up, w_down, g_f, g2)


def _wgrad(a, b, name):
    t, m = a.shape
    n = b.shape[1]
    bm = _pick(m, 1408)
    bn = _pick(n, 1664)
    tk = _pick(t, 1024, 16)
    nk = t // tk

    def body(a_ref, b_ref, o_ref):
        part = _dot_tn(a_ref[...], b_ref[...])

        @pl.when(pl.program_id(2) == 0)
        def _():
            o_ref[...] = part

        @pl.when(pl.program_id(2) != 0)
        def _():
            o_ref[...] += part

    return pl.pallas_call(
        body, name=name, grid=(m // bm, n // bn, nk),
        in_specs=[pl.BlockSpec((tk, bm), lambda i, j, k: (k, i)), pl.BlockSpec((tk, bn), lambda i, j, k: (k, j))],
        out_specs=pl.BlockSpec((bm, bn), lambda i, j, k: (i, j)),
        out_shape=jax.ShapeDtypeStruct((m, n), F32),
        compiler_params=_params(3),
    )(a, b)


def _rope_tables(seq):
    inv = 1.0 / (ROPE_THETA ** (jnp.arange(0, B_ROPE, 2, dtype=F32) / B_ROPE))
    ang = jnp.arange(seq, dtype=F32)[:, None] * inv[None, :]
    cos, sin = jnp.cos(ang), jnp.sin(ang)
    zeros = jnp.zeros((seq, 64), F32)
    return jnp.concatenate([cos, cos, zeros], axis=1), jnp.concatenate([-sin, sin, zeros], axis=1)


def _pad_weights(w_in, w_q_b, w_kv_b):
    w_in_p = jnp.pad(w_in, ((0, 0), (0, D_IN_PAD - D_IN)))
    w_q_p = jnp.pad(w_q_b.reshape(Q_LORA, B_HEADS, B_NOPE + B_ROPE), ((0, 0), (0, 0), (0, 64))).reshape(Q_LORA, 1024)
    kv = w_kv_b.reshape(KV_LORA, B_HEADS, B_NOPE + B_V)
    w_kv_p = jnp.concatenate([kv[:, :, :B_NOPE].reshape(KV_LORA, 512), kv[:, :, B_NOPE:].reshape(KV_LORA, 512)], axis=1)
    return w_in_p, w_q_p, w_kv_p


def _unpad_grads(g_in_p, g_q_p, g_kv_p):
    g_in = g_in_p[:, :D_IN]
    g_q = g_q_p.reshape(Q_LORA, B_HEADS, QK_PAD)[:, :, :B_NOPE + B_ROPE].reshape(Q_LORA, B_HEADS * (B_NOPE + B_ROPE))
    g_kv = jnp.concatenate([g_kv_p[:, :512].reshape(KV_LORA, B_HEADS, B_NOPE),
                            g_kv_p[:, 512:].reshape(KV_LORA, B_HEADS, B_V)], axis=2).reshape(KV_LORA, 1024)
    return g_in, g_q, g_kv


def _local_step(x, target, lbl, g1, g_hn, g_qa, g_kva, g_mla, g2, g_f, w_in, w_q_b, w_kv_b, wflat_b):
    n_batch, seq, _ = x.shape
    t = n_batch * seq
    x = x.reshape(t, D_MODEL)
    target = target.reshape(t, D_MODEL)
    w_in_p, w_q_p, w_kv_p = _pad_weights(w_in, w_q_b, w_kv_b)
    cosx, sinx = _rope_tables(seq)

    h1, hq, hi, hff, hfb, hg, cq, ckv, kr = _in_fwd(x, g1, w_in_p)
    o_f, st_f = _hgrn_fwd(hq, hi, hff, lbl, n_batch=n_batch, direction=0)
    o_r, st_r = _hgrn_fwd(hq, hi, hfb, lbl, n_batch=n_batch, direction=1)
    q, k, v, cqn, ckvn = _mla_prep(cq, ckv, kr, g_qa, g_kva, w_q_p, w_kv_p, cosx, sinx, n_batch=n_batch)
    o_attn, lse, wall_b = _attn_fwd(q, k, v, wflat_b, n_batch=n_batch)
    late = _split_flat(wall_b, _LATE, axis=1)
    w_out, w_gate, w_up, w_down = (_from_segments(n, late[n]) for n in _LATE)
    ycat, x2, h2 = _out_fwd(o_f, o_r, hg, o_attn, x, g_hn, g_mla, w_out, g2)
    act, dgate, dup, dx3_16, dx2, dx2_16, loss, dg_f, dg2 = _ffn_fwd_bwd(h2, x2, target, w_gate, w_up, w_down, g_f, g2)
    g_late = dict(w_out=_wgrad(ycat, dx2_16, "wgrad_out"), w_gate=_wgrad(h2, dgate, "wgrad_gate"),
                  w_up=_wgrad(h2, dup, "wgrad_up"), w_down=_wgrad(act, dx3_16, "wgrad_down"))
    gseg_b = jnp.concatenate([_to_segments(n, g_late[n]) for n in _LATE], axis=1)
    do_h, dhg, do_attn16, delta, dg_hn, dg_mla = _out_bwd(dx2_16, o_f, o_r, hg, o_attn, g_hn, g_mla, w_out)
    dqt, dk, dv, recv_b = _attn_bwd(q, k, v, do_attn16, lse, delta, gseg_b.astype(BF16), n_batch=n_batch)
    dcq, dckv, dkr, dqp16, dkvp16, dg_qa, dg_kva = _mla_prep_bwd(dqt, dk, dv, cq, ckv, g_qa, g_kva, w_q_p, w_kv_p,
                                                                  cosx, sinx, n_batch=n_batch)
    dq_f, dv_f, dz_f, dl_f = _hgrn_bwd(hq, hi, hff, do_h, st_f, lbl, n_batch=n_batch, direction=0)
    dq_r, dv_r, dz_r, dl_r = _hgrn_bwd(hq, hi, hfb, do_h, st_r, lbl, n_batch=n_batch, direction=1)
    dx, dproj16, dg1 = _in_bwd(dq_f, dq_r, dv_f, dv_r, dz_f, dz_r, dhg, hq, dcq, dckv, dkr, dx2, x, g1, w_in_p)

    gw_in, gw_q, gw_kv = _unpad_grads(_wgrad(h1, dproj16, "wgrad_in"), _wgrad(cqn, dqp16, "wgrad_q_b"),
                                      _wgrad(ckvn, dkvp16, "wgrad_kv_b"))
    grads = dict(
        norm1_g=dg1, w_in=gw_in, lb_logits=jnp.stack([dl_f, dl_r]), hgrn_norm_g=dg_hn, q_a_norm_g=dg_qa,
        w_q_b=gw_q, kv_a_norm_g=dg_kva, w_kv_b=gw_kv, mla_norm_g=dg_mla, norm2_g=dg2, final_norm_g=dg_f)
    return loss[0, 0], dx.reshape(n_batch, seq, D_MODEL), grads, gseg_b, recv_b


_HBM = pl.BlockSpec(memory_space=pltpu.HBM)
_MESH = pl.DeviceIdType.MESH


def _place():
    x, y, c = lax.axis_index("x"), lax.axis_index("y"), lax.axis_index("c")
    other_chips = [(1 - x, y), (x, 1 - y), (1 - x, 1 - y)]
    return x, y, c, other_chips


def _gather_copies(w_ref, wall_ref, send_sems, recv_sems, local_sem, base=0):
    x, y, c, chips = _place()
    mine = 2 * x + y

    def mk(j, chip_index, to):
        return pltpu.make_async_remote_copy(src_ref=w_ref, dst_ref=wall_ref.at[chip_index], send_sem=send_sems.at[base + j],
                                            recv_sem=recv_sems.at[base + j], device_id=to, device_id_type=_MESH)

    local = pltpu.make_async_copy(w_ref, wall_ref.at[mine], local_sem)
    sends = [mk(j, mine, (*chip, c)) for j, chip in enumerate(chips)]
    recvs = [mk(j, 2 * px + py, (x, y, c)) for j, (px, py) in enumerate(chips)]
    return local, sends, recvs


def _gather_start(*refs):
    local, sends, _ = _gather_copies(*refs)
    local.start()
    for cp in sends:
        cp.start()


def _gather_wait(*refs):
    local, sends, recvs = _gather_copies(*refs)
    for cp in recvs:
        cp.wait_recv()
    for cp in sends:
        cp.wait_send()
    local.wait()


def _scatter_copies(g_ref, recv_ref, send_sems, recv_sems):
    x, y, c, chips = _place()

    def mk(j, src_index, to):
        return pltpu.make_async_remote_copy(src_ref=g_ref.at[src_index], dst_ref=recv_ref.at[j], send_sem=send_sems.at[j],
                                            recv_sem=recv_sems.at[j], device_id=to, device_id_type=_MESH)

    sends = [mk(j, 2 * px + py, (px, py, c)) for j, (px, py) in enumerate(chips)]
    recvs = [mk(j, 0, (x, y, c)) for j in range(3)]
    return sends, recvs


def _scatter_start(*refs):
    for cp in _scatter_copies(*refs)[0]:
        cp.start()


def _scatter_wait(*refs):
    sends, recvs = _scatter_copies(*refs)
    for cp in recvs:
        cp.wait_recv()
    for cp in sends:
        cp.wait_send()


def _gather_weights(wflat16, lb8):
    def body(w_ref, lb_ref, wall_ref, lball_ref, send_sems, recv_sems, local_sems):
        big = (w_ref, wall_ref, send_sems, recv_sems, local_sems.at[0], 0)
        small = (lb_ref, lball_ref, send_sems, recv_sems, local_sems.at[1], 3)
        _gather_start(*big)
        _gather_start(*small)
        _gather_wait(*big)
        _gather_wait(*small)

    return pl.pallas_call(
        body, name="gather_weights", in_specs=[_HBM, _HBM], out_specs=[_HBM, _HBM],
        out_shape=[jax.ShapeDtypeStruct((N_CHIPS,) + wflat16.shape, BF16), jax.ShapeDtypeStruct((N_CHIPS,) + lb8.shape, F32)],
        scratch_shapes=[pltpu.SemaphoreType.DMA((6,)), pltpu.SemaphoreType.DMA((6,)), pltpu.SemaphoreType.DMA((2,))],
    )(wflat16, lb8)


def _scatter_grads(g16, small):
    def body(g_ref, s_ref, recv_ref, sall_ref, send_sems, recv_sems, local_sem):
        x, y, c, _ = _place()
        me = 4 * x + 2 * y + c
        flips = [(fx, fy, fc) for fx in (0, 1) for fy in (0, 1) for fc in (0, 1)][1:]

        def peer(f):
            return tuple((1 - a) if b else a for a, b in zip((x, y, c), f))

        def sm(r, index, to):
            return pltpu.make_async_remote_copy(src_ref=s_ref, dst_ref=sall_ref.at[index], send_sem=send_sems.at[3 + r],
                                                recv_sem=recv_sems.at[3 + r], device_id=to, device_id_type=_MESH)

        local = pltpu.make_async_copy(s_ref, sall_ref.at[me], local_sem)
        local.start()
        _scatter_start(g_ref, recv_ref, send_sems, recv_sems)
        sends = [sm(r, me, peer(f)) for r, f in enumerate(flips)]
        for cp in sends:
            cp.start()
        _scatter_wait(g_ref, recv_ref, send_sems, recv_sems)
        for r, f in enumerate(flips):
            px, py, pc = peer(f)
            sm(r, 4 * px + 2 * py + pc, (x, y, c)).wait_recv()
        for cp in sends:
            cp.wait_send()
        local.wait()

    return pl.pallas_call(
        body, name="scatter_grads", in_specs=[_HBM, _HBM], out_specs=[_HBM, _HBM],
        out_shape=[jax.ShapeDtypeStruct((3,) + g16.shape[1:], BF16), jax.ShapeDtypeStruct((8,) + small.shape, F32)],
        scratch_shapes=[pltpu.SemaphoreType.DMA((10,)), pltpu.SemaphoreType.DMA((10,)), pltpu.SemaphoreType.DMA(())],
    )(g16, small)


def _swap_sibling(pa, pb):
    def body(a_ref, b_ref, oa_ref, ob_ref, send_sems, recv_sems):
        x, y, c, _ = _place()
        cps = [pltpu.make_async_remote_copy(src_ref=src, dst_ref=dst, send_sem=send_sems.at[n], recv_sem=recv_sems.at[n],
                                            device_id=(x, y, 1 - c), device_id_type=_MESH)
               for n, (src, dst) in enumerate(((a_ref, oa_ref), (b_ref, ob_ref)))]
        for cp in cps:
            cp.start()
        for cp in cps:
            cp.wait()

    return pl.pallas_call(
        body, name="swap_sibling", in_specs=[_HBM, _HBM], out_specs=[_HBM, _HBM],
        out_shape=[jax.ShapeDtypeStruct(pa.shape, pa.dtype), jax.ShapeDtypeStruct(pb.shape, pb.dtype)],
        scratch_shapes=[pltpu.SemaphoreType.DMA((2,)), pltpu.SemaphoreType.DMA((2,))],
    )(pa, pb)


def _sum_segments(own, recv, name):
    rows = own.shape[0]
    tm = _pick(rows, 256, 16)

    def body(o_ref, r_ref, out_ref):
        acc = o_ref[...]
        for j in range(3):
            acc = acc + r_ref[j].astype(F32)
        out_ref[...] = acc

    return pl.pallas_call(
        body, name=name, grid=(rows // tm,),
        in_specs=[_rows(tm, FLAT_W), pl.BlockSpec((3, tm, FLAT_W), lambda i: (0, i, 0))],
        out_specs=_rows(tm, FLAT_W), out_shape=jax.ShapeDtypeStruct(own.shape, F32), compiler_params=_params(1),
    )(own, recv)


def _sum_devices(sall):
    def body(s_ref, o_ref):
        acc = s_ref[0]
        for d in range(1, 8):
            acc = acc + s_ref[d]
        o_ref[...] = acc

    return pl.pallas_call(body, name="sum_devices", out_shape=jax.ShapeDtypeStruct(sall.shape[1:], F32))(sall)


def _adamw(w, m, v, ga, gb, name):
    rows, cols = w.shape
    tm = _row_tile(rows)
    two = gb is not None

    def body(*refs):
        w_ref, m_ref, v_ref, ga_ref = refs[:4]
        g_ref, d_ref, m2_ref, v2_ref = refs[-4:]
        g = ga_ref[...] + refs[4][...] if two else ga_ref[...]
        m2 = ADAM_B1 * m_ref[...] + (1.0 - ADAM_B1) * g
        v2 = ADAM_B2 * v_ref[...] + (1.0 - ADAM_B2) * (g * g)
        m_hat = m2 / (1.0 - ADAM_B1 ** ADAM_STEP)
        v_hat = v2 / (1.0 - ADAM_B2 ** ADAM_STEP)
        g_ref[...] = g
        d_ref[...] = -ADAM_LR * (m_hat / (jnp.sqrt(v_hat) + ADAM_EPS) + ADAM_WD * w_ref[...])
        m2_ref[...] = m2
        v2_ref[...] = v2

    blk = _rows(tm, cols)
    args = (w, m, v, ga) + ((gb,) if two else ())
    return pl.pallas_call(
        body, name=name, grid=(rows // tm,), in_specs=[blk] * len(args), out_specs=[blk] * 4,
        out_shape=[jax.ShapeDtypeStruct(w.shape, F32)] * 4, compiler_params=_params(1),
    )(*args)


_COLUMN_SHARDED = ("w_in", "w_q_b", "w_kv_b", "w_gate", "w_up")
_FULL_SHAPES = dict(w_in=(D_MODEL, D_IN), w_q_b=(Q_LORA, 768), w_kv_b=(KV_LORA, 1024), w_out=(D_MODEL, D_MODEL),
                    w_gate=(D_MODEL, D_FF), w_up=(D_MODEL, D_FF), w_down=(D_FF, D_MODEL))
_SMALL = (("norm1_g", 1024), ("lb_logits", 2048), ("hgrn_norm_g", 512), ("q_a_norm_g", 384), ("kv_a_norm_g", 256),
          ("mla_norm_g", 512), ("norm2_g", 1024), ("final_norm_g", 1024))
_UPDATE_ROWS = 48


def _pad_rows(a, rows):
    return jnp.pad(a, ((0, rows - a.shape[0]), (0, 0)))


_EARLY = ("w_in", "w_q_b", "w_kv_b")
_LATE = ("w_out", "w_gate", "w_up", "w_down")
EARLY_ROWS = 960
LATE_ROWS = 2368


def _flatten_shards(shards, group, rows):
    sizes = dict(SHARD_ROWS)
    return _pad_rows(jnp.concatenate([shards[n].reshape(sizes[n], FLAT_W) for n in group], axis=0), rows)


def _split_flat(flat, group, axis=0):
    sizes = dict(SHARD_ROWS)
    out, off = {}, 0
    for n in group:
        out[n] = lax.slice_in_dim(flat, off, off + sizes[n], axis=axis)
        off += sizes[n]
    return out


def _to_segments(name, g):
    rows = dict(SHARD_ROWS)[name]
    if name in _COLUMN_SHARDED:
        r, c = g.shape
        g = g.reshape(r, N_CHIPS, c // N_CHIPS).transpose(1, 0, 2)
    return g.reshape(N_CHIPS, rows, FLAT_W)


def _from_segments(name, seg):
    r, c = _FULL_SHAPES[name]
    if name in _COLUMN_SHARDED:
        return seg.reshape(N_CHIPS, r, c // N_CHIPS).transpose(1, 0, 2).reshape(r, c)
    return seg.reshape(r, c)


def kernel(x, norm1_g, w_in, lb_logits, hgrn_norm_g, q_a_norm_g, w_q_b, kv_a_norm_g, w_kv_b, mla_norm_g, w_out, norm2_g, w_gate, w_up, w_down, final_norm_g, loss_target, m_norm1_g, m_w_in, m_lb_logits, m_hgrn_norm_g, m_q_a_norm_g, m_w_q_b, m_kv_a_norm_g, m_w_kv_b, m_mla_norm_g, m_w_out, m_norm2_g, m_w_gate, m_w_up, m_w_down, m_final_norm_g, v_norm1_g, v_w_in, v_lb_logits, v_hgrn_norm_g, v_q_a_norm_g, v_w_q_b, v_kv_a_norm_g, v_w_kv_b, v_mla_norm_g, v_w_out, v_norm2_g, v_w_gate, v_w_up, v_w_down, v_final_norm_g):
    names = ("norm1_g", "w_in", "lb_logits", "hgrn_norm_g", "q_a_norm_g", "w_q_b", "kv_a_norm_g", "w_kv_b", "mla_norm_g",
             "w_out", "norm2_g", "w_gate", "w_up", "w_down", "final_norm_g")
    w = dict(zip(names, (norm1_g, w_in, lb_logits, hgrn_norm_g, q_a_norm_g, w_q_b, kv_a_norm_g, w_kv_b, mla_norm_g,
                         w_out, norm2_g, w_gate, w_up, w_down, final_norm_g)))
    m = dict(zip(names, (m_norm1_g, m_w_in, m_lb_logits, m_hgrn_norm_g, m_q_a_norm_g, m_w_q_b, m_kv_a_norm_g, m_w_kv_b,
                         m_mla_norm_g, m_w_out, m_norm2_g, m_w_gate, m_w_up, m_w_down, m_final_norm_g)))
    v = dict(zip(names, (v_norm1_g, v_w_in, v_lb_logits, v_hgrn_norm_g, v_q_a_norm_g, v_w_q_b, v_kv_a_norm_g, v_w_kv_b,
                         v_mla_norm_g, v_w_out, v_norm2_g, v_w_gate, v_w_up, v_w_down, v_final_norm_g)))
    matrices = tuple(n for n, _ in SHARD_ROWS)
    chip = 2 * lax.axis_index("x") + lax.axis_index("y")

    w16 = {n: w[n][0].astype(BF16) for n in matrices}
    lb8 = _pad_rows(lb_logits.reshape(4, 128), 8)
    wall_a, lball = _gather_weights(_flatten_shards(w16, _EARLY, EARLY_ROWS), lb8)
    early = _split_flat(wall_a, _EARLY, axis=1)
    lbl = lball[:, :4].reshape(N_CHIPS, 2, 2, 128).transpose(1, 2, 0, 3).reshape(2, 2, A_WIDTH)

    loss_part, grad_x, g, gseg_b, recv_b = _local_step(
        x, loss_target, lbl, norm1_g, hgrn_norm_g, q_a_norm_g, kv_a_norm_g, mla_norm_g, norm2_g, final_norm_g[None, :],
        *(_from_segments(n, early[n]) for n in _EARLY), _flatten_shards(w16, _LATE, LATE_ROWS))
    loss = lax.psum(loss_part, ("x", "y", "c"))

    gseg_a = jnp.concatenate([_to_segments(n, g[n]) for n in _EARLY]
                             + [jnp.zeros((N_CHIPS, EARLY_ROWS - 952, FLAT_W), F32)], axis=1)
    small = _pad_rows(jnp.concatenate([g[n].reshape(-1) for n, _ in _SMALL]).reshape(-1, 128), SMALL_ROWS)
    recv_a, small_all = _scatter_grads(gseg_a.astype(BF16), small)
    part_a = _sum_segments(lax.dynamic_index_in_dim(gseg_a, chip, axis=0, keepdims=False), recv_a, "sum_early")
    part_b = _sum_segments(lax.dynamic_index_in_dim(gseg_b, chip, axis=0, keepdims=False), recv_b, "sum_late")
    sib_a, sib_b = _swap_sibling(part_a, part_b)
    small_sum = _sum_devices(small_all).reshape(-1)

    out = {}
    pa = {**_split_flat(part_a, _EARLY), **_split_flat(part_b, _LATE)}
    pb = {**_split_flat(sib_a, _EARLY), **_split_flat(sib_b, _LATE)}
    for n in matrices:
        shp = w[n].shape[1:]
        res = _adamw(w[n][0], m[n][0], v[n][0], pa[n].reshape(shp), pb[n].reshape(shp), f"adamw_{n}")
        out[n] = tuple(r[None] for r in res)

    small_g, off = {}, 0
    for n, size in _SMALL:
        small_g[n] = small_sum[off:off + size]
        off += size
    small_g["lb_logits"] = lax.dynamic_slice_in_dim(small_g["lb_logits"].reshape(2, 2, A_WIDTH), chip * 128, 128, axis=2)
    small_names = tuple(n for n, _ in _SMALL)

    def pack(d):
        return _pad_rows(jnp.concatenate([d[n].reshape(-1) for n in small_names]).reshape(-1, 128), _UPDATE_ROWS)

    res = _adamw(pack(w), pack(m), pack(v), pack(small_g), None, "adamw_small")
    off = 0
    flat = [r.reshape(-1) for r in res]
    for n in small_names:
        size = w[n].size
        out[n] = tuple(f[off:off + size].reshape(w[n].shape) for f in flat)
        off += size

    return (loss, grad_x) + tuple(out[n][i] for i in range(4) for n in names)
```

```python
import functools

import jax
import jax.numpy as jnp
from jax import lax
from jax.experimental import pallas as pl
from jax.experimental.pallas import tpu as pltpu

F32 = jnp.float32
BF16 = jnp.bfloat16

D_MODEL = 1024
A_WIDTH = 512
HEAD_PAIRS = 4
CHUNK = 64
B_HEADS = 4
B_NOPE = 128
B_ROPE = 64
B_V = 128
QK_PAD = 256
Q_LORA = 384
KV_LORA = 256
D_FF = 2816
D_IN = 3264
D_IN_PAD = 3328
IN_WIDTHS = (512, 512, 512, 512, 512, Q_LORA, KV_LORA, 128)
ROPE_THETA = 10000.0
EPS = 1e-6
ATTN_SCALE = (B_NOPE + B_ROPE) ** -0.5
LOG2E = 1.4426950408889634
SCALE_LOG2E = ATTN_SCALE * LOG2E

ADAM_LR = 0.001
ADAM_B1 = 0.9
ADAM_B2 = 0.999
ADAM_EPS = 1e-08
ADAM_WD = 0.01
ADAM_STEP = 10

VMEM_LIMIT_BYTES = 60 * 1024 * 1024
N_CHIPS = 4
SMALL_ROWS = 56


def _params(n_axes):
    return pltpu.CompilerParams(dimension_semantics=("arbitrary",) * n_axes,
                                vmem_limit_bytes=VMEM_LIMIT_BYTES)


def _dot(a, b):
    return jnp.dot(a, b, preferred_element_type=F32)


def _dot_nt(a, b):
    return lax.dot_general(a, b, (((1,), (1,)), ((), ())), preferred_element_type=F32)


def _dot_tn(a, b):
    return lax.dot_general(a, b, (((0,), (0,)), ((), ())), preferred_element_type=F32)


def _split3(x):
    x1 = x.astype(BF16)
    r = x - x1.astype(F32)
    x2 = r.astype(BF16)
    x3 = (r - x2.astype(F32)).astype(BF16)
    return x1, x2, x3


def _exact_left(m16, x):
    x1, x2, x3 = _split3(x)
    return _dot(m16, x1) + _dot(m16, x2) + _dot(m16, x3)


def _exact_right(x, m16):
    x1, x2, x3 = _split3(x)
    return _dot(x1, m16) + _dot(x2, m16) + _dot(x3, m16)


def _iota2(shape, dim):
    return lax.broadcasted_iota(jnp.int32, shape, dim)


def _sigmoid(x):
    return jax.nn.sigmoid(x)


def _pick(dim, cap, mult=128):
    if dim <= cap:
        return dim
    best = None
    for d in range(mult, cap + 1, mult):
        if dim % d == 0:
            best = d
    assert best is not None, (dim, cap, mult)
    return best


def _row_tile(t, cap=256):
    return _pick(t, cap, 8)


def _full(shape, single=False):
    if single:
        return pl.BlockSpec(shape, lambda *_: (0,) * len(shape), pipeline_mode=pl.Buffered(1))
    return pl.BlockSpec(shape, lambda *_: (0,) * len(shape))


def _rows(tm, width):
    return pl.BlockSpec((tm, width), lambda i: (i, 0))


def _acc_rows(ref, val, first):
    s = jnp.sum(val, axis=0, keepdims=True)

    @pl.when(first)
    def _():
        ref[...] = s

    @pl.when(jnp.logical_not(first))
    def _():
        ref[...] += s


def _in_fwd(x, g1, w_in_p):
    t = x.shape[0]
    tm = _row_tile(t)

    def body(x_ref, g_ref, w_ref, h_ref, *outs):
        xv = x_ref[...]
        r = lax.rsqrt(jnp.mean(xv * xv, axis=-1, keepdims=True) + EPS)
        h = ((xv * r) * g_ref[...]).astype(BF16)
        h_ref[...] = h
        off = 0
        for o_ref, w in zip(outs, IN_WIDTHS):
            o_ref[...] = _dot(h, w_ref[:, off:off + w])
            off += w

    return pl.pallas_call(
        body, name="in_fwd", grid=(t // tm,),
        in_specs=[_rows(tm, D_MODEL), _full((1, D_MODEL)), _full((D_MODEL, D_IN_PAD))],
        out_specs=[_rows(tm, D_MODEL)] + [_rows(tm, w) for w in IN_WIDTHS],
        out_shape=[jax.ShapeDtypeStruct((t, D_MODEL), BF16)]
        + [jax.ShapeDtypeStruct((t, w), F32) for w in IN_WIDTHS],
        compiler_params=_params(1),
    )(x, g1, w_in_p)


def _in_bwd(dq_f, dq_r, dv_f, dv_r, dz_f, dz_r, dhg, hq, dcq, dckv, dkr, dx2, x, g1, w_in_p):
    t = x.shape[0]
    tm = _row_tile(t)

    def body(dqf_ref, dqr_ref, dvf_ref, dvr_ref, dzf_ref, dzr_ref, dhg_ref, hq_ref, dcq_ref, dckv_ref,
             dkr_ref, dx2_ref, x_ref, g_ref, w_ref, dx_ref, dp_ref, dg_ref):
        hqv = hq_ref[...]
        sg = _sigmoid(hqv)
        dhq = (dqf_ref[...] + dqr_ref[...]) * (sg * (1.0 + hqv * (1.0 - sg)))
        pieces = (dhq, dvf_ref[...] + dvr_ref[...], dzf_ref[...], dzr_ref[...], dhg_ref[...],
                  dcq_ref[...], dckv_ref[...], dkr_ref[...])
        dh = None
        off = 0
        for p, w in zip(pieces, IN_WIDTHS):
            p16 = p.astype(BF16)
            dp_ref[:, off:off + w] = p16
            part = _dot_nt(p16, w_ref[:, off:off + w])
            dh = part if dh is None else dh + part
            off += w
        xv = x_ref[...]
        r = lax.rsqrt(jnp.mean(xv * xv, axis=-1, keepdims=True) + EPS)
        xh = xv * r
        _acc_rows(dg_ref, dh * xh, pl.program_id(0) == 0)
        dxh = dh * g_ref[...]
        dx_ref[...] = dx2_ref[...] + r * (dxh - xh * jnp.mean(dxh * xh, axis=-1, keepdims=True))

    a512 = _rows(tm, A_WIDTH)
    return pl.pallas_call(
        body, name="in_bwd", grid=(t // tm,),
        in_specs=[a512] * 8 + [_rows(tm, Q_LORA), _rows(tm, KV_LORA), _rows(tm, 128), _rows(tm, D_MODEL),
                               _rows(tm, D_MODEL), _full((1, D_MODEL)), _full((D_MODEL, D_IN_PAD))],
        out_specs=[_rows(tm, D_MODEL), _rows(tm, D_IN_PAD), _full((1, D_MODEL))],
        out_shape=[jax.ShapeDtypeStruct((t, D_MODEL), F32), jax.ShapeDtypeStruct((t, D_IN_PAD), BF16),
                   jax.ShapeDtypeStruct((1, D_MODEL), F32)],
        compiler_params=_params(1),
    )(dq_f, dq_r, dv_f, dv_r, dz_f, dz_r, dhg, hq, dcq, dckv, dkr, dx2, x, g1, w_in_p)


def _lower_bound(lbl_ref, direction):
    l0 = lbl_ref[direction, 0:1, :]
    l1 = lbl_ref[direction, 1:2, :]
    m = jnp.maximum(l0, l1)
    e0 = jnp.exp(l0 - m)
    e1 = jnp.exp(l1 - m)
    return e0 / (e0 + e1)


def _hgrn_consts(rb, reverse):
    row = _iota2((rb, rb), 0)
    col = _iota2((rb, rb), 1)
    same = (row // CHUNK) == (col // CHUNK)
    tri = jnp.logical_and(same, (col >= row) if reverse else (col <= row))
    tri_t = jnp.logical_and(same, (col <= row) if reverse else (col >= row))
    r128 = _iota2((128, 128), 0)
    c128 = _iota2((128, 128), 1)
    bd = (r128 < 64) == (c128 < 64)
    lane = _iota2((1, 128), 1)
    m0 = (lane < 64).astype(F32)
    return tri, tri_t, bd, (m0, 1.0 - m0)


def _per_chunk(x, fn):
    n = x.shape[0] // CHUNK
    return jnp.concatenate([jnp.broadcast_to(fn(x[c * CHUNK:(c + 1) * CHUNK]), (CHUNK, x.shape[1])) for c in range(n)],
                           axis=0)


def _hgrn_block(z, hqv, lb, tri16, reverse):
    sig = _sigmoid(z)
    sn = _sigmoid(-z)
    q = hqv * _sigmoid(hqv)
    f = lb + (1.0 - lb) * sig
    k = (1.0 - lb) * sn
    lf = jnp.log(f)
    cum = _exact_left(tri16, lf)
    last = _per_chunk(cum, (lambda a: a[0:1]) if reverse else (lambda a: a[CHUNK - 1:CHUNK]))
    e_neg = jnp.exp(-cum)
    e_end = jnp.exp(last - cum)
    a = jnp.exp(cum)
    return dict(sig=sig, sn=sn, q=q, f=f, k=k, a=a, e_neg=e_neg, e_end=e_end,
                q_dec=q * a, k_inv=k * e_neg, k_end=k * e_end, d=jnp.exp(last))


def _hgrn_dims(t, n_batch):
    s = t // n_batch
    rb = _pick(s, 256, CHUNK)
    return s, rb, s // rb, rb // CHUNK


def _hgrn_fwd(hq, hi, hf, lbl, *, n_batch, direction):
    t = hq.shape[0]
    reverse = direction == 1
    s, rb, nb, nc = _hgrn_dims(t, n_batch)

    def tmap(b, j):
        return (b * nb + ((nb - 1 - j) if reverse else j), 0)

    def smap(b, j):
        return (b * nb + ((nb - 1 - j) if reverse else j), 0, 0, 0)

    def body(hq_ref, hi_ref, hf_ref, lbl_ref, o_ref, st_ref, st_scr):
        @pl.when(pl.program_id(1) == 0)
        def _():
            st_scr[...] = jnp.zeros_like(st_scr)

        lb_all = _lower_bound(lbl_ref, direction)
        tri, _, bd, masks = _hgrn_consts(rb, reverse)
        tri16 = tri.astype(BF16)
        order = range(nc - 1, -1, -1) if reverse else range(nc)

        for p in range(HEAD_PAIRS):
            ls = slice(p * 128, (p + 1) * 128)
            w = _hgrn_block(hf_ref[:, ls], hq_ref[:, ls], lb_all[:, ls], tri16, reverse)
            v16 = hi_ref[:, ls].astype(BF16)
            qd16 = w["q_dec"].astype(BF16)
            ki16 = w["k_inv"].astype(BF16)
            ke16 = w["k_end"].astype(BF16)
            o_intra = jnp.zeros((rb, 128), F32)
            for mh in masks:
                sc = _dot_nt((w["q_dec"] * mh).astype(BF16), ki16)
                o_intra = o_intra + _dot(jnp.where(tri, sc, 0.0).astype(BF16), v16) * mh
            st = st_scr[p]
            for c in order:
                rs = slice(c * CHUNK, (c + 1) * CHUNK)
                o_ref[rs, ls] = o_intra[rs] + _dot_nt(qd16[rs], st.astype(BF16))
                st_ref[c, p] = st
                st = st * w["d"][c * CHUNK:c * CHUNK + 1] + jnp.where(bd, _dot_tn(v16[rs], ke16[rs]), 0.0)
            st_scr[p] = st

    blk = pl.BlockSpec((rb, A_WIDTH), tmap)
    return pl.pallas_call(
        body, name=f"hgrn_fwd_{direction}", grid=(n_batch, nb),
        in_specs=[blk, blk, blk, _full((2, 2, A_WIDTH))],
        out_specs=[blk, pl.BlockSpec((nc, HEAD_PAIRS, 128, 128), smap)],
        out_shape=[jax.ShapeDtypeStruct((t, A_WIDTH), F32),
                   jax.ShapeDtypeStruct((t // CHUNK, HEAD_PAIRS, 128, 128), F32)],
        scratch_shapes=[pltpu.VMEM((HEAD_PAIRS, 128, 128), F32)],
        compiler_params=_params(2),
    )(hq, hi, hf, lbl)


def _hgrn_bwd(hq, hi, hf, do, st, lbl, *, n_batch, direction):
    t = hq.shape[0]
    reverse = direction == 1
    s, rb, nb, nc = _hgrn_dims(t, n_batch)

    def tmap(b, j):
        return (b * nb + (j if reverse else (nb - 1 - j)), 0)

    def smap(b, j):
        return (b * nb + (j if reverse else (nb - 1 - j)), 0, 0, 0)

    def body(hq_ref, hi_ref, hf_ref, do_ref, st_ref, lbl_ref, dq_ref, dv_ref, dz_ref, dl_ref, g_scr, dlb_scr):
        b = pl.program_id(0)
        j = pl.program_id(1)

        @pl.when(jnp.logical_and(b == 0, j == 0))
        def _():
            dlb_scr[...] = jnp.zeros_like(dlb_scr)

        @pl.when(j == 0)
        def _():
            g_scr[...] = jnp.zeros_like(g_scr)

        lb_all = _lower_bound(lbl_ref, direction)
        tri, tri_t, bd, masks = _hgrn_consts(rb, reverse)
        tri16, tri_t16 = tri.astype(BF16), tri_t.astype(BF16)
        order = range(nc) if reverse else range(nc - 1, -1, -1)

        for p in range(HEAD_PAIRS):
            ls = slice(p * 128, (p + 1) * 128)
            lb = lb_all[:, ls]
            w = _hgrn_block(hf_ref[:, ls], hq_ref[:, ls], lb, tri16, reverse)
            dov = do_ref[:, ls]
            v16 = hi_ref[:, ls].astype(BF16)
            do16 = dov.astype(BF16)
            qd16 = w["q_dec"].astype(BF16)
            ki16 = w["k_inv"].astype(BF16)
            ke16 = w["k_end"].astype(BF16)
            dv = jnp.zeros((rb, 128), F32)
            dq_dec = jnp.zeros((rb, 128), F32)
            dk_inv = jnp.zeros((rb, 128), F32)
            for mh in masks:
                qm16 = (w["q_dec"] * mh).astype(BF16)
                dom16 = (dov * mh).astype(BF16)
                dp = jnp.where(tri, _dot_nt(dom16, v16), 0.0).astype(BF16)
                pm_t = jnp.where(tri_t, _dot_nt(ki16, qm16), 0.0).astype(BF16)
                dp_t = jnp.where(tri_t, _dot_nt(v16, dom16), 0.0).astype(BF16)
                dv = dv + _dot(pm_t, do16) * mh
                dq_dec = dq_dec + _dot(dp, ki16) * mh
                dk_inv = dk_inv + _dot(dp_t, qd16) * mh
            g = g_scr[p]
            dq_x, dk_end, dv_x, dd = [None] * nc, [None] * nc, [None] * nc, [None] * nc
            for c in order:
                rs = slice(c * CHUNK, (c + 1) * CHUNK)
                st_c = st_ref[c, p]
                g16 = g.astype(BF16)
                dq_x[c] = _dot(do16[rs], st_c.astype(BF16))
                dk_end[c] = _dot(v16[rs], g16)
                dv_x[c] = _dot_nt(ke16[rs], g16)
                dd[c] = jnp.broadcast_to(jnp.sum(g * st_c, axis=0, keepdims=True), (CHUNK, 128))
                g = g * w["d"][c * CHUNK:c * CHUNK + 1] + jnp.where(bd, _dot_tn(do16[rs], qd16[rs]), 0.0)
            g_scr[p] = g
            dq_dec = dq_dec + jnp.concatenate(dq_x, axis=0)
            dk_end = jnp.concatenate(dk_end, axis=0)
            dv = dv + jnp.concatenate(dv_x, axis=0)
            dd = jnp.concatenate(dd, axis=0)
            dke = dk_end * w["k_end"]
            dcum = dq_dec * w["q_dec"] - dk_inv * w["k_inv"] - dke
            dk = dk_inv * w["e_neg"] + dk_end * w["e_end"]
            dlast = _per_chunk(dke, lambda a: jnp.sum(a, axis=0, keepdims=True)) + dd * w["d"]
            dlf = _exact_left(tri_t16, dcum) + dlast
            tt = dlf / w["f"] - dk
            dq_ref[:, ls] = dq_dec * w["a"]
            dv_ref[:, ls] = dv
            dz_ref[:, ls] = (1.0 - lb) * w["sig"] * w["sn"] * tt
            dlb_scr[:, ls] += jnp.sum(w["sn"] * tt, axis=0, keepdims=True)

        @pl.when(jnp.logical_and(b == pl.num_programs(0) - 1, j == pl.num_programs(1) - 1))
        def _():
            d0 = dlb_scr[...] * lb_all * (1.0 - lb_all)
            dl_ref[0:1, :] = d0
            dl_ref[1:2, :] = -d0

    blk = pl.BlockSpec((rb, A_WIDTH), tmap)
    return pl.pallas_call(
        body, name=f"hgrn_bwd_{direction}", grid=(n_batch, nb),
        in_specs=[blk, blk, blk, blk, pl.BlockSpec((nc, HEAD_PAIRS, 128, 128), smap), _full((2, 2, A_WIDTH))],
        out_specs=[blk, blk, blk, _full((2, A_WIDTH))],
        out_shape=[jax.ShapeDtypeStruct((t, A_WIDTH), F32)] * 3 + [jax.ShapeDtypeStruct((2, A_WIDTH), F32)],
        scratch_shapes=[pltpu.VMEM((HEAD_PAIRS, 128, 128), F32), pltpu.VMEM((1, A_WIDTH), F32)],
        compiler_params=_params(2),
    )(hq, hi, hf, do, st, lbl)


def _swap_rope_halves(x):
    lane = _iota2(x.shape, 1)
    return jnp.where(lane < 32, pltpu.roll(x, 96, 1), pltpu.roll(x, 32, 1))


def _rms_fwd(xv, g):
    r = lax.rsqrt(jnp.mean(xv * xv, axis=-1, keepdims=True) + EPS)
    return (xv * r) * g


def _rms_bwd(dy, xv, g):
    r = lax.rsqrt(jnp.mean(xv * xv, axis=-1, keepdims=True) + EPS)
    xh = xv * r
    dxh = dy * g
    return r * (dxh - xh * jnp.mean(dxh * xh, axis=-1, keepdims=True)), dy * xh


def _mla_prep(cq, ckv, kr, g_qa, g_kva, w_q_p, w_kv_p, cosx, sinx, *, n_batch):
    t = cq.shape[0]
    s = t // n_batch
    tm = _row_tile(s)
    nt = s // tm

    def body(cq_ref, ckv_ref, kr_ref, gq_ref, gkv_ref, wq_ref, wkv_ref, cos_ref, sin_ref,
             q_ref, k_ref, v_ref, cqn_ref, ckvn_ref):
        cos, sin = cos_ref[...], sin_ref[...]
        cqn = _rms_fwd(cq_ref[...], gq_ref[...]).astype(BF16)
        ckvn = _rms_fwd(ckv_ref[...], gkv_ref[...]).astype(BF16)
        cqn_ref[...] = cqn
        ckvn_ref[...] = ckvn
        krv = kr_ref[...]
        kr_roped = (krv * cos + _swap_rope_halves(krv) * sin).astype(BF16)
        for h in range(B_HEADS):
            o = h * QK_PAD
            q_ref[:, o:o + 128] = _dot(cqn, wq_ref[:, o:o + 128]).astype(BF16)
            qr = _dot(cqn, wq_ref[:, o + 128:o + 256])
            q_ref[:, o + 128:o + 256] = (qr * cos + _swap_rope_halves(qr) * sin).astype(BF16)
            k_ref[:, o:o + 128] = _dot(ckvn, wkv_ref[:, h * 128:(h + 1) * 128]).astype(BF16)
            k_ref[:, o + 128:o + 256] = kr_roped
        v_ref[...] = _dot(ckvn, wkv_ref[:, 512:1024]).astype(BF16)

    tab = pl.BlockSpec((tm, 128), lambda i: (i % nt, 0))
    return pl.pallas_call(
        body, name="mla_prep", grid=(t // tm,),
        in_specs=[_rows(tm, Q_LORA), _rows(tm, KV_LORA), _rows(tm, 128), _full((1, Q_LORA)), _full((1, KV_LORA)),
                  _full((Q_LORA, 1024)), _full((KV_LORA, 1024)), tab, tab],
        out_specs=[_rows(tm, 1024), _rows(tm, 1024), _rows(tm, 512), _rows(tm, Q_LORA), _rows(tm, KV_LORA)],
        out_shape=[jax.ShapeDtypeStruct((t, 1024), BF16), jax.ShapeDtypeStruct((t, 1024), BF16),
                   jax.ShapeDtypeStruct((t, 512), BF16), jax.ShapeDtypeStruct((t, Q_LORA), BF16),
                   jax.ShapeDtypeStruct((t, KV_LORA), BF16)],
        compiler_params=_params(1),
    )(cq, ckv, kr, g_qa, g_kva, w_q_p, w_kv_p, cosx, sinx)


def _mla_prep_bwd(dqt, dk, dv, cq, ckv, g_qa, g_kva, w_q_p, w_kv_p, cosx, sinx, *, n_batch):
    t = cq.shape[0]
    s = t // n_batch
    tm = _row_tile(s)
    nt = s // tm

    def body(dqt_ref, dk_ref, dv_ref, cq_ref, ckv_ref, gq_ref, gkv_ref, wq_ref, wkv_ref, cos_ref, sin_ref,
             dcq_ref, dckv_ref, dkr_ref, dqp_ref, dkvp_ref, dgq_ref, dgkv_ref):
        cos, sin = cos_ref[...], sin_ref[...]
        first = pl.program_id(0) == 0

        def unrope(d):
            return d * cos + _swap_rope_halves(d * sin)

        dcqn = None
        dkr = None
        dckvn = None
        for h in range(B_HEADS):
            o = h * QK_PAD
            dq_h = jnp.transpose(dqt_ref[o:o + QK_PAD, :])
            dqn16 = dq_h[:, 0:128].astype(BF16)
            dqr16 = unrope(dq_h[:, 128:256]).astype(BF16)
            dqp_ref[:, o:o + 128] = dqn16
            dqp_ref[:, o + 128:o + 256] = dqr16
            part = _dot_nt(dqn16, wq_ref[:, o:o + 128]) + _dot_nt(dqr16, wq_ref[:, o + 128:o + 256])
            dcqn = part if dcqn is None else dcqn + part
            dkn16 = dk_ref[:, o:o + 128].astype(BF16)
            dkvp_ref[:, h * 128:(h + 1) * 128] = dkn16
            part = _dot_nt(dkn16, wkv_ref[:, h * 128:(h + 1) * 128])
            dckvn = part if dckvn is None else dckvn + part
            kr_part = dk_ref[:, o + 128:o + 256]
            dkr = kr_part if dkr is None else dkr + kr_part
        dv16 = dv_ref[...].astype(BF16)
        dkvp_ref[:, 512:1024] = dv16
        dckvn = dckvn + _dot_nt(dv16, wkv_ref[:, 512:1024])
        dkr_ref[...] = unrope(dkr)
        dcq, dgq = _rms_bwd(dcqn, cq_ref[...], gq_ref[...])
        dckv, dgkv = _rms_bwd(dckvn, ckv_ref[...], gkv_ref[...])
        dcq_ref[...] = dcq
        dckv_ref[...] = dckv
        _acc_rows(dgq_ref, dgq, first)
        _acc_rows(dgkv_ref, dgkv, first)

    tab = pl.BlockSpec((tm, 128), lambda i: (i % nt, 0))
    return pl.pallas_call(
        body, name="mla_prep_bwd", grid=(t // tm,),
        in_specs=[pl.BlockSpec((1024, tm), lambda i: (0, i)), _rows(tm, 1024), _rows(tm, 512), _rows(tm, Q_LORA),
                  _rows(tm, KV_LORA),
                  _full((1, Q_LORA)), _full((1, KV_LORA)), _full((Q_LORA, 1024)), _full((KV_LORA, 1024)), tab, tab],
        out_specs=[_rows(tm, Q_LORA), _rows(tm, KV_LORA), _rows(tm, 128), _rows(tm, 1024), _rows(tm, 1024),
                   _full((1, Q_LORA)), _full((1, KV_LORA))],
        out_shape=[jax.ShapeDtypeStruct((t, Q_LORA), F32), jax.ShapeDtypeStruct((t, KV_LORA), F32),
                   jax.ShapeDtypeStruct((t, 128), F32), jax.ShapeDtypeStruct((t, 1024), BF16),
                   jax.ShapeDtypeStruct((t, 1024), BF16), jax.ShapeDtypeStruct((1, Q_LORA), F32),
                   jax.ShapeDtypeStruct((1, KV_LORA), F32)],
        compiler_params=_params(1),
    )(dqt, dk, dv, cq, ckv, g_qa, g_kva, w_q_p, w_kv_p, cosx, sinx)


def _attn_dims(t, n_batch):
    s = t // n_batch
    tq = _pick(s, 256, 128)
    return s, tq, s // tq


def _grid_ends(n_axes):
    ids = [pl.program_id(a) for a in range(n_axes)]
    first = functools.reduce(jnp.logical_and, [i == 0 for i in ids])
    last = functools.reduce(jnp.logical_and, [i == pl.num_programs(a) - 1 for a, i in enumerate(ids)])
    return first, last


def _attn_fwd(q, k, v, wsrcs, *, n_batch):
    t = q.shape[0]
    s, tq, nq = _attn_dims(t, n_batch)
    nw = len(wsrcs)

    def body(q_ref, k_ref, v_ref, *refs):
        w_refs, (o_ref, lse_ref), wall_refs = refs[:nw], refs[nw:nw + 2], refs[nw + 2:2 * nw + 2]
        send_sems, recv_sems, local_sems = refs[2 * nw + 2:]
        first, last = _grid_ends(3)

        @pl.when(first)
        def _():
            for i in range(nw):
                _gather_start(w_refs[i], wall_refs[i], send_sems, recv_sems, local_sems.at[i], 3 * i)

        @pl.when(last)
        def _():
            for i in range(nw):
                _gather_wait(w_refs[i], wall_refs[i], send_sems, recv_sems, local_sems.at[i], 3 * i)

        raw = _dot_nt(q_ref[...], k_ref[...])
        m = jnp.max(raw, axis=-1, keepdims=True)
        p = jnp.exp2((raw - m) * SCALE_LOG2E)
        l = jnp.sum(p, axis=-1, keepdims=True)
        o_ref[...] = _dot(p.astype(BF16), v_ref[...]) / l
        lse2 = m * SCALE_LOG2E + jnp.log(l) * LOG2E
        lse_ref[...] = jnp.transpose(jnp.broadcast_to(lse2, (tq, 128)))[0:1, :]

    return pl.pallas_call(
        body, name="attn_fwd", grid=(n_batch, B_HEADS, nq),
        in_specs=[pl.BlockSpec((tq, QK_PAD), lambda b, h, i: (b * nq + i, h)),
                  pl.BlockSpec((s, QK_PAD), lambda b, h, i: (b, h)),
                  pl.BlockSpec((s, B_V), lambda b, h, i: (b, h))] + [_HBM] * nw,
        out_specs=[pl.BlockSpec((tq, B_V), lambda b, h, i: (b * nq + i, h)),
                   pl.BlockSpec((None, 1, tq), lambda b, h, i: (h, 0, b * nq + i))] + [_HBM] * nw,
        out_shape=[jax.ShapeDtypeStruct((t, B_HEADS * B_V), F32), jax.ShapeDtypeStruct((B_HEADS, 1, t), F32)]
        + [jax.ShapeDtypeStruct((N_CHIPS,) + w.shape, w.dtype) for w in wsrcs],
        scratch_shapes=[pltpu.SemaphoreType.DMA((3 * nw,)), pltpu.SemaphoreType.DMA((3 * nw,)),
                        pltpu.SemaphoreType.DMA((nw,))],
        compiler_params=_params(3),
    )(q, k, v, *wsrcs)


def _attn_bwd(q, k, v, do16, lse, delta, gsegs, *, n_batch):
    t = q.shape[0]
    s = t // n_batch
    tk = _pick(s, 512, 128)
    nk = s // tk
    ng = len(gsegs)

    def body(q_ref, k_ref, v_ref, do_ref, lse_ref, dl_ref, *refs):
        g_refs, (dqt_ref, dk_ref, dv_ref), recv_refs = refs[:ng], refs[ng:ng + 3], refs[ng + 3:2 * ng + 3]
        send_sems, recv_sems = refs[2 * ng + 3:]
        first, last = _grid_ends(3)

        @pl.when(first)
        def _():
            for i in range(ng):
                _scatter_start(g_refs[i], recv_refs[i], send_sems, recv_sems, 3 * i)

        @pl.when(last)
        def _():
            for i in range(ng):
                _scatter_wait(g_refs[i], recv_refs[i], send_sems, recv_sems, 3 * i)

        j = pl.program_id(2)
        qv, kv, dov = q_ref[...], k_ref[...], do_ref[...]
        pt = jnp.exp2(_dot_nt(kv, qv) * SCALE_LOG2E - lse_ref[...])
        dv_ref[...] = _dot(pt.astype(BF16), dov)
        dpt = _dot_nt(v_ref[...], dov)
        dst = (pt * (dpt - dl_ref[...])).astype(BF16)
        dk_ref[...] = _dot(dst, qv) * ATTN_SCALE
        part = _dot_tn(kv, dst)

        @pl.when(j == 0)
        def _():
            dqt_ref[...] = part

        @pl.when(j != 0)
        def _():
            dqt_ref[...] += part

        @pl.when(j == nk - 1)
        def _():
            dqt_ref[...] = dqt_ref[...] * ATTN_SCALE

    row = pl.BlockSpec((None, 1, s), lambda b, h, j: (h, 0, b))
    return pl.pallas_call(
        body, name="attn_bwd", grid=(n_batch, B_HEADS, nk),
        in_specs=[pl.BlockSpec((s, QK_PAD), lambda b, h, j: (b, h)),
                  pl.BlockSpec((tk, QK_PAD), lambda b, h, j: (b * nk + j, h)),
                  pl.BlockSpec((tk, B_V), lambda b, h, j: (b * nk + j, h)),
                  pl.BlockSpec((s, B_V), lambda b, h, j: (b, h)), row, row] + [_HBM] * ng,
        out_specs=[pl.BlockSpec((QK_PAD, s), lambda b, h, j: (h, b)),
                   pl.BlockSpec((tk, QK_PAD), lambda b, h, j: (b * nk + j, h)),
                   pl.BlockSpec((tk, B_V), lambda b, h, j: (b * nk + j, h))] + [_HBM] * ng,
        out_shape=[jax.ShapeDtypeStruct((B_HEADS * QK_PAD, t), F32), jax.ShapeDtypeStruct((t, B_HEADS * QK_PAD), F32),
                   jax.ShapeDtypeStruct((t, B_HEADS * B_V), F32)]
        + [jax.ShapeDtypeStruct((3,) + g.shape[1:], g.dtype) for g in gsegs],
        scratch_shapes=[pltpu.SemaphoreType.DMA((3 * ng,)), pltpu.SemaphoreType.DMA((3 * ng,))],
        compiler_params=_params(3),
    )(q, k, v, do16, lse, delta, *gsegs)


def _group_ones16():
    r = _iota2((A_WIDTH, A_WIDTH), 0) // 64
    c = _iota2((A_WIDTH, A_WIDTH), 1) // 64
    return (r == c).astype(BF16)


def _head_rms(o, ones16):
    return lax.rsqrt(_exact_right(o * o, ones16) * (1.0 / 64.0) + EPS)


def _out_fwd(o_f, o_r, hg, o_attn, x, g_hn, g_mla, w_out, g2):
    t = x.shape[0]
    tm = _row_tile(t)

    def body(of_ref, or_ref, hg_ref, oa_ref, x_ref, ghn_ref, gm_ref, w_ref, g2_ref, y_ref, x2_ref, h2_ref):
        ones16 = _group_ones16()
        o = of_ref[...] + or_ref[...]
        hgv = hg_ref[...]
        ya = ((o * _head_rms(o, ones16)) * ghn_ref[...]) * (hgv * _sigmoid(hgv))
        yb = _rms_fwd(oa_ref[...], gm_ref[...])
        ya16, yb16 = ya.astype(BF16), yb.astype(BF16)
        y_ref[:, 0:A_WIDTH] = ya16
        y_ref[:, A_WIDTH:D_MODEL] = yb16
        x2 = x_ref[...] + _dot(ya16, w_ref[0:A_WIDTH, :]) + _dot(yb16, w_ref[A_WIDTH:D_MODEL, :])
        x2_ref[...] = x2
        h2_ref[...] = _rms_fwd(x2, g2_ref[...]).astype(BF16)

    a512 = _rows(tm, A_WIDTH)
    return pl.pallas_call(
        body, name="out_fwd", grid=(t // tm,),
        in_specs=[a512, a512, a512, a512, _rows(tm, D_MODEL), _full((1, A_WIDTH)), _full((1, A_WIDTH)),
                  _full((D_MODEL, D_MODEL)), _full((1, D_MODEL))],
        out_specs=[_rows(tm, D_MODEL)] * 3,
        out_shape=[jax.ShapeDtypeStruct((t, D_MODEL), BF16), jax.ShapeDtypeStruct((t, D_MODEL), F32),
                   jax.ShapeDtypeStruct((t, D_MODEL), BF16)],
        compiler_params=_params(1),
    )(o_f, o_r, hg, o_attn, x, g_hn, g_mla, w_out, g2)


def _out_bwd(dx2_16, o_f, o_r, hg, o_attn, g_hn, g_mla, w_out):
    t = dx2_16.shape[0]
    tm = _row_tile(t)

    def body(dx_ref, of_ref, or_ref, hg_ref, oa_ref, ghn_ref, gm_ref, w_ref,
             do_ref, dhg_ref, doa_ref, dl_ref, dghn_ref, dgm_ref):
        first = pl.program_id(0) == 0
        ones16 = _group_ones16()
        dxv = dx_ref[...]
        dya = _dot_nt(dxv, w_ref[0:A_WIDTH, :])
        dyb = _dot_nt(dxv, w_ref[A_WIDTH:D_MODEL, :])
        o = of_ref[...] + or_ref[...]
        rh = _head_rms(o, ones16)
        oh = o * rh
        hgv = hg_ref[...]
        sg = _sigmoid(hgv)
        sl = hgv * sg
        ghn = ghn_ref[...]
        dhg_ref[...] = (dya * (oh * ghn)) * (sg * (1.0 + hgv * (1.0 - sg)))
        _acc_rows(dghn_ref, dya * sl * oh, first)
        doh = dya * sl * ghn
        do_ref[...] = rh * (doh - oh * (_exact_right(doh * oh, ones16) * (1.0 / 64.0)))
        oav = oa_ref[...]
        doa, dgm = _rms_bwd(dyb, oav, gm_ref[...])
        doa_ref[...] = doa.astype(BF16)
        _acc_rows(dgm_ref, dgm, first)
        sel16 = (_iota2((8, A_WIDTH), 0) == _iota2((8, A_WIDTH), 1) // B_V).astype(BF16)
        x1, x2, x3 = _split3(doa * oav)
        delta = _dot_nt(sel16, x1) + _dot_nt(sel16, x2) + _dot_nt(sel16, x3)
        for h in range(B_HEADS):
            dl_ref[h] = delta[h:h + 1, :]

    a512 = _rows(tm, A_WIDTH)
    return pl.pallas_call(
        body, name="out_bwd", grid=(t // tm,),
        in_specs=[_rows(tm, D_MODEL), a512, a512, a512, a512, _full((1, A_WIDTH)), _full((1, A_WIDTH)),
                  _full((D_MODEL, D_MODEL))],
        out_specs=[a512, a512, a512, pl.BlockSpec((B_HEADS, 1, tm), lambda i: (0, 0, i)),
                   _full((1, A_WIDTH)), _full((1, A_WIDTH))],
        out_shape=[jax.ShapeDtypeStruct((t, A_WIDTH), F32)] * 2
        + [jax.ShapeDtypeStruct((t, A_WIDTH), BF16), jax.ShapeDtypeStruct((B_HEADS, 1, t), F32)]
        + [jax.ShapeDtypeStruct((1, A_WIDTH), F32)] * 2,
        compiler_params=_params(1),
    )(dx2_16, o_f, o_r, hg, o_attn, g_hn, g_mla, w_out)


def _ffn_fwd_bwd(h2, x2, target, w_gate, w_up, w_down, g_f, g2):
    t = x2.shape[0]
    tm = _row_tile(t)
    inv_d = 1.0 / D_MODEL

    def body(h2_ref, x2_ref, tg_ref, wg_ref, wu_ref, wd_ref, gf_ref, g2_ref,
             act_ref, dgate_ref, dup_ref, dx3_ref, dx2_ref, dx2h_ref, loss_ref, dgf_ref, dg2_ref):
        first = pl.program_id(0) == 0
        h2v = h2_ref[...]
        gate = _dot(h2v, wg_ref[...])
        up = _dot(h2v, wu_ref[...])
        sg = _sigmoid(gate)
        sl = gate * sg
        act16 = (sl * up).astype(BF16)
        act_ref[...] = act16
        x2v = x2_ref[...]
        x3 = x2v + _dot(act16, wd_ref[...])
        r3 = lax.rsqrt(jnp.mean(x3 * x3, axis=-1, keepdims=True) + EPS)
        x3h = x3 * r3
        gf = gf_ref[...]
        err = x3h * gf - tg_ref[...]
        part = 0.5 * jnp.sum(jnp.mean(err * err, axis=-1, keepdims=True), axis=0, keepdims=True)

        @pl.when(first)
        def _():
            loss_ref[...] = jnp.zeros_like(loss_ref)

        loss_ref[...] += part
        dy = err * inv_d
        _acc_rows(dgf_ref, dy * x3h, first)
        dx3h = dy * gf
        dx3 = r3 * (dx3h - x3h * jnp.mean(dx3h * x3h, axis=-1, keepdims=True))
        dx3_16 = dx3.astype(BF16)
        dx3_ref[...] = dx3_16
        da = _dot_nt(dx3_16, wd_ref[...])
        dup16 = (da * sl).astype(BF16)
        dgate16 = (da * up * (sg * (1.0 + gate * (1.0 - sg)))).astype(BF16)
        dup_ref[...] = dup16
        dgate_ref[...] = dgate16
        dh2 = _dot_nt(dgate16, wg_ref[...]) + _dot_nt(dup16, wu_ref[...])
        dx2n, dg2 = _rms_bwd(dh2, x2v, g2_ref[...])
        _acc_rows(dg2_ref, dg2, first)
        dx2 = dx3 + dx2n
        dx2_ref[...] = dx2
        dx2h_ref[...] = dx2.astype(BF16)

    return pl.pallas_call(
        body, name="ffn_fwd_bwd", grid=(t // tm,),
        in_specs=[_rows(tm, D_MODEL), _rows(tm, D_MODEL), _rows(tm, D_MODEL), _full((D_MODEL, D_FF), True),
                  _full((D_MODEL, D_FF), True), _full((D_FF, D_MODEL), True), _full((1, D_MODEL)), _full((1, D_MODEL))],
        out_specs=[_rows(tm, D_FF), _rows(tm, D_FF), _rows(tm, D_FF), _rows(tm, D_MODEL), _rows(tm, D_MODEL),
                   _rows(tm, D_MODEL), _full((8, 128)), _full((1, D_MODEL)), _full((1, D_MODEL))],
        out_shape=[jax.ShapeDtypeStruct((t, D_FF), BF16)] * 3
        + [jax.ShapeDtypeStruct((t, D_MODEL), BF16), jax.ShapeDtypeStruct((t, D_MODEL), F32),
           jax.ShapeDtypeStruct((t, D_MODEL), BF16), jax.ShapeDtypeStruct((8, 128), F32),
           jax.ShapeDtypeStruct((1, D_MODEL), F32), jax.ShapeDtypeStruct((1, D_MODEL), F32)],
        compiler_params=_params(1),
    )(h2, x2, target, w_gate, w_up, w_down, g_f, g2)


def _wgrad(a, b, name):
    t, m = a.shape
    n = b.shape[1]
    bm = _pick(m, 1408)
    bn = _pick(n, 1664)
    tk = _pick(t, 1024, 16)
    nk = t // tk

    def body(a_ref, b_ref, o_ref):
        part = _dot_tn(a_ref[...], b_ref[...])

        @pl.when(pl.program_id(2) == 0)
        def _():
            o_ref[...] = part

        @pl.when(pl.program_id(2) != 0)
        def _():
            o_ref[...] += part

    return pl.pallas_call(
        body, name=name, grid=(m // bm, n // bn, nk),
        in_specs=[pl.BlockSpec((tk, bm), lambda i, j, k: (k, i)), pl.BlockSpec((tk, bn), lambda i, j, k: (k, j))],
        out_specs=pl.BlockSpec((bm, bn), lambda i, j, k: (i, j)),
        out_shape=jax.ShapeDtypeStruct((m, n), F32),
        compiler_params=_params(3),
    )(a, b)


def _rope_tables(seq):
    inv = 1.0 / (ROPE_THETA ** (jnp.arange(0, B_ROPE, 2, dtype=F32) / B_ROPE))
    ang = jnp.arange(seq, dtype=F32)[:, None] * inv[None, :]
    cos, sin = jnp.cos(ang), jnp.sin(ang)
    zeros = jnp.zeros((seq, 64), F32)
    return jnp.concatenate([cos, cos, zeros], axis=1), jnp.concatenate([-sin, sin, zeros], axis=1)


def _pad_weights(w_in, w_q_b, w_kv_b):
    w_in_p = jnp.pad(w_in, ((0, 0), (0, D_IN_PAD - D_IN)))
    w_q_p = jnp.pad(w_q_b.reshape(Q_LORA, B_HEADS, B_NOPE + B_ROPE), ((0, 0), (0, 0), (0, 64))).reshape(Q_LORA, 1024)
    kv = w_kv_b.reshape(KV_LORA, B_HEADS, B_NOPE + B_V)
    w_kv_p = jnp.concatenate([kv[:, :, :B_NOPE].reshape(KV_LORA, 512), kv[:, :, B_NOPE:].reshape(KV_LORA, 512)], axis=1)
    return w_in_p, w_q_p, w_kv_p


def _unpad_grads(g_in_p, g_q_p, g_kv_p):
    g_in = g_in_p[:, :D_IN]
    g_q = g_q_p.reshape(Q_LORA, B_HEADS, QK_PAD)[:, :, :B_NOPE + B_ROPE].reshape(Q_LORA, B_HEADS * (B_NOPE + B_ROPE))
    g_kv = jnp.concatenate([g_kv_p[:, :512].reshape(KV_LORA, B_HEADS, B_NOPE),
                            g_kv_p[:, 512:].reshape(KV_LORA, B_HEADS, B_V)], axis=2).reshape(KV_LORA, 1024)
    return g_in, g_q, g_kv


def _local_step(x, target, lbl, g1, g_hn, g_qa, g_kva, g_mla, g2, g_f, w_in, w_q_b, w_kv_b, late_shards):
    n_batch, seq, _ = x.shape
    t = n_batch * seq
    x = x.reshape(t, D_MODEL)
    target = target.reshape(t, D_MODEL)
    w_in_p, w_q_p, w_kv_p = _pad_weights(w_in, w_q_b, w_kv_b)
    cosx, sinx = _rope_tables(seq)

    h1, hq, hi, hff, hfb, hg, cq, ckv, kr = _in_fwd(x, g1, w_in_p)
    o_f, st_f = _hgrn_fwd(hq, hi, hff, lbl, n_batch=n_batch, direction=0)
    o_r, st_r = _hgrn_fwd(hq, hi, hfb, lbl, n_batch=n_batch, direction=1)
    q, k, v, cqn, ckvn = _mla_prep(cq, ckv, kr, g_qa, g_kva, w_q_p, w_kv_p, cosx, sinx, n_batch=n_batch)
    o_attn, lse, *gathered = _attn_fwd(q, k, v, late_shards, n_batch=n_batch)
    w_out, w_gate, w_up, w_down = (_join_shards(n, a) for n, a in zip(_LATE, gathered))
    ycat, x2, h2 = _out_fwd(o_f, o_r, hg, o_attn, x, g_hn, g_mla, w_out, g2)
    act, dgate, dup, dx3_16, dx2, dx2_16, loss, dg_f, dg2 = _ffn_fwd_bwd(h2, x2, target, w_gate, w_up, w_down, g_f, g2)
    g_late = dict(w_out=_wgrad(ycat, dx2_16, "wgrad_out"), w_gate=_wgrad(h2, dgate, "wgrad_gate"),
                  w_up=_wgrad(h2, dup, "wgrad_up"), w_down=_wgrad(act, dx3_16, "wgrad_down"))
    do_h, dhg, do_attn16, delta, dg_hn, dg_mla = _out_bwd(dx2_16, o_f, o_r, hg, o_attn, g_hn, g_mla, w_out)
    dqt, dk, dv, *recv_late = _attn_bwd(q, k, v, do_attn16, lse, delta,
                                        [_segments(n, g_late[n]).astype(BF16) for n in _LATE], n_batch=n_batch)
    dcq, dckv, dkr, dqp16, dkvp16, dg_qa, dg_kva = _mla_prep_bwd(dqt, dk, dv, cq, ckv, g_qa, g_kva, w_q_p, w_kv_p,
                                                                  cosx, sinx, n_batch=n_batch)
    dq_f, dv_f, dz_f, dl_f = _hgrn_bwd(hq, hi, hff, do_h, st_f, lbl, n_batch=n_batch, direction=0)
    dq_r, dv_r, dz_r, dl_r = _hgrn_bwd(hq, hi, hfb, do_h, st_r, lbl, n_batch=n_batch, direction=1)
    dx, dproj16, dg1 = _in_bwd(dq_f, dq_r, dv_f, dv_r, dz_f, dz_r, dhg, hq, dcq, dckv, dkr, dx2, x, g1, w_in_p)

    gw_in, gw_q, gw_kv = _unpad_grads(_wgrad(h1, dproj16, "wgrad_in"), _wgrad(cqn, dqp16, "wgrad_q_b"),
                                      _wgrad(ckvn, dkvp16, "wgrad_kv_b"))
    grads = dict(
        norm1_g=dg1, w_in=gw_in, lb_logits=jnp.stack([dl_f, dl_r]), hgrn_norm_g=dg_hn, q_a_norm_g=dg_qa,
        w_q_b=gw_q, kv_a_norm_g=dg_kva, w_kv_b=gw_kv, mla_norm_g=dg_mla, norm2_g=dg2, final_norm_g=dg_f)
    grads.update(g_late)
    return loss[0, 0], dx.reshape(n_batch, seq, D_MODEL), grads, dict(zip(_LATE, recv_late))


_HBM = pl.BlockSpec(memory_space=pltpu.HBM)
_MESH = pl.DeviceIdType.MESH


def _place():
    x, y, c = lax.axis_index("x"), lax.axis_index("y"), lax.axis_index("c")
    other_chips = [(1 - x, y), (x, 1 - y), (1 - x, 1 - y)]
    return x, y, c, other_chips


def _gather_copies(w_ref, wall_ref, send_sems, recv_sems, local_sem, base=0):
    x, y, c, chips = _place()
    mine = 2 * x + y

    def mk(j, chip_index, to):
        return pltpu.make_async_remote_copy(src_ref=w_ref, dst_ref=wall_ref.at[chip_index], send_sem=send_sems.at[base + j],
                                            recv_sem=recv_sems.at[base + j], device_id=to, device_id_type=_MESH)

    local = pltpu.make_async_copy(w_ref, wall_ref.at[mine], local_sem)
    sends = [mk(j, mine, (*chip, c)) for j, chip in enumerate(chips)]
    recvs = [mk(j, 2 * px + py, (x, y, c)) for j, (px, py) in enumerate(chips)]
    return local, sends, recvs


def _gather_start(*refs):
    local, sends, _ = _gather_copies(*refs)
    local.start()
    for cp in sends:
        cp.start()


def _gather_wait(*refs):
    local, sends, recvs = _gather_copies(*refs)
    for cp in recvs:
        cp.wait_recv()
    for cp in sends:
        cp.wait_send()
    local.wait()


def _scatter_copies(g_ref, recv_ref, send_sems, recv_sems, base=0):
    x, y, c, chips = _place()

    def mk(j, src_index, to):
        return pltpu.make_async_remote_copy(src_ref=g_ref.at[src_index], dst_ref=recv_ref.at[j],
                                            send_sem=send_sems.at[base + j], recv_sem=recv_sems.at[base + j],
                                            device_id=to, device_id_type=_MESH)

    sends = [mk(j, 2 * px + py, (px, py, c)) for j, (px, py) in enumerate(chips)]
    recvs = [mk(j, 0, (x, y, c)) for j in range(3)]
    return sends, recvs


def _scatter_start(*refs):
    for cp in _scatter_copies(*refs)[0]:
        cp.start()


def _scatter_wait(*refs):
    sends, recvs = _scatter_copies(*refs)
    for cp in recvs:
        cp.wait_recv()
    for cp in sends:
        cp.wait_send()


def _gather_shards(srcs):
    n = len(srcs)

    def body(*refs):
        src_refs, dst_refs = refs[:n], refs[n:2 * n]
        send_sems, recv_sems, local_sems = refs[2 * n:]
        for i in range(n):
            _gather_start(src_refs[i], dst_refs[i], send_sems, recv_sems, local_sems.at[i], 3 * i)
        for i in range(n):
            _gather_wait(src_refs[i], dst_refs[i], send_sems, recv_sems, local_sems.at[i], 3 * i)

    return pl.pallas_call(
        body, name="gather_shards", in_specs=[_HBM] * n, out_specs=[_HBM] * n,
        out_shape=[jax.ShapeDtypeStruct((N_CHIPS,) + a.shape, a.dtype) for a in srcs],
        scratch_shapes=[pltpu.SemaphoreType.DMA((3 * n,)), pltpu.SemaphoreType.DMA((3 * n,)), pltpu.SemaphoreType.DMA((n,))],
    )(*srcs)


def _scatter_grads(gsegs, small):
    n = len(gsegs)

    def body(*refs):
        g_refs, s_ref, recv_refs, sall_ref = refs[:n], refs[n], refs[n + 1:2 * n + 1], refs[2 * n + 1]
        send_sems, recv_sems, local_sem = refs[2 * n + 2:]
        x, y, c, _ = _place()
        me = 4 * x + 2 * y + c
        flips = [(fx, fy, fc) for fx in (0, 1) for fy in (0, 1) for fc in (0, 1)][1:]

        def peer(f):
            return tuple((1 - a) if b else a for a, b in zip((x, y, c), f))

        def sm(r, index, to):
            return pltpu.make_async_remote_copy(src_ref=s_ref, dst_ref=sall_ref.at[index], send_sem=send_sems.at[3 * n + r],
                                                recv_sem=recv_sems.at[3 * n + r], device_id=to, device_id_type=_MESH)

        local = pltpu.make_async_copy(s_ref, sall_ref.at[me], local_sem)
        local.start()
        for i in range(n):
            _scatter_start(g_refs[i], recv_refs[i], send_sems, recv_sems, 3 * i)
        sends = [sm(r, me, peer(f)) for r, f in enumerate(flips)]
        for cp in sends:
            cp.start()
        for i in range(n):
            _scatter_wait(g_refs[i], recv_refs[i], send_sems, recv_sems, 3 * i)
        for r, f in enumerate(flips):
            px, py, pc = peer(f)
            sm(r, 4 * px + 2 * py + pc, (x, y, c)).wait_recv()
        for cp in sends:
            cp.wait_send()
        local.wait()

    return pl.pallas_call(
        body, name="scatter_grads", in_specs=[_HBM] * (n + 1), out_specs=[_HBM] * (n + 1),
        out_shape=[jax.ShapeDtypeStruct((3,) + g.shape[1:], g.dtype) for g in gsegs]
        + [jax.ShapeDtypeStruct((8,) + small.shape, F32)],
        scratch_shapes=[pltpu.SemaphoreType.DMA((3 * n + 7,)), pltpu.SemaphoreType.DMA((3 * n + 7,)),
                        pltpu.SemaphoreType.DMA(())],
    )(*gsegs, small)


def _swap_sibling(parts):
    n = len(parts)

    def body(*refs):
        send_sems, recv_sems = refs[2 * n:]
        x, y, c, _ = _place()
        cps = [pltpu.make_async_remote_copy(src_ref=refs[i], dst_ref=refs[n + i], send_sem=send_sems.at[i],
                                            recv_sem=recv_sems.at[i], device_id=(x, y, 1 - c), device_id_type=_MESH)
               for i in range(n)]
        for cp in cps:
            cp.start()
        for cp in cps:
            cp.wait()

    return pl.pallas_call(
        body, name="swap_sibling", in_specs=[_HBM] * n, out_specs=[_HBM] * n,
        out_shape=[jax.ShapeDtypeStruct(p.shape, p.dtype) for p in parts],
        scratch_shapes=[pltpu.SemaphoreType.DMA((n,)), pltpu.SemaphoreType.DMA((n,))],
    )(*parts)


def _sum_segments(own, recv, name):
    rows, cols = own.shape
    tm = _pick(rows, 256, 16)

    def body(o_ref, r_ref, out_ref):
        acc = o_ref[...]
        for j in range(3):
            acc = acc + r_ref[j].astype(F32)
        out_ref[...] = acc

    return pl.pallas_call(
        body, name=name, grid=(rows // tm,),
        in_specs=[_rows(tm, cols), pl.BlockSpec((3, tm, cols), lambda i: (0, i, 0))],
        out_specs=_rows(tm, cols), out_shape=jax.ShapeDtypeStruct(own.shape, F32), compiler_params=_params(1),
    )(own, recv)


def _sum_devices(sall):
    def body(s_ref, o_ref):
        acc = s_ref[0]
        for d in range(1, 8):
            acc = acc + s_ref[d]
        o_ref[...] = acc

    return pl.pallas_call(body, name="sum_devices", out_shape=jax.ShapeDtypeStruct(sall.shape[1:], F32))(sall)


def _adamw(w, m, v, ga, gb, name):
    rows, cols = w.shape
    tm = _row_tile(rows)
    two = gb is not None

    def body(*refs):
        w_ref, m_ref, v_ref, ga_ref = refs[:4]
        g_ref, d_ref, m2_ref, v2_ref = refs[-4:]
        g = ga_ref[...] + refs[4][...] if two else ga_ref[...]
        m2 = ADAM_B1 * m_ref[...] + (1.0 - ADAM_B1) * g
        v2 = ADAM_B2 * v_ref[...] + (1.0 - ADAM_B2) * (g * g)
        m_hat = m2 / (1.0 - ADAM_B1 ** ADAM_STEP)
        v_hat = v2 / (1.0 - ADAM_B2 ** ADAM_STEP)
        g_ref[...] = g
        d_ref[...] = -ADAM_LR * (m_hat / (jnp.sqrt(v_hat) + ADAM_EPS) + ADAM_WD * w_ref[...])
        m2_ref[...] = m2
        v2_ref[...] = v2

    blk = _rows(tm, cols)
    args = (w, m, v, ga) + ((gb,) if two else ())
    return pl.pallas_call(
        body, name=name, grid=(rows // tm,), in_specs=[blk] * len(args), out_specs=[blk] * 4,
        out_shape=[jax.ShapeDtypeStruct(w.shape, F32)] * 4, compiler_params=_params(1),
    )(*args)


_COLUMN_SHARDED = ("w_in", "w_q_b", "w_kv_b", "w_gate", "w_up")
_FULL_SHAPES = dict(w_in=(D_MODEL, D_IN), w_q_b=(Q_LORA, 768), w_kv_b=(KV_LORA, 1024), w_out=(D_MODEL, D_MODEL),
                    w_gate=(D_MODEL, D_FF), w_up=(D_MODEL, D_FF), w_down=(D_FF, D_MODEL))
_SMALL = (("norm1_g", 1024), ("lb_logits", 2048), ("hgrn_norm_g", 512), ("q_a_norm_g", 384), ("kv_a_norm_g", 256),
          ("mla_norm_g", 512), ("norm2_g", 1024), ("final_norm_g", 1024))
_UPDATE_ROWS = 48


def _pad_rows(a, rows):
    return jnp.pad(a, ((0, rows - a.shape[0]), (0, 0)))


_EARLY = ("w_in", "w_q_b", "w_kv_b")
_LATE = ("w_out", "w_gate", "w_up", "w_down")


def _segments(name, g):
    r, c = g.shape
    if name in _COLUMN_SHARDED:
        return g.reshape(r, N_CHIPS, c // N_CHIPS).transpose(1, 0, 2)
    return g.reshape(N_CHIPS, r // N_CHIPS, c)


def _own_segment(name, g, chip):
    r, c = g.shape
    if name in _COLUMN_SHARDED:
        return lax.dynamic_slice_in_dim(g, chip * (c // N_CHIPS), c // N_CHIPS, axis=1)
    return lax.dynamic_slice_in_dim(g, chip * (r // N_CHIPS), r // N_CHIPS, axis=0)


def _join_shards(name, seg):
    r, c = _FULL_SHAPES[name]
    if name in _COLUMN_SHARDED:
        return seg.transpose(1, 0, 2).reshape(r, c)
    return seg.reshape(r, c)


def kernel(x, norm1_g, w_in, lb_logits, hgrn_norm_g, q_a_norm_g, w_q_b, kv_a_norm_g, w_kv_b, mla_norm_g, w_out, norm2_g, w_gate, w_up, w_down, final_norm_g, loss_target, m_norm1_g, m_w_in, m_lb_logits, m_hgrn_norm_g, m_q_a_norm_g, m_w_q_b, m_kv_a_norm_g, m_w_kv_b, m_mla_norm_g, m_w_out, m_norm2_g, m_w_gate, m_w_up, m_w_down, m_final_norm_g, v_norm1_g, v_w_in, v_lb_logits, v_hgrn_norm_g, v_q_a_norm_g, v_w_q_b, v_kv_a_norm_g, v_w_kv_b, v_mla_norm_g, v_w_out, v_norm2_g, v_w_gate, v_w_up, v_w_down, v_final_norm_g):
    names = ("norm1_g", "w_in", "lb_logits", "hgrn_norm_g", "q_a_norm_g", "w_q_b", "kv_a_norm_g", "w_kv_b", "mla_norm_g",
             "w_out", "norm2_g", "w_gate", "w_up", "w_down", "final_norm_g")
    w = dict(zip(names, (norm1_g, w_in, lb_logits, hgrn_norm_g, q_a_norm_g, w_q_b, kv_a_norm_g, w_kv_b, mla_norm_g,
                         w_out, norm2_g, w_gate, w_up, w_down, final_norm_g)))
    m = dict(zip(names, (m_norm1_g, m_w_in, m_lb_logits, m_hgrn_norm_g, m_q_a_norm_g, m_w_q_b, m_kv_a_norm_g, m_w_kv_b,
                         m_mla_norm_g, m_w_out, m_norm2_g, m_w_gate, m_w_up, m_w_down, m_final_norm_g)))
    v = dict(zip(names, (v_norm1_g, v_w_in, v_lb_logits, v_hgrn_norm_g, v_q_a_norm_g, v_w_q_b, v_kv_a_norm_g, v_w_kv_b,
                         v_mla_norm_g, v_w_out, v_norm2_g, v_w_gate, v_w_up, v_w_down, v_final_norm_g)))
    matrices = _EARLY + _LATE
    chip = 2 * lax.axis_index("x") + lax.axis_index("y")

    w16 = {n: w[n][0].astype(BF16) for n in matrices}
    lb8 = _pad_rows(lb_logits.reshape(4, 128), 8)
    *early, lball = _gather_shards([w16[n] for n in _EARLY] + [lb8])
    lbl = lball[:, :4].reshape(N_CHIPS, 2, 2, 128).transpose(1, 2, 0, 3).reshape(2, 2, A_WIDTH)

    loss_part, grad_x, g, recv = _local_step(
        x, loss_target, lbl, norm1_g, hgrn_norm_g, q_a_norm_g, kv_a_norm_g, mla_norm_g, norm2_g, final_norm_g[None, :],
        *(_join_shards(n, a) for n, a in zip(_EARLY, early)), [w16[n] for n in _LATE])
    loss = lax.psum(loss_part, ("x", "y", "c"))

    small = _pad_rows(jnp.concatenate([g[n].reshape(-1) for n, _ in _SMALL]).reshape(-1, 128), SMALL_ROWS)
    *recv_early, small_all = _scatter_grads([_segments(n, g[n]).astype(BF16) for n in _EARLY], small)
    recv.update(zip(_EARLY, recv_early))
    parts = [_sum_segments(_own_segment(n, g[n], chip), recv[n], f"sum_{n}") for n in matrices]
    sibs = _swap_sibling(parts)
    small_sum = _sum_devices(small_all).reshape(-1)

    out = {}
    for n, part, sib in zip(matrices, parts, sibs):
        res = _adamw(w[n][0], m[n][0], v[n][0], part, sib, f"adamw_{n}")
        out[n] = tuple(r[None] for r in res)

    small_g, off = {}, 0
    for n, size in _SMALL:
        small_g[n] = small_sum[off:off + size]
        off += size
    small_g["lb_logits"] = lax.dynamic_slice_in_dim(small_g["lb_logits"].reshape(2, 2, A_WIDTH), chip * 128, 128, axis=2)
    small_names = tuple(n for n, _ in _SMALL)

    def pack(d):
        return _pad_rows(jnp.concatenate([d[n].reshape(-1) for n in small_names]).reshape(-1, 128), _UPDATE_ROWS)

    res = _adamw(pack(w), pack(m), pack(v), pack(small_g), None, "adamw_small")
    off = 0
    flat = [r.reshape(-1) for r in res]
    for n in small_names:
        size = w[n].size
        out[n] = tuple(f[off:off + size].reshape(w[n].shape) for f in flat)
        off += size

    return (loss, grad_x) + tuple(out[n][i] for i in range(4) for n in names)
```

```python
import functools

import jax
import jax.numpy as jnp
from jax import lax
from jax.experimental import pallas as pl
from jax.experimental.pallas import tpu as pltpu

F32 = jnp.float32
BF16 = jnp.bfloat16

D_MODEL = 1024
A_WIDTH = 512
HEAD_PAIRS = 4
CHUNK = 64
B_HEADS = 4
B_NOPE = 128
B_ROPE = 64
B_V = 128
QK_PAD = 256
Q_LORA = 384
KV_LORA = 256
D_FF = 2816
D_IN = 3264
D_IN_PAD = 3328
IN_WIDTHS = (512, 512, 512, 512, 512, Q_LORA, KV_LORA, 128)
ROPE_THETA = 10000.0
EPS = 1e-6
ATTN_SCALE = (B_NOPE + B_ROPE) ** -0.5
LOG2E = 1.4426950408889634
SCALE_LOG2E = ATTN_SCALE * LOG2E

ADAM_LR = 0.001
ADAM_B1 = 0.9
ADAM_B2 = 0.999
ADAM_EPS = 1e-08
ADAM_WD = 0.01
ADAM_STEP = 10

VMEM_LIMIT_BYTES = 60 * 1024 * 1024
N_CHIPS = 4
SMALL_ROWS = 56


def _params(n_axes):
    return pltpu.CompilerParams(dimension_semantics=("arbitrary",) * n_axes,
                                vmem_limit_bytes=VMEM_LIMIT_BYTES)


def _dot(a, b):
    return jnp.dot(a, b, preferred_element_type=F32)


def _dot_nt(a, b):
    return lax.dot_general(a, b, (((1,), (1,)), ((), ())), preferred_element_type=F32)


def _dot_tn(a, b):
    return lax.dot_general(a, b, (((0,), (0,)), ((), ())), preferred_element_type=F32)


def _split3(x):
    x1 = x.astype(BF16)
    r = x - x1.astype(F32)
    x2 = r.astype(BF16)
    x3 = (r - x2.astype(F32)).astype(BF16)
    return x1, x2, x3


def _exact_left(m16, x):
    x1, x2, x3 = _split3(x)
    return _dot(m16, x1) + _dot(m16, x2) + _dot(m16, x3)


def _exact_right(x, m16):
    x1, x2, x3 = _split3(x)
    return _dot(x1, m16) + _dot(x2, m16) + _dot(x3, m16)


def _iota2(shape, dim):
    return lax.broadcasted_iota(jnp.int32, shape, dim)


def _sigmoid(x):
    return jax.nn.sigmoid(x)


def _pick(dim, cap, mult=128):
    if dim <= cap:
        return dim
    best = None
    for d in range(mult, cap + 1, mult):
        if dim % d == 0:
            best = d
    assert best is not None, (dim, cap, mult)
    return best


def _row_tile(t, cap=256):
    return _pick(t, cap, 8)


def _full(shape, single=False):
    if single:
        return pl.BlockSpec(shape, lambda *_: (0,) * len(shape), pipeline_mode=pl.Buffered(1))
    return pl.BlockSpec(shape, lambda *_: (0,) * len(shape))


def _rows(tm, width):
    return pl.BlockSpec((tm, width), lambda i: (i, 0))


def _acc_rows(ref, val, first):
    s = jnp.sum(val, axis=0, keepdims=True)

    @pl.when(first)
    def _():
        ref[...] = s

    @pl.when(jnp.logical_not(first))
    def _():
        ref[...] += s


def _in_fwd(x, g1, w_in_p):
    t = x.shape[0]
    tm = _row_tile(t)

    def body(x_ref, g_ref, w_ref, h_ref, *outs):
        xv = x_ref[...]
        r = lax.rsqrt(jnp.mean(xv * xv, axis=-1, keepdims=True) + EPS)
        h = ((xv * r) * g_ref[...]).astype(BF16)
        h_ref[...] = h
        off = 0
        for o_ref, w in zip(outs, IN_WIDTHS):
            o_ref[...] = _dot(h, w_ref[:, off:off + w])
            off += w

    return pl.pallas_call(
        body, name="in_fwd", grid=(t // tm,),
        in_specs=[_rows(tm, D_MODEL), _full((1, D_MODEL)), _full((D_MODEL, D_IN_PAD))],
        out_specs=[_rows(tm, D_MODEL)] + [_rows(tm, w) for w in IN_WIDTHS],
        out_shape=[jax.ShapeDtypeStruct((t, D_MODEL), BF16)]
        + [jax.ShapeDtypeStruct((t, w), F32) for w in IN_WIDTHS],
        compiler_params=_params(1),
    )(x, g1, w_in_p)


def _in_bwd(dq_f, dq_r, dv_f, dv_r, dz_f, dz_r, dhg, hq, dcq, dckv, dkr, dx2, x, g1, w_in_p):
    t = x.shape[0]
    tm = _row_tile(t)

    def body(dqf_ref, dqr_ref, dvf_ref, dvr_ref, dzf_ref, dzr_ref, dhg_ref, hq_ref, dcq_ref, dckv_ref,
             dkr_ref, dx2_ref, x_ref, g_ref, w_ref, dx_ref, dp_ref, dg_ref):
        hqv = hq_ref[...]
        sg = _sigmoid(hqv)
        dhq = (dqf_ref[...] + dqr_ref[...]) * (sg * (1.0 + hqv * (1.0 - sg)))
        pieces = (dhq, dvf_ref[...] + dvr_ref[...], dzf_ref[...], dzr_ref[...], dhg_ref[...],
                  dcq_ref[...], dckv_ref[...], dkr_ref[...])
        dh = None
        off = 0
        for p, w in zip(pieces, IN_WIDTHS):
            p16 = p.astype(BF16)
            dp_ref[:, off:off + w] = p16
            part = _dot_nt(p16, w_ref[:, off:off + w])
            dh = part if dh is None else dh + part
            off += w
        xv = x_ref[...]
        r = lax.rsqrt(jnp.mean(xv * xv, axis=-1, keepdims=True) + EPS)
        xh = xv * r
        _acc_rows(dg_ref, dh * xh, pl.program_id(0) == 0)
        dxh = dh * g_ref[...]
        dx_ref[...] = dx2_ref[...] + r * (dxh - xh * jnp.mean(dxh * xh, axis=-1, keepdims=True))

    a512 = _rows(tm, A_WIDTH)
    return pl.pallas_call(
        body, name="in_bwd", grid=(t // tm,),
        in_specs=[a512] * 8 + [_rows(tm, Q_LORA), _rows(tm, KV_LORA), _rows(tm, 128), _rows(tm, D_MODEL),
                               _rows(tm, D_MODEL), _full((1, D_MODEL)), _full((D_MODEL, D_IN_PAD))],
        out_specs=[_rows(tm, D_MODEL), _rows(tm, D_IN_PAD), _full((1, D_MODEL))],
        out_shape=[jax.ShapeDtypeStruct((t, D_MODEL), F32), jax.ShapeDtypeStruct((t, D_IN_PAD), BF16),
                   jax.ShapeDtypeStruct((1, D_MODEL), F32)],
        compiler_params=_params(1),
    )(dq_f, dq_r, dv_f, dv_r, dz_f, dz_r, dhg, hq, dcq, dckv, dkr, dx2, x, g1, w_in_p)


def _lower_bound(lbl_ref, direction):
    l0 = lbl_ref[direction, 0:1, :]
    l1 = lbl_ref[direction, 1:2, :]
    m = jnp.maximum(l0, l1)
    e0 = jnp.exp(l0 - m)
    e1 = jnp.exp(l1 - m)
    return e0 / (e0 + e1)


def _hgrn_consts(rb, reverse):
    row = _iota2((rb, rb), 0)
    col = _iota2((rb, rb), 1)
    same = (row // CHUNK) == (col // CHUNK)
    tri = jnp.logical_and(same, (col >= row) if reverse else (col <= row))
    tri_t = jnp.logical_and(same, (col <= row) if reverse else (col >= row))
    r128 = _iota2((128, 128), 0)
    c128 = _iota2((128, 128), 1)
    bd = (r128 < 64) == (c128 < 64)
    lane = _iota2((1, 128), 1)
    m0 = (lane < 64).astype(F32)
    return tri, tri_t, bd, (m0, 1.0 - m0)


def _per_chunk(x, fn):
    n = x.shape[0] // CHUNK
    return jnp.concatenate([jnp.broadcast_to(fn(x[c * CHUNK:(c + 1) * CHUNK]), (CHUNK, x.shape[1])) for c in range(n)],
                           axis=0)


def _chunk_cumsum(x, reverse):
    rb = x.shape[0]
    pos = _iota2(x.shape, 0) % CHUNK
    step = 1
    while step < CHUNK:
        if reverse:
            x = x + jnp.where(pos < CHUNK - step, pltpu.roll(x, rb - step, 0), 0.0)
        else:
            x = x + jnp.where(pos >= step, pltpu.roll(x, step, 0), 0.0)
        step *= 2
    return x


def _hgrn_block(z, hqv, lb, reverse):
    sig = _sigmoid(z)
    sn = _sigmoid(-z)
    q = hqv * _sigmoid(hqv)
    f = lb + (1.0 - lb) * sig
    k = (1.0 - lb) * sn
    lf = jnp.log(f)
    cum = _chunk_cumsum(lf, reverse)
    last = _per_chunk(cum, (lambda a: a[0:1]) if reverse else (lambda a: a[CHUNK - 1:CHUNK]))
    e_neg = jnp.exp(-cum)
    e_end = jnp.exp(last - cum)
    a = jnp.exp(cum)
    return dict(sig=sig, sn=sn, q=q, f=f, k=k, a=a, e_neg=e_neg, e_end=e_end,
                q_dec=q * a, k_inv=k * e_neg, k_end=k * e_end, d=jnp.exp(last))


def _hgrn_dims(t, n_batch):
    s = t // n_batch
    rb = _pick(s, 256, CHUNK)
    return s, rb, s // rb, rb // CHUNK


def _hgrn_fwd(hq, hi, hf, lbl, *, n_batch, direction):
    t = hq.shape[0]
    reverse = direction == 1
    s, rb, nb, nc = _hgrn_dims(t, n_batch)

    def tmap(b, j):
        return (b * nb + ((nb - 1 - j) if reverse else j), 0)

    def smap(b, j):
        return (b * nb + ((nb - 1 - j) if reverse else j), 0, 0, 0)

    def body(hq_ref, hi_ref, hf_ref, lbl_ref, o_ref, st_ref, st_scr):
        @pl.when(pl.program_id(1) == 0)
        def _():
            st_scr[...] = jnp.zeros_like(st_scr)

        lb_all = _lower_bound(lbl_ref, direction)
        tri, _, bd, masks = _hgrn_consts(rb, reverse)
        order = range(nc - 1, -1, -1) if reverse else range(nc)

        pairs = [slice(p * 128, (p + 1) * 128) for p in range(HEAD_PAIRS)]
        chunks = [slice(c * CHUNK, (c + 1) * CHUNK) for c in range(nc)]
        w = _hgrn_block(hf_ref[...], hq_ref[...], lb_all, reverse)
        v16 = hi_ref[...].astype(BF16)
        qd16 = w["q_dec"].astype(BF16)
        ki16 = w["k_inv"].astype(BF16)
        ke16 = w["k_end"].astype(BF16)
        sc = [[_dot_nt((w["q_dec"][:, ls] * mh).astype(BF16), ki16[:, ls]) for mh in masks] for ls in pairs]
        pv = [[_dot(jnp.where(tri, s_e, 0.0).astype(BF16), v16[:, ls]) for s_e in sc_p] for sc_p, ls in zip(sc, pairs)]
        o_intra = [pv_p[0] * masks[0] + pv_p[1] * masks[1] for pv_p in pv]
        ut = [[jnp.where(bd, _dot_tn(v16[rs, ls], ke16[rs, ls]), 0.0) for ls in pairs] for rs in chunks]
        st = [st_scr[p] for p in range(HEAD_PAIRS)]
        for c in order:
            rs = chunks[c]
            inter = [_dot_nt(qd16[rs, ls], st[p].astype(BF16)) for p, ls in enumerate(pairs)]
            for p, ls in enumerate(pairs):
                o_ref[rs, ls] = o_intra[p][rs] + inter[p]
                st_ref[c, p] = st[p]
                st[p] = st[p] * w["d"][c * CHUNK:c * CHUNK + 1, ls] + ut[c][p]
        for p in range(HEAD_PAIRS):
            st_scr[p] = st[p]

    blk = pl.BlockSpec((rb, A_WIDTH), tmap)
    return pl.pallas_call(
        body, name=f"hgrn_fwd_{direction}", grid=(n_batch, nb),
        in_specs=[blk, blk, blk, _full((2, 2, A_WIDTH))],
        out_specs=[blk, pl.BlockSpec((nc, HEAD_PAIRS, 128, 128), smap)],
        out_shape=[jax.ShapeDtypeStruct((t, A_WIDTH), F32),
                   jax.ShapeDtypeStruct((t // CHUNK, HEAD_PAIRS, 128, 128), F32)],
        scratch_shapes=[pltpu.VMEM((HEAD_PAIRS, 128, 128), F32)],
        compiler_params=_params(2),
    )(hq, hi, hf, lbl)


def _hgrn_bwd(hq, hi, hf, do, st, lbl, *, n_batch, direction):
    t = hq.shape[0]
    reverse = direction == 1
    s, rb, nb, nc = _hgrn_dims(t, n_batch)

    def tmap(b, j):
        return (b * nb + (j if reverse else (nb - 1 - j)), 0)

    def smap(b, j):
        return (b * nb + (j if reverse else (nb - 1 - j)), 0, 0, 0)

    def body(hq_ref, hi_ref, hf_ref, do_ref, st_ref, lbl_ref, dq_ref, dv_ref, dz_ref, dl_ref, g_scr, dlb_scr):
        b = pl.program_id(0)
        j = pl.program_id(1)

        @pl.when(jnp.logical_and(b == 0, j == 0))
        def _():
            dlb_scr[...] = jnp.zeros_like(dlb_scr)

        @pl.when(j == 0)
        def _():
            g_scr[...] = jnp.zeros_like(g_scr)

        lb_all = _lower_bound(lbl_ref, direction)
        tri, tri_t, bd, masks = _hgrn_consts(rb, reverse)
        order = range(nc) if reverse else range(nc - 1, -1, -1)

        pairs = [slice(p * 128, (p + 1) * 128) for p in range(HEAD_PAIRS)]
        chunks = [slice(c * CHUNK, (c + 1) * CHUNK) for c in range(nc)]

        def lanes(per_pair):
            return jnp.concatenate(per_pair, axis=1)

        w = _hgrn_block(hf_ref[...], hq_ref[...], lb_all, reverse)
        dov = do_ref[...]
        v16 = hi_ref[...].astype(BF16)
        do16 = dov.astype(BF16)
        qd16 = w["q_dec"].astype(BF16)
        ki16 = w["k_inv"].astype(BF16)
        ke16 = w["k_end"].astype(BF16)
        qm16 = [[(w["q_dec"][:, ls] * mh).astype(BF16) for mh in masks] for ls in pairs]
        dom16 = [[(dov[:, ls] * mh).astype(BF16) for mh in masks] for ls in pairs]
        heads = [(p, e) for p in range(HEAD_PAIRS) for e in range(2)]
        dp = {(p, e): _dot_nt(dom16[p][e], v16[:, pairs[p]]) for p, e in heads}
        pm_t = {(p, e): _dot_nt(ki16[:, pairs[p]], qm16[p][e]) for p, e in heads}
        dp_t = {(p, e): _dot_nt(v16[:, pairs[p]], dom16[p][e]) for p, e in heads}
        dp = {h: jnp.where(tri, a, 0.0).astype(BF16) for h, a in dp.items()}
        pm_t = {h: jnp.where(tri_t, a, 0.0).astype(BF16) for h, a in pm_t.items()}
        dp_t = {h: jnp.where(tri_t, a, 0.0).astype(BF16) for h, a in dp_t.items()}
        dv_e = {(p, e): _dot(pm_t[p, e], do16[:, pairs[p]]) for p, e in heads}
        dq_e = {(p, e): _dot(dp[p, e], ki16[:, pairs[p]]) for p, e in heads}
        dk_e = {(p, e): _dot(dp_t[p, e], qd16[:, pairs[p]]) for p, e in heads}
        st = [[st_ref[c, p] for p in range(HEAD_PAIRS)] for c in range(nc)]
        dq_x = [[_dot(do16[rs, ls], st[c][p].astype(BF16)) for p, ls in enumerate(pairs)] for c, rs in enumerate(chunks)]
        gq = [[jnp.where(bd, _dot_tn(do16[rs, ls], qd16[rs, ls]), 0.0) for ls in pairs] for rs in chunks]
        g = [g_scr[p] for p in range(HEAD_PAIRS)]
        dk_end, dv_x, dd = [None] * nc, [None] * nc, [None] * nc
        for c in order:
            rs = chunks[c]
            g16 = [a.astype(BF16) for a in g]
            dk_end[c] = lanes([_dot(v16[rs, ls], g16[p]) for p, ls in enumerate(pairs)])
            dv_x[c] = lanes([_dot_nt(ke16[rs, ls], g16[p]) for p, ls in enumerate(pairs)])
            dd[c] = jnp.broadcast_to(lanes([jnp.sum(g[p] * st[c][p], axis=0, keepdims=True) for p in range(HEAD_PAIRS)]),
                                     (CHUNK, A_WIDTH))
            for p, ls in enumerate(pairs):
                g[p] = g[p] * w["d"][c * CHUNK:c * CHUNK + 1, ls] + gq[c][p]
        for p in range(HEAD_PAIRS):
            g_scr[p] = g[p]

        def both_heads(d):
            return lanes([d[p, 0] * masks[0] + d[p, 1] * masks[1] for p in range(HEAD_PAIRS)])

        dq_dec = both_heads(dq_e) + jnp.concatenate([lanes(a) for a in dq_x], axis=0)
        dk_inv = both_heads(dk_e)
        dk_end = jnp.concatenate(dk_end, axis=0)
        dv = both_heads(dv_e) + jnp.concatenate(dv_x, axis=0)
        dd = jnp.concatenate(dd, axis=0)
        dke = dk_end * w["k_end"]
        dcum = dq_dec * w["q_dec"] - dk_inv * w["k_inv"] - dke
        dk = dk_inv * w["e_neg"] + dk_end * w["e_end"]
        dlast = _per_chunk(dke, lambda a: jnp.sum(a, axis=0, keepdims=True)) + dd * w["d"]
        dlf = _chunk_cumsum(dcum, not reverse) + dlast
        tt = dlf / w["f"] - dk
        dq_ref[...] = dq_dec * w["a"]
        dv_ref[...] = dv
        dz_ref[...] = ((1.0 - lb_all) * w["sig"] * w["sn"] * tt).astype(BF16)
        dlb_scr[...] += jnp.sum(w["sn"] * tt, axis=0, keepdims=True)

        @pl.when(jnp.logical_and(b == pl.num_programs(0) - 1, j == pl.num_programs(1) - 1))
        def _():
            d0 = dlb_scr[...] * lb_all * (1.0 - lb_all)
            dl_ref[0:1, :] = d0
            dl_ref[1:2, :] = -d0

    blk = pl.BlockSpec((rb, A_WIDTH), tmap)
    return pl.pallas_call(
        body, name=f"hgrn_bwd_{direction}", grid=(n_batch, nb),
        in_specs=[blk, blk, blk, blk, pl.BlockSpec((nc, HEAD_PAIRS, 128, 128), smap), _full((2, 2, A_WIDTH))],
        out_specs=[blk, blk, blk, _full((2, A_WIDTH))],
        out_shape=[jax.ShapeDtypeStruct((t, A_WIDTH), F32)] * 2
        + [jax.ShapeDtypeStruct((t, A_WIDTH), BF16), jax.ShapeDtypeStruct((2, A_WIDTH), F32)],
        scratch_shapes=[pltpu.VMEM((HEAD_PAIRS, 128, 128), F32), pltpu.VMEM((1, A_WIDTH), F32)],
        compiler_params=_params(2),
    )(hq, hi, hf, do, st, lbl)


def _swap_rope_halves(x):
    lane = _iota2(x.shape, 1)
    return jnp.where(lane < 32, pltpu.roll(x, 96, 1), pltpu.roll(x, 32, 1))


def _rms_fwd(xv, g):
    r = lax.rsqrt(jnp.mean(xv * xv, axis=-1, keepdims=True) + EPS)
    return (xv * r) * g


def _rms_bwd(dy, xv, g):
    r = lax.rsqrt(jnp.mean(xv * xv, axis=-1, keepdims=True) + EPS)
    xh = xv * r
    dxh = dy * g
    return r * (dxh - xh * jnp.mean(dxh * xh, axis=-1, keepdims=True)), dy * xh


def _mla_prep(cq, ckv, kr, g_qa, g_kva, w_q_p, w_kv_p, cosx, sinx, *, n_batch):
    t = cq.shape[0]
    s = t // n_batch
    tm = _row_tile(s)
    nt = s // tm

    def body(cq_ref, ckv_ref, kr_ref, gq_ref, gkv_ref, wq_ref, wkv_ref, cos_ref, sin_ref,
             q_ref, k_ref, v_ref, cqn_ref, ckvn_ref):
        cos, sin = cos_ref[...], sin_ref[...]
        cqn = _rms_fwd(cq_ref[...], gq_ref[...]).astype(BF16)
        ckvn = _rms_fwd(ckv_ref[...], gkv_ref[...]).astype(BF16)
        cqn_ref[...] = cqn
        ckvn_ref[...] = ckvn
        krv = kr_ref[...]
        kr_roped = (krv * cos + _swap_rope_halves(krv) * sin).astype(BF16)
        for h in range(B_HEADS):
            o = h * QK_PAD
            q_ref[:, o:o + 128] = _dot(cqn, wq_ref[:, o:o + 128]).astype(BF16)
            qr = _dot(cqn, wq_ref[:, o + 128:o + 256])
            q_ref[:, o + 128:o + 256] = (qr * cos + _swap_rope_halves(qr) * sin).astype(BF16)
            k_ref[:, o:o + 128] = _dot(ckvn, wkv_ref[:, h * 128:(h + 1) * 128]).astype(BF16)
            k_ref[:, o + 128:o + 256] = kr_roped
        v_ref[...] = _dot(ckvn, wkv_ref[:, 512:1024]).astype(BF16)

    tab = pl.BlockSpec((tm, 128), lambda i: (i % nt, 0))
    return pl.pallas_call(
        body, name="mla_prep", grid=(t // tm,),
        in_specs=[_rows(tm, Q_LORA), _rows(tm, KV_LORA), _rows(tm, 128), _full((1, Q_LORA)), _full((1, KV_LORA)),
                  _full((Q_LORA, 1024)), _full((KV_LORA, 1024)), tab, tab],
        out_specs=[_rows(tm, 1024), _rows(tm, 1024), _rows(tm, 512), _rows(tm, Q_LORA), _rows(tm, KV_LORA)],
        out_shape=[jax.ShapeDtypeStruct((t, 1024), BF16), jax.ShapeDtypeStruct((t, 1024), BF16),
                   jax.ShapeDtypeStruct((t, 512), BF16), jax.ShapeDtypeStruct((t, Q_LORA), BF16),
                   jax.ShapeDtypeStruct((t, KV_LORA), BF16)],
        compiler_params=_params(1),
    )(cq, ckv, kr, g_qa, g_kva, w_q_p, w_kv_p, cosx, sinx)


def _mla_prep_bwd(dqt, dk, dv, cq, ckv, g_qa, g_kva, w_q_p, w_kv_p, cosx, sinx, *, n_batch):
    t = cq.shape[0]
    s = t // n_batch
    tm = _row_tile(s)
    nt = s // tm

    def body(dqt_ref, dk_ref, dv_ref, cq_ref, ckv_ref, gq_ref, gkv_ref, wq_ref, wkv_ref, cos_ref, sin_ref,
             dcq_ref, dckv_ref, dkr_ref, dqp_ref, dkvp_ref, dgq_ref, dgkv_ref):
        cos, sin = cos_ref[...], sin_ref[...]
        first = pl.program_id(0) == 0

        def unrope(d):
            return d * cos + _swap_rope_halves(d * sin)

        dcqn = None
        dkr = None
        dckvn = None
        for h in range(B_HEADS):
            o = h * QK_PAD
            dq_h = jnp.transpose(dqt_ref[o:o + QK_PAD, :])
            dqn16 = dq_h[:, 0:128].astype(BF16)
            dqr16 = unrope(dq_h[:, 128:256]).astype(BF16)
            dqp_ref[:, o:o + 128] = dqn16
            dqp_ref[:, o + 128:o + 256] = dqr16
            part = _dot_nt(dqn16, wq_ref[:, o:o + 128]) + _dot_nt(dqr16, wq_ref[:, o + 128:o + 256])
            dcqn = part if dcqn is None else dcqn + part
            dkn16 = dk_ref[:, o:o + 128].astype(BF16)
            dkvp_ref[:, h * 128:(h + 1) * 128] = dkn16
            part = _dot_nt(dkn16, wkv_ref[:, h * 128:(h + 1) * 128])
            dckvn = part if dckvn is None else dckvn + part
            kr_part = dk_ref[:, o + 128:o + 256]
            dkr = kr_part if dkr is None else dkr + kr_part
        dv16 = dv_ref[...].astype(BF16)
        dkvp_ref[:, 512:1024] = dv16
        dckvn = dckvn + _dot_nt(dv16, wkv_ref[:, 512:1024])
        dkr_ref[...] = unrope(dkr).astype(BF16)
        dcq, dgq = _rms_bwd(dcqn, cq_ref[...], gq_ref[...])
        dckv, dgkv = _rms_bwd(dckvn, ckv_ref[...], gkv_ref[...])
        dcq_ref[...] = dcq.astype(BF16)
        dckv_ref[...] = dckv.astype(BF16)
        _acc_rows(dgq_ref, dgq, first)
        _acc_rows(dgkv_ref, dgkv, first)

    tab = pl.BlockSpec((tm, 128), lambda i: (i % nt, 0))
    return pl.pallas_call(
        body, name="mla_prep_bwd", grid=(t // tm,),
        in_specs=[pl.BlockSpec((1024, tm), lambda i: (0, i)), _rows(tm, 1024), _rows(tm, 512), _rows(tm, Q_LORA),
                  _rows(tm, KV_LORA),
                  _full((1, Q_LORA)), _full((1, KV_LORA)), _full((Q_LORA, 1024)), _full((KV_LORA, 1024)), tab, tab],
        out_specs=[_rows(tm, Q_LORA), _rows(tm, KV_LORA), _rows(tm, 128), _rows(tm, 1024), _rows(tm, 1024),
                   _full((1, Q_LORA)), _full((1, KV_LORA))],
        out_shape=[jax.ShapeDtypeStruct((t, Q_LORA), BF16), jax.ShapeDtypeStruct((t, KV_LORA), BF16),
                   jax.ShapeDtypeStruct((t, 128), BF16), jax.ShapeDtypeStruct((t, 1024), BF16),
                   jax.ShapeDtypeStruct((t, 1024), BF16), jax.ShapeDtypeStruct((1, Q_LORA), F32),
                   jax.ShapeDtypeStruct((1, KV_LORA), F32)],
        compiler_params=_params(1),
    )(dqt, dk, dv, cq, ckv, g_qa, g_kva, w_q_p, w_kv_p, cosx, sinx)


def _attn_dims(t, n_batch):
    s = t // n_batch
    tq = _pick(s, 256, 128)
    return s, tq, s // tq


def _grid_ends(n_axes):
    ids = [pl.program_id(a) for a in range(n_axes)]
    first = functools.reduce(jnp.logical_and, [i == 0 for i in ids])
    last = functools.reduce(jnp.logical_and, [i == pl.num_programs(a) - 1 for a, i in enumerate(ids)])
    return first, last


def _attn_fwd(q, k, v, wsrcs, *, n_batch):
    t = q.shape[0]
    s, tq, nq = _attn_dims(t, n_batch)
    nw = len(wsrcs)

    def body(q_ref, k_ref, v_ref, *refs):
        w_refs, (o_ref, lse_ref), wall_refs = refs[:nw], refs[nw:nw + 2], refs[nw + 2:2 * nw + 2]
        send_sems, recv_sems, local_sems = refs[2 * nw + 2:]
        first, last = _grid_ends(3)

        @pl.when(first)
        def _():
            for i in range(nw):
                _gather_start(w_refs[i], wall_refs[i], send_sems, recv_sems, local_sems.at[i], 3 * i)

        @pl.when(last)
        def _():
            for i in range(nw):
                _gather_wait(w_refs[i], wall_refs[i], send_sems, recv_sems, local_sems.at[i], 3 * i)

        raw = _dot_nt(q_ref[...], k_ref[...])
        m = jnp.max(raw, axis=-1, keepdims=True)
        p = jnp.exp2((raw - m) * SCALE_LOG2E)
        l = jnp.sum(p, axis=-1, keepdims=True)
        o_ref[...] = _dot(p.astype(BF16), v_ref[...]) / l
        lse2 = m * SCALE_LOG2E + jnp.log(l) * LOG2E
        lse_ref[...] = jnp.transpose(jnp.broadcast_to(lse2, (tq, 128)))[0:1, :]

    return pl.pallas_call(
        body, name="attn_fwd", grid=(n_batch, B_HEADS, nq),
        in_specs=[pl.BlockSpec((tq, QK_PAD), lambda b, h, i: (b * nq + i, h)),
                  pl.BlockSpec((s, QK_PAD), lambda b, h, i: (b, h)),
                  pl.BlockSpec((s, B_V), lambda b, h, i: (b, h))] + [_HBM] * nw,
        out_specs=[pl.BlockSpec((tq, B_V), lambda b, h, i: (b * nq + i, h)),
                   pl.BlockSpec((None, 1, tq), lambda b, h, i: (h, 0, b * nq + i))] + [_HBM] * nw,
        out_shape=[jax.ShapeDtypeStruct((t, B_HEADS * B_V), F32), jax.ShapeDtypeStruct((B_HEADS, 1, t), F32)]
        + [jax.ShapeDtypeStruct((N_CHIPS,) + w.shape, w.dtype) for w in wsrcs],
        scratch_shapes=[pltpu.SemaphoreType.DMA((3 * nw,)), pltpu.SemaphoreType.DMA((3 * nw,)),
                        pltpu.SemaphoreType.DMA((nw,))],
        compiler_params=_params(3),
    )(q, k, v, *wsrcs)


def _attn_bwd(q, k, v, do16, lse, delta, gsegs, *, n_batch):
    t = q.shape[0]
    s = t // n_batch
    tk = _pick(s, 512, 128)
    nk = s // tk
    ng = len(gsegs)

    def body(q_ref, k_ref, v_ref, do_ref, lse_ref, dl_ref, *refs):
        g_refs, (dqt_ref, dk_ref, dv_ref), recv_refs = refs[:ng], refs[ng:ng + 3], refs[ng + 3:2 * ng + 3]
        send_sems, recv_sems = refs[2 * ng + 3:]
        first, last = _grid_ends(3)

        @pl.when(first)
        def _():
            for i in range(ng):
                _scatter_start(g_refs[i], recv_refs[i], send_sems, recv_sems, 3 * i)

        @pl.when(last)
        def _():
            for i in range(ng):
                _scatter_wait(g_refs[i], recv_refs[i], send_sems, recv_sems, 3 * i)

        j = pl.program_id(2)
        qv, kv, dov = q_ref[...], k_ref[...], do_ref[...]
        pt = jnp.exp2(_dot_nt(kv, qv) * SCALE_LOG2E - lse_ref[...])
        dv_ref[...] = _dot(pt.astype(BF16), dov)
        dpt = _dot_nt(v_ref[...], dov)
        dst = (pt * (dpt - dl_ref[...])).astype(BF16)
        dk_ref[...] = _dot(dst, qv) * ATTN_SCALE
        part = _dot_tn(kv, dst)

        @pl.when(j == 0)
        def _():
            dqt_ref[...] = part

        @pl.when(j != 0)
        def _():
            dqt_ref[...] += part

        @pl.when(j == nk - 1)
        def _():
            dqt_ref[...] = dqt_ref[...] * ATTN_SCALE

    row = pl.BlockSpec((None, 1, s), lambda b, h, j: (h, 0, b))
    return pl.pallas_call(
        body, name="attn_bwd", grid=(n_batch, B_HEADS, nk),
        in_specs=[pl.BlockSpec((s, QK_PAD), lambda b, h, j: (b, h)),
                  pl.BlockSpec((tk, QK_PAD), lambda b, h, j: (b * nk + j, h)),
                  pl.BlockSpec((tk, B_V), lambda b, h, j: (b * nk + j, h)),
                  pl.BlockSpec((s, B_V), lambda b, h, j: (b, h)), row, row] + [_HBM] * ng,
        out_specs=[pl.BlockSpec((QK_PAD, s), lambda b, h, j: (h, b)),
                   pl.BlockSpec((tk, QK_PAD), lambda b, h, j: (b * nk + j, h)),
                   pl.BlockSpec((tk, B_V), lambda b, h, j: (b * nk + j, h))] + [_HBM] * ng,
        out_shape=[jax.ShapeDtypeStruct((B_HEADS * QK_PAD, t), F32), jax.ShapeDtypeStruct((t, B_HEADS * QK_PAD), F32),
                   jax.ShapeDtypeStruct((t, B_HEADS * B_V), F32)]
        + [jax.ShapeDtypeStruct((3,) + g.shape[1:], g.dtype) for g in gsegs],
        scratch_shapes=[pltpu.SemaphoreType.DMA((3 * ng,)), pltpu.SemaphoreType.DMA((3 * ng,))],
        compiler_params=_params(3),
    )(q, k, v, do16, lse, delta, *gsegs)


def _group_ones16():
    r = _iota2((A_WIDTH, A_WIDTH), 0) // 64
    c = _iota2((A_WIDTH, A_WIDTH), 1) // 64
    return (r == c).astype(BF16)


def _head_rms(o, ones16):
    return lax.rsqrt(_exact_right(o * o, ones16) * (1.0 / 64.0) + EPS)


def _out_fwd(o_f, o_r, hg, o_attn, x, g_hn, g_mla, w_out, g2):
    t = x.shape[0]
    tm = _row_tile(t)

    def body(of_ref, or_ref, hg_ref, oa_ref, x_ref, ghn_ref, gm_ref, w_ref, g2_ref, y_ref, x2_ref, h2_ref):
        ones16 = _group_ones16()
        o = of_ref[...] + or_ref[...]
        hgv = hg_ref[...]
        ya = ((o * _head_rms(o, ones16)) * ghn_ref[...]) * (hgv * _sigmoid(hgv))
        yb = _rms_fwd(oa_ref[...], gm_ref[...])
        ya16, yb16 = ya.astype(BF16), yb.astype(BF16)
        y_ref[:, 0:A_WIDTH] = ya16
        y_ref[:, A_WIDTH:D_MODEL] = yb16
        x2 = x_ref[...] + _dot(ya16, w_ref[0:A_WIDTH, :]) + _dot(yb16, w_ref[A_WIDTH:D_MODEL, :])
        x2_ref[...] = x2
        h2_ref[...] = _rms_fwd(x2, g2_ref[...]).astype(BF16)

    a512 = _rows(tm, A_WIDTH)
    return pl.pallas_call(
        body, name="out_fwd", grid=(t // tm,),
        in_specs=[a512, a512, a512, a512, _rows(tm, D_MODEL), _full((1, A_WIDTH)), _full((1, A_WIDTH)),
                  _full((D_MODEL, D_MODEL)), _full((1, D_MODEL))],
        out_specs=[_rows(tm, D_MODEL)] * 3,
        out_shape=[jax.ShapeDtypeStruct((t, D_MODEL), BF16), jax.ShapeDtypeStruct((t, D_MODEL), F32),
                   jax.ShapeDtypeStruct((t, D_MODEL), BF16)],
        compiler_params=_params(1),
    )(o_f, o_r, hg, o_attn, x, g_hn, g_mla, w_out, g2)


def _out_bwd(dx2_16, o_f, o_r, hg, o_attn, g_hn, g_mla, w_out):
    t = dx2_16.shape[0]
    tm = _row_tile(t)

    def body(dx_ref, of_ref, or_ref, hg_ref, oa_ref, ghn_ref, gm_ref, w_ref,
             do_ref, dhg_ref, doa_ref, dl_ref, dghn_ref, dgm_ref):
        first = pl.program_id(0) == 0
        ones16 = _group_ones16()
        dxv = dx_ref[...]
        dya = _dot_nt(dxv, w_ref[0:A_WIDTH, :])
        dyb = _dot_nt(dxv, w_ref[A_WIDTH:D_MODEL, :])
        o = of_ref[...] + or_ref[...]
        rh = _head_rms(o, ones16)
        oh = o * rh
        hgv = hg_ref[...]
        sg = _sigmoid(hgv)
        sl = hgv * sg
        ghn = ghn_ref[...]
        dhg_ref[...] = ((dya * (oh * ghn)) * (sg * (1.0 + hgv * (1.0 - sg)))).astype(BF16)
        _acc_rows(dghn_ref, dya * sl * oh, first)
        doh = dya * sl * ghn
        do_ref[...] = rh * (doh - oh * (_exact_right(doh * oh, ones16) * (1.0 / 64.0)))
        oav = oa_ref[...]
        doa, dgm = _rms_bwd(dyb, oav, gm_ref[...])
        doa_ref[...] = doa.astype(BF16)
        _acc_rows(dgm_ref, dgm, first)
        sel16 = (_iota2((8, A_WIDTH), 0) == _iota2((8, A_WIDTH), 1) // B_V).astype(BF16)
        x1, x2, x3 = _split3(doa * oav)
        delta = _dot_nt(sel16, x1) + _dot_nt(sel16, x2) + _dot_nt(sel16, x3)
        for h in range(B_HEADS):
            dl_ref[h] = delta[h:h + 1, :]

    a512 = _rows(tm, A_WIDTH)
    return pl.pallas_call(
        body, name="out_bwd", grid=(t // tm,),
        in_specs=[_rows(tm, D_MODEL), a512, a512, a512, a512, _full((1, A_WIDTH)), _full((1, A_WIDTH)),
                  _full((D_MODEL, D_MODEL))],
        out_specs=[a512, a512, a512, pl.BlockSpec((B_HEADS, 1, tm), lambda i: (0, 0, i)),
                   _full((1, A_WIDTH)), _full((1, A_WIDTH))],
        out_shape=[jax.ShapeDtypeStruct((t, A_WIDTH), F32)] + [jax.ShapeDtypeStruct((t, A_WIDTH), BF16)] * 2
        + [jax.ShapeDtypeStruct((B_HEADS, 1, t), F32)]
        + [jax.ShapeDtypeStruct((1, A_WIDTH), F32)] * 2,
        compiler_params=_params(1),
    )(dx2_16, o_f, o_r, hg, o_attn, g_hn, g_mla, w_out)


def _ffn_fwd_bwd(h2, x2, target, w_gate, w_up, w_down, g_f, g2):
    t = x2.shape[0]
    tm = _row_tile(t)
    inv_d = 1.0 / D_MODEL

    def body(h2_ref, x2_ref, tg_ref, wg_ref, wu_ref, wd_ref, gf_ref, g2_ref,
             act_ref, dgate_ref, dup_ref, dx3_ref, dx2_ref, dx2h_ref, loss_ref, dgf_ref, dg2_ref):
        first = pl.program_id(0) == 0
        h2v = h2_ref[...]
        gate = _dot(h2v, wg_ref[...])
        up = _dot(h2v, wu_ref[...])
        sg = _sigmoid(gate)
        sl = gate * sg
        act16 = (sl * up).astype(BF16)
        act_ref[...] = act16
        x2v = x2_ref[...]
        x3 = x2v + _dot(act16, wd_ref[...])
        r3 = lax.rsqrt(jnp.mean(x3 * x3, axis=-1, keepdims=True) + EPS)
        x3h = x3 * r3
        gf = gf_ref[...]
        err = x3h * gf - tg_ref[...]
        part = 0.5 * jnp.sum(jnp.mean(err * err, axis=-1, keepdims=True), axis=0, keepdims=True)

        @pl.when(first)
        def _():
            loss_ref[...] = jnp.zeros_like(loss_ref)

        loss_ref[...] += part
        dy = err * inv_d
        _acc_rows(dgf_ref, dy * x3h, first)
        dx3h = dy * gf
        dx3 = r3 * (dx3h - x3h * jnp.mean(dx3h * x3h, axis=-1, keepdims=True))
        dx3_16 = dx3.astype(BF16)
        dx3_ref[...] = dx3_16
        da = _dot_nt(dx3_16, wd_ref[...])
        dup16 = (da * sl).astype(BF16)
        dgate16 = (da * up * (sg * (1.0 + gate * (1.0 - sg)))).astype(BF16)
        dup_ref[...] = dup16
        dgate_ref[...] = dgate16
        dh2 = _dot_nt(dgate16, wg_ref[...]) + _dot_nt(dup16, wu_ref[...])
        dx2n, dg2 = _rms_bwd(dh2, x2v, g2_ref[...])
        _acc_rows(dg2_ref, dg2, first)
        dx2 = dx3 + dx2n
        dx2_ref[...] = dx2
        dx2h_ref[...] = dx2.astype(BF16)

    return pl.pallas_call(
        body, name="ffn_fwd_bwd", grid=(t // tm,),
        in_specs=[_rows(tm, D_MODEL), _rows(tm, D_MODEL), _rows(tm, D_MODEL), _full((D_MODEL, D_FF), True),
                  _full((D_MODEL, D_FF), True), _full((D_FF, D_MODEL), True), _full((1, D_MODEL)), _full((1, D_MODEL))],
        out_specs=[_rows(tm, D_FF), _rows(tm, D_FF), _rows(tm, D_FF), _rows(tm, D_MODEL), _rows(tm, D_MODEL),
                   _rows(tm, D_MODEL), _full((8, 128)), _full((1, D_MODEL)), _full((1, D_MODEL))],
        out_shape=[jax.ShapeDtypeStruct((t, D_FF), BF16)] * 3
        + [jax.ShapeDtypeStruct((t, D_MODEL), BF16), jax.ShapeDtypeStruct((t, D_MODEL), F32),
           jax.ShapeDtypeStruct((t, D_MODEL), BF16), jax.ShapeDtypeStruct((8, 128), F32),
           jax.ShapeDtypeStruct((1, D_MODEL), F32), jax.ShapeDtypeStruct((1, D_MODEL), F32)],
        compiler_params=_params(1),
    )(h2, x2, target, w_gate, w_up, w_down, g_f, g2)


def _wgrad(a, b, name):
    t, m = a.shape
    n = b.shape[1]
    bm = _pick(m, 1408)
    bn = _pick(n, 1664)
    tk = _pick(t, 1024, 16)
    nk = t // tk

    def body(a_ref, b_ref, o_ref):
        part = _dot_tn(a_ref[...], b_ref[...])

        @pl.when(pl.program_id(2) == 0)
        def _():
            o_ref[...] = part

        @pl.when(pl.program_id(2) != 0)
        def _():
            o_ref[...] += part

    return pl.pallas_call(
        body, name=name, grid=(m // bm, n // bn, nk),
        in_specs=[pl.BlockSpec((tk, bm), lambda i, j, k: (k, i)), pl.BlockSpec((tk, bn), lambda i, j, k: (k, j))],
        out_specs=pl.BlockSpec((bm, bn), lambda i, j, k: (i, j)),
        out_shape=jax.ShapeDtypeStruct((m, n), F32),
        compiler_params=_params(3),
    )(a, b)


def _rope_tables(seq):
    inv = 1.0 / (ROPE_THETA ** (jnp.arange(0, B_ROPE, 2, dtype=F32) / B_ROPE))
    ang = jnp.arange(seq, dtype=F32)[:, None] * inv[None, :]
    cos, sin = jnp.cos(ang), jnp.sin(ang)
    zeros = jnp.zeros((seq, 64), F32)
    return jnp.concatenate([cos, cos, zeros], axis=1), jnp.concatenate([-sin, sin, zeros], axis=1)


def _pad_weights(w_in, w_q_b, w_kv_b):
    w_in_p = jnp.pad(w_in, ((0, 0), (0, D_IN_PAD - D_IN)))
    w_q_p = jnp.pad(w_q_b.reshape(Q_LORA, B_HEADS, B_NOPE + B_ROPE), ((0, 0), (0, 0), (0, 64))).reshape(Q_LORA, 1024)
    kv = w_kv_b.reshape(KV_LORA, B_HEADS, B_NOPE + B_V)
    w_kv_p = jnp.concatenate([kv[:, :, :B_NOPE].reshape(KV_LORA, 512), kv[:, :, B_NOPE:].reshape(KV_LORA, 512)], axis=1)
    return w_in_p, w_q_p, w_kv_p


def _unpad_grads(g_in_p, g_q_p, g_kv_p):
    g_in = g_in_p[:, :D_IN]
    g_q = g_q_p.reshape(Q_LORA, B_HEADS, QK_PAD)[:, :, :B_NOPE + B_ROPE].reshape(Q_LORA, B_HEADS * (B_NOPE + B_ROPE))
    g_kv = jnp.concatenate([g_kv_p[:, :512].reshape(KV_LORA, B_HEADS, B_NOPE),
                            g_kv_p[:, 512:].reshape(KV_LORA, B_HEADS, B_V)], axis=2).reshape(KV_LORA, 1024)
    return g_in, g_q, g_kv


def _local_step(x, target, lbl, g1, g_hn, g_qa, g_kva, g_mla, g2, g_f, w_in, w_q_b, w_kv_b, late_shards):
    n_batch, seq, _ = x.shape
    t = n_batch * seq
    x = x.reshape(t, D_MODEL)
    target = target.reshape(t, D_MODEL)
    w_in_p, w_q_p, w_kv_p = _pad_weights(w_in, w_q_b, w_kv_b)
    cosx, sinx = _rope_tables(seq)

    h1, hq, hi, hff, hfb, hg, cq, ckv, kr = _in_fwd(x, g1, w_in_p)
    o_f, st_f = _hgrn_fwd(hq, hi, hff, lbl, n_batch=n_batch, direction=0)
    o_r, st_r = _hgrn_fwd(hq, hi, hfb, lbl, n_batch=n_batch, direction=1)
    q, k, v, cqn, ckvn = _mla_prep(cq, ckv, kr, g_qa, g_kva, w_q_p, w_kv_p, cosx, sinx, n_batch=n_batch)
    o_attn, lse, *gathered = _attn_fwd(q, k, v, late_shards, n_batch=n_batch)
    w_out, w_gate, w_up, w_down = (_join_shards(n, a) for n, a in zip(_LATE, gathered))
    ycat, x2, h2 = _out_fwd(o_f, o_r, hg, o_attn, x, g_hn, g_mla, w_out, g2)
    act, dgate, dup, dx3_16, dx2, dx2_16, loss, dg_f, dg2 = _ffn_fwd_bwd(h2, x2, target, w_gate, w_up, w_down, g_f, g2)
    g_late = dict(w_out=_wgrad(ycat, dx2_16, "wgrad_out"), w_gate=_wgrad(h2, dgate, "wgrad_gate"),
                  w_up=_wgrad(h2, dup, "wgrad_up"), w_down=_wgrad(act, dx3_16, "wgrad_down"))
    do_h, dhg, do_attn16, delta, dg_hn, dg_mla = _out_bwd(dx2_16, o_f, o_r, hg, o_attn, g_hn, g_mla, w_out)
    dqt, dk, dv, *recv_late = _attn_bwd(q, k, v, do_attn16, lse, delta,
                                        [_segments(n, g_late[n]).astype(BF16) for n in _LATE], n_batch=n_batch)
    dcq, dckv, dkr, dqp16, dkvp16, dg_qa, dg_kva = _mla_prep_bwd(dqt, dk, dv, cq, ckv, g_qa, g_kva, w_q_p, w_kv_p,
                                                                  cosx, sinx, n_batch=n_batch)
    dq_f, dv_f, dz_f, dl_f = _hgrn_bwd(hq, hi, hff, do_h, st_f, lbl, n_batch=n_batch, direction=0)
    dq_r, dv_r, dz_r, dl_r = _hgrn_bwd(hq, hi, hfb, do_h, st_r, lbl, n_batch=n_batch, direction=1)
    dx, dproj16, dg1 = _in_bwd(dq_f, dq_r, dv_f, dv_r, dz_f, dz_r, dhg, hq, dcq, dckv, dkr, dx2, x, g1, w_in_p)

    gw_in, gw_q, gw_kv = _unpad_grads(_wgrad(h1, dproj16, "wgrad_in"), _wgrad(cqn, dqp16, "wgrad_q_b"),
                                      _wgrad(ckvn, dkvp16, "wgrad_kv_b"))
    grads = dict(
        norm1_g=dg1, w_in=gw_in, lb_logits=jnp.stack([dl_f, dl_r]), hgrn_norm_g=dg_hn, q_a_norm_g=dg_qa,
        w_q_b=gw_q, kv_a_norm_g=dg_kva, w_kv_b=gw_kv, mla_norm_g=dg_mla, norm2_g=dg2, final_norm_g=dg_f)
    grads.update(g_late)
    return loss[0, 0], dx.reshape(n_batch, seq, D_MODEL), grads, dict(zip(_LATE, recv_late))


_HBM = pl.BlockSpec(memory_space=pltpu.HBM)
_MESH = pl.DeviceIdType.MESH


def _place():
    x, y, c = lax.axis_index("x"), lax.axis_index("y"), lax.axis_index("c")
    other_chips = [(1 - x, y), (x, 1 - y), (1 - x, 1 - y)]
    return x, y, c, other_chips


def _gather_copies(w_ref, wall_ref, send_sems, recv_sems, local_sem, base=0):
    x, y, c, chips = _place()
    mine = 2 * x + y

    def mk(j, chip_index, to):
        return pltpu.make_async_remote_copy(src_ref=w_ref, dst_ref=wall_ref.at[chip_index], send_sem=send_sems.at[base + j],
                                            recv_sem=recv_sems.at[base + j], device_id=to, device_id_type=_MESH)

    local = pltpu.make_async_copy(w_ref, wall_ref.at[mine], local_sem)
    sends = [mk(j, mine, (*chip, c)) for j, chip in enumerate(chips)]
    recvs = [mk(j, 2 * px + py, (x, y, c)) for j, (px, py) in enumerate(chips)]
    return local, sends, recvs


def _gather_start(*refs):
    local, sends, _ = _gather_copies(*refs)
    local.start()
    for cp in sends:
        cp.start()


def _gather_wait(*refs):
    local, sends, recvs = _gather_copies(*refs)
    for cp in recvs:
        cp.wait_recv()
    for cp in sends:
        cp.wait_send()
    local.wait()


def _scatter_copies(g_ref, recv_ref, send_sems, recv_sems, base=0):
    x, y, c, chips = _place()

    def mk(j, src_index, to):
        return pltpu.make_async_remote_copy(src_ref=g_ref.at[src_index], dst_ref=recv_ref.at[j],
                                            send_sem=send_sems.at[base + j], recv_sem=recv_sems.at[base + j],
                                            device_id=to, device_id_type=_MESH)

    sends = [mk(j, 2 * px + py, (px, py, c)) for j, (px, py) in enumerate(chips)]
    recvs = [mk(j, 0, (x, y, c)) for j in range(3)]
    return sends, recvs


def _scatter_start(*refs):
    for cp in _scatter_copies(*refs)[0]:
        cp.start()


def _scatter_wait(*refs):
    sends, recvs = _scatter_copies(*refs)
    for cp in recvs:
        cp.wait_recv()
    for cp in sends:
        cp.wait_send()


def _gather_shards(srcs):
    n = len(srcs)

    def body(*refs):
        src_refs, dst_refs = refs[:n], refs[n:2 * n]
        send_sems, recv_sems, local_sems = refs[2 * n:]
        for i in range(n):
            _gather_start(src_refs[i], dst_refs[i], send_sems, recv_sems, local_sems.at[i], 3 * i)
        for i in range(n):
            _gather_wait(src_refs[i], dst_refs[i], send_sems, recv_sems, local_sems.at[i], 3 * i)

    return pl.pallas_call(
        body, name="gather_shards", in_specs=[_HBM] * n, out_specs=[_HBM] * n,
        out_shape=[jax.ShapeDtypeStruct((N_CHIPS,) + a.shape, a.dtype) for a in srcs],
        scratch_shapes=[pltpu.SemaphoreType.DMA((3 * n,)), pltpu.SemaphoreType.DMA((3 * n,)), pltpu.SemaphoreType.DMA((n,))],
    )(*srcs)


def _scatter_grads(gsegs, small):
    n = len(gsegs)

    def body(*refs):
        g_refs, s_ref, recv_refs, sall_ref = refs[:n], refs[n], refs[n + 1:2 * n + 1], refs[2 * n + 1]
        send_sems, recv_sems, local_sem = refs[2 * n + 2:]
        x, y, c, _ = _place()
        me = 4 * x + 2 * y + c
        flips = [(fx, fy, fc) for fx in (0, 1) for fy in (0, 1) for fc in (0, 1)][1:]

        def peer(f):
            return tuple((1 - a) if b else a for a, b in zip((x, y, c), f))

        def sm(r, index, to):
            return pltpu.make_async_remote_copy(src_ref=s_ref, dst_ref=sall_ref.at[index], send_sem=send_sems.at[3 * n + r],
                                                recv_sem=recv_sems.at[3 * n + r], device_id=to, device_id_type=_MESH)

        local = pltpu.make_async_copy(s_ref, sall_ref.at[me], local_sem)
        local.start()
        for i in range(n):
            _scatter_start(g_refs[i], recv_refs[i], send_sems, recv_sems, 3 * i)
        sends = [sm(r, me, peer(f)) for r, f in enumerate(flips)]
        for cp in sends:
            cp.start()
        for i in range(n):
            _scatter_wait(g_refs[i], recv_refs[i], send_sems, recv_sems, 3 * i)
        for r, f in enumerate(flips):
            px, py, pc = peer(f)
            sm(r, 4 * px + 2 * py + pc, (x, y, c)).wait_recv()
        for cp in sends:
            cp.wait_send()
        local.wait()

    return pl.pallas_call(
        body, name="scatter_grads", in_specs=[_HBM] * (n + 1), out_specs=[_HBM] * (n + 1),
        out_shape=[jax.ShapeDtypeStruct((3,) + g.shape[1:], g.dtype) for g in gsegs]
        + [jax.ShapeDtypeStruct((8,) + small.shape, F32)],
        scratch_shapes=[pltpu.SemaphoreType.DMA((3 * n + 7,)), pltpu.SemaphoreType.DMA((3 * n + 7,)),
                        pltpu.SemaphoreType.DMA(())],
    )(*gsegs, small)


def _swap_sibling(parts):
    n = len(parts)

    def body(*refs):
        send_sems, recv_sems = refs[2 * n:]
        x, y, c, _ = _place()
        cps = [pltpu.make_async_remote_copy(src_ref=refs[i], dst_ref=refs[n + i], send_sem=send_sems.at[i],
                                            recv_sem=recv_sems.at[i], device_id=(x, y, 1 - c), device_id_type=_MESH)
               for i in range(n)]
        for cp in cps:
            cp.start()
        for cp in cps:
            cp.wait()

    return pl.pallas_call(
        body, name="swap_sibling", in_specs=[_HBM] * n, out_specs=[_HBM] * n,
        out_shape=[jax.ShapeDtypeStruct(p.shape, p.dtype) for p in parts],
        scratch_shapes=[pltpu.SemaphoreType.DMA((n,)), pltpu.SemaphoreType.DMA((n,))],
    )(*parts)


def _sum_segments(own, recv, name):
    rows, cols = own.shape
    tm = _pick(rows, 256, 16)

    def body(o_ref, r_ref, out_ref):
        acc = o_ref[...]
        for j in range(3):
            acc = acc + r_ref[j].astype(F32)
        out_ref[...] = acc

    return pl.pallas_call(
        body, name=name, grid=(rows // tm,),
        in_specs=[_rows(tm, cols), pl.BlockSpec((3, tm, cols), lambda i: (0, i, 0))],
        out_specs=_rows(tm, cols), out_shape=jax.ShapeDtypeStruct(own.shape, F32), compiler_params=_params(1),
    )(own, recv)


def _sum_devices(sall):
    def body(s_ref, o_ref):
        acc = s_ref[0]
        for d in range(1, 8):
            acc = acc + s_ref[d]
        o_ref[...] = acc

    return pl.pallas_call(body, name="sum_devices", out_shape=jax.ShapeDtypeStruct(sall.shape[1:], F32))(sall)


def _adamw(w, m, v, ga, gb, name):
    rows, cols = w.shape
    tm = _row_tile(rows)
    two = gb is not None

    def body(*refs):
        w_ref, m_ref, v_ref, ga_ref = refs[:4]
        g_ref, d_ref, m2_ref, v2_ref = refs[-4:]
        g = ga_ref[...] + refs[4][...] if two else ga_ref[...]
        m2 = ADAM_B1 * m_ref[...] + (1.0 - ADAM_B1) * g
        v2 = ADAM_B2 * v_ref[...] + (1.0 - ADAM_B2) * (g * g)
        m_hat = m2 / (1.0 - ADAM_B1 ** ADAM_STEP)
        v_hat = v2 / (1.0 - ADAM_B2 ** ADAM_STEP)
        g_ref[...] = g
        d_ref[...] = -ADAM_LR * (m_hat / (jnp.sqrt(v_hat) + ADAM_EPS) + ADAM_WD * w_ref[...])
        m2_ref[...] = m2
        v2_ref[...] = v2

    blk = _rows(tm, cols)
    args = (w, m, v, ga) + ((gb,) if two else ())
    return pl.pallas_call(
        body, name=name, grid=(rows // tm,), in_specs=[blk] * len(args), out_specs=[blk] * 4,
        out_shape=[jax.ShapeDtypeStruct(w.shape, F32)] * 4, compiler_params=_params(1),
    )(*args)


_COLUMN_SHARDED = ("w_in", "w_q_b", "w_kv_b", "w_gate", "w_up")
_FULL_SHAPES = dict(w_in=(D_MODEL, D_IN), w_q_b=(Q_LORA, 768), w_kv_b=(KV_LORA, 1024), w_out=(D_MODEL, D_MODEL),
                    w_gate=(D_MODEL, D_FF), w_up=(D_MODEL, D_FF), w_down=(D_FF, D_MODEL))
_SMALL = (("norm1_g", 1024), ("lb_logits", 2048), ("hgrn_norm_g", 512), ("q_a_norm_g", 384), ("kv_a_norm_g", 256),
          ("mla_norm_g", 512), ("norm2_g", 1024), ("final_norm_g", 1024))
_UPDATE_ROWS = 48


def _pad_rows(a, rows):
    return jnp.pad(a, ((0, rows - a.shape[0]), (0, 0)))


_EARLY = ("w_in", "w_q_b", "w_kv_b")
_LATE = ("w_out", "w_gate", "w_up", "w_down")


def _segments(name, g):
    r, c = g.shape
    if name in _COLUMN_SHARDED:
        return g.reshape(r, N_CHIPS, c // N_CHIPS).transpose(1, 0, 2)
    return g.reshape(N_CHIPS, r // N_CHIPS, c)


def _own_segment(name, g, chip):
    r, c = g.shape
    if name in _COLUMN_SHARDED:
        return lax.dynamic_slice_in_dim(g, chip * (c // N_CHIPS), c // N_CHIPS, axis=1)
    return lax.dynamic_slice_in_dim(g, chip * (r // N_CHIPS), r // N_CHIPS, axis=0)


def _join_shards(name, seg):
    r, c = _FULL_SHAPES[name]
    if name in _COLUMN_SHARDED:
        return seg.transpose(1, 0, 2).reshape(r, c)
    return seg.reshape(r, c)


def kernel(x, norm1_g, w_in, lb_logits, hgrn_norm_g, q_a_norm_g, w_q_b, kv_a_norm_g, w_kv_b, mla_norm_g, w_out, norm2_g, w_gate, w_up, w_down, final_norm_g, loss_target, m_norm1_g, m_w_in, m_lb_logits, m_hgrn_norm_g, m_q_a_norm_g, m_w_q_b, m_kv_a_norm_g, m_w_kv_b, m_mla_norm_g, m_w_out, m_norm2_g, m_w_gate, m_w_up, m_w_down, m_final_norm_g, v_norm1_g, v_w_in, v_lb_logits, v_hgrn_norm_g, v_q_a_norm_g, v_w_q_b, v_kv_a_norm_g, v_w_kv_b, v_mla_norm_g, v_w_out, v_norm2_g, v_w_gate, v_w_up, v_w_down, v_final_norm_g):
    names = ("norm1_g", "w_in", "lb_logits", "hgrn_norm_g", "q_a_norm_g", "w_q_b", "kv_a_norm_g", "w_kv_b", "mla_norm_g",
             "w_out", "norm2_g", "w_gate", "w_up", "w_down", "final_norm_g")
    w = dict(zip(names, (norm1_g, w_in, lb_logits, hgrn_norm_g, q_a_norm_g, w_q_b, kv_a_norm_g, w_kv_b, mla_norm_g,
                         w_out, norm2_g, w_gate, w_up, w_down, final_norm_g)))
    m = dict(zip(names, (m_norm1_g, m_w_in, m_lb_logits, m_hgrn_norm_g, m_q_a_norm_g, m_w_q_b, m_kv_a_norm_g, m_w_kv_b,
                         m_mla_norm_g, m_w_out, m_norm2_g, m_w_gate, m_w_up, m_w_down, m_final_norm_g)))
    v = dict(zip(names, (v_norm1_g, v_w_in, v_lb_logits, v_hgrn_norm_g, v_q_a_norm_g, v_w_q_b, v_kv_a_norm_g, v_w_kv_b,
                         v_mla_norm_g, v_w_out, v_norm2_g, v_w_gate, v_w_up, v_w_down, v_final_norm_g)))
    matrices = _EARLY + _LATE
    chip = 2 * lax.axis_index("x") + lax.axis_index("y")

    w16 = {n: w[n][0].astype(BF16) for n in matrices}
    lb8 = _pad_rows(lb_logits.reshape(4, 128), 8)
    *early, lball = _gather_shards([w16[n] for n in _EARLY] + [lb8])
    lbl = lball[:, :4].reshape(N_CHIPS, 2, 2, 128).transpose(1, 2, 0, 3).reshape(2, 2, A_WIDTH)

    loss_part, grad_x, g, recv = _local_step(
        x, loss_target, lbl, norm1_g, hgrn_norm_g, q_a_norm_g, kv_a_norm_g, mla_norm_g, norm2_g, final_norm_g[None, :],
        *(_join_shards(n, a) for n, a in zip(_EARLY, early)), [w16[n] for n in _LATE])
    loss = lax.psum(loss_part, ("x", "y", "c"))

    small = _pad_rows(jnp.concatenate([g[n].reshape(-1) for n, _ in _SMALL]).reshape(-1, 128), SMALL_ROWS)
    *recv_early, small_all = _scatter_grads([_segments(n, g[n]).astype(BF16) for n in _EARLY], small)
    recv.update(zip(_EARLY, recv_early))
    parts = [_sum_segments(_own_segment(n, g[n], chip), recv[n], f"sum_{n}") for n in matrices]
    sibs = _swap_sibling(parts)
    small_sum = _sum_devices(small_all).reshape(-1)

    out = {}
    for n, part, sib in zip(matrices, parts, sibs):
        res = _adamw(w[n][0], m[n][0], v[n][0], part, sib, f"adamw_{n}")
        out[n] = tuple(r[None] for r in res)

    small_g, off = {}, 0
    for n, size in _SMALL:
        small_g[n] = small_sum[off:off + size]
        off += size
    small_g["lb_logits"] = lax.dynamic_slice_in_dim(small_g["lb_logits"].reshape(2, 2, A_WIDTH), chip * 128, 128, axis=2)
    small_names = tuple(n for n, _ in _SMALL)

    def pack(d):
        return _pad_rows(jnp.concatenate([d[n].reshape(-1) for n in small_names]).reshape(-1, 128), _UPDATE_ROWS)

    res = _adamw(pack(w), pack(m), pack(v), pack(small_g), None, "adamw_small")
    off = 0
    flat = [r.reshape(-1) for r in res]
    for n in small_names:
        size = w[n].size
        out[n] = tuple(f[off:off + size].reshape(w[n].shape) for f in flat)
        off += size

    return (loss, grad_x) + tuple(out[n][i] for i in range(4) for n in names)
```

```python
import functools

import jax
import jax.numpy as jnp
from jax import lax
from jax.experimental import pallas as pl
from jax.experimental.pallas import tpu as pltpu

F32 = jnp.float32
BF16 = jnp.bfloat16

D_MODEL = 1024
A_WIDTH = 512
HEAD_PAIRS = 4
CHUNK = 64
B_HEADS = 4
B_NOPE = 128
B_ROPE = 64
B_V = 128
QK_PAD = 256
Q_LORA = 384
KV_LORA = 256
D_FF = 2816
D_IN = 3264
D_IN_PAD = 3328
IN_WIDTHS = (512, 512, 512, 512, 512, Q_LORA, KV_LORA, 128)
ROPE_THETA = 10000.0
EPS = 1e-6
ATTN_SCALE = (B_NOPE + B_ROPE) ** -0.5
LOG2E = 1.4426950408889634
SCALE_LOG2E = ATTN_SCALE * LOG2E

ADAM_LR = 0.001
ADAM_B1 = 0.9
ADAM_B2 = 0.999
ADAM_EPS = 1e-08
ADAM_WD = 0.01
ADAM_STEP = 10

VMEM_LIMIT_BYTES = 60 * 1024 * 1024
N_CHIPS = 4
SMALL_ROWS = 56


def _params(n_axes):
    return pltpu.CompilerParams(dimension_semantics=("arbitrary",) * n_axes,
                                vmem_limit_bytes=VMEM_LIMIT_BYTES)


def _dot(a, b):
    return jnp.dot(a, b, preferred_element_type=F32)


def _dot_nt(a, b):
    return lax.dot_general(a, b, (((1,), (1,)), ((), ())), preferred_element_type=F32)


def _dot_tn(a, b):
    return lax.dot_general(a, b, (((0,), (0,)), ((), ())), preferred_element_type=F32)


def _split3(x):
    x1 = x.astype(BF16)
    r = x - x1.astype(F32)
    x2 = r.astype(BF16)
    x3 = (r - x2.astype(F32)).astype(BF16)
    return x1, x2, x3


def _exact_left(m16, x):
    x1, x2, x3 = _split3(x)
    return _dot(m16, x1) + _dot(m16, x2) + _dot(m16, x3)


def _exact_right(x, m16):
    x1, x2, x3 = _split3(x)
    return _dot(x1, m16) + _dot(x2, m16) + _dot(x3, m16)


def _iota2(shape, dim):
    return lax.broadcasted_iota(jnp.int32, shape, dim)


def _sigmoid(x):
    return jax.nn.sigmoid(x)


def _pick(dim, cap, mult=128):
    if dim <= cap:
        return dim
    best = None
    for d in range(mult, cap + 1, mult):
        if dim % d == 0:
            best = d
    assert best is not None, (dim, cap, mult)
    return best


def _row_tile(t, cap=256):
    return _pick(t, cap, 8)


def _full(shape, single=False):
    if single:
        return pl.BlockSpec(shape, lambda *_: (0,) * len(shape), pipeline_mode=pl.Buffered(1))
    return pl.BlockSpec(shape, lambda *_: (0,) * len(shape))


def _rows(tm, width):
    return pl.BlockSpec((tm, width), lambda i: (i, 0))


def _acc_rows(ref, val, first):
    s = jnp.sum(val, axis=0, keepdims=True)

    @pl.when(first)
    def _():
        ref[...] = s

    @pl.when(jnp.logical_not(first))
    def _():
        ref[...] += s


def _in_fwd(x, g1, w_in_p):
    t = x.shape[0]
    tm = _row_tile(t)

    def body(x_ref, g_ref, w_ref, h_ref, *outs):
        xv = x_ref[...]
        r = lax.rsqrt(jnp.mean(xv * xv, axis=-1, keepdims=True) + EPS)
        h = ((xv * r) * g_ref[...]).astype(BF16)
        h_ref[...] = h
        off = 0
        for o_ref, w in zip(outs, IN_WIDTHS):
            o_ref[...] = _dot_nt(h, w_ref[off:off + w, :])
            off += w

    return pl.pallas_call(
        body, name="in_fwd", grid=(t // tm,),
        in_specs=[_rows(tm, D_MODEL), _full((1, D_MODEL)), _full((D_IN_PAD, D_MODEL))],
        out_specs=[_rows(tm, D_MODEL)] + [_rows(tm, w) for w in IN_WIDTHS],
        out_shape=[jax.ShapeDtypeStruct((t, D_MODEL), BF16)]
        + [jax.ShapeDtypeStruct((t, w), F32) for w in IN_WIDTHS],
        compiler_params=_params(1),
    )(x, g1, w_in_p)


def _in_bwd(dq_f, dq_r, dv_f, dv_r, dz_f, dz_r, dhg, hq, dcq, dckv, dkr, dx2, x, g1, w_in_p):
    t = x.shape[0]
    tm = _row_tile(t)

    def body(dqf_ref, dqr_ref, dvf_ref, dvr_ref, dzf_ref, dzr_ref, dhg_ref, hq_ref, dcq_ref, dckv_ref,
             dkr_ref, dx2_ref, x_ref, g_ref, w_ref, dx_ref, dp_ref, dg_ref):
        hqv = hq_ref[...]
        sg = _sigmoid(hqv)
        dhq = (dqf_ref[...] + dqr_ref[...]) * (sg * (1.0 + hqv * (1.0 - sg)))
        pieces = (dhq, dvf_ref[...] + dvr_ref[...], dzf_ref[...], dzr_ref[...], dhg_ref[...],
                  dcq_ref[...], dckv_ref[...], dkr_ref[...])
        dh = None
        off = 0
        for p, w in zip(pieces, IN_WIDTHS):
            p16 = p.astype(BF16)
            dp_ref[:, off:off + w] = p16
            part = _dot(p16, w_ref[off:off + w, :])
            dh = part if dh is None else dh + part
            off += w
        xv = x_ref[...]
        r = lax.rsqrt(jnp.mean(xv * xv, axis=-1, keepdims=True) + EPS)
        xh = xv * r
        _acc_rows(dg_ref, dh * xh, pl.program_id(0) == 0)
        dxh = dh * g_ref[...]
        dx_ref[...] = dx2_ref[...] + r * (dxh - xh * jnp.mean(dxh * xh, axis=-1, keepdims=True))

    a512 = _rows(tm, A_WIDTH)
    return pl.pallas_call(
        body, name="in_bwd", grid=(t // tm,),
        in_specs=[a512] * 8 + [_rows(tm, Q_LORA), _rows(tm, KV_LORA), _rows(tm, 128), _rows(tm, D_MODEL),
                               _rows(tm, D_MODEL), _full((1, D_MODEL)), _full((D_IN_PAD, D_MODEL))],
        out_specs=[_rows(tm, D_MODEL), _rows(tm, D_IN_PAD), _full((1, D_MODEL))],
        out_shape=[jax.ShapeDtypeStruct((t, D_MODEL), F32), jax.ShapeDtypeStruct((t, D_IN_PAD), BF16),
                   jax.ShapeDtypeStruct((1, D_MODEL), F32)],
        compiler_params=_params(1),
    )(dq_f, dq_r, dv_f, dv_r, dz_f, dz_r, dhg, hq, dcq, dckv, dkr, dx2, x, g1, w_in_p)


def _lower_bound(lbl_ref, direction):
    l0 = lbl_ref[direction, 0:1, :]
    l1 = lbl_ref[direction, 1:2, :]
    m = jnp.maximum(l0, l1)
    e0 = jnp.exp(l0 - m)
    e1 = jnp.exp(l1 - m)
    return e0 / (e0 + e1)


def _hgrn_consts(rb, reverse):
    row = _iota2((rb, rb), 0)
    col = _iota2((rb, rb), 1)
    same = (row // CHUNK) == (col // CHUNK)
    tri = jnp.logical_and(same, (col >= row) if reverse else (col <= row))
    tri_t = jnp.logical_and(same, (col <= row) if reverse else (col >= row))
    r128 = _iota2((128, 128), 0)
    c128 = _iota2((128, 128), 1)
    bd = (r128 < 64) == (c128 < 64)
    lane = _iota2((1, 128), 1)
    m0 = (lane < 64).astype(F32)
    return tri, tri_t, bd, (m0, 1.0 - m0)


def _per_chunk(x, fn):
    n = x.shape[0] // CHUNK
    return jnp.concatenate([jnp.broadcast_to(fn(x[c * CHUNK:(c + 1) * CHUNK]), (CHUNK, x.shape[1])) for c in range(n)],
                           axis=0)


def _chunk_cumsum(x, reverse):
    rb = x.shape[0]
    pos = _iota2(x.shape, 0) % CHUNK
    step = 1
    while step < CHUNK:
        if reverse:
            x = x + jnp.where(pos < CHUNK - step, pltpu.roll(x, rb - step, 0), 0.0)
        else:
            x = x + jnp.where(pos >= step, pltpu.roll(x, step, 0), 0.0)
        step *= 2
    return x


def _hgrn_block(z, hqv, lb, reverse):
    sig = _sigmoid(z)
    sn = _sigmoid(-z)
    q = hqv * _sigmoid(hqv)
    f = lb + (1.0 - lb) * sig
    k = (1.0 - lb) * sn
    lf = jnp.log(f)
    cum = _chunk_cumsum(lf, reverse)
    last = _per_chunk(cum, (lambda a: a[0:1]) if reverse else (lambda a: a[CHUNK - 1:CHUNK]))
    e_neg = jnp.exp(-cum)
    e_end = jnp.exp(last - cum)
    a = jnp.exp(cum)
    return dict(sig=sig, sn=sn, q=q, f=f, k=k, a=a, e_neg=e_neg, e_end=e_end,
                q_dec=q * a, k_inv=k * e_neg, k_end=k * e_end, d=jnp.exp(last))


def _hgrn_dims(t, n_batch):
    s = t // n_batch
    rb = _pick(s, 256, CHUNK)
    return s, rb, s // rb, rb // CHUNK


def _hgrn_fwd(hq, hi, hf, lbl, *, n_batch, direction):
    t = hq.shape[0]
    reverse = direction == 1
    s, rb, nb, nc = _hgrn_dims(t, n_batch)

    def tmap(b, j):
        return (b * nb + ((nb - 1 - j) if reverse else j), 0)

    def smap(b, j):
        return (b * nb + ((nb - 1 - j) if reverse else j), 0, 0, 0)

    def body(hq_ref, hi_ref, hf_ref, lbl_ref, o_ref, st_ref, st_scr):
        @pl.when(pl.program_id(1) == 0)
        def _():
            st_scr[...] = jnp.zeros_like(st_scr)

        lb_all = _lower_bound(lbl_ref, direction)
        tri, _, bd, masks = _hgrn_consts(rb, reverse)
        order = range(nc - 1, -1, -1) if reverse else range(nc)

        pairs = [slice(p * 128, (p + 1) * 128) for p in range(HEAD_PAIRS)]
        chunks = [slice(c * CHUNK, (c + 1) * CHUNK) for c in range(nc)]
        w = _hgrn_block(hf_ref[...], hq_ref[...], lb_all, reverse)
        v16 = hi_ref[...].astype(BF16)
        qd16 = w["q_dec"].astype(BF16)
        ki16 = w["k_inv"].astype(BF16)
        ke16 = w["k_end"].astype(BF16)
        sc = [[_dot_nt((w["q_dec"][:, ls] * mh).astype(BF16), ki16[:, ls]) for mh in masks] for ls in pairs]
        pv = [[_dot(jnp.where(tri, s_e, 0.0).astype(BF16), v16[:, ls]) for s_e in sc_p] for sc_p, ls in zip(sc, pairs)]
        o_intra = [pv_p[0] * masks[0] + pv_p[1] * masks[1] for pv_p in pv]
        ut = [[jnp.where(bd, _dot_tn(v16[rs, ls], ke16[rs, ls]), 0.0) for ls in pairs] for rs in chunks]
        st = [st_scr[p] for p in range(HEAD_PAIRS)]
        for c in order:
            rs = chunks[c]
            inter = [_dot_nt(qd16[rs, ls], st[p].astype(BF16)) for p, ls in enumerate(pairs)]
            for p, ls in enumerate(pairs):
                o_ref[rs, ls] = o_intra[p][rs] + inter[p]
                st_ref[c, p] = st[p]
                st[p] = st[p] * w["d"][c * CHUNK:c * CHUNK + 1, ls] + ut[c][p]
        for p in range(HEAD_PAIRS):
            st_scr[p] = st[p]

    blk = pl.BlockSpec((rb, A_WIDTH), tmap)
    return pl.pallas_call(
        body, name=f"hgrn_fwd_{direction}", grid=(n_batch, nb),
        in_specs=[blk, blk, blk, _full((2, 2, A_WIDTH))],
        out_specs=[blk, pl.BlockSpec((nc, HEAD_PAIRS, 128, 128), smap)],
        out_shape=[jax.ShapeDtypeStruct((t, A_WIDTH), F32),
                   jax.ShapeDtypeStruct((t // CHUNK, HEAD_PAIRS, 128, 128), F32)],
        scratch_shapes=[pltpu.VMEM((HEAD_PAIRS, 128, 128), F32)],
        compiler_params=_params(2),
    )(hq, hi, hf, lbl)


def _hgrn_bwd(hq, hi, hf, do, st, lbl, *, n_batch, direction):
    t = hq.shape[0]
    reverse = direction == 1
    s, rb, nb, nc = _hgrn_dims(t, n_batch)

    def tmap(b, j):
        return (b * nb + (j if reverse else (nb - 1 - j)), 0)

    def smap(b, j):
        return (b * nb + (j if reverse else (nb - 1 - j)), 0, 0, 0)

    def body(hq_ref, hi_ref, hf_ref, do_ref, st_ref, lbl_ref, dq_ref, dv_ref, dz_ref, dl_ref, g_scr, dlb_scr):
        b = pl.program_id(0)
        j = pl.program_id(1)

        @pl.when(jnp.logical_and(b == 0, j == 0))
        def _():
            dlb_scr[...] = jnp.zeros_like(dlb_scr)

        @pl.when(j == 0)
        def _():
            g_scr[...] = jnp.zeros_like(g_scr)

        lb_all = _lower_bound(lbl_ref, direction)
        tri, tri_t, bd, masks = _hgrn_consts(rb, reverse)
        order = range(nc) if reverse else range(nc - 1, -1, -1)

        pairs = [slice(p * 128, (p + 1) * 128) for p in range(HEAD_PAIRS)]
        chunks = [slice(c * CHUNK, (c + 1) * CHUNK) for c in range(nc)]

        def lanes(per_pair):
            return jnp.concatenate(per_pair, axis=1)

        w = _hgrn_block(hf_ref[...], hq_ref[...], lb_all, reverse)
        dov = do_ref[...]
        v16 = hi_ref[...].astype(BF16)
        do16 = dov.astype(BF16)
        qd16 = w["q_dec"].astype(BF16)
        ki16 = w["k_inv"].astype(BF16)
        ke16 = w["k_end"].astype(BF16)
        qm16 = [[(w["q_dec"][:, ls] * mh).astype(BF16) for mh in masks] for ls in pairs]
        dom16 = [[(dov[:, ls] * mh).astype(BF16) for mh in masks] for ls in pairs]
        heads = [(p, e) for p in range(HEAD_PAIRS) for e in range(2)]
        dp = {(p, e): _dot_nt(dom16[p][e], v16[:, pairs[p]]) for p, e in heads}
        pm_t = {(p, e): _dot_nt(ki16[:, pairs[p]], qm16[p][e]) for p, e in heads}
        dp_t = {(p, e): _dot_nt(v16[:, pairs[p]], dom16[p][e]) for p, e in heads}
        dp = {h: jnp.where(tri, a, 0.0).astype(BF16) for h, a in dp.items()}
        pm_t = {h: jnp.where(tri_t, a, 0.0).astype(BF16) for h, a in pm_t.items()}
        dp_t = {h: jnp.where(tri_t, a, 0.0).astype(BF16) for h, a in dp_t.items()}
        dv_e = {(p, e): _dot(pm_t[p, e], do16[:, pairs[p]]) for p, e in heads}
        dq_e = {(p, e): _dot(dp[p, e], ki16[:, pairs[p]]) for p, e in heads}
        dk_e = {(p, e): _dot(dp_t[p, e], qd16[:, pairs[p]]) for p, e in heads}
        st = [[st_ref[c, p] for p in range(HEAD_PAIRS)] for c in range(nc)]
        dq_x = [[_dot(do16[rs, ls], st[c][p].astype(BF16)) for p, ls in enumerate(pairs)] for c, rs in enumerate(chunks)]
        gq = [[jnp.where(bd, _dot_tn(do16[rs, ls], qd16[rs, ls]), 0.0) for ls in pairs] for rs in chunks]
        g = [g_scr[p] for p in range(HEAD_PAIRS)]
        dk_end, dv_x, dd = [None] * nc, [None] * nc, [None] * nc
        for c in order:
            rs = chunks[c]
            g16 = [a.astype(BF16) for a in g]
            dk_end[c] = lanes([_dot(v16[rs, ls], g16[p]) for p, ls in enumerate(pairs)])
            dv_x[c] = lanes([_dot_nt(ke16[rs, ls], g16[p]) for p, ls in enumerate(pairs)])
            dd[c] = jnp.broadcast_to(lanes([jnp.sum(g[p] * st[c][p], axis=0, keepdims=True) for p in range(HEAD_PAIRS)]),
                                     (CHUNK, A_WIDTH))
            for p, ls in enumerate(pairs):
                g[p] = g[p] * w["d"][c * CHUNK:c * CHUNK + 1, ls] + gq[c][p]
        for p in range(HEAD_PAIRS):
            g_scr[p] = g[p]

        def both_heads(d):
            return lanes([d[p, 0] * masks[0] + d[p, 1] * masks[1] for p in range(HEAD_PAIRS)])

        dq_dec = both_heads(dq_e) + jnp.concatenate([lanes(a) for a in dq_x], axis=0)
        dk_inv = both_heads(dk_e)
        dk_end = jnp.concatenate(dk_end, axis=0)
        dv = both_heads(dv_e) + jnp.concatenate(dv_x, axis=0)
        dd = jnp.concatenate(dd, axis=0)
        dke = dk_end * w["k_end"]
        dcum = dq_dec * w["q_dec"] - dk_inv * w["k_inv"] - dke
        dk = dk_inv * w["e_neg"] + dk_end * w["e_end"]
        dlast = _per_chunk(dke, lambda a: jnp.sum(a, axis=0, keepdims=True)) + dd * w["d"]
        dlf = _chunk_cumsum(dcum, not reverse) + dlast
        tt = dlf / w["f"] - dk
        dq_ref[...] = dq_dec * w["a"]
        dv_ref[...] = dv
        dz_ref[...] = ((1.0 - lb_all) * w["sig"] * w["sn"] * tt).astype(BF16)
        dlb_scr[...] += jnp.sum(w["sn"] * tt, axis=0, keepdims=True)

        @pl.when(jnp.logical_and(b == pl.num_programs(0) - 1, j == pl.num_programs(1) - 1))
        def _():
            d0 = dlb_scr[...] * lb_all * (1.0 - lb_all)
            dl_ref[0:1, :] = d0
            dl_ref[1:2, :] = -d0

    blk = pl.BlockSpec((rb, A_WIDTH), tmap)
    return pl.pallas_call(
        body, name=f"hgrn_bwd_{direction}", grid=(n_batch, nb),
        in_specs=[blk, blk, blk, blk, pl.BlockSpec((nc, HEAD_PAIRS, 128, 128), smap), _full((2, 2, A_WIDTH))],
        out_specs=[blk, blk, blk, _full((2, A_WIDTH))],
        out_shape=[jax.ShapeDtypeStruct((t, A_WIDTH), F32)] * 2
        + [jax.ShapeDtypeStruct((t, A_WIDTH), BF16), jax.ShapeDtypeStruct((2, A_WIDTH), F32)],
        scratch_shapes=[pltpu.VMEM((HEAD_PAIRS, 128, 128), F32), pltpu.VMEM((1, A_WIDTH), F32)],
        compiler_params=_params(2),
    )(hq, hi, hf, do, st, lbl)


def _swap_rope_halves(x):
    lane = _iota2(x.shape, 1)
    return jnp.where(lane < 32, pltpu.roll(x, 96, 1), pltpu.roll(x, 32, 1))


def _rms_fwd(xv, g):
    r = lax.rsqrt(jnp.mean(xv * xv, axis=-1, keepdims=True) + EPS)
    return (xv * r) * g


def _rms_bwd(dy, xv, g):
    r = lax.rsqrt(jnp.mean(xv * xv, axis=-1, keepdims=True) + EPS)
    xh = xv * r
    dxh = dy * g
    return r * (dxh - xh * jnp.mean(dxh * xh, axis=-1, keepdims=True)), dy * xh


def _mla_prep(cq, ckv, kr, g_qa, g_kva, w_q_p, w_kv_p, cosx, sinx, *, n_batch):
    t = cq.shape[0]
    s = t // n_batch
    tm = _row_tile(s)
    nt = s // tm

    def body(cq_ref, ckv_ref, kr_ref, gq_ref, gkv_ref, wq_ref, wkv_ref, cos_ref, sin_ref,
             q_ref, k_ref, v_ref, cqn_ref, ckvn_ref):
        cos, sin = cos_ref[...], sin_ref[...]
        cqn = _rms_fwd(cq_ref[...], gq_ref[...]).astype(BF16)
        ckvn = _rms_fwd(ckv_ref[...], gkv_ref[...]).astype(BF16)
        cqn_ref[...] = cqn
        ckvn_ref[...] = ckvn
        krv = kr_ref[...]
        kr_roped = (krv * cos + _swap_rope_halves(krv) * sin).astype(BF16)
        for h in range(B_HEADS):
            o = h * QK_PAD
            q_ref[:, o:o + 128] = _dot_nt(cqn, wq_ref[o:o + 128, :]).astype(BF16)
            qr = _dot_nt(cqn, wq_ref[o + 128:o + 256, :])
            q_ref[:, o + 128:o + 256] = (qr * cos + _swap_rope_halves(qr) * sin).astype(BF16)
            k_ref[:, o:o + 128] = _dot(ckvn, wkv_ref[:, h * 128:(h + 1) * 128]).astype(BF16)
            k_ref[:, o + 128:o + 256] = kr_roped
        v_ref[...] = _dot(ckvn, wkv_ref[:, 512:1024]).astype(BF16)

    tab = pl.BlockSpec((tm, 128), lambda i: (i % nt, 0))
    return pl.pallas_call(
        body, name="mla_prep", grid=(t // tm,),
        in_specs=[_rows(tm, Q_LORA), _rows(tm, KV_LORA), _rows(tm, 128), _full((1, Q_LORA)), _full((1, KV_LORA)),
                  _full((1024, Q_LORA)), _full((KV_LORA, 1024)), tab, tab],
        out_specs=[_rows(tm, 1024), _rows(tm, 1024), _rows(tm, 512), _rows(tm, Q_LORA), _rows(tm, KV_LORA)],
        out_shape=[jax.ShapeDtypeStruct((t, 1024), BF16), jax.ShapeDtypeStruct((t, 1024), BF16),
                   jax.ShapeDtypeStruct((t, 512), BF16), jax.ShapeDtypeStruct((t, Q_LORA), BF16),
                   jax.ShapeDtypeStruct((t, KV_LORA), BF16)],
        compiler_params=_params(1),
    )(cq, ckv, kr, g_qa, g_kva, w_q_p, w_kv_p, cosx, sinx)


def _mla_prep_bwd(dqt, dk, dv, cq, ckv, g_qa, g_kva, w_q_p, w_kv_p, cosx, sinx, *, n_batch):
    t = cq.shape[0]
    s = t // n_batch
    tm = _row_tile(s)
    nt = s // tm

    def body(dqt_ref, dk_ref, dv_ref, cq_ref, ckv_ref, gq_ref, gkv_ref, wq_ref, wkv_ref, cos_ref, sin_ref,
             dcq_ref, dckv_ref, dkr_ref, dqp_ref, dkvp_ref, dgq_ref, dgkv_ref):
        cos, sin = cos_ref[...], sin_ref[...]
        first = pl.program_id(0) == 0

        def unrope(d):
            return d * cos + _swap_rope_halves(d * sin)

        dcqn = None
        dkr = None
        dckvn = None
        for h in range(B_HEADS):
            o = h * QK_PAD
            dq_h = jnp.transpose(dqt_ref[o:o + QK_PAD, :])
            dqn16 = dq_h[:, 0:128].astype(BF16)
            dqr16 = unrope(dq_h[:, 128:256]).astype(BF16)
            dqp_ref[:, o:o + 128] = dqn16
            dqp_ref[:, o + 128:o + 256] = dqr16
            part = _dot(dqn16, wq_ref[o:o + 128, :]) + _dot(dqr16, wq_ref[o + 128:o + 256, :])
            dcqn = part if dcqn is None else dcqn + part
            dkn16 = dk_ref[:, o:o + 128].astype(BF16)
            dkvp_ref[:, h * 128:(h + 1) * 128] = dkn16
            part = _dot_nt(dkn16, wkv_ref[:, h * 128:(h + 1) * 128])
            dckvn = part if dckvn is None else dckvn + part
            kr_part = dk_ref[:, o + 128:o + 256]
            dkr = kr_part if dkr is None else dkr + kr_part
        dv16 = dv_ref[...].astype(BF16)
        dkvp_ref[:, 512:1024] = dv16
        dckvn = dckvn + _dot_nt(dv16, wkv_ref[:, 512:1024])
        dkr_ref[...] = unrope(dkr).astype(BF16)
        dcq, dgq = _rms_bwd(dcqn, cq_ref[...], gq_ref[...])
        dckv, dgkv = _rms_bwd(dckvn, ckv_ref[...], gkv_ref[...])
        dcq_ref[...] = dcq.astype(BF16)
        dckv_ref[...] = dckv.astype(BF16)
        _acc_rows(dgq_ref, dgq, first)
        _acc_rows(dgkv_ref, dgkv, first)

    tab = pl.BlockSpec((tm, 128), lambda i: (i % nt, 0))
    return pl.pallas_call(
        body, name="mla_prep_bwd", grid=(t // tm,),
        in_specs=[pl.BlockSpec((1024, tm), lambda i: (0, i)), _rows(tm, 1024), _rows(tm, 512), _rows(tm, Q_LORA),
                  _rows(tm, KV_LORA),
                  _full((1, Q_LORA)), _full((1, KV_LORA)), _full((1024, Q_LORA)), _full((KV_LORA, 1024)), tab, tab],
        out_specs=[_rows(tm, Q_LORA), _rows(tm, KV_LORA), _rows(tm, 128), _rows(tm, 1024), _rows(tm, 1024),
                   _full((1, Q_LORA)), _full((1, KV_LORA))],
        out_shape=[jax.ShapeDtypeStruct((t, Q_LORA), BF16), jax.ShapeDtypeStruct((t, KV_LORA), BF16),
                   jax.ShapeDtypeStruct((t, 128), BF16), jax.ShapeDtypeStruct((t, 1024), BF16),
                   jax.ShapeDtypeStruct((t, 1024), BF16), jax.ShapeDtypeStruct((1, Q_LORA), F32),
                   jax.ShapeDtypeStruct((1, KV_LORA), F32)],
        compiler_params=_params(1),
    )(dqt, dk, dv, cq, ckv, g_qa, g_kva, w_q_p, w_kv_p, cosx, sinx)


def _attn_dims(t, n_batch):
    s = t // n_batch
    tq = _pick(s, 256, 128)
    return s, tq, s // tq


def _grid_ends(n_axes):
    ids = [pl.program_id(a) for a in range(n_axes)]
    first = functools.reduce(jnp.logical_and, [i == 0 for i in ids])
    last = functools.reduce(jnp.logical_and, [i == pl.num_programs(a) - 1 for a, i in enumerate(ids)])
    return first, last


def _attn_fwd(q, k, v, wsrcs, *, n_batch):
    t = q.shape[0]
    s, tq, nq = _attn_dims(t, n_batch)
    nw = len(wsrcs)

    def body(q_ref, k_ref, v_ref, *refs):
        w_refs, (o_ref, lse_ref), wall_refs = refs[:nw], refs[nw:nw + 2], refs[nw + 2:2 * nw + 2]
        send_sems, recv_sems, local_sems = refs[2 * nw + 2:]
        first, last = _grid_ends(3)

        @pl.when(first)
        def _():
            for i in range(nw):
                _gather_start(w_refs[i], wall_refs[i], send_sems, recv_sems, local_sems.at[i], 3 * i)

        @pl.when(last)
        def _():
            for i in range(nw):
                _gather_wait(w_refs[i], wall_refs[i], send_sems, recv_sems, local_sems.at[i], 3 * i)

        raw = _dot_nt(q_ref[...], k_ref[...])
        m = jnp.max(raw, axis=-1, keepdims=True)
        p = jnp.exp2((raw - m) * SCALE_LOG2E)
        l = jnp.sum(p, axis=-1, keepdims=True)
        o_ref[...] = _dot(p.astype(BF16), v_ref[...]) / l
        lse2 = m * SCALE_LOG2E + jnp.log(l) * LOG2E
        lse_ref[...] = jnp.transpose(jnp.broadcast_to(lse2, (tq, 128)))[0:1, :]

    return pl.pallas_call(
        body, name="attn_fwd", grid=(n_batch, B_HEADS, nq),
        in_specs=[pl.BlockSpec((tq, QK_PAD), lambda b, h, i: (b * nq + i, h)),
                  pl.BlockSpec((s, QK_PAD), lambda b, h, i: (b, h)),
                  pl.BlockSpec((s, B_V), lambda b, h, i: (b, h))] + [_HBM] * nw,
        out_specs=[pl.BlockSpec((tq, B_V), lambda b, h, i: (b * nq + i, h)),
                   pl.BlockSpec((None, 1, tq), lambda b, h, i: (h, 0, b * nq + i))] + [_HBM] * nw,
        out_shape=[jax.ShapeDtypeStruct((t, B_HEADS * B_V), F32), jax.ShapeDtypeStruct((B_HEADS, 1, t), F32)]
        + [jax.ShapeDtypeStruct((N_CHIPS,) + w.shape, w.dtype) for w in wsrcs],
        scratch_shapes=[pltpu.SemaphoreType.DMA((3 * nw,)), pltpu.SemaphoreType.DMA((3 * nw,)),
                        pltpu.SemaphoreType.DMA((nw,))],
        compiler_params=_params(3),
    )(q, k, v, *wsrcs)


def _attn_bwd(q, k, v, do16, lse, delta, gsegs, *, n_batch):
    t = q.shape[0]
    s = t // n_batch
    tk = _pick(s, 512, 128)
    nk = s // tk
    ng = len(gsegs)

    def body(q_ref, k_ref, v_ref, do_ref, lse_ref, dl_ref, *refs):
        g_refs, (dqt_ref, dk_ref, dv_ref), recv_refs = refs[:ng], refs[ng:ng + 3], refs[ng + 3:2 * ng + 3]
        send_sems, recv_sems = refs[2 * ng + 3:]
        first, last = _grid_ends(3)

        @pl.when(first)
        def _():
            for i in range(ng):
                _scatter_start(g_refs[i], recv_refs[i], send_sems, recv_sems, 3 * i)

        @pl.when(last)
        def _():
            for i in range(ng):
                _scatter_wait(g_refs[i], recv_refs[i], send_sems, recv_sems, 3 * i)

        j = pl.program_id(2)
        qv, kv, dov = q_ref[...], k_ref[...], do_ref[...]
        pt = jnp.exp2(_dot_nt(kv, qv) * SCALE_LOG2E - lse_ref[...])
        dv_ref[...] = _dot(pt.astype(BF16), dov)
        dpt = _dot_nt(v_ref[...], dov)
        dst = (pt * (dpt - dl_ref[...])).astype(BF16)
        dk_ref[...] = _dot(dst, qv) * ATTN_SCALE
        part = _dot_tn(kv, dst)

        @pl.when(j == 0)
        def _():
            dqt_ref[...] = part

        @pl.when(j != 0)
        def _():
            dqt_ref[...] += part

        @pl.when(j == nk - 1)
        def _():
            dqt_ref[...] = dqt_ref[...] * ATTN_SCALE

    row = pl.BlockSpec((None, 1, s), lambda b, h, j: (h, 0, b))
    return pl.pallas_call(
        body, name="attn_bwd", grid=(n_batch, B_HEADS, nk),
        in_specs=[pl.BlockSpec((s, QK_PAD), lambda b, h, j: (b, h)),
                  pl.BlockSpec((tk, QK_PAD), lambda b, h, j: (b * nk + j, h)),
                  pl.BlockSpec((tk, B_V), lambda b, h, j: (b * nk + j, h)),
                  pl.BlockSpec((s, B_V), lambda b, h, j: (b, h)), row, row] + [_HBM] * ng,
        out_specs=[pl.BlockSpec((QK_PAD, s), lambda b, h, j: (h, b)),
                   pl.BlockSpec((tk, QK_PAD), lambda b, h, j: (b * nk + j, h)),
                   pl.BlockSpec((tk, B_V), lambda b, h, j: (b * nk + j, h))] + [_HBM] * ng,
        out_shape=[jax.ShapeDtypeStruct((B_HEADS * QK_PAD, t), F32), jax.ShapeDtypeStruct((t, B_HEADS * QK_PAD), F32),
                   jax.ShapeDtypeStruct((t, B_HEADS * B_V), F32)]
        + [jax.ShapeDtypeStruct((3,) + g.shape[1:], g.dtype) for g in gsegs],
        scratch_shapes=[pltpu.SemaphoreType.DMA((3 * ng,)), pltpu.SemaphoreType.DMA((3 * ng,))],
        compiler_params=_params(3),
    )(q, k, v, do16, lse, delta, *gsegs)


def _group_ones16():
    r = _iota2((A_WIDTH, A_WIDTH), 0) // 64
    c = _iota2((A_WIDTH, A_WIDTH), 1) // 64
    return (r == c).astype(BF16)


def _head_rms(o, ones16):
    return lax.rsqrt(_exact_right(o * o, ones16) * (1.0 / 64.0) + EPS)


def _out_fwd(o_f, o_r, hg, o_attn, x, g_hn, g_mla, w_out, g2):
    t = x.shape[0]
    tm = _row_tile(t)

    def body(of_ref, or_ref, hg_ref, oa_ref, x_ref, ghn_ref, gm_ref, w_ref, g2_ref, y_ref, x2_ref, h2_ref):
        ones16 = _group_ones16()
        o = of_ref[...] + or_ref[...]
        hgv = hg_ref[...]
        ya = ((o * _head_rms(o, ones16)) * ghn_ref[...]) * (hgv * _sigmoid(hgv))
        yb = _rms_fwd(oa_ref[...], gm_ref[...])
        ya16, yb16 = ya.astype(BF16), yb.astype(BF16)
        y_ref[:, 0:A_WIDTH] = ya16
        y_ref[:, A_WIDTH:D_MODEL] = yb16
        x2 = x_ref[...] + _dot(ya16, w_ref[0:A_WIDTH, :]) + _dot(yb16, w_ref[A_WIDTH:D_MODEL, :])
        x2_ref[...] = x2
        h2_ref[...] = _rms_fwd(x2, g2_ref[...]).astype(BF16)

    a512 = _rows(tm, A_WIDTH)
    return pl.pallas_call(
        body, name="out_fwd", grid=(t // tm,),
        in_specs=[a512, a512, a512, a512, _rows(tm, D_MODEL), _full((1, A_WIDTH)), _full((1, A_WIDTH)),
                  _full((D_MODEL, D_MODEL)), _full((1, D_MODEL))],
        out_specs=[_rows(tm, D_MODEL)] * 3,
        out_shape=[jax.ShapeDtypeStruct((t, D_MODEL), BF16), jax.ShapeDtypeStruct((t, D_MODEL), F32),
                   jax.ShapeDtypeStruct((t, D_MODEL), BF16)],
        compiler_params=_params(1),
    )(o_f, o_r, hg, o_attn, x, g_hn, g_mla, w_out, g2)


def _out_bwd(dx2_16, o_f, o_r, hg, o_attn, g_hn, g_mla, w_out):
    t = dx2_16.shape[0]
    tm = _row_tile(t)

    def body(dx_ref, of_ref, or_ref, hg_ref, oa_ref, ghn_ref, gm_ref, w_ref,
             do_ref, dhg_ref, doa_ref, dl_ref, dghn_ref, dgm_ref):
        first = pl.program_id(0) == 0
        ones16 = _group_ones16()
        dxv = dx_ref[...]
        dya = _dot_nt(dxv, w_ref[0:A_WIDTH, :])
        dyb = _dot_nt(dxv, w_ref[A_WIDTH:D_MODEL, :])
        o = of_ref[...] + or_ref[...]
        rh = _head_rms(o, ones16)
        oh = o * rh
        hgv = hg_ref[...]
        sg = _sigmoid(hgv)
        sl = hgv * sg
        ghn = ghn_ref[...]
        dhg_ref[...] = ((dya * (oh * ghn)) * (sg * (1.0 + hgv * (1.0 - sg)))).astype(BF16)
        _acc_rows(dghn_ref, dya * sl * oh, first)
        doh = dya * sl * ghn
        do_ref[...] = rh * (doh - oh * (_exact_right(doh * oh, ones16) * (1.0 / 64.0)))
        oav = oa_ref[...]
        doa, dgm = _rms_bwd(dyb, oav, gm_ref[...])
        doa_ref[...] = doa.astype(BF16)
        _acc_rows(dgm_ref, dgm, first)
        sel16 = (_iota2((8, A_WIDTH), 0) == _iota2((8, A_WIDTH), 1) // B_V).astype(BF16)
        x1, x2, x3 = _split3(doa * oav)
        delta = _dot_nt(sel16, x1) + _dot_nt(sel16, x2) + _dot_nt(sel16, x3)
        for h in range(B_HEADS):
            dl_ref[h] = delta[h:h + 1, :]

    a512 = _rows(tm, A_WIDTH)
    return pl.pallas_call(
        body, name="out_bwd", grid=(t // tm,),
        in_specs=[_rows(tm, D_MODEL), a512, a512, a512, a512, _full((1, A_WIDTH)), _full((1, A_WIDTH)),
                  _full((D_MODEL, D_MODEL))],
        out_specs=[a512, a512, a512, pl.BlockSpec((B_HEADS, 1, tm), lambda i: (0, 0, i)),
                   _full((1, A_WIDTH)), _full((1, A_WIDTH))],
        out_shape=[jax.ShapeDtypeStruct((t, A_WIDTH), F32)] + [jax.ShapeDtypeStruct((t, A_WIDTH), BF16)] * 2
        + [jax.ShapeDtypeStruct((B_HEADS, 1, t), F32)]
        + [jax.ShapeDtypeStruct((1, A_WIDTH), F32)] * 2,
        compiler_params=_params(1),
    )(dx2_16, o_f, o_r, hg, o_attn, g_hn, g_mla, w_out)


def _ffn_fwd_bwd(h2, x2, target, w_gate, w_up, w_down, g_f, g2):
    t = x2.shape[0]
    tm = _row_tile(t)
    inv_d = 1.0 / D_MODEL

    def body(h2_ref, x2_ref, tg_ref, wg_ref, wu_ref, wd_ref, gf_ref, g2_ref,
             act_ref, dgate_ref, dup_ref, dx3_ref, dx2_ref, dx2h_ref, loss_ref, dgf_ref, dg2_ref):
        first = pl.program_id(0) == 0
        h2v = h2_ref[...]
        gate = _dot_nt(h2v, wg_ref[...])
        up = _dot_nt(h2v, wu_ref[...])
        sg = _sigmoid(gate)
        sl = gate * sg
        act16 = (sl * up).astype(BF16)
        act_ref[...] = act16
        x2v = x2_ref[...]
        x3 = x2v + _dot(act16, wd_ref[...])
        r3 = lax.rsqrt(jnp.mean(x3 * x3, axis=-1, keepdims=True) + EPS)
        x3h = x3 * r3
        gf = gf_ref[...]
        err = x3h * gf - tg_ref[...]
        part = 0.5 * jnp.sum(jnp.mean(err * err, axis=-1, keepdims=True), axis=0, keepdims=True)

        @pl.when(first)
        def _():
            loss_ref[...] = jnp.zeros_like(loss_ref)

        loss_ref[...] += part
        dy = err * inv_d
        _acc_rows(dgf_ref, dy * x3h, first)
        dx3h = dy * gf
        dx3 = r3 * (dx3h - x3h * jnp.mean(dx3h * x3h, axis=-1, keepdims=True))
        dx3_16 = dx3.astype(BF16)
        dx3_ref[...] = dx3_16
        da = _dot_nt(dx3_16, wd_ref[...])
        dup16 = (da * sl).astype(BF16)
        dgate16 = (da * up * (sg * (1.0 + gate * (1.0 - sg)))).astype(BF16)
        dup_ref[...] = dup16
        dgate_ref[...] = dgate16
        dh2 = _dot(dgate16, wg_ref[...]) + _dot(dup16, wu_ref[...])
        dx2n, dg2 = _rms_bwd(dh2, x2v, g2_ref[...])
        _acc_rows(dg2_ref, dg2, first)
        dx2 = dx3 + dx2n
        dx2_ref[...] = dx2
        dx2h_ref[...] = dx2.astype(BF16)

    return pl.pallas_call(
        body, name="ffn_fwd_bwd", grid=(t // tm,),
        in_specs=[_rows(tm, D_MODEL), _rows(tm, D_MODEL), _rows(tm, D_MODEL), _full((D_FF, D_MODEL), True),
                  _full((D_FF, D_MODEL), True), _full((D_FF, D_MODEL), True), _full((1, D_MODEL)), _full((1, D_MODEL))],
        out_specs=[_rows(tm, D_FF), _rows(tm, D_FF), _rows(tm, D_FF), _rows(tm, D_MODEL), _rows(tm, D_MODEL),
                   _rows(tm, D_MODEL), _full((8, 128)), _full((1, D_MODEL)), _full((1, D_MODEL))],
        out_shape=[jax.ShapeDtypeStruct((t, D_FF), BF16)] * 3
        + [jax.ShapeDtypeStruct((t, D_MODEL), BF16), jax.ShapeDtypeStruct((t, D_MODEL), F32),
           jax.ShapeDtypeStruct((t, D_MODEL), BF16), jax.ShapeDtypeStruct((8, 128), F32),
           jax.ShapeDtypeStruct((1, D_MODEL), F32), jax.ShapeDtypeStruct((1, D_MODEL), F32)],
        compiler_params=_params(1),
    )(h2, x2, target, w_gate, w_up, w_down, g_f, g2)


def _wgrad(a, b, name):
    t, m = a.shape
    n = b.shape[1]
    bm = _pick(m, 1664)
    bn = _pick(n, 1664)
    tk = _pick(t, 1024, 16)
    nk = t // tk

    def body(a_ref, b_ref, o_ref):
        part = _dot_tn(a_ref[...], b_ref[...])

        @pl.when(pl.program_id(2) == 0)
        def _():
            o_ref[...] = part

        @pl.when(pl.program_id(2) != 0)
        def _():
            o_ref[...] += part

    return pl.pallas_call(
        body, name=name, grid=(m // bm, n // bn, nk),
        in_specs=[pl.BlockSpec((tk, bm), lambda i, j, k: (k, i)), pl.BlockSpec((tk, bn), lambda i, j, k: (k, j))],
        out_specs=pl.BlockSpec((bm, bn), lambda i, j, k: (i, j)),
        out_shape=jax.ShapeDtypeStruct((m, n), F32),
        compiler_params=_params(3),
    )(a, b)


def _rope_tables(seq):
    inv = 1.0 / (ROPE_THETA ** (jnp.arange(0, B_ROPE, 2, dtype=F32) / B_ROPE))
    ang = jnp.arange(seq, dtype=F32)[:, None] * inv[None, :]
    cos, sin = jnp.cos(ang), jnp.sin(ang)
    zeros = jnp.zeros((seq, 64), F32)
    return jnp.concatenate([cos, cos, zeros], axis=1), jnp.concatenate([-sin, sin, zeros], axis=1)


def _pad_weights(w_in_t, w_q_t, w_kv_b):
    w_in_p = jnp.pad(w_in_t, ((0, D_IN_PAD - D_IN), (0, 0)))
    w_q_p = jnp.pad(w_q_t.reshape(B_HEADS, B_NOPE + B_ROPE, Q_LORA), ((0, 0), (0, 64), (0, 0))).reshape(1024, Q_LORA)
    kv = w_kv_b.reshape(KV_LORA, B_HEADS, B_NOPE + B_V)
    w_kv_p = jnp.concatenate([kv[:, :, :B_NOPE].reshape(KV_LORA, 512), kv[:, :, B_NOPE:].reshape(KV_LORA, 512)], axis=1)
    return w_in_p, w_q_p, w_kv_p


def _unpad_grads(g_in_p, g_q_p, g_kv_p):
    g_in = g_in_p[:D_IN]
    g_q = g_q_p.reshape(B_HEADS, QK_PAD, Q_LORA)[:, :B_NOPE + B_ROPE].reshape(B_HEADS * (B_NOPE + B_ROPE), Q_LORA)
    g_kv = jnp.concatenate([g_kv_p[:, :512].reshape(KV_LORA, B_HEADS, B_NOPE),
                            g_kv_p[:, 512:].reshape(KV_LORA, B_HEADS, B_V)], axis=2).reshape(KV_LORA, 1024)
    return g_in, g_q, g_kv


def _local_step(x, target, lbl, g1, g_hn, g_qa, g_kva, g_mla, g2, g_f, w_in, w_q_b, w_kv_b, late_shards):
    n_batch, seq, _ = x.shape
    t = n_batch * seq
    x = x.reshape(t, D_MODEL)
    target = target.reshape(t, D_MODEL)
    w_in_p, w_q_p, w_kv_p = _pad_weights(w_in, w_q_b, w_kv_b)
    cosx, sinx = _rope_tables(seq)

    h1, hq, hi, hff, hfb, hg, cq, ckv, kr = _in_fwd(x, g1, w_in_p)
    o_f, st_f = _hgrn_fwd(hq, hi, hff, lbl, n_batch=n_batch, direction=0)
    o_r, st_r = _hgrn_fwd(hq, hi, hfb, lbl, n_batch=n_batch, direction=1)
    q, k, v, cqn, ckvn = _mla_prep(cq, ckv, kr, g_qa, g_kva, w_q_p, w_kv_p, cosx, sinx, n_batch=n_batch)
    o_attn, lse, *gathered = _attn_fwd(q, k, v, late_shards, n_batch=n_batch)
    w_out, w_gate, w_up, w_down = (_join_shards(n, a) for n, a in zip(_LATE, gathered))
    ycat, x2, h2 = _out_fwd(o_f, o_r, hg, o_attn, x, g_hn, g_mla, w_out, g2)
    act, dgate, dup, dx3_16, dx2, dx2_16, loss, dg_f, dg2 = _ffn_fwd_bwd(h2, x2, target, w_gate, w_up, w_down, g_f, g2)
    g_late = dict(w_out=_wgrad(ycat, dx2_16, "wgrad_out"), w_gate=_wgrad(dgate, h2, "wgrad_gate"),
                  w_up=_wgrad(dup, h2, "wgrad_up"), w_down=_wgrad(act, dx3_16, "wgrad_down"))
    do_h, dhg, do_attn16, delta, dg_hn, dg_mla = _out_bwd(dx2_16, o_f, o_r, hg, o_attn, g_hn, g_mla, w_out)
    dqt, dk, dv, *recv_late = _attn_bwd(q, k, v, do_attn16, lse, delta,
                                        [_segments(n, g_late[n]).astype(BF16) for n in _LATE], n_batch=n_batch)
    dcq, dckv, dkr, dqp16, dkvp16, dg_qa, dg_kva = _mla_prep_bwd(dqt, dk, dv, cq, ckv, g_qa, g_kva, w_q_p, w_kv_p,
                                                                  cosx, sinx, n_batch=n_batch)
    dq_f, dv_f, dz_f, dl_f = _hgrn_bwd(hq, hi, hff, do_h, st_f, lbl, n_batch=n_batch, direction=0)
    dq_r, dv_r, dz_r, dl_r = _hgrn_bwd(hq, hi, hfb, do_h, st_r, lbl, n_batch=n_batch, direction=1)
    dx, dproj16, dg1 = _in_bwd(dq_f, dq_r, dv_f, dv_r, dz_f, dz_r, dhg, hq, dcq, dckv, dkr, dx2, x, g1, w_in_p)

    gw_in, gw_q, gw_kv = _unpad_grads(_wgrad(dproj16, h1, "wgrad_in"), _wgrad(dqp16, cqn, "wgrad_q_b"),
                                      _wgrad(ckvn, dkvp16, "wgrad_kv_b"))
    grads = dict(
        norm1_g=dg1, w_in=gw_in, lb_logits=jnp.stack([dl_f, dl_r]), hgrn_norm_g=dg_hn, q_a_norm_g=dg_qa,
        w_q_b=gw_q, kv_a_norm_g=dg_kva, w_kv_b=gw_kv, mla_norm_g=dg_mla, norm2_g=dg2, final_norm_g=dg_f)
    grads.update(g_late)
    return loss[0, 0], dx.reshape(n_batch, seq, D_MODEL), grads, dict(zip(_LATE, recv_late))


_HBM = pl.BlockSpec(memory_space=pltpu.HBM)
_MESH = pl.DeviceIdType.MESH


def _place():
    x, y, c = lax.axis_index("x"), lax.axis_index("y"), lax.axis_index("c")
    other_chips = [(1 - x, y), (x, 1 - y), (1 - x, 1 - y)]
    return x, y, c, other_chips


def _gather_copies(w_ref, wall_ref, send_sems, recv_sems, local_sem, base=0):
    x, y, c, chips = _place()
    mine = 2 * x + y

    def mk(j, chip_index, to):
        return pltpu.make_async_remote_copy(src_ref=w_ref, dst_ref=wall_ref.at[chip_index], send_sem=send_sems.at[base + j],
                                            recv_sem=recv_sems.at[base + j], device_id=to, device_id_type=_MESH)

    local = pltpu.make_async_copy(w_ref, wall_ref.at[mine], local_sem)
    sends = [mk(j, mine, (*chip, c)) for j, chip in enumerate(chips)]
    recvs = [mk(j, 2 * px + py, (x, y, c)) for j, (px, py) in enumerate(chips)]
    return local, sends, recvs


def _gather_start(*refs):
    local, sends, _ = _gather_copies(*refs)
    local.start()
    for cp in sends:
        cp.start()


def _gather_wait(*refs):
    local, sends, recvs = _gather_copies(*refs)
    for cp in recvs:
        cp.wait_recv()
    for cp in sends:
        cp.wait_send()
    local.wait()


def _scatter_copies(g_ref, recv_ref, send_sems, recv_sems, base=0):
    x, y, c, chips = _place()

    def mk(j, src_index, to):
        return pltpu.make_async_remote_copy(src_ref=g_ref.at[src_index], dst_ref=recv_ref.at[j],
                                            send_sem=send_sems.at[base + j], recv_sem=recv_sems.at[base + j],
                                            device_id=to, device_id_type=_MESH)

    sends = [mk(j, 2 * px + py, (px, py, c)) for j, (px, py) in enumerate(chips)]
    recvs = [mk(j, 0, (x, y, c)) for j in range(3)]
    return sends, recvs


def _scatter_start(*refs):
    for cp in _scatter_copies(*refs)[0]:
        cp.start()


def _scatter_wait(*refs):
    sends, recvs = _scatter_copies(*refs)
    for cp in recvs:
        cp.wait_recv()
    for cp in sends:
        cp.wait_send()


def _gather_shards(srcs):
    n = len(srcs)

    def body(*refs):
        src_refs, dst_refs = refs[:n], refs[n:2 * n]
        send_sems, recv_sems, local_sems = refs[2 * n:]
        for i in range(n):
            _gather_start(src_refs[i], dst_refs[i], send_sems, recv_sems, local_sems.at[i], 3 * i)
        for i in range(n):
            _gather_wait(src_refs[i], dst_refs[i], send_sems, recv_sems, local_sems.at[i], 3 * i)

    return pl.pallas_call(
        body, name="gather_shards", in_specs=[_HBM] * n, out_specs=[_HBM] * n,
        out_shape=[jax.ShapeDtypeStruct((N_CHIPS,) + a.shape, a.dtype) for a in srcs],
        scratch_shapes=[pltpu.SemaphoreType.DMA((3 * n,)), pltpu.SemaphoreType.DMA((3 * n,)), pltpu.SemaphoreType.DMA((n,))],
    )(*srcs)


def _scatter_grads(gsegs, small):
    n = len(gsegs)

    def body(*refs):
        g_refs, s_ref, recv_refs, sall_ref = refs[:n], refs[n], refs[n + 1:2 * n + 1], refs[2 * n + 1]
        send_sems, recv_sems, local_sem = refs[2 * n + 2:]
        x, y, c, _ = _place()
        me = 4 * x + 2 * y + c
        flips = [(fx, fy, fc) for fx in (0, 1) for fy in (0, 1) for fc in (0, 1)][1:]

        def peer(f):
            return tuple((1 - a) if b else a for a, b in zip((x, y, c), f))

        def sm(r, index, to):
            return pltpu.make_async_remote_copy(src_ref=s_ref, dst_ref=sall_ref.at[index], send_sem=send_sems.at[3 * n + r],
                                                recv_sem=recv_sems.at[3 * n + r], device_id=to, device_id_type=_MESH)

        local = pltpu.make_async_copy(s_ref, sall_ref.at[me], local_sem)
        local.start()
        for i in range(n):
            _scatter_start(g_refs[i], recv_refs[i], send_sems, recv_sems, 3 * i)
        sends = [sm(r, me, peer(f)) for r, f in enumerate(flips)]
        for cp in sends:
            cp.start()
        for i in range(n):
            _scatter_wait(g_refs[i], recv_refs[i], send_sems, recv_sems, 3 * i)
        for r, f in enumerate(flips):
            px, py, pc = peer(f)
            sm(r, 4 * px + 2 * py + pc, (x, y, c)).wait_recv()
        for cp in sends:
            cp.wait_send()
        local.wait()

    return pl.pallas_call(
        body, name="scatter_grads", in_specs=[_HBM] * (n + 1), out_specs=[_HBM] * (n + 1),
        out_shape=[jax.ShapeDtypeStruct((3,) + g.shape[1:], g.dtype) for g in gsegs]
        + [jax.ShapeDtypeStruct((8,) + small.shape, F32)],
        scratch_shapes=[pltpu.SemaphoreType.DMA((3 * n + 7,)), pltpu.SemaphoreType.DMA((3 * n + 7,)),
                        pltpu.SemaphoreType.DMA(())],
    )(*gsegs, small)


def _swap_sibling(parts):
    n = len(parts)

    def body(*refs):
        send_sems, recv_sems = refs[2 * n:]
        x, y, c, _ = _place()
        cps = [pltpu.make_async_remote_copy(src_ref=refs[i], dst_ref=refs[n + i], send_sem=send_sems.at[i],
                                            recv_sem=recv_sems.at[i], device_id=(x, y, 1 - c), device_id_type=_MESH)
               for i in range(n)]
        for cp in cps:
            cp.start()
        for cp in cps:
            cp.wait()

    return pl.pallas_call(
        body, name="swap_sibling", in_specs=[_HBM] * n, out_specs=[_HBM] * n,
        out_shape=[jax.ShapeDtypeStruct(p.shape, p.dtype) for p in parts],
        scratch_shapes=[pltpu.SemaphoreType.DMA((n,)), pltpu.SemaphoreType.DMA((n,))],
    )(*parts)


def _sum_segments(own, recv, name):
    rows, cols = own.shape
    tm = _pick(rows, 512, 16)

    def body(o_ref, r_ref, out_ref):
        acc = o_ref[...]
        for j in range(3):
            acc = acc + r_ref[j].astype(F32)
        out_ref[...] = acc

    return pl.pallas_call(
        body, name=name, grid=(rows // tm,),
        in_specs=[_rows(tm, cols), pl.BlockSpec((3, tm, cols), lambda i: (0, i, 0))],
        out_specs=_rows(tm, cols), out_shape=jax.ShapeDtypeStruct(own.shape, F32), compiler_params=_params(1),
    )(own, recv)


def _sum_devices(sall):
    def body(s_ref, o_ref):
        acc = s_ref[0]
        for d in range(1, 8):
            acc = acc + s_ref[d]
        o_ref[...] = acc

    return pl.pallas_call(body, name="sum_devices", out_shape=jax.ShapeDtypeStruct(sall.shape[1:], F32))(sall)


def _adamw(w, m, v, ga, gb, name):
    rows, cols = w.shape
    tm = _row_tile(rows)
    two = gb is not None

    def body(*refs):
        w_ref, m_ref, v_ref, ga_ref = refs[:4]
        g_ref, d_ref, m2_ref, v2_ref = refs[-4:]
        g = ga_ref[...] + refs[4][...] if two else ga_ref[...]
        m2 = ADAM_B1 * m_ref[...] + (1.0 - ADAM_B1) * g
        v2 = ADAM_B2 * v_ref[...] + (1.0 - ADAM_B2) * (g * g)
        m_hat = m2 / (1.0 - ADAM_B1 ** ADAM_STEP)
        v_hat = v2 / (1.0 - ADAM_B2 ** ADAM_STEP)
        g_ref[...] = g
        d_ref[...] = -ADAM_LR * (m_hat / (jnp.sqrt(v_hat) + ADAM_EPS) + ADAM_WD * w_ref[...])
        m2_ref[...] = m2
        v2_ref[...] = v2

    blk = _rows(tm, cols)
    args = (w, m, v, ga) + ((gb,) if two else ())
    return pl.pallas_call(
        body, name=name, grid=(rows // tm,), in_specs=[blk] * len(args), out_specs=[blk] * 4,
        out_shape=[jax.ShapeDtypeStruct(w.shape, F32)] * 4, compiler_params=_params(1),
    )(*args)


_TRANSPOSED = ("w_in", "w_q_b", "w_gate", "w_up")
_COLUMN_SHARDED = ("w_kv_b",)
_FULL_SHAPES = dict(w_in=(D_IN, D_MODEL), w_q_b=(768, Q_LORA), w_kv_b=(KV_LORA, 1024), w_out=(D_MODEL, D_MODEL),
                    w_gate=(D_FF, D_MODEL), w_up=(D_FF, D_MODEL), w_down=(D_FF, D_MODEL))
_SMALL = (("norm1_g", 1024), ("lb_logits", 2048), ("hgrn_norm_g", 512), ("q_a_norm_g", 384), ("kv_a_norm_g", 256),
          ("mla_norm_g", 512), ("norm2_g", 1024), ("final_norm_g", 1024))
_UPDATE_ROWS = 48


def _pad_rows(a, rows):
    return jnp.pad(a, ((0, rows - a.shape[0]), (0, 0)))


_EARLY = ("w_in", "w_q_b", "w_kv_b")
_LATE = ("w_out", "w_gate", "w_up", "w_down")


def _segments(name, g):
    r, c = g.shape
    if name in _COLUMN_SHARDED:
        return g.reshape(r, N_CHIPS, c // N_CHIPS).transpose(1, 0, 2)
    return g.reshape(N_CHIPS, r // N_CHIPS, c)


def _own_segment(name, g, chip):
    r, c = g.shape
    if name in _COLUMN_SHARDED:
        return lax.dynamic_slice_in_dim(g, chip * (c // N_CHIPS), c // N_CHIPS, axis=1)
    return lax.dynamic_slice_in_dim(g, chip * (r // N_CHIPS), r // N_CHIPS, axis=0)


def _join_shards(name, seg):
    r, c = _FULL_SHAPES[name]
    if name in _COLUMN_SHARDED:
        return seg.transpose(1, 0, 2).reshape(r, c)
    return seg.reshape(r, c)


def kernel(x, norm1_g, w_in, lb_logits, hgrn_norm_g, q_a_norm_g, w_q_b, kv_a_norm_g, w_kv_b, mla_norm_g, w_out, norm2_g, w_gate, w_up, w_down, final_norm_g, loss_target, m_norm1_g, m_w_in, m_lb_logits, m_hgrn_norm_g, m_q_a_norm_g, m_w_q_b, m_kv_a_norm_g, m_w_kv_b, m_mla_norm_g, m_w_out, m_norm2_g, m_w_gate, m_w_up, m_w_down, m_final_norm_g, v_norm1_g, v_w_in, v_lb_logits, v_hgrn_norm_g, v_q_a_norm_g, v_w_q_b, v_kv_a_norm_g, v_w_kv_b, v_mla_norm_g, v_w_out, v_norm2_g, v_w_gate, v_w_up, v_w_down, v_final_norm_g):
    names = ("norm1_g", "w_in", "lb_logits", "hgrn_norm_g", "q_a_norm_g", "w_q_b", "kv_a_norm_g", "w_kv_b", "mla_norm_g",
             "w_out", "norm2_g", "w_gate", "w_up", "w_down", "final_norm_g")
    w = dict(zip(names, (norm1_g, w_in, lb_logits, hgrn_norm_g, q_a_norm_g, w_q_b, kv_a_norm_g, w_kv_b, mla_norm_g,
                         w_out, norm2_g, w_gate, w_up, w_down, final_norm_g)))
    m = dict(zip(names, (m_norm1_g, m_w_in, m_lb_logits, m_hgrn_norm_g, m_q_a_norm_g, m_w_q_b, m_kv_a_norm_g, m_w_kv_b,
                         m_mla_norm_g, m_w_out, m_norm2_g, m_w_gate, m_w_up, m_w_down, m_final_norm_g)))
    v = dict(zip(names, (v_norm1_g, v_w_in, v_lb_logits, v_hgrn_norm_g, v_q_a_norm_g, v_w_q_b, v_kv_a_norm_g, v_w_kv_b,
                         v_mla_norm_g, v_w_out, v_norm2_g, v_w_gate, v_w_up, v_w_down, v_final_norm_g)))
    matrices = _EARLY + _LATE
    chip = 2 * lax.axis_index("x") + lax.axis_index("y")

    def shard2d(a, n):
        return jnp.swapaxes(a[0], 0, 1) if n in _TRANSPOSED else a[0]

    def unshard2d(a, n):
        return (jnp.swapaxes(a, 0, 1) if n in _TRANSPOSED else a)[None]

    w16 = {n: shard2d(w[n], n).astype(BF16) for n in matrices}
    lb8 = _pad_rows(lb_logits.reshape(4, 128), 8)
    *early, lball = _gather_shards([w16[n] for n in _EARLY] + [lb8])
    lbl = lball[:, :4].reshape(N_CHIPS, 2, 2, 128).transpose(1, 2, 0, 3).reshape(2, 2, A_WIDTH)

    loss_part, grad_x, g, recv = _local_step(
        x, loss_target, lbl, norm1_g, hgrn_norm_g, q_a_norm_g, kv_a_norm_g, mla_norm_g, norm2_g, final_norm_g[None, :],
        *(_join_shards(n, a) for n, a in zip(_EARLY, early)), [w16[n] for n in _LATE])
    loss = lax.psum(loss_part, ("x", "y", "c"))

    small = _pad_rows(jnp.concatenate([g[n].reshape(-1) for n, _ in _SMALL]).reshape(-1, 128), SMALL_ROWS)
    *recv_early, small_all = _scatter_grads([_segments(n, g[n]).astype(BF16) for n in _EARLY], small)
    recv.update(zip(_EARLY, recv_early))
    parts = [_sum_segments(_own_segment(n, g[n], chip), recv[n], f"sum_{n}") for n in matrices]
    sibs = _swap_sibling(parts)
    small_sum = _sum_devices(small_all).reshape(-1)

    out = {}
    for n, part, sib in zip(matrices, parts, sibs):
        res = _adamw(shard2d(w[n], n), shard2d(m[n], n), shard2d(v[n], n), part, sib, f"adamw_{n}")
        out[n] = tuple(unshard2d(r, n) for r in res)

    small_g, off = {}, 0
    for n, size in _SMALL:
        small_g[n] = small_sum[off:off + size]
        off += size
    small_g["lb_logits"] = lax.dynamic_slice_in_dim(small_g["lb_logits"].reshape(2, 2, A_WIDTH), chip * 128, 128, axis=2)
    small_names = tuple(n for n, _ in _SMALL)

    def pack(d):
        return _pad_rows(jnp.concatenate([d[n].reshape(-1) for n in small_names]).reshape(-1, 128), _UPDATE_ROWS)

    res = _adamw(pack(w), pack(m), pack(v), pack(small_g), None, "adamw_small")
    off = 0
    flat = [r.reshape(-1) for r in res]
    for n in small_names:
        size = w[n].size
        out[n] = tuple(f[off:off + size].reshape(w[n].shape) for f in flat)
        off += size

    return (loss, grad_x) + tuple(out[n][i] for i in range(4) for n in names)
```

```python
import functools

import jax
import jax.numpy as jnp
from jax import lax
from jax.experimental import pallas as pl
from jax.experimental.pallas import tpu as pltpu

F32 = jnp.float32
BF16 = jnp.bfloat16

D_MODEL = 1024
A_WIDTH = 512
HEAD_PAIRS = 4
CHUNK = 64
B_HEADS = 4
B_NOPE = 128
B_ROPE = 64
B_V = 128
QK_PAD = 256
Q_LORA = 384
KV_LORA = 256
D_FF = 2816
D_IN = 3264
D_IN_PAD = 3328
IN_WIDTHS = (512, 512, 512, 512, 512, Q_LORA, KV_LORA, 128)
ROPE_THETA = 10000.0
EPS = 1e-6
ATTN_SCALE = (B_NOPE + B_ROPE) ** -0.5
LOG2E = 1.4426950408889634
SCALE_LOG2E = ATTN_SCALE * LOG2E

ADAM_LR = 0.001
ADAM_B1 = 0.9
ADAM_B2 = 0.999
ADAM_EPS = 1e-08
ADAM_WD = 0.01
ADAM_STEP = 10

VMEM_LIMIT_BYTES = 60 * 1024 * 1024
N_CHIPS = 4
SMALL_ROWS = 56


def _params(n_axes):
    return pltpu.CompilerParams(dimension_semantics=("arbitrary",) * n_axes,
                                vmem_limit_bytes=VMEM_LIMIT_BYTES)


def _dot(a, b):
    return jnp.dot(a, b, preferred_element_type=F32)


def _dot_nt(a, b):
    return lax.dot_general(a, b, (((1,), (1,)), ((), ())), preferred_element_type=F32)


def _dot_tn(a, b):
    return lax.dot_general(a, b, (((0,), (0,)), ((), ())), preferred_element_type=F32)


def _split3(x):
    x1 = x.astype(BF16)
    r = x - x1.astype(F32)
    x2 = r.astype(BF16)
    x3 = (r - x2.astype(F32)).astype(BF16)
    return x1, x2, x3


def _exact_left(m16, x):
    x1, x2, x3 = _split3(x)
    return _dot(m16, x1) + _dot(m16, x2) + _dot(m16, x3)


def _exact_right(x, m16):
    x1, x2, x3 = _split3(x)
    return _dot(x1, m16) + _dot(x2, m16) + _dot(x3, m16)


def _iota2(shape, dim):
    return lax.broadcasted_iota(jnp.int32, shape, dim)


def _sigmoid(x):
    return jax.nn.sigmoid(x)


def _pick(dim, cap, mult=128):
    if dim <= cap:
        return dim
    best = None
    for d in range(mult, cap + 1, mult):
        if dim % d == 0:
            best = d
    assert best is not None, (dim, cap, mult)
    return best


def _row_tile(t, cap=256):
    return _pick(t, cap, 8)


def _full(shape, single=False):
    if single:
        return pl.BlockSpec(shape, lambda *_: (0,) * len(shape), pipeline_mode=pl.Buffered(1))
    return pl.BlockSpec(shape, lambda *_: (0,) * len(shape))


def _rows(tm, width):
    return pl.BlockSpec((tm, width), lambda i: (i, 0))


def _acc_rows(ref, val, first):
    s = jnp.sum(val, axis=0, keepdims=True)

    @pl.when(first)
    def _():
        ref[...] = s

    @pl.when(jnp.logical_not(first))
    def _():
        ref[...] += s


def _in_fwd(x, g1, w_in_p):
    t = x.shape[0]
    tm = _row_tile(t)

    def body(x_ref, g_ref, w_ref, h_ref, *outs):
        xv = x_ref[...]
        r = lax.rsqrt(jnp.mean(xv * xv, axis=-1, keepdims=True) + EPS)
        h = ((xv * r) * g_ref[...]).astype(BF16)
        h_ref[...] = h
        off = 0
        for o_ref, w in zip(outs, IN_WIDTHS):
            o_ref[...] = _dot_nt(h, w_ref[off:off + w, :])
            off += w

    return pl.pallas_call(
        body, name="in_fwd", grid=(t // tm,),
        in_specs=[_rows(tm, D_MODEL), _full((1, D_MODEL)), _full((D_IN_PAD, D_MODEL))],
        out_specs=[_rows(tm, D_MODEL)] + [_rows(tm, w) for w in IN_WIDTHS],
        out_shape=[jax.ShapeDtypeStruct((t, D_MODEL), BF16)]
        + [jax.ShapeDtypeStruct((t, w), F32) for w in IN_WIDTHS],
        compiler_params=_params(1),
    )(x, g1, w_in_p)


def _in_bwd(dq_f, dq_r, dv_f, dv_r, dz_f, dz_r, dhg, hq, dcq, dckv, dkr, dx2, x, g1, w_in_p):
    t = x.shape[0]
    tm = _row_tile(t)

    def body(dqf_ref, dqr_ref, dvf_ref, dvr_ref, dzf_ref, dzr_ref, dhg_ref, hq_ref, dcq_ref, dckv_ref,
             dkr_ref, dx2_ref, x_ref, g_ref, w_ref, dx_ref, dp_ref, dg_ref):
        hqv = hq_ref[...]
        sg = _sigmoid(hqv)
        dhq = (dqf_ref[...] + dqr_ref[...]) * (sg * (1.0 + hqv * (1.0 - sg)))
        pieces = (dhq, dvf_ref[...] + dvr_ref[...], dzf_ref[...], dzr_ref[...], dhg_ref[...],
                  dcq_ref[...], dckv_ref[...], dkr_ref[...])
        dh = None
        off = 0
        for p, w in zip(pieces, IN_WIDTHS):
            p16 = p.astype(BF16)
            dp_ref[:, off:off + w] = p16
            part = _dot(p16, w_ref[off:off + w, :])
            dh = part if dh is None else dh + part
            off += w
        xv = x_ref[...]
        r = lax.rsqrt(jnp.mean(xv * xv, axis=-1, keepdims=True) + EPS)
        xh = xv * r
        _acc_rows(dg_ref, dh * xh, pl.program_id(0) == 0)
        dxh = dh * g_ref[...]
        dx_ref[...] = dx2_ref[...] + r * (dxh - xh * jnp.mean(dxh * xh, axis=-1, keepdims=True))

    a512 = _rows(tm, A_WIDTH)
    return pl.pallas_call(
        body, name="in_bwd", grid=(t // tm,),
        in_specs=[a512] * 8 + [_rows(tm, Q_LORA), _rows(tm, KV_LORA), _rows(tm, 128), _rows(tm, D_MODEL),
                               _rows(tm, D_MODEL), _full((1, D_MODEL)), _full((D_IN_PAD, D_MODEL))],
        out_specs=[_rows(tm, D_MODEL), _rows(tm, D_IN_PAD), _full((1, D_MODEL))],
        out_shape=[jax.ShapeDtypeStruct((t, D_MODEL), F32), jax.ShapeDtypeStruct((t, D_IN_PAD), BF16),
                   jax.ShapeDtypeStruct((1, D_MODEL), F32)],
        compiler_params=_params(1),
    )(dq_f, dq_r, dv_f, dv_r, dz_f, dz_r, dhg, hq, dcq, dckv, dkr, dx2, x, g1, w_in_p)


def _lower_bound(lbl_ref, direction):
    l0 = lbl_ref[direction, 0:1, :]
    l1 = lbl_ref[direction, 1:2, :]
    m = jnp.maximum(l0, l1)
    e0 = jnp.exp(l0 - m)
    e1 = jnp.exp(l1 - m)
    return e0 / (e0 + e1)


def _hgrn_consts(rb, reverse):
    row = _iota2((rb, rb), 0)
    col = _iota2((rb, rb), 1)
    same = (row // CHUNK) == (col // CHUNK)
    tri = jnp.logical_and(same, (col >= row) if reverse else (col <= row))
    tri_t = jnp.logical_and(same, (col <= row) if reverse else (col >= row))
    r128 = _iota2((128, 128), 0)
    c128 = _iota2((128, 128), 1)
    bd = (r128 < 64) == (c128 < 64)
    lane = _iota2((1, 128), 1)
    m0 = (lane < 64).astype(F32)
    return tri, tri_t, bd, (m0, 1.0 - m0)


def _per_chunk(x, fn):
    n = x.shape[0] // CHUNK
    return jnp.concatenate([jnp.broadcast_to(fn(x[c * CHUNK:(c + 1) * CHUNK]), (CHUNK, x.shape[1])) for c in range(n)],
                           axis=0)


def _chunk_cumsum(x, reverse):
    rb = x.shape[0]
    pos = _iota2(x.shape, 0) % CHUNK
    step = 1
    while step < CHUNK:
        if reverse:
            x = x + jnp.where(pos < CHUNK - step, pltpu.roll(x, rb - step, 0), 0.0)
        else:
            x = x + jnp.where(pos >= step, pltpu.roll(x, step, 0), 0.0)
        step *= 2
    return x


def _hgrn_block(z, hqv, lb, reverse):
    sig = _sigmoid(z)
    sn = _sigmoid(-z)
    q = hqv * _sigmoid(hqv)
    f = lb + (1.0 - lb) * sig
    k = (1.0 - lb) * sn
    lf = jnp.log(f)
    cum = _chunk_cumsum(lf, reverse)
    last = _per_chunk(cum, (lambda a: a[0:1]) if reverse else (lambda a: a[CHUNK - 1:CHUNK]))
    e_neg = jnp.exp(-cum)
    e_end = jnp.exp(last - cum)
    a = jnp.exp(cum)
    return dict(sig=sig, sn=sn, q=q, f=f, k=k, a=a, e_neg=e_neg, e_end=e_end,
                q_dec=q * a, k_inv=k * e_neg, k_end=k * e_end, d=jnp.exp(last))


def _hgrn_dims(t, n_batch):
    s = t // n_batch
    rb = _pick(s, 256, CHUNK)
    return s, rb, s // rb, rb // CHUNK


def _hgrn_fwd(hq, hi, hf, lbl, *, n_batch, direction):
    t = hq.shape[0]
    reverse = direction == 1
    s, rb, nb, nc = _hgrn_dims(t, n_batch)

    def tmap(b, j):
        return (b * nb + ((nb - 1 - j) if reverse else j), 0)

    def smap(b, j):
        return (b * nb + ((nb - 1 - j) if reverse else j), 0, 0, 0)

    def body(hq_ref, hi_ref, hf_ref, lbl_ref, o_ref, st_ref, st_scr):
        @pl.when(pl.program_id(1) == 0)
        def _():
            st_scr[...] = jnp.zeros_like(st_scr)

        lb_all = _lower_bound(lbl_ref, direction)
        tri, _, bd, masks = _hgrn_consts(rb, reverse)
        order = range(nc - 1, -1, -1) if reverse else range(nc)

        pairs = [slice(p * 128, (p + 1) * 128) for p in range(HEAD_PAIRS)]
        chunks = [slice(c * CHUNK, (c + 1) * CHUNK) for c in range(nc)]
        w = _hgrn_block(hf_ref[...], hq_ref[...], lb_all, reverse)
        v16 = hi_ref[...].astype(BF16)
        qd16 = w["q_dec"].astype(BF16)
        ki16 = w["k_inv"].astype(BF16)
        ke16 = w["k_end"].astype(BF16)
        sc = [[_dot_nt(jnp.where(mh > 0.0, qd16[:, ls], 0.0).astype(BF16), ki16[:, ls]) for mh in masks] for ls in pairs]
        pv = [[_dot(jnp.where(tri, s_e, 0.0).astype(BF16), v16[:, ls]) for s_e in sc_p] for sc_p, ls in zip(sc, pairs)]
        o_intra = [jnp.where(masks[0] > 0.0, pv_p[0], pv_p[1]) for pv_p in pv]
        ut = [[jnp.where(bd, _dot_tn(v16[rs, ls], ke16[rs, ls]), 0.0) for ls in pairs] for rs in chunks]
        st = [st_scr[p] for p in range(HEAD_PAIRS)]
        for c in order:
            rs = chunks[c]
            inter = [_dot_nt(qd16[rs, ls], st[p].astype(BF16)) for p, ls in enumerate(pairs)]
            for p, ls in enumerate(pairs):
                o_ref[rs, ls] = o_intra[p][rs] + inter[p]
                st_ref[c, p] = st[p]
                st[p] = st[p] * w["d"][c * CHUNK:c * CHUNK + 1, ls] + ut[c][p]
        for p in range(HEAD_PAIRS):
            st_scr[p] = st[p]

    blk = pl.BlockSpec((rb, A_WIDTH), tmap)
    return pl.pallas_call(
        body, name=f"hgrn_fwd_{direction}", grid=(n_batch, nb),
        in_specs=[blk, blk, blk, _full((2, 2, A_WIDTH))],
        out_specs=[blk, pl.BlockSpec((nc, HEAD_PAIRS, 128, 128), smap)],
        out_shape=[jax.ShapeDtypeStruct((t, A_WIDTH), F32),
                   jax.ShapeDtypeStruct((t // CHUNK, HEAD_PAIRS, 128, 128), F32)],
        scratch_shapes=[pltpu.VMEM((HEAD_PAIRS, 128, 128), F32)],
        compiler_params=_params(2),
    )(hq, hi, hf, lbl)


def _hgrn_bwd(hq, hi, hf, do, st, lbl, *, n_batch, direction):
    t = hq.shape[0]
    reverse = direction == 1
    s, rb, nb, nc = _hgrn_dims(t, n_batch)

    def tmap(b, j):
        return (b * nb + (j if reverse else (nb - 1 - j)), 0)

    def smap(b, j):
        return (b * nb + (j if reverse else (nb - 1 - j)), 0, 0, 0)

    def body(hq_ref, hi_ref, hf_ref, do_ref, st_ref, lbl_ref, dq_ref, dv_ref, dz_ref, dl_ref, g_scr, dlb_scr):
        b = pl.program_id(0)
        j = pl.program_id(1)

        @pl.when(jnp.logical_and(b == 0, j == 0))
        def _():
            dlb_scr[...] = jnp.zeros_like(dlb_scr)

        @pl.when(j == 0)
        def _():
            g_scr[...] = jnp.zeros_like(g_scr)

        lb_all = _lower_bound(lbl_ref, direction)
        tri, tri_t, bd, masks = _hgrn_consts(rb, reverse)
        order = range(nc) if reverse else range(nc - 1, -1, -1)

        pairs = [slice(p * 128, (p + 1) * 128) for p in range(HEAD_PAIRS)]
        chunks = [slice(c * CHUNK, (c + 1) * CHUNK) for c in range(nc)]

        def lanes(per_pair):
            return jnp.concatenate(per_pair, axis=1)

        w = _hgrn_block(hf_ref[...], hq_ref[...], lb_all, reverse)
        dov = do_ref[...]
        v16 = hi_ref[...].astype(BF16)
        do16 = dov.astype(BF16)
        qd16 = w["q_dec"].astype(BF16)
        ki16 = w["k_inv"].astype(BF16)
        ke16 = w["k_end"].astype(BF16)
        qm16 = [[jnp.where(mh > 0.0, qd16[:, ls], 0.0).astype(BF16) for mh in masks] for ls in pairs]
        dom16 = [[jnp.where(mh > 0.0, do16[:, ls], 0.0).astype(BF16) for mh in masks] for ls in pairs]
        heads = [(p, e) for p in range(HEAD_PAIRS) for e in range(2)]
        dp = {(p, e): _dot_nt(dom16[p][e], v16[:, pairs[p]]) for p, e in heads}
        pm_t = {(p, e): _dot_nt(ki16[:, pairs[p]], qm16[p][e]) for p, e in heads}
        dp_t = {(p, e): _dot_nt(v16[:, pairs[p]], dom16[p][e]) for p, e in heads}
        dp = {h: jnp.where(tri, a, 0.0).astype(BF16) for h, a in dp.items()}
        pm_t = {h: jnp.where(tri_t, a, 0.0).astype(BF16) for h, a in pm_t.items()}
        dp_t = {h: jnp.where(tri_t, a, 0.0).astype(BF16) for h, a in dp_t.items()}
        dv_e = {(p, e): _dot(pm_t[p, e], do16[:, pairs[p]]) for p, e in heads}
        dq_e = {(p, e): _dot(dp[p, e], ki16[:, pairs[p]]) for p, e in heads}
        dk_e = {(p, e): _dot(dp_t[p, e], qd16[:, pairs[p]]) for p, e in heads}
        st = [[st_ref[c, p] for p in range(HEAD_PAIRS)] for c in range(nc)]
        dq_x = [[_dot(do16[rs, ls], st[c][p].astype(BF16)) for p, ls in enumerate(pairs)] for c, rs in enumerate(chunks)]
        gq = [[jnp.where(bd, _dot_tn(do16[rs, ls], qd16[rs, ls]), 0.0) for ls in pairs] for rs in chunks]
        g = [g_scr[p] for p in range(HEAD_PAIRS)]
        dk_end, dv_x, dd = [None] * nc, [None] * nc, [None] * nc
        for c in order:
            rs = chunks[c]
            g16 = [a.astype(BF16) for a in g]
            dk_end[c] = lanes([_dot(v16[rs, ls], g16[p]) for p, ls in enumerate(pairs)])
            dv_x[c] = lanes([_dot_nt(ke16[rs, ls], g16[p]) for p, ls in enumerate(pairs)])
            dd[c] = jnp.broadcast_to(lanes([jnp.sum(g[p] * st[c][p], axis=0, keepdims=True) for p in range(HEAD_PAIRS)]),
                                     (CHUNK, A_WIDTH))
            for p, ls in enumerate(pairs):
                g[p] = g[p] * w["d"][c * CHUNK:c * CHUNK + 1, ls] + gq[c][p]
        for p in range(HEAD_PAIRS):
            g_scr[p] = g[p]

        def both_heads(d):
            return lanes([jnp.where(masks[0] > 0.0, d[p, 0], d[p, 1]) for p in range(HEAD_PAIRS)])

        dq_dec = both_heads(dq_e) + jnp.concatenate([lanes(a) for a in dq_x], axis=0)
        dk_inv = both_heads(dk_e)
        dk_end = jnp.concatenate(dk_end, axis=0)
        dv = both_heads(dv_e) + jnp.concatenate(dv_x, axis=0)
        dd = jnp.concatenate(dd, axis=0)
        dke = dk_end * w["k_end"]
        dcum = dq_dec * w["q_dec"] - dk_inv * w["k_inv"] - dke
        dk = dk_inv * w["e_neg"] + dk_end * w["e_end"]
        dlast = _per_chunk(dke, lambda a: jnp.sum(a, axis=0, keepdims=True)) + dd * w["d"]
        dlf = _chunk_cumsum(dcum, not reverse) + dlast
        tt = dlf / w["f"] - dk
        dq_ref[...] = dq_dec * w["a"]
        dv_ref[...] = dv
        dz_ref[...] = ((1.0 - lb_all) * w["sig"] * w["sn"] * tt).astype(BF16)
        dlb_scr[...] += jnp.sum(w["sn"] * tt, axis=0, keepdims=True)

        @pl.when(jnp.logical_and(b == pl.num_programs(0) - 1, j == pl.num_programs(1) - 1))
        def _():
            d0 = dlb_scr[...] * lb_all * (1.0 - lb_all)
            dl_ref[0:1, :] = d0
            dl_ref[1:2, :] = -d0

    blk = pl.BlockSpec((rb, A_WIDTH), tmap)
    return pl.pallas_call(
        body, name=f"hgrn_bwd_{direction}", grid=(n_batch, nb),
        in_specs=[blk, blk, blk, blk, pl.BlockSpec((nc, HEAD_PAIRS, 128, 128), smap), _full((2, 2, A_WIDTH))],
        out_specs=[blk, blk, blk, _full((2, A_WIDTH))],
        out_shape=[jax.ShapeDtypeStruct((t, A_WIDTH), F32)] * 2
        + [jax.ShapeDtypeStruct((t, A_WIDTH), BF16), jax.ShapeDtypeStruct((2, A_WIDTH), F32)],
        scratch_shapes=[pltpu.VMEM((HEAD_PAIRS, 128, 128), F32), pltpu.VMEM((1, A_WIDTH), F32)],
        compiler_params=_params(2),
    )(hq, hi, hf, do, st, lbl)


def _swap_rope_halves(x):
    lane = _iota2(x.shape, 1)
    return jnp.where(lane < 32, pltpu.roll(x, 96, 1), pltpu.roll(x, 32, 1))


def _rms_fwd(xv, g):
    r = lax.rsqrt(jnp.mean(xv * xv, axis=-1, keepdims=True) + EPS)
    return (xv * r) * g


def _rms_bwd(dy, xv, g):
    r = lax.rsqrt(jnp.mean(xv * xv, axis=-1, keepdims=True) + EPS)
    xh = xv * r
    dxh = dy * g
    return r * (dxh - xh * jnp.mean(dxh * xh, axis=-1, keepdims=True)), dy * xh


def _mla_prep(cq, ckv, kr, g_qa, g_kva, w_q_p, w_kv_p, cosx, sinx, *, n_batch):
    t = cq.shape[0]
    s = t // n_batch
    tm = _row_tile(s)
    nt = s // tm

    def body(cq_ref, ckv_ref, kr_ref, gq_ref, gkv_ref, wq_ref, wkv_ref, cos_ref, sin_ref,
             q_ref, k_ref, v_ref, cqn_ref, ckvn_ref):
        cos, sin = cos_ref[...], sin_ref[...]
        cqn = _rms_fwd(cq_ref[...], gq_ref[...]).astype(BF16)
        ckvn = _rms_fwd(ckv_ref[...], gkv_ref[...]).astype(BF16)
        cqn_ref[...] = cqn
        ckvn_ref[...] = ckvn
        krv = kr_ref[...]
        kr_roped = (krv * cos + _swap_rope_halves(krv) * sin).astype(BF16)
        for h in range(B_HEADS):
            o = h * QK_PAD
            q_ref[:, o:o + 128] = _dot_nt(cqn, wq_ref[o:o + 128, :]).astype(BF16)
            qr = _dot_nt(cqn, wq_ref[o + 128:o + 256, :])
            q_ref[:, o + 128:o + 256] = (qr * cos + _swap_rope_halves(qr) * sin).astype(BF16)
            k_ref[:, o:o + 128] = _dot(ckvn, wkv_ref[:, h * 128:(h + 1) * 128]).astype(BF16)
            k_ref[:, o + 128:o + 256] = kr_roped
        v_ref[...] = _dot(ckvn, wkv_ref[:, 512:1024]).astype(BF16)

    tab = pl.BlockSpec((tm, 128), lambda i: (i % nt, 0))
    return pl.pallas_call(
        body, name="mla_prep", grid=(t // tm,),
        in_specs=[_rows(tm, Q_LORA), _rows(tm, KV_LORA), _rows(tm, 128), _full((1, Q_LORA)), _full((1, KV_LORA)),
                  _full((1024, Q_LORA)), _full((KV_LORA, 1024)), tab, tab],
        out_specs=[_rows(tm, 1024), _rows(tm, 1024), _rows(tm, 512), _rows(tm, Q_LORA), _rows(tm, KV_LORA)],
        out_shape=[jax.ShapeDtypeStruct((t, 1024), BF16), jax.ShapeDtypeStruct((t, 1024), BF16),
                   jax.ShapeDtypeStruct((t, 512), BF16), jax.ShapeDtypeStruct((t, Q_LORA), BF16),
                   jax.ShapeDtypeStruct((t, KV_LORA), BF16)],
        compiler_params=_params(1),
    )(cq, ckv, kr, g_qa, g_kva, w_q_p, w_kv_p, cosx, sinx)


def _mla_prep_bwd(dqt, dk, dv, cq, ckv, g_qa, g_kva, w_q_p, w_kv_p, cosx, sinx, *, n_batch):
    t = cq.shape[0]
    s = t // n_batch
    tm = _row_tile(s)
    nt = s // tm

    def body(dqt_ref, dk_ref, dv_ref, cq_ref, ckv_ref, gq_ref, gkv_ref, wq_ref, wkv_ref, cos_ref, sin_ref,
             dcq_ref, dckv_ref, dkr_ref, dqp_ref, dkvp_ref, dgq_ref, dgkv_ref):
        cos, sin = cos_ref[...], sin_ref[...]
        first = pl.program_id(0) == 0

        def unrope(d):
            return d * cos + _swap_rope_halves(d * sin)

        dcqn = None
        dkr = None
        dckvn = None
        for h in range(B_HEADS):
            o = h * QK_PAD
            dq_h = jnp.transpose(dqt_ref[o:o + QK_PAD, :])
            dqn16 = dq_h[:, 0:128].astype(BF16)
            dqr16 = unrope(dq_h[:, 128:256]).astype(BF16)
            dqp_ref[:, o:o + 128] = dqn16
            dqp_ref[:, o + 128:o + 256] = dqr16
            part = _dot(dqn16, wq_ref[o:o + 128, :]) + _dot(dqr16, wq_ref[o + 128:o + 256, :])
            dcqn = part if dcqn is None else dcqn + part
            dkn16 = dk_ref[:, o:o + 128].astype(BF16)
            dkvp_ref[:, h * 128:(h + 1) * 128] = dkn16
            part = _dot_nt(dkn16, wkv_ref[:, h * 128:(h + 1) * 128])
            dckvn = part if dckvn is None else dckvn + part
            kr_part = dk_ref[:, o + 128:o + 256]
            dkr = kr_part if dkr is None else dkr + kr_part
        dv16 = dv_ref[...].astype(BF16)
        dkvp_ref[:, 512:1024] = dv16
        dckvn = dckvn + _dot_nt(dv16, wkv_ref[:, 512:1024])
        dkr_ref[...] = unrope(dkr).astype(BF16)
        dcq, dgq = _rms_bwd(dcqn, cq_ref[...], gq_ref[...])
        dckv, dgkv = _rms_bwd(dckvn, ckv_ref[...], gkv_ref[...])
        dcq_ref[...] = dcq.astype(BF16)
        dckv_ref[...] = dckv.astype(BF16)
        _acc_rows(dgq_ref, dgq, first)
        _acc_rows(dgkv_ref, dgkv, first)

    tab = pl.BlockSpec((tm, 128), lambda i: (i % nt, 0))
    return pl.pallas_call(
        body, name="mla_prep_bwd", grid=(t // tm,),
        in_specs=[pl.BlockSpec((1024, tm), lambda i: (0, i)), _rows(tm, 1024), _rows(tm, 512), _rows(tm, Q_LORA),
                  _rows(tm, KV_LORA),
                  _full((1, Q_LORA)), _full((1, KV_LORA)), _full((1024, Q_LORA)), _full((KV_LORA, 1024)), tab, tab],
        out_specs=[_rows(tm, Q_LORA), _rows(tm, KV_LORA), _rows(tm, 128), _rows(tm, 1024), _rows(tm, 1024),
                   _full((1, Q_LORA)), _full((1, KV_LORA))],
        out_shape=[jax.ShapeDtypeStruct((t, Q_LORA), BF16), jax.ShapeDtypeStruct((t, KV_LORA), BF16),
                   jax.ShapeDtypeStruct((t, 128), BF16), jax.ShapeDtypeStruct((t, 1024), BF16),
                   jax.ShapeDtypeStruct((t, 1024), BF16), jax.ShapeDtypeStruct((1, Q_LORA), F32),
                   jax.ShapeDtypeStruct((1, KV_LORA), F32)],
        compiler_params=_params(1),
    )(dqt, dk, dv, cq, ckv, g_qa, g_kva, w_q_p, w_kv_p, cosx, sinx)


def _attn_dims(t, n_batch):
    s = t // n_batch
    tq = _pick(s, 1024, 128)
    return s, tq, s // tq


ATTN_ROWS = 256


def _grid_ends(n_axes):
    ids = [pl.program_id(a) for a in range(n_axes)]
    first = functools.reduce(jnp.logical_and, [i == 0 for i in ids])
    last = functools.reduce(jnp.logical_and, [i == pl.num_programs(a) - 1 for a, i in enumerate(ids)])
    return first, last


def _attn_fwd(q, k, v, wsrcs, *, n_batch):
    t = q.shape[0]
    s, tq, nq = _attn_dims(t, n_batch)
    nw = len(wsrcs)

    def body(q_ref, k_ref, v_ref, *refs):
        w_refs, (o_ref, lse_ref), wall_refs = refs[:nw], refs[nw:nw + 2], refs[nw + 2:2 * nw + 2]
        send_sems, recv_sems, local_sems = refs[2 * nw + 2:]
        first, last = _grid_ends(3)

        @pl.when(first)
        def _():
            for i in range(nw):
                _gather_start(w_refs[i], wall_refs[i], send_sems, recv_sems, local_sems.at[i], 3 * i)

        @pl.when(last)
        def _():
            for i in range(nw):
                _gather_wait(w_refs[i], wall_refs[i], send_sems, recv_sems, local_sems.at[i], 3 * i)

        kv, vv = k_ref[...], v_ref[...]
        groups = [slice(r, r + ATTN_ROWS) for r in range(0, tq, ATTN_ROWS)]
        raw = [_dot_nt(q_ref[g, :], kv) for g in groups]
        m = [jnp.max(a, axis=-1, keepdims=True) for a in raw]
        p = [jnp.exp2((a - b) * SCALE_LOG2E) for a, b in zip(raw, m)]
        l = [jnp.sum(a, axis=-1, keepdims=True) for a in p]
        for g, pg, mg, lg in zip(groups, p, m, l):
            o_ref[g, :] = _dot(pg.astype(BF16), vv) / lg
            lse2 = mg * SCALE_LOG2E + jnp.log(lg) * LOG2E
            lse_ref[:, g] = jnp.transpose(jnp.broadcast_to(lse2, (ATTN_ROWS, 128)))[0:1, :]

    return pl.pallas_call(
        body, name="attn_fwd", grid=(n_batch, B_HEADS, nq),
        in_specs=[pl.BlockSpec((tq, QK_PAD), lambda b, h, i: (b * nq + i, h)),
                  pl.BlockSpec((s, QK_PAD), lambda b, h, i: (b, h)),
                  pl.BlockSpec((s, B_V), lambda b, h, i: (b, h))] + [_HBM] * nw,
        out_specs=[pl.BlockSpec((tq, B_V), lambda b, h, i: (b * nq + i, h)),
                   pl.BlockSpec((None, 1, tq), lambda b, h, i: (h, 0, b * nq + i))] + [_HBM] * nw,
        out_shape=[jax.ShapeDtypeStruct((t, B_HEADS * B_V), F32), jax.ShapeDtypeStruct((B_HEADS, 1, t), F32)]
        + [jax.ShapeDtypeStruct((N_CHIPS,) + w.shape, w.dtype) for w in wsrcs],
        scratch_shapes=[pltpu.SemaphoreType.DMA((3 * nw,)), pltpu.SemaphoreType.DMA((3 * nw,)),
                        pltpu.SemaphoreType.DMA((nw,))],
        compiler_params=_params(3),
    )(q, k, v, *wsrcs)


def _attn_bwd(q, k, v, do16, lse, delta, gsegs, *, n_batch):
    t = q.shape[0]
    s = t // n_batch
    tk = _pick(s, 512, 128)
    nk = s // tk
    ng = len(gsegs)

    def body(q_ref, k_ref, v_ref, do_ref, lse_ref, dl_ref, *refs):
        g_refs, (dqt_ref, dk_ref, dv_ref), recv_refs = refs[:ng], refs[ng:ng + 3], refs[ng + 3:2 * ng + 3]
        send_sems, recv_sems = refs[2 * ng + 3:]
        first, last = _grid_ends(3)

        @pl.when(first)
        def _():
            for i in range(ng):
                _scatter_start(g_refs[i], recv_refs[i], send_sems, recv_sems, 3 * i)

        @pl.when(last)
        def _():
            for i in range(ng):
                _scatter_wait(g_refs[i], recv_refs[i], send_sems, recv_sems, 3 * i)

        j = pl.program_id(2)
        qv, kv, dov = q_ref[...], k_ref[...], do_ref[...]
        pt = jnp.exp2(_dot_nt(kv, qv) * SCALE_LOG2E - lse_ref[...])
        dv_ref[...] = _dot(pt.astype(BF16), dov)
        dpt = _dot_nt(v_ref[...], dov)
        dst = (pt * (dpt - dl_ref[...])).astype(BF16)
        dk_ref[...] = _dot(dst, qv) * ATTN_SCALE
        part = _dot_tn(kv, dst)

        @pl.when(j == 0)
        def _():
            dqt_ref[...] = part

        @pl.when(j != 0)
        def _():
            dqt_ref[...] += part

        @pl.when(j == nk - 1)
        def _():
            dqt_ref[...] = dqt_ref[...] * ATTN_SCALE

    row = pl.BlockSpec((None, 1, s), lambda b, h, j: (h, 0, b))
    return pl.pallas_call(
        body, name="attn_bwd", grid=(n_batch, B_HEADS, nk),
        in_specs=[pl.BlockSpec((s, QK_PAD), lambda b, h, j: (b, h)),
                  pl.BlockSpec((tk, QK_PAD), lambda b, h, j: (b * nk + j, h)),
                  pl.BlockSpec((tk, B_V), lambda b, h, j: (b * nk + j, h)),
                  pl.BlockSpec((s, B_V), lambda b, h, j: (b, h)), row, row] + [_HBM] * ng,
        out_specs=[pl.BlockSpec((QK_PAD, s), lambda b, h, j: (h, b)),
                   pl.BlockSpec((tk, QK_PAD), lambda b, h, j: (b * nk + j, h)),
                   pl.BlockSpec((tk, B_V), lambda b, h, j: (b * nk + j, h))] + [_HBM] * ng,
        out_shape=[jax.ShapeDtypeStruct((B_HEADS * QK_PAD, t), F32), jax.ShapeDtypeStruct((t, B_HEADS * QK_PAD), F32),
                   jax.ShapeDtypeStruct((t, B_HEADS * B_V), F32)]
        + [jax.ShapeDtypeStruct((3,) + g.shape[1:], g.dtype) for g in gsegs],
        scratch_shapes=[pltpu.SemaphoreType.DMA((3 * ng,)), pltpu.SemaphoreType.DMA((3 * ng,))],
        compiler_params=_params(3),
    )(q, k, v, do16, lse, delta, *gsegs)


def _group_ones16():
    r = _iota2((A_WIDTH, A_WIDTH), 0) // 64
    c = _iota2((A_WIDTH, A_WIDTH), 1) // 64
    return (r == c).astype(BF16)


def _head_rms(o, ones16):
    return lax.rsqrt(_exact_right(o * o, ones16) * (1.0 / 64.0) + EPS)


def _out_fwd(o_f, o_r, hg, o_attn, x, g_hn, g_mla, w_out, g2):
    t = x.shape[0]
    tm = _row_tile(t)

    def body(of_ref, or_ref, hg_ref, oa_ref, x_ref, ghn_ref, gm_ref, w_ref, g2_ref, y_ref, x2_ref, h2_ref):
        ones16 = _group_ones16()
        o = of_ref[...] + or_ref[...]
        hgv = hg_ref[...]
        ya = ((o * _head_rms(o, ones16)) * ghn_ref[...]) * (hgv * _sigmoid(hgv))
        yb = _rms_fwd(oa_ref[...], gm_ref[...])
        ya16, yb16 = ya.astype(BF16), yb.astype(BF16)
        y_ref[:, 0:A_WIDTH] = ya16
        y_ref[:, A_WIDTH:D_MODEL] = yb16
        x2 = x_ref[...] + _dot(ya16, w_ref[0:A_WIDTH, :]) + _dot(yb16, w_ref[A_WIDTH:D_MODEL, :])
        x2_ref[...] = x2
        h2_ref[...] = _rms_fwd(x2, g2_ref[...]).astype(BF16)

    a512 = _rows(tm, A_WIDTH)
    return pl.pallas_call(
        body, name="out_fwd", grid=(t // tm,),
        in_specs=[a512, a512, a512, a512, _rows(tm, D_MODEL), _full((1, A_WIDTH)), _full((1, A_WIDTH)),
                  _full((D_MODEL, D_MODEL)), _full((1, D_MODEL))],
        out_specs=[_rows(tm, D_MODEL)] * 3,
        out_shape=[jax.ShapeDtypeStruct((t, D_MODEL), BF16), jax.ShapeDtypeStruct((t, D_MODEL), F32),
                   jax.ShapeDtypeStruct((t, D_MODEL), BF16)],
        compiler_params=_params(1),
    )(o_f, o_r, hg, o_attn, x, g_hn, g_mla, w_out, g2)


def _out_bwd(dx2_16, o_f, o_r, hg, o_attn, g_hn, g_mla, w_out):
    t = dx2_16.shape[0]
    tm = _row_tile(t)

    def body(dx_ref, of_ref, or_ref, hg_ref, oa_ref, ghn_ref, gm_ref, w_ref,
             do_ref, dhg_ref, doa_ref, dl_ref, dghn_ref, dgm_ref):
        first = pl.program_id(0) == 0
        ones16 = _group_ones16()
        dxv = dx_ref[...]
        dya = _dot_nt(dxv, w_ref[0:A_WIDTH, :])
        dyb = _dot_nt(dxv, w_ref[A_WIDTH:D_MODEL, :])
        o = of_ref[...] + or_ref[...]
        rh = _head_rms(o, ones16)
        oh = o * rh
        hgv = hg_ref[...]
        sg = _sigmoid(hgv)
        sl = hgv * sg
        ghn = ghn_ref[...]
        dhg_ref[...] = ((dya * (oh * ghn)) * (sg * (1.0 + hgv * (1.0 - sg)))).astype(BF16)
        _acc_rows(dghn_ref, dya * sl * oh, first)
        doh = dya * sl * ghn
        do_ref[...] = rh * (doh - oh * (_exact_right(doh * oh, ones16) * (1.0 / 64.0)))
        oav = oa_ref[...]
        doa, dgm = _rms_bwd(dyb, oav, gm_ref[...])
        doa_ref[...] = doa.astype(BF16)
        _acc_rows(dgm_ref, dgm, first)
        sel16 = (_iota2((8, A_WIDTH), 0) == _iota2((8, A_WIDTH), 1) // B_V).astype(BF16)
        x1, x2, x3 = _split3(doa * oav)
        delta = _dot_nt(sel16, x1) + _dot_nt(sel16, x2) + _dot_nt(sel16, x3)
        for h in range(B_HEADS):
            dl_ref[h] = delta[h:h + 1, :]

    a512 = _rows(tm, A_WIDTH)
    return pl.pallas_call(
        body, name="out_bwd", grid=(t // tm,),
        in_specs=[_rows(tm, D_MODEL), a512, a512, a512, a512, _full((1, A_WIDTH)), _full((1, A_WIDTH)),
                  _full((D_MODEL, D_MODEL))],
        out_specs=[a512, a512, a512, pl.BlockSpec((B_HEADS, 1, tm), lambda i: (0, 0, i)),
                   _full((1, A_WIDTH)), _full((1, A_WIDTH))],
        out_shape=[jax.ShapeDtypeStruct((t, A_WIDTH), F32)] + [jax.ShapeDtypeStruct((t, A_WIDTH), BF16)] * 2
        + [jax.ShapeDtypeStruct((B_HEADS, 1, t), F32)]
        + [jax.ShapeDtypeStruct((1, A_WIDTH), F32)] * 2,
        compiler_params=_params(1),
    )(dx2_16, o_f, o_r, hg, o_attn, g_hn, g_mla, w_out)


def _ffn_fwd_bwd(h2, x2, target, w_gate, w_up, w_down, g_f, g2):
    t = x2.shape[0]
    tm = _row_tile(t)
    inv_d = 1.0 / D_MODEL

    def body(h2_ref, x2_ref, tg_ref, wg_ref, wu_ref, wd_ref, gf_ref, g2_ref,
             act_ref, dgate_ref, dup_ref, dx3_ref, dx2_ref, dx2h_ref, loss_ref, dgf_ref, dg2_ref):
        first = pl.program_id(0) == 0
        h2v = h2_ref[...]
        gate = _dot_nt(h2v, wg_ref[...])
        up = _dot_nt(h2v, wu_ref[...])
        sg = _sigmoid(gate)
        sl = gate * sg
        act16 = (sl * up).astype(BF16)
        act_ref[...] = act16
        x2v = x2_ref[...]
        x3 = x2v + _dot(act16, wd_ref[...])
        r3 = lax.rsqrt(jnp.mean(x3 * x3, axis=-1, keepdims=True) + EPS)
        x3h = x3 * r3
        gf = gf_ref[...]
        err = x3h * gf - tg_ref[...]
        part = 0.5 * jnp.sum(jnp.mean(err * err, axis=-1, keepdims=True), axis=0, keepdims=True)

        @pl.when(first)
        def _():
            loss_ref[...] = jnp.zeros_like(loss_ref)

        loss_ref[...] += part
        dy = err * inv_d
        _acc_rows(dgf_ref, dy * x3h, first)
        dx3h = dy * gf
        dx3 = r3 * (dx3h - x3h * jnp.mean(dx3h * x3h, axis=-1, keepdims=True))
        dx3_16 = dx3.astype(BF16)
        dx3_ref[...] = dx3_16
        da = _dot_nt(dx3_16, wd_ref[...])
        dup16 = (da * sl).astype(BF16)
        dgate16 = (da * up * (sg * (1.0 + gate * (1.0 - sg)))).astype(BF16)
        dup_ref[...] = dup16
        dgate_ref[...] = dgate16
        dh2 = _dot(dgate16, wg_ref[...]) + _dot(dup16, wu_ref[...])
        dx2n, dg2 = _rms_bwd(dh2, x2v, g2_ref[...])
        _acc_rows(dg2_ref, dg2, first)
        dx2 = dx3 + dx2n
        dx2_ref[...] = dx2
        dx2h_ref[...] = dx2.astype(BF16)

    return pl.pallas_call(
        body, name="ffn_fwd_bwd", grid=(t // tm,),
        in_specs=[_rows(tm, D_MODEL), _rows(tm, D_MODEL), _rows(tm, D_MODEL), _full((D_FF, D_MODEL), True),
                  _full((D_FF, D_MODEL), True), _full((D_FF, D_MODEL), True), _full((1, D_MODEL)), _full((1, D_MODEL))],
        out_specs=[_rows(tm, D_FF), _rows(tm, D_FF), _rows(tm, D_FF), _rows(tm, D_MODEL), _rows(tm, D_MODEL),
                   _rows(tm, D_MODEL), _full((8, 128)), _full((1, D_MODEL)), _full((1, D_MODEL))],
        out_shape=[jax.ShapeDtypeStruct((t, D_FF), BF16)] * 3
        + [jax.ShapeDtypeStruct((t, D_MODEL), BF16), jax.ShapeDtypeStruct((t, D_MODEL), F32),
           jax.ShapeDtypeStruct((t, D_MODEL), BF16), jax.ShapeDtypeStruct((8, 128), F32),
           jax.ShapeDtypeStruct((1, D_MODEL), F32), jax.ShapeDtypeStruct((1, D_MODEL), F32)],
        compiler_params=_params(1),
    )(h2, x2, target, w_gate, w_up, w_down, g_f, g2)


def _wgrad(a, b, name):
    t, m = a.shape
    n = b.shape[1]
    bm = _pick(m, 1664)
    bn = _pick(n, 1664)
    tk = _pick(t, 2048, 16)
    nk = t // tk

    def body(a_ref, b_ref, o_ref):
        part = _dot_tn(a_ref[...], b_ref[...])

        @pl.when(pl.program_id(2) == 0)
        def _():
            o_ref[...] = part

        @pl.when(pl.program_id(2) != 0)
        def _():
            o_ref[...] += part

    return pl.pallas_call(
        body, name=name, grid=(m // bm, n // bn, nk),
        in_specs=[pl.BlockSpec((tk, bm), lambda i, j, k: (k, i)), pl.BlockSpec((tk, bn), lambda i, j, k: (k, j))],
        out_specs=pl.BlockSpec((bm, bn), lambda i, j, k: (i, j)),
        out_shape=jax.ShapeDtypeStruct((m, n), F32),
        compiler_params=_params(3),
    )(a, b)


def _rope_tables(seq):
    inv = 1.0 / (ROPE_THETA ** (jnp.arange(0, B_ROPE, 2, dtype=F32) / B_ROPE))
    ang = jnp.arange(seq, dtype=F32)[:, None] * inv[None, :]
    cos, sin = jnp.cos(ang), jnp.sin(ang)
    zeros = jnp.zeros((seq, 64), F32)
    return jnp.concatenate([cos, cos, zeros], axis=1), jnp.concatenate([-sin, sin, zeros], axis=1)


def _pad_weights(w_in_t, w_q_t, w_kv_b):
    w_in_p = jnp.pad(w_in_t, ((0, D_IN_PAD - D_IN), (0, 0)))
    w_q_p = jnp.pad(w_q_t.reshape(B_HEADS, B_NOPE + B_ROPE, Q_LORA), ((0, 0), (0, 64), (0, 0))).reshape(1024, Q_LORA)
    kv = w_kv_b.reshape(KV_LORA, B_HEADS, B_NOPE + B_V)
    w_kv_p = jnp.concatenate([kv[:, :, :B_NOPE].reshape(KV_LORA, 512), kv[:, :, B_NOPE:].reshape(KV_LORA, 512)], axis=1)
    return w_in_p, w_q_p, w_kv_p


def _unpad_grads(g_in_p, g_q_p, g_kv_p):
    g_in = g_in_p[:D_IN]
    g_q = g_q_p.reshape(B_HEADS, QK_PAD, Q_LORA)[:, :B_NOPE + B_ROPE].reshape(B_HEADS * (B_NOPE + B_ROPE), Q_LORA)
    g_kv = jnp.concatenate([g_kv_p[:, :512].reshape(KV_LORA, B_HEADS, B_NOPE),
                            g_kv_p[:, 512:].reshape(KV_LORA, B_HEADS, B_V)], axis=2).reshape(KV_LORA, 1024)
    return g_in, g_q, g_kv


def _local_step(x, target, lbl, g1, g_hn, g_qa, g_kva, g_mla, g2, g_f, w_in, w_q_b, w_kv_b, late_shards):
    n_batch, seq, _ = x.shape
    t = n_batch * seq
    x = x.reshape(t, D_MODEL)
    target = target.reshape(t, D_MODEL)
    w_in_p, w_q_p, w_kv_p = _pad_weights(w_in, w_q_b, w_kv_b)
    cosx, sinx = _rope_tables(seq)

    h1, hq, hi, hff, hfb, hg, cq, ckv, kr = _in_fwd(x, g1, w_in_p)
    o_f, st_f = _hgrn_fwd(hq, hi, hff, lbl, n_batch=n_batch, direction=0)
    o_r, st_r = _hgrn_fwd(hq, hi, hfb, lbl, n_batch=n_batch, direction=1)
    q, k, v, cqn, ckvn = _mla_prep(cq, ckv, kr, g_qa, g_kva, w_q_p, w_kv_p, cosx, sinx, n_batch=n_batch)
    o_attn, lse, *gathered = _attn_fwd(q, k, v, late_shards, n_batch=n_batch)
    w_out, w_gate, w_up, w_down = (_join_shards(n, a) for n, a in zip(_LATE, gathered))
    ycat, x2, h2 = _out_fwd(o_f, o_r, hg, o_attn, x, g_hn, g_mla, w_out, g2)
    act, dgate, dup, dx3_16, dx2, dx2_16, loss, dg_f, dg2 = _ffn_fwd_bwd(h2, x2, target, w_gate, w_up, w_down, g_f, g2)
    g_late = dict(w_out=_wgrad(ycat, dx2_16, "wgrad_out"), w_gate=_wgrad(dgate, h2, "wgrad_gate"),
                  w_up=_wgrad(dup, h2, "wgrad_up"), w_down=_wgrad(act, dx3_16, "wgrad_down"))
    do_h, dhg, do_attn16, delta, dg_hn, dg_mla = _out_bwd(dx2_16, o_f, o_r, hg, o_attn, g_hn, g_mla, w_out)
    dqt, dk, dv, *recv_late = _attn_bwd(q, k, v, do_attn16, lse, delta,
                                        [_segments(n, g_late[n]).astype(BF16) for n in _LATE], n_batch=n_batch)
    dcq, dckv, dkr, dqp16, dkvp16, dg_qa, dg_kva = _mla_prep_bwd(dqt, dk, dv, cq, ckv, g_qa, g_kva, w_q_p, w_kv_p,
                                                                  cosx, sinx, n_batch=n_batch)
    dq_f, dv_f, dz_f, dl_f = _hgrn_bwd(hq, hi, hff, do_h, st_f, lbl, n_batch=n_batch, direction=0)
    dq_r, dv_r, dz_r, dl_r = _hgrn_bwd(hq, hi, hfb, do_h, st_r, lbl, n_batch=n_batch, direction=1)
    dx, dproj16, dg1 = _in_bwd(dq_f, dq_r, dv_f, dv_r, dz_f, dz_r, dhg, hq, dcq, dckv, dkr, dx2, x, g1, w_in_p)

    gw_in, gw_q, gw_kv = _unpad_grads(_wgrad(dproj16, h1, "wgrad_in"), _wgrad(dqp16, cqn, "wgrad_q_b"),
                                      _wgrad(ckvn, dkvp16, "wgrad_kv_b"))
    grads = dict(
        norm1_g=dg1, w_in=gw_in, lb_logits=jnp.stack([dl_f, dl_r]), hgrn_norm_g=dg_hn, q_a_norm_g=dg_qa,
        w_q_b=gw_q, kv_a_norm_g=dg_kva, w_kv_b=gw_kv, mla_norm_g=dg_mla, norm2_g=dg2, final_norm_g=dg_f)
    grads.update(g_late)
    return loss[0, 0], dx.reshape(n_batch, seq, D_MODEL), grads, dict(zip(_LATE, recv_late))


_HBM = pl.BlockSpec(memory_space=pltpu.HBM)
_MESH = pl.DeviceIdType.MESH


def _place():
    x, y, c = lax.axis_index("x"), lax.axis_index("y"), lax.axis_index("c")
    other_chips = [(1 - x, y), (x, 1 - y), (1 - x, 1 - y)]
    return x, y, c, other_chips


def _gather_copies(w_ref, wall_ref, send_sems, recv_sems, local_sem, base=0):
    x, y, c, chips = _place()
    mine = 2 * x + y

    def mk(j, chip_index, to):
        return pltpu.make_async_remote_copy(src_ref=w_ref, dst_ref=wall_ref.at[chip_index], send_sem=send_sems.at[base + j],
                                            recv_sem=recv_sems.at[base + j], device_id=to, device_id_type=_MESH)

    local = pltpu.make_async_copy(w_ref, wall_ref.at[mine], local_sem)
    sends = [mk(j, mine, (*chip, c)) for j, chip in enumerate(chips)]
    recvs = [mk(j, 2 * px + py, (x, y, c)) for j, (px, py) in enumerate(chips)]
    return local, sends, recvs


def _gather_start(*refs):
    local, sends, _ = _gather_copies(*refs)
    local.start()
    for cp in sends:
        cp.start()


def _gather_wait(*refs):
    local, sends, recvs = _gather_copies(*refs)
    for cp in recvs:
        cp.wait_recv()
    for cp in sends:
        cp.wait_send()
    local.wait()


def _scatter_copies(g_ref, recv_ref, send_sems, recv_sems, base=0):
    x, y, c, chips = _place()

    def mk(j, src_index, to):
        return pltpu.make_async_remote_copy(src_ref=g_ref.at[src_index], dst_ref=recv_ref.at[j],
                                            send_sem=send_sems.at[base + j], recv_sem=recv_sems.at[base + j],
                                            device_id=to, device_id_type=_MESH)

    sends = [mk(j, 2 * px + py, (px, py, c)) for j, (px, py) in enumerate(chips)]
    recvs = [mk(j, 0, (x, y, c)) for j in range(3)]
    return sends, recvs


def _scatter_start(*refs):
    for cp in _scatter_copies(*refs)[0]:
        cp.start()


def _scatter_wait(*refs):
    sends, recvs = _scatter_copies(*refs)
    for cp in recvs:
        cp.wait_recv()
    for cp in sends:
        cp.wait_send()


def _gather_shards(srcs):
    n = len(srcs)

    def body(*refs):
        src_refs, dst_refs = refs[:n], refs[n:2 * n]
        send_sems, recv_sems, local_sems = refs[2 * n:]
        for i in range(n):
            _gather_start(src_refs[i], dst_refs[i], send_sems, recv_sems, local_sems.at[i], 3 * i)
        for i in range(n):
            _gather_wait(src_refs[i], dst_refs[i], send_sems, recv_sems, local_sems.at[i], 3 * i)

    return pl.pallas_call(
        body, name="gather_shards", in_specs=[_HBM] * n, out_specs=[_HBM] * n,
        out_shape=[jax.ShapeDtypeStruct((N_CHIPS,) + a.shape, a.dtype) for a in srcs],
        scratch_shapes=[pltpu.SemaphoreType.DMA((3 * n,)), pltpu.SemaphoreType.DMA((3 * n,)), pltpu.SemaphoreType.DMA((n,))],
    )(*srcs)


def _scatter_grads(gsegs, small):
    n = len(gsegs)

    def body(*refs):
        g_refs, s_ref, recv_refs, sall_ref = refs[:n], refs[n], refs[n + 1:2 * n + 1], refs[2 * n + 1]
        send_sems, recv_sems, local_sem = refs[2 * n + 2:]
        x, y, c, _ = _place()
        me = 4 * x + 2 * y + c
        flips = [(fx, fy, fc) for fx in (0, 1) for fy in (0, 1) for fc in (0, 1)][1:]

        def peer(f):
            return tuple((1 - a) if b else a for a, b in zip((x, y, c), f))

        def sm(r, index, to):
            return pltpu.make_async_remote_copy(src_ref=s_ref, dst_ref=sall_ref.at[index], send_sem=send_sems.at[3 * n + r],
                                                recv_sem=recv_sems.at[3 * n + r], device_id=to, device_id_type=_MESH)

        local = pltpu.make_async_copy(s_ref, sall_ref.at[me], local_sem)
        local.start()
        for i in range(n):
            _scatter_start(g_refs[i], recv_refs[i], send_sems, recv_sems, 3 * i)
        sends = [sm(r, me, peer(f)) for r, f in enumerate(flips)]
        for cp in sends:
            cp.start()
        for i in range(n):
            _scatter_wait(g_refs[i], recv_refs[i], send_sems, recv_sems, 3 * i)
        for r, f in enumerate(flips):
            px, py, pc = peer(f)
            sm(r, 4 * px + 2 * py + pc, (x, y, c)).wait_recv()
        for cp in sends:
            cp.wait_send()
        local.wait()

    return pl.pallas_call(
        body, name="scatter_grads", in_specs=[_HBM] * (n + 1), out_specs=[_HBM] * (n + 1),
        out_shape=[jax.ShapeDtypeStruct((3,) + g.shape[1:], g.dtype) for g in gsegs]
        + [jax.ShapeDtypeStruct((8,) + small.shape, F32)],
        scratch_shapes=[pltpu.SemaphoreType.DMA((3 * n + 7,)), pltpu.SemaphoreType.DMA((3 * n + 7,)),
                        pltpu.SemaphoreType.DMA(())],
    )(*gsegs, small)


def _swap_sibling(parts):
    n = len(parts)

    def body(*refs):
        send_sems, recv_sems = refs[2 * n:]
        x, y, c, _ = _place()
        cps = [pltpu.make_async_remote_copy(src_ref=refs[i], dst_ref=refs[n + i], send_sem=send_sems.at[i],
                                            recv_sem=recv_sems.at[i], device_id=(x, y, 1 - c), device_id_type=_MESH)
               for i in range(n)]
        for cp in cps:
            cp.start()
        for cp in cps:
            cp.wait()

    return pl.pallas_call(
        body, name="swap_sibling", in_specs=[_HBM] * n, out_specs=[_HBM] * n,
        out_shape=[jax.ShapeDtypeStruct(p.shape, p.dtype) for p in parts],
        scratch_shapes=[pltpu.SemaphoreType.DMA((n,)), pltpu.SemaphoreType.DMA((n,))],
    )(*parts)


def _sum_segments(own, recv, name):
    rows, cols = own.shape
    tm = _pick(rows, 512, 16)

    def body(o_ref, r_ref, out_ref):
        acc = o_ref[...]
        for j in range(3):
            acc = acc + r_ref[j].astype(F32)
        out_ref[...] = acc

    return pl.pallas_call(
        body, name=name, grid=(rows // tm,),
        in_specs=[_rows(tm, cols), pl.BlockSpec((3, tm, cols), lambda i: (0, i, 0))],
        out_specs=_rows(tm, cols), out_shape=jax.ShapeDtypeStruct(own.shape, F32), compiler_params=_params(1),
    )(own, recv)


def _sum_devices(sall):
    def body(s_ref, o_ref):
        acc = s_ref[0]
        for d in range(1, 8):
            acc = acc + s_ref[d]
        o_ref[...] = acc

    return pl.pallas_call(body, name="sum_devices", out_shape=jax.ShapeDtypeStruct(sall.shape[1:], F32))(sall)


def _adamw(w, m, v, ga, gb, name):
    rows, cols = w.shape
    tm = _row_tile(rows)
    two = gb is not None

    def body(*refs):
        w_ref, m_ref, v_ref, ga_ref = refs[:4]
        g_ref, d_ref, m2_ref, v2_ref = refs[-4:]
        g = ga_ref[...] + refs[4][...] if two else ga_ref[...]
        m2 = ADAM_B1 * m_ref[...] + (1.0 - ADAM_B1) * g
        v2 = ADAM_B2 * v_ref[...] + (1.0 - ADAM_B2) * (g * g)
        m_hat = m2 / (1.0 - ADAM_B1 ** ADAM_STEP)
        v_hat = v2 / (1.0 - ADAM_B2 ** ADAM_STEP)
        g_ref[...] = g
        d_ref[...] = -ADAM_LR * (m_hat / (jnp.sqrt(v_hat) + ADAM_EPS) + ADAM_WD * w_ref[...])
        m2_ref[...] = m2
        v2_ref[...] = v2

    blk = _rows(tm, cols)
    args = (w, m, v, ga) + ((gb,) if two else ())
    return pl.pallas_call(
        body, name=name, grid=(rows // tm,), in_specs=[blk] * len(args), out_specs=[blk] * 4,
        out_shape=[jax.ShapeDtypeStruct(w.shape, F32)] * 4, compiler_params=_params(1),
    )(*args)


_TRANSPOSED = ("w_in", "w_q_b", "w_gate", "w_up")
_COLUMN_SHARDED = ("w_kv_b",)
_FULL_SHAPES = dict(w_in=(D_IN, D_MODEL), w_q_b=(768, Q_LORA), w_kv_b=(KV_LORA, 1024), w_out=(D_MODEL, D_MODEL),
                    w_gate=(D_FF, D_MODEL), w_up=(D_FF, D_MODEL), w_down=(D_FF, D_MODEL))
_SMALL = (("norm1_g", 1024), ("lb_logits", 2048), ("hgrn_norm_g", 512), ("q_a_norm_g", 384), ("kv_a_norm_g", 256),
          ("mla_norm_g", 512), ("norm2_g", 1024), ("final_norm_g", 1024))
_UPDATE_ROWS = 48


def _pad_rows(a, rows):
    return jnp.pad(a, ((0, rows - a.shape[0]), (0, 0)))


_EARLY = ("w_in", "w_q_b", "w_kv_b")
_LATE = ("w_out", "w_gate", "w_up", "w_down")


def _segments(name, g):
    r, c = g.shape
    if name in _COLUMN_SHARDED:
        return g.reshape(r, N_CHIPS, c // N_CHIPS).transpose(1, 0, 2)
    return g.reshape(N_CHIPS, r // N_CHIPS, c)


def _own_segment(name, g, chip):
    r, c = g.shape
    if name in _COLUMN_SHARDED:
        return lax.dynamic_slice_in_dim(g, chip * (c // N_CHIPS), c // N_CHIPS, axis=1)
    return lax.dynamic_slice_in_dim(g, chip * (r // N_CHIPS), r // N_CHIPS, axis=0)


def _join_shards(name, seg):
    r, c = _FULL_SHAPES[name]
    if name in _COLUMN_SHARDED:
        return seg.transpose(1, 0, 2).reshape(r, c)
    return seg.reshape(r, c)


def kernel(x, norm1_g, w_in, lb_logits, hgrn_norm_g, q_a_norm_g, w_q_b, kv_a_norm_g, w_kv_b, mla_norm_g, w_out, norm2_g, w_gate, w_up, w_down, final_norm_g, loss_target, m_norm1_g, m_w_in, m_lb_logits, m_hgrn_norm_g, m_q_a_norm_g, m_w_q_b, m_kv_a_norm_g, m_w_kv_b, m_mla_norm_g, m_w_out, m_norm2_g, m_w_gate, m_w_up, m_w_down, m_final_norm_g, v_norm1_g, v_w_in, v_lb_logits, v_hgrn_norm_g, v_q_a_norm_g, v_w_q_b, v_kv_a_norm_g, v_w_kv_b, v_mla_norm_g, v_w_out, v_norm2_g, v_w_gate, v_w_up, v_w_down, v_final_norm_g):
    names = ("norm1_g", "w_in", "lb_logits", "hgrn_norm_g", "q_a_norm_g", "w_q_b", "kv_a_norm_g", "w_kv_b", "mla_norm_g",
             "w_out", "norm2_g", "w_gate", "w_up", "w_down", "final_norm_g")
    w = dict(zip(names, (norm1_g, w_in, lb_logits, hgrn_norm_g, q_a_norm_g, w_q_b, kv_a_norm_g, w_kv_b, mla_norm_g,
                         w_out, norm2_g, w_gate, w_up, w_down, final_norm_g)))
    m = dict(zip(names, (m_norm1_g, m_w_in, m_lb_logits, m_hgrn_norm_g, m_q_a_norm_g, m_w_q_b, m_kv_a_norm_g, m_w_kv_b,
                         m_mla_norm_g, m_w_out, m_norm2_g, m_w_gate, m_w_up, m_w_down, m_final_norm_g)))
    v = dict(zip(names, (v_norm1_g, v_w_in, v_lb_logits, v_hgrn_norm_g, v_q_a_norm_g, v_w_q_b, v_kv_a_norm_g, v_w_kv_b,
                         v_mla_norm_g, v_w_out, v_norm2_g, v_w_gate, v_w_up, v_w_down, v_final_norm_g)))
    matrices = _EARLY + _LATE
    chip = 2 * lax.axis_index("x") + lax.axis_index("y")

    def shard2d(a, n):
        return jnp.swapaxes(a[0], 0, 1) if n in _TRANSPOSED else a[0]

    def unshard2d(a, n):
        return (jnp.swapaxes(a, 0, 1) if n in _TRANSPOSED else a)[None]

    w16 = {n: shard2d(w[n], n).astype(BF16) for n in matrices}
    lb8 = _pad_rows(lb_logits.reshape(4, 128), 8)
    *early, lball = _gather_shards([w16[n] for n in _EARLY] + [lb8])
    lbl = lball[:, :4].reshape(N_CHIPS, 2, 2, 128).transpose(1, 2, 0, 3).reshape(2, 2, A_WIDTH)

    loss_part, grad_x, g, recv = _local_step(
        x, loss_target, lbl, norm1_g, hgrn_norm_g, q_a_norm_g, kv_a_norm_g, mla_norm_g, norm2_g, final_norm_g[None, :],
        *(_join_shards(n, a) for n, a in zip(_EARLY, early)), [w16[n] for n in _LATE])
    loss = lax.psum(loss_part, ("x", "y", "c"))

    small = _pad_rows(jnp.concatenate([g[n].reshape(-1) for n, _ in _SMALL]).reshape(-1, 128), SMALL_ROWS)
    *recv_early, small_all = _scatter_grads([_segments(n, g[n]).astype(BF16) for n in _EARLY], small)
    recv.update(zip(_EARLY, recv_early))
    parts = [_sum_segments(_own_segment(n, g[n], chip), recv[n], f"sum_{n}") for n in matrices]
    sibs = _swap_sibling(parts)
    small_sum = _sum_devices(small_all).reshape(-1)

    out = {}
    for n, part, sib in zip(matrices, parts, sibs):
        res = _adamw(shard2d(w[n], n), shard2d(m[n], n), shard2d(v[n], n), part, sib, f"adamw_{n}")
        out[n] = tuple(unshard2d(r, n) for r in res)

    small_g, off = {}, 0
    for n, size in _SMALL:
        small_g[n] = small_sum[off:off + size]
        off += size
    small_g["lb_logits"] = lax.dynamic_slice_in_dim(small_g["lb_logits"].reshape(2, 2, A_WIDTH), chip * 128, 128, axis=2)
    small_names = tuple(n for n, _ in _SMALL)

    def pack(d):
        return _pad_rows(jnp.concatenate([d[n].reshape(-1) for n in small_names]).reshape(-1, 128), _UPDATE_ROWS)

    res = _adamw(pack(w), pack(m), pack(v), pack(small_g), None, "adamw_small")
    off = 0
    flat = [r.reshape(-1) for r in res]
    for n in small_names:
        size = w[n].size
        out[n] = tuple(f[off:off + size].reshape(w[n].shape) for f in flat)
        off += size

    return (loss, grad_x) + tuple(out[n][i] for i in range(4) for n in names)
```

```python
import functools

import jax
import jax.numpy as jnp
from jax import lax
from jax.experimental import pallas as pl
from jax.experimental.pallas import tpu as pltpu

F32 = jnp.float32
BF16 = jnp.bfloat16

D_MODEL = 1024
A_WIDTH = 512
HEAD_PAIRS = 4
CHUNK = 64
B_HEADS = 4
B_NOPE = 128
B_ROPE = 64
B_V = 128
QK_PAD = 256
Q_LORA = 384
KV_LORA = 256
D_FF = 2816
D_IN = 3264
D_IN_PAD = 3328
IN_WIDTHS = (512, 512, 512, 512, 512, Q_LORA, KV_LORA, 128)
ROPE_THETA = 10000.0
EPS = 1e-6
ATTN_SCALE = (B_NOPE + B_ROPE) ** -0.5
LOG2E = 1.4426950408889634
SCALE_LOG2E = ATTN_SCALE * LOG2E

ADAM_LR = 0.001
ADAM_B1 = 0.9
ADAM_B2 = 0.999
ADAM_EPS = 1e-08
ADAM_WD = 0.01
ADAM_STEP = 10

VMEM_LIMIT_BYTES = 60 * 1024 * 1024
N_CHIPS = 4
SMALL_ROWS = 56


def _params(n_axes):
    return pltpu.CompilerParams(dimension_semantics=("arbitrary",) * n_axes,
                                vmem_limit_bytes=VMEM_LIMIT_BYTES)


def _dot(a, b):
    return jnp.dot(a, b, preferred_element_type=F32)


def _dot_nt(a, b):
    return lax.dot_general(a, b, (((1,), (1,)), ((), ())), preferred_element_type=F32)


def _dot_tn(a, b):
    return lax.dot_general(a, b, (((0,), (0,)), ((), ())), preferred_element_type=F32)


def _split3(x):
    x1 = x.astype(BF16)
    r = x - x1.astype(F32)
    x2 = r.astype(BF16)
    x3 = (r - x2.astype(F32)).astype(BF16)
    return x1, x2, x3


def _exact_left(m16, x):
    x1, x2, x3 = _split3(x)
    return _dot(m16, x1) + _dot(m16, x2) + _dot(m16, x3)


def _exact_right(x, m16):
    x1, x2, x3 = _split3(x)
    return _dot(x1, m16) + _dot(x2, m16) + _dot(x3, m16)


def _iota2(shape, dim):
    return lax.broadcasted_iota(jnp.int32, shape, dim)


def _sigmoid(x):
    return jax.nn.sigmoid(x)


def _pick(dim, cap, mult=128):
    if dim <= cap:
        return dim
    best = None
    for d in range(mult, cap + 1, mult):
        if dim % d == 0:
            best = d
    assert best is not None, (dim, cap, mult)
    return best


def _row_tile(t, cap=256):
    return _pick(t, cap, 8)


def _full(shape, single=False):
    if single:
        return pl.BlockSpec(shape, lambda *_: (0,) * len(shape), pipeline_mode=pl.Buffered(1))
    return pl.BlockSpec(shape, lambda *_: (0,) * len(shape))


def _rows(tm, width):
    return pl.BlockSpec((tm, width), lambda i: (i, 0))


def _acc_rows(ref, val, first):
    s = jnp.sum(val, axis=0, keepdims=True)

    @pl.when(first)
    def _():
        ref[...] = s

    @pl.when(jnp.logical_not(first))
    def _():
        ref[...] += s


def _in_fwd(x, g1, w_in_p):
    t = x.shape[0]
    tm = _row_tile(t)

    def body(x_ref, g_ref, w_ref, h_ref, *outs):
        xv = x_ref[...]
        r = lax.rsqrt(jnp.mean(xv * xv, axis=-1, keepdims=True) + EPS)
        h = ((xv * r) * g_ref[...]).astype(BF16)
        h_ref[...] = h
        off = 0
        for o_ref, w in zip(outs, IN_WIDTHS):
            o_ref[...] = _dot_nt(h, w_ref[off:off + w, :])
            off += w

    return pl.pallas_call(
        body, name="in_fwd", grid=(t // tm,),
        in_specs=[_rows(tm, D_MODEL), _full((1, D_MODEL)), _full((D_IN_PAD, D_MODEL))],
        out_specs=[_rows(tm, D_MODEL)] + [_rows(tm, w) for w in IN_WIDTHS],
        out_shape=[jax.ShapeDtypeStruct((t, D_MODEL), BF16)]
        + [jax.ShapeDtypeStruct((t, w), F32) for w in IN_WIDTHS],
        compiler_params=_params(1),
    )(x, g1, w_in_p)


def _in_bwd(dq_f, dq_r, dv_f, dv_r, dz_f, dz_r, dhg, hq, dcq, dckv, dkr, dx2, x, g1, w_in_p):
    t = x.shape[0]
    tm = _row_tile(t)

    def body(dqf_ref, dqr_ref, dvf_ref, dvr_ref, dzf_ref, dzr_ref, dhg_ref, hq_ref, dcq_ref, dckv_ref,
             dkr_ref, dx2_ref, x_ref, g_ref, w_ref, dx_ref, dp_ref, dg_ref):
        hqv = hq_ref[...]
        sg = _sigmoid(hqv)
        dhq = (dqf_ref[...] + dqr_ref[...]) * (sg * (1.0 + hqv * (1.0 - sg)))
        pieces = (dhq, dvf_ref[...] + dvr_ref[...], dzf_ref[...], dzr_ref[...], dhg_ref[...],
                  dcq_ref[...], dckv_ref[...], dkr_ref[...])
        dh = None
        off = 0
        for p, w in zip(pieces, IN_WIDTHS):
            p16 = p.astype(BF16)
            dp_ref[:, off:off + w] = p16
            part = _dot(p16, w_ref[off:off + w, :])
            dh = part if dh is None else dh + part
            off += w
        xv = x_ref[...]
        r = lax.rsqrt(jnp.mean(xv * xv, axis=-1, keepdims=True) + EPS)
        xh = xv * r
        _acc_rows(dg_ref, dh * xh, pl.program_id(0) == 0)
        dxh = dh * g_ref[...]
        dx_ref[...] = dx2_ref[...] + r * (dxh - xh * jnp.mean(dxh * xh, axis=-1, keepdims=True))

    a512 = _rows(tm, A_WIDTH)
    return pl.pallas_call(
        body, name="in_bwd", grid=(t // tm,),
        in_specs=[a512] * 8 + [_rows(tm, Q_LORA), _rows(tm, KV_LORA), _rows(tm, 128), _rows(tm, D_MODEL),
                               _rows(tm, D_MODEL), _full((1, D_MODEL)), _full((D_IN_PAD, D_MODEL))],
        out_specs=[_rows(tm, D_MODEL), _rows(tm, D_IN_PAD), _full((1, D_MODEL))],
        out_shape=[jax.ShapeDtypeStruct((t, D_MODEL), F32), jax.ShapeDtypeStruct((t, D_IN_PAD), BF16),
                   jax.ShapeDtypeStruct((1, D_MODEL), F32)],
        compiler_params=_params(1),
    )(dq_f, dq_r, dv_f, dv_r, dz_f, dz_r, dhg, hq, dcq, dckv, dkr, dx2, x, g1, w_in_p)


def _lower_bound(lbl_ref, direction):
    l0 = lbl_ref[direction, 0:1, :]
    l1 = lbl_ref[direction, 1:2, :]
    m = jnp.maximum(l0, l1)
    e0 = jnp.exp(l0 - m)
    e1 = jnp.exp(l1 - m)
    return e0 / (e0 + e1)


def _hgrn_consts(rb, reverse):
    row = _iota2((rb, rb), 0)
    col = _iota2((rb, rb), 1)
    same = (row // CHUNK) == (col // CHUNK)
    tri = jnp.logical_and(same, (col >= row) if reverse else (col <= row))
    tri_t = jnp.logical_and(same, (col <= row) if reverse else (col >= row))
    r128 = _iota2((128, 128), 0)
    c128 = _iota2((128, 128), 1)
    bd = (r128 < 64) == (c128 < 64)
    lane = _iota2((1, 128), 1)
    m0 = (lane < 64).astype(F32)
    return tri, tri_t, bd, (m0, 1.0 - m0)


def _per_chunk(x, fn):
    n = x.shape[0] // CHUNK
    return jnp.concatenate([jnp.broadcast_to(fn(x[c * CHUNK:(c + 1) * CHUNK]), (CHUNK, x.shape[1])) for c in range(n)],
                           axis=0)


def _chunk_cumsum(x, reverse):
    rb = x.shape[0]
    pos = _iota2(x.shape, 0) % CHUNK
    step = 1
    while step < CHUNK:
        if reverse:
            x = x + jnp.where(pos < CHUNK - step, pltpu.roll(x, rb - step, 0), 0.0)
        else:
            x = x + jnp.where(pos >= step, pltpu.roll(x, step, 0), 0.0)
        step *= 2
    return x


def _hgrn_block(z, hqv, lb, reverse):
    sig = _sigmoid(z)
    sn = _sigmoid(-z)
    q = hqv * _sigmoid(hqv)
    f = lb + (1.0 - lb) * sig
    k = (1.0 - lb) * sn
    lf = jnp.log(f)
    cum = _chunk_cumsum(lf, reverse)
    last = _per_chunk(cum, (lambda a: a[0:1]) if reverse else (lambda a: a[CHUNK - 1:CHUNK]))
    e_neg = jnp.exp(-cum)
    e_end = jnp.exp(last - cum)
    a = jnp.exp(cum)
    return dict(sig=sig, sn=sn, q=q, f=f, k=k, a=a, e_neg=e_neg, e_end=e_end,
                q_dec=q * a, k_inv=k * e_neg, k_end=k * e_end, d=jnp.exp(last))


def _hgrn_dims(t, n_batch):
    s = t // n_batch
    rb = _pick(s, 256, CHUNK)
    return s, rb, s // rb, rb // CHUNK


def _hgrn_fwd(hq, hi, hf, lbl, *, n_batch, direction):
    t = hq.shape[0]
    reverse = direction == 1
    s, rb, nb, nc = _hgrn_dims(t, n_batch)

    def tmap(b, j):
        return (b * nb + ((nb - 1 - j) if reverse else j), 0)

    def smap(b, j):
        return (b * nb + ((nb - 1 - j) if reverse else j), 0, 0, 0)

    def body(hq_ref, hi_ref, hf_ref, lbl_ref, o_ref, st_ref, st_scr):
        @pl.when(pl.program_id(1) == 0)
        def _():
            st_scr[...] = jnp.zeros_like(st_scr)

        lb_all = _lower_bound(lbl_ref, direction)
        tri, _, bd, masks = _hgrn_consts(rb, reverse)
        order = range(nc - 1, -1, -1) if reverse else range(nc)

        pairs = [slice(p * 128, (p + 1) * 128) for p in range(HEAD_PAIRS)]
        chunks = [slice(c * CHUNK, (c + 1) * CHUNK) for c in range(nc)]
        w = _hgrn_block(hf_ref[...], hq_ref[...], lb_all, reverse)
        v16 = hi_ref[...].astype(BF16)
        qd16 = w["q_dec"].astype(BF16)
        ki16 = w["k_inv"].astype(BF16)
        ke16 = w["k_end"].astype(BF16)
        sc = [[_dot_nt(jnp.where(mh > 0.0, qd16[:, ls], 0.0).astype(BF16), ki16[:, ls]) for mh in masks] for ls in pairs]
        pv = [[_dot(jnp.where(tri, s_e, 0.0).astype(BF16), v16[:, ls]) for s_e in sc_p] for sc_p, ls in zip(sc, pairs)]
        o_intra = [jnp.where(masks[0] > 0.0, pv_p[0], pv_p[1]) for pv_p in pv]
        ut = [[jnp.where(bd, _dot_tn(v16[rs, ls], ke16[rs, ls]), 0.0) for ls in pairs] for rs in chunks]
        st = [st_scr[p] for p in range(HEAD_PAIRS)]
        for c in order:
            rs = chunks[c]
            inter = [_dot_nt(qd16[rs, ls], st[p].astype(BF16)) for p, ls in enumerate(pairs)]
            for p, ls in enumerate(pairs):
                o_ref[rs, ls] = o_intra[p][rs] + inter[p]
                st_ref[c, p] = st[p]
                st[p] = st[p] * w["d"][c * CHUNK:c * CHUNK + 1, ls] + ut[c][p]
        for p in range(HEAD_PAIRS):
            st_scr[p] = st[p]

    blk = pl.BlockSpec((rb, A_WIDTH), tmap)
    return pl.pallas_call(
        body, name=f"hgrn_fwd_{direction}", grid=(n_batch, nb),
        in_specs=[blk, blk, blk, _full((2, 2, A_WIDTH))],
        out_specs=[blk, pl.BlockSpec((nc, HEAD_PAIRS, 128, 128), smap)],
        out_shape=[jax.ShapeDtypeStruct((t, A_WIDTH), F32),
                   jax.ShapeDtypeStruct((t // CHUNK, HEAD_PAIRS, 128, 128), F32)],
        scratch_shapes=[pltpu.VMEM((HEAD_PAIRS, 128, 128), F32)],
        compiler_params=_params(2),
    )(hq, hi, hf, lbl)


def _hgrn_bwd(hq, hi, hf, do, st, lbl, *, n_batch, direction):
    t = hq.shape[0]
    reverse = direction == 1
    s, rb, nb, nc = _hgrn_dims(t, n_batch)

    def tmap(b, j):
        return (b * nb + (j if reverse else (nb - 1 - j)), 0)

    def smap(b, j):
        return (b * nb + (j if reverse else (nb - 1 - j)), 0, 0, 0)

    def body(hq_ref, hi_ref, hf_ref, do_ref, st_ref, lbl_ref, dq_ref, dv_ref, dz_ref, dl_ref, g_scr, dlb_scr):
        b = pl.program_id(0)
        j = pl.program_id(1)

        @pl.when(jnp.logical_and(b == 0, j == 0))
        def _():
            dlb_scr[...] = jnp.zeros_like(dlb_scr)

        @pl.when(j == 0)
        def _():
            g_scr[...] = jnp.zeros_like(g_scr)

        lb_all = _lower_bound(lbl_ref, direction)
        tri, tri_t, bd, masks = _hgrn_consts(rb, reverse)
        order = range(nc) if reverse else range(nc - 1, -1, -1)

        pairs = [slice(p * 128, (p + 1) * 128) for p in range(HEAD_PAIRS)]
        chunks = [slice(c * CHUNK, (c + 1) * CHUNK) for c in range(nc)]

        def lanes(per_pair):
            return jnp.concatenate(per_pair, axis=1)

        w = _hgrn_block(hf_ref[...], hq_ref[...], lb_all, reverse)
        dov = do_ref[...]
        v16 = hi_ref[...].astype(BF16)
        do16 = dov.astype(BF16)
        qd16 = w["q_dec"].astype(BF16)
        ki16 = w["k_inv"].astype(BF16)
        ke16 = w["k_end"].astype(BF16)
        qm16 = [[jnp.where(mh > 0.0, qd16[:, ls], 0.0).astype(BF16) for mh in masks] for ls in pairs]
        dom16 = [[jnp.where(mh > 0.0, do16[:, ls], 0.0).astype(BF16) for mh in masks] for ls in pairs]
        heads = [(p, e) for p in range(HEAD_PAIRS) for e in range(2)]
        dp = {(p, e): _dot_nt(dom16[p][e], v16[:, pairs[p]]) for p, e in heads}
        pm_t = {(p, e): _dot_nt(ki16[:, pairs[p]], qm16[p][e]) for p, e in heads}
        dp_t = {(p, e): _dot_nt(v16[:, pairs[p]], dom16[p][e]) for p, e in heads}
        dp = {h: jnp.where(tri, a, 0.0).astype(BF16) for h, a in dp.items()}
        pm_t = {h: jnp.where(tri_t, a, 0.0).astype(BF16) for h, a in pm_t.items()}
        dp_t = {h: jnp.where(tri_t, a, 0.0).astype(BF16) for h, a in dp_t.items()}
        dv_e = {(p, e): _dot(pm_t[p, e], do16[:, pairs[p]]) for p, e in heads}
        dq_e = {(p, e): _dot(dp[p, e], ki16[:, pairs[p]]) for p, e in heads}
        dk_e = {(p, e): _dot(dp_t[p, e], qd16[:, pairs[p]]) for p, e in heads}
        st = [[st_ref[c, p] for p in range(HEAD_PAIRS)] for c in range(nc)]
        dq_x = [[_dot(do16[rs, ls], st[c][p].astype(BF16)) for p, ls in enumerate(pairs)] for c, rs in enumerate(chunks)]
        gq = [[jnp.where(bd, _dot_tn(do16[rs, ls], qd16[rs, ls]), 0.0) for ls in pairs] for rs in chunks]
        g = [g_scr[p] for p in range(HEAD_PAIRS)]
        dk_end, dv_x, dd = [None] * nc, [None] * nc, [None] * nc
        for c in order:
            rs = chunks[c]
            g16 = [a.astype(BF16) for a in g]
            dk_end[c] = lanes([_dot(v16[rs, ls], g16[p]) for p, ls in enumerate(pairs)])
            dv_x[c] = lanes([_dot_nt(ke16[rs, ls], g16[p]) for p, ls in enumerate(pairs)])
            dd[c] = jnp.broadcast_to(lanes([jnp.sum(g[p] * st[c][p], axis=0, keepdims=True) for p in range(HEAD_PAIRS)]),
                                     (CHUNK, A_WIDTH))
            for p, ls in enumerate(pairs):
                g[p] = g[p] * w["d"][c * CHUNK:c * CHUNK + 1, ls] + gq[c][p]
        for p in range(HEAD_PAIRS):
            g_scr[p] = g[p]

        def both_heads(d):
            return lanes([jnp.where(masks[0] > 0.0, d[p, 0], d[p, 1]) for p in range(HEAD_PAIRS)])

        dq_dec = both_heads(dq_e) + jnp.concatenate([lanes(a) for a in dq_x], axis=0)
        dk_inv = both_heads(dk_e)
        dk_end = jnp.concatenate(dk_end, axis=0)
        dv = both_heads(dv_e) + jnp.concatenate(dv_x, axis=0)
        dd = jnp.concatenate(dd, axis=0)
        dke = dk_end * w["k_end"]
        dcum = dq_dec * w["q_dec"] - dk_inv * w["k_inv"] - dke
        dk = dk_inv * w["e_neg"] + dk_end * w["e_end"]
        dlast = _per_chunk(dke, lambda a: jnp.sum(a, axis=0, keepdims=True)) + dd * w["d"]
        dlf = _chunk_cumsum(dcum, not reverse) + dlast
        tt = dlf / w["f"] - dk
        dq_ref[...] = dq_dec * w["a"]
        dv_ref[...] = dv
        dz_ref[...] = ((1.0 - lb_all) * w["sig"] * w["sn"] * tt).astype(BF16)
        dlb_scr[...] += jnp.sum(w["sn"] * tt, axis=0, keepdims=True)

        @pl.when(jnp.logical_and(b == pl.num_programs(0) - 1, j == pl.num_programs(1) - 1))
        def _():
            d0 = dlb_scr[...] * lb_all * (1.0 - lb_all)
            dl_ref[0:1, :] = d0
            dl_ref[1:2, :] = -d0

    blk = pl.BlockSpec((rb, A_WIDTH), tmap)
    return pl.pallas_call(
        body, name=f"hgrn_bwd_{direction}", grid=(n_batch, nb),
        in_specs=[blk, blk, blk, blk, pl.BlockSpec((nc, HEAD_PAIRS, 128, 128), smap), _full((2, 2, A_WIDTH))],
        out_specs=[blk, blk, blk, _full((2, A_WIDTH))],
        out_shape=[jax.ShapeDtypeStruct((t, A_WIDTH), F32)] * 2
        + [jax.ShapeDtypeStruct((t, A_WIDTH), BF16), jax.ShapeDtypeStruct((2, A_WIDTH), F32)],
        scratch_shapes=[pltpu.VMEM((HEAD_PAIRS, 128, 128), F32), pltpu.VMEM((1, A_WIDTH), F32)],
        compiler_params=_params(2),
    )(hq, hi, hf, do, st, lbl)


def _swap_rope_halves(x):
    lane = _iota2(x.shape, 1)
    return jnp.where(lane < 32, pltpu.roll(x, 96, 1), pltpu.roll(x, 32, 1))


def _rms_fwd(xv, g):
    r = lax.rsqrt(jnp.mean(xv * xv, axis=-1, keepdims=True) + EPS)
    return (xv * r) * g


def _rms_bwd(dy, xv, g):
    r = lax.rsqrt(jnp.mean(xv * xv, axis=-1, keepdims=True) + EPS)
    xh = xv * r
    dxh = dy * g
    return r * (dxh - xh * jnp.mean(dxh * xh, axis=-1, keepdims=True)), dy * xh


def _mla_prep(cq, ckv, kr, g_qa, g_kva, w_q_p, w_kv_p, cosx, sinx, *, n_batch):
    t = cq.shape[0]
    s = t // n_batch
    tm = _row_tile(s)
    nt = s // tm

    def body(cq_ref, ckv_ref, kr_ref, gq_ref, gkv_ref, wq_ref, wkv_ref, cos_ref, sin_ref,
             q_ref, k_ref, v_ref, cqn_ref, ckvn_ref):
        cos, sin = cos_ref[...], sin_ref[...]
        cqn = _rms_fwd(cq_ref[...], gq_ref[...]).astype(BF16)
        ckvn = _rms_fwd(ckv_ref[...], gkv_ref[...]).astype(BF16)
        cqn_ref[...] = cqn
        ckvn_ref[...] = ckvn
        krv = kr_ref[...]
        kr_roped = (krv * cos + _swap_rope_halves(krv) * sin).astype(BF16)
        for h in range(B_HEADS):
            o = h * QK_PAD
            q_ref[:, o:o + 128] = _dot_nt(cqn, wq_ref[o:o + 128, :]).astype(BF16)
            qr = _dot_nt(cqn, wq_ref[o + 128:o + 256, :])
            q_ref[:, o + 128:o + 256] = (qr * cos + _swap_rope_halves(qr) * sin).astype(BF16)
            k_ref[:, o:o + 128] = _dot(ckvn, wkv_ref[:, h * 128:(h + 1) * 128]).astype(BF16)
            k_ref[:, o + 128:o + 256] = kr_roped
        v_ref[...] = _dot(ckvn, wkv_ref[:, 512:1024]).astype(BF16)

    tab = pl.BlockSpec((tm, 128), lambda i: (i % nt, 0))
    return pl.pallas_call(
        body, name="mla_prep", grid=(t // tm,),
        in_specs=[_rows(tm, Q_LORA), _rows(tm, KV_LORA), _rows(tm, 128), _full((1, Q_LORA)), _full((1, KV_LORA)),
                  _full((1024, Q_LORA)), _full((KV_LORA, 1024)), tab, tab],
        out_specs=[_rows(tm, 1024), _rows(tm, 1024), _rows(tm, 512), _rows(tm, Q_LORA), _rows(tm, KV_LORA)],
        out_shape=[jax.ShapeDtypeStruct((t, 1024), BF16), jax.ShapeDtypeStruct((t, 1024), BF16),
                   jax.ShapeDtypeStruct((t, 512), BF16), jax.ShapeDtypeStruct((t, Q_LORA), BF16),
                   jax.ShapeDtypeStruct((t, KV_LORA), BF16)],
        compiler_params=_params(1),
    )(cq, ckv, kr, g_qa, g_kva, w_q_p, w_kv_p, cosx, sinx)


def _mla_prep_bwd(dqt, dk, dv, cq, ckv, g_qa, g_kva, w_q_p, w_kv_p, cosx, sinx, *, n_batch):
    t = cq.shape[0]
    s = t // n_batch
    tm = _row_tile(s)
    nt = s // tm

    def body(dqt_ref, dk_ref, dv_ref, cq_ref, ckv_ref, gq_ref, gkv_ref, wq_ref, wkv_ref, cos_ref, sin_ref,
             dcq_ref, dckv_ref, dkr_ref, dqp_ref, dkvp_ref, dgq_ref, dgkv_ref):
        cos, sin = cos_ref[...], sin_ref[...]
        first = pl.program_id(0) == 0

        def unrope(d):
            return d * cos + _swap_rope_halves(d * sin)

        dcqn = None
        dkr = None
        dckvn = None
        for h in range(B_HEADS):
            o = h * QK_PAD
            dq_h = jnp.transpose(dqt_ref[o:o + QK_PAD, :])
            dqn16 = dq_h[:, 0:128].astype(BF16)
            dqr16 = unrope(dq_h[:, 128:256]).astype(BF16)
            dqp_ref[:, o:o + 128] = dqn16
            dqp_ref[:, o + 128:o + 256] = dqr16
            part = _dot(dqn16, wq_ref[o:o + 128, :]) + _dot(dqr16, wq_ref[o + 128:o + 256, :])
            dcqn = part if dcqn is None else dcqn + part
            dkn16 = dk_ref[:, o:o + 128].astype(BF16)
            dkvp_ref[:, h * 128:(h + 1) * 128] = dkn16
            part = _dot_nt(dkn16, wkv_ref[:, h * 128:(h + 1) * 128])
            dckvn = part if dckvn is None else dckvn + part
            kr_part = dk_ref[:, o + 128:o + 256]
            dkr = kr_part if dkr is None else dkr + kr_part
        dv16 = dv_ref[...].astype(BF16)
        dkvp_ref[:, 512:1024] = dv16
        dckvn = dckvn + _dot_nt(dv16, wkv_ref[:, 512:1024])
        dkr_ref[...] = unrope(dkr).astype(BF16)
        dcq, dgq = _rms_bwd(dcqn, cq_ref[...], gq_ref[...])
        dckv, dgkv = _rms_bwd(dckvn, ckv_ref[...], gkv_ref[...])
        dcq_ref[...] = dcq.astype(BF16)
        dckv_ref[...] = dckv.astype(BF16)
        _acc_rows(dgq_ref, dgq, first)
        _acc_rows(dgkv_ref, dgkv, first)

    tab = pl.BlockSpec((tm, 128), lambda i: (i % nt, 0))
    return pl.pallas_call(
        body, name="mla_prep_bwd", grid=(t // tm,),
        in_specs=[pl.BlockSpec((1024, tm), lambda i: (0, i)), _rows(tm, 1024), _rows(tm, 512), _rows(tm, Q_LORA),
                  _rows(tm, KV_LORA),
                  _full((1, Q_LORA)), _full((1, KV_LORA)), _full((1024, Q_LORA)), _full((KV_LORA, 1024)), tab, tab],
        out_specs=[_rows(tm, Q_LORA), _rows(tm, KV_LORA), _rows(tm, 128), _rows(tm, 1024), _rows(tm, 1024),
                   _full((1, Q_LORA)), _full((1, KV_LORA))],
        out_shape=[jax.ShapeDtypeStruct((t, Q_LORA), BF16), jax.ShapeDtypeStruct((t, KV_LORA), BF16),
                   jax.ShapeDtypeStruct((t, 128), BF16), jax.ShapeDtypeStruct((t, 1024), BF16),
                   jax.ShapeDtypeStruct((t, 1024), BF16), jax.ShapeDtypeStruct((1, Q_LORA), F32),
                   jax.ShapeDtypeStruct((1, KV_LORA), F32)],
        compiler_params=_params(1),
    )(dqt, dk, dv, cq, ckv, g_qa, g_kva, w_q_p, w_kv_p, cosx, sinx)


def _attn_dims(t, n_batch):
    s = t // n_batch
    tq = _pick(s, 1024, 128)
    return s, tq, s // tq


ATTN_ROWS = 256


def _grid_ends(n_axes):
    ids = [pl.program_id(a) for a in range(n_axes)]
    first = functools.reduce(jnp.logical_and, [i == 0 for i in ids])
    last = functools.reduce(jnp.logical_and, [i == pl.num_programs(a) - 1 for a, i in enumerate(ids)])
    return first, last


def _attn_fwd(q, k, v, wsrcs, *, n_batch):
    t = q.shape[0]
    s, tq, nq = _attn_dims(t, n_batch)
    nw = len(wsrcs)

    def body(q_ref, k_ref, v_ref, *refs):
        w_refs, (o_ref, lse_ref), wall_refs = refs[:nw], refs[nw:nw + 2], refs[nw + 2:2 * nw + 2]
        send_sems, recv_sems, local_sems = refs[2 * nw + 2:]
        first, last = _grid_ends(3)

        @pl.when(first)
        def _():
            for i in range(nw):
                _gather_start(w_refs[i], wall_refs[i], send_sems, recv_sems, local_sems.at[i], 3 * i)

        @pl.when(last)
        def _():
            for i in range(nw):
                _gather_wait(w_refs[i], wall_refs[i], send_sems, recv_sems, local_sems.at[i], 3 * i)

        kv, vv = k_ref[...], v_ref[...]
        groups = [slice(r, r + ATTN_ROWS) for r in range(0, tq, ATTN_ROWS)]
        raw = [_dot_nt(q_ref[g, :], kv) for g in groups]
        m = [jnp.max(a, axis=-1, keepdims=True) for a in raw]
        p = [jnp.exp2((a - b) * SCALE_LOG2E) for a, b in zip(raw, m)]
        l = [jnp.sum(a, axis=-1, keepdims=True) for a in p]
        for g, pg, mg, lg in zip(groups, p, m, l):
            o_ref[g, :] = _dot(pg.astype(BF16), vv) / lg
            lse2 = mg * SCALE_LOG2E + jnp.log(lg) * LOG2E
            lse_ref[:, g] = jnp.transpose(jnp.broadcast_to(lse2, (ATTN_ROWS, 128)))[0:1, :]

    return pl.pallas_call(
        body, name="attn_fwd", grid=(n_batch, B_HEADS, nq),
        in_specs=[pl.BlockSpec((tq, QK_PAD), lambda b, h, i: (b * nq + i, h)),
                  pl.BlockSpec((s, QK_PAD), lambda b, h, i: (b, h)),
                  pl.BlockSpec((s, B_V), lambda b, h, i: (b, h))] + [_HBM] * nw,
        out_specs=[pl.BlockSpec((tq, B_V), lambda b, h, i: (b * nq + i, h)),
                   pl.BlockSpec((None, 1, tq), lambda b, h, i: (h, 0, b * nq + i))] + [_HBM] * nw,
        out_shape=[jax.ShapeDtypeStruct((t, B_HEADS * B_V), F32), jax.ShapeDtypeStruct((B_HEADS, 1, t), F32)]
        + [jax.ShapeDtypeStruct((N_CHIPS,) + w.shape, w.dtype) for w in wsrcs],
        scratch_shapes=[pltpu.SemaphoreType.DMA((3 * nw,)), pltpu.SemaphoreType.DMA((3 * nw,)),
                        pltpu.SemaphoreType.DMA((nw,))],
        compiler_params=_params(3),
    )(q, k, v, *wsrcs)


def _attn_bwd(q, k, v, do16, lse, delta, gsegs, *, n_batch):
    t = q.shape[0]
    s = t // n_batch
    tk = _pick(s, 512, 128)
    nk = s // tk
    ng = len(gsegs)

    def body(q_ref, k_ref, v_ref, do_ref, lse_ref, dl_ref, *refs):
        g_refs, (dqt_ref, dk_ref, dv_ref), recv_refs = refs[:ng], refs[ng:ng + 3], refs[ng + 3:2 * ng + 3]
        send_sems, recv_sems = refs[2 * ng + 3:]
        first, last = _grid_ends(3)

        @pl.when(first)
        def _():
            for i in range(ng):
                _scatter_start(g_refs[i], recv_refs[i], send_sems, recv_sems, 3 * i)

        @pl.when(last)
        def _():
            for i in range(ng):
                _scatter_wait(g_refs[i], recv_refs[i], send_sems, recv_sems, 3 * i)

        j = pl.program_id(2)
        qv, kv, dov = q_ref[...], k_ref[...], do_ref[...]
        pt = jnp.exp2(_dot_nt(kv, qv) * SCALE_LOG2E - lse_ref[...])
        dv_ref[...] = _dot(pt.astype(BF16), dov)
        dpt = _dot_nt(v_ref[...], dov)
        dst = (pt * (dpt - dl_ref[...])).astype(BF16)
        dk_ref[...] = _dot(dst, qv) * ATTN_SCALE
        part = _dot_tn(kv, dst)

        @pl.when(j == 0)
        def _():
            dqt_ref[...] = part

        @pl.when(j != 0)
        def _():
            dqt_ref[...] += part

        @pl.when(j == nk - 1)
        def _():
            dqt_ref[...] = dqt_ref[...] * ATTN_SCALE

    row = pl.BlockSpec((None, 1, s), lambda b, h, j: (h, 0, b))
    return pl.pallas_call(
        body, name="attn_bwd", grid=(n_batch, B_HEADS, nk),
        in_specs=[pl.BlockSpec((s, QK_PAD), lambda b, h, j: (b, h)),
                  pl.BlockSpec((tk, QK_PAD), lambda b, h, j: (b * nk + j, h)),
                  pl.BlockSpec((tk, B_V), lambda b, h, j: (b * nk + j, h)),
                  pl.BlockSpec((s, B_V), lambda b, h, j: (b, h)), row, row] + [_HBM] * ng,
        out_specs=[pl.BlockSpec((QK_PAD, s), lambda b, h, j: (h, b)),
                   pl.BlockSpec((tk, QK_PAD), lambda b, h, j: (b * nk + j, h)),
                   pl.BlockSpec((tk, B_V), lambda b, h, j: (b * nk + j, h))] + [_HBM] * ng,
        out_shape=[jax.ShapeDtypeStruct((B_HEADS * QK_PAD, t), F32), jax.ShapeDtypeStruct((t, B_HEADS * QK_PAD), F32),
                   jax.ShapeDtypeStruct((t, B_HEADS * B_V), F32)]
        + [jax.ShapeDtypeStruct((3,) + g.shape[1:], g.dtype) for g in gsegs],
        scratch_shapes=[pltpu.SemaphoreType.DMA((3 * ng,)), pltpu.SemaphoreType.DMA((3 * ng,))],
        compiler_params=_params(3),
    )(q, k, v, do16, lse, delta, *gsegs)


def _group_ones16():
    r = _iota2((A_WIDTH, A_WIDTH), 0) // 64
    c = _iota2((A_WIDTH, A_WIDTH), 1) // 64
    return (r == c).astype(BF16)


def _head_rms(o, ones16):
    return lax.rsqrt(_exact_right(o * o, ones16) * (1.0 / 64.0) + EPS)


def _out_fwd(o_f, o_r, hg, o_attn, x, g_hn, g_mla, w_out, g2):
    t = x.shape[0]
    tm = _row_tile(t)

    def body(of_ref, or_ref, hg_ref, oa_ref, x_ref, ghn_ref, gm_ref, w_ref, g2_ref, y_ref, x2_ref, h2_ref):
        ones16 = _group_ones16()
        o = of_ref[...] + or_ref[...]
        hgv = hg_ref[...]
        ya = ((o * _head_rms(o, ones16)) * ghn_ref[...]) * (hgv * _sigmoid(hgv))
        yb = _rms_fwd(oa_ref[...], gm_ref[...])
        ya16, yb16 = ya.astype(BF16), yb.astype(BF16)
        y_ref[:, 0:A_WIDTH] = ya16
        y_ref[:, A_WIDTH:D_MODEL] = yb16
        x2 = x_ref[...] + _dot(ya16, w_ref[0:A_WIDTH, :]) + _dot(yb16, w_ref[A_WIDTH:D_MODEL, :])
        x2_ref[...] = x2
        h2_ref[...] = _rms_fwd(x2, g2_ref[...]).astype(BF16)

    a512 = _rows(tm, A_WIDTH)
    return pl.pallas_call(
        body, name="out_fwd", grid=(t // tm,),
        in_specs=[a512, a512, a512, a512, _rows(tm, D_MODEL), _full((1, A_WIDTH)), _full((1, A_WIDTH)),
                  _full((D_MODEL, D_MODEL)), _full((1, D_MODEL))],
        out_specs=[_rows(tm, D_MODEL)] * 3,
        out_shape=[jax.ShapeDtypeStruct((t, D_MODEL), BF16), jax.ShapeDtypeStruct((t, D_MODEL), F32),
                   jax.ShapeDtypeStruct((t, D_MODEL), BF16)],
        compiler_params=_params(1),
    )(o_f, o_r, hg, o_attn, x, g_hn, g_mla, w_out, g2)


def _out_bwd(dx2_16, o_f, o_r, hg, o_attn, g_hn, g_mla, w_out):
    t = dx2_16.shape[0]
    tm = _row_tile(t)

    def body(dx_ref, of_ref, or_ref, hg_ref, oa_ref, ghn_ref, gm_ref, w_ref,
             do_ref, dhg_ref, doa_ref, dl_ref, dghn_ref, dgm_ref):
        first = pl.program_id(0) == 0
        ones16 = _group_ones16()
        dxv = dx_ref[...]
        dya = _dot_nt(dxv, w_ref[0:A_WIDTH, :])
        dyb = _dot_nt(dxv, w_ref[A_WIDTH:D_MODEL, :])
        o = of_ref[...] + or_ref[...]
        rh = _head_rms(o, ones16)
        oh = o * rh
        hgv = hg_ref[...]
        sg = _sigmoid(hgv)
        sl = hgv * sg
        ghn = ghn_ref[...]
        dhg_ref[...] = ((dya * (oh * ghn)) * (sg * (1.0 + hgv * (1.0 - sg)))).astype(BF16)
        _acc_rows(dghn_ref, dya * sl * oh, first)
        doh = dya * sl * ghn
        do_ref[...] = rh * (doh - oh * (_exact_right(doh * oh, ones16) * (1.0 / 64.0)))
        oav = oa_ref[...]
        doa, dgm = _rms_bwd(dyb, oav, gm_ref[...])
        doa_ref[...] = doa.astype(BF16)
        _acc_rows(dgm_ref, dgm, first)
        sel16 = (_iota2((8, A_WIDTH), 0) == _iota2((8, A_WIDTH), 1) // B_V).astype(BF16)
        x1, x2, x3 = _split3(doa * oav)
        delta = _dot_nt(sel16, x1) + _dot_nt(sel16, x2) + _dot_nt(sel16, x3)
        for h in range(B_HEADS):
            dl_ref[h] = delta[h:h + 1, :]

    a512 = _rows(tm, A_WIDTH)
    return pl.pallas_call(
        body, name="out_bwd", grid=(t // tm,),
        in_specs=[_rows(tm, D_MODEL), a512, a512, a512, a512, _full((1, A_WIDTH)), _full((1, A_WIDTH)),
                  _full((D_MODEL, D_MODEL))],
        out_specs=[a512, a512, a512, pl.BlockSpec((B_HEADS, 1, tm), lambda i: (0, 0, i)),
                   _full((1, A_WIDTH)), _full((1, A_WIDTH))],
        out_shape=[jax.ShapeDtypeStruct((t, A_WIDTH), F32)] + [jax.ShapeDtypeStruct((t, A_WIDTH), BF16)] * 2
        + [jax.ShapeDtypeStruct((B_HEADS, 1, t), F32)]
        + [jax.ShapeDtypeStruct((1, A_WIDTH), F32)] * 2,
        compiler_params=_params(1),
    )(dx2_16, o_f, o_r, hg, o_attn, g_hn, g_mla, w_out)


def _ffn_fwd_bwd(h2, x2, target, w_gate, w_up, w_down, g_f, g2):
    t = x2.shape[0]
    tm = _row_tile(t)
    inv_d = 1.0 / D_MODEL

    def body(h2_ref, x2_ref, tg_ref, wg_ref, wu_ref, wd_ref, gf_ref, g2_ref,
             act_ref, dgate_ref, dup_ref, dx3_ref, dx2_ref, dx2h_ref, loss_ref, dgf_ref, dg2_ref):
        first = pl.program_id(0) == 0
        h2v = h2_ref[...]
        gate = _dot_nt(h2v, wg_ref[...])
        up = _dot_nt(h2v, wu_ref[...])
        sg = _sigmoid(gate)
        sl = gate * sg
        act16 = (sl * up).astype(BF16)
        act_ref[...] = act16
        x2v = x2_ref[...]
        x3 = x2v + _dot(act16, wd_ref[...])
        r3 = lax.rsqrt(jnp.mean(x3 * x3, axis=-1, keepdims=True) + EPS)
        x3h = x3 * r3
        gf = gf_ref[...]
        err = x3h * gf - tg_ref[...]
        part = 0.5 * jnp.sum(jnp.mean(err * err, axis=-1, keepdims=True), axis=0, keepdims=True)

        @pl.when(first)
        def _():
            loss_ref[...] = jnp.zeros_like(loss_ref)

        loss_ref[...] += part
        dy = err * inv_d
        _acc_rows(dgf_ref, dy * x3h, first)
        dx3h = dy * gf
        dx3 = r3 * (dx3h - x3h * jnp.mean(dx3h * x3h, axis=-1, keepdims=True))
        dx3_16 = dx3.astype(BF16)
        dx3_ref[...] = dx3_16
        da = _dot_nt(dx3_16, wd_ref[...])
        dup16 = (da * sl).astype(BF16)
        dgate16 = (da * up * (sg * (1.0 + gate * (1.0 - sg)))).astype(BF16)
        dup_ref[...] = dup16
        dgate_ref[...] = dgate16
        dh2 = _dot(dgate16, wg_ref[...]) + _dot(dup16, wu_ref[...])
        dx2n, dg2 = _rms_bwd(dh2, x2v, g2_ref[...])
        _acc_rows(dg2_ref, dg2, first)
        dx2 = dx3 + dx2n
        dx2_ref[...] = dx2
        dx2h_ref[...] = dx2.astype(BF16)

    return pl.pallas_call(
        body, name="ffn_fwd_bwd", grid=(t // tm,),
        in_specs=[_rows(tm, D_MODEL), _rows(tm, D_MODEL), _rows(tm, D_MODEL), _full((D_FF, D_MODEL), True),
                  _full((D_FF, D_MODEL), True), _full((D_FF, D_MODEL), True), _full((1, D_MODEL)), _full((1, D_MODEL))],
        out_specs=[_rows(tm, D_FF), _rows(tm, D_FF), _rows(tm, D_FF), _rows(tm, D_MODEL), _rows(tm, D_MODEL),
                   _rows(tm, D_MODEL), _full((8, 128)), _full((1, D_MODEL)), _full((1, D_MODEL))],
        out_shape=[jax.ShapeDtypeStruct((t, D_FF), BF16)] * 3
        + [jax.ShapeDtypeStruct((t, D_MODEL), BF16), jax.ShapeDtypeStruct((t, D_MODEL), F32),
           jax.ShapeDtypeStruct((t, D_MODEL), BF16), jax.ShapeDtypeStruct((8, 128), F32),
           jax.ShapeDtypeStruct((1, D_MODEL), F32), jax.ShapeDtypeStruct((1, D_MODEL), F32)],
        compiler_params=_params(1),
    )(h2, x2, target, w_gate, w_up, w_down, g_f, g2)


def _wgrad(a, b, name):
    t, m = a.shape
    n = b.shape[1]
    bm = _pick(m, 1664)
    bn = _pick(n, 1664)
    tk = _pick(t, 2048, 16)
    nk = t // tk

    def body(a_ref, b_ref, o_ref):
        part = _dot_tn(a_ref[...], b_ref[...])

        @pl.when(pl.program_id(2) == 0)
        def _():
            o_ref[...] = part

        @pl.when(pl.program_id(2) != 0)
        def _():
            o_ref[...] += part

    return pl.pallas_call(
        body, name=name, grid=(m // bm, n // bn, nk),
        in_specs=[pl.BlockSpec((tk, bm), lambda i, j, k: (k, i)), pl.BlockSpec((tk, bn), lambda i, j, k: (k, j))],
        out_specs=pl.BlockSpec((bm, bn), lambda i, j, k: (i, j)),
        out_shape=jax.ShapeDtypeStruct((m, n), F32),
        compiler_params=_params(3),
    )(a, b)


def _rope_tables(seq):
    inv = 1.0 / (ROPE_THETA ** (jnp.arange(0, B_ROPE, 2, dtype=F32) / B_ROPE))
    ang = jnp.arange(seq, dtype=F32)[:, None] * inv[None, :]
    cos, sin = jnp.cos(ang), jnp.sin(ang)
    zeros = jnp.zeros((seq, 64), F32)
    return jnp.concatenate([cos, cos, zeros], axis=1), jnp.concatenate([-sin, sin, zeros], axis=1)


def _pad_weights(w_in_t, w_q_t, w_kv_b):
    w_in_p = jnp.pad(w_in_t, ((0, D_IN_PAD - D_IN), (0, 0)))
    w_q_p = jnp.pad(w_q_t.reshape(B_HEADS, B_NOPE + B_ROPE, Q_LORA), ((0, 0), (0, 64), (0, 0))).reshape(1024, Q_LORA)
    kv = w_kv_b.reshape(KV_LORA, B_HEADS, B_NOPE + B_V)
    w_kv_p = jnp.concatenate([kv[:, :, :B_NOPE].reshape(KV_LORA, 512), kv[:, :, B_NOPE:].reshape(KV_LORA, 512)], axis=1)
    return w_in_p, w_q_p, w_kv_p


def _unpad_grads(g_in_p, g_q_p, g_kv_p):
    g_in = g_in_p[:D_IN]
    g_q = g_q_p.reshape(B_HEADS, QK_PAD, Q_LORA)[:, :B_NOPE + B_ROPE].reshape(B_HEADS * (B_NOPE + B_ROPE), Q_LORA)
    g_kv = jnp.concatenate([g_kv_p[:, :512].reshape(KV_LORA, B_HEADS, B_NOPE),
                            g_kv_p[:, 512:].reshape(KV_LORA, B_HEADS, B_V)], axis=2).reshape(KV_LORA, 1024)
    return g_in, g_q, g_kv


def _local_step(x, target, lbl, g1, g_hn, g_qa, g_kva, g_mla, g2, g_f, w_in, w_q_b, w_kv_b, late_shards):
    n_batch, seq, _ = x.shape
    t = n_batch * seq
    x = x.reshape(t, D_MODEL)
    target = target.reshape(t, D_MODEL)
    w_in_p, w_q_p, w_kv_p = _pad_weights(w_in, w_q_b, w_kv_b)
    cosx, sinx = _rope_tables(seq)

    h1, hq, hi, hff, hfb, hg, cq, ckv, kr = _in_fwd(x, g1, w_in_p)
    o_f, st_f = _hgrn_fwd(hq, hi, hff, lbl, n_batch=n_batch, direction=0)
    o_r, st_r = _hgrn_fwd(hq, hi, hfb, lbl, n_batch=n_batch, direction=1)
    q, k, v, cqn, ckvn = _mla_prep(cq, ckv, kr, g_qa, g_kva, w_q_p, w_kv_p, cosx, sinx, n_batch=n_batch)
    o_attn, lse, *gathered = _attn_fwd(q, k, v, late_shards, n_batch=n_batch)
    w_out, w_gate, w_up, w_down = (_join_shards(n, a) for n, a in zip(_LATE, gathered))
    ycat, x2, h2 = _out_fwd(o_f, o_r, hg, o_attn, x, g_hn, g_mla, w_out, g2)
    act, dgate, dup, dx3_16, dx2, dx2_16, loss, dg_f, dg2 = _ffn_fwd_bwd(h2, x2, target, w_gate, w_up, w_down, g_f, g2)
    g_late = dict(w_out=_wgrad(ycat, dx2_16, "wgrad_out"), w_gate=_wgrad(dgate, h2, "wgrad_gate"),
                  w_up=_wgrad(dup, h2, "wgrad_up"), w_down=_wgrad(act, dx3_16, "wgrad_down"))
    do_h, dhg, do_attn16, delta, dg_hn, dg_mla = _out_bwd(dx2_16, o_f, o_r, hg, o_attn, g_hn, g_mla, w_out)
    dqt, dk, dv, *recv_late = _attn_bwd(q, k, v, do_attn16, lse, delta,
                                        [_segments(n, g_late[n]).astype(BF16) for n in _LATE], n_batch=n_batch)
    dcq, dckv, dkr, dqp16, dkvp16, dg_qa, dg_kva = _mla_prep_bwd(dqt, dk, dv, cq, ckv, g_qa, g_kva, w_q_p, w_kv_p,
                                                                  cosx, sinx, n_batch=n_batch)
    dq_f, dv_f, dz_f, dl_f = _hgrn_bwd(hq, hi, hff, do_h, st_f, lbl, n_batch=n_batch, direction=0)
    dq_r, dv_r, dz_r, dl_r = _hgrn_bwd(hq, hi, hfb, do_h, st_r, lbl, n_batch=n_batch, direction=1)
    dx, dproj16, dg1 = _in_bwd(dq_f, dq_r, dv_f, dv_r, dz_f, dz_r, dhg, hq, dcq, dckv, dkr, dx2, x, g1, w_in_p)

    gw_in, gw_q, gw_kv = _unpad_grads(_wgrad(dproj16, h1, "wgrad_in"), _wgrad(dqp16, cqn, "wgrad_q_b"),
                                      _wgrad(ckvn, dkvp16, "wgrad_kv_b"))
    grads = dict(
        norm1_g=dg1, w_in=gw_in, lb_logits=jnp.stack([dl_f, dl_r]), hgrn_norm_g=dg_hn, q_a_norm_g=dg_qa,
        w_q_b=gw_q, kv_a_norm_g=dg_kva, w_kv_b=gw_kv, mla_norm_g=dg_mla, norm2_g=dg2, final_norm_g=dg_f)
    grads.update(g_late)
    return loss[0, 0], dx.reshape(n_batch, seq, D_MODEL), grads, dict(zip(_LATE, recv_late))


_HBM = pl.BlockSpec(memory_space=pltpu.HBM)
_MESH = pl.DeviceIdType.MESH


def _place():
    x, y, c = lax.axis_index("x"), lax.axis_index("y"), lax.axis_index("c")
    other_chips = [(1 - x, y), (x, 1 - y), (1 - x, 1 - y)]
    return x, y, c, other_chips


def _gather_copies(w_ref, wall_ref, send_sems, recv_sems, local_sem, base=0):
    x, y, c, chips = _place()
    mine = 2 * x + y

    def mk(j, chip_index, to):
        return pltpu.make_async_remote_copy(src_ref=w_ref, dst_ref=wall_ref.at[chip_index], send_sem=send_sems.at[base + j],
                                            recv_sem=recv_sems.at[base + j], device_id=to, device_id_type=_MESH)

    local = pltpu.make_async_copy(w_ref, wall_ref.at[mine], local_sem)
    sends = [mk(j, mine, (*chip, c)) for j, chip in enumerate(chips)]
    recvs = [mk(j, 2 * px + py, (x, y, c)) for j, (px, py) in enumerate(chips)]
    return local, sends, recvs


def _gather_start(*refs):
    local, sends, _ = _gather_copies(*refs)
    local.start()
    for cp in sends:
        cp.start()


def _gather_wait(*refs):
    local, sends, recvs = _gather_copies(*refs)
    for cp in recvs:
        cp.wait_recv()
    for cp in sends:
        cp.wait_send()
    local.wait()


def _scatter_copies(g_ref, recv_ref, send_sems, recv_sems, base=0):
    x, y, c, chips = _place()

    def mk(j, src_index, to):
        return pltpu.make_async_remote_copy(src_ref=g_ref.at[src_index], dst_ref=recv_ref.at[j],
                                            send_sem=send_sems.at[base + j], recv_sem=recv_sems.at[base + j],
                                            device_id=to, device_id_type=_MESH)

    sends = [mk(j, 2 * px + py, (px, py, c)) for j, (px, py) in enumerate(chips)]
    recvs = [mk(j, 0, (x, y, c)) for j in range(3)]
    return sends, recvs


def _scatter_start(*refs):
    for cp in _scatter_copies(*refs)[0]:
        cp.start()


def _scatter_wait(*refs):
    sends, recvs = _scatter_copies(*refs)
    for cp in recvs:
        cp.wait_recv()
    for cp in sends:
        cp.wait_send()


def _half(ref, which, axis):
    n = ref.shape[axis] // 2
    idx = [slice(None)] * len(ref.shape)
    idx[axis] = pl.ds(which * n, n)
    return ref.at[tuple(idx)]


def _split_axis(a):
    return 1 if a.shape[1] % 256 == 0 else 0


def _gather_halves(src_ref, dst_ref, send_sems, recv_sems, local_sem, base, axis):
    x, y, c, chips = _place()
    mine = 2 * x + y

    def remote(src, dst, n, to):
        return pltpu.make_async_remote_copy(src_ref=src, dst_ref=dst, send_sem=send_sems.at[base + n],
                                            recv_sem=recv_sems.at[base + n], device_id=to, device_id_type=_MESH)

    def slot(chip_index, which):
        return _half(dst_ref.at[chip_index], which, axis)

    local = pltpu.make_async_copy(src_ref, dst_ref.at[mine], local_sem)
    first = [remote(_half(src_ref, c, axis), slot(mine, c), j, (*chip, c)) for j, chip in enumerate(chips)]
    landed = [remote(_half(src_ref, c, axis), slot(2 * px + py, c), j, (x, y, c)) for j, (px, py) in enumerate(chips)]
    passed = [remote(slot(2 * px + py, c), slot(2 * px + py, c), 3 + j, (x, y, 1 - c)) for j, (px, py) in enumerate(chips)]
    handed = [remote(_half(src_ref, c, axis), slot(2 * px + py, 1 - c), 3 + j, (x, y, c)) for j, (px, py) in enumerate(chips)]

    def start():
        local.start()
        for cp in first:
            cp.start()

    def pass_on():
        for arrived, onward in zip(landed, passed):
            arrived.wait_recv()
            onward.start()

    def finish():
        for cp in handed:
            cp.wait_recv()
        for cp in first + passed:
            cp.wait_send()
        local.wait()

    return start, pass_on, finish


def _gather_shards(srcs, small):
    n = len(srcs)

    def body(*refs):
        src_refs, s_ref, dst_refs, sall_ref = refs[:n], refs[n], refs[n + 1:2 * n + 1], refs[2 * n + 1]
        send_sems, recv_sems, local_sems = refs[2 * n + 2:]
        _gather_start(s_ref, sall_ref, send_sems, recv_sems, local_sems.at[n], 6 * n)
        steps = [_gather_halves(src_refs[i], dst_refs[i], send_sems, recv_sems, local_sems.at[i], 6 * i, _split_axis(srcs[i]))
                 for i in range(n)]
        for phase in range(3):
            for step in steps:
                step[phase]()
        _gather_wait(s_ref, sall_ref, send_sems, recv_sems, local_sems.at[n], 6 * n)

    return pl.pallas_call(
        body, name="gather_shards", in_specs=[_HBM] * (n + 1), out_specs=[_HBM] * (n + 1),
        out_shape=[jax.ShapeDtypeStruct((N_CHIPS,) + a.shape, a.dtype) for a in list(srcs) + [small]],
        scratch_shapes=[pltpu.SemaphoreType.DMA((6 * n + 3,)), pltpu.SemaphoreType.DMA((6 * n + 3,)),
                        pltpu.SemaphoreType.DMA((n + 1,))],
    )(*srcs, small)


def _scatter_grads(gsegs, small):
    n = len(gsegs)

    def body(*refs):
        g_refs, s_ref, recv_refs, sall_ref = refs[:n], refs[n], refs[n + 1:2 * n + 1], refs[2 * n + 1]
        send_sems, recv_sems, local_sem = refs[2 * n + 2:]
        x, y, c, _ = _place()
        me = 4 * x + 2 * y + c
        flips = [(fx, fy, fc) for fx in (0, 1) for fy in (0, 1) for fc in (0, 1)][1:]

        def peer(f):
            return tuple((1 - a) if b else a for a, b in zip((x, y, c), f))

        def sm(r, index, to):
            return pltpu.make_async_remote_copy(src_ref=s_ref, dst_ref=sall_ref.at[index], send_sem=send_sems.at[3 * n + r],
                                                recv_sem=recv_sems.at[3 * n + r], device_id=to, device_id_type=_MESH)

        local = pltpu.make_async_copy(s_ref, sall_ref.at[me], local_sem)
        local.start()
        for i in range(n):
            _scatter_start(g_refs[i], recv_refs[i], send_sems, recv_sems, 3 * i)
        sends = [sm(r, me, peer(f)) for r, f in enumerate(flips)]
        for cp in sends:
            cp.start()
        for i in range(n):
            _scatter_wait(g_refs[i], recv_refs[i], send_sems, recv_sems, 3 * i)
        for r, f in enumerate(flips):
            px, py, pc = peer(f)
            sm(r, 4 * px + 2 * py + pc, (x, y, c)).wait_recv()
        for cp in sends:
            cp.wait_send()
        local.wait()

    return pl.pallas_call(
        body, name="scatter_grads", in_specs=[_HBM] * (n + 1), out_specs=[_HBM] * (n + 1),
        out_shape=[jax.ShapeDtypeStruct((3,) + g.shape[1:], g.dtype) for g in gsegs]
        + [jax.ShapeDtypeStruct((8,) + small.shape, F32)],
        scratch_shapes=[pltpu.SemaphoreType.DMA((3 * n + 7,)), pltpu.SemaphoreType.DMA((3 * n + 7,)),
                        pltpu.SemaphoreType.DMA(())],
    )(*gsegs, small)


def _swap_sibling(parts):
    n = len(parts)

    def body(*refs):
        send_sems, recv_sems = refs[2 * n:]
        x, y, c, _ = _place()
        cps = [pltpu.make_async_remote_copy(src_ref=refs[i], dst_ref=refs[n + i], send_sem=send_sems.at[i],
                                            recv_sem=recv_sems.at[i], device_id=(x, y, 1 - c), device_id_type=_MESH)
               for i in range(n)]
        for cp in cps:
            cp.start()
        for cp in cps:
            cp.wait()

    return pl.pallas_call(
        body, name="swap_sibling", in_specs=[_HBM] * n, out_specs=[_HBM] * n,
        out_shape=[jax.ShapeDtypeStruct(p.shape, p.dtype) for p in parts],
        scratch_shapes=[pltpu.SemaphoreType.DMA((n,)), pltpu.SemaphoreType.DMA((n,))],
    )(*parts)


def _sum_segments(own, recv, name):
    rows, cols = own.shape
    tm = _pick(rows, 512, 16)

    def body(o_ref, r_ref, out_ref):
        acc = o_ref[...]
        for j in range(3):
            acc = acc + r_ref[j].astype(F32)
        out_ref[...] = acc

    return pl.pallas_call(
        body, name=name, grid=(rows // tm,),
        in_specs=[_rows(tm, cols), pl.BlockSpec((3, tm, cols), lambda i: (0, i, 0))],
        out_specs=_rows(tm, cols), out_shape=jax.ShapeDtypeStruct(own.shape, F32), compiler_params=_params(1),
    )(own, recv)


def _sum_devices(sall):
    def body(s_ref, o_ref):
        acc = s_ref[0]
        for d in range(1, 8):
            acc = acc + s_ref[d]
        o_ref[...] = acc

    return pl.pallas_call(body, name="sum_devices", out_shape=jax.ShapeDtypeStruct(sall.shape[1:], F32))(sall)


def _adamw(w, m, v, ga, gb, name):
    rows, cols = w.shape
    tm = _row_tile(rows)
    two = gb is not None

    def body(*refs):
        w_ref, m_ref, v_ref, ga_ref = refs[:4]
        g_ref, d_ref, m2_ref, v2_ref = refs[-4:]
        g = ga_ref[...] + refs[4][...] if two else ga_ref[...]
        m2 = ADAM_B1 * m_ref[...] + (1.0 - ADAM_B1) * g
        v2 = ADAM_B2 * v_ref[...] + (1.0 - ADAM_B2) * (g * g)
        m_hat = m2 / (1.0 - ADAM_B1 ** ADAM_STEP)
        v_hat = v2 / (1.0 - ADAM_B2 ** ADAM_STEP)
        g_ref[...] = g
        d_ref[...] = -ADAM_LR * (m_hat / (jnp.sqrt(v_hat) + ADAM_EPS) + ADAM_WD * w_ref[...])
        m2_ref[...] = m2
        v2_ref[...] = v2

    blk = _rows(tm, cols)
    args = (w, m, v, ga) + ((gb,) if two else ())
    return pl.pallas_call(
        body, name=name, grid=(rows // tm,), in_specs=[blk] * len(args), out_specs=[blk] * 4,
        out_shape=[jax.ShapeDtypeStruct(w.shape, F32)] * 4, compiler_params=_params(1),
    )(*args)


_TRANSPOSED = ("w_in", "w_q_b", "w_gate", "w_up")
_COLUMN_SHARDED = ("w_kv_b",)
_FULL_SHAPES = dict(w_in=(D_IN, D_MODEL), w_q_b=(768, Q_LORA), w_kv_b=(KV_LORA, 1024), w_out=(D_MODEL, D_MODEL),
                    w_gate=(D_FF, D_MODEL), w_up=(D_FF, D_MODEL), w_down=(D_FF, D_MODEL))
_SMALL = (("norm1_g", 1024), ("lb_logits", 2048), ("hgrn_norm_g", 512), ("q_a_norm_g", 384), ("kv_a_norm_g", 256),
          ("mla_norm_g", 512), ("norm2_g", 1024), ("final_norm_g", 1024))
_UPDATE_ROWS = 48


def _pad_rows(a, rows):
    return jnp.pad(a, ((0, rows - a.shape[0]), (0, 0)))


_EARLY = ("w_in", "w_q_b", "w_kv_b")
_LATE = ("w_out", "w_gate", "w_up", "w_down")


def _segments(name, g):
    r, c = g.shape
    if name in _COLUMN_SHARDED:
        return g.reshape(r, N_CHIPS, c // N_CHIPS).transpose(1, 0, 2)
    return g.reshape(N_CHIPS, r // N_CHIPS, c)


def _own_segment(name, g, chip):
    r, c = g.shape
    if name in _COLUMN_SHARDED:
        return lax.dynamic_slice_in_dim(g, chip * (c // N_CHIPS), c // N_CHIPS, axis=1)
    return lax.dynamic_slice_in_dim(g, chip * (r // N_CHIPS), r // N_CHIPS, axis=0)


def _join_shards(name, seg):
    r, c = _FULL_SHAPES[name]
    if name in _COLUMN_SHARDED:
        return seg.transpose(1, 0, 2).reshape(r, c)
    return seg.reshape(r, c)


def kernel(x, norm1_g, w_in, lb_logits, hgrn_norm_g, q_a_norm_g, w_q_b, kv_a_norm_g, w_kv_b, mla_norm_g, w_out, norm2_g, w_gate, w_up, w_down, final_norm_g, loss_target, m_norm1_g, m_w_in, m_lb_logits, m_hgrn_norm_g, m_q_a_norm_g, m_w_q_b, m_kv_a_norm_g, m_w_kv_b, m_mla_norm_g, m_w_out, m_norm2_g, m_w_gate, m_w_up, m_w_down, m_final_norm_g, v_norm1_g, v_w_in, v_lb_logits, v_hgrn_norm_g, v_q_a_norm_g, v_w_q_b, v_kv_a_norm_g, v_w_kv_b, v_mla_norm_g, v_w_out, v_norm2_g, v_w_gate, v_w_up, v_w_down, v_final_norm_g):
    names = ("norm1_g", "w_in", "lb_logits", "hgrn_norm_g", "q_a_norm_g", "w_q_b", "kv_a_norm_g", "w_kv_b", "mla_norm_g",
             "w_out", "norm2_g", "w_gate", "w_up", "w_down", "final_norm_g")
    w = dict(zip(names, (norm1_g, w_in, lb_logits, hgrn_norm_g, q_a_norm_g, w_q_b, kv_a_norm_g, w_kv_b, mla_norm_g,
                         w_out, norm2_g, w_gate, w_up, w_down, final_norm_g)))
    m = dict(zip(names, (m_norm1_g, m_w_in, m_lb_logits, m_hgrn_norm_g, m_q_a_norm_g, m_w_q_b, m_kv_a_norm_g, m_w_kv_b,
                         m_mla_norm_g, m_w_out, m_norm2_g, m_w_gate, m_w_up, m_w_down, m_final_norm_g)))
    v = dict(zip(names, (v_norm1_g, v_w_in, v_lb_logits, v_hgrn_norm_g, v_q_a_norm_g, v_w_q_b, v_kv_a_norm_g, v_w_kv_b,
                         v_mla_norm_g, v_w_out, v_norm2_g, v_w_gate, v_w_up, v_w_down, v_final_norm_g)))
    matrices = _EARLY + _LATE
    chip = 2 * lax.axis_index("x") + lax.axis_index("y")

    def shard2d(a, n):
        return jnp.swapaxes(a[0], 0, 1) if n in _TRANSPOSED else a[0]

    def unshard2d(a, n):
        return (jnp.swapaxes(a, 0, 1) if n in _TRANSPOSED else a)[None]

    w16 = {n: shard2d(w[n], n).astype(BF16) for n in matrices}
    lb8 = _pad_rows(lb_logits.reshape(4, 128), 8)
    *early, lball = _gather_shards([w16[n] for n in _EARLY], lb8)
    lbl = lball[:, :4].reshape(N_CHIPS, 2, 2, 128).transpose(1, 2, 0, 3).reshape(2, 2, A_WIDTH)

    loss_part, grad_x, g, recv = _local_step(
        x, loss_target, lbl, norm1_g, hgrn_norm_g, q_a_norm_g, kv_a_norm_g, mla_norm_g, norm2_g, final_norm_g[None, :],
        *(_join_shards(n, a) for n, a in zip(_EARLY, early)), [w16[n] for n in _LATE])
    loss = lax.psum(loss_part, ("x", "y", "c"))

    small = _pad_rows(jnp.concatenate([g[n].reshape(-1) for n, _ in _SMALL]).reshape(-1, 128), SMALL_ROWS)
    *recv_early, small_all = _scatter_grads([_segments(n, g[n]).astype(BF16) for n in _EARLY], small)
    recv.update(zip(_EARLY, recv_early))
    parts = [_sum_segments(_own_segment(n, g[n], chip), recv[n], f"sum_{n}") for n in matrices]
    sibs = _swap_sibling(parts)
    small_sum = _sum_devices(small_all).reshape(-1)

    out = {}
    for n, part, sib in zip(matrices, parts, sibs):
        res = _adamw(shard2d(w[n], n), shard2d(m[n], n), shard2d(v[n], n), part, sib, f"adamw_{n}")
        out[n] = tuple(unshard2d(r, n) for r in res)

    small_g, off = {}, 0
    for n, size in _SMALL:
        small_g[n] = small_sum[off:off + size]
        off += size
    small_g["lb_logits"] = lax.dynamic_slice_in_dim(small_g["lb_logits"].reshape(2, 2, A_WIDTH), chip * 128, 128, axis=2)
    small_names = tuple(n for n, _ in _SMALL)

    def pack(d):
        return _pad_rows(jnp.concatenate([d[n].reshape(-1) for n in small_names]).reshape(-1, 128), _UPDATE_ROWS)

    res = _adamw(pack(w), pack(m), pack(v), pack(small_g), None, "adamw_small")
    off = 0
    flat = [r.reshape(-1) for r in res]
    for n in small_names:
        size = w[n].size
        out[n] = tuple(f[off:off + size].reshape(w[n].shape) for f in flat)
        off += size

    return (loss, grad_x) + tuple(out[n][i] for i in range(4) for n in names)
```

```python
import functools

import jax
import jax.numpy as jnp
from jax import lax
from jax.experimental import pallas as pl
from jax.experimental.pallas import tpu as pltpu

F32 = jnp.float32
BF16 = jnp.bfloat16

D_MODEL = 1024
A_WIDTH = 512
HEAD_PAIRS = 4
CHUNK = 64
B_HEADS = 4
B_NOPE = 128
B_ROPE = 64
B_V = 128
QK_PAD = 256
Q_LORA = 384
KV_LORA = 256
D_FF = 2816
D_IN = 3264
D_IN_PAD = 3328
IN_WIDTHS = (512, 512, 512, 512, 512, Q_LORA, KV_LORA, 128)
ROPE_THETA = 10000.0
EPS = 1e-6
ATTN_SCALE = (B_NOPE + B_ROPE) ** -0.5
LOG2E = 1.4426950408889634
SCALE_LOG2E = ATTN_SCALE * LOG2E

ADAM_LR = 0.001
ADAM_B1 = 0.9
ADAM_B2 = 0.999
ADAM_EPS = 1e-08
ADAM_WD = 0.01
ADAM_STEP = 10

VMEM_LIMIT_BYTES = 60 * 1024 * 1024
N_CHIPS = 4
SMALL_ROWS = 56


def _params(n_axes):
    return pltpu.CompilerParams(dimension_semantics=("arbitrary",) * n_axes,
                                vmem_limit_bytes=VMEM_LIMIT_BYTES)


def _dot(a, b):
    return jnp.dot(a, b, preferred_element_type=F32)


def _dot_nt(a, b):
    return lax.dot_general(a, b, (((1,), (1,)), ((), ())), preferred_element_type=F32)


def _dot_tn(a, b):
    return lax.dot_general(a, b, (((0,), (0,)), ((), ())), preferred_element_type=F32)


def _split3(x):
    x1 = x.astype(BF16)
    r = x - x1.astype(F32)
    x2 = r.astype(BF16)
    x3 = (r - x2.astype(F32)).astype(BF16)
    return x1, x2, x3


def _exact_left(m16, x):
    x1, x2, x3 = _split3(x)
    return _dot(m16, x1) + _dot(m16, x2) + _dot(m16, x3)


def _exact_right(x, m16):
    x1, x2, x3 = _split3(x)
    return _dot(x1, m16) + _dot(x2, m16) + _dot(x3, m16)


def _iota2(shape, dim):
    return lax.broadcasted_iota(jnp.int32, shape, dim)


def _sigmoid(x):
    return jax.nn.sigmoid(x)


def _pick(dim, cap, mult=128):
    if dim <= cap:
        return dim
    best = None
    for d in range(mult, cap + 1, mult):
        if dim % d == 0:
            best = d
    assert best is not None, (dim, cap, mult)
    return best


def _row_tile(t, cap=256):
    return _pick(t, cap, 8)


def _full(shape, single=False):
    if single:
        return pl.BlockSpec(shape, lambda *_: (0,) * len(shape), pipeline_mode=pl.Buffered(1))
    return pl.BlockSpec(shape, lambda *_: (0,) * len(shape))


def _rows(tm, width):
    return pl.BlockSpec((tm, width), lambda i: (i, 0))


def _acc_rows(ref, val, first):
    s = jnp.sum(val, axis=0, keepdims=True)

    @pl.when(first)
    def _():
        ref[...] = s

    @pl.when(jnp.logical_not(first))
    def _():
        ref[...] += s


def _in_fwd(x, g1, w_in_p):
    t = x.shape[0]
    tm = _row_tile(t)

    def body(x_ref, g_ref, w_ref, h_ref, *outs):
        xv = x_ref[...]
        r = lax.rsqrt(jnp.mean(xv * xv, axis=-1, keepdims=True) + EPS)
        h = ((xv * r) * g_ref[...]).astype(BF16)
        h_ref[...] = h
        off = 0
        for o_ref, w in zip(outs, IN_WIDTHS):
            o_ref[...] = _dot_nt(h, w_ref[off:off + w, :])
            off += w

    return pl.pallas_call(
        body, name="in_fwd", grid=(t // tm,),
        in_specs=[_rows(tm, D_MODEL), _full((1, D_MODEL)), _full((D_IN_PAD, D_MODEL))],
        out_specs=[_rows(tm, D_MODEL)] + [_rows(tm, w) for w in IN_WIDTHS],
        out_shape=[jax.ShapeDtypeStruct((t, D_MODEL), BF16)]
        + [jax.ShapeDtypeStruct((t, w), F32) for w in IN_WIDTHS],
        compiler_params=_params(1),
    )(x, g1, w_in_p)


def _in_bwd(dq_f, dq_r, dv_f, dv_r, dz_f, dz_r, dhg, hq, dcq, dckv, dkr, dx2, x, g1, w_in_p):
    t = x.shape[0]
    tm = _row_tile(t)

    def body(dqf_ref, dqr_ref, dvf_ref, dvr_ref, dzf_ref, dzr_ref, dhg_ref, hq_ref, dcq_ref, dckv_ref,
             dkr_ref, dx2_ref, x_ref, g_ref, w_ref, dx_ref, dp_ref, dg_ref):
        hqv = hq_ref[...]
        sg = _sigmoid(hqv)
        dhq = (dqf_ref[...] + dqr_ref[...]) * (sg * (1.0 + hqv * (1.0 - sg)))
        pieces = (dhq, dvf_ref[...] + dvr_ref[...], dzf_ref[...], dzr_ref[...], dhg_ref[...],
                  dcq_ref[...], dckv_ref[...], dkr_ref[...])
        dh = None
        off = 0
        for p, w in zip(pieces, IN_WIDTHS):
            p16 = p.astype(BF16)
            dp_ref[:, off:off + w] = p16
            part = _dot(p16, w_ref[off:off + w, :])
            dh = part if dh is None else dh + part
            off += w
        xv = x_ref[...]
        r = lax.rsqrt(jnp.mean(xv * xv, axis=-1, keepdims=True) + EPS)
        xh = xv * r
        _acc_rows(dg_ref, dh * xh, pl.program_id(0) == 0)
        dxh = dh * g_ref[...]
        dx_ref[...] = dx2_ref[...] + r * (dxh - xh * jnp.mean(dxh * xh, axis=-1, keepdims=True))

    a512 = _rows(tm, A_WIDTH)
    return pl.pallas_call(
        body, name="in_bwd", grid=(t // tm,),
        in_specs=[a512] * 8 + [_rows(tm, Q_LORA), _rows(tm, KV_LORA), _rows(tm, 128), _rows(tm, D_MODEL),
                               _rows(tm, D_MODEL), _full((1, D_MODEL)), _full((D_IN_PAD, D_MODEL))],
        out_specs=[_rows(tm, D_MODEL), _rows(tm, D_IN_PAD), _full((1, D_MODEL))],
        out_shape=[jax.ShapeDtypeStruct((t, D_MODEL), F32), jax.ShapeDtypeStruct((t, D_IN_PAD), BF16),
                   jax.ShapeDtypeStruct((1, D_MODEL), F32)],
        compiler_params=_params(1),
    )(dq_f, dq_r, dv_f, dv_r, dz_f, dz_r, dhg, hq, dcq, dckv, dkr, dx2, x, g1, w_in_p)


def _lower_bound(lbl_ref, direction):
    l0 = lbl_ref[direction, 0:1, :]
    l1 = lbl_ref[direction, 1:2, :]
    m = jnp.maximum(l0, l1)
    e0 = jnp.exp(l0 - m)
    e1 = jnp.exp(l1 - m)
    return e0 / (e0 + e1)


def _hgrn_consts(rb, reverse):
    row = _iota2((rb, rb), 0)
    col = _iota2((rb, rb), 1)
    same = (row // CHUNK) == (col // CHUNK)
    tri = jnp.logical_and(same, (col >= row) if reverse else (col <= row))
    tri_t = jnp.logical_and(same, (col <= row) if reverse else (col >= row))
    r128 = _iota2((128, 128), 0)
    c128 = _iota2((128, 128), 1)
    bd = (r128 < 64) == (c128 < 64)
    lane = _iota2((1, 128), 1)
    m0 = (lane < 64).astype(F32)
    return tri, tri_t, bd, (m0, 1.0 - m0)


def _per_chunk(x, fn):
    n = x.shape[0] // CHUNK
    return jnp.concatenate([jnp.broadcast_to(fn(x[c * CHUNK:(c + 1) * CHUNK]), (CHUNK, x.shape[1])) for c in range(n)],
                           axis=0)


def _chunk_cumsum(x, reverse):
    rb = x.shape[0]
    pos = _iota2(x.shape, 0) % CHUNK
    step = 1
    while step < CHUNK:
        if reverse:
            x = x + jnp.where(pos < CHUNK - step, pltpu.roll(x, rb - step, 0), 0.0)
        else:
            x = x + jnp.where(pos >= step, pltpu.roll(x, step, 0), 0.0)
        step *= 2
    return x


def _hgrn_block(z, hqv, lb, reverse):
    sig = _sigmoid(z)
    sn = _sigmoid(-z)
    q = hqv * _sigmoid(hqv)
    f = lb + (1.0 - lb) * sig
    k = (1.0 - lb) * sn
    lf = jnp.log(f)
    cum = _chunk_cumsum(lf, reverse)
    last = _per_chunk(cum, (lambda a: a[0:1]) if reverse else (lambda a: a[CHUNK - 1:CHUNK]))
    e_neg = jnp.exp(-cum)
    e_end = jnp.exp(last - cum)
    a = jnp.exp(cum)
    return dict(sig=sig, sn=sn, q=q, f=f, k=k, a=a, e_neg=e_neg, e_end=e_end,
                q_dec=q * a, k_inv=k * e_neg, k_end=k * e_end, d=jnp.exp(last))


def _hgrn_dims(t, n_batch):
    s = t // n_batch
    rb = _pick(s, 256, CHUNK)
    return s, rb, s // rb, rb // CHUNK


def _hgrn_fwd(hq, hi, hf, lbl, *, n_batch, direction):
    t = hq.shape[0]
    reverse = direction == 1
    s, rb, nb, nc = _hgrn_dims(t, n_batch)

    def tmap(b, j):
        return (b * nb + ((nb - 1 - j) if reverse else j), 0)

    def smap(b, j):
        return (b * nb + ((nb - 1 - j) if reverse else j), 0, 0, 0)

    def body(hq_ref, hi_ref, hf_ref, lbl_ref, o_ref, st_ref, st_scr):
        @pl.when(pl.program_id(1) == 0)
        def _():
            st_scr[...] = jnp.zeros_like(st_scr)

        lb_all = _lower_bound(lbl_ref, direction)
        tri, _, bd, masks = _hgrn_consts(rb, reverse)
        order = range(nc - 1, -1, -1) if reverse else range(nc)

        pairs = [slice(p * 128, (p + 1) * 128) for p in range(HEAD_PAIRS)]
        chunks = [slice(c * CHUNK, (c + 1) * CHUNK) for c in range(nc)]
        w = _hgrn_block(hf_ref[...], hq_ref[...], lb_all, reverse)
        v16 = hi_ref[...].astype(BF16)
        qd16 = w["q_dec"].astype(BF16)
        ki16 = w["k_inv"].astype(BF16)
        ke16 = w["k_end"].astype(BF16)
        sc = [[_dot_nt(jnp.where(mh > 0.0, qd16[:, ls], 0.0).astype(BF16), ki16[:, ls]) for mh in masks] for ls in pairs]
        pv = [[_dot(jnp.where(tri, s_e, 0.0).astype(BF16), v16[:, ls]) for s_e in sc_p] for sc_p, ls in zip(sc, pairs)]
        o_intra = [jnp.where(masks[0] > 0.0, pv_p[0], pv_p[1]) for pv_p in pv]
        ut = [[jnp.where(bd, _dot_tn(v16[rs, ls], ke16[rs, ls]), 0.0) for ls in pairs] for rs in chunks]
        st = [st_scr[p] for p in range(HEAD_PAIRS)]
        for c in order:
            rs = chunks[c]
            inter = [_dot_nt(qd16[rs, ls], st[p].astype(BF16)) for p, ls in enumerate(pairs)]
            for p, ls in enumerate(pairs):
                o_ref[rs, ls] = o_intra[p][rs] + inter[p]
                st_ref[c, p] = st[p]
                st[p] = st[p] * w["d"][c * CHUNK:c * CHUNK + 1, ls] + ut[c][p]
        for p in range(HEAD_PAIRS):
            st_scr[p] = st[p]

    blk = pl.BlockSpec((rb, A_WIDTH), tmap)
    return pl.pallas_call(
        body, name=f"hgrn_fwd_{direction}", grid=(n_batch, nb),
        in_specs=[blk, blk, blk, _full((2, 2, A_WIDTH))],
        out_specs=[blk, pl.BlockSpec((nc, HEAD_PAIRS, 128, 128), smap)],
        out_shape=[jax.ShapeDtypeStruct((t, A_WIDTH), F32),
                   jax.ShapeDtypeStruct((t // CHUNK, HEAD_PAIRS, 128, 128), F32)],
        scratch_shapes=[pltpu.VMEM((HEAD_PAIRS, 128, 128), F32)],
        compiler_params=_params(2),
    )(hq, hi, hf, lbl)


def _hgrn_bwd(hq, hi, hf, do, st, lbl, *, n_batch, direction):
    t = hq.shape[0]
    reverse = direction == 1
    s, rb, nb, nc = _hgrn_dims(t, n_batch)

    def tmap(b, j):
        return (b * nb + (j if reverse else (nb - 1 - j)), 0)

    def smap(b, j):
        return (b * nb + (j if reverse else (nb - 1 - j)), 0, 0, 0)

    def body(hq_ref, hi_ref, hf_ref, do_ref, st_ref, lbl_ref, dq_ref, dv_ref, dz_ref, dl_ref, g_scr, dlb_scr):
        b = pl.program_id(0)
        j = pl.program_id(1)

        @pl.when(jnp.logical_and(b == 0, j == 0))
        def _():
            dlb_scr[...] = jnp.zeros_like(dlb_scr)

        @pl.when(j == 0)
        def _():
            g_scr[...] = jnp.zeros_like(g_scr)

        lb_all = _lower_bound(lbl_ref, direction)
        tri, tri_t, bd, masks = _hgrn_consts(rb, reverse)
        order = range(nc) if reverse else range(nc - 1, -1, -1)

        pairs = [slice(p * 128, (p + 1) * 128) for p in range(HEAD_PAIRS)]
        chunks = [slice(c * CHUNK, (c + 1) * CHUNK) for c in range(nc)]

        def lanes(per_pair):
            return jnp.concatenate(per_pair, axis=1)

        w = _hgrn_block(hf_ref[...], hq_ref[...], lb_all, reverse)
        dov = do_ref[...]
        v16 = hi_ref[...].astype(BF16)
        do16 = dov.astype(BF16)
        qd16 = w["q_dec"].astype(BF16)
        ki16 = w["k_inv"].astype(BF16)
        ke16 = w["k_end"].astype(BF16)
        qm16 = [[jnp.where(mh > 0.0, qd16[:, ls], 0.0).astype(BF16) for mh in masks] for ls in pairs]
        dom16 = [[jnp.where(mh > 0.0, do16[:, ls], 0.0).astype(BF16) for mh in masks] for ls in pairs]
        heads = [(p, e) for p in range(HEAD_PAIRS) for e in range(2)]
        dp = {(p, e): _dot_nt(dom16[p][e], v16[:, pairs[p]]) for p, e in heads}
        pm_t = {(p, e): _dot_nt(ki16[:, pairs[p]], qm16[p][e]) for p, e in heads}
        dp_t = {(p, e): _dot_nt(v16[:, pairs[p]], dom16[p][e]) for p, e in heads}
        dp = {h: jnp.where(tri, a, 0.0).astype(BF16) for h, a in dp.items()}
        pm_t = {h: jnp.where(tri_t, a, 0.0).astype(BF16) for h, a in pm_t.items()}
        dp_t = {h: jnp.where(tri_t, a, 0.0).astype(BF16) for h, a in dp_t.items()}
        dv_e = {(p, e): _dot(pm_t[p, e], do16[:, pairs[p]]) for p, e in heads}
        dq_e = {(p, e): _dot(dp[p, e], ki16[:, pairs[p]]) for p, e in heads}
        dk_e = {(p, e): _dot(dp_t[p, e], qd16[:, pairs[p]]) for p, e in heads}
        st = [[st_ref[c, p] for p in range(HEAD_PAIRS)] for c in range(nc)]
        dq_x = [[_dot(do16[rs, ls], st[c][p].astype(BF16)) for p, ls in enumerate(pairs)] for c, rs in enumerate(chunks)]
        gq = [[jnp.where(bd, _dot_tn(do16[rs, ls], qd16[rs, ls]), 0.0) for ls in pairs] for rs in chunks]
        g = [g_scr[p] for p in range(HEAD_PAIRS)]
        dk_end, dv_x, dd = [None] * nc, [None] * nc, [None] * nc
        for c in order:
            rs = chunks[c]
            g16 = [a.astype(BF16) for a in g]
            dk_end[c] = lanes([_dot(v16[rs, ls], g16[p]) for p, ls in enumerate(pairs)])
            dv_x[c] = lanes([_dot_nt(ke16[rs, ls], g16[p]) for p, ls in enumerate(pairs)])
            dd[c] = jnp.broadcast_to(lanes([jnp.sum(g[p] * st[c][p], axis=0, keepdims=True) for p in range(HEAD_PAIRS)]),
                                     (CHUNK, A_WIDTH))
            for p, ls in enumerate(pairs):
                g[p] = g[p] * w["d"][c * CHUNK:c * CHUNK + 1, ls] + gq[c][p]
        for p in range(HEAD_PAIRS):
            g_scr[p] = g[p]

        def both_heads(d):
            return lanes([jnp.where(masks[0] > 0.0, d[p, 0], d[p, 1]) for p in range(HEAD_PAIRS)])

        dq_dec = both_heads(dq_e) + jnp.concatenate([lanes(a) for a in dq_x], axis=0)
        dk_inv = both_heads(dk_e)
        dk_end = jnp.concatenate(dk_end, axis=0)
        dv = both_heads(dv_e) + jnp.concatenate(dv_x, axis=0)
        dd = jnp.concatenate(dd, axis=0)
        dke = dk_end * w["k_end"]
        dcum = dq_dec * w["q_dec"] - dk_inv * w["k_inv"] - dke
        dk = dk_inv * w["e_neg"] + dk_end * w["e_end"]
        dlast = _per_chunk(dke, lambda a: jnp.sum(a, axis=0, keepdims=True)) + dd * w["d"]
        dlf = _chunk_cumsum(dcum, not reverse) + dlast
        tt = dlf / w["f"] - dk
        dq_ref[...] = dq_dec * w["a"]
        dv_ref[...] = dv
        dz_ref[...] = ((1.0 - lb_all) * w["sig"] * w["sn"] * tt).astype(BF16)
        dlb_scr[...] += jnp.sum(w["sn"] * tt, axis=0, keepdims=True)

        @pl.when(jnp.logical_and(b == pl.num_programs(0) - 1, j == pl.num_programs(1) - 1))
        def _():
            d0 = dlb_scr[...] * lb_all * (1.0 - lb_all)
            dl_ref[0:1, :] = d0
            dl_ref[1:2, :] = -d0

    blk = pl.BlockSpec((rb, A_WIDTH), tmap)
    return pl.pallas_call(
        body, name=f"hgrn_bwd_{direction}", grid=(n_batch, nb),
        in_specs=[blk, blk, blk, blk, pl.BlockSpec((nc, HEAD_PAIRS, 128, 128), smap), _full((2, 2, A_WIDTH))],
        out_specs=[blk, blk, blk, _full((2, A_WIDTH))],
        out_shape=[jax.ShapeDtypeStruct((t, A_WIDTH), F32)] * 2
        + [jax.ShapeDtypeStruct((t, A_WIDTH), BF16), jax.ShapeDtypeStruct((2, A_WIDTH), F32)],
        scratch_shapes=[pltpu.VMEM((HEAD_PAIRS, 128, 128), F32), pltpu.VMEM((1, A_WIDTH), F32)],
        compiler_params=_params(2),
    )(hq, hi, hf, do, st, lbl)


def _swap_rope_halves(x):
    lane = _iota2(x.shape, 1)
    return jnp.where(lane < 32, pltpu.roll(x, 96, 1), pltpu.roll(x, 32, 1))


def _rms_fwd(xv, g):
    r = lax.rsqrt(jnp.mean(xv * xv, axis=-1, keepdims=True) + EPS)
    return (xv * r) * g


def _rms_bwd(dy, xv, g):
    r = lax.rsqrt(jnp.mean(xv * xv, axis=-1, keepdims=True) + EPS)
    xh = xv * r
    dxh = dy * g
    return r * (dxh - xh * jnp.mean(dxh * xh, axis=-1, keepdims=True)), dy * xh


def _mla_prep(cq, ckv, kr, g_qa, g_kva, w_q_p, w_kv_p, cosx, sinx, *, n_batch):
    t = cq.shape[0]
    s = t // n_batch
    tm = _row_tile(s)
    nt = s // tm

    def body(cq_ref, ckv_ref, kr_ref, gq_ref, gkv_ref, wq_ref, wkv_ref, cos_ref, sin_ref,
             q_ref, k_ref, v_ref, cqn_ref, ckvn_ref):
        cos, sin = cos_ref[...], sin_ref[...]
        cqn = _rms_fwd(cq_ref[...], gq_ref[...]).astype(BF16)
        ckvn = _rms_fwd(ckv_ref[...], gkv_ref[...]).astype(BF16)
        cqn_ref[...] = cqn
        ckvn_ref[...] = ckvn
        krv = kr_ref[...]
        kr_roped = (krv * cos + _swap_rope_halves(krv) * sin).astype(BF16)
        for h in range(B_HEADS):
            o = h * QK_PAD
            q_ref[:, o:o + 128] = _dot_nt(cqn, wq_ref[o:o + 128, :]).astype(BF16)
            qr = _dot_nt(cqn, wq_ref[o + 128:o + 256, :])
            q_ref[:, o + 128:o + 256] = (qr * cos + _swap_rope_halves(qr) * sin).astype(BF16)
            k_ref[:, o:o + 128] = _dot(ckvn, wkv_ref[:, h * 128:(h + 1) * 128]).astype(BF16)
            k_ref[:, o + 128:o + 256] = kr_roped
        v_ref[...] = _dot(ckvn, wkv_ref[:, 512:1024]).astype(BF16)

    tab = pl.BlockSpec((tm, 128), lambda i: (i % nt, 0))
    return pl.pallas_call(
        body, name="mla_prep", grid=(t // tm,),
        in_specs=[_rows(tm, Q_LORA), _rows(tm, KV_LORA), _rows(tm, 128), _full((1, Q_LORA)), _full((1, KV_LORA)),
                  _full((1024, Q_LORA)), _full((KV_LORA, 1024)), tab, tab],
        out_specs=[_rows(tm, 1024), _rows(tm, 1024), _rows(tm, 512), _rows(tm, Q_LORA), _rows(tm, KV_LORA)],
        out_shape=[jax.ShapeDtypeStruct((t, 1024), BF16), jax.ShapeDtypeStruct((t, 1024), BF16),
                   jax.ShapeDtypeStruct((t, 512), BF16), jax.ShapeDtypeStruct((t, Q_LORA), BF16),
                   jax.ShapeDtypeStruct((t, KV_LORA), BF16)],
        compiler_params=_params(1),
    )(cq, ckv, kr, g_qa, g_kva, w_q_p, w_kv_p, cosx, sinx)


def _mla_prep_bwd(dqt, dk, dv, cq, ckv, g_qa, g_kva, w_q_p, w_kv_p, cosx, sinx, *, n_batch):
    t = cq.shape[0]
    s = t // n_batch
    tm = _row_tile(s)
    nt = s // tm

    def body(dqt_ref, dk_ref, dv_ref, cq_ref, ckv_ref, gq_ref, gkv_ref, wq_ref, wkv_ref, cos_ref, sin_ref,
             dcq_ref, dckv_ref, dkr_ref, dqp_ref, dkvp_ref, dgq_ref, dgkv_ref):
        cos, sin = cos_ref[...], sin_ref[...]
        first = pl.program_id(0) == 0

        def unrope(d):
            return d * cos + _swap_rope_halves(d * sin)

        dcqn = None
        dkr = None
        dckvn = None
        for h in range(B_HEADS):
            o = h * QK_PAD
            dq_h = jnp.transpose(dqt_ref[o:o + QK_PAD, :])
            dqn16 = dq_h[:, 0:128].astype(BF16)
            dqr16 = unrope(dq_h[:, 128:256]).astype(BF16)
            dqp_ref[:, o:o + 128] = dqn16
            dqp_ref[:, o + 128:o + 256] = dqr16
            part = _dot(dqn16, wq_ref[o:o + 128, :]) + _dot(dqr16, wq_ref[o + 128:o + 256, :])
            dcqn = part if dcqn is None else dcqn + part
            dkn16 = dk_ref[:, o:o + 128].astype(BF16)
            dkvp_ref[:, h * 128:(h + 1) * 128] = dkn16
            part = _dot_nt(dkn16, wkv_ref[:, h * 128:(h + 1) * 128])
            dckvn = part if dckvn is None else dckvn + part
            kr_part = dk_ref[:, o + 128:o + 256]
            dkr = kr_part if dkr is None else dkr + kr_part
        dv16 = dv_ref[...].astype(BF16)
        dkvp_ref[:, 512:1024] = dv16
        dckvn = dckvn + _dot_nt(dv16, wkv_ref[:, 512:1024])
        dkr_ref[...] = unrope(dkr).astype(BF16)
        dcq, dgq = _rms_bwd(dcqn, cq_ref[...], gq_ref[...])
        dckv, dgkv = _rms_bwd(dckvn, ckv_ref[...], gkv_ref[...])
        dcq_ref[...] = dcq.astype(BF16)
        dckv_ref[...] = dckv.astype(BF16)
        _acc_rows(dgq_ref, dgq, first)
        _acc_rows(dgkv_ref, dgkv, first)

    tab = pl.BlockSpec((tm, 128), lambda i: (i % nt, 0))
    return pl.pallas_call(
        body, name="mla_prep_bwd", grid=(t // tm,),
        in_specs=[pl.BlockSpec((1024, tm), lambda i: (0, i)), _rows(tm, 1024), _rows(tm, 512), _rows(tm, Q_LORA),
                  _rows(tm, KV_LORA),
                  _full((1, Q_LORA)), _full((1, KV_LORA)), _full((1024, Q_LORA)), _full((KV_LORA, 1024)), tab, tab],
        out_specs=[_rows(tm, Q_LORA), _rows(tm, KV_LORA), _rows(tm, 128), _rows(tm, 1024), _rows(tm, 1024),
                   _full((1, Q_LORA)), _full((1, KV_LORA))],
        out_shape=[jax.ShapeDtypeStruct((t, Q_LORA), BF16), jax.ShapeDtypeStruct((t, KV_LORA), BF16),
                   jax.ShapeDtypeStruct((t, 128), BF16), jax.ShapeDtypeStruct((t, 1024), BF16),
                   jax.ShapeDtypeStruct((t, 1024), BF16), jax.ShapeDtypeStruct((1, Q_LORA), F32),
                   jax.ShapeDtypeStruct((1, KV_LORA), F32)],
        compiler_params=_params(1),
    )(dqt, dk, dv, cq, ckv, g_qa, g_kva, w_q_p, w_kv_p, cosx, sinx)


def _attn_dims(t, n_batch):
    s = t // n_batch
    tq = _pick(s, 1024, 128)
    return s, tq, s // tq


ATTN_ROWS = 256


def _grid_ends(n_axes):
    ids = [pl.program_id(a) for a in range(n_axes)]
    first = functools.reduce(jnp.logical_and, [i == 0 for i in ids])
    last = functools.reduce(jnp.logical_and, [i == pl.num_programs(a) - 1 for a, i in enumerate(ids)])
    return first, last


def _attn_fwd(q, k, v, wsrcs, *, n_batch):
    t = q.shape[0]
    s, tq, nq = _attn_dims(t, n_batch)
    nw = len(wsrcs)

    def body(q_ref, k_ref, v_ref, *refs):
        w_refs, (o_ref, lse_ref), wall_refs = refs[:nw], refs[nw:nw + 2], refs[nw + 2:2 * nw + 2]
        send_sems, recv_sems, local_sems = refs[2 * nw + 2:]
        first, last = _grid_ends(3)

        @pl.when(first)
        def _():
            for i in range(nw):
                _gather_start(w_refs[i], wall_refs[i], send_sems, recv_sems, local_sems.at[i], 3 * i)

        @pl.when(last)
        def _():
            for i in range(nw):
                _gather_wait(w_refs[i], wall_refs[i], send_sems, recv_sems, local_sems.at[i], 3 * i)

        kv, vv = k_ref[...], v_ref[...]
        groups = [slice(r, r + ATTN_ROWS) for r in range(0, tq, ATTN_ROWS)]
        raw = [_dot_nt(q_ref[g, :], kv) for g in groups]
        m = [jnp.max(a, axis=-1, keepdims=True) for a in raw]
        p = [jnp.exp2((a - b) * SCALE_LOG2E) for a, b in zip(raw, m)]
        l = [jnp.sum(a, axis=-1, keepdims=True) for a in p]
        for g, pg, mg, lg in zip(groups, p, m, l):
            o_ref[g, :] = _dot(pg.astype(BF16), vv) / lg
            lse2 = mg * SCALE_LOG2E + jnp.log(lg) * LOG2E
            lse_ref[:, g] = jnp.transpose(jnp.broadcast_to(lse2, (ATTN_ROWS, 128)))[0:1, :]

    return pl.pallas_call(
        body, name="attn_fwd", grid=(n_batch, B_HEADS, nq),
        in_specs=[pl.BlockSpec((tq, QK_PAD), lambda b, h, i: (b * nq + i, h)),
                  pl.BlockSpec((s, QK_PAD), lambda b, h, i: (b, h)),
                  pl.BlockSpec((s, B_V), lambda b, h, i: (b, h))] + [_HBM] * nw,
        out_specs=[pl.BlockSpec((tq, B_V), lambda b, h, i: (b * nq + i, h)),
                   pl.BlockSpec((None, 1, tq), lambda b, h, i: (h, 0, b * nq + i))] + [_HBM] * nw,
        out_shape=[jax.ShapeDtypeStruct((t, B_HEADS * B_V), F32), jax.ShapeDtypeStruct((B_HEADS, 1, t), F32)]
        + [jax.ShapeDtypeStruct((N_CHIPS,) + w.shape, w.dtype) for w in wsrcs],
        scratch_shapes=[pltpu.SemaphoreType.DMA((3 * nw,)), pltpu.SemaphoreType.DMA((3 * nw,)),
                        pltpu.SemaphoreType.DMA((nw,))],
        compiler_params=_params(3),
    )(q, k, v, *wsrcs)


def _attn_bwd(q, k, v, do16, lse, delta, gsegs, *, n_batch):
    t = q.shape[0]
    s = t // n_batch
    tk = _pick(s, 512, 128)
    nk = s // tk
    ng = len(gsegs)

    def body(q_ref, k_ref, v_ref, do_ref, lse_ref, dl_ref, *refs):
        g_refs, (dqt_ref, dk_ref, dv_ref), recv_refs = refs[:ng], refs[ng:ng + 3], refs[ng + 3:2 * ng + 3]
        send_sems, recv_sems = refs[2 * ng + 3:]
        first, last = _grid_ends(3)

        @pl.when(first)
        def _():
            for i in range(ng):
                _scatter_start(g_refs[i], recv_refs[i], send_sems, recv_sems, 3 * i)

        @pl.when(last)
        def _():
            for i in range(ng):
                _scatter_wait(g_refs[i], recv_refs[i], send_sems, recv_sems, 3 * i)

        j = pl.program_id(2)
        qv, kv, dov = q_ref[...], k_ref[...], do_ref[...]
        pt = jnp.exp2(_dot_nt(kv, qv) * SCALE_LOG2E - lse_ref[...])
        dv_ref[...] = _dot(pt.astype(BF16), dov)
        dpt = _dot_nt(v_ref[...], dov)
        dst = (pt * (dpt - dl_ref[...])).astype(BF16)
        dk_ref[...] = _dot(dst, qv) * ATTN_SCALE
        part = _dot_tn(kv, dst)

        @pl.when(j == 0)
        def _():
            dqt_ref[...] = part

        @pl.when(j != 0)
        def _():
            dqt_ref[...] += part

        @pl.when(j == nk - 1)
        def _():
            dqt_ref[...] = dqt_ref[...] * ATTN_SCALE

    row = pl.BlockSpec((None, 1, s), lambda b, h, j: (h, 0, b))
    return pl.pallas_call(
        body, name="attn_bwd", grid=(n_batch, B_HEADS, nk),
        in_specs=[pl.BlockSpec((s, QK_PAD), lambda b, h, j: (b, h)),
                  pl.BlockSpec((tk, QK_PAD), lambda b, h, j: (b * nk + j, h)),
                  pl.BlockSpec((tk, B_V), lambda b, h, j: (b * nk + j, h)),
                  pl.BlockSpec((s, B_V), lambda b, h, j: (b, h)), row, row] + [_HBM] * ng,
        out_specs=[pl.BlockSpec((QK_PAD, s), lambda b, h, j: (h, b)),
                   pl.BlockSpec((tk, QK_PAD), lambda b, h, j: (b * nk + j, h)),
                   pl.BlockSpec((tk, B_V), lambda b, h, j: (b * nk + j, h))] + [_HBM] * ng,
        out_shape=[jax.ShapeDtypeStruct((B_HEADS * QK_PAD, t), F32), jax.ShapeDtypeStruct((t, B_HEADS * QK_PAD), F32),
                   jax.ShapeDtypeStruct((t, B_HEADS * B_V), F32)]
        + [jax.ShapeDtypeStruct((3,) + g.shape[1:], g.dtype) for g in gsegs],
        scratch_shapes=[pltpu.SemaphoreType.DMA((3 * ng,)), pltpu.SemaphoreType.DMA((3 * ng,))],
        compiler_params=_params(3),
    )(q, k, v, do16, lse, delta, *gsegs)


def _group_ones16():
    r = _iota2((A_WIDTH, A_WIDTH), 0) // 64
    c = _iota2((A_WIDTH, A_WIDTH), 1) // 64
    return (r == c).astype(BF16)


def _head_rms(o, ones16):
    return lax.rsqrt(_exact_right(o * o, ones16) * (1.0 / 64.0) + EPS)


def _out_fwd(o_f, o_r, hg, o_attn, x, g_hn, g_mla, w_out, g2):
    t = x.shape[0]
    tm = _row_tile(t)

    def body(of_ref, or_ref, hg_ref, oa_ref, x_ref, ghn_ref, gm_ref, w_ref, g2_ref, y_ref, x2_ref, h2_ref):
        ones16 = _group_ones16()
        o = of_ref[...] + or_ref[...]
        hgv = hg_ref[...]
        ya = ((o * _head_rms(o, ones16)) * ghn_ref[...]) * (hgv * _sigmoid(hgv))
        yb = _rms_fwd(oa_ref[...], gm_ref[...])
        ya16, yb16 = ya.astype(BF16), yb.astype(BF16)
        y_ref[:, 0:A_WIDTH] = ya16
        y_ref[:, A_WIDTH:D_MODEL] = yb16
        x2 = x_ref[...] + _dot(ya16, w_ref[0:A_WIDTH, :]) + _dot(yb16, w_ref[A_WIDTH:D_MODEL, :])
        x2_ref[...] = x2
        h2_ref[...] = _rms_fwd(x2, g2_ref[...]).astype(BF16)

    a512 = _rows(tm, A_WIDTH)
    return pl.pallas_call(
        body, name="out_fwd", grid=(t // tm,),
        in_specs=[a512, a512, a512, a512, _rows(tm, D_MODEL), _full((1, A_WIDTH)), _full((1, A_WIDTH)),
                  _full((D_MODEL, D_MODEL)), _full((1, D_MODEL))],
        out_specs=[_rows(tm, D_MODEL)] * 3,
        out_shape=[jax.ShapeDtypeStruct((t, D_MODEL), BF16), jax.ShapeDtypeStruct((t, D_MODEL), F32),
                   jax.ShapeDtypeStruct((t, D_MODEL), BF16)],
        compiler_params=_params(1),
    )(o_f, o_r, hg, o_attn, x, g_hn, g_mla, w_out, g2)


def _out_bwd(dx2_16, o_f, o_r, hg, o_attn, g_hn, g_mla, w_out):
    t = dx2_16.shape[0]
    tm = _row_tile(t)

    def body(dx_ref, of_ref, or_ref, hg_ref, oa_ref, ghn_ref, gm_ref, w_ref,
             do_ref, dhg_ref, doa_ref, dl_ref, dghn_ref, dgm_ref):
        first = pl.program_id(0) == 0
        ones16 = _group_ones16()
        dxv = dx_ref[...]
        dya = _dot_nt(dxv, w_ref[0:A_WIDTH, :])
        dyb = _dot_nt(dxv, w_ref[A_WIDTH:D_MODEL, :])
        o = of_ref[...] + or_ref[...]
        rh = _head_rms(o, ones16)
        oh = o * rh
        hgv = hg_ref[...]
        sg = _sigmoid(hgv)
        sl = hgv * sg
        ghn = ghn_ref[...]
        dhg_ref[...] = ((dya * (oh * ghn)) * (sg * (1.0 + hgv * (1.0 - sg)))).astype(BF16)
        _acc_rows(dghn_ref, dya * sl * oh, first)
        doh = dya * sl * ghn
        do_ref[...] = rh * (doh - oh * (_exact_right(doh * oh, ones16) * (1.0 / 64.0)))
        oav = oa_ref[...]
        doa, dgm = _rms_bwd(dyb, oav, gm_ref[...])
        doa_ref[...] = doa.astype(BF16)
        _acc_rows(dgm_ref, dgm, first)
        sel16 = (_iota2((8, A_WIDTH), 0) == _iota2((8, A_WIDTH), 1) // B_V).astype(BF16)
        x1, x2, x3 = _split3(doa * oav)
        delta = _dot_nt(sel16, x1) + _dot_nt(sel16, x2) + _dot_nt(sel16, x3)
        for h in range(B_HEADS):
            dl_ref[h] = delta[h:h + 1, :]

    a512 = _rows(tm, A_WIDTH)
    return pl.pallas_call(
        body, name="out_bwd", grid=(t // tm,),
        in_specs=[_rows(tm, D_MODEL), a512, a512, a512, a512, _full((1, A_WIDTH)), _full((1, A_WIDTH)),
                  _full((D_MODEL, D_MODEL))],
        out_specs=[a512, a512, a512, pl.BlockSpec((B_HEADS, 1, tm), lambda i: (0, 0, i)),
                   _full((1, A_WIDTH)), _full((1, A_WIDTH))],
        out_shape=[jax.ShapeDtypeStruct((t, A_WIDTH), F32)] + [jax.ShapeDtypeStruct((t, A_WIDTH), BF16)] * 2
        + [jax.ShapeDtypeStruct((B_HEADS, 1, t), F32)]
        + [jax.ShapeDtypeStruct((1, A_WIDTH), F32)] * 2,
        compiler_params=_params(1),
    )(dx2_16, o_f, o_r, hg, o_attn, g_hn, g_mla, w_out)


def _ffn_fwd_bwd(h2, x2, target, w_gate, w_up, w_down, g_f, g2):
    t = x2.shape[0]
    tm = _row_tile(t)
    inv_d = 1.0 / D_MODEL

    def body(h2_ref, x2_ref, tg_ref, wg_ref, wu_ref, wd_ref, gf_ref, g2_ref,
             act_ref, dgate_ref, dup_ref, dx3_ref, dx2_ref, dx2h_ref, loss_ref, dgf_ref, dg2_ref):
        first = pl.program_id(0) == 0
        h2v = h2_ref[...]
        gate = _dot_nt(h2v, wg_ref[...])
        up = _dot_nt(h2v, wu_ref[...])
        sg = _sigmoid(gate)
        sl = gate * sg
        act16 = (sl * up).astype(BF16)
        act_ref[...] = act16
        x2v = x2_ref[...]
        x3 = x2v + _dot(act16, wd_ref[...])
        r3 = lax.rsqrt(jnp.mean(x3 * x3, axis=-1, keepdims=True) + EPS)
        x3h = x3 * r3
        gf = gf_ref[...]
        err = x3h * gf - tg_ref[...]
        part = 0.5 * jnp.sum(jnp.mean(err * err, axis=-1, keepdims=True), axis=0, keepdims=True)

        @pl.when(first)
        def _():
            loss_ref[...] = jnp.zeros_like(loss_ref)

        loss_ref[...] += part
        dy = err * inv_d
        _acc_rows(dgf_ref, dy * x3h, first)
        dx3h = dy * gf
        dx3 = r3 * (dx3h - x3h * jnp.mean(dx3h * x3h, axis=-1, keepdims=True))
        dx3_16 = dx3.astype(BF16)
        dx3_ref[...] = dx3_16
        da = _dot_nt(dx3_16, wd_ref[...])
        dup16 = (da * sl).astype(BF16)
        dgate16 = (da * up * (sg * (1.0 + gate * (1.0 - sg)))).astype(BF16)
        dup_ref[...] = dup16
        dgate_ref[...] = dgate16
        dh2 = _dot(dgate16, wg_ref[...]) + _dot(dup16, wu_ref[...])
        dx2n, dg2 = _rms_bwd(dh2, x2v, g2_ref[...])
        _acc_rows(dg2_ref, dg2, first)
        dx2 = dx3 + dx2n
        dx2_ref[...] = dx2
        dx2h_ref[...] = dx2.astype(BF16)

    return pl.pallas_call(
        body, name="ffn_fwd_bwd", grid=(t // tm,),
        in_specs=[_rows(tm, D_MODEL), _rows(tm, D_MODEL), _rows(tm, D_MODEL), _full((D_FF, D_MODEL), True),
                  _full((D_FF, D_MODEL), True), _full((D_FF, D_MODEL), True), _full((1, D_MODEL)), _full((1, D_MODEL))],
        out_specs=[_rows(tm, D_FF), _rows(tm, D_FF), _rows(tm, D_FF), _rows(tm, D_MODEL), _rows(tm, D_MODEL),
                   _rows(tm, D_MODEL), _full((8, 128)), _full((1, D_MODEL)), _full((1, D_MODEL))],
        out_shape=[jax.ShapeDtypeStruct((t, D_FF), BF16)] * 3
        + [jax.ShapeDtypeStruct((t, D_MODEL), BF16), jax.ShapeDtypeStruct((t, D_MODEL), F32),
           jax.ShapeDtypeStruct((t, D_MODEL), BF16), jax.ShapeDtypeStruct((8, 128), F32),
           jax.ShapeDtypeStruct((1, D_MODEL), F32), jax.ShapeDtypeStruct((1, D_MODEL), F32)],
        compiler_params=_params(1),
    )(h2, x2, target, w_gate, w_up, w_down, g_f, g2)


def _wgrad(a, b, name):
    t, m = a.shape
    n = b.shape[1]
    bm = _pick(m, 1664)
    bn = _pick(n, 1664)
    tk = _pick(t, 2048, 16)
    nk = t // tk

    def body(a_ref, b_ref, o_ref, o16_ref):
        k = pl.program_id(2)
        part = _dot_tn(a_ref[...], b_ref[...])

        @pl.when(k == 0)
        def _():
            o_ref[...] = part

        @pl.when(k != 0)
        def _():
            o_ref[...] += part

        @pl.when(k == nk - 1)
        def _():
            o16_ref[...] = o_ref[...].astype(BF16)

    out = pl.BlockSpec((bm, bn), lambda i, j, k: (i, j))
    return pl.pallas_call(
        body, name=name, grid=(m // bm, n // bn, nk),
        in_specs=[pl.BlockSpec((tk, bm), lambda i, j, k: (k, i)), pl.BlockSpec((tk, bn), lambda i, j, k: (k, j))],
        out_specs=[out, out],
        out_shape=[jax.ShapeDtypeStruct((m, n), F32), jax.ShapeDtypeStruct((m, n), BF16)],
        compiler_params=_params(3),
    )(a, b)


def _rope_tables(seq):
    inv = 1.0 / (ROPE_THETA ** (jnp.arange(0, B_ROPE, 2, dtype=F32) / B_ROPE))
    ang = jnp.arange(seq, dtype=F32)[:, None] * inv[None, :]
    cos, sin = jnp.cos(ang), jnp.sin(ang)
    zeros = jnp.zeros((seq, 64), F32)
    return jnp.concatenate([cos, cos, zeros], axis=1), jnp.concatenate([-sin, sin, zeros], axis=1)


def _pad_weights(w_in_t, w_q_t, w_kv_b):
    w_in_p = jnp.pad(w_in_t, ((0, D_IN_PAD - D_IN), (0, 0)))
    w_q_p = jnp.pad(w_q_t.reshape(B_HEADS, B_NOPE + B_ROPE, Q_LORA), ((0, 0), (0, 64), (0, 0))).reshape(1024, Q_LORA)
    kv = w_kv_b.reshape(KV_LORA, B_HEADS, B_NOPE + B_V)
    w_kv_p = jnp.concatenate([kv[:, :, :B_NOPE].reshape(KV_LORA, 512), kv[:, :, B_NOPE:].reshape(KV_LORA, 512)], axis=1)
    return w_in_p, w_q_p, w_kv_p


def _unpad_kv(g_kv_p):
    return jnp.concatenate([g_kv_p[:, :512].reshape(KV_LORA, B_HEADS, B_NOPE),
                            g_kv_p[:, 512:].reshape(KV_LORA, B_HEADS, B_V)], axis=2).reshape(KV_LORA, 1024)


_OWN_ROWS = dict(w_in=(272, 3), w_q_b=(64, 4), w_kv_b=(256, 0), w_out=(256, 1), w_gate=(176, 4), w_up=(176, 4), w_down=(176, 4))


def _local_step(x, target, lbl, g1, g_hn, g_qa, g_kva, g_mla, g2, g_f, w_in, w_q_b, w_kv_b, late_shards):
    n_batch, seq, _ = x.shape
    t = n_batch * seq
    x = x.reshape(t, D_MODEL)
    target = target.reshape(t, D_MODEL)
    w_in_p, w_q_p, w_kv_p = _pad_weights(w_in, w_q_b, w_kv_b)
    cosx, sinx = _rope_tables(seq)

    h1, hq, hi, hff, hfb, hg, cq, ckv, kr = _in_fwd(x, g1, w_in_p)
    o_f, st_f = _hgrn_fwd(hq, hi, hff, lbl, n_batch=n_batch, direction=0)
    o_r, st_r = _hgrn_fwd(hq, hi, hfb, lbl, n_batch=n_batch, direction=1)
    q, k, v, cqn, ckvn = _mla_prep(cq, ckv, kr, g_qa, g_kva, w_q_p, w_kv_p, cosx, sinx, n_batch=n_batch)
    o_attn, lse, *gathered = _attn_fwd(q, k, v, late_shards, n_batch=n_batch)
    w_out, w_gate, w_up, w_down = (_join_shards(n, a) for n, a in zip(_LATE, gathered))
    ycat, x2, h2 = _out_fwd(o_f, o_r, hg, o_attn, x, g_hn, g_mla, w_out, g2)
    act, dgate, dup, dx3_16, dx2, dx2_16, loss, dg_f, dg2 = _ffn_fwd_bwd(h2, x2, target, w_gate, w_up, w_down, g_f, g2)
    full = dict(w_out=_wgrad(ycat, dx2_16, "wgrad_out"), w_gate=_wgrad(dgate, h2, "wgrad_gate"),
                w_up=_wgrad(dup, h2, "wgrad_up"), w_down=_wgrad(act, dx3_16, "wgrad_down"))
    do_h, dhg, do_attn16, delta, dg_hn, dg_mla = _out_bwd(dx2_16, o_f, o_r, hg, o_attn, g_hn, g_mla, w_out)
    dqt, dk, dv, *recv_late = _attn_bwd(q, k, v, do_attn16, lse, delta,
                                        [_segments(n, full[n][1]) for n in _LATE], n_batch=n_batch)
    dcq, dckv, dkr, dqp16, dkvp16, dg_qa, dg_kva = _mla_prep_bwd(dqt, dk, dv, cq, ckv, g_qa, g_kva, w_q_p, w_kv_p,
                                                                  cosx, sinx, n_batch=n_batch)
    dq_f, dv_f, dz_f, dl_f = _hgrn_bwd(hq, hi, hff, do_h, st_f, lbl, n_batch=n_batch, direction=0)
    dq_r, dv_r, dz_r, dl_r = _hgrn_bwd(hq, hi, hfb, do_h, st_r, lbl, n_batch=n_batch, direction=1)
    dx, dproj16, dg1 = _in_bwd(dq_f, dq_r, dv_f, dv_r, dz_f, dz_r, dhg, hq, dcq, dckv, dkr, dx2, x, g1, w_in_p)

    full.update(w_in=_wgrad(dproj16, h1, "wgrad_in"), w_q_b=_wgrad(dqp16, cqn, "wgrad_q_b"))
    g_kv = _unpad_kv(_wgrad(ckvn, dkvp16, "wgrad_kv_b")[0])
    early16 = dict(w_in=full["w_in"][1][:D_IN].reshape(N_CHIPS, D_IN // N_CHIPS, D_MODEL),
                   w_q_b=full["w_q_b"][1].reshape(B_HEADS, QK_PAD, Q_LORA)[:, :B_NOPE + B_ROPE],
                   w_kv_b=_segments("w_kv_b", g_kv).astype(BF16))
    small = dict(norm1_g=dg1, lb_logits=jnp.stack([dl_f, dl_r]), hgrn_norm_g=dg_hn, q_a_norm_g=dg_qa, kv_a_norm_g=dg_kva,
                 mla_norm_g=dg_mla, norm2_g=dg2, final_norm_g=dg_f, loss=loss[0:1, 0])
    return dx.reshape(n_batch, seq, D_MODEL), small, {**{n: f for n, (f, _) in full.items()}, "w_kv_b": g_kv}, early16, \
        dict(zip(_LATE, recv_late))


_HBM = pl.BlockSpec(memory_space=pltpu.HBM)
_MESH = pl.DeviceIdType.MESH


def _place():
    x, y, c = lax.axis_index("x"), lax.axis_index("y"), lax.axis_index("c")
    other_chips = [(1 - x, y), (x, 1 - y), (1 - x, 1 - y)]
    return x, y, c, other_chips


def _gather_copies(w_ref, wall_ref, send_sems, recv_sems, local_sem, base=0):
    x, y, c, chips = _place()
    mine = 2 * x + y

    def mk(j, chip_index, to):
        return pltpu.make_async_remote_copy(src_ref=w_ref, dst_ref=wall_ref.at[chip_index], send_sem=send_sems.at[base + j],
                                            recv_sem=recv_sems.at[base + j], device_id=to, device_id_type=_MESH)

    local = pltpu.make_async_copy(w_ref, wall_ref.at[mine], local_sem)
    sends = [mk(j, mine, (*chip, c)) for j, chip in enumerate(chips)]
    recvs = [mk(j, 2 * px + py, (x, y, c)) for j, (px, py) in enumerate(chips)]
    return local, sends, recvs


def _gather_start(*refs):
    local, sends, _ = _gather_copies(*refs)
    local.start()
    for cp in sends:
        cp.start()


def _gather_wait(*refs):
    local, sends, recvs = _gather_copies(*refs)
    for cp in recvs:
        cp.wait_recv()
    for cp in sends:
        cp.wait_send()
    local.wait()


def _scatter_copies(g_ref, recv_ref, send_sems, recv_sems, base=0):
    x, y, c, chips = _place()

    def mk(j, src_index, to):
        return pltpu.make_async_remote_copy(src_ref=g_ref.at[src_index], dst_ref=recv_ref.at[j],
                                            send_sem=send_sems.at[base + j], recv_sem=recv_sems.at[base + j],
                                            device_id=to, device_id_type=_MESH)

    sends = [mk(j, 2 * px + py, (px, py, c)) for j, (px, py) in enumerate(chips)]
    recvs = [mk(j, 0, (x, y, c)) for j in range(3)]
    return sends, recvs


def _scatter_start(*refs):
    for cp in _scatter_copies(*refs)[0]:
        cp.start()


def _scatter_wait(*refs):
    sends, recvs = _scatter_copies(*refs)
    for cp in recvs:
        cp.wait_recv()
    for cp in sends:
        cp.wait_send()


def _half(ref, which, axis):
    n = ref.shape[axis] // 2
    idx = [slice(None)] * len(ref.shape)
    idx[axis] = pl.ds(which * n, n)
    return ref.at[tuple(idx)]


def _split_axis(a):
    return 1 if a.shape[1] % 256 == 0 else 0


def _gather_halves(src_ref, dst_ref, send_sems, recv_sems, local_sem, base, axis):
    x, y, c, chips = _place()
    mine = 2 * x + y

    def remote(src, dst, n, to):
        return pltpu.make_async_remote_copy(src_ref=src, dst_ref=dst, send_sem=send_sems.at[base + n],
                                            recv_sem=recv_sems.at[base + n], device_id=to, device_id_type=_MESH)

    def slot(chip_index, which):
        return _half(dst_ref.at[chip_index], which, axis)

    local = pltpu.make_async_copy(src_ref, dst_ref.at[mine], local_sem)
    first = [remote(_half(src_ref, c, axis), slot(mine, c), j, (*chip, c)) for j, chip in enumerate(chips)]
    landed = [remote(_half(src_ref, c, axis), slot(2 * px + py, c), j, (x, y, c)) for j, (px, py) in enumerate(chips)]
    passed = [remote(slot(2 * px + py, c), slot(2 * px + py, c), 3 + j, (x, y, 1 - c)) for j, (px, py) in enumerate(chips)]
    handed = [remote(_half(src_ref, c, axis), slot(2 * px + py, 1 - c), 3 + j, (x, y, c)) for j, (px, py) in enumerate(chips)]

    def start():
        local.start()
        for cp in first:
            cp.start()

    def pass_on():
        for arrived, onward in zip(landed, passed):
            arrived.wait_recv()
            onward.start()

    def finish():
        for cp in handed:
            cp.wait_recv()
        for cp in first + passed:
            cp.wait_send()
        local.wait()

    return start, pass_on, finish


def _gather_shards(srcs, small):
    n = len(srcs)

    def body(*refs):
        src_refs, s_ref, dst_refs, sall_ref = refs[:n], refs[n], refs[n + 1:2 * n + 1], refs[2 * n + 1]
        send_sems, recv_sems, local_sems = refs[2 * n + 2:]
        _gather_start(s_ref, sall_ref, send_sems, recv_sems, local_sems.at[n], 6 * n)
        steps = [_gather_halves(src_refs[i], dst_refs[i], send_sems, recv_sems, local_sems.at[i], 6 * i, _split_axis(srcs[i]))
                 for i in range(n)]
        for phase in range(3):
            for step in steps:
                step[phase]()
        _gather_wait(s_ref, sall_ref, send_sems, recv_sems, local_sems.at[n], 6 * n)

    return pl.pallas_call(
        body, name="gather_shards", in_specs=[_HBM] * (n + 1), out_specs=[_HBM] * (n + 1),
        out_shape=[jax.ShapeDtypeStruct((N_CHIPS,) + a.shape, a.dtype) for a in list(srcs) + [small]],
        scratch_shapes=[pltpu.SemaphoreType.DMA((6 * n + 3,)), pltpu.SemaphoreType.DMA((6 * n + 3,)),
                        pltpu.SemaphoreType.DMA((n + 1,))],
    )(*srcs, small)


def _scatter_grads(gsegs, small):
    n = len(gsegs)

    def body(*refs):
        g_refs, s_ref, recv_refs, sall_ref = refs[:n], refs[n], refs[n + 1:2 * n + 1], refs[2 * n + 1]
        send_sems, recv_sems, local_sem = refs[2 * n + 2:]
        x, y, c, _ = _place()
        me = 4 * x + 2 * y + c
        flips = [(fx, fy, fc) for fx in (0, 1) for fy in (0, 1) for fc in (0, 1)][1:]

        def peer(f):
            return tuple((1 - a) if b else a for a, b in zip((x, y, c), f))

        def sm(r, index, to):
            return pltpu.make_async_remote_copy(src_ref=s_ref, dst_ref=sall_ref.at[index], send_sem=send_sems.at[3 * n + r],
                                                recv_sem=recv_sems.at[3 * n + r], device_id=to, device_id_type=_MESH)

        local = pltpu.make_async_copy(s_ref, sall_ref.at[me], local_sem)
        local.start()
        for i in range(n):
            _scatter_start(g_refs[i], recv_refs[i], send_sems, recv_sems, 3 * i)
        sends = [sm(r, me, peer(f)) for r, f in enumerate(flips)]
        for cp in sends:
            cp.start()
        for i in range(n):
            _scatter_wait(g_refs[i], recv_refs[i], send_sems, recv_sems, 3 * i)
        for r, f in enumerate(flips):
            px, py, pc = peer(f)
            sm(r, 4 * px + 2 * py + pc, (x, y, c)).wait_recv()
        for cp in sends:
            cp.wait_send()
        local.wait()

    return pl.pallas_call(
        body, name="scatter_grads", in_specs=[_HBM] * (n + 1), out_specs=[_HBM] * (n + 1),
        out_shape=[jax.ShapeDtypeStruct((3,) + g.shape[1:], g.dtype) for g in gsegs]
        + [jax.ShapeDtypeStruct((8,) + small.shape, F32)],
        scratch_shapes=[pltpu.SemaphoreType.DMA((3 * n + 7,)), pltpu.SemaphoreType.DMA((3 * n + 7,)),
                        pltpu.SemaphoreType.DMA(())],
    )(*gsegs, small)


def _swap_sibling(parts):
    n = len(parts)

    def body(*refs):
        send_sems, recv_sems = refs[2 * n:]
        x, y, c, _ = _place()
        cps = [pltpu.make_async_remote_copy(src_ref=refs[i], dst_ref=refs[n + i], send_sem=send_sems.at[i],
                                            recv_sem=recv_sems.at[i], device_id=(x, y, 1 - c), device_id_type=_MESH)
               for i in range(n)]
        for cp in cps:
            cp.start()
        for cp in cps:
            cp.wait()

    return pl.pallas_call(
        body, name="swap_sibling", in_specs=[_HBM] * n, out_specs=[_HBM] * n,
        out_shape=[jax.ShapeDtypeStruct(p.shape, p.dtype) for p in parts],
        scratch_shapes=[pltpu.SemaphoreType.DMA((n,)), pltpu.SemaphoreType.DMA((n,))],
    )(*parts)


def _sum_segments(chip, full, recv, tm, pitch, name):
    _, rows, cols = recv.shape

    def body(chip_ref, o_ref, r_ref, out_ref):
        acc = o_ref[...]
        for j in range(3):
            acc = acc + r_ref[j].astype(F32)
        out_ref[...] = acc

    return pl.pallas_call(
        body, name=name,
        grid_spec=pltpu.PrefetchScalarGridSpec(
            num_scalar_prefetch=1, grid=(rows // tm,),
            in_specs=[pl.BlockSpec((tm, cols), lambda i, c: (c[0] * pitch + i, 0)),
                      pl.BlockSpec((3, tm, cols), lambda i, c: (0, i, 0))],
            out_specs=pl.BlockSpec((tm, cols), lambda i, c: (i, 0))),
        out_shape=jax.ShapeDtypeStruct((rows, cols), F32), compiler_params=_params(1),
    )(chip, full, recv)


def _sum_devices(sall):
    def body(s_ref, o_ref):
        acc = s_ref[0]
        for d in range(1, 8):
            acc = acc + s_ref[d]
        o_ref[...] = acc

    return pl.pallas_call(body, name="sum_devices", out_shape=jax.ShapeDtypeStruct(sall.shape[1:], F32))(sall)


def _adamw(w, m, v, ga, gb, name):
    rows, cols = w.shape
    tm = _row_tile(rows)
    two = gb is not None

    def body(*refs):
        w_ref, m_ref, v_ref, ga_ref = refs[:4]
        g_ref, d_ref, m2_ref, v2_ref = refs[-4:]
        g = ga_ref[...] + refs[4][...] if two else ga_ref[...]
        m2 = ADAM_B1 * m_ref[...] + (1.0 - ADAM_B1) * g
        v2 = ADAM_B2 * v_ref[...] + (1.0 - ADAM_B2) * (g * g)
        m_hat = m2 / (1.0 - ADAM_B1 ** ADAM_STEP)
        v_hat = v2 / (1.0 - ADAM_B2 ** ADAM_STEP)
        g_ref[...] = g
        d_ref[...] = -ADAM_LR * (m_hat / (jnp.sqrt(v_hat) + ADAM_EPS) + ADAM_WD * w_ref[...])
        m2_ref[...] = m2
        v2_ref[...] = v2

    blk = _rows(tm, cols)
    args = (w, m, v, ga) + ((gb,) if two else ())
    return pl.pallas_call(
        body, name=name, grid=(rows // tm,), in_specs=[blk] * len(args), out_specs=[blk] * 4,
        out_shape=[jax.ShapeDtypeStruct(w.shape, F32)] * 4, compiler_params=_params(1),
    )(*args)


_TRANSPOSED = ("w_in", "w_q_b", "w_gate", "w_up")
_COLUMN_SHARDED = ("w_kv_b",)
_FULL_SHAPES = dict(w_in=(D_IN, D_MODEL), w_q_b=(768, Q_LORA), w_kv_b=(KV_LORA, 1024), w_out=(D_MODEL, D_MODEL),
                    w_gate=(D_FF, D_MODEL), w_up=(D_FF, D_MODEL), w_down=(D_FF, D_MODEL))
_SMALL = (("norm1_g", 1024), ("lb_logits", 2048), ("hgrn_norm_g", 512), ("q_a_norm_g", 384), ("kv_a_norm_g", 256),
          ("mla_norm_g", 512), ("norm2_g", 1024), ("final_norm_g", 1024))
_UPDATE_ROWS = 48


def _pad_rows(a, rows):
    return jnp.pad(a, ((0, rows - a.shape[0]), (0, 0)))


_EARLY = ("w_in", "w_q_b", "w_kv_b")
_LATE = ("w_out", "w_gate", "w_up", "w_down")


def _segments(name, g):
    r, c = g.shape
    if name in _COLUMN_SHARDED:
        return g.reshape(r, N_CHIPS, c // N_CHIPS).transpose(1, 0, 2)
    return g.reshape(N_CHIPS, r // N_CHIPS, c)


def _own_segment(name, g, chip):
    r, c = g.shape
    if name in _COLUMN_SHARDED:
        return lax.dynamic_slice_in_dim(g, chip * (c // N_CHIPS), c // N_CHIPS, axis=1)
    return lax.dynamic_slice_in_dim(g, chip * (r // N_CHIPS), r // N_CHIPS, axis=0)


def _join_shards(name, seg):
    r, c = _FULL_SHAPES[name]
    if name in _COLUMN_SHARDED:
        return seg.transpose(1, 0, 2).reshape(r, c)
    return seg.reshape(r, c)


def kernel(x, norm1_g, w_in, lb_logits, hgrn_norm_g, q_a_norm_g, w_q_b, kv_a_norm_g, w_kv_b, mla_norm_g, w_out, norm2_g, w_gate, w_up, w_down, final_norm_g, loss_target, m_norm1_g, m_w_in, m_lb_logits, m_hgrn_norm_g, m_q_a_norm_g, m_w_q_b, m_kv_a_norm_g, m_w_kv_b, m_mla_norm_g, m_w_out, m_norm2_g, m_w_gate, m_w_up, m_w_down, m_final_norm_g, v_norm1_g, v_w_in, v_lb_logits, v_hgrn_norm_g, v_q_a_norm_g, v_w_q_b, v_kv_a_norm_g, v_w_kv_b, v_mla_norm_g, v_w_out, v_norm2_g, v_w_gate, v_w_up, v_w_down, v_final_norm_g):
    names = ("norm1_g", "w_in", "lb_logits", "hgrn_norm_g", "q_a_norm_g", "w_q_b", "kv_a_norm_g", "w_kv_b", "mla_norm_g",
             "w_out", "norm2_g", "w_gate", "w_up", "w_down", "final_norm_g")
    w = dict(zip(names, (norm1_g, w_in, lb_logits, hgrn_norm_g, q_a_norm_g, w_q_b, kv_a_norm_g, w_kv_b, mla_norm_g,
                         w_out, norm2_g, w_gate, w_up, w_down, final_norm_g)))
    m = dict(zip(names, (m_norm1_g, m_w_in, m_lb_logits, m_hgrn_norm_g, m_q_a_norm_g, m_w_q_b, m_kv_a_norm_g, m_w_kv_b,
                         m_mla_norm_g, m_w_out, m_norm2_g, m_w_gate, m_w_up, m_w_down, m_final_norm_g)))
    v = dict(zip(names, (v_norm1_g, v_w_in, v_lb_logits, v_hgrn_norm_g, v_q_a_norm_g, v_w_q_b, v_kv_a_norm_g, v_w_kv_b,
                         v_mla_norm_g, v_w_out, v_norm2_g, v_w_gate, v_w_up, v_w_down, v_final_norm_g)))
    matrices = _EARLY + _LATE
    chip = 2 * lax.axis_index("x") + lax.axis_index("y")

    def shard2d(a, n):
        return jnp.swapaxes(a[0], 0, 1) if n in _TRANSPOSED else a[0]

    def unshard2d(a, n):
        return (jnp.swapaxes(a, 0, 1) if n in _TRANSPOSED else a)[None]

    w16 = {n: shard2d(w[n], n).astype(BF16) for n in matrices}
    lb8 = _pad_rows(lb_logits.reshape(4, 128), 8)
    *early, lball = _gather_shards([w16[n] for n in _EARLY], lb8)
    lbl = lball[:, :4].reshape(N_CHIPS, 2, 2, 128).transpose(1, 2, 0, 3).reshape(2, 2, A_WIDTH)

    grad_x, g, full, early16, recv = _local_step(
        x, loss_target, lbl, norm1_g, hgrn_norm_g, q_a_norm_g, kv_a_norm_g, mla_norm_g, norm2_g, final_norm_g[None, :],
        *(_join_shards(n, a) for n, a in zip(_EARLY, early)), [w16[n] for n in _LATE])

    n_small = sum(size for _, size in _SMALL)
    small = jnp.concatenate([g[n].reshape(-1) for n, _ in _SMALL] + [g["loss"], jnp.zeros((SMALL_ROWS * 128 - n_small - 1,), F32)])
    *recv_early, small_all = _scatter_grads([early16[n] for n in _EARLY], small.reshape(SMALL_ROWS, 128))
    recv.update(zip(_EARLY, recv_early))
    full["w_kv_b"] = _own_segment("w_kv_b", full["w_kv_b"], chip)
    chip1 = chip.reshape(1).astype(jnp.int32)
    parts = [_sum_segments(chip1, full[n], recv[n], *_OWN_ROWS[n], f"sum_{n}") for n in matrices]
    sibs = _swap_sibling(parts)
    small_sum = _sum_devices(small_all).reshape(-1)
    loss = small_sum[n_small]

    out = {}
    for n, part, sib in zip(matrices, parts, sibs):
        res = _adamw(shard2d(w[n], n), shard2d(m[n], n), shard2d(v[n], n), part, sib, f"adamw_{n}")
        out[n] = tuple(unshard2d(r, n) for r in res)

    small_g, off = {}, 0
    for n, size in _SMALL:
        small_g[n] = small_sum[off:off + size]
        off += size
    small_g["lb_logits"] = lax.dynamic_slice_in_dim(small_g["lb_logits"].reshape(2, 2, A_WIDTH), chip * 128, 128, axis=2)
    small_names = tuple(n for n, _ in _SMALL)

    def pack(d):
        return _pad_rows(jnp.concatenate([d[n].reshape(-1) for n in small_names]).reshape(-1, 128), _UPDATE_ROWS)

    res = _adamw(pack(w), pack(m), pack(v), pack(small_g), None, "adamw_small")
    off = 0
    flat = [r.reshape(-1) for r in res]
    for n in small_names:
        size = w[n].size
        out[n] = tuple(f[off:off + size].reshape(w[n].shape) for f in flat)
        off += size

    return (loss, grad_x) + tuple(out[n][i] for i in range(4) for n in names)
```

```python
import functools

import jax
import jax.numpy as jnp
from jax import lax
from jax.experimental import pallas as pl
from jax.experimental.pallas import tpu as pltpu

F32 = jnp.float32
BF16 = jnp.bfloat16

D_MODEL = 1024
A_WIDTH = 512
HEAD_PAIRS = 4
CHUNK = 64
B_HEADS = 4
B_NOPE = 128
B_ROPE = 64
B_V = 128
QK_PAD = 256
Q_LORA = 384
KV_LORA = 256
D_FF = 2816
D_IN = 3264
D_IN_PAD = 3328
IN_WIDTHS = (512, 512, 512, 512, 512, Q_LORA, KV_LORA, 128)
ROPE_THETA = 10000.0
EPS = 1e-6
ATTN_SCALE = (B_NOPE + B_ROPE) ** -0.5
LOG2E = 1.4426950408889634
SCALE_LOG2E = ATTN_SCALE * LOG2E

ADAM_LR = 0.001
ADAM_B1 = 0.9
ADAM_B2 = 0.999
ADAM_EPS = 1e-08
ADAM_WD = 0.01
ADAM_STEP = 10

VMEM_LIMIT_BYTES = 60 * 1024 * 1024
N_CHIPS = 4
SMALL_ROWS = 56


def _params(n_axes):
    return pltpu.CompilerParams(dimension_semantics=("arbitrary",) * n_axes,
                                vmem_limit_bytes=VMEM_LIMIT_BYTES)


def _dot(a, b):
    return jnp.dot(a, b, preferred_element_type=F32)


def _dot_nt(a, b):
    return lax.dot_general(a, b, (((1,), (1,)), ((), ())), preferred_element_type=F32)


def _dot_tn(a, b):
    return lax.dot_general(a, b, (((0,), (0,)), ((), ())), preferred_element_type=F32)


def _split3(x):
    x1 = x.astype(BF16)
    r = x - x1.astype(F32)
    x2 = r.astype(BF16)
    x3 = (r - x2.astype(F32)).astype(BF16)
    return x1, x2, x3


def _exact_left(m16, x):
    x1, x2, x3 = _split3(x)
    return _dot(m16, x1) + _dot(m16, x2) + _dot(m16, x3)


def _exact_right(x, m16):
    x1, x2, x3 = _split3(x)
    return _dot(x1, m16) + _dot(x2, m16) + _dot(x3, m16)


def _iota2(shape, dim):
    return lax.broadcasted_iota(jnp.int32, shape, dim)


def _sigmoid(x):
    return jax.nn.sigmoid(x)


def _pick(dim, cap, mult=128):
    if dim <= cap:
        return dim
    best = None
    for d in range(mult, cap + 1, mult):
        if dim % d == 0:
            best = d
    assert best is not None, (dim, cap, mult)
    return best


def _row_tile(t, cap=256):
    return _pick(t, cap, 8)


ROW_GROUP = 256


def _row_groups(tm):
    size = min(tm, ROW_GROUP)
    return [slice(r, r + size) for r in range(0, tm, size)]


def _full(shape, single=False):
    if single:
        return pl.BlockSpec(shape, lambda *_: (0,) * len(shape), pipeline_mode=pl.Buffered(1))
    return pl.BlockSpec(shape, lambda *_: (0,) * len(shape))


def _rows(tm, width):
    return pl.BlockSpec((tm, width), lambda i: (i, 0))


def _acc_rows(ref, val, first):
    s = jnp.sum(val, axis=0, keepdims=True)

    @pl.when(first)
    def _():
        ref[...] = s

    @pl.when(jnp.logical_not(first))
    def _():
        ref[...] += s


def _in_fwd(x, g1, w_in_p):
    t = x.shape[0]
    tm = _row_tile(t, 2 * ROW_GROUP)

    def body(x_ref, g_ref, w_ref, h_ref, *outs):
        gain = g_ref[...]

        def group(xv):
            r = lax.rsqrt(jnp.mean(xv * xv, axis=-1, keepdims=True) + EPS)
            h = ((xv * r) * gain).astype(BF16)
            proj, off = [], 0
            for w in IN_WIDTHS:
                proj.append(_dot_nt(h, w_ref[off:off + w, :]))
                off += w
            return h, proj

        groups = _row_groups(tm)
        done = [group(xv) for xv in [x_ref[g, :] for g in groups]]
        for g, (h, proj) in zip(groups, done):
            h_ref[g, :] = h
            for o_ref, p in zip(outs, proj):
                o_ref[g, :] = p

    return pl.pallas_call(
        body, name="in_fwd", grid=(t // tm,),
        in_specs=[_rows(tm, D_MODEL), _full((1, D_MODEL)), _full((D_IN_PAD, D_MODEL))],
        out_specs=[_rows(tm, D_MODEL)] + [_rows(tm, w) for w in IN_WIDTHS],
        out_shape=[jax.ShapeDtypeStruct((t, D_MODEL), BF16)]
        + [jax.ShapeDtypeStruct((t, w), F32) for w in IN_WIDTHS],
        compiler_params=_params(1),
    )(x, g1, w_in_p)


def _in_bwd(dq_f, dq_r, dv_f, dv_r, dz_f, dz_r, dhg, hq, dcq, dckv, dkr, dx2, x, g1, w_in_p):
    t = x.shape[0]
    tm = _row_tile(t, 2 * ROW_GROUP)

    def body(dqf_ref, dqr_ref, dvf_ref, dvr_ref, dzf_ref, dzr_ref, dhg_ref, hq_ref, dcq_ref, dckv_ref,
             dkr_ref, dx2_ref, x_ref, g_ref, w_ref, dx_ref, dp_ref, dg_ref):
        gain = g_ref[...]

        def group(dqf, dqr, dvf, dvr, dzf, dzr, dhg, hqv, dcq, dckv, dkr, dx2, xv):
            sg = _sigmoid(hqv)
            dhq = (dqf + dqr) * (sg * (1.0 + hqv * (1.0 - sg)))
            pieces = [p.astype(BF16) for p in (dhq, dvf + dvr, dzf, dzr, dhg, dcq, dckv, dkr)]
            dh, off = None, 0
            for p16, w in zip(pieces, IN_WIDTHS):
                part = _dot(p16, w_ref[off:off + w, :])
                dh = part if dh is None else dh + part
                off += w
            r = lax.rsqrt(jnp.mean(xv * xv, axis=-1, keepdims=True) + EPS)
            xh = xv * r
            dxh = dh * gain
            return dx2 + r * (dxh - xh * jnp.mean(dxh * xh, axis=-1, keepdims=True)), pieces, dh * xh

        ins = (dqf_ref, dqr_ref, dvf_ref, dvr_ref, dzf_ref, dzr_ref, dhg_ref, hq_ref, dcq_ref, dckv_ref, dkr_ref,
               dx2_ref, x_ref)
        groups = _row_groups(tm)
        done = [group(*vals) for vals in [[ref[g, :] for ref in ins] for g in groups]]
        for g, (dx, pieces, _) in zip(groups, done):
            dx_ref[g, :] = dx
            off = 0
            for p16, w in zip(pieces, IN_WIDTHS):
                dp_ref[g, off:off + w] = p16
                off += w
        _acc_rows(dg_ref, jnp.concatenate([d[2] for d in done], axis=0), pl.program_id(0) == 0)

    a512 = _rows(tm, A_WIDTH)
    return pl.pallas_call(
        body, name="in_bwd", grid=(t // tm,),
        in_specs=[a512] * 8 + [_rows(tm, Q_LORA), _rows(tm, KV_LORA), _rows(tm, 128), _rows(tm, D_MODEL),
                               _rows(tm, D_MODEL), _full((1, D_MODEL)), _full((D_IN_PAD, D_MODEL))],
        out_specs=[_rows(tm, D_MODEL), _rows(tm, D_IN_PAD), _full((1, D_MODEL))],
        out_shape=[jax.ShapeDtypeStruct((t, D_MODEL), F32), jax.ShapeDtypeStruct((t, D_IN_PAD), BF16),
                   jax.ShapeDtypeStruct((1, D_MODEL), F32)],
        compiler_params=_params(1),
    )(dq_f, dq_r, dv_f, dv_r, dz_f, dz_r, dhg, hq, dcq, dckv, dkr, dx2, x, g1, w_in_p)


def _lower_bound(lbl_ref, direction):
    l0 = lbl_ref[direction, 0:1, :]
    l1 = lbl_ref[direction, 1:2, :]
    m = jnp.maximum(l0, l1)
    e0 = jnp.exp(l0 - m)
    e1 = jnp.exp(l1 - m)
    return e0 / (e0 + e1)


def _hgrn_consts(rb, reverse):
    row = _iota2((rb, rb), 0)
    col = _iota2((rb, rb), 1)
    same = (row // CHUNK) == (col // CHUNK)
    tri = jnp.logical_and(same, (col >= row) if reverse else (col <= row))
    tri_t = jnp.logical_and(same, (col <= row) if reverse else (col >= row))
    r128 = _iota2((128, 128), 0)
    c128 = _iota2((128, 128), 1)
    bd = (r128 < 64) == (c128 < 64)
    lane = _iota2((1, 128), 1)
    m0 = (lane < 64).astype(F32)
    return tri, tri_t, bd, (m0, 1.0 - m0)


def _per_chunk(x, fn):
    n = x.shape[0] // CHUNK
    return jnp.concatenate([jnp.broadcast_to(fn(x[c * CHUNK:(c + 1) * CHUNK]), (CHUNK, x.shape[1])) for c in range(n)],
                           axis=0)


def _chunk_cumsum(x, reverse):
    rb = x.shape[0]
    pos = _iota2(x.shape, 0) % CHUNK
    step = 1
    while step < CHUNK:
        if reverse:
            x = x + jnp.where(pos < CHUNK - step, pltpu.roll(x, rb - step, 0), 0.0)
        else:
            x = x + jnp.where(pos >= step, pltpu.roll(x, step, 0), 0.0)
        step *= 2
    return x


def _hgrn_block(z, hqv, lb, reverse):
    sig = _sigmoid(z)
    sn = _sigmoid(-z)
    q = hqv * _sigmoid(hqv)
    f = lb + (1.0 - lb) * sig
    k = (1.0 - lb) * sn
    lf = jnp.log(f)
    cum = _chunk_cumsum(lf, reverse)
    last = _per_chunk(cum, (lambda a: a[0:1]) if reverse else (lambda a: a[CHUNK - 1:CHUNK]))
    e_neg = jnp.exp(-cum)
    e_end = jnp.exp(last - cum)
    a = jnp.exp(cum)
    return dict(sig=sig, sn=sn, q=q, f=f, k=k, a=a, e_neg=e_neg, e_end=e_end,
                q_dec=q * a, k_inv=k * e_neg, k_end=k * e_end, d=jnp.exp(last))


def _hgrn_dims(t, n_batch):
    s = t // n_batch
    rb = _pick(s, 256, CHUNK)
    return s, rb, s // rb, rb // CHUNK


def _hgrn_fwd(hq, hi, hf, lbl, *, n_batch, direction):
    t = hq.shape[0]
    reverse = direction == 1
    s, rb, nb, nc = _hgrn_dims(t, n_batch)

    def tmap(b, j):
        return (b * nb + ((nb - 1 - j) if reverse else j), 0)

    def smap(b, j):
        return (b * nb + ((nb - 1 - j) if reverse else j), 0, 0, 0)

    def body(hq_ref, hi_ref, hf_ref, lbl_ref, o_ref, st_ref, st_scr):
        @pl.when(pl.program_id(1) == 0)
        def _():
            st_scr[...] = jnp.zeros_like(st_scr)

        lb_all = _lower_bound(lbl_ref, direction)
        tri, _, bd, masks = _hgrn_consts(rb, reverse)
        order = range(nc - 1, -1, -1) if reverse else range(nc)

        pairs = [slice(p * 128, (p + 1) * 128) for p in range(HEAD_PAIRS)]
        chunks = [slice(c * CHUNK, (c + 1) * CHUNK) for c in range(nc)]
        w = _hgrn_block(hf_ref[...], hq_ref[...], lb_all, reverse)
        v16 = hi_ref[...].astype(BF16)
        qd16 = w["q_dec"].astype(BF16)
        ki16 = w["k_inv"].astype(BF16)
        ke16 = w["k_end"].astype(BF16)
        sc = [[_dot_nt(jnp.where(mh > 0.0, qd16[:, ls], 0.0).astype(BF16), ki16[:, ls]) for mh in masks] for ls in pairs]
        pv = [[_dot(jnp.where(tri, s_e, 0.0).astype(BF16), v16[:, ls]) for s_e in sc_p] for sc_p, ls in zip(sc, pairs)]
        o_intra = [jnp.where(masks[0] > 0.0, pv_p[0], pv_p[1]) for pv_p in pv]
        ut = [[jnp.where(bd, _dot_tn(v16[rs, ls], ke16[rs, ls]), 0.0) for ls in pairs] for rs in chunks]
        st = [st_scr[p] for p in range(HEAD_PAIRS)]
        for c in order:
            rs = chunks[c]
            inter = [_dot_nt(qd16[rs, ls], st[p].astype(BF16)) for p, ls in enumerate(pairs)]
            for p, ls in enumerate(pairs):
                o_ref[rs, ls] = o_intra[p][rs] + inter[p]
                st_ref[c, p] = st[p]
                st[p] = st[p] * w["d"][c * CHUNK:c * CHUNK + 1, ls] + ut[c][p]
        for p in range(HEAD_PAIRS):
            st_scr[p] = st[p]

    blk = pl.BlockSpec((rb, A_WIDTH), tmap)
    return pl.pallas_call(
        body, name=f"hgrn_fwd_{direction}", grid=(n_batch, nb),
        in_specs=[blk, blk, blk, _full((2, 2, A_WIDTH))],
        out_specs=[blk, pl.BlockSpec((nc, HEAD_PAIRS, 128, 128), smap)],
        out_shape=[jax.ShapeDtypeStruct((t, A_WIDTH), F32),
                   jax.ShapeDtypeStruct((t // CHUNK, HEAD_PAIRS, 128, 128), F32)],
        scratch_shapes=[pltpu.VMEM((HEAD_PAIRS, 128, 128), F32)],
        compiler_params=_params(2),
    )(hq, hi, hf, lbl)


def _hgrn_bwd(hq, hi, hf, do, st, lbl, *, n_batch, direction):
    t = hq.shape[0]
    reverse = direction == 1
    s, rb, nb, nc = _hgrn_dims(t, n_batch)

    def tmap(b, j):
        return (b * nb + (j if reverse else (nb - 1 - j)), 0)

    def smap(b, j):
        return (b * nb + (j if reverse else (nb - 1 - j)), 0, 0, 0)

    def body(hq_ref, hi_ref, hf_ref, do_ref, st_ref, lbl_ref, dq_ref, dv_ref, dz_ref, dl_ref, g_scr, dlb_scr):
        b = pl.program_id(0)
        j = pl.program_id(1)

        @pl.when(jnp.logical_and(b == 0, j == 0))
        def _():
            dlb_scr[...] = jnp.zeros_like(dlb_scr)

        @pl.when(j == 0)
        def _():
            g_scr[...] = jnp.zeros_like(g_scr)

        lb_all = _lower_bound(lbl_ref, direction)
        tri, tri_t, bd, masks = _hgrn_consts(rb, reverse)
        order = range(nc) if reverse else range(nc - 1, -1, -1)

        pairs = [slice(p * 128, (p + 1) * 128) for p in range(HEAD_PAIRS)]
        chunks = [slice(c * CHUNK, (c + 1) * CHUNK) for c in range(nc)]

        def lanes(per_pair):
            return jnp.concatenate(per_pair, axis=1)

        w = _hgrn_block(hf_ref[...], hq_ref[...], lb_all, reverse)
        dov = do_ref[...]
        v16 = hi_ref[...].astype(BF16)
        do16 = dov.astype(BF16)
        qd16 = w["q_dec"].astype(BF16)
        ki16 = w["k_inv"].astype(BF16)
        ke16 = w["k_end"].astype(BF16)
        qm16 = [[jnp.where(mh > 0.0, qd16[:, ls], 0.0).astype(BF16) for mh in masks] for ls in pairs]
        dom16 = [[jnp.where(mh > 0.0, do16[:, ls], 0.0).astype(BF16) for mh in masks] for ls in pairs]
        heads = [(p, e) for p in range(HEAD_PAIRS) for e in range(2)]
        dp = {(p, e): _dot_nt(dom16[p][e], v16[:, pairs[p]]) for p, e in heads}
        pm_t = {(p, e): _dot_nt(ki16[:, pairs[p]], qm16[p][e]) for p, e in heads}
        dp_t = {(p, e): _dot_nt(v16[:, pairs[p]], dom16[p][e]) for p, e in heads}
        dp = {h: jnp.where(tri, a, 0.0).astype(BF16) for h, a in dp.items()}
        pm_t = {h: jnp.where(tri_t, a, 0.0).astype(BF16) for h, a in pm_t.items()}
        dp_t = {h: jnp.where(tri_t, a, 0.0).astype(BF16) for h, a in dp_t.items()}
        dv_e = {(p, e): _dot(pm_t[p, e], do16[:, pairs[p]]) for p, e in heads}
        dq_e = {(p, e): _dot(dp[p, e], ki16[:, pairs[p]]) for p, e in heads}
        dk_e = {(p, e): _dot(dp_t[p, e], qd16[:, pairs[p]]) for p, e in heads}
        st = [[st_ref[c, p] for p in range(HEAD_PAIRS)] for c in range(nc)]
        dq_x = [[_dot(do16[rs, ls], st[c][p].astype(BF16)) for p, ls in enumerate(pairs)] for c, rs in enumerate(chunks)]
        gq = [[jnp.where(bd, _dot_tn(do16[rs, ls], qd16[rs, ls]), 0.0) for ls in pairs] for rs in chunks]
        g = [g_scr[p] for p in range(HEAD_PAIRS)]
        dk_end, dv_x, dd = [None] * nc, [None] * nc, [None] * nc
        for c in order:
            rs = chunks[c]
            g16 = [a.astype(BF16) for a in g]
            dk_end[c] = lanes([_dot(v16[rs, ls], g16[p]) for p, ls in enumerate(pairs)])
            dv_x[c] = lanes([_dot_nt(ke16[rs, ls], g16[p]) for p, ls in enumerate(pairs)])
            dd[c] = jnp.broadcast_to(lanes([jnp.sum(g[p] * st[c][p], axis=0, keepdims=True) for p in range(HEAD_PAIRS)]),
                                     (CHUNK, A_WIDTH))
            for p, ls in enumerate(pairs):
                g[p] = g[p] * w["d"][c * CHUNK:c * CHUNK + 1, ls] + gq[c][p]
        for p in range(HEAD_PAIRS):
            g_scr[p] = g[p]

        def both_heads(d):
            return lanes([jnp.where(masks[0] > 0.0, d[p, 0], d[p, 1]) for p in range(HEAD_PAIRS)])

        dq_dec = both_heads(dq_e) + jnp.concatenate([lanes(a) for a in dq_x], axis=0)
        dk_inv = both_heads(dk_e)
        dk_end = jnp.concatenate(dk_end, axis=0)
        dv = both_heads(dv_e) + jnp.concatenate(dv_x, axis=0)
        dd = jnp.concatenate(dd, axis=0)
        dke = dk_end * w["k_end"]
        dcum = dq_dec * w["q_dec"] - dk_inv * w["k_inv"] - dke
        dk = dk_inv * w["e_neg"] + dk_end * w["e_end"]
        dlast = _per_chunk(dke, lambda a: jnp.sum(a, axis=0, keepdims=True)) + dd * w["d"]
        dlf = _chunk_cumsum(dcum, not reverse) + dlast
        tt = dlf / w["f"] - dk
        dq_ref[...] = dq_dec * w["a"]
        dv_ref[...] = dv
        dz_ref[...] = ((1.0 - lb_all) * w["sig"] * w["sn"] * tt).astype(BF16)
        dlb_scr[...] += jnp.sum(w["sn"] * tt, axis=0, keepdims=True)

        @pl.when(jnp.logical_and(b == pl.num_programs(0) - 1, j == pl.num_programs(1) - 1))
        def _():
            d0 = dlb_scr[...] * lb_all * (1.0 - lb_all)
            dl_ref[0:1, :] = d0
            dl_ref[1:2, :] = -d0

    blk = pl.BlockSpec((rb, A_WIDTH), tmap)
    return pl.pallas_call(
        body, name=f"hgrn_bwd_{direction}", grid=(n_batch, nb),
        in_specs=[blk, blk, blk, blk, pl.BlockSpec((nc, HEAD_PAIRS, 128, 128), smap), _full((2, 2, A_WIDTH))],
        out_specs=[blk, blk, blk, _full((2, A_WIDTH))],
        out_shape=[jax.ShapeDtypeStruct((t, A_WIDTH), F32)] * 2
        + [jax.ShapeDtypeStruct((t, A_WIDTH), BF16), jax.ShapeDtypeStruct((2, A_WIDTH), F32)],
        scratch_shapes=[pltpu.VMEM((HEAD_PAIRS, 128, 128), F32), pltpu.VMEM((1, A_WIDTH), F32)],
        compiler_params=_params(2),
    )(hq, hi, hf, do, st, lbl)


def _swap_rope_halves(x):
    lane = _iota2(x.shape, 1)
    return jnp.where(lane < 32, pltpu.roll(x, 96, 1), pltpu.roll(x, 32, 1))


def _rms_fwd(xv, g):
    r = lax.rsqrt(jnp.mean(xv * xv, axis=-1, keepdims=True) + EPS)
    return (xv * r) * g


def _rms_bwd(dy, xv, g):
    r = lax.rsqrt(jnp.mean(xv * xv, axis=-1, keepdims=True) + EPS)
    xh = xv * r
    dxh = dy * g
    return r * (dxh - xh * jnp.mean(dxh * xh, axis=-1, keepdims=True)), dy * xh


def _mla_prep(cq, ckv, kr, g_qa, g_kva, w_q_p, w_kv_p, cosx, sinx, *, n_batch):
    t = cq.shape[0]
    s = t // n_batch
    tm = _row_tile(s)
    nt = s // tm

    def body(cq_ref, ckv_ref, kr_ref, gq_ref, gkv_ref, wq_ref, wkv_ref, cos_ref, sin_ref,
             q_ref, k_ref, v_ref, cqn_ref, ckvn_ref):
        cos, sin = cos_ref[...], sin_ref[...]
        cqn = _rms_fwd(cq_ref[...], gq_ref[...]).astype(BF16)
        ckvn = _rms_fwd(ckv_ref[...], gkv_ref[...]).astype(BF16)
        cqn_ref[...] = cqn
        ckvn_ref[...] = ckvn
        krv = kr_ref[...]
        kr_roped = (krv * cos + _swap_rope_halves(krv) * sin).astype(BF16)
        for h in range(B_HEADS):
            o = h * QK_PAD
            q_ref[:, o:o + 128] = _dot_nt(cqn, wq_ref[o:o + 128, :]).astype(BF16)
            qr = _dot_nt(cqn, wq_ref[o + 128:o + 256, :])
            q_ref[:, o + 128:o + 256] = (qr * cos + _swap_rope_halves(qr) * sin).astype(BF16)
            k_ref[:, o:o + 128] = _dot(ckvn, wkv_ref[:, h * 128:(h + 1) * 128]).astype(BF16)
            k_ref[:, o + 128:o + 256] = kr_roped
        v_ref[...] = _dot(ckvn, wkv_ref[:, 512:1024]).astype(BF16)

    tab = pl.BlockSpec((tm, 128), lambda i: (i % nt, 0))
    return pl.pallas_call(
        body, name="mla_prep", grid=(t // tm,),
        in_specs=[_rows(tm, Q_LORA), _rows(tm, KV_LORA), _rows(tm, 128), _full((1, Q_LORA)), _full((1, KV_LORA)),
                  _full((1024, Q_LORA)), _full((KV_LORA, 1024)), tab, tab],
        out_specs=[_rows(tm, 1024), _rows(tm, 1024), _rows(tm, 512), _rows(tm, Q_LORA), _rows(tm, KV_LORA)],
        out_shape=[jax.ShapeDtypeStruct((t, 1024), BF16), jax.ShapeDtypeStruct((t, 1024), BF16),
                   jax.ShapeDtypeStruct((t, 512), BF16), jax.ShapeDtypeStruct((t, Q_LORA), BF16),
                   jax.ShapeDtypeStruct((t, KV_LORA), BF16)],
        compiler_params=_params(1),
    )(cq, ckv, kr, g_qa, g_kva, w_q_p, w_kv_p, cosx, sinx)


def _mla_prep_bwd(dqt, dk, dv, cq, ckv, g_qa, g_kva, w_q_p, w_kv_p, cosx, sinx, *, n_batch):
    t = cq.shape[0]
    s = t // n_batch
    tm = _row_tile(s)
    nt = s // tm

    def body(dqt_ref, dk_ref, dv_ref, cq_ref, ckv_ref, gq_ref, gkv_ref, wq_ref, wkv_ref, cos_ref, sin_ref,
             dcq_ref, dckv_ref, dkr_ref, dqp_ref, dkvp_ref, dgq_ref, dgkv_ref):
        cos, sin = cos_ref[...], sin_ref[...]
        first = pl.program_id(0) == 0

        def unrope(d):
            return d * cos + _swap_rope_halves(d * sin)

        dcqn = None
        dkr = None
        dckvn = None
        for h in range(B_HEADS):
            o = h * QK_PAD
            dq_h = jnp.transpose(dqt_ref[o:o + QK_PAD, :])
            dqn16 = dq_h[:, 0:128].astype(BF16)
            dqr16 = unrope(dq_h[:, 128:256]).astype(BF16)
            dqp_ref[:, o:o + 128] = dqn16
            dqp_ref[:, o + 128:o + 256] = dqr16
            part = _dot(dqn16, wq_ref[o:o + 128, :]) + _dot(dqr16, wq_ref[o + 128:o + 256, :])
            dcqn = part if dcqn is None else dcqn + part
            dkn16 = dk_ref[:, o:o + 128].astype(BF16)
            dkvp_ref[:, h * 128:(h + 1) * 128] = dkn16
            part = _dot_nt(dkn16, wkv_ref[:, h * 128:(h + 1) * 128])
            dckvn = part if dckvn is None else dckvn + part
            kr_part = dk_ref[:, o + 128:o + 256]
            dkr = kr_part if dkr is None else dkr + kr_part
        dv16 = dv_ref[...].astype(BF16)
        dkvp_ref[:, 512:1024] = dv16
        dckvn = dckvn + _dot_nt(dv16, wkv_ref[:, 512:1024])
        dkr_ref[...] = unrope(dkr).astype(BF16)
        dcq, dgq = _rms_bwd(dcqn, cq_ref[...], gq_ref[...])
        dckv, dgkv = _rms_bwd(dckvn, ckv_ref[...], gkv_ref[...])
        dcq_ref[...] = dcq.astype(BF16)
        dckv_ref[...] = dckv.astype(BF16)
        _acc_rows(dgq_ref, dgq, first)
        _acc_rows(dgkv_ref, dgkv, first)

    tab = pl.BlockSpec((tm, 128), lambda i: (i % nt, 0))
    return pl.pallas_call(
        body, name="mla_prep_bwd", grid=(t // tm,),
        in_specs=[pl.BlockSpec((1024, tm), lambda i: (0, i)), _rows(tm, 1024), _rows(tm, 512), _rows(tm, Q_LORA),
                  _rows(tm, KV_LORA),
                  _full((1, Q_LORA)), _full((1, KV_LORA)), _full((1024, Q_LORA)), _full((KV_LORA, 1024)), tab, tab],
        out_specs=[_rows(tm, Q_LORA), _rows(tm, KV_LORA), _rows(tm, 128), _rows(tm, 1024), _rows(tm, 1024),
                   _full((1, Q_LORA)), _full((1, KV_LORA))],
        out_shape=[jax.ShapeDtypeStruct((t, Q_LORA), BF16), jax.ShapeDtypeStruct((t, KV_LORA), BF16),
                   jax.ShapeDtypeStruct((t, 128), BF16), jax.ShapeDtypeStruct((t, 1024), BF16),
                   jax.ShapeDtypeStruct((t, 1024), BF16), jax.ShapeDtypeStruct((1, Q_LORA), F32),
                   jax.ShapeDtypeStruct((1, KV_LORA), F32)],
        compiler_params=_params(1),
    )(dqt, dk, dv, cq, ckv, g_qa, g_kva, w_q_p, w_kv_p, cosx, sinx)


def _attn_dims(t, n_batch):
    s = t // n_batch
    tq = _pick(s, 1024, 128)
    return s, tq, s // tq


ATTN_ROWS = 256


def _grid_ends(n_axes):
    ids = [pl.program_id(a) for a in range(n_axes)]
    first = functools.reduce(jnp.logical_and, [i == 0 for i in ids])
    last = functools.reduce(jnp.logical_and, [i == pl.num_programs(a) - 1 for a, i in enumerate(ids)])
    return first, last


def _attn_fwd(q, k, v, wsrcs, *, n_batch):
    t = q.shape[0]
    s, tq, nq = _attn_dims(t, n_batch)
    nw = len(wsrcs)

    def body(q_ref, k_ref, v_ref, *refs):
        w_refs, (o_ref, lse_ref), wall_refs = refs[:nw], refs[nw:nw + 2], refs[nw + 2:2 * nw + 2]
        send_sems, recv_sems, local_sems = refs[2 * nw + 2:]
        first, last = _grid_ends(3)

        @pl.when(first)
        def _():
            for i in range(nw):
                _gather_start(w_refs[i], wall_refs[i], send_sems, recv_sems, local_sems.at[i], 3 * i)

        @pl.when(last)
        def _():
            for i in range(nw):
                _gather_wait(w_refs[i], wall_refs[i], send_sems, recv_sems, local_sems.at[i], 3 * i)

        kv, vv = k_ref[...], v_ref[...]
        groups = [slice(r, r + ATTN_ROWS) for r in range(0, tq, ATTN_ROWS)]
        raw = [_dot_nt(q_ref[g, :], kv) for g in groups]
        m = [jnp.max(a, axis=-1, keepdims=True) for a in raw]
        p = [jnp.exp2((a - b) * SCALE_LOG2E) for a, b in zip(raw, m)]
        l = [jnp.sum(a, axis=-1, keepdims=True) for a in p]
        for g, pg, mg, lg in zip(groups, p, m, l):
            o_ref[g, :] = _dot(pg.astype(BF16), vv) / lg
            lse2 = mg * SCALE_LOG2E + jnp.log(lg) * LOG2E
            lse_ref[:, g] = jnp.transpose(jnp.broadcast_to(lse2, (ATTN_ROWS, 128)))[0:1, :]

    return pl.pallas_call(
        body, name="attn_fwd", grid=(n_batch, B_HEADS, nq),
        in_specs=[pl.BlockSpec((tq, QK_PAD), lambda b, h, i: (b * nq + i, h)),
                  pl.BlockSpec((s, QK_PAD), lambda b, h, i: (b, h)),
                  pl.BlockSpec((s, B_V), lambda b, h, i: (b, h))] + [_HBM] * nw,
        out_specs=[pl.BlockSpec((tq, B_V), lambda b, h, i: (b * nq + i, h)),
                   pl.BlockSpec((None, 1, tq), lambda b, h, i: (h, 0, b * nq + i))] + [_HBM] * nw,
        out_shape=[jax.ShapeDtypeStruct((t, B_HEADS * B_V), F32), jax.ShapeDtypeStruct((B_HEADS, 1, t), F32)]
        + [jax.ShapeDtypeStruct((N_CHIPS,) + w.shape, w.dtype) for w in wsrcs],
        scratch_shapes=[pltpu.SemaphoreType.DMA((3 * nw,)), pltpu.SemaphoreType.DMA((3 * nw,)),
                        pltpu.SemaphoreType.DMA((nw,))],
        compiler_params=_params(3),
    )(q, k, v, *wsrcs)


def _attn_bwd(q, k, v, do16, lse, delta, gsegs, *, n_batch):
    t = q.shape[0]
    s = t // n_batch
    tk = _pick(s, 512, 128)
    nk = s // tk
    ng = len(gsegs)

    def body(q_ref, k_ref, v_ref, do_ref, lse_ref, dl_ref, *refs):
        g_refs, (dqt_ref, dk_ref, dv_ref), recv_refs = refs[:ng], refs[ng:ng + 3], refs[ng + 3:2 * ng + 3]
        send_sems, recv_sems = refs[2 * ng + 3:]
        first, last = _grid_ends(3)

        @pl.when(first)
        def _():
            for i in range(ng):
                _scatter_start(g_refs[i], recv_refs[i], send_sems, recv_sems, 3 * i)

        @pl.when(last)
        def _():
            for i in range(ng):
                _scatter_wait(g_refs[i], recv_refs[i], send_sems, recv_sems, 3 * i)

        j = pl.program_id(2)
        qv, kv, dov = q_ref[...], k_ref[...], do_ref[...]
        pt = jnp.exp2(_dot_nt(kv, qv) * SCALE_LOG2E - lse_ref[...])
        dv_ref[...] = _dot(pt.astype(BF16), dov)
        dpt = _dot_nt(v_ref[...], dov)
        dst = (pt * (dpt - dl_ref[...])).astype(BF16)
        dk_ref[...] = _dot(dst, qv) * ATTN_SCALE
        part = _dot_tn(kv, dst)

        @pl.when(j == 0)
        def _():
            dqt_ref[...] = part

        @pl.when(j != 0)
        def _():
            dqt_ref[...] += part

        @pl.when(j == nk - 1)
        def _():
            dqt_ref[...] = dqt_ref[...] * ATTN_SCALE

    row = pl.BlockSpec((None, 1, s), lambda b, h, j: (h, 0, b))
    return pl.pallas_call(
        body, name="attn_bwd", grid=(n_batch, B_HEADS, nk),
        in_specs=[pl.BlockSpec((s, QK_PAD), lambda b, h, j: (b, h)),
                  pl.BlockSpec((tk, QK_PAD), lambda b, h, j: (b * nk + j, h)),
                  pl.BlockSpec((tk, B_V), lambda b, h, j: (b * nk + j, h)),
                  pl.BlockSpec((s, B_V), lambda b, h, j: (b, h)), row, row] + [_HBM] * ng,
        out_specs=[pl.BlockSpec((QK_PAD, s), lambda b, h, j: (h, b)),
                   pl.BlockSpec((tk, QK_PAD), lambda b, h, j: (b * nk + j, h)),
                   pl.BlockSpec((tk, B_V), lambda b, h, j: (b * nk + j, h))] + [_HBM] * ng,
        out_shape=[jax.ShapeDtypeStruct((B_HEADS * QK_PAD, t), F32), jax.ShapeDtypeStruct((t, B_HEADS * QK_PAD), F32),
                   jax.ShapeDtypeStruct((t, B_HEADS * B_V), F32)]
        + [jax.ShapeDtypeStruct((3,) + g.shape[1:], g.dtype) for g in gsegs],
        scratch_shapes=[pltpu.SemaphoreType.DMA((3 * ng,)), pltpu.SemaphoreType.DMA((3 * ng,))],
        compiler_params=_params(3),
    )(q, k, v, do16, lse, delta, *gsegs)


def _group_ones16():
    r = _iota2((A_WIDTH, A_WIDTH), 0) // 64
    c = _iota2((A_WIDTH, A_WIDTH), 1) // 64
    return (r == c).astype(BF16)


def _head_rms(o, ones16):
    return lax.rsqrt(_exact_right(o * o, ones16) * (1.0 / 64.0) + EPS)


def _out_fwd(o_f, o_r, hg, o_attn, x, g_hn, g_mla, w_out, g2):
    t = x.shape[0]
    tm = _row_tile(t, 2 * ROW_GROUP)

    def body(of_ref, or_ref, hg_ref, oa_ref, x_ref, ghn_ref, gm_ref, w_ref, g2_ref, y_ref, x2_ref, h2_ref):
        ones16 = _group_ones16()
        ghn, gm, g2v = ghn_ref[...], gm_ref[...], g2_ref[...]

        def group(ofv, orv, hgv, oav, xv):
            o = ofv + orv
            ya16 = (((o * _head_rms(o, ones16)) * ghn) * (hgv * _sigmoid(hgv))).astype(BF16)
            yb16 = _rms_fwd(oav, gm).astype(BF16)
            x2 = xv + _dot(ya16, w_ref[0:A_WIDTH, :]) + _dot(yb16, w_ref[A_WIDTH:D_MODEL, :])
            return ya16, yb16, x2, _rms_fwd(x2, g2v).astype(BF16)

        ins = (of_ref, or_ref, hg_ref, oa_ref, x_ref)
        groups = _row_groups(tm)
        done = [group(*vals) for vals in [[ref[g, :] for ref in ins] for g in groups]]
        for g, (ya16, yb16, x2, h2) in zip(groups, done):
            y_ref[g, 0:A_WIDTH] = ya16
            y_ref[g, A_WIDTH:D_MODEL] = yb16
            x2_ref[g, :] = x2
            h2_ref[g, :] = h2

    a512 = _rows(tm, A_WIDTH)
    return pl.pallas_call(
        body, name="out_fwd", grid=(t // tm,),
        in_specs=[a512, a512, a512, a512, _rows(tm, D_MODEL), _full((1, A_WIDTH)), _full((1, A_WIDTH)),
                  _full((D_MODEL, D_MODEL)), _full((1, D_MODEL))],
        out_specs=[_rows(tm, D_MODEL)] * 3,
        out_shape=[jax.ShapeDtypeStruct((t, D_MODEL), BF16), jax.ShapeDtypeStruct((t, D_MODEL), F32),
                   jax.ShapeDtypeStruct((t, D_MODEL), BF16)],
        compiler_params=_params(1),
    )(o_f, o_r, hg, o_attn, x, g_hn, g_mla, w_out, g2)


def _out_bwd(dx2_16, o_f, o_r, hg, o_attn, g_hn, g_mla, w_out):
    t = dx2_16.shape[0]
    tm = _row_tile(t, 2 * ROW_GROUP)

    def body(dx_ref, of_ref, or_ref, hg_ref, oa_ref, ghn_ref, gm_ref, w_ref,
             do_ref, dhg_ref, doa_ref, dl_ref, dghn_ref, dgm_ref):
        first = pl.program_id(0) == 0
        ones16 = _group_ones16()
        sel16 = (_iota2((8, A_WIDTH), 0) == _iota2((8, A_WIDTH), 1) // B_V).astype(BF16)
        ghn, gm = ghn_ref[...], gm_ref[...]

        def group(dxv, ofv, orv, hgv, oav):
            dya = _dot_nt(dxv, w_ref[0:A_WIDTH, :])
            dyb = _dot_nt(dxv, w_ref[A_WIDTH:D_MODEL, :])
            o = ofv + orv
            rh = _head_rms(o, ones16)
            oh = o * rh
            sg = _sigmoid(hgv)
            sl = hgv * sg
            dhg = ((dya * (oh * ghn)) * (sg * (1.0 + hgv * (1.0 - sg)))).astype(BF16)
            doh = dya * sl * ghn
            do = rh * (doh - oh * (_exact_right(doh * oh, ones16) * (1.0 / 64.0)))
            doa, dgm = _rms_bwd(dyb, oav, gm)
            x1, x2, x3 = _split3(doa * oav)
            delta = _dot_nt(sel16, x1) + _dot_nt(sel16, x2) + _dot_nt(sel16, x3)
            return do, dhg, doa.astype(BF16), delta, dya * sl * oh, dgm

        groups = _row_groups(tm)
        loaded = [(dx_ref[g, :], of_ref[g, :], or_ref[g, :], hg_ref[g, :], oa_ref[g, :]) for g in groups]
        done = [group(*vals) for vals in loaded]
        for g, (do, dhg, doa16, delta, _, _) in zip(groups, done):
            do_ref[g, :] = do
            dhg_ref[g, :] = dhg
            doa_ref[g, :] = doa16
            for h in range(B_HEADS):
                dl_ref[h, :, g] = delta[h:h + 1, :]
        _acc_rows(dghn_ref, jnp.concatenate([d[4] for d in done], axis=0), first)
        _acc_rows(dgm_ref, jnp.concatenate([d[5] for d in done], axis=0), first)

    a512 = _rows(tm, A_WIDTH)
    return pl.pallas_call(
        body, name="out_bwd", grid=(t // tm,),
        in_specs=[_rows(tm, D_MODEL), a512, a512, a512, a512, _full((1, A_WIDTH)), _full((1, A_WIDTH)),
                  _full((D_MODEL, D_MODEL))],
        out_specs=[a512, a512, a512, pl.BlockSpec((B_HEADS, 1, tm), lambda i: (0, 0, i)),
                   _full((1, A_WIDTH)), _full((1, A_WIDTH))],
        out_shape=[jax.ShapeDtypeStruct((t, A_WIDTH), F32)] + [jax.ShapeDtypeStruct((t, A_WIDTH), BF16)] * 2
        + [jax.ShapeDtypeStruct((B_HEADS, 1, t), F32)]
        + [jax.ShapeDtypeStruct((1, A_WIDTH), F32)] * 2,
        compiler_params=_params(1),
    )(dx2_16, o_f, o_r, hg, o_attn, g_hn, g_mla, w_out)


def _ffn_fwd_bwd(h2, x2, target, w_gate, w_up, w_down, g_f, g2):
    t = x2.shape[0]
    tm = _row_tile(t)
    inv_d = 1.0 / D_MODEL

    def body(h2_ref, x2_ref, tg_ref, wg_ref, wu_ref, wd_ref, gf_ref, g2_ref,
             act_ref, dgate_ref, dup_ref, dx3_ref, dx2_ref, dx2h_ref, loss_ref, dgf_ref, dg2_ref):
        first = pl.program_id(0) == 0
        h2v = h2_ref[...]
        gate = _dot_nt(h2v, wg_ref[...])
        up = _dot_nt(h2v, wu_ref[...])
        sg = _sigmoid(gate)
        sl = gate * sg
        act16 = (sl * up).astype(BF16)
        act_ref[...] = act16
        x2v = x2_ref[...]
        x3 = x2v + _dot(act16, wd_ref[...])
        r3 = lax.rsqrt(jnp.mean(x3 * x3, axis=-1, keepdims=True) + EPS)
        x3h = x3 * r3
        gf = gf_ref[...]
        err = x3h * gf - tg_ref[...]
        part = 0.5 * jnp.sum(jnp.mean(err * err, axis=-1, keepdims=True), axis=0, keepdims=True)

        @pl.when(first)
        def _():
            loss_ref[...] = jnp.zeros_like(loss_ref)

        loss_ref[...] += part
        dy = err * inv_d
        _acc_rows(dgf_ref, dy * x3h, first)
        dx3h = dy * gf
        dx3 = r3 * (dx3h - x3h * jnp.mean(dx3h * x3h, axis=-1, keepdims=True))
        dx3_16 = dx3.astype(BF16)
        dx3_ref[...] = dx3_16
        da = _dot_nt(dx3_16, wd_ref[...])
        dup16 = (da * sl).astype(BF16)
        dgate16 = (da * up * (sg * (1.0 + gate * (1.0 - sg)))).astype(BF16)
        dup_ref[...] = dup16
        dgate_ref[...] = dgate16
        dh2 = _dot(dgate16, wg_ref[...]) + _dot(dup16, wu_ref[...])
        dx2n, dg2 = _rms_bwd(dh2, x2v, g2_ref[...])
        _acc_rows(dg2_ref, dg2, first)
        dx2 = dx3 + dx2n
        dx2_ref[...] = dx2
        dx2h_ref[...] = dx2.astype(BF16)

    return pl.pallas_call(
        body, name="ffn_fwd_bwd", grid=(t // tm,),
        in_specs=[_rows(tm, D_MODEL), _rows(tm, D_MODEL), _rows(tm, D_MODEL), _full((D_FF, D_MODEL), True),
                  _full((D_FF, D_MODEL), True), _full((D_FF, D_MODEL), True), _full((1, D_MODEL)), _full((1, D_MODEL))],
        out_specs=[_rows(tm, D_FF), _rows(tm, D_FF), _rows(tm, D_FF), _rows(tm, D_MODEL), _rows(tm, D_MODEL),
                   _rows(tm, D_MODEL), _full((8, 128)), _full((1, D_MODEL)), _full((1, D_MODEL))],
        out_shape=[jax.ShapeDtypeStruct((t, D_FF), BF16)] * 3
        + [jax.ShapeDtypeStruct((t, D_MODEL), BF16), jax.ShapeDtypeStruct((t, D_MODEL), F32),
           jax.ShapeDtypeStruct((t, D_MODEL), BF16), jax.ShapeDtypeStruct((8, 128), F32),
           jax.ShapeDtypeStruct((1, D_MODEL), F32), jax.ShapeDtypeStruct((1, D_MODEL), F32)],
        compiler_params=_params(1),
    )(h2, x2, target, w_gate, w_up, w_down, g_f, g2)


def _wgrad(a, b, name):
    t, m = a.shape
    n = b.shape[1]
    bm = _pick(m, 1664)
    bn = _pick(n, 1664)
    tk = _pick(t, 2048, 16)
    nk = t // tk

    def body(a_ref, b_ref, o_ref, o16_ref):
        k = pl.program_id(2)
        part = _dot_tn(a_ref[...], b_ref[...])

        @pl.when(k == 0)
        def _():
            o_ref[...] = part

        @pl.when(k != 0)
        def _():
            o_ref[...] += part

        @pl.when(k == nk - 1)
        def _():
            o16_ref[...] = o_ref[...].astype(BF16)

    out = pl.BlockSpec((bm, bn), lambda i, j, k: (i, j))
    return pl.pallas_call(
        body, name=name, grid=(m // bm, n // bn, nk),
        in_specs=[pl.BlockSpec((tk, bm), lambda i, j, k: (k, i)), pl.BlockSpec((tk, bn), lambda i, j, k: (k, j))],
        out_specs=[out, out],
        out_shape=[jax.ShapeDtypeStruct((m, n), F32), jax.ShapeDtypeStruct((m, n), BF16)],
        compiler_params=_params(3),
    )(a, b)


def _rope_tables(seq):
    inv = 1.0 / (ROPE_THETA ** (jnp.arange(0, B_ROPE, 2, dtype=F32) / B_ROPE))
    ang = jnp.arange(seq, dtype=F32)[:, None] * inv[None, :]
    cos, sin = jnp.cos(ang), jnp.sin(ang)
    zeros = jnp.zeros((seq, 64), F32)
    return jnp.concatenate([cos, cos, zeros], axis=1), jnp.concatenate([-sin, sin, zeros], axis=1)


def _pad_weights(w_in_t, w_q_t, w_kv_b):
    w_in_p = jnp.pad(w_in_t, ((0, D_IN_PAD - D_IN), (0, 0)))
    w_q_p = jnp.pad(w_q_t.reshape(B_HEADS, B_NOPE + B_ROPE, Q_LORA), ((0, 0), (0, 64), (0, 0))).reshape(1024, Q_LORA)
    kv = w_kv_b.reshape(KV_LORA, B_HEADS, B_NOPE + B_V)
    w_kv_p = jnp.concatenate([kv[:, :, :B_NOPE].reshape(KV_LORA, 512), kv[:, :, B_NOPE:].reshape(KV_LORA, 512)], axis=1)
    return w_in_p, w_q_p, w_kv_p


def _unpad_kv(g_kv_p):
    return jnp.concatenate([g_kv_p[:, :512].reshape(KV_LORA, B_HEADS, B_NOPE),
                            g_kv_p[:, 512:].reshape(KV_LORA, B_HEADS, B_V)], axis=2).reshape(KV_LORA, 1024)


_OWN_ROWS = dict(w_in=(272, 3), w_q_b=(64, 4), w_kv_b=(256, 0), w_out=(256, 1), w_gate=(176, 4), w_up=(176, 4), w_down=(176, 4))


def _local_step(x, target, lbl, g1, g_hn, g_qa, g_kva, g_mla, g2, g_f, w_in, w_q_b, w_kv_b, late_shards):
    n_batch, seq, _ = x.shape
    t = n_batch * seq
    x = x.reshape(t, D_MODEL)
    target = target.reshape(t, D_MODEL)
    w_in_p, w_q_p, w_kv_p = _pad_weights(w_in, w_q_b, w_kv_b)
    cosx, sinx = _rope_tables(seq)

    h1, hq, hi, hff, hfb, hg, cq, ckv, kr = _in_fwd(x, g1, w_in_p)
    o_f, st_f = _hgrn_fwd(hq, hi, hff, lbl, n_batch=n_batch, direction=0)
    o_r, st_r = _hgrn_fwd(hq, hi, hfb, lbl, n_batch=n_batch, direction=1)
    q, k, v, cqn, ckvn = _mla_prep(cq, ckv, kr, g_qa, g_kva, w_q_p, w_kv_p, cosx, sinx, n_batch=n_batch)
    o_attn, lse, *gathered = _attn_fwd(q, k, v, late_shards, n_batch=n_batch)
    w_out, w_gate, w_up, w_down = (_join_shards(n, a) for n, a in zip(_LATE, gathered))
    ycat, x2, h2 = _out_fwd(o_f, o_r, hg, o_attn, x, g_hn, g_mla, w_out, g2)
    act, dgate, dup, dx3_16, dx2, dx2_16, loss, dg_f, dg2 = _ffn_fwd_bwd(h2, x2, target, w_gate, w_up, w_down, g_f, g2)
    full = dict(w_out=_wgrad(ycat, dx2_16, "wgrad_out"), w_gate=_wgrad(dgate, h2, "wgrad_gate"),
                w_up=_wgrad(dup, h2, "wgrad_up"), w_down=_wgrad(act, dx3_16, "wgrad_down"))
    do_h, dhg, do_attn16, delta, dg_hn, dg_mla = _out_bwd(dx2_16, o_f, o_r, hg, o_attn, g_hn, g_mla, w_out)
    dqt, dk, dv, *recv_late = _attn_bwd(q, k, v, do_attn16, lse, delta,
                                        [_segments(n, full[n][1]) for n in _LATE], n_batch=n_batch)
    dcq, dckv, dkr, dqp16, dkvp16, dg_qa, dg_kva = _mla_prep_bwd(dqt, dk, dv, cq, ckv, g_qa, g_kva, w_q_p, w_kv_p,
                                                                  cosx, sinx, n_batch=n_batch)
    dq_f, dv_f, dz_f, dl_f = _hgrn_bwd(hq, hi, hff, do_h, st_f, lbl, n_batch=n_batch, direction=0)
    dq_r, dv_r, dz_r, dl_r = _hgrn_bwd(hq, hi, hfb, do_h, st_r, lbl, n_batch=n_batch, direction=1)
    dx, dproj16, dg1 = _in_bwd(dq_f, dq_r, dv_f, dv_r, dz_f, dz_r, dhg, hq, dcq, dckv, dkr, dx2, x, g1, w_in_p)

    full.update(w_in=_wgrad(dproj16, h1, "wgrad_in"), w_q_b=_wgrad(dqp16, cqn, "wgrad_q_b"))
    g_kv = _unpad_kv(_wgrad(ckvn, dkvp16, "wgrad_kv_b")[0])
    early16 = dict(w_in=full["w_in"][1][:D_IN].reshape(N_CHIPS, D_IN // N_CHIPS, D_MODEL),
                   w_q_b=full["w_q_b"][1].reshape(B_HEADS, QK_PAD, Q_LORA)[:, :B_NOPE + B_ROPE],
                   w_kv_b=_segments("w_kv_b", g_kv).astype(BF16))
    small = dict(norm1_g=dg1, lb_logits=jnp.stack([dl_f, dl_r]), hgrn_norm_g=dg_hn, q_a_norm_g=dg_qa, kv_a_norm_g=dg_kva,
                 mla_norm_g=dg_mla, norm2_g=dg2, final_norm_g=dg_f, loss=loss[0:1, 0])
    return dx.reshape(n_batch, seq, D_MODEL), small, {**{n: f for n, (f, _) in full.items()}, "w_kv_b": g_kv}, early16, \
        dict(zip(_LATE, recv_late))


_HBM = pl.BlockSpec(memory_space=pltpu.HBM)
_MESH = pl.DeviceIdType.MESH


def _place():
    x, y, c = lax.axis_index("x"), lax.axis_index("y"), lax.axis_index("c")
    other_chips = [(1 - x, y), (x, 1 - y), (1 - x, 1 - y)]
    return x, y, c, other_chips


def _gather_copies(w_ref, wall_ref, send_sems, recv_sems, local_sem, base=0):
    x, y, c, chips = _place()
    mine = 2 * x + y

    def mk(j, chip_index, to):
        return pltpu.make_async_remote_copy(src_ref=w_ref, dst_ref=wall_ref.at[chip_index], send_sem=send_sems.at[base + j],
                                            recv_sem=recv_sems.at[base + j], device_id=to, device_id_type=_MESH)

    local = pltpu.make_async_copy(w_ref, wall_ref.at[mine], local_sem)
    sends = [mk(j, mine, (*chip, c)) for j, chip in enumerate(chips)]
    recvs = [mk(j, 2 * px + py, (x, y, c)) for j, (px, py) in enumerate(chips)]
    return local, sends, recvs


def _gather_start(*refs):
    local, sends, _ = _gather_copies(*refs)
    local.start()
    for cp in sends:
        cp.start()


def _gather_wait(*refs):
    local, sends, recvs = _gather_copies(*refs)
    for cp in recvs:
        cp.wait_recv()
    for cp in sends:
        cp.wait_send()
    local.wait()


def _scatter_copies(g_ref, recv_ref, send_sems, recv_sems, base=0):
    x, y, c, chips = _place()

    def mk(j, src_index, to):
        return pltpu.make_async_remote_copy(src_ref=g_ref.at[src_index], dst_ref=recv_ref.at[j],
                                            send_sem=send_sems.at[base + j], recv_sem=recv_sems.at[base + j],
                                            device_id=to, device_id_type=_MESH)

    sends = [mk(j, 2 * px + py, (px, py, c)) for j, (px, py) in enumerate(chips)]
    recvs = [mk(j, 0, (x, y, c)) for j in range(3)]
    return sends, recvs


def _scatter_start(*refs):
    for cp in _scatter_copies(*refs)[0]:
        cp.start()


def _scatter_wait(*refs):
    sends, recvs = _scatter_copies(*refs)
    for cp in recvs:
        cp.wait_recv()
    for cp in sends:
        cp.wait_send()


def _half(ref, which, axis):
    n = ref.shape[axis] // 2
    idx = [slice(None)] * len(ref.shape)
    idx[axis] = pl.ds(which * n, n)
    return ref.at[tuple(idx)]


def _split_axis(a):
    return 1 if a.shape[1] % 256 == 0 else 0


def _gather_halves(src_ref, dst_ref, send_sems, recv_sems, local_sem, base, axis):
    x, y, c, chips = _place()
    mine = 2 * x + y

    def remote(src, dst, n, to):
        return pltpu.make_async_remote_copy(src_ref=src, dst_ref=dst, send_sem=send_sems.at[base + n],
                                            recv_sem=recv_sems.at[base + n], device_id=to, device_id_type=_MESH)

    def slot(chip_index, which):
        return _half(dst_ref.at[chip_index], which, axis)

    local = pltpu.make_async_copy(src_ref, dst_ref.at[mine], local_sem)
    first = [remote(_half(src_ref, c, axis), slot(mine, c), j, (*chip, c)) for j, chip in enumerate(chips)]
    landed = [remote(_half(src_ref, c, axis), slot(2 * px + py, c), j, (x, y, c)) for j, (px, py) in enumerate(chips)]
    passed = [remote(slot(2 * px + py, c), slot(2 * px + py, c), 3 + j, (x, y, 1 - c)) for j, (px, py) in enumerate(chips)]
    handed = [remote(_half(src_ref, c, axis), slot(2 * px + py, 1 - c), 3 + j, (x, y, c)) for j, (px, py) in enumerate(chips)]

    def start():
        local.start()
        for cp in first:
            cp.start()

    def pass_on():
        for arrived, onward in zip(landed, passed):
            arrived.wait_recv()
            onward.start()

    def finish():
        for cp in handed:
            cp.wait_recv()
        for cp in first + passed:
            cp.wait_send()
        local.wait()

    return start, pass_on, finish


def _gather_shards(srcs, small):
    n = len(srcs)

    def body(*refs):
        src_refs, s_ref, dst_refs, sall_ref = refs[:n], refs[n], refs[n + 1:2 * n + 1], refs[2 * n + 1]
        send_sems, recv_sems, local_sems = refs[2 * n + 2:]
        _gather_start(s_ref, sall_ref, send_sems, recv_sems, local_sems.at[n], 6 * n)
        steps = [_gather_halves(src_refs[i], dst_refs[i], send_sems, recv_sems, local_sems.at[i], 6 * i, _split_axis(srcs[i]))
                 for i in range(n)]
        for phase in range(3):
            for step in steps:
                step[phase]()
        _gather_wait(s_ref, sall_ref, send_sems, recv_sems, local_sems.at[n], 6 * n)

    return pl.pallas_call(
        body, name="gather_shards", in_specs=[_HBM] * (n + 1), out_specs=[_HBM] * (n + 1),
        out_shape=[jax.ShapeDtypeStruct((N_CHIPS,) + a.shape, a.dtype) for a in list(srcs) + [small]],
        scratch_shapes=[pltpu.SemaphoreType.DMA((6 * n + 3,)), pltpu.SemaphoreType.DMA((6 * n + 3,)),
                        pltpu.SemaphoreType.DMA((n + 1,))],
    )(*srcs, small)


def _scatter_grads(gsegs, small):
    n = len(gsegs)

    def body(*refs):
        g_refs, s_ref, recv_refs, sall_ref = refs[:n], refs[n], refs[n + 1:2 * n + 1], refs[2 * n + 1]
        send_sems, recv_sems, local_sem = refs[2 * n + 2:]
        x, y, c, _ = _place()
        me = 4 * x + 2 * y + c
        flips = [(fx, fy, fc) for fx in (0, 1) for fy in (0, 1) for fc in (0, 1)][1:]

        def peer(f):
            return tuple((1 - a) if b else a for a, b in zip((x, y, c), f))

        def sm(r, index, to):
            return pltpu.make_async_remote_copy(src_ref=s_ref, dst_ref=sall_ref.at[index], send_sem=send_sems.at[3 * n + r],
                                                recv_sem=recv_sems.at[3 * n + r], device_id=to, device_id_type=_MESH)

        local = pltpu.make_async_copy(s_ref, sall_ref.at[me], local_sem)
        local.start()
        for i in range(n):
            _scatter_start(g_refs[i], recv_refs[i], send_sems, recv_sems, 3 * i)
        sends = [sm(r, me, peer(f)) for r, f in enumerate(flips)]
        for cp in sends:
            cp.start()
        for i in range(n):
            _scatter_wait(g_refs[i], recv_refs[i], send_sems, recv_sems, 3 * i)
        for r, f in enumerate(flips):
            px, py, pc = peer(f)
            sm(r, 4 * px + 2 * py + pc, (x, y, c)).wait_recv()
        for cp in sends:
            cp.wait_send()
        local.wait()

    return pl.pallas_call(
        body, name="scatter_grads", in_specs=[_HBM] * (n + 1), out_specs=[_HBM] * (n + 1),
        out_shape=[jax.ShapeDtypeStruct((3,) + g.shape[1:], g.dtype) for g in gsegs]
        + [jax.ShapeDtypeStruct((8,) + small.shape, F32)],
        scratch_shapes=[pltpu.SemaphoreType.DMA((3 * n + 7,)), pltpu.SemaphoreType.DMA((3 * n + 7,)),
                        pltpu.SemaphoreType.DMA(())],
    )(*gsegs, small)


def _swap_sibling(parts):
    n = len(parts)

    def body(*refs):
        send_sems, recv_sems = refs[2 * n:]
        x, y, c, _ = _place()
        cps = [pltpu.make_async_remote_copy(src_ref=refs[i], dst_ref=refs[n + i], send_sem=send_sems.at[i],
                                            recv_sem=recv_sems.at[i], device_id=(x, y, 1 - c), device_id_type=_MESH)
               for i in range(n)]
        for cp in cps:
            cp.start()
        for cp in cps:
            cp.wait()

    return pl.pallas_call(
        body, name="swap_sibling", in_specs=[_HBM] * n, out_specs=[_HBM] * n,
        out_shape=[jax.ShapeDtypeStruct(p.shape, p.dtype) for p in parts],
        scratch_shapes=[pltpu.SemaphoreType.DMA((n,)), pltpu.SemaphoreType.DMA((n,))],
    )(*parts)


def _sum_segments(chip, full, recv, tm, pitch, name):
    _, rows, cols = recv.shape

    def body(chip_ref, o_ref, r_ref, out_ref):
        acc = o_ref[...]
        for j in range(3):
            acc = acc + r_ref[j].astype(F32)
        out_ref[...] = acc

    return pl.pallas_call(
        body, name=name,
        grid_spec=pltpu.PrefetchScalarGridSpec(
            num_scalar_prefetch=1, grid=(rows // tm,),
            in_specs=[pl.BlockSpec((tm, cols), lambda i, c: (c[0] * pitch + i, 0)),
                      pl.BlockSpec((3, tm, cols), lambda i, c: (0, i, 0))],
            out_specs=pl.BlockSpec((tm, cols), lambda i, c: (i, 0))),
        out_shape=jax.ShapeDtypeStruct((rows, cols), F32), compiler_params=_params(1),
    )(chip, full, recv)


def _sum_devices(sall):
    def body(s_ref, o_ref):
        acc = s_ref[0]
        for d in range(1, 8):
            acc = acc + s_ref[d]
        o_ref[...] = acc

    return pl.pallas_call(body, name="sum_devices", out_shape=jax.ShapeDtypeStruct(sall.shape[1:], F32))(sall)


def _adamw(w, m, v, ga, gb, name):
    rows, cols = w.shape
    tm = _row_tile(rows)
    two = gb is not None

    def body(*refs):
        w_ref, m_ref, v_ref, ga_ref = refs[:4]
        g_ref, d_ref, m2_ref, v2_ref = refs[-4:]
        g = ga_ref[...] + refs[4][...] if two else ga_ref[...]
        m2 = ADAM_B1 * m_ref[...] + (1.0 - ADAM_B1) * g
        v2 = ADAM_B2 * v_ref[...] + (1.0 - ADAM_B2) * (g * g)
        m_hat = m2 / (1.0 - ADAM_B1 ** ADAM_STEP)
        v_hat = v2 / (1.0 - ADAM_B2 ** ADAM_STEP)
        g_ref[...] = g
        d_ref[...] = -ADAM_LR * (m_hat / (jnp.sqrt(v_hat) + ADAM_EPS) + ADAM_WD * w_ref[...])
        m2_ref[...] = m2
        v2_ref[...] = v2

    blk = _rows(tm, cols)
    args = (w, m, v, ga) + ((gb,) if two else ())
    return pl.pallas_call(
        body, name=name, grid=(rows // tm,), in_specs=[blk] * len(args), out_specs=[blk] * 4,
        out_shape=[jax.ShapeDtypeStruct(w.shape, F32)] * 4, compiler_params=_params(1),
    )(*args)


_TRANSPOSED = ("w_in", "w_q_b", "w_gate", "w_up")
_COLUMN_SHARDED = ("w_kv_b",)
_FULL_SHAPES = dict(w_in=(D_IN, D_MODEL), w_q_b=(768, Q_LORA), w_kv_b=(KV_LORA, 1024), w_out=(D_MODEL, D_MODEL),
                    w_gate=(D_FF, D_MODEL), w_up=(D_FF, D_MODEL), w_down=(D_FF, D_MODEL))
_SMALL = (("norm1_g", 1024), ("lb_logits", 2048), ("hgrn_norm_g", 512), ("q_a_norm_g", 384), ("kv_a_norm_g", 256),
          ("mla_norm_g", 512), ("norm2_g", 1024), ("final_norm_g", 1024))
_UPDATE_ROWS = 48


def _pad_rows(a, rows):
    return jnp.pad(a, ((0, rows - a.shape[0]), (0, 0)))


_EARLY = ("w_in", "w_q_b", "w_kv_b")
_LATE = ("w_out", "w_gate", "w_up", "w_down")


def _segments(name, g):
    r, c = g.shape
    if name in _COLUMN_SHARDED:
        return g.reshape(r, N_CHIPS, c // N_CHIPS).transpose(1, 0, 2)
    return g.reshape(N_CHIPS, r // N_CHIPS, c)


def _own_segment(name, g, chip):
    r, c = g.shape
    if name in _COLUMN_SHARDED:
        return lax.dynamic_slice_in_dim(g, chip * (c // N_CHIPS), c // N_CHIPS, axis=1)
    return lax.dynamic_slice_in_dim(g, chip * (r // N_CHIPS), r // N_CHIPS, axis=0)


def _join_shards(name, seg):
    r, c = _FULL_SHAPES[name]
    if name in _COLUMN_SHARDED:
        return seg.transpose(1, 0, 2).reshape(r, c)
    return seg.reshape(r, c)


def kernel(x, norm1_g, w_in, lb_logits, hgrn_norm_g, q_a_norm_g, w_q_b, kv_a_norm_g, w_kv_b, mla_norm_g, w_out, norm2_g, w_gate, w_up, w_down, final_norm_g, loss_target, m_norm1_g, m_w_in, m_lb_logits, m_hgrn_norm_g, m_q_a_norm_g, m_w_q_b, m_kv_a_norm_g, m_w_kv_b, m_mla_norm_g, m_w_out, m_norm2_g, m_w_gate, m_w_up, m_w_down, m_final_norm_g, v_norm1_g, v_w_in, v_lb_logits, v_hgrn_norm_g, v_q_a_norm_g, v_w_q_b, v_kv_a_norm_g, v_w_kv_b, v_mla_norm_g, v_w_out, v_norm2_g, v_w_gate, v_w_up, v_w_down, v_final_norm_g):
    names = ("norm1_g", "w_in", "lb_logits", "hgrn_norm_g", "q_a_norm_g", "w_q_b", "kv_a_norm_g", "w_kv_b", "mla_norm_g",
             "w_out", "norm2_g", "w_gate", "w_up", "w_down", "final_norm_g")
    w = dict(zip(names, (norm1_g, w_in, lb_logits, hgrn_norm_g, q_a_norm_g, w_q_b, kv_a_norm_g, w_kv_b, mla_norm_g,
                         w_out, norm2_g, w_gate, w_up, w_down, final_norm_g)))
    m = dict(zip(names, (m_norm1_g, m_w_in, m_lb_logits, m_hgrn_norm_g, m_q_a_norm_g, m_w_q_b, m_kv_a_norm_g, m_w_kv_b,
                         m_mla_norm_g, m_w_out, m_norm2_g, m_w_gate, m_w_up, m_w_down, m_final_norm_g)))
    v = dict(zip(names, (v_norm1_g, v_w_in, v_lb_logits, v_hgrn_norm_g, v_q_a_norm_g, v_w_q_b, v_kv_a_norm_g, v_w_kv_b,
                         v_mla_norm_g, v_w_out, v_norm2_g, v_w_gate, v_w_up, v_w_down, v_final_norm_g)))
    matrices = _EARLY + _LATE
    chip = 2 * lax.axis_index("x") + lax.axis_index("y")

    def shard2d(a, n):
        return jnp.swapaxes(a[0], 0, 1) if n in _TRANSPOSED else a[0]

    def unshard2d(a, n):
        return (jnp.swapaxes(a, 0, 1) if n in _TRANSPOSED else a)[None]

    w16 = {n: shard2d(w[n], n).astype(BF16) for n in matrices}
    lb8 = _pad_rows(lb_logits.reshape(4, 128), 8)
    *early, lball = _gather_shards([w16[n] for n in _EARLY], lb8)
    lbl = lball[:, :4].reshape(N_CHIPS, 2, 2, 128).transpose(1, 2, 0, 3).reshape(2, 2, A_WIDTH)

    grad_x, g, full, early16, recv = _local_step(
        x, loss_target, lbl, norm1_g, hgrn_norm_g, q_a_norm_g, kv_a_norm_g, mla_norm_g, norm2_g, final_norm_g[None, :],
        *(_join_shards(n, a) for n, a in zip(_EARLY, early)), [w16[n] for n in _LATE])

    n_small = sum(size for _, size in _SMALL)
    small = jnp.concatenate([g[n].reshape(-1) for n, _ in _SMALL] + [g["loss"], jnp.zeros((SMALL_ROWS * 128 - n_small - 1,), F32)])
    *recv_early, small_all = _scatter_grads([early16[n] for n in _EARLY], small.reshape(SMALL_ROWS, 128))
    recv.update(zip(_EARLY, recv_early))
    full["w_kv_b"] = _own_segment("w_kv_b", full["w_kv_b"], chip)
    chip1 = chip.reshape(1).astype(jnp.int32)
    parts = [_sum_segments(chip1, full[n], recv[n], *_OWN_ROWS[n], f"sum_{n}") for n in matrices]
    sibs = _swap_sibling(parts)
    small_sum = _sum_devices(small_all).reshape(-1)
    loss = small_sum[n_small]

    out = {}
    for n, part, sib in zip(matrices, parts, sibs):
        res = _adamw(shard2d(w[n], n), shard2d(m[n], n), shard2d(v[n], n), part, sib, f"adamw_{n}")
        out[n] = tuple(unshard2d(r, n) for r in res)

    small_g, off = {}, 0
    for n, size in _SMALL:
        small_g[n] = small_sum[off:off + size]
        off += size
    small_g["lb_logits"] = lax.dynamic_slice_in_dim(small_g["lb_logits"].reshape(2, 2, A_WIDTH), chip * 128, 128, axis=2)
    small_names = tuple(n for n, _ in _SMALL)

    def pack(d):
        return _pad_rows(jnp.concatenate([d[n].reshape(-1) for n in small_names]).reshape(-1, 128), _UPDATE_ROWS)

    res = _adamw(pack(w), pack(m), pack(v), pack(small_g), None, "adamw_small")
    off = 0
    flat = [r.reshape(-1) for r in res]
    for n in small_names:
        size = w[n].size
        out[n] = tuple(f[off:off + size].reshape(w[n].shape) for f in flat)
        off += size

    return (loss, grad_x) + tuple(out[n][i] for i in range(4) for n in names)
```

```python
import functools

import jax
import jax.numpy as jnp
from jax import lax
from jax.experimental import pallas as pl
from jax.experimental.pallas import tpu as pltpu

F32 = jnp.float32
BF16 = jnp.bfloat16

D_MODEL = 1024
A_WIDTH = 512
HEAD_PAIRS = 4
CHUNK = 64
B_HEADS = 4
B_NOPE = 128
B_ROPE = 64
B_V = 128
QK_PAD = 256
Q_LORA = 384
KV_LORA = 256
D_FF = 2816
D_IN = 3264
D_IN_PAD = 3328
IN_WIDTHS = (512, 512, 512, 512, 512, Q_LORA, KV_LORA, 128)
ROPE_THETA = 10000.0
EPS = 1e-6
ATTN_SCALE = (B_NOPE + B_ROPE) ** -0.5
LOG2E = 1.4426950408889634
SCALE_LOG2E = ATTN_SCALE * LOG2E

ADAM_LR = 0.001
ADAM_B1 = 0.9
ADAM_B2 = 0.999
ADAM_EPS = 1e-08
ADAM_WD = 0.01
ADAM_STEP = 10

VMEM_LIMIT_BYTES = 60 * 1024 * 1024
N_CHIPS = 4
SMALL_ROWS = 56


def _params(n_axes):
    return pltpu.CompilerParams(dimension_semantics=("arbitrary",) * n_axes,
                                vmem_limit_bytes=VMEM_LIMIT_BYTES)


def _dot(a, b):
    return jnp.dot(a, b, preferred_element_type=F32)


def _dot_nt(a, b):
    return lax.dot_general(a, b, (((1,), (1,)), ((), ())), preferred_element_type=F32)


def _dot_tn(a, b):
    return lax.dot_general(a, b, (((0,), (0,)), ((), ())), preferred_element_type=F32)


def _split3(x):
    x1 = x.astype(BF16)
    r = x - x1.astype(F32)
    x2 = r.astype(BF16)
    x3 = (r - x2.astype(F32)).astype(BF16)
    return x1, x2, x3


def _exact_left(m16, x):
    x1, x2, x3 = _split3(x)
    return _dot(m16, x1) + _dot(m16, x2) + _dot(m16, x3)


def _exact_right(x, m16):
    x1, x2, x3 = _split3(x)
    return _dot(x1, m16) + _dot(x2, m16) + _dot(x3, m16)


def _iota2(shape, dim):
    return lax.broadcasted_iota(jnp.int32, shape, dim)


def _sigmoid(x):
    return jax.nn.sigmoid(x)


def _pick(dim, cap, mult=128):
    if dim <= cap:
        return dim
    best = None
    for d in range(mult, cap + 1, mult):
        if dim % d == 0:
            best = d
    assert best is not None, (dim, cap, mult)
    return best


def _row_tile(t, cap=256):
    return _pick(t, cap, 8)


ROW_GROUP = 256


def _row_groups(tm):
    size = min(tm, ROW_GROUP)
    return [slice(r, r + size) for r in range(0, tm, size)]


def _full(shape, single=False):
    if single:
        return pl.BlockSpec(shape, lambda *_: (0,) * len(shape), pipeline_mode=pl.Buffered(1))
    return pl.BlockSpec(shape, lambda *_: (0,) * len(shape))


def _rows(tm, width):
    return pl.BlockSpec((tm, width), lambda i: (i, 0))


def _acc_rows(ref, val, first):
    s = jnp.sum(val, axis=0, keepdims=True)

    @pl.when(first)
    def _():
        ref[...] = s

    @pl.when(jnp.logical_not(first))
    def _():
        ref[...] += s


def _in_fwd(x, g1, w_in_p):
    t = x.shape[0]
    tm = _row_tile(t, 2 * ROW_GROUP)

    def body(x_ref, g_ref, w_ref, h_ref, *outs):
        gain = g_ref[...]

        def group(xv):
            r = lax.rsqrt(jnp.mean(xv * xv, axis=-1, keepdims=True) + EPS)
            h = ((xv * r) * gain).astype(BF16)
            proj, off = [], 0
            for w in IN_WIDTHS:
                proj.append(_dot_nt(h, w_ref[off:off + w, :]))
                off += w
            return h, proj

        groups = _row_groups(tm)
        done = [group(xv) for xv in [x_ref[g, :] for g in groups]]
        for g, (h, proj) in zip(groups, done):
            h_ref[g, :] = h
            for o_ref, p in zip(outs, proj):
                o_ref[g, :] = p

    return pl.pallas_call(
        body, name="in_fwd", grid=(t // tm,),
        in_specs=[_rows(tm, D_MODEL), _full((1, D_MODEL)), _full((D_IN_PAD, D_MODEL))],
        out_specs=[_rows(tm, D_MODEL)] + [_rows(tm, w) for w in IN_WIDTHS],
        out_shape=[jax.ShapeDtypeStruct((t, D_MODEL), BF16)]
        + [jax.ShapeDtypeStruct((t, w), F32) for w in IN_WIDTHS],
        compiler_params=_params(1),
    )(x, g1, w_in_p)


def _in_bwd(dq_f, dq_r, dv_f, dv_r, dz_f, dz_r, dhg, hq, dcq, dckv, dkr, dx2, x, g1, w_in_p):
    t = x.shape[0]
    tm = _row_tile(t, 2 * ROW_GROUP)

    def body(dqf_ref, dqr_ref, dvf_ref, dvr_ref, dzf_ref, dzr_ref, dhg_ref, hq_ref, dcq_ref, dckv_ref,
             dkr_ref, dx2_ref, x_ref, g_ref, w_ref, dx_ref, dp_ref, dg_ref):
        gain = g_ref[...]

        def group(dqf, dqr, dvf, dvr, dzf, dzr, dhg, hqv, dcq, dckv, dkr, dx2, xv):
            sg = _sigmoid(hqv)
            dhq = (dqf + dqr) * (sg * (1.0 + hqv * (1.0 - sg)))
            pieces = [p.astype(BF16) for p in (dhq, dvf + dvr, dzf, dzr, dhg, dcq, dckv, dkr)]
            dh, off = None, 0
            for p16, w in zip(pieces, IN_WIDTHS):
                part = _dot(p16, w_ref[off:off + w, :])
                dh = part if dh is None else dh + part
                off += w
            r = lax.rsqrt(jnp.mean(xv * xv, axis=-1, keepdims=True) + EPS)
            xh = xv * r
            dxh = dh * gain
            return dx2 + r * (dxh - xh * jnp.mean(dxh * xh, axis=-1, keepdims=True)), pieces, dh * xh

        ins = (dqf_ref, dqr_ref, dvf_ref, dvr_ref, dzf_ref, dzr_ref, dhg_ref, hq_ref, dcq_ref, dckv_ref, dkr_ref,
               dx2_ref, x_ref)
        groups = _row_groups(tm)
        done = [group(*vals) for vals in [[ref[g, :] for ref in ins] for g in groups]]
        for g, (dx, pieces, _) in zip(groups, done):
            dx_ref[g, :] = dx
            off = 0
            for p16, w in zip(pieces, IN_WIDTHS):
                dp_ref[g, off:off + w] = p16
                off += w
        _acc_rows(dg_ref, jnp.concatenate([d[2] for d in done], axis=0), pl.program_id(0) == 0)

    a512 = _rows(tm, A_WIDTH)
    return pl.pallas_call(
        body, name="in_bwd", grid=(t // tm,),
        in_specs=[a512] * 8 + [_rows(tm, Q_LORA), _rows(tm, KV_LORA), _rows(tm, 128), _rows(tm, D_MODEL),
                               _rows(tm, D_MODEL), _full((1, D_MODEL)), _full((D_IN_PAD, D_MODEL))],
        out_specs=[_rows(tm, D_MODEL), _rows(tm, D_IN_PAD), _full((1, D_MODEL))],
        out_shape=[jax.ShapeDtypeStruct((t, D_MODEL), F32), jax.ShapeDtypeStruct((t, D_IN_PAD), BF16),
                   jax.ShapeDtypeStruct((1, D_MODEL), F32)],
        compiler_params=_params(1),
    )(dq_f, dq_r, dv_f, dv_r, dz_f, dz_r, dhg, hq, dcq, dckv, dkr, dx2, x, g1, w_in_p)


def _lower_bound(lbl_ref, direction):
    l0 = lbl_ref[direction, 0:1, :]
    l1 = lbl_ref[direction, 1:2, :]
    m = jnp.maximum(l0, l1)
    e0 = jnp.exp(l0 - m)
    e1 = jnp.exp(l1 - m)
    return e0 / (e0 + e1)


def _hgrn_consts(rb, reverse):
    row = _iota2((rb, rb), 0)
    col = _iota2((rb, rb), 1)
    same = (row // CHUNK) == (col // CHUNK)
    tri = jnp.logical_and(same, (col >= row) if reverse else (col <= row))
    tri_t = jnp.logical_and(same, (col <= row) if reverse else (col >= row))
    r128 = _iota2((128, 128), 0)
    c128 = _iota2((128, 128), 1)
    bd = (r128 < 64) == (c128 < 64)
    lane = _iota2((1, 128), 1)
    m0 = (lane < 64).astype(F32)
    return tri, tri_t, bd, (m0, 1.0 - m0)


def _per_chunk(x, fn):
    n = x.shape[0] // CHUNK
    return jnp.concatenate([jnp.broadcast_to(fn(x[c * CHUNK:(c + 1) * CHUNK]), (CHUNK, x.shape[1])) for c in range(n)],
                           axis=0)


def _chunk_cumsum(x, reverse):
    rb = x.shape[0]
    pos = _iota2(x.shape, 0) % CHUNK
    step = 1
    while step < CHUNK:
        if reverse:
            x = x + jnp.where(pos < CHUNK - step, pltpu.roll(x, rb - step, 0), 0.0)
        else:
            x = x + jnp.where(pos >= step, pltpu.roll(x, step, 0), 0.0)
        step *= 2
    return x


def _hgrn_block(z, hqv, lb, reverse):
    sig = _sigmoid(z)
    sn = _sigmoid(-z)
    q = hqv * _sigmoid(hqv)
    f = lb + (1.0 - lb) * sig
    k = (1.0 - lb) * sn
    lf = jnp.log(f)
    cum = _chunk_cumsum(lf, reverse)
    last = _per_chunk(cum, (lambda a: a[0:1]) if reverse else (lambda a: a[CHUNK - 1:CHUNK]))
    e_neg = jnp.exp(-cum)
    e_end = jnp.exp(last - cum)
    a = jnp.exp(cum)
    return dict(sig=sig, sn=sn, q=q, f=f, k=k, a=a, e_neg=e_neg, e_end=e_end,
                q_dec=q * a, k_inv=k * e_neg, k_end=k * e_end, d=jnp.exp(last))


def _hgrn_dims(t, n_batch):
    s = t // n_batch
    rb = _pick(s, 256, CHUNK)
    return s, rb, s // rb, rb // CHUNK


def _hgrn_fwd(hq, hi, hf, lbl, *, n_batch, direction):
    t = hq.shape[0]
    reverse = direction == 1
    s, rb, nb, nc = _hgrn_dims(t, n_batch)

    def tmap(b, j):
        return (b * nb + ((nb - 1 - j) if reverse else j), 0)

    def smap(b, j):
        return (b * nb + ((nb - 1 - j) if reverse else j), 0, 0, 0)

    def body(hq_ref, hi_ref, hf_ref, lbl_ref, o_ref, st_ref, st_scr):
        @pl.when(pl.program_id(1) == 0)
        def _():
            st_scr[...] = jnp.zeros_like(st_scr)

        lb_all = _lower_bound(lbl_ref, direction)
        tri, _, bd, masks = _hgrn_consts(rb, reverse)
        order = range(nc - 1, -1, -1) if reverse else range(nc)

        pairs = [slice(p * 128, (p + 1) * 128) for p in range(HEAD_PAIRS)]
        chunks = [slice(c * CHUNK, (c + 1) * CHUNK) for c in range(nc)]
        w = _hgrn_block(hf_ref[...], hq_ref[...], lb_all, reverse)
        v16 = hi_ref[...].astype(BF16)
        qd16 = w["q_dec"].astype(BF16)
        ki16 = w["k_inv"].astype(BF16)
        ke16 = w["k_end"].astype(BF16)
        sc = [[_dot_nt(jnp.where(mh > 0.0, qd16[:, ls], 0.0).astype(BF16), ki16[:, ls]) for mh in masks] for ls in pairs]
        pv = [[_dot(jnp.where(tri, s_e, 0.0).astype(BF16), v16[:, ls]) for s_e in sc_p] for sc_p, ls in zip(sc, pairs)]
        o_intra = [jnp.where(masks[0] > 0.0, pv_p[0], pv_p[1]) for pv_p in pv]
        ut = [[jnp.where(bd, _dot_tn(v16[rs, ls], ke16[rs, ls]), 0.0) for ls in pairs] for rs in chunks]
        st = [st_scr[p] for p in range(HEAD_PAIRS)]
        for c in order:
            rs = chunks[c]
            inter = [_dot_nt(qd16[rs, ls], st[p].astype(BF16)) for p, ls in enumerate(pairs)]
            for p, ls in enumerate(pairs):
                o_ref[rs, ls] = o_intra[p][rs] + inter[p]
                st_ref[c, p] = st[p]
                st[p] = st[p] * w["d"][c * CHUNK:c * CHUNK + 1, ls] + ut[c][p]
        for p in range(HEAD_PAIRS):
            st_scr[p] = st[p]

    blk = pl.BlockSpec((rb, A_WIDTH), tmap)
    return pl.pallas_call(
        body, name=f"hgrn_fwd_{direction}", grid=(n_batch, nb),
        in_specs=[blk, blk, blk, _full((2, 2, A_WIDTH))],
        out_specs=[blk, pl.BlockSpec((nc, HEAD_PAIRS, 128, 128), smap)],
        out_shape=[jax.ShapeDtypeStruct((t, A_WIDTH), F32),
                   jax.ShapeDtypeStruct((t // CHUNK, HEAD_PAIRS, 128, 128), F32)],
        scratch_shapes=[pltpu.VMEM((HEAD_PAIRS, 128, 128), F32)],
        compiler_params=_params(2),
    )(hq, hi, hf, lbl)


def _hgrn_bwd(hq, hi, hf, do, st, lbl, *, n_batch, direction):
    t = hq.shape[0]
    reverse = direction == 1
    s, rb, nb, nc = _hgrn_dims(t, n_batch)

    def tmap(b, j):
        return (b * nb + (j if reverse else (nb - 1 - j)), 0)

    def smap(b, j):
        return (b * nb + (j if reverse else (nb - 1 - j)), 0, 0, 0)

    def body(hq_ref, hi_ref, hf_ref, do_ref, st_ref, lbl_ref, dq_ref, dv_ref, dz_ref, dl_ref, g_scr, dlb_scr):
        b = pl.program_id(0)
        j = pl.program_id(1)

        @pl.when(jnp.logical_and(b == 0, j == 0))
        def _():
            dlb_scr[...] = jnp.zeros_like(dlb_scr)

        @pl.when(j == 0)
        def _():
            g_scr[...] = jnp.zeros_like(g_scr)

        lb_all = _lower_bound(lbl_ref, direction)
        tri, tri_t, bd, masks = _hgrn_consts(rb, reverse)
        order = range(nc) if reverse else range(nc - 1, -1, -1)

        pairs = [slice(p * 128, (p + 1) * 128) for p in range(HEAD_PAIRS)]
        chunks = [slice(c * CHUNK, (c + 1) * CHUNK) for c in range(nc)]

        def lanes(per_pair):
            return jnp.concatenate(per_pair, axis=1)

        w = _hgrn_block(hf_ref[...], hq_ref[...], lb_all, reverse)
        dov = do_ref[...]
        v16 = hi_ref[...].astype(BF16)
        do16 = dov.astype(BF16)
        qd16 = w["q_dec"].astype(BF16)
        ki16 = w["k_inv"].astype(BF16)
        ke16 = w["k_end"].astype(BF16)
        qm16 = [[jnp.where(mh > 0.0, qd16[:, ls], 0.0).astype(BF16) for mh in masks] for ls in pairs]
        dom16 = [[jnp.where(mh > 0.0, do16[:, ls], 0.0).astype(BF16) for mh in masks] for ls in pairs]
        heads = [(p, e) for p in range(HEAD_PAIRS) for e in range(2)]
        dp = {(p, e): _dot_nt(dom16[p][e], v16[:, pairs[p]]) for p, e in heads}
        pm_t = {(p, e): _dot_nt(ki16[:, pairs[p]], qm16[p][e]) for p, e in heads}
        dp_t = {(p, e): _dot_nt(v16[:, pairs[p]], dom16[p][e]) for p, e in heads}
        dp = {h: jnp.where(tri, a, 0.0).astype(BF16) for h, a in dp.items()}
        pm_t = {h: jnp.where(tri_t, a, 0.0).astype(BF16) for h, a in pm_t.items()}
        dp_t = {h: jnp.where(tri_t, a, 0.0).astype(BF16) for h, a in dp_t.items()}
        dv_e = {(p, e): _dot(pm_t[p, e], do16[:, pairs[p]]) for p, e in heads}
        dq_e = {(p, e): _dot(dp[p, e], ki16[:, pairs[p]]) for p, e in heads}
        dk_e = {(p, e): _dot(dp_t[p, e], qd16[:, pairs[p]]) for p, e in heads}
        st = [[st_ref[c, p] for p in range(HEAD_PAIRS)] for c in range(nc)]
        dq_x = [[_dot(do16[rs, ls], st[c][p].astype(BF16)) for p, ls in enumerate(pairs)] for c, rs in enumerate(chunks)]
        gq = [[jnp.where(bd, _dot_tn(do16[rs, ls], qd16[rs, ls]), 0.0) for ls in pairs] for rs in chunks]
        g = [g_scr[p] for p in range(HEAD_PAIRS)]
        dk_end, dv_x, dd = [None] * nc, [None] * nc, [None] * nc
        for c in order:
            rs = chunks[c]
            g16 = [a.astype(BF16) for a in g]
            dk_end[c] = lanes([_dot(v16[rs, ls], g16[p]) for p, ls in enumerate(pairs)])
            dv_x[c] = lanes([_dot_nt(ke16[rs, ls], g16[p]) for p, ls in enumerate(pairs)])
            dd[c] = jnp.broadcast_to(lanes([jnp.sum(g[p] * st[c][p], axis=0, keepdims=True) for p in range(HEAD_PAIRS)]),
                                     (CHUNK, A_WIDTH))
            for p, ls in enumerate(pairs):
                g[p] = g[p] * w["d"][c * CHUNK:c * CHUNK + 1, ls] + gq[c][p]
        for p in range(HEAD_PAIRS):
            g_scr[p] = g[p]

        def both_heads(d):
            return lanes([jnp.where(masks[0] > 0.0, d[p, 0], d[p, 1]) for p in range(HEAD_PAIRS)])

        dq_dec = both_heads(dq_e) + jnp.concatenate([lanes(a) for a in dq_x], axis=0)
        dk_inv = both_heads(dk_e)
        dk_end = jnp.concatenate(dk_end, axis=0)
        dv = both_heads(dv_e) + jnp.concatenate(dv_x, axis=0)
        dd = jnp.concatenate(dd, axis=0)
        dke = dk_end * w["k_end"]
        dcum = dq_dec * w["q_dec"] - dk_inv * w["k_inv"] - dke
        dk = dk_inv * w["e_neg"] + dk_end * w["e_end"]
        dlast = _per_chunk(dke, lambda a: jnp.sum(a, axis=0, keepdims=True)) + dd * w["d"]
        dlf = _chunk_cumsum(dcum, not reverse) + dlast
        tt = dlf / w["f"] - dk
        dq_ref[...] = dq_dec * w["a"]
        dv_ref[...] = dv
        dz_ref[...] = ((1.0 - lb_all) * w["sig"] * w["sn"] * tt).astype(BF16)
        dlb_scr[...] += jnp.sum(w["sn"] * tt, axis=0, keepdims=True)

        @pl.when(jnp.logical_and(b == pl.num_programs(0) - 1, j == pl.num_programs(1) - 1))
        def _():
            d0 = dlb_scr[...] * lb_all * (1.0 - lb_all)
            dl_ref[0:1, :] = d0
            dl_ref[1:2, :] = -d0

    blk = pl.BlockSpec((rb, A_WIDTH), tmap)
    return pl.pallas_call(
        body, name=f"hgrn_bwd_{direction}", grid=(n_batch, nb),
        in_specs=[blk, blk, blk, blk, pl.BlockSpec((nc, HEAD_PAIRS, 128, 128), smap), _full((2, 2, A_WIDTH))],
        out_specs=[blk, blk, blk, _full((2, A_WIDTH))],
        out_shape=[jax.ShapeDtypeStruct((t, A_WIDTH), F32)] * 2
        + [jax.ShapeDtypeStruct((t, A_WIDTH), BF16), jax.ShapeDtypeStruct((2, A_WIDTH), F32)],
        scratch_shapes=[pltpu.VMEM((HEAD_PAIRS, 128, 128), F32), pltpu.VMEM((1, A_WIDTH), F32)],
        compiler_params=_params(2),
    )(hq, hi, hf, do, st, lbl)


def _swap_rope_halves(x):
    lane = _iota2(x.shape, 1)
    return jnp.where(lane < 32, pltpu.roll(x, 96, 1), pltpu.roll(x, 32, 1))


def _rms_fwd(xv, g):
    r = lax.rsqrt(jnp.mean(xv * xv, axis=-1, keepdims=True) + EPS)
    return (xv * r) * g


def _rms_bwd(dy, xv, g):
    r = lax.rsqrt(jnp.mean(xv * xv, axis=-1, keepdims=True) + EPS)
    xh = xv * r
    dxh = dy * g
    return r * (dxh - xh * jnp.mean(dxh * xh, axis=-1, keepdims=True)), dy * xh


def _mla_prep(cq, ckv, kr, g_qa, g_kva, w_q_p, w_kv_p, cosx, sinx, *, n_batch):
    t = cq.shape[0]
    s = t // n_batch
    tm = _row_tile(s)
    nt = s // tm

    def body(cq_ref, ckv_ref, kr_ref, gq_ref, gkv_ref, wq_ref, wkv_ref, cos_ref, sin_ref,
             q_ref, k_ref, v_ref, cqn_ref, ckvn_ref):
        cos, sin = cos_ref[...], sin_ref[...]
        cqn = _rms_fwd(cq_ref[...], gq_ref[...]).astype(BF16)
        ckvn = _rms_fwd(ckv_ref[...], gkv_ref[...]).astype(BF16)
        cqn_ref[...] = cqn
        ckvn_ref[...] = ckvn
        krv = kr_ref[...]
        kr_roped = (krv * cos + _swap_rope_halves(krv) * sin).astype(BF16)
        for h in range(B_HEADS):
            o = h * QK_PAD
            q_ref[:, o:o + 128] = _dot_nt(cqn, wq_ref[o:o + 128, :]).astype(BF16)
            qr = _dot_nt(cqn, wq_ref[o + 128:o + 256, :])
            q_ref[:, o + 128:o + 256] = (qr * cos + _swap_rope_halves(qr) * sin).astype(BF16)
            k_ref[:, o:o + 128] = _dot(ckvn, wkv_ref[:, h * 128:(h + 1) * 128]).astype(BF16)
            k_ref[:, o + 128:o + 256] = kr_roped
        v_ref[...] = _dot(ckvn, wkv_ref[:, 512:1024]).astype(BF16)

    tab = pl.BlockSpec((tm, 128), lambda i: (i % nt, 0))
    return pl.pallas_call(
        body, name="mla_prep", grid=(t // tm,),
        in_specs=[_rows(tm, Q_LORA), _rows(tm, KV_LORA), _rows(tm, 128), _full((1, Q_LORA)), _full((1, KV_LORA)),
                  _full((1024, Q_LORA)), _full((KV_LORA, 1024)), tab, tab],
        out_specs=[_rows(tm, 1024), _rows(tm, 1024), _rows(tm, 512), _rows(tm, Q_LORA), _rows(tm, KV_LORA)],
        out_shape=[jax.ShapeDtypeStruct((t, 1024), BF16), jax.ShapeDtypeStruct((t, 1024), BF16),
                   jax.ShapeDtypeStruct((t, 512), BF16), jax.ShapeDtypeStruct((t, Q_LORA), BF16),
                   jax.ShapeDtypeStruct((t, KV_LORA), BF16)],
        compiler_params=_params(1),
    )(cq, ckv, kr, g_qa, g_kva, w_q_p, w_kv_p, cosx, sinx)


def _mla_prep_bwd(dqt, dk, dv, cq, ckv, g_qa, g_kva, w_q_p, w_kv_p, cosx, sinx, *, n_batch):
    t = cq.shape[0]
    s = t // n_batch
    tm = _row_tile(s)
    nt = s // tm

    def body(dqt_ref, dk_ref, dv_ref, cq_ref, ckv_ref, gq_ref, gkv_ref, wq_ref, wkv_ref, cos_ref, sin_ref,
             dcq_ref, dckv_ref, dkr_ref, dqp_ref, dkvp_ref, dgq_ref, dgkv_ref):
        cos, sin = cos_ref[...], sin_ref[...]
        first = pl.program_id(0) == 0

        def unrope(d):
            return d * cos + _swap_rope_halves(d * sin)

        dcqn = None
        dkr = None
        dckvn = None
        for h in range(B_HEADS):
            o = h * QK_PAD
            dq_h = jnp.transpose(dqt_ref[o:o + QK_PAD, :])
            dqn16 = dq_h[:, 0:128].astype(BF16)
            dqr16 = unrope(dq_h[:, 128:256]).astype(BF16)
            dqp_ref[:, o:o + 128] = dqn16
            dqp_ref[:, o + 128:o + 256] = dqr16
            part = _dot(dqn16, wq_ref[o:o + 128, :]) + _dot(dqr16, wq_ref[o + 128:o + 256, :])
            dcqn = part if dcqn is None else dcqn + part
            dkn16 = dk_ref[:, o:o + 128].astype(BF16)
            dkvp_ref[:, h * 128:(h + 1) * 128] = dkn16
            part = _dot_nt(dkn16, wkv_ref[:, h * 128:(h + 1) * 128])
            dckvn = part if dckvn is None else dckvn + part
            kr_part = dk_ref[:, o + 128:o + 256]
            dkr = kr_part if dkr is None else dkr + kr_part
        dv16 = dv_ref[...].astype(BF16)
        dkvp_ref[:, 512:1024] = dv16
        dckvn = dckvn + _dot_nt(dv16, wkv_ref[:, 512:1024])
        dkr_ref[...] = unrope(dkr).astype(BF16)
        dcq, dgq = _rms_bwd(dcqn, cq_ref[...], gq_ref[...])
        dckv, dgkv = _rms_bwd(dckvn, ckv_ref[...], gkv_ref[...])
        dcq_ref[...] = dcq.astype(BF16)
        dckv_ref[...] = dckv.astype(BF16)
        _acc_rows(dgq_ref, dgq, first)
        _acc_rows(dgkv_ref, dgkv, first)

    tab = pl.BlockSpec((tm, 128), lambda i: (i % nt, 0))
    return pl.pallas_call(
        body, name="mla_prep_bwd", grid=(t // tm,),
        in_specs=[pl.BlockSpec((1024, tm), lambda i: (0, i)), _rows(tm, 1024), _rows(tm, 512), _rows(tm, Q_LORA),
                  _rows(tm, KV_LORA),
                  _full((1, Q_LORA)), _full((1, KV_LORA)), _full((1024, Q_LORA)), _full((KV_LORA, 1024)), tab, tab],
        out_specs=[_rows(tm, Q_LORA), _rows(tm, KV_LORA), _rows(tm, 128), _rows(tm, 1024), _rows(tm, 1024),
                   _full((1, Q_LORA)), _full((1, KV_LORA))],
        out_shape=[jax.ShapeDtypeStruct((t, Q_LORA), BF16), jax.ShapeDtypeStruct((t, KV_LORA), BF16),
                   jax.ShapeDtypeStruct((t, 128), BF16), jax.ShapeDtypeStruct((t, 1024), BF16),
                   jax.ShapeDtypeStruct((t, 1024), BF16), jax.ShapeDtypeStruct((1, Q_LORA), F32),
                   jax.ShapeDtypeStruct((1, KV_LORA), F32)],
        compiler_params=_params(1),
    )(dqt, dk, dv, cq, ckv, g_qa, g_kva, w_q_p, w_kv_p, cosx, sinx)


def _attn_dims(t, n_batch):
    s = t // n_batch
    tq = _pick(s, 1024, 128)
    return s, tq, s // tq


ATTN_ROWS = 256


def _grid_ends(n_axes):
    ids = [pl.program_id(a) for a in range(n_axes)]
    first = functools.reduce(jnp.logical_and, [i == 0 for i in ids])
    last = functools.reduce(jnp.logical_and, [i == pl.num_programs(a) - 1 for a, i in enumerate(ids)])
    return first, last


def _attn_fwd(q, k, v, wsrcs, *, n_batch):
    t = q.shape[0]
    s, tq, nq = _attn_dims(t, n_batch)
    nw = len(wsrcs)

    def body(q_ref, k_ref, v_ref, *refs):
        w_refs, (o_ref, lse_ref), wall_refs = refs[:nw], refs[nw:nw + 2], refs[nw + 2:2 * nw + 2]
        send_sems, recv_sems, local_sems = refs[2 * nw + 2:]
        first, last = _grid_ends(3)

        @pl.when(first)
        def _():
            for i in range(nw):
                _gather_start(w_refs[i], wall_refs[i], send_sems, recv_sems, local_sems.at[i], 3 * i)

        @pl.when(last)
        def _():
            for i in range(nw):
                _gather_wait(w_refs[i], wall_refs[i], send_sems, recv_sems, local_sems.at[i], 3 * i)

        kv, vv = k_ref[...], v_ref[...]
        groups = [slice(r, r + ATTN_ROWS) for r in range(0, tq, ATTN_ROWS)]
        raw = [_dot_nt(q_ref[g, :], kv) for g in groups]
        m = [jnp.max(a, axis=-1, keepdims=True) for a in raw]
        p = [jnp.exp2((a - b) * SCALE_LOG2E) for a, b in zip(raw, m)]
        l = [jnp.sum(a, axis=-1, keepdims=True) for a in p]
        for g, pg, mg, lg in zip(groups, p, m, l):
            o_ref[g, :] = _dot(pg.astype(BF16), vv) / lg
            lse2 = mg * SCALE_LOG2E + jnp.log(lg) * LOG2E
            lse_ref[:, g] = jnp.transpose(jnp.broadcast_to(lse2, (ATTN_ROWS, 128)))[0:1, :]

    return pl.pallas_call(
        body, name="attn_fwd", grid=(n_batch, B_HEADS, nq),
        in_specs=[pl.BlockSpec((tq, QK_PAD), lambda b, h, i: (b * nq + i, h)),
                  pl.BlockSpec((s, QK_PAD), lambda b, h, i: (b, h)),
                  pl.BlockSpec((s, B_V), lambda b, h, i: (b, h))] + [_HBM] * nw,
        out_specs=[pl.BlockSpec((tq, B_V), lambda b, h, i: (b * nq + i, h)),
                   pl.BlockSpec((None, 1, tq), lambda b, h, i: (h, 0, b * nq + i))] + [_HBM] * nw,
        out_shape=[jax.ShapeDtypeStruct((t, B_HEADS * B_V), F32), jax.ShapeDtypeStruct((B_HEADS, 1, t), F32)]
        + [jax.ShapeDtypeStruct((N_CHIPS,) + w.shape, w.dtype) for w in wsrcs],
        scratch_shapes=[pltpu.SemaphoreType.DMA((3 * nw,)), pltpu.SemaphoreType.DMA((3 * nw,)),
                        pltpu.SemaphoreType.DMA((nw,))],
        compiler_params=_params(3),
    )(q, k, v, *wsrcs)


def _attn_bwd(q, k, v, do16, lse, delta, gsegs, *, n_batch):
    t = q.shape[0]
    s = t // n_batch
    tk = _pick(s, 512, 128)
    nk = s // tk
    ng = len(gsegs)

    def body(q_ref, k_ref, v_ref, do_ref, lse_ref, dl_ref, *refs):
        g_refs, (dqt_ref, dk_ref, dv_ref), recv_refs = refs[:ng], refs[ng:ng + 3], refs[ng + 3:2 * ng + 3]
        send_sems, recv_sems = refs[2 * ng + 3:]
        first, last = _grid_ends(3)

        @pl.when(first)
        def _():
            for i in range(ng):
                _scatter_start(g_refs[i], recv_refs[i], send_sems, recv_sems, 3 * i)

        @pl.when(last)
        def _():
            for i in range(ng):
                _scatter_wait(g_refs[i], recv_refs[i], send_sems, recv_sems, 3 * i)

        j = pl.program_id(2)
        qv, kv, dov = q_ref[...], k_ref[...], do_ref[...]
        pt = jnp.exp2(_dot_nt(kv, qv) * SCALE_LOG2E - lse_ref[...])
        dv_ref[...] = _dot(pt.astype(BF16), dov)
        dpt = _dot_nt(v_ref[...], dov)
        dst = (pt * (dpt - dl_ref[...])).astype(BF16)
        dk_ref[...] = _dot(dst, qv) * ATTN_SCALE
        part = _dot_tn(kv, dst)

        @pl.when(j == 0)
        def _():
            dqt_ref[...] = part

        @pl.when(j != 0)
        def _():
            dqt_ref[...] += part

        @pl.when(j == nk - 1)
        def _():
            dqt_ref[...] = dqt_ref[...] * ATTN_SCALE

    row = pl.BlockSpec((None, 1, s), lambda b, h, j: (h, 0, b))
    return pl.pallas_call(
        body, name="attn_bwd", grid=(n_batch, B_HEADS, nk),
        in_specs=[pl.BlockSpec((s, QK_PAD), lambda b, h, j: (b, h)),
                  pl.BlockSpec((tk, QK_PAD), lambda b, h, j: (b * nk + j, h)),
                  pl.BlockSpec((tk, B_V), lambda b, h, j: (b * nk + j, h)),
                  pl.BlockSpec((s, B_V), lambda b, h, j: (b, h)), row, row] + [_HBM] * ng,
        out_specs=[pl.BlockSpec((QK_PAD, s), lambda b, h, j: (h, b)),
                   pl.BlockSpec((tk, QK_PAD), lambda b, h, j: (b * nk + j, h)),
                   pl.BlockSpec((tk, B_V), lambda b, h, j: (b * nk + j, h))] + [_HBM] * ng,
        out_shape=[jax.ShapeDtypeStruct((B_HEADS * QK_PAD, t), F32), jax.ShapeDtypeStruct((t, B_HEADS * QK_PAD), F32),
                   jax.ShapeDtypeStruct((t, B_HEADS * B_V), F32)]
        + [jax.ShapeDtypeStruct((3,) + g.shape[1:], g.dtype) for g in gsegs],
        scratch_shapes=[pltpu.SemaphoreType.DMA((3 * ng,)), pltpu.SemaphoreType.DMA((3 * ng,))],
        compiler_params=_params(3),
    )(q, k, v, do16, lse, delta, *gsegs)


def _group_ones16():
    r = _iota2((A_WIDTH, A_WIDTH), 0) // 64
    c = _iota2((A_WIDTH, A_WIDTH), 1) // 64
    return (r == c).astype(BF16)


def _head_rms(o, ones16):
    return lax.rsqrt(_exact_right(o * o, ones16) * (1.0 / 64.0) + EPS)


def _out_fwd(o_f, o_r, hg, o_attn, x, g_hn, g_mla, w_out, g2):
    t = x.shape[0]
    tm = _row_tile(t, 2 * ROW_GROUP)

    def body(of_ref, or_ref, hg_ref, oa_ref, x_ref, ghn_ref, gm_ref, w_ref, g2_ref, y_ref, x2_ref, h2_ref):
        ones16 = _group_ones16()
        ghn, gm, g2v = ghn_ref[...], gm_ref[...], g2_ref[...]

        def group(ofv, orv, hgv, oav, xv):
            o = ofv + orv
            ya16 = (((o * _head_rms(o, ones16)) * ghn) * (hgv * _sigmoid(hgv))).astype(BF16)
            yb16 = _rms_fwd(oav, gm).astype(BF16)
            x2 = xv + _dot(ya16, w_ref[0:A_WIDTH, :]) + _dot(yb16, w_ref[A_WIDTH:D_MODEL, :])
            return ya16, yb16, x2, _rms_fwd(x2, g2v).astype(BF16)

        ins = (of_ref, or_ref, hg_ref, oa_ref, x_ref)
        groups = _row_groups(tm)
        done = [group(*vals) for vals in [[ref[g, :] for ref in ins] for g in groups]]
        for g, (ya16, yb16, x2, h2) in zip(groups, done):
            y_ref[g, 0:A_WIDTH] = ya16
            y_ref[g, A_WIDTH:D_MODEL] = yb16
            x2_ref[g, :] = x2
            h2_ref[g, :] = h2

    a512 = _rows(tm, A_WIDTH)
    return pl.pallas_call(
        body, name="out_fwd", grid=(t // tm,),
        in_specs=[a512, a512, a512, a512, _rows(tm, D_MODEL), _full((1, A_WIDTH)), _full((1, A_WIDTH)),
                  _full((D_MODEL, D_MODEL)), _full((1, D_MODEL))],
        out_specs=[_rows(tm, D_MODEL)] * 3,
        out_shape=[jax.ShapeDtypeStruct((t, D_MODEL), BF16), jax.ShapeDtypeStruct((t, D_MODEL), F32),
                   jax.ShapeDtypeStruct((t, D_MODEL), BF16)],
        compiler_params=_params(1),
    )(o_f, o_r, hg, o_attn, x, g_hn, g_mla, w_out, g2)


def _out_bwd(dx2_16, o_f, o_r, hg, o_attn, g_hn, g_mla, w_out):
    t = dx2_16.shape[0]
    tm = _row_tile(t, 2 * ROW_GROUP)

    def body(dx_ref, of_ref, or_ref, hg_ref, oa_ref, ghn_ref, gm_ref, w_ref,
             do_ref, dhg_ref, doa_ref, dl_ref, dghn_ref, dgm_ref):
        first = pl.program_id(0) == 0
        ones16 = _group_ones16()
        sel16 = (_iota2((8, A_WIDTH), 0) == _iota2((8, A_WIDTH), 1) // B_V).astype(BF16)
        ghn, gm = ghn_ref[...], gm_ref[...]

        def group(dxv, ofv, orv, hgv, oav):
            dya = _dot_nt(dxv, w_ref[0:A_WIDTH, :])
            dyb = _dot_nt(dxv, w_ref[A_WIDTH:D_MODEL, :])
            o = ofv + orv
            rh = _head_rms(o, ones16)
            oh = o * rh
            sg = _sigmoid(hgv)
            sl = hgv * sg
            dhg = ((dya * (oh * ghn)) * (sg * (1.0 + hgv * (1.0 - sg)))).astype(BF16)
            doh = dya * sl * ghn
            do = rh * (doh - oh * (_exact_right(doh * oh, ones16) * (1.0 / 64.0)))
            doa, dgm = _rms_bwd(dyb, oav, gm)
            x1, x2, x3 = _split3(doa * oav)
            delta = _dot_nt(sel16, x1) + _dot_nt(sel16, x2) + _dot_nt(sel16, x3)
            return do, dhg, doa.astype(BF16), delta, dya * sl * oh, dgm

        groups = _row_groups(tm)
        loaded = [(dx_ref[g, :], of_ref[g, :], or_ref[g, :], hg_ref[g, :], oa_ref[g, :]) for g in groups]
        done = [group(*vals) for vals in loaded]
        for g, (do, dhg, doa16, delta, _, _) in zip(groups, done):
            do_ref[g, :] = do
            dhg_ref[g, :] = dhg
            doa_ref[g, :] = doa16
            for h in range(B_HEADS):
                dl_ref[h, :, g] = delta[h:h + 1, :]
        _acc_rows(dghn_ref, jnp.concatenate([d[4] for d in done], axis=0), first)
        _acc_rows(dgm_ref, jnp.concatenate([d[5] for d in done], axis=0), first)

    a512 = _rows(tm, A_WIDTH)
    return pl.pallas_call(
        body, name="out_bwd", grid=(t // tm,),
        in_specs=[_rows(tm, D_MODEL), a512, a512, a512, a512, _full((1, A_WIDTH)), _full((1, A_WIDTH)),
                  _full((D_MODEL, D_MODEL))],
        out_specs=[a512, a512, a512, pl.BlockSpec((B_HEADS, 1, tm), lambda i: (0, 0, i)),
                   _full((1, A_WIDTH)), _full((1, A_WIDTH))],
        out_shape=[jax.ShapeDtypeStruct((t, A_WIDTH), F32)] + [jax.ShapeDtypeStruct((t, A_WIDTH), BF16)] * 2
        + [jax.ShapeDtypeStruct((B_HEADS, 1, t), F32)]
        + [jax.ShapeDtypeStruct((1, A_WIDTH), F32)] * 2,
        compiler_params=_params(1),
    )(dx2_16, o_f, o_r, hg, o_attn, g_hn, g_mla, w_out)


def _ffn_fwd_bwd(h2, x2, target, w_gate, w_up, w_down, g_f, g2):
    t = x2.shape[0]
    tm = _row_tile(t)
    inv_d = 1.0 / D_MODEL

    def body(h2_ref, x2_ref, tg_ref, wg_ref, wu_ref, wd_ref, gf_ref, g2_ref,
             act_ref, dgate_ref, dup_ref, dx3_ref, dx2_ref, dx2h_ref, loss_ref, dgf_ref, dg2_ref):
        first = pl.program_id(0) == 0
        h2v = h2_ref[...]
        gate = _dot_nt(h2v, wg_ref[...])
        up = _dot_nt(h2v, wu_ref[...])
        sg = _sigmoid(gate)
        sl = gate * sg
        act16 = (sl * up).astype(BF16)
        act_ref[...] = act16
        x2v = x2_ref[...]
        x3 = x2v + _dot(act16, wd_ref[...])
        r3 = lax.rsqrt(jnp.mean(x3 * x3, axis=-1, keepdims=True) + EPS)
        x3h = x3 * r3
        gf = gf_ref[...]
        err = x3h * gf - tg_ref[...]
        part = 0.5 * jnp.sum(jnp.mean(err * err, axis=-1, keepdims=True), axis=0, keepdims=True)

        @pl.when(first)
        def _():
            loss_ref[...] = jnp.zeros_like(loss_ref)

        loss_ref[...] += part
        dy = err * inv_d
        _acc_rows(dgf_ref, dy * x3h, first)
        dx3h = dy * gf
        dx3 = r3 * (dx3h - x3h * jnp.mean(dx3h * x3h, axis=-1, keepdims=True))
        dx3_16 = dx3.astype(BF16)
        dx3_ref[...] = dx3_16
        da = _dot_nt(dx3_16, wd_ref[...])
        dup16 = (da * sl).astype(BF16)
        dgate16 = (da * up * (sg * (1.0 + gate * (1.0 - sg)))).astype(BF16)
        dup_ref[...] = dup16
        dgate_ref[...] = dgate16
        dh2 = _dot(dgate16, wg_ref[...]) + _dot(dup16, wu_ref[...])
        dx2n, dg2 = _rms_bwd(dh2, x2v, g2_ref[...])
        _acc_rows(dg2_ref, dg2, first)
        dx2 = dx3 + dx2n
        dx2_ref[...] = dx2
        dx2h_ref[...] = dx2.astype(BF16)

    return pl.pallas_call(
        body, name="ffn_fwd_bwd", grid=(t // tm,),
        in_specs=[_rows(tm, D_MODEL), _rows(tm, D_MODEL), _rows(tm, D_MODEL), _full((D_FF, D_MODEL), True),
                  _full((D_FF, D_MODEL), True), _full((D_FF, D_MODEL), True), _full((1, D_MODEL)), _full((1, D_MODEL))],
        out_specs=[_rows(tm, D_FF), _rows(tm, D_FF), _rows(tm, D_FF), _rows(tm, D_MODEL), _rows(tm, D_MODEL),
                   _rows(tm, D_MODEL), _full((8, 128)), _full((1, D_MODEL)), _full((1, D_MODEL))],
        out_shape=[jax.ShapeDtypeStruct((t, D_FF), BF16)] * 3
        + [jax.ShapeDtypeStruct((t, D_MODEL), BF16), jax.ShapeDtypeStruct((t, D_MODEL), F32),
           jax.ShapeDtypeStruct((t, D_MODEL), BF16), jax.ShapeDtypeStruct((8, 128), F32),
           jax.ShapeDtypeStruct((1, D_MODEL), F32), jax.ShapeDtypeStruct((1, D_MODEL), F32)],
        compiler_params=_params(1),
    )(h2, x2, target, w_gate, w_up, w_down, g_f, g2)


def _wgrad(a, b, name):
    t, m = a.shape
    n = b.shape[1]
    bm = _pick(m, 1664)
    bn = _pick(n, 1664)
    tk = _pick(t, 2048, 16)
    nk = t // tk

    def body(a_ref, b_ref, o_ref, o16_ref):
        k = pl.program_id(2)
        part = _dot_tn(a_ref[...], b_ref[...])

        @pl.when(k == 0)
        def _():
            o_ref[...] = part

        @pl.when(k != 0)
        def _():
            o_ref[...] += part

        @pl.when(k == nk - 1)
        def _():
            o16_ref[...] = o_ref[...].astype(BF16)

    out = pl.BlockSpec((bm, bn), lambda i, j, k: (i, j))
    return pl.pallas_call(
        body, name=name, grid=(m // bm, n // bn, nk),
        in_specs=[pl.BlockSpec((tk, bm), lambda i, j, k: (k, i)), pl.BlockSpec((tk, bn), lambda i, j, k: (k, j))],
        out_specs=[out, out],
        out_shape=[jax.ShapeDtypeStruct((m, n), F32), jax.ShapeDtypeStruct((m, n), BF16)],
        compiler_params=_params(3),
    )(a, b)


def _rope_tables(seq):
    inv = 1.0 / (ROPE_THETA ** (jnp.arange(0, B_ROPE, 2, dtype=F32) / B_ROPE))
    ang = jnp.arange(seq, dtype=F32)[:, None] * inv[None, :]
    cos, sin = jnp.cos(ang), jnp.sin(ang)
    zeros = jnp.zeros((seq, 64), F32)
    return jnp.concatenate([cos, cos, zeros], axis=1), jnp.concatenate([-sin, sin, zeros], axis=1)


def _pad_weights(w_in_t, w_q_t, w_kv_b):
    w_in_p = jnp.pad(w_in_t, ((0, D_IN_PAD - D_IN), (0, 0)))
    w_q_p = jnp.pad(w_q_t.reshape(B_HEADS, B_NOPE + B_ROPE, Q_LORA), ((0, 0), (0, 64), (0, 0))).reshape(1024, Q_LORA)
    kv = w_kv_b.reshape(KV_LORA, B_HEADS, B_NOPE + B_V)
    w_kv_p = jnp.concatenate([kv[:, :, :B_NOPE].reshape(KV_LORA, 512), kv[:, :, B_NOPE:].reshape(KV_LORA, 512)], axis=1)
    return w_in_p, w_q_p, w_kv_p


def _unpad_kv(g_kv_p):
    return jnp.concatenate([g_kv_p[:, :512].reshape(KV_LORA, B_HEADS, B_NOPE),
                            g_kv_p[:, 512:].reshape(KV_LORA, B_HEADS, B_V)], axis=2).reshape(KV_LORA, 1024)


_OWN_ROWS = dict(w_in=(272, 3), w_q_b=(64, 4), w_kv_b=(256, 0), w_out=(256, 1), w_gate=(176, 4), w_up=(176, 4), w_down=(176, 4))


def _local_step(x, target, lbl, g1, g_hn, g_qa, g_kva, g_mla, g2, g_f, w_in, w_q_b, w_kv_b, late_shards):
    n_batch, seq, _ = x.shape
    t = n_batch * seq
    x = x.reshape(t, D_MODEL)
    target = target.reshape(t, D_MODEL)
    w_in_p, w_q_p, w_kv_p = _pad_weights(w_in, w_q_b, w_kv_b)
    cosx, sinx = _rope_tables(seq)

    h1, hq, hi, hff, hfb, hg, cq, ckv, kr = _in_fwd(x, g1, w_in_p)
    o_f, st_f = _hgrn_fwd(hq, hi, hff, lbl, n_batch=n_batch, direction=0)
    o_r, st_r = _hgrn_fwd(hq, hi, hfb, lbl, n_batch=n_batch, direction=1)
    q, k, v, cqn, ckvn = _mla_prep(cq, ckv, kr, g_qa, g_kva, w_q_p, w_kv_p, cosx, sinx, n_batch=n_batch)
    o_attn, lse, *gathered = _attn_fwd(q, k, v, late_shards, n_batch=n_batch)
    w_out, w_gate, w_up, w_down = (_join_shards(n, a) for n, a in zip(_LATE, gathered))
    ycat, x2, h2 = _out_fwd(o_f, o_r, hg, o_attn, x, g_hn, g_mla, w_out, g2)
    act, dgate, dup, dx3_16, dx2, dx2_16, loss, dg_f, dg2 = _ffn_fwd_bwd(h2, x2, target, w_gate, w_up, w_down, g_f, g2)
    full = dict(w_out=_wgrad(ycat, dx2_16, "wgrad_out"), w_gate=_wgrad(dgate, h2, "wgrad_gate"),
                w_up=_wgrad(dup, h2, "wgrad_up"), w_down=_wgrad(act, dx3_16, "wgrad_down"))
    do_h, dhg, do_attn16, delta, dg_hn, dg_mla = _out_bwd(dx2_16, o_f, o_r, hg, o_attn, g_hn, g_mla, w_out)
    dqt, dk, dv, *recv_late = _attn_bwd(q, k, v, do_attn16, lse, delta,
                                        [_segments(n, full[n][1]) for n in _LATE], n_batch=n_batch)
    dcq, dckv, dkr, dqp16, dkvp16, dg_qa, dg_kva = _mla_prep_bwd(dqt, dk, dv, cq, ckv, g_qa, g_kva, w_q_p, w_kv_p,
                                                                  cosx, sinx, n_batch=n_batch)
    dq_f, dv_f, dz_f, dl_f = _hgrn_bwd(hq, hi, hff, do_h, st_f, lbl, n_batch=n_batch, direction=0)
    dq_r, dv_r, dz_r, dl_r = _hgrn_bwd(hq, hi, hfb, do_h, st_r, lbl, n_batch=n_batch, direction=1)
    dx, dproj16, dg1 = _in_bwd(dq_f, dq_r, dv_f, dv_r, dz_f, dz_r, dhg, hq, dcq, dckv, dkr, dx2, x, g1, w_in_p)

    full.update(w_in=_wgrad(dproj16, h1, "wgrad_in"), w_q_b=_wgrad(dqp16, cqn, "wgrad_q_b"))
    g_kv = _unpad_kv(_wgrad(ckvn, dkvp16, "wgrad_kv_b")[0])
    early16 = dict(w_in=full["w_in"][1][:D_IN].reshape(N_CHIPS, D_IN // N_CHIPS, D_MODEL),
                   w_q_b=full["w_q_b"][1].reshape(B_HEADS, QK_PAD, Q_LORA)[:, :B_NOPE + B_ROPE],
                   w_kv_b=_segments("w_kv_b", g_kv).astype(BF16))
    small = dict(norm1_g=dg1, lb_logits=jnp.stack([dl_f, dl_r]), hgrn_norm_g=dg_hn, q_a_norm_g=dg_qa, kv_a_norm_g=dg_kva,
                 mla_norm_g=dg_mla, norm2_g=dg2, final_norm_g=dg_f, loss=loss[0:1, 0])
    return dx.reshape(n_batch, seq, D_MODEL), small, {**{n: f for n, (f, _) in full.items()}, "w_kv_b": g_kv}, early16, \
        dict(zip(_LATE, recv_late))


_HBM = pl.BlockSpec(memory_space=pltpu.HBM)
_MESH = pl.DeviceIdType.MESH


def _place():
    x, y, c = lax.axis_index("x"), lax.axis_index("y"), lax.axis_index("c")
    other_chips = [(1 - x, y), (x, 1 - y), (1 - x, 1 - y)]
    return x, y, c, other_chips


def _gather_copies(w_ref, wall_ref, send_sems, recv_sems, local_sem, base=0):
    x, y, c, chips = _place()
    mine = 2 * x + y

    def mk(j, chip_index, to):
        return pltpu.make_async_remote_copy(src_ref=w_ref, dst_ref=wall_ref.at[chip_index], send_sem=send_sems.at[base + j],
                                            recv_sem=recv_sems.at[base + j], device_id=to, device_id_type=_MESH)

    local = pltpu.make_async_copy(w_ref, wall_ref.at[mine], local_sem)
    sends = [mk(j, mine, (*chip, c)) for j, chip in enumerate(chips)]
    recvs = [mk(j, 2 * px + py, (x, y, c)) for j, (px, py) in enumerate(chips)]
    return local, sends, recvs


def _gather_start(*refs):
    local, sends, _ = _gather_copies(*refs)
    local.start()
    for cp in sends:
        cp.start()


def _gather_wait(*refs):
    local, sends, recvs = _gather_copies(*refs)
    for cp in recvs:
        cp.wait_recv()
    for cp in sends:
        cp.wait_send()
    local.wait()


def _scatter_copies(g_ref, recv_ref, send_sems, recv_sems, base=0):
    x, y, c, chips = _place()

    def mk(j, src_index, to):
        return pltpu.make_async_remote_copy(src_ref=g_ref.at[src_index], dst_ref=recv_ref.at[j],
                                            send_sem=send_sems.at[base + j], recv_sem=recv_sems.at[base + j],
                                            device_id=to, device_id_type=_MESH)

    sends = [mk(j, 2 * px + py, (px, py, c)) for j, (px, py) in enumerate(chips)]
    recvs = [mk(j, 0, (x, y, c)) for j in range(3)]
    return sends, recvs


def _scatter_start(*refs):
    for cp in _scatter_copies(*refs)[0]:
        cp.start()


def _scatter_wait(*refs):
    sends, recvs = _scatter_copies(*refs)
    for cp in recvs:
        cp.wait_recv()
    for cp in sends:
        cp.wait_send()


def _half(ref, which, axis):
    n = ref.shape[axis] // 2
    idx = [slice(None)] * len(ref.shape)
    idx[axis] = pl.ds(which * n, n)
    return ref.at[tuple(idx)]


def _split_axis(a):
    return 1 if a.shape[1] % 256 == 0 else 0


def _gather_halves(src_ref, dst_ref, send_sems, recv_sems, local_sem, base, axis):
    x, y, c, chips = _place()
    mine = 2 * x + y

    def remote(src, dst, n, to):
        return pltpu.make_async_remote_copy(src_ref=src, dst_ref=dst, send_sem=send_sems.at[base + n],
                                            recv_sem=recv_sems.at[base + n], device_id=to, device_id_type=_MESH)

    def slot(chip_index, which):
        return _half(dst_ref.at[chip_index], which, axis)

    local = pltpu.make_async_copy(src_ref, dst_ref.at[mine], local_sem)
    first = [remote(_half(src_ref, c, axis), slot(mine, c), j, (*chip, c)) for j, chip in enumerate(chips)]
    landed = [remote(_half(src_ref, c, axis), slot(2 * px + py, c), j, (x, y, c)) for j, (px, py) in enumerate(chips)]
    passed = [remote(slot(2 * px + py, c), slot(2 * px + py, c), 3 + j, (x, y, 1 - c)) for j, (px, py) in enumerate(chips)]
    handed = [remote(_half(src_ref, c, axis), slot(2 * px + py, 1 - c), 3 + j, (x, y, c)) for j, (px, py) in enumerate(chips)]

    def start():
        local.start()
        for cp in first:
            cp.start()

    def pass_on():
        for arrived, onward in zip(landed, passed):
            arrived.wait_recv()
            onward.start()

    def finish():
        for cp in handed:
            cp.wait_recv()
        for cp in first + passed:
            cp.wait_send()
        local.wait()

    return start, pass_on, finish


def _gather_shards(srcs, small):
    n = len(srcs)

    def body(*refs):
        src_refs, s_ref, dst_refs, sall_ref = refs[:n], refs[n], refs[n + 1:2 * n + 1], refs[2 * n + 1]
        send_sems, recv_sems, local_sems = refs[2 * n + 2:]
        _gather_start(s_ref, sall_ref, send_sems, recv_sems, local_sems.at[n], 6 * n)
        steps = [_gather_halves(src_refs[i], dst_refs[i], send_sems, recv_sems, local_sems.at[i], 6 * i, _split_axis(srcs[i]))
                 for i in range(n)]
        for phase in range(3):
            for step in steps:
                step[phase]()
        _gather_wait(s_ref, sall_ref, send_sems, recv_sems, local_sems.at[n], 6 * n)

    return pl.pallas_call(
        body, name="gather_shards", in_specs=[_HBM] * (n + 1), out_specs=[_HBM] * (n + 1),
        out_shape=[jax.ShapeDtypeStruct((N_CHIPS,) + a.shape, a.dtype) for a in list(srcs) + [small]],
        scratch_shapes=[pltpu.SemaphoreType.DMA((6 * n + 3,)), pltpu.SemaphoreType.DMA((6 * n + 3,)),
                        pltpu.SemaphoreType.DMA((n + 1,))],
    )(*srcs, small)


_SEM = pl.BlockSpec(memory_space=pltpu.SEMAPHORE)
_SIDE_EFFECT = pltpu.SideEffectType.DATAFLOW_SIDE_EFFECTING


def _scatter_begin(gsegs):
    n = len(gsegs)
    def hbm(a):
        return pltpu.with_memory_space_constraint(a, pltpu.HBM)

    lands = [hbm(lax.empty((3,) + g.shape[1:], g.dtype)) for g in gsegs]

    def body(*refs):
        g_refs, land_refs, send_sems, recv_sems, token = refs[:n], refs[n:2 * n], refs[2 * n], refs[2 * n + 1], refs[-1]
        for i in range(n):
            _scatter_start(g_refs[i], land_refs[i], send_sems, recv_sems, 3 * i)
        token[...] = jnp.zeros_like(token)

    out = pl.pallas_call(
        body, name="scatter_begin", in_specs=[_HBM] * (2 * n),
        out_specs=(_SEM, _SEM) + (_HBM,) * (2 * n) + (pl.BlockSpec(memory_space=pltpu.VMEM),),
        out_shape=(pltpu.SemaphoreType.DMA((3 * n,)), pltpu.SemaphoreType.DMA((3 * n,)))
        + tuple(pltpu.HBM(a.shape, a.dtype) for a in list(gsegs) + lands) + (jax.ShapeDtypeStruct((8, 128), F32),),
        input_output_aliases={i: 2 + i for i in range(2 * n)},
        compiler_params=pltpu.CompilerParams(has_side_effects=_SIDE_EFFECT),
    )(*[hbm(g) for g in gsegs], *lands)
    return out[0], out[1], out[2:2 + n], out[2 + n:2 + 2 * n], out[-1]


def _scatter_end(send_sems, recv_sems, gsegs, lands, after):
    n = len(gsegs)

    def body(*refs):
        g_refs, land_refs, ssems, rsems = refs[:n], refs[n:2 * n], refs[2 * n], refs[2 * n + 1]
        for i in range(n):
            _scatter_wait(g_refs[i], land_refs[i], ssems, rsems, 3 * i)

    out = pl.pallas_call(
        body, name="scatter_end", in_specs=[_HBM] * (2 * n) + [_SEM, _SEM] + [pl.BlockSpec(memory_space=pl.ANY)] * len(after),
        out_specs=(_HBM,) * (2 * n), out_shape=tuple(pltpu.HBM(a.shape, a.dtype) for a in list(gsegs) + list(lands)),
        input_output_aliases={i: i for i in range(2 * n)},
        compiler_params=pltpu.CompilerParams(has_side_effects=_SIDE_EFFECT),
    )(*gsegs, *lands, send_sems, recv_sems, *after)
    return out[n:]


def _swap_sibling(parts, small=None):
    n = len(parts)
    extra = small is not None

    def body(*refs):
        ins, outs = refs[:n + extra], refs[n + extra:2 * (n + extra)]
        send_sems, recv_sems, local_sem = refs[2 * (n + extra):]
        x, y, c, _ = _place()
        cps = [pltpu.make_async_remote_copy(src_ref=ins[i], dst_ref=outs[i], send_sem=send_sems.at[i],
                                            recv_sem=recv_sems.at[i], device_id=(x, y, 1 - c), device_id_type=_MESH)
               for i in range(n)]
        for cp in cps:
            cp.start()
        if extra:
            s_ref, sall_ref = ins[n], outs[n]
            me = 4 * x + 2 * y + c
            flips = [(fx, fy, fc) for fx in (0, 1) for fy in (0, 1) for fc in (0, 1)][1:]

            def peer(f):
                return tuple((1 - a) if b else a for a, b in zip((x, y, c), f))

            def sm(r, index, to):
                return pltpu.make_async_remote_copy(src_ref=s_ref, dst_ref=sall_ref.at[index], send_sem=send_sems.at[n + r],
                                                    recv_sem=recv_sems.at[n + r], device_id=to, device_id_type=_MESH)

            local = pltpu.make_async_copy(s_ref, sall_ref.at[me], local_sem)
            local.start()
            sends = [sm(r, me, peer(f)) for r, f in enumerate(flips)]
            for cp in sends:
                cp.start()
            for r, f in enumerate(flips):
                px, py, pc = peer(f)
                sm(r, 4 * px + 2 * py + pc, (x, y, c)).wait_recv()
            for cp in sends:
                cp.wait_send()
            local.wait()
        for cp in cps:
            cp.wait()

    return pl.pallas_call(
        body, name="swap_small" if extra else "swap_sibling", in_specs=[_HBM] * (n + extra), out_specs=[_HBM] * (n + extra),
        out_shape=[jax.ShapeDtypeStruct(p.shape, p.dtype) for p in parts]
        + ([jax.ShapeDtypeStruct((8,) + small.shape, F32)] if extra else []),
        scratch_shapes=[pltpu.SemaphoreType.DMA((n + 7,)), pltpu.SemaphoreType.DMA((n + 7,)), pltpu.SemaphoreType.DMA(())],
    )(*parts, *([small] if extra else []))


def _sum_segments(chip, full, recv, tm, pitch, name, after=()):
    _, rows, cols = recv.shape

    def body(chip_ref, o_ref, r_ref, *rest):
        acc = o_ref[...]
        for j in range(3):
            acc = acc + r_ref[j].astype(F32)
        rest[-1][...] = acc

    return pl.pallas_call(
        body, name=name,
        grid_spec=pltpu.PrefetchScalarGridSpec(
            num_scalar_prefetch=1, grid=(rows // tm,),
            in_specs=[pl.BlockSpec((tm, cols), lambda i, c: (c[0] * pitch + i, 0)),
                      pl.BlockSpec((3, tm, cols), lambda i, c: (0, i, 0))]
            + [pl.BlockSpec(a.shape, lambda i, c, nd=a.ndim: (0,) * nd) for a in after],
            out_specs=pl.BlockSpec((tm, cols), lambda i, c: (i, 0))),
        out_shape=jax.ShapeDtypeStruct((rows, cols), F32), compiler_params=_params(1),
    )(chip, full, recv, *after)


def _sum_devices(sall):
    def body(s_ref, o_ref):
        acc = s_ref[0]
        for d in range(1, 8):
            acc = acc + s_ref[d]
        o_ref[...] = acc

    return pl.pallas_call(body, name="sum_devices", out_shape=jax.ShapeDtypeStruct(sall.shape[1:], F32))(sall)


def _adamw(w, m, v, ga, gb, name):
    rows, cols = w.shape
    tm = _row_tile(rows)
    two = gb is not None

    def body(*refs):
        w_ref, m_ref, v_ref, ga_ref = refs[:4]
        g_ref, d_ref, m2_ref, v2_ref = refs[-4:]
        g = ga_ref[...] + refs[4][...] if two else ga_ref[...]
        m2 = ADAM_B1 * m_ref[...] + (1.0 - ADAM_B1) * g
        v2 = ADAM_B2 * v_ref[...] + (1.0 - ADAM_B2) * (g * g)
        m_hat = m2 / (1.0 - ADAM_B1 ** ADAM_STEP)
        v_hat = v2 / (1.0 - ADAM_B2 ** ADAM_STEP)
        g_ref[...] = g
        d_ref[...] = -ADAM_LR * (m_hat / (jnp.sqrt(v_hat) + ADAM_EPS) + ADAM_WD * w_ref[...])
        m2_ref[...] = m2
        v2_ref[...] = v2

    blk = _rows(tm, cols)
    args = (w, m, v, ga) + ((gb,) if two else ())
    return pl.pallas_call(
        body, name=name, grid=(rows // tm,), in_specs=[blk] * len(args), out_specs=[blk] * 4,
        out_shape=[jax.ShapeDtypeStruct(w.shape, F32)] * 4, compiler_params=_params(1),
    )(*args)


_TRANSPOSED = ("w_in", "w_q_b", "w_gate", "w_up")
_COLUMN_SHARDED = ("w_kv_b",)
_FULL_SHAPES = dict(w_in=(D_IN, D_MODEL), w_q_b=(768, Q_LORA), w_kv_b=(KV_LORA, 1024), w_out=(D_MODEL, D_MODEL),
                    w_gate=(D_FF, D_MODEL), w_up=(D_FF, D_MODEL), w_down=(D_FF, D_MODEL))
_SMALL = (("norm1_g", 1024), ("lb_logits", 2048), ("hgrn_norm_g", 512), ("q_a_norm_g", 384), ("kv_a_norm_g", 256),
          ("mla_norm_g", 512), ("norm2_g", 1024), ("final_norm_g", 1024))
_UPDATE_ROWS = 48


def _pad_rows(a, rows):
    return jnp.pad(a, ((0, rows - a.shape[0]), (0, 0)))


_EARLY = ("w_in", "w_q_b", "w_kv_b")
_LATE = ("w_out", "w_gate", "w_up", "w_down")


def _segments(name, g):
    r, c = g.shape
    if name in _COLUMN_SHARDED:
        return g.reshape(r, N_CHIPS, c // N_CHIPS).transpose(1, 0, 2)
    return g.reshape(N_CHIPS, r // N_CHIPS, c)


def _own_segment(name, g, chip):
    r, c = g.shape
    if name in _COLUMN_SHARDED:
        return lax.dynamic_slice_in_dim(g, chip * (c // N_CHIPS), c // N_CHIPS, axis=1)
    return lax.dynamic_slice_in_dim(g, chip * (r // N_CHIPS), r // N_CHIPS, axis=0)


def _join_shards(name, seg):
    r, c = _FULL_SHAPES[name]
    if name in _COLUMN_SHARDED:
        return seg.transpose(1, 0, 2).reshape(r, c)
    return seg.reshape(r, c)


def kernel(x, norm1_g, w_in, lb_logits, hgrn_norm_g, q_a_norm_g, w_q_b, kv_a_norm_g, w_kv_b, mla_norm_g, w_out, norm2_g, w_gate, w_up, w_down, final_norm_g, loss_target, m_norm1_g, m_w_in, m_lb_logits, m_hgrn_norm_g, m_q_a_norm_g, m_w_q_b, m_kv_a_norm_g, m_w_kv_b, m_mla_norm_g, m_w_out, m_norm2_g, m_w_gate, m_w_up, m_w_down, m_final_norm_g, v_norm1_g, v_w_in, v_lb_logits, v_hgrn_norm_g, v_q_a_norm_g, v_w_q_b, v_kv_a_norm_g, v_w_kv_b, v_mla_norm_g, v_w_out, v_norm2_g, v_w_gate, v_w_up, v_w_down, v_final_norm_g):
    names = ("norm1_g", "w_in", "lb_logits", "hgrn_norm_g", "q_a_norm_g", "w_q_b", "kv_a_norm_g", "w_kv_b", "mla_norm_g",
             "w_out", "norm2_g", "w_gate", "w_up", "w_down", "final_norm_g")
    w = dict(zip(names, (norm1_g, w_in, lb_logits, hgrn_norm_g, q_a_norm_g, w_q_b, kv_a_norm_g, w_kv_b, mla_norm_g,
                         w_out, norm2_g, w_gate, w_up, w_down, final_norm_g)))
    m = dict(zip(names, (m_norm1_g, m_w_in, m_lb_logits, m_hgrn_norm_g, m_q_a_norm_g, m_w_q_b, m_kv_a_norm_g, m_w_kv_b,
                         m_mla_norm_g, m_w_out, m_norm2_g, m_w_gate, m_w_up, m_w_down, m_final_norm_g)))
    v = dict(zip(names, (v_norm1_g, v_w_in, v_lb_logits, v_hgrn_norm_g, v_q_a_norm_g, v_w_q_b, v_kv_a_norm_g, v_w_kv_b,
                         v_mla_norm_g, v_w_out, v_norm2_g, v_w_gate, v_w_up, v_w_down, v_final_norm_g)))
    matrices = _EARLY + _LATE
    chip = 2 * lax.axis_index("x") + lax.axis_index("y")

    def shard2d(a, n):
        return jnp.swapaxes(a[0], 0, 1) if n in _TRANSPOSED else a[0]

    def unshard2d(a, n):
        return (jnp.swapaxes(a, 0, 1) if n in _TRANSPOSED else a)[None]

    w16 = {n: shard2d(w[n], n).astype(BF16) for n in matrices}
    lb8 = _pad_rows(lb_logits.reshape(4, 128), 8)
    *early, lball = _gather_shards([w16[n] for n in _EARLY], lb8)
    lbl = lball[:, :4].reshape(N_CHIPS, 2, 2, 128).transpose(1, 2, 0, 3).reshape(2, 2, A_WIDTH)

    grad_x, g, full, early16, recv = _local_step(
        x, loss_target, lbl, norm1_g, hgrn_norm_g, q_a_norm_g, kv_a_norm_g, mla_norm_g, norm2_g, final_norm_g[None, :],
        *(_join_shards(n, a) for n, a in zip(_EARLY, early)), [w16[n] for n in _LATE])

    n_small = sum(size for _, size in _SMALL)
    small = jnp.concatenate([g[n].reshape(-1) for n, _ in _SMALL] + [g["loss"], jnp.zeros((SMALL_ROWS * 128 - n_small - 1,), F32)])
    chip1 = chip.reshape(1).astype(jnp.int32)
    out, raw = {}, {}

    def update(n, part, sib):
        raw[n] = _adamw(shard2d(w[n], n), shard2d(m[n], n), shard2d(v[n], n), part, sib, f"adamw_{n}")
        out[n] = tuple(unshard2d(r, n) for r in raw[n])

    send_sems, recv_sems, segs, lands, token = _scatter_begin([early16[n] for n in _EARLY])
    parts = [_sum_segments(chip1, full[n], recv[n], *_OWN_ROWS[n], f"sum_{n}", after=(token,)) for n in _LATE]
    *sibs, small_all = _swap_sibling(parts, small.reshape(SMALL_ROWS, 128))
    for n, part, sib in zip(_LATE, parts, sibs):
        update(n, part, sib)
    recv_early = _scatter_end(send_sems, recv_sems, segs, lands, after=[raw[n][1] for n in _LATE])
    full["w_kv_b"] = _own_segment("w_kv_b", full["w_kv_b"], chip)
    parts = [_sum_segments(chip1, full[n], r, *_OWN_ROWS[n], f"sum_{n}") for n, r in zip(_EARLY, recv_early)]
    for n, part, sib in zip(_EARLY, parts, _swap_sibling(parts)):
        update(n, part, sib)
    small_sum = _sum_devices(small_all).reshape(-1)
    loss = small_sum[n_small]

    small_g, off = {}, 0
    for n, size in _SMALL:
        small_g[n] = small_sum[off:off + size]
        off += size
    small_g["lb_logits"] = lax.dynamic_slice_in_dim(small_g["lb_logits"].reshape(2, 2, A_WIDTH), chip * 128, 128, axis=2)
    small_names = tuple(n for n, _ in _SMALL)

    def pack(d):
        return _pad_rows(jnp.concatenate([d[n].reshape(-1) for n in small_names]).reshape(-1, 128), _UPDATE_ROWS)

    res = _adamw(pack(w), pack(m), pack(v), pack(small_g), None, "adamw_small")
    off = 0
    flat = [r.reshape(-1) for r in res]
    for n in small_names:
        size = w[n].size
        out[n] = tuple(f[off:off + size].reshape(w[n].shape) for f in flat)
        off += size

    return (loss, grad_x) + tuple(out[n][i] for i in range(4) for n in names)
```

```python
import functools

import jax
import jax.numpy as jnp
from jax import lax
from jax.experimental import pallas as pl
from jax.experimental.pallas import tpu as pltpu

F32 = jnp.float32
BF16 = jnp.bfloat16

D_MODEL = 1024
A_WIDTH = 512
HEAD_PAIRS = 4
CHUNK = 64
B_HEADS = 4
B_NOPE = 128
B_ROPE = 64
B_V = 128
QK_PAD = 256
Q_LORA = 384
KV_LORA = 256
D_FF = 2816
D_IN = 3264
D_IN_PAD = 3328
IN_WIDTHS = (512, 512, 512, 512, 512, Q_LORA, KV_LORA, 128)
ROPE_THETA = 10000.0
EPS = 1e-6
ATTN_SCALE = (B_NOPE + B_ROPE) ** -0.5
LOG2E = 1.4426950408889634
SCALE_LOG2E = ATTN_SCALE * LOG2E

ADAM_LR = 0.001
ADAM_B1 = 0.9
ADAM_B2 = 0.999
ADAM_EPS = 1e-08
ADAM_WD = 0.01
ADAM_STEP = 10

VMEM_LIMIT_BYTES = 60 * 1024 * 1024
N_CHIPS = 4
SMALL_ROWS = 56


def _params(n_axes):
    return pltpu.CompilerParams(dimension_semantics=("arbitrary",) * n_axes,
                                vmem_limit_bytes=VMEM_LIMIT_BYTES)


def _dot(a, b):
    return jnp.dot(a, b, preferred_element_type=F32)


def _dot_nt(a, b):
    return lax.dot_general(a, b, (((1,), (1,)), ((), ())), preferred_element_type=F32)


def _dot_tn(a, b):
    return lax.dot_general(a, b, (((0,), (0,)), ((), ())), preferred_element_type=F32)


def _split3(x):
    x1 = x.astype(BF16)
    r = x - x1.astype(F32)
    x2 = r.astype(BF16)
    x3 = (r - x2.astype(F32)).astype(BF16)
    return x1, x2, x3


def _exact_left(m16, x):
    x1, x2, x3 = _split3(x)
    return _dot(m16, x1) + _dot(m16, x2) + _dot(m16, x3)


def _exact_right(x, m16):
    x1, x2, x3 = _split3(x)
    return _dot(x1, m16) + _dot(x2, m16) + _dot(x3, m16)


def _iota2(shape, dim):
    return lax.broadcasted_iota(jnp.int32, shape, dim)


def _sigmoid(x):
    return jax.nn.sigmoid(x)


def _pick(dim, cap, mult=128):
    if dim <= cap:
        return dim
    best = None
    for d in range(mult, cap + 1, mult):
        if dim % d == 0:
            best = d
    assert best is not None, (dim, cap, mult)
    return best


def _row_tile(t, cap=256):
    return _pick(t, cap, 8)


ROW_GROUP = 256


def _row_groups(tm):
    size = min(tm, ROW_GROUP)
    return [slice(r, r + size) for r in range(0, tm, size)]


def _full(shape, single=False):
    if single:
        return pl.BlockSpec(shape, lambda *_: (0,) * len(shape), pipeline_mode=pl.Buffered(1))
    return pl.BlockSpec(shape, lambda *_: (0,) * len(shape))


def _rows(tm, width):
    return pl.BlockSpec((tm, width), lambda i: (i, 0))


def _acc_rows(ref, val, first):
    s = jnp.sum(val, axis=0, keepdims=True)

    @pl.when(first)
    def _():
        ref[...] = s

    @pl.when(jnp.logical_not(first))
    def _():
        ref[...] += s


def _in_fwd(x, g1, w_in_p):
    t = x.shape[0]
    tm = _row_tile(t, 2 * ROW_GROUP)

    def body(x_ref, g_ref, w_ref, h_ref, *outs):
        gain = g_ref[...]

        def group(xv):
            r = lax.rsqrt(jnp.mean(xv * xv, axis=-1, keepdims=True) + EPS)
            h = ((xv * r) * gain).astype(BF16)
            proj, off = [], 0
            for w in IN_WIDTHS:
                proj.append(_dot_nt(h, w_ref[off:off + w, :]))
                off += w
            return h, proj

        groups = _row_groups(tm)
        done = [group(xv) for xv in [x_ref[g, :] for g in groups]]
        for g, (h, proj) in zip(groups, done):
            h_ref[g, :] = h
            for o_ref, p in zip(outs, proj):
                o_ref[g, :] = p

    return pl.pallas_call(
        body, name="in_fwd", grid=(t // tm,),
        in_specs=[_rows(tm, D_MODEL), _full((1, D_MODEL)), _full((D_IN_PAD, D_MODEL))],
        out_specs=[_rows(tm, D_MODEL)] + [_rows(tm, w) for w in IN_WIDTHS],
        out_shape=[jax.ShapeDtypeStruct((t, D_MODEL), BF16)]
        + [jax.ShapeDtypeStruct((t, w), F32) for w in IN_WIDTHS],
        compiler_params=_params(1),
    )(x, g1, w_in_p)


def _in_bwd(dq_f, dq_r, dv_f, dv_r, dz_f, dz_r, dhg, hq, dcq, dckv, dkr, dx2, x, g1, w_in_p):
    t = x.shape[0]
    tm = _row_tile(t, 2 * ROW_GROUP)

    def body(dqf_ref, dqr_ref, dvf_ref, dvr_ref, dzf_ref, dzr_ref, dhg_ref, hq_ref, dcq_ref, dckv_ref,
             dkr_ref, dx2_ref, x_ref, g_ref, w_ref, dx_ref, dp_ref, dg_ref):
        gain = g_ref[...]

        def group(dqf, dqr, dvf, dvr, dzf, dzr, dhg, hqv, dcq, dckv, dkr, dx2, xv):
            sg = _sigmoid(hqv)
            dhq = (dqf + dqr) * (sg * (1.0 + hqv * (1.0 - sg)))
            pieces = [p.astype(BF16) for p in (dhq, dvf + dvr, dzf, dzr, dhg, dcq, dckv, dkr)]
            dh, off = None, 0
            for p16, w in zip(pieces, IN_WIDTHS):
                part = _dot(p16, w_ref[off:off + w, :])
                dh = part if dh is None else dh + part
                off += w
            r = lax.rsqrt(jnp.mean(xv * xv, axis=-1, keepdims=True) + EPS)
            xh = xv * r
            dxh = dh * gain
            return dx2 + r * (dxh - xh * jnp.mean(dxh * xh, axis=-1, keepdims=True)), pieces, dh * xh

        ins = (dqf_ref, dqr_ref, dvf_ref, dvr_ref, dzf_ref, dzr_ref, dhg_ref, hq_ref, dcq_ref, dckv_ref, dkr_ref,
               dx2_ref, x_ref)
        groups = _row_groups(tm)
        done = [group(*vals) for vals in [[ref[g, :] for ref in ins] for g in groups]]
        for g, (dx, pieces, _) in zip(groups, done):
            dx_ref[g, :] = dx
            off = 0
            for p16, w in zip(pieces, IN_WIDTHS):
                dp_ref[g, off:off + w] = p16
                off += w
        _acc_rows(dg_ref, jnp.concatenate([d[2] for d in done], axis=0), pl.program_id(0) == 0)

    a512 = _rows(tm, A_WIDTH)
    return pl.pallas_call(
        body, name="in_bwd", grid=(t // tm,),
        in_specs=[a512] * 8 + [_rows(tm, Q_LORA), _rows(tm, KV_LORA), _rows(tm, 128), _rows(tm, D_MODEL),
                               _rows(tm, D_MODEL), _full((1, D_MODEL)), _full((D_IN_PAD, D_MODEL))],
        out_specs=[_rows(tm, D_MODEL), _rows(tm, D_IN_PAD), _full((1, D_MODEL))],
        out_shape=[jax.ShapeDtypeStruct((t, D_MODEL), F32), jax.ShapeDtypeStruct((t, D_IN_PAD), BF16),
                   jax.ShapeDtypeStruct((1, D_MODEL), F32)],
        compiler_params=_params(1),
    )(dq_f, dq_r, dv_f, dv_r, dz_f, dz_r, dhg, hq, dcq, dckv, dkr, dx2, x, g1, w_in_p)


def _lower_bound(lbl_ref, direction):
    l0 = lbl_ref[direction, 0:1, :]
    l1 = lbl_ref[direction, 1:2, :]
    m = jnp.maximum(l0, l1)
    e0 = jnp.exp(l0 - m)
    e1 = jnp.exp(l1 - m)
    return e0 / (e0 + e1)


def _hgrn_consts(rb, reverse):
    row = _iota2((rb, rb), 0)
    col = _iota2((rb, rb), 1)
    same = (row // CHUNK) == (col // CHUNK)
    tri = jnp.logical_and(same, (col >= row) if reverse else (col <= row))
    tri_t = jnp.logical_and(same, (col <= row) if reverse else (col >= row))
    r128 = _iota2((128, 128), 0)
    c128 = _iota2((128, 128), 1)
    bd = (r128 < 64) == (c128 < 64)
    lane = _iota2((1, 128), 1)
    m0 = (lane < 64).astype(F32)
    return tri, tri_t, bd, (m0, 1.0 - m0)


def _per_chunk(x, fn):
    n = x.shape[0] // CHUNK
    return jnp.concatenate([jnp.broadcast_to(fn(x[c * CHUNK:(c + 1) * CHUNK]), (CHUNK, x.shape[1])) for c in range(n)],
                           axis=0)


def _chunk_cumsum(x, reverse):
    rb = x.shape[0]
    pos = _iota2(x.shape, 0) % CHUNK
    step = 1
    while step < CHUNK:
        if reverse:
            x = x + jnp.where(pos < CHUNK - step, pltpu.roll(x, rb - step, 0), 0.0)
        else:
            x = x + jnp.where(pos >= step, pltpu.roll(x, step, 0), 0.0)
        step *= 2
    return x


def _hgrn_block(z, hqv, lb, reverse):
    sig = _sigmoid(z)
    sn = _sigmoid(-z)
    q = hqv * _sigmoid(hqv)
    f = lb + (1.0 - lb) * sig
    k = (1.0 - lb) * sn
    lf = jnp.log(f)
    cum = _chunk_cumsum(lf, reverse)
    last = _per_chunk(cum, (lambda a: a[0:1]) if reverse else (lambda a: a[CHUNK - 1:CHUNK]))
    e_neg = jnp.exp(-cum)
    e_end = jnp.exp(last - cum)
    a = jnp.exp(cum)
    return dict(sig=sig, sn=sn, q=q, f=f, k=k, a=a, e_neg=e_neg, e_end=e_end,
                q_dec=q * a, k_inv=k * e_neg, k_end=k * e_end, d=jnp.exp(last))


def _hgrn_dims(t, n_batch):
    s = t // n_batch
    rb = _pick(s, 256, CHUNK)
    return s, rb, s // rb, rb // CHUNK


def _hgrn_fwd(hq, hi, hf, lbl, *, n_batch, direction):
    t = hq.shape[0]
    reverse = direction == 1
    s, rb, nb, nc = _hgrn_dims(t, n_batch)

    def tmap(b, j):
        return (b * nb + ((nb - 1 - j) if reverse else j), 0)

    def smap(b, j):
        return (b * nb + ((nb - 1 - j) if reverse else j), 0, 0, 0)

    def body(hq_ref, hi_ref, hf_ref, lbl_ref, o_ref, st_ref, st_scr):
        @pl.when(pl.program_id(1) == 0)
        def _():
            st_scr[...] = jnp.zeros_like(st_scr)

        lb_all = _lower_bound(lbl_ref, direction)
        tri, _, bd, masks = _hgrn_consts(rb, reverse)
        order = range(nc - 1, -1, -1) if reverse else range(nc)

        pairs = [slice(p * 128, (p + 1) * 128) for p in range(HEAD_PAIRS)]
        chunks = [slice(c * CHUNK, (c + 1) * CHUNK) for c in range(nc)]
        w = _hgrn_block(hf_ref[...], hq_ref[...], lb_all, reverse)
        v16 = hi_ref[...].astype(BF16)
        qd16 = w["q_dec"].astype(BF16)
        ki16 = w["k_inv"].astype(BF16)
        ke16 = w["k_end"].astype(BF16)
        sc = [[_dot_nt(jnp.where(mh > 0.0, qd16[:, ls], 0.0).astype(BF16), ki16[:, ls]) for mh in masks] for ls in pairs]
        pv = [[_dot(jnp.where(tri, s_e, 0.0).astype(BF16), v16[:, ls]) for s_e in sc_p] for sc_p, ls in zip(sc, pairs)]
        o_intra = [jnp.where(masks[0] > 0.0, pv_p[0], pv_p[1]) for pv_p in pv]
        ut = [[jnp.where(bd, _dot_tn(v16[rs, ls], ke16[rs, ls]), 0.0) for ls in pairs] for rs in chunks]
        st = [st_scr[p] for p in range(HEAD_PAIRS)]
        for c in order:
            rs = chunks[c]
            inter = [_dot_nt(qd16[rs, ls], st[p].astype(BF16)) for p, ls in enumerate(pairs)]
            for p, ls in enumerate(pairs):
                o_ref[rs, ls] = o_intra[p][rs] + inter[p]
                st_ref[c, p] = st[p]
                st[p] = st[p] * w["d"][c * CHUNK:c * CHUNK + 1, ls] + ut[c][p]
        for p in range(HEAD_PAIRS):
            st_scr[p] = st[p]

    blk = pl.BlockSpec((rb, A_WIDTH), tmap)
    return pl.pallas_call(
        body, name=f"hgrn_fwd_{direction}", grid=(n_batch, nb),
        in_specs=[blk, blk, blk, _full((2, 2, A_WIDTH))],
        out_specs=[blk, pl.BlockSpec((nc, HEAD_PAIRS, 128, 128), smap)],
        out_shape=[jax.ShapeDtypeStruct((t, A_WIDTH), F32),
                   jax.ShapeDtypeStruct((t // CHUNK, HEAD_PAIRS, 128, 128), F32)],
        scratch_shapes=[pltpu.VMEM((HEAD_PAIRS, 128, 128), F32)],
        compiler_params=_params(2),
    )(hq, hi, hf, lbl)


def _hgrn_bwd(hq, hi, hf, do, st, lbl, *, n_batch, direction):
    t = hq.shape[0]
    reverse = direction == 1
    s, rb, nb, nc = _hgrn_dims(t, n_batch)

    def tmap(b, j):
        return (b * nb + (j if reverse else (nb - 1 - j)), 0)

    def smap(b, j):
        return (b * nb + (j if reverse else (nb - 1 - j)), 0, 0, 0)

    def body(hq_ref, hi_ref, hf_ref, do_ref, st_ref, lbl_ref, dq_ref, dv_ref, dz_ref, dl_ref, g_scr, dlb_scr):
        b = pl.program_id(0)
        j = pl.program_id(1)

        @pl.when(jnp.logical_and(b == 0, j == 0))
        def _():
            dlb_scr[...] = jnp.zeros_like(dlb_scr)

        @pl.when(j == 0)
        def _():
            g_scr[...] = jnp.zeros_like(g_scr)

        lb_all = _lower_bound(lbl_ref, direction)
        tri, tri_t, bd, masks = _hgrn_consts(rb, reverse)
        order = range(nc) if reverse else range(nc - 1, -1, -1)

        pairs = [slice(p * 128, (p + 1) * 128) for p in range(HEAD_PAIRS)]
        chunks = [slice(c * CHUNK, (c + 1) * CHUNK) for c in range(nc)]

        def lanes(per_pair):
            return jnp.concatenate(per_pair, axis=1)

        w = _hgrn_block(hf_ref[...], hq_ref[...], lb_all, reverse)
        dov = do_ref[...]
        v16 = hi_ref[...].astype(BF16)
        do16 = dov.astype(BF16)
        qd16 = w["q_dec"].astype(BF16)
        ki16 = w["k_inv"].astype(BF16)
        ke16 = w["k_end"].astype(BF16)
        qm16 = [[jnp.where(mh > 0.0, qd16[:, ls], 0.0).astype(BF16) for mh in masks] for ls in pairs]
        dom16 = [[jnp.where(mh > 0.0, do16[:, ls], 0.0).astype(BF16) for mh in masks] for ls in pairs]
        heads = [(p, e) for p in range(HEAD_PAIRS) for e in range(2)]
        dp = {(p, e): _dot_nt(dom16[p][e], v16[:, pairs[p]]) for p, e in heads}
        pm_t = {(p, e): _dot_nt(ki16[:, pairs[p]], qm16[p][e]) for p, e in heads}
        dp_t = {(p, e): _dot_nt(v16[:, pairs[p]], dom16[p][e]) for p, e in heads}
        dp = {h: jnp.where(tri, a, 0.0).astype(BF16) for h, a in dp.items()}
        pm_t = {h: jnp.where(tri_t, a, 0.0).astype(BF16) for h, a in pm_t.items()}
        dp_t = {h: jnp.where(tri_t, a, 0.0).astype(BF16) for h, a in dp_t.items()}
        dv_e = {(p, e): _dot(pm_t[p, e], do16[:, pairs[p]]) for p, e in heads}
        dq_e = {(p, e): _dot(dp[p, e], ki16[:, pairs[p]]) for p, e in heads}
        dk_e = {(p, e): _dot(dp_t[p, e], qd16[:, pairs[p]]) for p, e in heads}
        st = [[st_ref[c, p] for p in range(HEAD_PAIRS)] for c in range(nc)]
        dq_x = [[_dot(do16[rs, ls], st[c][p].astype(BF16)) for p, ls in enumerate(pairs)] for c, rs in enumerate(chunks)]
        gq = [[jnp.where(bd, _dot_tn(do16[rs, ls], qd16[rs, ls]), 0.0) for ls in pairs] for rs in chunks]
        g = [g_scr[p] for p in range(HEAD_PAIRS)]
        dk_end, dv_x, dd = [None] * nc, [None] * nc, [None] * nc
        for c in order:
            rs = chunks[c]
            g16 = [a.astype(BF16) for a in g]
            dk_end[c] = lanes([_dot(v16[rs, ls], g16[p]) for p, ls in enumerate(pairs)])
            dv_x[c] = lanes([_dot_nt(ke16[rs, ls], g16[p]) for p, ls in enumerate(pairs)])
            dd[c] = jnp.broadcast_to(lanes([jnp.sum(g[p] * st[c][p], axis=0, keepdims=True) for p in range(HEAD_PAIRS)]),
                                     (CHUNK, A_WIDTH))
            for p, ls in enumerate(pairs):
                g[p] = g[p] * w["d"][c * CHUNK:c * CHUNK + 1, ls] + gq[c][p]
        for p in range(HEAD_PAIRS):
            g_scr[p] = g[p]

        def both_heads(d):
            return lanes([jnp.where(masks[0] > 0.0, d[p, 0], d[p, 1]) for p in range(HEAD_PAIRS)])

        dq_dec = both_heads(dq_e) + jnp.concatenate([lanes(a) for a in dq_x], axis=0)
        dk_inv = both_heads(dk_e)
        dk_end = jnp.concatenate(dk_end, axis=0)
        dv = both_heads(dv_e) + jnp.concatenate(dv_x, axis=0)
        dd = jnp.concatenate(dd, axis=0)
        dke = dk_end * w["k_end"]
        dcum = dq_dec * w["q_dec"] - dk_inv * w["k_inv"] - dke
        dk = dk_inv * w["e_neg"] + dk_end * w["e_end"]
        dlast = _per_chunk(dke, lambda a: jnp.sum(a, axis=0, keepdims=True)) + dd * w["d"]
        dlf = _chunk_cumsum(dcum, not reverse) + dlast
        tt = dlf / w["f"] - dk
        dq_ref[...] = dq_dec * w["a"]
        dv_ref[...] = dv
        dz_ref[...] = ((1.0 - lb_all) * w["sig"] * w["sn"] * tt).astype(BF16)
        dlb_scr[...] += jnp.sum(w["sn"] * tt, axis=0, keepdims=True)

        @pl.when(jnp.logical_and(b == pl.num_programs(0) - 1, j == pl.num_programs(1) - 1))
        def _():
            d0 = dlb_scr[...] * lb_all * (1.0 - lb_all)
            dl_ref[0:1, :] = d0
            dl_ref[1:2, :] = -d0

    blk = pl.BlockSpec((rb, A_WIDTH), tmap)
    return pl.pallas_call(
        body, name=f"hgrn_bwd_{direction}", grid=(n_batch, nb),
        in_specs=[blk, blk, blk, blk, pl.BlockSpec((nc, HEAD_PAIRS, 128, 128), smap), _full((2, 2, A_WIDTH))],
        out_specs=[blk, blk, blk, _full((2, A_WIDTH))],
        out_shape=[jax.ShapeDtypeStruct((t, A_WIDTH), F32)] * 2
        + [jax.ShapeDtypeStruct((t, A_WIDTH), BF16), jax.ShapeDtypeStruct((2, A_WIDTH), F32)],
        scratch_shapes=[pltpu.VMEM((HEAD_PAIRS, 128, 128), F32), pltpu.VMEM((1, A_WIDTH), F32)],
        compiler_params=_params(2),
    )(hq, hi, hf, do, st, lbl)


def _swap_rope_halves(x):
    lane = _iota2(x.shape, 1)
    return jnp.where(lane < 32, pltpu.roll(x, 96, 1), pltpu.roll(x, 32, 1))


def _rms_fwd(xv, g):
    r = lax.rsqrt(jnp.mean(xv * xv, axis=-1, keepdims=True) + EPS)
    return (xv * r) * g


def _rms_bwd(dy, xv, g):
    r = lax.rsqrt(jnp.mean(xv * xv, axis=-1, keepdims=True) + EPS)
    xh = xv * r
    dxh = dy * g
    return r * (dxh - xh * jnp.mean(dxh * xh, axis=-1, keepdims=True)), dy * xh


def _mla_prep(cq, ckv, kr, g_qa, g_kva, w_q_p, w_kv_p, cosx, sinx, *, n_batch):
    t = cq.shape[0]
    s = t // n_batch
    tm = _row_tile(s)
    nt = s // tm

    def body(cq_ref, ckv_ref, kr_ref, gq_ref, gkv_ref, wq_ref, wkv_ref, cos_ref, sin_ref,
             q_ref, k_ref, v_ref, cqn_ref, ckvn_ref):
        cos, sin = cos_ref[...], sin_ref[...]
        cqn = _rms_fwd(cq_ref[...], gq_ref[...]).astype(BF16)
        ckvn = _rms_fwd(ckv_ref[...], gkv_ref[...]).astype(BF16)
        cqn_ref[...] = cqn
        ckvn_ref[...] = ckvn
        krv = kr_ref[...]
        kr_roped = (krv * cos + _swap_rope_halves(krv) * sin).astype(BF16)
        for h in range(B_HEADS):
            o = h * QK_PAD
            q_ref[:, o:o + 128] = _dot_nt(cqn, wq_ref[o:o + 128, :]).astype(BF16)
            qr = _dot_nt(cqn, wq_ref[o + 128:o + 256, :])
            q_ref[:, o + 128:o + 256] = (qr * cos + _swap_rope_halves(qr) * sin).astype(BF16)
            k_ref[:, o:o + 128] = _dot(ckvn, wkv_ref[:, h * 128:(h + 1) * 128]).astype(BF16)
            k_ref[:, o + 128:o + 256] = kr_roped
        v_ref[...] = _dot(ckvn, wkv_ref[:, 512:1024]).astype(BF16)

    tab = pl.BlockSpec((tm, 128), lambda i: (i % nt, 0))
    return pl.pallas_call(
        body, name="mla_prep", grid=(t // tm,),
        in_specs=[_rows(tm, Q_LORA), _rows(tm, KV_LORA), _rows(tm, 128), _full((1, Q_LORA)), _full((1, KV_LORA)),
                  _full((1024, Q_LORA)), _full((KV_LORA, 1024)), tab, tab],
        out_specs=[_rows(tm, 1024), _rows(tm, 1024), _rows(tm, 512), _rows(tm, Q_LORA), _rows(tm, KV_LORA)],
        out_shape=[jax.ShapeDtypeStruct((t, 1024), BF16), jax.ShapeDtypeStruct((t, 1024), BF16),
                   jax.ShapeDtypeStruct((t, 512), BF16), jax.ShapeDtypeStruct((t, Q_LORA), BF16),
                   jax.ShapeDtypeStruct((t, KV_LORA), BF16)],
        compiler_params=_params(1),
    )(cq, ckv, kr, g_qa, g_kva, w_q_p, w_kv_p, cosx, sinx)


def _mla_prep_bwd(dqt, dk, dv, cq, ckv, g_qa, g_kva, w_q_p, w_kv_p, cosx, sinx, *, n_batch):
    t = cq.shape[0]
    s = t // n_batch
    tm = _row_tile(s)
    nt = s // tm

    def body(dqt_ref, dk_ref, dv_ref, cq_ref, ckv_ref, gq_ref, gkv_ref, wq_ref, wkv_ref, cos_ref, sin_ref,
             dcq_ref, dckv_ref, dkr_ref, dqp_ref, dkvp_ref, dgq_ref, dgkv_ref):
        cos, sin = cos_ref[...], sin_ref[...]
        first = pl.program_id(0) == 0

        def unrope(d):
            return d * cos + _swap_rope_halves(d * sin)

        dcqn = None
        dkr = None
        dckvn = None
        for h in range(B_HEADS):
            o = h * QK_PAD
            dq_h = jnp.transpose(dqt_ref[o:o + QK_PAD, :])
            dqn16 = dq_h[:, 0:128].astype(BF16)
            dqr16 = unrope(dq_h[:, 128:256]).astype(BF16)
            dqp_ref[:, o:o + 128] = dqn16
            dqp_ref[:, o + 128:o + 256] = dqr16
            part = _dot(dqn16, wq_ref[o:o + 128, :]) + _dot(dqr16, wq_ref[o + 128:o + 256, :])
            dcqn = part if dcqn is None else dcqn + part
            dkn16 = dk_ref[:, o:o + 128].astype(BF16)
            dkvp_ref[:, h * 128:(h + 1) * 128] = dkn16
            part = _dot_nt(dkn16, wkv_ref[:, h * 128:(h + 1) * 128])
            dckvn = part if dckvn is None else dckvn + part
            kr_part = dk_ref[:, o + 128:o + 256]
            dkr = kr_part if dkr is None else dkr + kr_part
        dv16 = dv_ref[...].astype(BF16)
        dkvp_ref[:, 512:1024] = dv16
        dckvn = dckvn + _dot_nt(dv16, wkv_ref[:, 512:1024])
        dkr_ref[...] = unrope(dkr).astype(BF16)
        dcq, dgq = _rms_bwd(dcqn, cq_ref[...], gq_ref[...])
        dckv, dgkv = _rms_bwd(dckvn, ckv_ref[...], gkv_ref[...])
        dcq_ref[...] = dcq.astype(BF16)
        dckv_ref[...] = dckv.astype(BF16)
        _acc_rows(dgq_ref, dgq, first)
        _acc_rows(dgkv_ref, dgkv, first)

    tab = pl.BlockSpec((tm, 128), lambda i: (i % nt, 0))
    return pl.pallas_call(
        body, name="mla_prep_bwd", grid=(t // tm,),
        in_specs=[pl.BlockSpec((1024, tm), lambda i: (0, i)), _rows(tm, 1024), _rows(tm, 512), _rows(tm, Q_LORA),
                  _rows(tm, KV_LORA),
                  _full((1, Q_LORA)), _full((1, KV_LORA)), _full((1024, Q_LORA)), _full((KV_LORA, 1024)), tab, tab],
        out_specs=[_rows(tm, Q_LORA), _rows(tm, KV_LORA), _rows(tm, 128), _rows(tm, 1024), _rows(tm, 1024),
                   _full((1, Q_LORA)), _full((1, KV_LORA))],
        out_shape=[jax.ShapeDtypeStruct((t, Q_LORA), BF16), jax.ShapeDtypeStruct((t, KV_LORA), BF16),
                   jax.ShapeDtypeStruct((t, 128), BF16), jax.ShapeDtypeStruct((t, 1024), BF16),
                   jax.ShapeDtypeStruct((t, 1024), BF16), jax.ShapeDtypeStruct((1, Q_LORA), F32),
                   jax.ShapeDtypeStruct((1, KV_LORA), F32)],
        compiler_params=_params(1),
    )(dqt, dk, dv, cq, ckv, g_qa, g_kva, w_q_p, w_kv_p, cosx, sinx)


def _attn_dims(t, n_batch):
    s = t // n_batch
    tq = _pick(s, 1024, 128)
    return s, tq, s // tq


ATTN_ROWS = 256


def _grid_ends(n_axes):
    ids = [pl.program_id(a) for a in range(n_axes)]
    first = functools.reduce(jnp.logical_and, [i == 0 for i in ids])
    last = functools.reduce(jnp.logical_and, [i == pl.num_programs(a) - 1 for a, i in enumerate(ids)])
    return first, last


def _attn_fwd(q, k, v, wsrcs, *, n_batch):
    t = q.shape[0]
    s, tq, nq = _attn_dims(t, n_batch)
    nw = len(wsrcs)

    def body(q_ref, k_ref, v_ref, *refs):
        w_refs, (o_ref, lse_ref), wall_refs = refs[:nw], refs[nw:nw + 2], refs[nw + 2:2 * nw + 2]
        send_sems, recv_sems, local_sems = refs[2 * nw + 2:]
        first, last = _grid_ends(3)

        @pl.when(first)
        def _():
            for i in range(nw):
                _gather_start(w_refs[i], wall_refs[i], send_sems, recv_sems, local_sems.at[i], 3 * i)

        @pl.when(last)
        def _():
            for i in range(nw):
                _gather_wait(w_refs[i], wall_refs[i], send_sems, recv_sems, local_sems.at[i], 3 * i)

        kv, vv = k_ref[...], v_ref[...]
        groups = [slice(r, r + ATTN_ROWS) for r in range(0, tq, ATTN_ROWS)]
        raw = [_dot_nt(q_ref[g, :], kv) for g in groups]
        m = [jnp.max(a, axis=-1, keepdims=True) for a in raw]
        p = [jnp.exp2((a - b) * SCALE_LOG2E) for a, b in zip(raw, m)]
        l = [jnp.sum(a, axis=-1, keepdims=True) for a in p]
        for g, pg, mg, lg in zip(groups, p, m, l):
            o_ref[g, :] = _dot(pg.astype(BF16), vv) / lg
            lse2 = mg * SCALE_LOG2E + jnp.log(lg) * LOG2E
            lse_ref[:, g] = jnp.transpose(jnp.broadcast_to(lse2, (ATTN_ROWS, 128)))[0:1, :]

    return pl.pallas_call(
        body, name="attn_fwd", grid=(n_batch, B_HEADS, nq),
        in_specs=[pl.BlockSpec((tq, QK_PAD), lambda b, h, i: (b * nq + i, h)),
                  pl.BlockSpec((s, QK_PAD), lambda b, h, i: (b, h)),
                  pl.BlockSpec((s, B_V), lambda b, h, i: (b, h))] + [_HBM] * nw,
        out_specs=[pl.BlockSpec((tq, B_V), lambda b, h, i: (b * nq + i, h)),
                   pl.BlockSpec((None, 1, tq), lambda b, h, i: (h, 0, b * nq + i))] + [_HBM] * nw,
        out_shape=[jax.ShapeDtypeStruct((t, B_HEADS * B_V), F32), jax.ShapeDtypeStruct((B_HEADS, 1, t), F32)]
        + [jax.ShapeDtypeStruct((N_CHIPS,) + w.shape, w.dtype) for w in wsrcs],
        scratch_shapes=[pltpu.SemaphoreType.DMA((3 * nw,)), pltpu.SemaphoreType.DMA((3 * nw,)),
                        pltpu.SemaphoreType.DMA((nw,))],
        compiler_params=_params(3),
    )(q, k, v, *wsrcs)


def _attn_bwd(q, k, v, do16, lse, delta, gsegs, *, n_batch):
    t = q.shape[0]
    s = t // n_batch
    tk = _pick(s, 512, 128)
    nk = s // tk
    ng = len(gsegs)

    def body(q_ref, k_ref, v_ref, do_ref, lse_ref, dl_ref, *refs):
        g_refs, (dqt_ref, dk_ref, dv_ref), recv_refs = refs[:ng], refs[ng:ng + 3], refs[ng + 3:2 * ng + 3]
        send_sems, recv_sems = refs[2 * ng + 3:]
        first, last = _grid_ends(3)

        @pl.when(first)
        def _():
            for i in range(ng):
                _scatter_start(g_refs[i], recv_refs[i], send_sems, recv_sems, 3 * i)

        @pl.when(last)
        def _():
            for i in range(ng):
                _scatter_wait(g_refs[i], recv_refs[i], send_sems, recv_sems, 3 * i)

        j = pl.program_id(2)
        qv, kv, dov = q_ref[...], k_ref[...], do_ref[...]
        pt = jnp.exp2(_dot_nt(kv, qv) * SCALE_LOG2E - lse_ref[...])
        dv_ref[...] = _dot(pt.astype(BF16), dov)
        dpt = _dot_nt(v_ref[...], dov)
        dst = (pt * (dpt - dl_ref[...])).astype(BF16)
        dk_ref[...] = _dot(dst, qv) * ATTN_SCALE
        part = _dot_tn(kv, dst)

        @pl.when(j == 0)
        def _():
            dqt_ref[...] = part

        @pl.when(j != 0)
        def _():
            dqt_ref[...] += part

        @pl.when(j == nk - 1)
        def _():
            dqt_ref[...] = dqt_ref[...] * ATTN_SCALE

    row = pl.BlockSpec((None, 1, s), lambda b, h, j: (h, 0, b))
    return pl.pallas_call(
        body, name="attn_bwd", grid=(n_batch, B_HEADS, nk),
        in_specs=[pl.BlockSpec((s, QK_PAD), lambda b, h, j: (b, h)),
                  pl.BlockSpec((tk, QK_PAD), lambda b, h, j: (b * nk + j, h)),
                  pl.BlockSpec((tk, B_V), lambda b, h, j: (b * nk + j, h)),
                  pl.BlockSpec((s, B_V), lambda b, h, j: (b, h)), row, row] + [_HBM] * ng,
        out_specs=[pl.BlockSpec((QK_PAD, s), lambda b, h, j: (h, b)),
                   pl.BlockSpec((tk, QK_PAD), lambda b, h, j: (b * nk + j, h)),
                   pl.BlockSpec((tk, B_V), lambda b, h, j: (b * nk + j, h))] + [_HBM] * ng,
        out_shape=[jax.ShapeDtypeStruct((B_HEADS * QK_PAD, t), F32), jax.ShapeDtypeStruct((t, B_HEADS * QK_PAD), F32),
                   jax.ShapeDtypeStruct((t, B_HEADS * B_V), F32)]
        + [jax.ShapeDtypeStruct((3,) + g.shape[1:], g.dtype) for g in gsegs],
        scratch_shapes=[pltpu.SemaphoreType.DMA((3 * ng,)), pltpu.SemaphoreType.DMA((3 * ng,))],
        compiler_params=_params(3),
    )(q, k, v, do16, lse, delta, *gsegs)


def _group_ones16():
    r = _iota2((A_WIDTH, A_WIDTH), 0) // 64
    c = _iota2((A_WIDTH, A_WIDTH), 1) // 64
    return (r == c).astype(BF16)


def _head_rms(o, ones16):
    return lax.rsqrt(_exact_right(o * o, ones16) * (1.0 / 64.0) + EPS)


def _out_fwd(o_f, o_r, hg, o_attn, x, g_hn, g_mla, w_out, g2):
    t = x.shape[0]
    tm = _row_tile(t, 2 * ROW_GROUP)

    def body(of_ref, or_ref, hg_ref, oa_ref, x_ref, ghn_ref, gm_ref, w_ref, g2_ref, y_ref, x2_ref, h2_ref):
        ones16 = _group_ones16()
        ghn, gm, g2v = ghn_ref[...], gm_ref[...], g2_ref[...]

        def group(ofv, orv, hgv, oav, xv):
            o = ofv + orv
            ya16 = (((o * _head_rms(o, ones16)) * ghn) * (hgv * _sigmoid(hgv))).astype(BF16)
            yb16 = _rms_fwd(oav, gm).astype(BF16)
            x2 = xv + _dot(ya16, w_ref[0:A_WIDTH, :]) + _dot(yb16, w_ref[A_WIDTH:D_MODEL, :])
            return ya16, yb16, x2, _rms_fwd(x2, g2v).astype(BF16)

        ins = (of_ref, or_ref, hg_ref, oa_ref, x_ref)
        groups = _row_groups(tm)
        done = [group(*vals) for vals in [[ref[g, :] for ref in ins] for g in groups]]
        for g, (ya16, yb16, x2, h2) in zip(groups, done):
            y_ref[g, 0:A_WIDTH] = ya16
            y_ref[g, A_WIDTH:D_MODEL] = yb16
            x2_ref[g, :] = x2
            h2_ref[g, :] = h2

    a512 = _rows(tm, A_WIDTH)
    return pl.pallas_call(
        body, name="out_fwd", grid=(t // tm,),
        in_specs=[a512, a512, a512, a512, _rows(tm, D_MODEL), _full((1, A_WIDTH)), _full((1, A_WIDTH)),
                  _full((D_MODEL, D_MODEL)), _full((1, D_MODEL))],
        out_specs=[_rows(tm, D_MODEL)] * 3,
        out_shape=[jax.ShapeDtypeStruct((t, D_MODEL), BF16), jax.ShapeDtypeStruct((t, D_MODEL), F32),
                   jax.ShapeDtypeStruct((t, D_MODEL), BF16)],
        compiler_params=_params(1),
    )(o_f, o_r, hg, o_attn, x, g_hn, g_mla, w_out, g2)


def _out_bwd(dx2_16, o_f, o_r, hg, o_attn, g_hn, g_mla, w_out):
    t = dx2_16.shape[0]
    tm = _row_tile(t, 2 * ROW_GROUP)

    def body(dx_ref, of_ref, or_ref, hg_ref, oa_ref, ghn_ref, gm_ref, w_ref,
             do_ref, dhg_ref, doa_ref, dl_ref, dghn_ref, dgm_ref):
        first = pl.program_id(0) == 0
        ones16 = _group_ones16()
        sel16 = (_iota2((8, A_WIDTH), 0) == _iota2((8, A_WIDTH), 1) // B_V).astype(BF16)
        ghn, gm = ghn_ref[...], gm_ref[...]

        def group(dxv, ofv, orv, hgv, oav):
            dya = _dot_nt(dxv, w_ref[0:A_WIDTH, :])
            dyb = _dot_nt(dxv, w_ref[A_WIDTH:D_MODEL, :])
            o = ofv + orv
            rh = _head_rms(o, ones16)
            oh = o * rh
            sg = _sigmoid(hgv)
            sl = hgv * sg
            dhg = ((dya * (oh * ghn)) * (sg * (1.0 + hgv * (1.0 - sg)))).astype(BF16)
            doh = dya * sl * ghn
            do = rh * (doh - oh * (_exact_right(doh * oh, ones16) * (1.0 / 64.0)))
            doa, dgm = _rms_bwd(dyb, oav, gm)
            x1, x2, x3 = _split3(doa * oav)
            delta = _dot_nt(sel16, x1) + _dot_nt(sel16, x2) + _dot_nt(sel16, x3)
            return do, dhg, doa.astype(BF16), delta, dya * sl * oh, dgm

        groups = _row_groups(tm)
        loaded = [(dx_ref[g, :], of_ref[g, :], or_ref[g, :], hg_ref[g, :], oa_ref[g, :]) for g in groups]
        done = [group(*vals) for vals in loaded]
        for g, (do, dhg, doa16, delta, _, _) in zip(groups, done):
            do_ref[g, :] = do
            dhg_ref[g, :] = dhg
            doa_ref[g, :] = doa16
            for h in range(B_HEADS):
                dl_ref[h, :, g] = delta[h:h + 1, :]
        _acc_rows(dghn_ref, jnp.concatenate([d[4] for d in done], axis=0), first)
        _acc_rows(dgm_ref, jnp.concatenate([d[5] for d in done], axis=0), first)

    a512 = _rows(tm, A_WIDTH)
    return pl.pallas_call(
        body, name="out_bwd", grid=(t // tm,),
        in_specs=[_rows(tm, D_MODEL), a512, a512, a512, a512, _full((1, A_WIDTH)), _full((1, A_WIDTH)),
                  _full((D_MODEL, D_MODEL))],
        out_specs=[a512, a512, a512, pl.BlockSpec((B_HEADS, 1, tm), lambda i: (0, 0, i)),
                   _full((1, A_WIDTH)), _full((1, A_WIDTH))],
        out_shape=[jax.ShapeDtypeStruct((t, A_WIDTH), F32)] + [jax.ShapeDtypeStruct((t, A_WIDTH), BF16)] * 2
        + [jax.ShapeDtypeStruct((B_HEADS, 1, t), F32)]
        + [jax.ShapeDtypeStruct((1, A_WIDTH), F32)] * 2,
        compiler_params=_params(1),
    )(dx2_16, o_f, o_r, hg, o_attn, g_hn, g_mla, w_out)


def _ffn_fwd_bwd(h2, x2, target, w_gate, w_up, w_down, g_f, g2):
    t = x2.shape[0]
    tm = _row_tile(t)
    inv_d = 1.0 / D_MODEL

    def body(h2_ref, x2_ref, tg_ref, wg_ref, wu_ref, wd_ref, gf_ref, g2_ref,
             act_ref, dgate_ref, dup_ref, dx3_ref, dx2_ref, dx2h_ref, loss_ref, dgf_ref, dg2_ref):
        first = pl.program_id(0) == 0
        h2v = h2_ref[...]
        gate = _dot_nt(h2v, wg_ref[...])
        up = _dot_nt(h2v, wu_ref[...])
        sg = _sigmoid(gate)
        sl = gate * sg
        act16 = (sl * up).astype(BF16)
        act_ref[...] = act16
        x2v = x2_ref[...]
        x3 = x2v + _dot(act16, wd_ref[...])
        r3 = lax.rsqrt(jnp.mean(x3 * x3, axis=-1, keepdims=True) + EPS)
        x3h = x3 * r3
        gf = gf_ref[...]
        err = x3h * gf - tg_ref[...]
        part = 0.5 * jnp.sum(jnp.mean(err * err, axis=-1, keepdims=True), axis=0, keepdims=True)

        @pl.when(first)
        def _():
            loss_ref[...] = jnp.zeros_like(loss_ref)

        loss_ref[...] += part
        dy = err * inv_d
        _acc_rows(dgf_ref, dy * x3h, first)
        dx3h = dy * gf
        dx3 = r3 * (dx3h - x3h * jnp.mean(dx3h * x3h, axis=-1, keepdims=True))
        dx3_16 = dx3.astype(BF16)
        dx3_ref[...] = dx3_16
        da = _dot_nt(dx3_16, wd_ref[...])
        dup16 = (da * sl).astype(BF16)
        dgate16 = (da * up * (sg * (1.0 + gate * (1.0 - sg)))).astype(BF16)
        dup_ref[...] = dup16
        dgate_ref[...] = dgate16
        dh2 = _dot(dgate16, wg_ref[...]) + _dot(dup16, wu_ref[...])
        dx2n, dg2 = _rms_bwd(dh2, x2v, g2_ref[...])
        _acc_rows(dg2_ref, dg2, first)
        dx2 = dx3 + dx2n
        dx2_ref[...] = dx2
        dx2h_ref[...] = dx2.astype(BF16)

    return pl.pallas_call(
        body, name="ffn_fwd_bwd", grid=(t // tm,),
        in_specs=[_rows(tm, D_MODEL), _rows(tm, D_MODEL), _rows(tm, D_MODEL), _full((D_FF, D_MODEL), True),
                  _full((D_FF, D_MODEL), True), _full((D_FF, D_MODEL), True), _full((1, D_MODEL)), _full((1, D_MODEL))],
        out_specs=[_rows(tm, D_FF), _rows(tm, D_FF), _rows(tm, D_FF), _rows(tm, D_MODEL), _rows(tm, D_MODEL),
                   _rows(tm, D_MODEL), _full((8, 128)), _full((1, D_MODEL)), _full((1, D_MODEL))],
        out_shape=[jax.ShapeDtypeStruct((t, D_FF), BF16)] * 3
        + [jax.ShapeDtypeStruct((t, D_MODEL), BF16), jax.ShapeDtypeStruct((t, D_MODEL), F32),
           jax.ShapeDtypeStruct((t, D_MODEL), BF16), jax.ShapeDtypeStruct((8, 128), F32),
           jax.ShapeDtypeStruct((1, D_MODEL), F32), jax.ShapeDtypeStruct((1, D_MODEL), F32)],
        compiler_params=_params(1),
    )(h2, x2, target, w_gate, w_up, w_down, g_f, g2)


def _wgrad(a, b, name):
    t, m = a.shape
    n = b.shape[1]
    bm = _pick(m, 1664)
    bn = _pick(n, 1664)
    tk = _pick(t, 2048, 16)
    nk = t // tk

    def body(a_ref, b_ref, o_ref, o16_ref):
        k = pl.program_id(2)
        part = _dot_tn(a_ref[...], b_ref[...])

        @pl.when(k == 0)
        def _():
            o_ref[...] = part

        @pl.when(k != 0)
        def _():
            o_ref[...] += part

        @pl.when(k == nk - 1)
        def _():
            o16_ref[...] = o_ref[...].astype(BF16)

    out = pl.BlockSpec((bm, bn), lambda i, j, k: (i, j))
    return pl.pallas_call(
        body, name=name, grid=(m // bm, n // bn, nk),
        in_specs=[pl.BlockSpec((tk, bm), lambda i, j, k: (k, i)), pl.BlockSpec((tk, bn), lambda i, j, k: (k, j))],
        out_specs=[out, out],
        out_shape=[jax.ShapeDtypeStruct((m, n), F32), jax.ShapeDtypeStruct((m, n), BF16)],
        compiler_params=_params(3),
    )(a, b)


def _rope_tables(seq):
    inv = 1.0 / (ROPE_THETA ** (jnp.arange(0, B_ROPE, 2, dtype=F32) / B_ROPE))
    ang = jnp.arange(seq, dtype=F32)[:, None] * inv[None, :]
    cos, sin = jnp.cos(ang), jnp.sin(ang)
    zeros = jnp.zeros((seq, 64), F32)
    return jnp.concatenate([cos, cos, zeros], axis=1), jnp.concatenate([-sin, sin, zeros], axis=1)


def _pad_weights(w_in_t, w_q_t, w_kv_b):
    w_in_p = jnp.pad(w_in_t, ((0, D_IN_PAD - D_IN), (0, 0)))
    w_q_p = jnp.pad(w_q_t.reshape(B_HEADS, B_NOPE + B_ROPE, Q_LORA), ((0, 0), (0, 64), (0, 0))).reshape(1024, Q_LORA)
    kv = w_kv_b.reshape(KV_LORA, B_HEADS, B_NOPE + B_V)
    w_kv_p = jnp.concatenate([kv[:, :, :B_NOPE].reshape(KV_LORA, 512), kv[:, :, B_NOPE:].reshape(KV_LORA, 512)], axis=1)
    return w_in_p, w_q_p, w_kv_p


def _unpad_kv(g_kv_p):
    return jnp.concatenate([g_kv_p[:, :512].reshape(KV_LORA, B_HEADS, B_NOPE),
                            g_kv_p[:, 512:].reshape(KV_LORA, B_HEADS, B_V)], axis=2).reshape(KV_LORA, 1024)


_OWN_ROWS = dict(w_in=(272, 3), w_q_b=(64, 4), w_kv_b=(256, 0), w_out=(256, 1), w_gate=(176, 4), w_up=(176, 4), w_down=(176, 4))


def _local_step(x, target, lbl, g1, g_hn, g_qa, g_kva, g_mla, g2, g_f, w_in, w_q_b, w_kv_b, late_shards):
    n_batch, seq, _ = x.shape
    t = n_batch * seq
    x = x.reshape(t, D_MODEL)
    target = target.reshape(t, D_MODEL)
    w_in_p, w_q_p, w_kv_p = _pad_weights(w_in, w_q_b, w_kv_b)
    cosx, sinx = _rope_tables(seq)

    h1, hq, hi, hff, hfb, hg, cq, ckv, kr = _in_fwd(x, g1, w_in_p)
    o_f, st_f = _hgrn_fwd(hq, hi, hff, lbl, n_batch=n_batch, direction=0)
    o_r, st_r = _hgrn_fwd(hq, hi, hfb, lbl, n_batch=n_batch, direction=1)
    q, k, v, cqn, ckvn = _mla_prep(cq, ckv, kr, g_qa, g_kva, w_q_p, w_kv_p, cosx, sinx, n_batch=n_batch)
    o_attn, lse, *gathered = _attn_fwd(q, k, v, late_shards, n_batch=n_batch)
    w_out, w_gate, w_up, w_down = (_join_shards(n, a) for n, a in zip(_LATE, gathered))
    ycat, x2, h2 = _out_fwd(o_f, o_r, hg, o_attn, x, g_hn, g_mla, w_out, g2)
    act, dgate, dup, dx3_16, dx2, dx2_16, loss, dg_f, dg2 = _ffn_fwd_bwd(h2, x2, target, w_gate, w_up, w_down, g_f, g2)
    full = dict(w_out=_wgrad(ycat, dx2_16, "wgrad_out"), w_gate=_wgrad(dgate, h2, "wgrad_gate"),
                w_up=_wgrad(dup, h2, "wgrad_up"), w_down=_wgrad(act, dx3_16, "wgrad_down"))
    do_h, dhg, do_attn16, delta, dg_hn, dg_mla = _out_bwd(dx2_16, o_f, o_r, hg, o_attn, g_hn, g_mla, w_out)
    dqt, dk, dv, *recv_late = _attn_bwd(q, k, v, do_attn16, lse, delta,
                                        [_segments(n, full[n][1]) for n in _LATE], n_batch=n_batch)
    dcq, dckv, dkr, dqp16, dkvp16, dg_qa, dg_kva = _mla_prep_bwd(dqt, dk, dv, cq, ckv, g_qa, g_kva, w_q_p, w_kv_p,
                                                                  cosx, sinx, n_batch=n_batch)
    dq_f, dv_f, dz_f, dl_f = _hgrn_bwd(hq, hi, hff, do_h, st_f, lbl, n_batch=n_batch, direction=0)
    dq_r, dv_r, dz_r, dl_r = _hgrn_bwd(hq, hi, hfb, do_h, st_r, lbl, n_batch=n_batch, direction=1)
    dx, dproj16, dg1 = _in_bwd(dq_f, dq_r, dv_f, dv_r, dz_f, dz_r, dhg, hq, dcq, dckv, dkr, dx2, x, g1, w_in_p)

    full.update(w_in=_wgrad(dproj16, h1, "wgrad_in"), w_q_b=_wgrad(dqp16, cqn, "wgrad_q_b"))
    g_kv = _unpad_kv(_wgrad(ckvn, dkvp16, "wgrad_kv_b")[0])
    early16 = dict(w_in=full["w_in"][1][:D_IN].reshape(N_CHIPS, D_IN // N_CHIPS, D_MODEL),
                   w_q_b=full["w_q_b"][1].reshape(B_HEADS, QK_PAD, Q_LORA)[:, :B_NOPE + B_ROPE],
                   w_kv_b=_segments("w_kv_b", g_kv).astype(BF16))
    small = dict(norm1_g=dg1, lb_logits=jnp.stack([dl_f, dl_r]), hgrn_norm_g=dg_hn, q_a_norm_g=dg_qa, kv_a_norm_g=dg_kva,
                 mla_norm_g=dg_mla, norm2_g=dg2, final_norm_g=dg_f, loss=loss[0:1, 0])
    return dx.reshape(n_batch, seq, D_MODEL), small, {**{n: f for n, (f, _) in full.items()}, "w_kv_b": g_kv}, early16, \
        dict(zip(_LATE, recv_late))


_HBM = pl.BlockSpec(memory_space=pltpu.HBM)
_MESH = pl.DeviceIdType.MESH


def _place():
    x, y, c = lax.axis_index("x"), lax.axis_index("y"), lax.axis_index("c")
    other_chips = [(1 - x, y), (x, 1 - y), (1 - x, 1 - y)]
    return x, y, c, other_chips


def _gather_copies(w_ref, wall_ref, send_sems, recv_sems, local_sem, base=0):
    x, y, c, chips = _place()
    mine = 2 * x + y

    def mk(j, chip_index, to):
        return pltpu.make_async_remote_copy(src_ref=w_ref, dst_ref=wall_ref.at[chip_index], send_sem=send_sems.at[base + j],
                                            recv_sem=recv_sems.at[base + j], device_id=to, device_id_type=_MESH)

    local = pltpu.make_async_copy(w_ref, wall_ref.at[mine], local_sem)
    sends = [mk(j, mine, (*chip, c)) for j, chip in enumerate(chips)]
    recvs = [mk(j, 2 * px + py, (x, y, c)) for j, (px, py) in enumerate(chips)]
    return local, sends, recvs


def _gather_start(*refs):
    local, sends, _ = _gather_copies(*refs)
    local.start()
    for cp in sends:
        cp.start()


def _gather_wait(*refs):
    local, sends, recvs = _gather_copies(*refs)
    for cp in recvs:
        cp.wait_recv()
    for cp in sends:
        cp.wait_send()
    local.wait()


def _scatter_copies(g_ref, recv_ref, send_sems, recv_sems, base=0):
    x, y, c, chips = _place()

    def mk(j, src_index, to):
        return pltpu.make_async_remote_copy(src_ref=g_ref.at[src_index], dst_ref=recv_ref.at[j],
                                            send_sem=send_sems.at[base + j], recv_sem=recv_sems.at[base + j],
                                            device_id=to, device_id_type=_MESH)

    sends = [mk(j, 2 * px + py, (px, py, c)) for j, (px, py) in enumerate(chips)]
    recvs = [mk(j, 0, (x, y, c)) for j in range(3)]
    return sends, recvs


def _scatter_start(*refs):
    for cp in _scatter_copies(*refs)[0]:
        cp.start()


def _scatter_wait(*refs):
    sends, recvs = _scatter_copies(*refs)
    for cp in recvs:
        cp.wait_recv()
    for cp in sends:
        cp.wait_send()


def _half(ref, which, axis):
    n = ref.shape[axis] // 2
    idx = [slice(None)] * len(ref.shape)
    idx[axis] = pl.ds(which * n, n)
    return ref.at[tuple(idx)]


def _split_axis(a):
    return 1 if a.shape[1] % 256 == 0 else 0


def _gather_halves(src_ref, dst_ref, send_sems, recv_sems, local_sem, base, axis):
    x, y, c, chips = _place()
    mine = 2 * x + y

    def remote(src, dst, n, to):
        return pltpu.make_async_remote_copy(src_ref=src, dst_ref=dst, send_sem=send_sems.at[base + n],
                                            recv_sem=recv_sems.at[base + n], device_id=to, device_id_type=_MESH)

    def slot(chip_index, which):
        return _half(dst_ref.at[chip_index], which, axis)

    local = pltpu.make_async_copy(src_ref, dst_ref.at[mine], local_sem)
    first = [remote(_half(src_ref, c, axis), slot(mine, c), j, (*chip, c)) for j, chip in enumerate(chips)]
    landed = [remote(_half(src_ref, c, axis), slot(2 * px + py, c), j, (x, y, c)) for j, (px, py) in enumerate(chips)]
    passed = [remote(slot(2 * px + py, c), slot(2 * px + py, c), 3 + j, (x, y, 1 - c)) for j, (px, py) in enumerate(chips)]
    handed = [remote(_half(src_ref, c, axis), slot(2 * px + py, 1 - c), 3 + j, (x, y, c)) for j, (px, py) in enumerate(chips)]

    def start():
        local.start()
        for cp in first:
            cp.start()

    def pass_on():
        for arrived, onward in zip(landed, passed):
            arrived.wait_recv()
            onward.start()

    def finish():
        for cp in handed:
            cp.wait_recv()
        for cp in first + passed:
            cp.wait_send()
        local.wait()

    return start, pass_on, finish


def _gather_shards(srcs, small):
    n = len(srcs)

    def body(*refs):
        src_refs, s_ref, dst_refs, sall_ref = refs[:n], refs[n], refs[n + 1:2 * n + 1], refs[2 * n + 1]
        send_sems, recv_sems, local_sems = refs[2 * n + 2:]
        _gather_start(s_ref, sall_ref, send_sems, recv_sems, local_sems.at[n], 6 * n)
        steps = [_gather_halves(src_refs[i], dst_refs[i], send_sems, recv_sems, local_sems.at[i], 6 * i, _split_axis(srcs[i]))
                 for i in range(n)]
        for phase in range(3):
            for step in steps:
                step[phase]()
        _gather_wait(s_ref, sall_ref, send_sems, recv_sems, local_sems.at[n], 6 * n)

    return pl.pallas_call(
        body, name="gather_shards", in_specs=[_HBM] * (n + 1), out_specs=[_HBM] * (n + 1),
        out_shape=[jax.ShapeDtypeStruct((N_CHIPS,) + a.shape, a.dtype) for a in list(srcs) + [small]],
        scratch_shapes=[pltpu.SemaphoreType.DMA((6 * n + 3,)), pltpu.SemaphoreType.DMA((6 * n + 3,)),
                        pltpu.SemaphoreType.DMA((n + 1,))],
    )(*srcs, small)


_SEM = pl.BlockSpec(memory_space=pltpu.SEMAPHORE)
_SIDE_EFFECT = pltpu.SideEffectType.DATAFLOW_SIDE_EFFECTING


def _scatter_begin(gsegs):
    n = len(gsegs)
    def hbm(a):
        return pltpu.with_memory_space_constraint(a, pltpu.HBM)

    lands = [hbm(lax.empty((3,) + g.shape[1:], g.dtype)) for g in gsegs]

    def body(*refs):
        g_refs, land_refs, send_sems, recv_sems, token = refs[:n], refs[n:2 * n], refs[2 * n], refs[2 * n + 1], refs[-1]
        for i in range(n):
            _scatter_start(g_refs[i], land_refs[i], send_sems, recv_sems, 3 * i)
        token[...] = jnp.zeros_like(token)

    out = pl.pallas_call(
        body, name="scatter_begin", in_specs=[_HBM] * (2 * n),
        out_specs=(_SEM, _SEM) + (_HBM,) * (2 * n) + (pl.BlockSpec(memory_space=pltpu.VMEM),),
        out_shape=(pltpu.SemaphoreType.DMA((3 * n,)), pltpu.SemaphoreType.DMA((3 * n,)))
        + tuple(pltpu.HBM(a.shape, a.dtype) for a in list(gsegs) + lands) + (jax.ShapeDtypeStruct((8, 128), F32),),
        input_output_aliases={i: 2 + i for i in range(2 * n)},
        compiler_params=pltpu.CompilerParams(has_side_effects=_SIDE_EFFECT),
    )(*[hbm(g) for g in gsegs], *lands)
    return out[0], out[1], out[2:2 + n], out[2 + n:2 + 2 * n], out[-1]


def _scatter_end(send_sems, recv_sems, gsegs, lands, after):
    n = len(gsegs)

    def body(*refs):
        g_refs, land_refs, ssems, rsems = refs[:n], refs[n:2 * n], refs[2 * n], refs[2 * n + 1]
        for i in range(n):
            _scatter_wait(g_refs[i], land_refs[i], ssems, rsems, 3 * i)

    out = pl.pallas_call(
        body, name="scatter_end", in_specs=[_HBM] * (2 * n) + [_SEM, _SEM] + [pl.BlockSpec(memory_space=pl.ANY)] * len(after),
        out_specs=(_HBM,) * (2 * n), out_shape=tuple(pltpu.HBM(a.shape, a.dtype) for a in list(gsegs) + list(lands)),
        input_output_aliases={i: i for i in range(2 * n)},
        compiler_params=pltpu.CompilerParams(has_side_effects=_SIDE_EFFECT),
    )(*gsegs, *lands, send_sems, recv_sems, *after)
    return out[n:]


def _swap_sibling(parts, small=None):
    n = len(parts)
    extra = small is not None

    def body(*refs):
        ins, outs = refs[:n + extra], refs[n + extra:2 * (n + extra)]
        send_sems, recv_sems, local_sem = refs[2 * (n + extra):]
        x, y, c, _ = _place()
        cps = [pltpu.make_async_remote_copy(src_ref=ins[i], dst_ref=outs[i], send_sem=send_sems.at[i],
                                            recv_sem=recv_sems.at[i], device_id=(x, y, 1 - c), device_id_type=_MESH)
               for i in range(n)]
        for cp in cps:
            cp.start()
        if extra:
            s_ref, sall_ref = ins[n], outs[n]
            me = 4 * x + 2 * y + c
            flips = [(fx, fy, fc) for fx in (0, 1) for fy in (0, 1) for fc in (0, 1)][1:]

            def peer(f):
                return tuple((1 - a) if b else a for a, b in zip((x, y, c), f))

            def sm(r, index, to):
                return pltpu.make_async_remote_copy(src_ref=s_ref, dst_ref=sall_ref.at[index], send_sem=send_sems.at[n + r],
                                                    recv_sem=recv_sems.at[n + r], device_id=to, device_id_type=_MESH)

            local = pltpu.make_async_copy(s_ref, sall_ref.at[me], local_sem)
            local.start()
            sends = [sm(r, me, peer(f)) for r, f in enumerate(flips)]
            for cp in sends:
                cp.start()
            for r, f in enumerate(flips):
                px, py, pc = peer(f)
                sm(r, 4 * px + 2 * py + pc, (x, y, c)).wait_recv()
            for cp in sends:
                cp.wait_send()
            local.wait()
        for cp in cps:
            cp.wait()

    return pl.pallas_call(
        body, name="swap_small" if extra else "swap_sibling", in_specs=[_HBM] * (n + extra), out_specs=[_HBM] * (n + extra),
        out_shape=[jax.ShapeDtypeStruct(p.shape, p.dtype) for p in parts]
        + ([jax.ShapeDtypeStruct((8,) + small.shape, F32)] if extra else []),
        scratch_shapes=[pltpu.SemaphoreType.DMA((n + 7,)), pltpu.SemaphoreType.DMA((n + 7,)), pltpu.SemaphoreType.DMA(())],
    )(*parts, *([small] if extra else []))


def _sum_segments(chip, full, recv, tm, pitch, name, after=()):
    _, rows, cols = recv.shape

    def body(chip_ref, o_ref, r_ref, *rest):
        acc = o_ref[...]
        for j in range(3):
            acc = acc + r_ref[j].astype(F32)
        rest[-1][...] = acc

    return pl.pallas_call(
        body, name=name,
        grid_spec=pltpu.PrefetchScalarGridSpec(
            num_scalar_prefetch=1, grid=(rows // tm,),
            in_specs=[pl.BlockSpec((tm, cols), lambda i, c: (c[0] * pitch + i, 0)),
                      pl.BlockSpec((3, tm, cols), lambda i, c: (0, i, 0))]
            + [pl.BlockSpec(a.shape, lambda i, c, nd=a.ndim: (0,) * nd) for a in after],
            out_specs=pl.BlockSpec((tm, cols), lambda i, c: (i, 0))),
        out_shape=jax.ShapeDtypeStruct((rows, cols), F32), compiler_params=_params(1),
    )(chip, full, recv, *after)


def _sum_devices(sall):
    def body(s_ref, o_ref):
        acc = s_ref[0]
        for d in range(1, 8):
            acc = acc + s_ref[d]
        o_ref[...] = acc

    return pl.pallas_call(body, name="sum_devices", out_shape=jax.ShapeDtypeStruct(sall.shape[1:], F32))(sall)


def _adamw(w, m, v, ga, gb, name):
    rows, cols = w.shape
    tm = _row_tile(rows)
    two = gb is not None

    def body(*refs):
        w_ref, m_ref, v_ref, ga_ref = refs[:4]
        g_ref, d_ref, m2_ref, v2_ref = refs[-4:]
        g = ga_ref[...] + refs[4][...] if two else ga_ref[...]
        m2 = ADAM_B1 * m_ref[...] + (1.0 - ADAM_B1) * g
        v2 = ADAM_B2 * v_ref[...] + (1.0 - ADAM_B2) * (g * g)
        m_hat = m2 / (1.0 - ADAM_B1 ** ADAM_STEP)
        v_hat = v2 / (1.0 - ADAM_B2 ** ADAM_STEP)
        g_ref[...] = g
        d_ref[...] = -ADAM_LR * (m_hat / (jnp.sqrt(v_hat) + ADAM_EPS) + ADAM_WD * w_ref[...])
        m2_ref[...] = m2
        v2_ref[...] = v2

    blk = _rows(tm, cols)
    args = (w, m, v, ga) + ((gb,) if two else ())
    return pl.pallas_call(
        body, name=name, grid=(rows // tm,), in_specs=[blk] * len(args), out_specs=[blk] * 4,
        out_shape=[jax.ShapeDtypeStruct(w.shape, F32)] * 4, compiler_params=_params(1),
    )(*args)


_TRANSPOSED = ("w_in", "w_q_b", "w_gate", "w_up")
_COLUMN_SHARDED = ("w_kv_b",)
_FULL_SHAPES = dict(w_in=(D_IN, D_MODEL), w_q_b=(768, Q_LORA), w_kv_b=(KV_LORA, 1024), w_out=(D_MODEL, D_MODEL),
                    w_gate=(D_FF, D_MODEL), w_up=(D_FF, D_MODEL), w_down=(D_FF, D_MODEL))
_SMALL = (("norm1_g", 1024), ("lb_logits", 2048), ("hgrn_norm_g", 512), ("q_a_norm_g", 384), ("kv_a_norm_g", 256),
          ("mla_norm_g", 512), ("norm2_g", 1024), ("final_norm_g", 1024))
_UPDATE_ROWS = 48


def _pad_rows(a, rows):
    return jnp.pad(a, ((0, rows - a.shape[0]), (0, 0)))


_EARLY = ("w_in", "w_q_b", "w_kv_b")
_LATE = ("w_out", "w_gate", "w_up", "w_down")


def _segments(name, g):
    r, c = g.shape
    if name in _COLUMN_SHARDED:
        return g.reshape(r, N_CHIPS, c // N_CHIPS).transpose(1, 0, 2)
    return g.reshape(N_CHIPS, r // N_CHIPS, c)


def _own_segment(name, g, chip):
    r, c = g.shape
    if name in _COLUMN_SHARDED:
        return lax.dynamic_slice_in_dim(g, chip * (c // N_CHIPS), c // N_CHIPS, axis=1)
    return lax.dynamic_slice_in_dim(g, chip * (r // N_CHIPS), r // N_CHIPS, axis=0)


def _join_shards(name, seg):
    r, c = _FULL_SHAPES[name]
    if name in _COLUMN_SHARDED:
        return seg.transpose(1, 0, 2).reshape(r, c)
    return seg.reshape(r, c)


def kernel(x, norm1_g, w_in, lb_logits, hgrn_norm_g, q_a_norm_g, w_q_b, kv_a_norm_g, w_kv_b, mla_norm_g, w_out, norm2_g, w_gate, w_up, w_down, final_norm_g, loss_target, m_norm1_g, m_w_in, m_lb_logits, m_hgrn_norm_g, m_q_a_norm_g, m_w_q_b, m_kv_a_norm_g, m_w_kv_b, m_mla_norm_g, m_w_out, m_norm2_g, m_w_gate, m_w_up, m_w_down, m_final_norm_g, v_norm1_g, v_w_in, v_lb_logits, v_hgrn_norm_g, v_q_a_norm_g, v_w_q_b, v_kv_a_norm_g, v_w_kv_b, v_mla_norm_g, v_w_out, v_norm2_g, v_w_gate, v_w_up, v_w_down, v_final_norm_g):
    names = ("norm1_g", "w_in", "lb_logits", "hgrn_norm_g", "q_a_norm_g", "w_q_b", "kv_a_norm_g", "w_kv_b", "mla_norm_g",
             "w_out", "norm2_g", "w_gate", "w_up", "w_down", "final_norm_g")
    w = dict(zip(names, (norm1_g, w_in, lb_logits, hgrn_norm_g, q_a_norm_g, w_q_b, kv_a_norm_g, w_kv_b, mla_norm_g,
                         w_out, norm2_g, w_gate, w_up, w_down, final_norm_g)))
    m = dict(zip(names, (m_norm1_g, m_w_in, m_lb_logits, m_hgrn_norm_g, m_q_a_norm_g, m_w_q_b, m_kv_a_norm_g, m_w_kv_b,
                         m_mla_norm_g, m_w_out, m_norm2_g, m_w_gate, m_w_up, m_w_down, m_final_norm_g)))
    v = dict(zip(names, (v_norm1_g, v_w_in, v_lb_logits, v_hgrn_norm_g, v_q_a_norm_g, v_w_q_b, v_kv_a_norm_g, v_w_kv_b,
                         v_mla_norm_g, v_w_out, v_norm2_g, v_w_gate, v_w_up, v_w_down, v_final_norm_g)))
    matrices = _EARLY + _LATE
    chip = 2 * lax.axis_index("x") + lax.axis_index("y")

    def shard2d(a, n):
        return jnp.swapaxes(a[0], 0, 1) if n in _TRANSPOSED else a[0]

    def unshard2d(a, n):
        return (jnp.swapaxes(a, 0, 1) if n in _TRANSPOSED else a)[None]

    w16 = {n: shard2d(w[n], n).astype(BF16) for n in matrices}
    lb8 = _pad_rows(lb_logits.reshape(4, 128), 8)
    *early, lball = _gather_shards([w16[n] for n in _EARLY], lb8)
    lbl = lball[:, :4].reshape(N_CHIPS, 2, 2, 128).transpose(1, 2, 0, 3).reshape(2, 2, A_WIDTH)

    grad_x, g, full, early16, recv = _local_step(
        x, loss_target, lbl, norm1_g, hgrn_norm_g, q_a_norm_g, kv_a_norm_g, mla_norm_g, norm2_g, final_norm_g[None, :],
        *(_join_shards(n, a) for n, a in zip(_EARLY, early)), [w16[n] for n in _LATE])

    n_small = sum(size for _, size in _SMALL)
    small = jnp.concatenate([g[n].reshape(-1) for n, _ in _SMALL] + [g["loss"], jnp.zeros((SMALL_ROWS * 128 - n_small - 1,), F32)])
    chip1 = chip.reshape(1).astype(jnp.int32)
    out, raw = {}, {}

    def update(n, part, sib):
        raw[n] = _adamw(shard2d(w[n], n), shard2d(m[n], n), shard2d(v[n], n), part, sib, f"adamw_{n}")
        out[n] = tuple(unshard2d(r, n) for r in raw[n])

    send_sems, recv_sems, segs, lands, token = _scatter_begin([early16[n] for n in _EARLY])
    parts = [_sum_segments(chip1, full[n], recv[n], *_OWN_ROWS[n], f"sum_{n}", after=(token,)) for n in _LATE]
    for n, part, sib in zip(_LATE, parts, _swap_sibling(parts)):
        update(n, part, sib)
    recv_early = _scatter_end(send_sems, recv_sems, segs, lands, after=[raw[n][1] for n in _LATE])
    full["w_kv_b"] = _own_segment("w_kv_b", full["w_kv_b"], chip)
    parts = [_sum_segments(chip1, full[n], r, *_OWN_ROWS[n], f"sum_{n}") for n, r in zip(_EARLY, recv_early)]
    *sibs, small_all = _swap_sibling(parts, small.reshape(SMALL_ROWS, 128))
    for n, part, sib in zip(_EARLY, parts, sibs):
        update(n, part, sib)
    small_sum = _sum_devices(small_all).reshape(-1)
    loss = small_sum[n_small]

    small_g, off = {}, 0
    for n, size in _SMALL:
        small_g[n] = small_sum[off:off + size]
        off += size
    small_g["lb_logits"] = lax.dynamic_slice_in_dim(small_g["lb_logits"].reshape(2, 2, A_WIDTH), chip * 128, 128, axis=2)
    small_names = tuple(n for n, _ in _SMALL)

    def pack(d):
        return _pad_rows(jnp.concatenate([d[n].reshape(-1) for n in small_names]).reshape(-1, 128), _UPDATE_ROWS)

    res = _adamw(pack(w), pack(m), pack(v), pack(small_g), None, "adamw_small")
    off = 0
    flat = [r.reshape(-1) for r in res]
    for n in small_names:
        size = w[n].size
        out[n] = tuple(f[off:off + size].reshape(w[n].shape) for f in flat)
        off += size

    return (loss, grad_x) + tuple(out[n][i] for i in range(4) for n in names)
```

```python
import functools

import jax
import jax.numpy as jnp
from jax import lax
from jax.experimental import pallas as pl
from jax.experimental.pallas import tpu as pltpu

F32 = jnp.float32
BF16 = jnp.bfloat16

D_MODEL = 1024
A_WIDTH = 512
HEAD_PAIRS = 4
PAIRS_AT_ONCE = 2
CHUNK = 64
B_HEADS = 4
B_NOPE = 128
B_ROPE = 64
B_V = 128
QK_PAD = 256
Q_LORA = 384
KV_LORA = 256
D_FF = 2816
D_IN = 3264
D_IN_PAD = 3328
IN_WIDTHS = (512, 512, 512, 512, 512, Q_LORA, KV_LORA, 128)
ROPE_THETA = 10000.0
EPS = 1e-6
ATTN_SCALE = (B_NOPE + B_ROPE) ** -0.5
LOG2E = 1.4426950408889634
SCALE_LOG2E = ATTN_SCALE * LOG2E

ADAM_LR = 0.001
ADAM_B1 = 0.9
ADAM_B2 = 0.999
ADAM_EPS = 1e-08
ADAM_WD = 0.01
ADAM_STEP = 10

VMEM_LIMIT_BYTES = 60 * 1024 * 1024
N_CHIPS = 4
SMALL_ROWS = 56


def _params(n_axes):
    return pltpu.CompilerParams(dimension_semantics=("arbitrary",) * n_axes,
                                vmem_limit_bytes=VMEM_LIMIT_BYTES)


def _dot(a, b):
    return jnp.dot(a, b, preferred_element_type=F32)


def _dot_nt(a, b):
    return lax.dot_general(a, b, (((1,), (1,)), ((), ())), preferred_element_type=F32)


def _dot_tn(a, b):
    return lax.dot_general(a, b, (((0,), (0,)), ((), ())), preferred_element_type=F32)


def _split3(x):
    x1 = x.astype(BF16)
    r = x - x1.astype(F32)
    x2 = r.astype(BF16)
    x3 = (r - x2.astype(F32)).astype(BF16)
    return x1, x2, x3


def _exact_right(x, m16):
    x1, x2, x3 = _split3(x)
    return _dot(x1, m16) + _dot(x2, m16) + _dot(x3, m16)


def _iota2(shape, dim):
    return lax.broadcasted_iota(jnp.int32, shape, dim)


def _sigmoid(x):
    return jax.nn.sigmoid(x)


def _pick(dim, cap, mult=128):
    if dim <= cap:
        return dim
    best = None
    for d in range(mult, cap + 1, mult):
        if dim % d == 0:
            best = d
    assert best is not None, (dim, cap, mult)
    return best


def _row_tile(t, cap=256):
    return _pick(t, cap, 8)


ROW_GROUP = 256


def _row_groups(tm):
    size = min(tm, ROW_GROUP)
    return [slice(r, r + size) for r in range(0, tm, size)]


def _full(shape, single=False):
    if single:
        return pl.BlockSpec(shape, lambda *_: (0,) * len(shape), pipeline_mode=pl.Buffered(1))
    return pl.BlockSpec(shape, lambda *_: (0,) * len(shape))


def _rows(tm, width):
    return pl.BlockSpec((tm, width), lambda i: (i, 0))


def _acc_rows(ref, val, first):
    s = jnp.sum(val, axis=0, keepdims=True)

    @pl.when(first)
    def _():
        ref[...] = s

    @pl.when(jnp.logical_not(first))
    def _():
        ref[...] += s


def _in_fwd(x, g1, w_in_p):
    t = x.shape[0]
    tm = _row_tile(t, 2 * ROW_GROUP)

    def body(x_ref, g_ref, w_ref, h_ref, *outs):
        gain = g_ref[...]

        def group(xv):
            r = lax.rsqrt(jnp.mean(xv * xv, axis=-1, keepdims=True) + EPS)
            h = ((xv * r) * gain).astype(BF16)
            proj, off = [], 0
            for w in IN_WIDTHS:
                proj.append(_dot_nt(h, w_ref[off:off + w, :]))
                off += w
            return h, proj

        groups = _row_groups(tm)
        done = [group(xv) for xv in [x_ref[g, :] for g in groups]]
        for g, (h, proj) in zip(groups, done):
            h_ref[g, :] = h
            for o_ref, p in zip(outs, proj):
                o_ref[g, :] = p

    return pl.pallas_call(
        body, name="in_fwd", grid=(t // tm,),
        in_specs=[_rows(tm, D_MODEL), _full((1, D_MODEL)), _full((D_IN_PAD, D_MODEL))],
        out_specs=[_rows(tm, D_MODEL)] + [_rows(tm, w) for w in IN_WIDTHS],
        out_shape=[jax.ShapeDtypeStruct((t, D_MODEL), BF16)]
        + [jax.ShapeDtypeStruct((t, w), F32) for w in IN_WIDTHS],
        compiler_params=_params(1),
    )(x, g1, w_in_p)


def _in_bwd(dq_f, dq_r, dv_f, dv_r, dz_f, dz_r, dhg, hq, dcq, dckv, dkr, dx2, x, g1, w_in_p):
    t = x.shape[0]
    tm = _row_tile(t, 2 * ROW_GROUP)

    def body(dqf_ref, dqr_ref, dvf_ref, dvr_ref, dzf_ref, dzr_ref, dhg_ref, hq_ref, dcq_ref, dckv_ref,
             dkr_ref, dx2_ref, x_ref, g_ref, w_ref, dx_ref, dp_ref, dg_ref):
        gain = g_ref[...]

        def group(dqf, dqr, dvf, dvr, dzf, dzr, dhg, hqv, dcq, dckv, dkr, dx2, xv):
            sg = _sigmoid(hqv)
            dhq = (dqf + dqr) * (sg * (1.0 + hqv * (1.0 - sg)))
            pieces = [p.astype(BF16) for p in (dhq, dvf + dvr, dzf, dzr, dhg, dcq, dckv, dkr)]
            dh, off = None, 0
            for p16, w in zip(pieces, IN_WIDTHS):
                part = _dot(p16, w_ref[off:off + w, :])
                dh = part if dh is None else dh + part
                off += w
            r = lax.rsqrt(jnp.mean(xv * xv, axis=-1, keepdims=True) + EPS)
            xh = xv * r
            dxh = dh * gain
            return dx2 + r * (dxh - xh * jnp.mean(dxh * xh, axis=-1, keepdims=True)), pieces, dh * xh

        ins = (dqf_ref, dqr_ref, dvf_ref, dvr_ref, dzf_ref, dzr_ref, dhg_ref, hq_ref, dcq_ref, dckv_ref, dkr_ref,
               dx2_ref, x_ref)
        groups = _row_groups(tm)
        done = [group(*vals) for vals in [[ref[g, :] for ref in ins] for g in groups]]
        for g, (dx, pieces, _) in zip(groups, done):
            dx_ref[g, :] = dx
            off = 0
            for p16, w in zip(pieces, IN_WIDTHS):
                dp_ref[g, off:off + w] = p16
                off += w
        _acc_rows(dg_ref, jnp.concatenate([d[2] for d in done], axis=0), pl.program_id(0) == 0)

    a512 = _rows(tm, A_WIDTH)
    return pl.pallas_call(
        body, name="in_bwd", grid=(t // tm,),
        in_specs=[a512] * 8 + [_rows(tm, Q_LORA), _rows(tm, KV_LORA), _rows(tm, 128), _rows(tm, D_MODEL),
                               _rows(tm, D_MODEL), _full((1, D_MODEL)), _full((D_IN_PAD, D_MODEL))],
        out_specs=[_rows(tm, D_MODEL), _rows(tm, D_IN_PAD), _full((1, D_MODEL))],
        out_shape=[jax.ShapeDtypeStruct((t, D_MODEL), F32), jax.ShapeDtypeStruct((t, D_IN_PAD), BF16),
                   jax.ShapeDtypeStruct((1, D_MODEL), F32)],
        compiler_params=_params(1),
    )(dq_f, dq_r, dv_f, dv_r, dz_f, dz_r, dhg, hq, dcq, dckv, dkr, dx2, x, g1, w_in_p)


def _lower_bound(lbl_ref, direction):
    l0 = lbl_ref[direction, 0:1, :]
    l1 = lbl_ref[direction, 1:2, :]
    m = jnp.maximum(l0, l1)
    e0 = jnp.exp(l0 - m)
    e1 = jnp.exp(l1 - m)
    return e0 / (e0 + e1)


def _hgrn_consts(rb, reverse):
    row = _iota2((rb, rb), 0)
    col = _iota2((rb, rb), 1)
    same = (row // CHUNK) == (col // CHUNK)
    tri = jnp.logical_and(same, (col >= row) if reverse else (col <= row))
    tri_t = jnp.logical_and(same, (col <= row) if reverse else (col >= row))
    r128 = _iota2((128, 128), 0)
    c128 = _iota2((128, 128), 1)
    bd = (r128 < 64) == (c128 < 64)
    lane = _iota2((1, 128), 1)
    m0 = (lane < 64).astype(F32)
    return tri, tri_t, bd, (m0, 1.0 - m0)


def _per_chunk(x, fn):
    n = x.shape[0] // CHUNK
    return jnp.concatenate([jnp.broadcast_to(fn(x[c * CHUNK:(c + 1) * CHUNK]), (CHUNK, x.shape[1])) for c in range(n)],
                           axis=0)


def _chunk_cumsum(x, reverse):
    rb = x.shape[0]
    pos = _iota2(x.shape, 0) % CHUNK
    step = 1
    while step < CHUNK:
        if reverse:
            x = x + jnp.where(pos < CHUNK - step, pltpu.roll(x, rb - step, 0), 0.0)
        else:
            x = x + jnp.where(pos >= step, pltpu.roll(x, step, 0), 0.0)
        step *= 2
    return x


def _hgrn_block(z, hqv, lb, reverse):
    sig = _sigmoid(z)
    sn = _sigmoid(-z)
    q = hqv * _sigmoid(hqv)
    f = lb + (1.0 - lb) * sig
    k = (1.0 - lb) * sn
    lf = jnp.log(f)
    cum = _chunk_cumsum(lf, reverse)
    last = _per_chunk(cum, (lambda a: a[0:1]) if reverse else (lambda a: a[CHUNK - 1:CHUNK]))
    e_neg = jnp.exp(-cum)
    e_end = jnp.exp(last - cum)
    a = jnp.exp(cum)
    return dict(sig=sig, sn=sn, q=q, f=f, k=k, a=a, e_neg=e_neg, e_end=e_end,
                q_dec=q * a, k_inv=k * e_neg, k_end=k * e_end, d=jnp.exp(last))


def _hgrn_dims(t, n_batch):
    s = t // n_batch
    rb = _pick(s, 256, CHUNK)
    return s, rb, s // rb, rb // CHUNK


def _hgrn_fwd(hq, hi, hf, lbl, *, n_batch, direction):
    t = hq.shape[0]
    reverse = direction == 1
    s, rb, nb, nc = _hgrn_dims(t, n_batch)

    def tmap(b, j):
        return (b * nb + ((nb - 1 - j) if reverse else j), 0)

    def smap(b, j):
        return (b * nb + ((nb - 1 - j) if reverse else j), 0, 0, 0)

    def body(hq_ref, hi_ref, hf_ref, lbl_ref, o_ref, st_ref, st_scr):
        @pl.when(pl.program_id(1) == 0)
        def _():
            st_scr[...] = jnp.zeros_like(st_scr)

        lb_all = _lower_bound(lbl_ref, direction)
        tri, _, bd, masks = _hgrn_consts(rb, reverse)
        order = range(nc - 1, -1, -1) if reverse else range(nc)

        pairs = [slice(p * 128, (p + 1) * 128) for p in range(HEAD_PAIRS)]
        chunks = [slice(c * CHUNK, (c + 1) * CHUNK) for c in range(nc)]
        w = _hgrn_block(hf_ref[...], hq_ref[...], lb_all, reverse)
        v16 = hi_ref[...].astype(BF16)
        qd16 = w["q_dec"].astype(BF16)
        ki16 = w["k_inv"].astype(BF16)
        ke16 = w["k_end"].astype(BF16)
        sc = [[_dot_nt(jnp.where(mh > 0.0, qd16[:, ls], 0.0).astype(BF16), ki16[:, ls]) for mh in masks] for ls in pairs]
        pv = [[_dot(jnp.where(tri, s_e, 0.0).astype(BF16), v16[:, ls]) for s_e in sc_p] for sc_p, ls in zip(sc, pairs)]
        o_intra = [jnp.where(masks[0] > 0.0, pv_p[0], pv_p[1]) for pv_p in pv]
        ut = [[jnp.where(bd, _dot_tn(v16[rs, ls], ke16[rs, ls]), 0.0) for ls in pairs] for rs in chunks]
        st = [st_scr[p] for p in range(HEAD_PAIRS)]
        for c in order:
            rs = chunks[c]
            inter = [_dot_nt(qd16[rs, ls], st[p].astype(BF16)) for p, ls in enumerate(pairs)]
            for p, ls in enumerate(pairs):
                o_ref[rs, ls] = o_intra[p][rs] + inter[p]
                st_ref[c, p] = st[p]
                st[p] = st[p] * w["d"][c * CHUNK:c * CHUNK + 1, ls] + ut[c][p]
        for p in range(HEAD_PAIRS):
            st_scr[p] = st[p]

    blk = pl.BlockSpec((rb, A_WIDTH), tmap)
    return pl.pallas_call(
        body, name=f"hgrn_fwd_{direction}", grid=(n_batch, nb),
        in_specs=[blk, blk, blk, _full((2, 2, A_WIDTH))],
        out_specs=[blk, pl.BlockSpec((nc, HEAD_PAIRS, 128, 128), smap)],
        out_shape=[jax.ShapeDtypeStruct((t, A_WIDTH), F32),
                   jax.ShapeDtypeStruct((t // CHUNK, HEAD_PAIRS, 128, 128), F32)],
        scratch_shapes=[pltpu.VMEM((HEAD_PAIRS, 128, 128), F32)],
        compiler_params=_params(2),
    )(hq, hi, hf, lbl)


def _hgrn_bwd(hq, hi, hf, do, st, lbl, *, n_batch, direction):
    t = hq.shape[0]
    reverse = direction == 1
    s, rb, nb, nc = _hgrn_dims(t, n_batch)

    def tmap(b, j):
        return (b * nb + (j if reverse else (nb - 1 - j)), 0)

    def smap(b, j):
        return (b * nb + (j if reverse else (nb - 1 - j)), 0, 0, 0)

    def body(hq_ref, hi_ref, hf_ref, do_ref, st_ref, lbl_ref, dq_ref, dv_ref, dz_ref, dl_ref, g_scr, dlb_scr):
        b = pl.program_id(0)
        j = pl.program_id(1)

        @pl.when(jnp.logical_and(b == 0, j == 0))
        def _():
            dlb_scr[...] = jnp.zeros_like(dlb_scr)

        @pl.when(j == 0)
        def _():
            g_scr[...] = jnp.zeros_like(g_scr)

        lb_all = _lower_bound(lbl_ref, direction)
        tri, tri_t, bd, masks = _hgrn_consts(rb, reverse)
        order = range(nc) if reverse else range(nc - 1, -1, -1)

        chunks = [slice(c * CHUNK, (c + 1) * CHUNK) for c in range(nc)]

        def lanes(per_pair):
            return jnp.concatenate(per_pair, axis=1)

        for first_pair in range(0, HEAD_PAIRS, PAIRS_AT_ONCE):
            here = slice(first_pair * 128, (first_pair + PAIRS_AT_ONCE) * 128)
            pairs = [slice(p * 128, (p + 1) * 128) for p in range(PAIRS_AT_ONCE)]
            lb = lb_all[:, here]
            w = _hgrn_block(hf_ref[:, here], hq_ref[:, here], lb, reverse)
            dov = do_ref[:, here]
            v16 = hi_ref[:, here].astype(BF16)
            do16 = dov.astype(BF16)
            qd16 = w["q_dec"].astype(BF16)
            ki16 = w["k_inv"].astype(BF16)
            ke16 = w["k_end"].astype(BF16)
            qm16 = [[jnp.where(mh > 0.0, qd16[:, ls], 0.0).astype(BF16) for mh in masks] for ls in pairs]
            dom16 = [[jnp.where(mh > 0.0, do16[:, ls], 0.0).astype(BF16) for mh in masks] for ls in pairs]
            heads = [(p, e) for p in range(PAIRS_AT_ONCE) for e in range(2)]
            dp = {(p, e): _dot_nt(dom16[p][e], v16[:, pairs[p]]) for p, e in heads}
            pm_t = {(p, e): _dot_nt(ki16[:, pairs[p]], qm16[p][e]) for p, e in heads}
            dp_t = {(p, e): _dot_nt(v16[:, pairs[p]], dom16[p][e]) for p, e in heads}
            dp = {h: jnp.where(tri, a, 0.0).astype(BF16) for h, a in dp.items()}
            pm_t = {h: jnp.where(tri_t, a, 0.0).astype(BF16) for h, a in pm_t.items()}
            dp_t = {h: jnp.where(tri_t, a, 0.0).astype(BF16) for h, a in dp_t.items()}
            dv_e = {(p, e): _dot(pm_t[p, e], do16[:, pairs[p]]) for p, e in heads}
            dq_e = {(p, e): _dot(dp[p, e], ki16[:, pairs[p]]) for p, e in heads}
            dk_e = {(p, e): _dot(dp_t[p, e], qd16[:, pairs[p]]) for p, e in heads}
            st = [[st_ref[c, first_pair + p] for p in range(PAIRS_AT_ONCE)] for c in range(nc)]
            dq_x = [[_dot(do16[rs, ls], st[c][p].astype(BF16)) for p, ls in enumerate(pairs)] for c, rs in enumerate(chunks)]
            gq = [[jnp.where(bd, _dot_tn(do16[rs, ls], qd16[rs, ls]), 0.0) for ls in pairs] for rs in chunks]
            g = [g_scr[first_pair + p] for p in range(PAIRS_AT_ONCE)]
            dk_end, dv_x, dd = [None] * nc, [None] * nc, [None] * nc
            for c in order:
                rs = chunks[c]
                g16 = [a.astype(BF16) for a in g]
                dk_end[c] = lanes([_dot(v16[rs, ls], g16[p]) for p, ls in enumerate(pairs)])
                dv_x[c] = lanes([_dot_nt(ke16[rs, ls], g16[p]) for p, ls in enumerate(pairs)])
                dd[c] = jnp.broadcast_to(lanes([jnp.sum(g[p] * st[c][p], axis=0, keepdims=True) for p in range(PAIRS_AT_ONCE)]),
                                         (CHUNK, PAIRS_AT_ONCE * 128))
                for p, ls in enumerate(pairs):
                    g[p] = g[p] * w["d"][c * CHUNK:c * CHUNK + 1, ls] + gq[c][p]
            for p in range(PAIRS_AT_ONCE):
                g_scr[first_pair + p] = g[p]

            def both_heads(d):
                return lanes([jnp.where(masks[0] > 0.0, d[p, 0], d[p, 1]) for p in range(PAIRS_AT_ONCE)])

            dq_dec = both_heads(dq_e) + jnp.concatenate([lanes(a) for a in dq_x], axis=0)
            dk_inv = both_heads(dk_e)
            dk_end = jnp.concatenate(dk_end, axis=0)
            dv = both_heads(dv_e) + jnp.concatenate(dv_x, axis=0)
            dd = jnp.concatenate(dd, axis=0)
            dke = dk_end * w["k_end"]
            dcum = dq_dec * w["q_dec"] - dk_inv * w["k_inv"] - dke
            dk = dk_inv * w["e_neg"] + dk_end * w["e_end"]
            dlast = _per_chunk(dke, lambda a: jnp.sum(a, axis=0, keepdims=True)) + dd * w["d"]
            dlf = _chunk_cumsum(dcum, not reverse) + dlast
            tt = dlf / w["f"] - dk
            dq_ref[:, here] = dq_dec * w["a"]
            dv_ref[:, here] = dv
            dz_ref[:, here] = ((1.0 - lb) * w["sig"] * w["sn"] * tt).astype(BF16)
            dlb_scr[:, here] += jnp.sum(w["sn"] * tt, axis=0, keepdims=True)

        @pl.when(jnp.logical_and(b == pl.num_programs(0) - 1, j == pl.num_programs(1) - 1))
        def _():
            d0 = dlb_scr[...] * lb_all * (1.0 - lb_all)
            dl_ref[0:1, :] = d0
            dl_ref[1:2, :] = -d0

    blk = pl.BlockSpec((rb, A_WIDTH), tmap)
    return pl.pallas_call(
        body, name=f"hgrn_bwd_{direction}", grid=(n_batch, nb),
        in_specs=[blk, blk, blk, blk, pl.BlockSpec((nc, HEAD_PAIRS, 128, 128), smap), _full((2, 2, A_WIDTH))],
        out_specs=[blk, blk, blk, _full((2, A_WIDTH))],
        out_shape=[jax.ShapeDtypeStruct((t, A_WIDTH), F32)] * 2
        + [jax.ShapeDtypeStruct((t, A_WIDTH), BF16), jax.ShapeDtypeStruct((2, A_WIDTH), F32)],
        scratch_shapes=[pltpu.VMEM((HEAD_PAIRS, 128, 128), F32), pltpu.VMEM((1, A_WIDTH), F32)],
        compiler_params=_params(2),
    )(hq, hi, hf, do, st, lbl)


def _swap_rope_halves(x):
    lane = _iota2(x.shape, 1)
    return jnp.where(lane < 32, pltpu.roll(x, 96, 1), pltpu.roll(x, 32, 1))


def _rms_fwd(xv, g):
    r = lax.rsqrt(jnp.mean(xv * xv, axis=-1, keepdims=True) + EPS)
    return (xv * r) * g


def _rms_bwd(dy, xv, g):
    r = lax.rsqrt(jnp.mean(xv * xv, axis=-1, keepdims=True) + EPS)
    xh = xv * r
    dxh = dy * g
    return r * (dxh - xh * jnp.mean(dxh * xh, axis=-1, keepdims=True)), dy * xh


def _mla_prep(cq, ckv, kr, g_qa, g_kva, w_q_p, w_kv_p, cosx, sinx, *, n_batch):
    t = cq.shape[0]
    s = t // n_batch
    tm = _row_tile(s, 512)
    nt = s // tm

    def body(cq_ref, ckv_ref, kr_ref, gq_ref, gkv_ref, wq_ref, wkv_ref, cos_ref, sin_ref,
             q_ref, k_ref, v_ref, cqn_ref, ckvn_ref):
        cos, sin = cos_ref[...], sin_ref[...]
        cqn = _rms_fwd(cq_ref[...], gq_ref[...]).astype(BF16)
        ckvn = _rms_fwd(ckv_ref[...], gkv_ref[...]).astype(BF16)
        cqn_ref[...] = cqn
        ckvn_ref[...] = ckvn
        krv = kr_ref[...]
        kr_roped = (krv * cos + _swap_rope_halves(krv) * sin).astype(BF16)
        for h in range(B_HEADS):
            o = h * QK_PAD
            q_ref[:, o:o + 128] = _dot_nt(cqn, wq_ref[o:o + 128, :]).astype(BF16)
            qr = _dot_nt(cqn, wq_ref[o + 128:o + 256, :])
            q_ref[:, o + 128:o + 256] = (qr * cos + _swap_rope_halves(qr) * sin).astype(BF16)
            k_ref[:, o:o + 128] = _dot(ckvn, wkv_ref[:, h * 128:(h + 1) * 128]).astype(BF16)
            k_ref[:, o + 128:o + 256] = kr_roped
        v_ref[...] = _dot(ckvn, wkv_ref[:, 512:1024]).astype(BF16)

    tab = pl.BlockSpec((tm, 128), lambda i: (i % nt, 0))
    return pl.pallas_call(
        body, name="mla_prep", grid=(t // tm,),
        in_specs=[_rows(tm, Q_LORA), _rows(tm, KV_LORA), _rows(tm, 128), _full((1, Q_LORA)), _full((1, KV_LORA)),
                  _full((1024, Q_LORA)), _full((KV_LORA, 1024)), tab, tab],
        out_specs=[_rows(tm, 1024), _rows(tm, 1024), _rows(tm, 512), _rows(tm, Q_LORA), _rows(tm, KV_LORA)],
        out_shape=[jax.ShapeDtypeStruct((t, 1024), BF16), jax.ShapeDtypeStruct((t, 1024), BF16),
                   jax.ShapeDtypeStruct((t, 512), BF16), jax.ShapeDtypeStruct((t, Q_LORA), BF16),
                   jax.ShapeDtypeStruct((t, KV_LORA), BF16)],
        compiler_params=_params(1),
    )(cq, ckv, kr, g_qa, g_kva, w_q_p, w_kv_p, cosx, sinx)


def _mla_prep_bwd(dqt, dk, dv, cq, ckv, g_qa, g_kva, w_q_p, w_kv_p, cosx, sinx, *, n_batch):
    t = cq.shape[0]
    s = t // n_batch
    tm = _row_tile(s, 512)
    nt = s // tm

    def body(dqt_ref, dk_ref, dv_ref, cq_ref, ckv_ref, gq_ref, gkv_ref, wq_ref, wkv_ref, cos_ref, sin_ref,
             dcq_ref, dckv_ref, dkr_ref, dqp_ref, dkvp_ref, dgq_ref, dgkv_ref):
        cos, sin = cos_ref[...], sin_ref[...]
        first = pl.program_id(0) == 0

        def unrope(d):
            return d * cos + _swap_rope_halves(d * sin)

        dcqn = None
        dkr = None
        dckvn = None
        for h in range(B_HEADS):
            o = h * QK_PAD
            dq_h = jnp.transpose(dqt_ref[o:o + QK_PAD, :])
            dqn16 = dq_h[:, 0:128].astype(BF16)
            dqr16 = unrope(dq_h[:, 128:256]).astype(BF16)
            dqp_ref[:, o:o + 128] = dqn16
            dqp_ref[:, o + 128:o + 256] = dqr16
            part = _dot(dqn16, wq_ref[o:o + 128, :]) + _dot(dqr16, wq_ref[o + 128:o + 256, :])
            dcqn = part if dcqn is None else dcqn + part
            dkn16 = dk_ref[:, o:o + 128].astype(BF16)
            dkvp_ref[:, h * 128:(h + 1) * 128] = dkn16
            part = _dot_nt(dkn16, wkv_ref[:, h * 128:(h + 1) * 128])
            dckvn = part if dckvn is None else dckvn + part
            kr_part = dk_ref[:, o + 128:o + 256]
            dkr = kr_part if dkr is None else dkr + kr_part
        dv16 = dv_ref[...].astype(BF16)
        dkvp_ref[:, 512:1024] = dv16
        dckvn = dckvn + _dot_nt(dv16, wkv_ref[:, 512:1024])
        dkr_ref[...] = unrope(dkr).astype(BF16)
        dcq, dgq = _rms_bwd(dcqn, cq_ref[...], gq_ref[...])
        dckv, dgkv = _rms_bwd(dckvn, ckv_ref[...], gkv_ref[...])
        dcq_ref[...] = dcq.astype(BF16)
        dckv_ref[...] = dckv.astype(BF16)
        _acc_rows(dgq_ref, dgq, first)
        _acc_rows(dgkv_ref, dgkv, first)

    tab = pl.BlockSpec((tm, 128), lambda i: (i % nt, 0))
    return pl.pallas_call(
        body, name="mla_prep_bwd", grid=(t // tm,),
        in_specs=[pl.BlockSpec((1024, tm), lambda i: (0, i)), _rows(tm, 1024), _rows(tm, 512), _rows(tm, Q_LORA),
                  _rows(tm, KV_LORA),
                  _full((1, Q_LORA)), _full((1, KV_LORA)), _full((1024, Q_LORA)), _full((KV_LORA, 1024)), tab, tab],
        out_specs=[_rows(tm, Q_LORA), _rows(tm, KV_LORA), _rows(tm, 128), _rows(tm, 1024), _rows(tm, 1024),
                   _full((1, Q_LORA)), _full((1, KV_LORA))],
        out_shape=[jax.ShapeDtypeStruct((t, Q_LORA), BF16), jax.ShapeDtypeStruct((t, KV_LORA), BF16),
                   jax.ShapeDtypeStruct((t, 128), BF16), jax.ShapeDtypeStruct((t, 1024), BF16),
                   jax.ShapeDtypeStruct((t, 1024), BF16), jax.ShapeDtypeStruct((1, Q_LORA), F32),
                   jax.ShapeDtypeStruct((1, KV_LORA), F32)],
        compiler_params=_params(1),
    )(dqt, dk, dv, cq, ckv, g_qa, g_kva, w_q_p, w_kv_p, cosx, sinx)


def _attn_dims(t, n_batch):
    s = t // n_batch
    tq = _pick(s, 1024, 128)
    return s, tq, s // tq


def _grid_ends(n_axes):
    ids = [pl.program_id(a) for a in range(n_axes)]
    first = functools.reduce(jnp.logical_and, [i == 0 for i in ids])
    last = functools.reduce(jnp.logical_and, [i == pl.num_programs(a) - 1 for a, i in enumerate(ids)])
    return first, last


def _attn_fwd(q, k, v, wsrcs, *, n_batch):
    t = q.shape[0]
    s, tq, nq = _attn_dims(t, n_batch)
    nw = len(wsrcs)

    def body(q_ref, k_ref, v_ref, *refs):
        w_refs, (o_ref, lse_ref), wall_refs = refs[:nw], refs[nw:nw + 2], refs[nw + 2:2 * nw + 2]
        send_sems, recv_sems, local_sems = refs[2 * nw + 2:]
        first, last = _grid_ends(3)

        @pl.when(first)
        def _():
            for i in range(nw):
                _gather_start(w_refs[i], wall_refs[i], send_sems, recv_sems, local_sems.at[i], 3 * i)

        @pl.when(last)
        def _():
            for i in range(nw):
                _gather_wait(w_refs[i], wall_refs[i], send_sems, recv_sems, local_sems.at[i], 3 * i)

        kv, vv = k_ref[...], v_ref[...]
        groups = _row_groups(tq)
        raw = [_dot_nt(q_ref[g, :], kv) for g in groups]
        m = [jnp.max(a, axis=-1, keepdims=True) for a in raw]
        p = [jnp.exp2((a - b) * SCALE_LOG2E) for a, b in zip(raw, m)]
        l = [jnp.sum(a, axis=-1, keepdims=True) for a in p]
        for g, pg, mg, lg in zip(groups, p, m, l):
            o_ref[g, :] = _dot(pg.astype(BF16), vv) / lg
            lse2 = mg * SCALE_LOG2E + jnp.log(lg) * LOG2E
            lse_ref[:, g] = jnp.transpose(jnp.broadcast_to(lse2, (lse2.shape[0], 128)))[0:1, :]

    return pl.pallas_call(
        body, name="attn_fwd", grid=(n_batch, B_HEADS, nq),
        in_specs=[pl.BlockSpec((tq, QK_PAD), lambda b, h, i: (b * nq + i, h)),
                  pl.BlockSpec((s, QK_PAD), lambda b, h, i: (b, h)),
                  pl.BlockSpec((s, B_V), lambda b, h, i: (b, h))] + [_HBM] * nw,
        out_specs=[pl.BlockSpec((tq, B_V), lambda b, h, i: (b * nq + i, h)),
                   pl.BlockSpec((None, 1, tq), lambda b, h, i: (h, 0, b * nq + i))] + [_HBM] * nw,
        out_shape=[jax.ShapeDtypeStruct((t, B_HEADS * B_V), F32), jax.ShapeDtypeStruct((B_HEADS, 1, t), F32)]
        + [jax.ShapeDtypeStruct((N_CHIPS,) + w.shape, w.dtype) for w in wsrcs],
        scratch_shapes=[pltpu.SemaphoreType.DMA((3 * nw,)), pltpu.SemaphoreType.DMA((3 * nw,)),
                        pltpu.SemaphoreType.DMA((nw,))],
        compiler_params=_params(3),
    )(q, k, v, *wsrcs)


def _attn_bwd(q, k, v, do16, lse, delta, gsegs, *, n_batch):
    t = q.shape[0]
    s = t // n_batch
    tk = _pick(s, 512, 128)
    nk = s // tk
    ng = len(gsegs)

    def body(q_ref, k_ref, v_ref, do_ref, lse_ref, dl_ref, *refs):
        g_refs, (dqt_ref, dk_ref, dv_ref), recv_refs = refs[:ng], refs[ng:ng + 3], refs[ng + 3:2 * ng + 3]
        send_sems, recv_sems = refs[2 * ng + 3:]
        first, last = _grid_ends(3)

        @pl.when(first)
        def _():
            for i in range(ng):
                _scatter_start(g_refs[i], recv_refs[i], send_sems, recv_sems, 3 * i)

        @pl.when(last)
        def _():
            for i in range(ng):
                _scatter_wait(g_refs[i], recv_refs[i], send_sems, recv_sems, 3 * i)

        j = pl.program_id(2)
        qv, kv, dov = q_ref[...], k_ref[...], do_ref[...]
        pt = jnp.exp2(_dot_nt(kv, qv) * SCALE_LOG2E - lse_ref[...])
        dv_ref[...] = _dot(pt.astype(BF16), dov)
        dpt = _dot_nt(v_ref[...], dov)
        dst = (pt * (dpt - dl_ref[...])).astype(BF16)
        dk_ref[...] = _dot(dst, qv) * ATTN_SCALE
        part = _dot_tn(kv, dst)

        @pl.when(j == 0)
        def _():
            dqt_ref[...] = part

        @pl.when(j != 0)
        def _():
            dqt_ref[...] += part

        @pl.when(j == nk - 1)
        def _():
            dqt_ref[...] = dqt_ref[...] * ATTN_SCALE

    row = pl.BlockSpec((None, 1, s), lambda b, h, j: (h, 0, b))
    return pl.pallas_call(
        body, name="attn_bwd", grid=(n_batch, B_HEADS, nk),
        in_specs=[pl.BlockSpec((s, QK_PAD), lambda b, h, j: (b, h)),
                  pl.BlockSpec((tk, QK_PAD), lambda b, h, j: (b * nk + j, h)),
                  pl.BlockSpec((tk, B_V), lambda b, h, j: (b * nk + j, h)),
                  pl.BlockSpec((s, B_V), lambda b, h, j: (b, h)), row, row] + [_HBM] * ng,
        out_specs=[pl.BlockSpec((QK_PAD, s), lambda b, h, j: (h, b)),
                   pl.BlockSpec((tk, QK_PAD), lambda b, h, j: (b * nk + j, h)),
                   pl.BlockSpec((tk, B_V), lambda b, h, j: (b * nk + j, h))] + [_HBM] * ng,
        out_shape=[jax.ShapeDtypeStruct((B_HEADS * QK_PAD, t), F32), jax.ShapeDtypeStruct((t, B_HEADS * QK_PAD), F32),
                   jax.ShapeDtypeStruct((t, B_HEADS * B_V), F32)]
        + [jax.ShapeDtypeStruct((3,) + g.shape[1:], g.dtype) for g in gsegs],
        scratch_shapes=[pltpu.SemaphoreType.DMA((3 * ng,)), pltpu.SemaphoreType.DMA((3 * ng,))],
        compiler_params=_params(3),
    )(q, k, v, do16, lse, delta, *gsegs)


def _group_ones16():
    r = _iota2((A_WIDTH, A_WIDTH), 0) // 64
    c = _iota2((A_WIDTH, A_WIDTH), 1) // 64
    return (r == c).astype(BF16)


def _head_rms(o, ones16):
    return lax.rsqrt(_exact_right(o * o, ones16) * (1.0 / 64.0) + EPS)


def _out_fwd(o_f, o_r, hg, o_attn, x, g_hn, g_mla, w_out, g2):
    t = x.shape[0]
    tm = _row_tile(t, 2 * ROW_GROUP)

    def body(of_ref, or_ref, hg_ref, oa_ref, x_ref, ghn_ref, gm_ref, w_ref, g2_ref, y_ref, x2_ref, h2_ref):
        ones16 = _group_ones16()
        ghn, gm, g2v = ghn_ref[...], gm_ref[...], g2_ref[...]

        def group(ofv, orv, hgv, oav, xv):
            o = ofv + orv
            ya16 = (((o * _head_rms(o, ones16)) * ghn) * (hgv * _sigmoid(hgv))).astype(BF16)
            yb16 = _rms_fwd(oav, gm).astype(BF16)
            x2 = xv + _dot(ya16, w_ref[0:A_WIDTH, :]) + _dot(yb16, w_ref[A_WIDTH:D_MODEL, :])
            return ya16, yb16, x2, _rms_fwd(x2, g2v).astype(BF16)

        ins = (of_ref, or_ref, hg_ref, oa_ref, x_ref)
        groups = _row_groups(tm)
        done = [group(*vals) for vals in [[ref[g, :] for ref in ins] for g in groups]]
        for g, (ya16, yb16, x2, h2) in zip(groups, done):
            y_ref[g, 0:A_WIDTH] = ya16
            y_ref[g, A_WIDTH:D_MODEL] = yb16
            x2_ref[g, :] = x2
            h2_ref[g, :] = h2

    a512 = _rows(tm, A_WIDTH)
    return pl.pallas_call(
        body, name="out_fwd", grid=(t // tm,),
        in_specs=[a512, a512, a512, a512, _rows(tm, D_MODEL), _full((1, A_WIDTH)), _full((1, A_WIDTH)),
                  _full((D_MODEL, D_MODEL)), _full((1, D_MODEL))],
        out_specs=[_rows(tm, D_MODEL)] * 3,
        out_shape=[jax.ShapeDtypeStruct((t, D_MODEL), BF16), jax.ShapeDtypeStruct((t, D_MODEL), F32),
                   jax.ShapeDtypeStruct((t, D_MODEL), BF16)],
        compiler_params=_params(1),
    )(o_f, o_r, hg, o_attn, x, g_hn, g_mla, w_out, g2)


def _out_bwd(dx2_16, o_f, o_r, hg, o_attn, g_hn, g_mla, w_out):
    t = dx2_16.shape[0]
    tm = _row_tile(t, 2 * ROW_GROUP)

    def body(dx_ref, of_ref, or_ref, hg_ref, oa_ref, ghn_ref, gm_ref, w_ref,
             do_ref, dhg_ref, doa_ref, dl_ref, dghn_ref, dgm_ref):
        first = pl.program_id(0) == 0
        ones16 = _group_ones16()
        sel16 = (_iota2((8, A_WIDTH), 0) == _iota2((8, A_WIDTH), 1) // B_V).astype(BF16)
        ghn, gm = ghn_ref[...], gm_ref[...]

        def group(dxv, ofv, orv, hgv, oav):
            dya = _dot_nt(dxv, w_ref[0:A_WIDTH, :])
            dyb = _dot_nt(dxv, w_ref[A_WIDTH:D_MODEL, :])
            o = ofv + orv
            rh = _head_rms(o, ones16)
            oh = o * rh
            sg = _sigmoid(hgv)
            sl = hgv * sg
            dhg = ((dya * (oh * ghn)) * (sg * (1.0 + hgv * (1.0 - sg)))).astype(BF16)
            doh = dya * sl * ghn
            do = rh * (doh - oh * (_exact_right(doh * oh, ones16) * (1.0 / 64.0)))
            doa, dgm = _rms_bwd(dyb, oav, gm)
            x1, x2, x3 = _split3(doa * oav)
            delta = _dot_nt(sel16, x1) + _dot_nt(sel16, x2) + _dot_nt(sel16, x3)
            return do, dhg, doa.astype(BF16), delta, dya * sl * oh, dgm

        groups = _row_groups(tm)
        loaded = [(dx_ref[g, :], of_ref[g, :], or_ref[g, :], hg_ref[g, :], oa_ref[g, :]) for g in groups]
        done = [group(*vals) for vals in loaded]
        for g, (do, dhg, doa16, delta, _, _) in zip(groups, done):
            do_ref[g, :] = do
            dhg_ref[g, :] = dhg
            doa_ref[g, :] = doa16
            for h in range(B_HEADS):
                dl_ref[h, :, g] = delta[h:h + 1, :]
        _acc_rows(dghn_ref, jnp.concatenate([d[4] for d in done], axis=0), first)
        _acc_rows(dgm_ref, jnp.concatenate([d[5] for d in done], axis=0), first)

    a512 = _rows(tm, A_WIDTH)
    return pl.pallas_call(
        body, name="out_bwd", grid=(t // tm,),
        in_specs=[_rows(tm, D_MODEL), a512, a512, a512, a512, _full((1, A_WIDTH)), _full((1, A_WIDTH)),
                  _full((D_MODEL, D_MODEL))],
        out_specs=[a512, a512, a512, pl.BlockSpec((B_HEADS, 1, tm), lambda i: (0, 0, i)),
                   _full((1, A_WIDTH)), _full((1, A_WIDTH))],
        out_shape=[jax.ShapeDtypeStruct((t, A_WIDTH), F32)] + [jax.ShapeDtypeStruct((t, A_WIDTH), BF16)] * 2
        + [jax.ShapeDtypeStruct((B_HEADS, 1, t), F32)]
        + [jax.ShapeDtypeStruct((1, A_WIDTH), F32)] * 2,
        compiler_params=_params(1),
    )(dx2_16, o_f, o_r, hg, o_attn, g_hn, g_mla, w_out)


def _ffn_fwd_bwd(h2, x2, target, w_gate, w_up, w_down, g_f, g2):
    t = x2.shape[0]
    tm = _row_tile(t)
    inv_d = 1.0 / D_MODEL

    def body(h2_ref, x2_ref, tg_ref, wg_ref, wu_ref, wd_ref, gf_ref, g2_ref,
             act_ref, dgate_ref, dup_ref, dx3_ref, dx2_ref, dx2h_ref, loss_ref, dgf_ref, dg2_ref):
        first = pl.program_id(0) == 0
        h2v = h2_ref[...]
        gate = _dot_nt(h2v, wg_ref[...])
        up = _dot_nt(h2v, wu_ref[...])
        sg = _sigmoid(gate)
        sl = gate * sg
        act16 = (sl * up).astype(BF16)
        act_ref[...] = act16
        x2v = x2_ref[...]
        x3 = x2v + _dot(act16, wd_ref[...])
        r3 = lax.rsqrt(jnp.mean(x3 * x3, axis=-1, keepdims=True) + EPS)
        x3h = x3 * r3
        gf = gf_ref[...]
        err = x3h * gf - tg_ref[...]
        part = 0.5 * jnp.sum(jnp.mean(err * err, axis=-1, keepdims=True), axis=0, keepdims=True)

        @pl.when(first)
        def _():
            loss_ref[...] = jnp.zeros_like(loss_ref)

        loss_ref[...] += part
        dy = err * inv_d
        _acc_rows(dgf_ref, dy * x3h, first)
        dx3h = dy * gf
        dx3 = r3 * (dx3h - x3h * jnp.mean(dx3h * x3h, axis=-1, keepdims=True))
        dx3_16 = dx3.astype(BF16)
        dx3_ref[...] = dx3_16
        da = _dot_nt(dx3_16, wd_ref[...])
        dup16 = (da * sl).astype(BF16)
        dgate16 = (da * up * (sg * (1.0 + gate * (1.0 - sg)))).astype(BF16)
        dup_ref[...] = dup16
        dgate_ref[...] = dgate16
        dh2 = _dot(dgate16, wg_ref[...]) + _dot(dup16, wu_ref[...])
        dx2n, dg2 = _rms_bwd(dh2, x2v, g2_ref[...])
        _acc_rows(dg2_ref, dg2, first)
        dx2 = dx3 + dx2n
        dx2_ref[...] = dx2
        dx2h_ref[...] = dx2.astype(BF16)

    return pl.pallas_call(
        body, name="ffn_fwd_bwd", grid=(t // tm,),
        in_specs=[_rows(tm, D_MODEL), _rows(tm, D_MODEL), _rows(tm, D_MODEL), _full((D_FF, D_MODEL), True),
                  _full((D_FF, D_MODEL), True), _full((D_FF, D_MODEL), True), _full((1, D_MODEL)), _full((1, D_MODEL))],
        out_specs=[_rows(tm, D_FF), _rows(tm, D_FF), _rows(tm, D_FF), _rows(tm, D_MODEL), _rows(tm, D_MODEL),
                   _rows(tm, D_MODEL), _full((8, 128)), _full((1, D_MODEL)), _full((1, D_MODEL))],
        out_shape=[jax.ShapeDtypeStruct((t, D_FF), BF16)] * 3
        + [jax.ShapeDtypeStruct((t, D_MODEL), BF16), jax.ShapeDtypeStruct((t, D_MODEL), F32),
           jax.ShapeDtypeStruct((t, D_MODEL), BF16), jax.ShapeDtypeStruct((8, 128), F32),
           jax.ShapeDtypeStruct((1, D_MODEL), F32), jax.ShapeDtypeStruct((1, D_MODEL), F32)],
        compiler_params=_params(1),
    )(h2, x2, target, w_gate, w_up, w_down, g_f, g2)


def _wgrad(a, b, name):
    t, m = a.shape
    n = b.shape[1]
    bm = _pick(m, 1664)
    bn = _pick(n, 1664)
    tk = _pick(t, 2048, 16)
    nk = t // tk

    def body(a_ref, b_ref, o_ref, o16_ref):
        k = pl.program_id(2)
        part = _dot_tn(a_ref[...], b_ref[...])

        @pl.when(k == 0)
        def _():
            o_ref[...] = part

        @pl.when(k != 0)
        def _():
            o_ref[...] += part

        @pl.when(k == nk - 1)
        def _():
            o16_ref[...] = o_ref[...].astype(BF16)

    out = pl.BlockSpec((bm, bn), lambda i, j, k: (i, j))
    return pl.pallas_call(
        body, name=name, grid=(m // bm, n // bn, nk),
        in_specs=[pl.BlockSpec((tk, bm), lambda i, j, k: (k, i)), pl.BlockSpec((tk, bn), lambda i, j, k: (k, j))],
        out_specs=[out, out],
        out_shape=[jax.ShapeDtypeStruct((m, n), F32), jax.ShapeDtypeStruct((m, n), BF16)],
        compiler_params=_params(3),
    )(a, b)


def _rope_tables(seq):
    inv = 1.0 / (ROPE_THETA ** (jnp.arange(0, B_ROPE, 2, dtype=F32) / B_ROPE))
    ang = jnp.arange(seq, dtype=F32)[:, None] * inv[None, :]
    cos, sin = jnp.cos(ang), jnp.sin(ang)
    zeros = jnp.zeros((seq, 64), F32)
    return jnp.concatenate([cos, cos, zeros], axis=1), jnp.concatenate([-sin, sin, zeros], axis=1)


def _pad_weights(w_in_t, w_q_t, w_kv_b):
    w_in_p = jnp.pad(w_in_t, ((0, D_IN_PAD - D_IN), (0, 0)))
    w_q_p = jnp.pad(w_q_t.reshape(B_HEADS, B_NOPE + B_ROPE, Q_LORA), ((0, 0), (0, 64), (0, 0))).reshape(1024, Q_LORA)
    kv = w_kv_b.reshape(KV_LORA, B_HEADS, B_NOPE + B_V)
    w_kv_p = jnp.concatenate([kv[:, :, :B_NOPE].reshape(KV_LORA, 512), kv[:, :, B_NOPE:].reshape(KV_LORA, 512)], axis=1)
    return w_in_p, w_q_p, w_kv_p


def _unpad_kv(g_kv_p):
    return jnp.concatenate([g_kv_p[:, :512].reshape(KV_LORA, B_HEADS, B_NOPE),
                            g_kv_p[:, 512:].reshape(KV_LORA, B_HEADS, B_V)], axis=2).reshape(KV_LORA, 1024)


_OWN_ROWS = dict(w_in=(272, 3), w_q_b=(64, 4), w_kv_b=(256, 0), w_out=(256, 1), w_gate=(176, 4), w_up=(176, 4), w_down=(176, 4))


def _local_step(x, target, lbl, g1, g_hn, g_qa, g_kva, g_mla, g2, g_f, w_in, w_q_b, w_kv_b, late_shards):
    n_batch, seq, _ = x.shape
    t = n_batch * seq
    x = x.reshape(t, D_MODEL)
    target = target.reshape(t, D_MODEL)
    w_in_p, w_q_p, w_kv_p = _pad_weights(w_in, w_q_b, w_kv_b)
    cosx, sinx = _rope_tables(seq)

    h1, hq, hi, hff, hfb, hg, cq, ckv, kr = _in_fwd(x, g1, w_in_p)
    o_f, st_f = _hgrn_fwd(hq, hi, hff, lbl, n_batch=n_batch, direction=0)
    o_r, st_r = _hgrn_fwd(hq, hi, hfb, lbl, n_batch=n_batch, direction=1)
    q, k, v, cqn, ckvn = _mla_prep(cq, ckv, kr, g_qa, g_kva, w_q_p, w_kv_p, cosx, sinx, n_batch=n_batch)
    o_attn, lse, *gathered = _attn_fwd(q, k, v, late_shards, n_batch=n_batch)
    w_out, w_gate, w_up, w_down = (_join_shards(n, a) for n, a in zip(_LATE, gathered))
    ycat, x2, h2 = _out_fwd(o_f, o_r, hg, o_attn, x, g_hn, g_mla, w_out, g2)
    act, dgate, dup, dx3_16, dx2, dx2_16, loss, dg_f, dg2 = _ffn_fwd_bwd(h2, x2, target, w_gate, w_up, w_down, g_f, g2)
    full = dict(w_out=_wgrad(ycat, dx2_16, "wgrad_out"), w_gate=_wgrad(dgate, h2, "wgrad_gate"),
                w_up=_wgrad(dup, h2, "wgrad_up"), w_down=_wgrad(act, dx3_16, "wgrad_down"))
    do_h, dhg, do_attn16, delta, dg_hn, dg_mla = _out_bwd(dx2_16, o_f, o_r, hg, o_attn, g_hn, g_mla, w_out)
    dqt, dk, dv, *recv_late = _attn_bwd(q, k, v, do_attn16, lse, delta,
                                        [_segments(n, full[n][1]) for n in _LATE], n_batch=n_batch)
    dcq, dckv, dkr, dqp16, dkvp16, dg_qa, dg_kva = _mla_prep_bwd(dqt, dk, dv, cq, ckv, g_qa, g_kva, w_q_p, w_kv_p,
                                                                  cosx, sinx, n_batch=n_batch)
    dq_f, dv_f, dz_f, dl_f = _hgrn_bwd(hq, hi, hff, do_h, st_f, lbl, n_batch=n_batch, direction=0)
    dq_r, dv_r, dz_r, dl_r = _hgrn_bwd(hq, hi, hfb, do_h, st_r, lbl, n_batch=n_batch, direction=1)
    dx, dproj16, dg1 = _in_bwd(dq_f, dq_r, dv_f, dv_r, dz_f, dz_r, dhg, hq, dcq, dckv, dkr, dx2, x, g1, w_in_p)

    full.update(w_in=_wgrad(dproj16, h1, "wgrad_in"), w_q_b=_wgrad(dqp16, cqn, "wgrad_q_b"))
    g_kv = _unpad_kv(_wgrad(ckvn, dkvp16, "wgrad_kv_b")[0])
    early16 = dict(w_in=full["w_in"][1][:D_IN].reshape(N_CHIPS, D_IN // N_CHIPS, D_MODEL),
                   w_q_b=full["w_q_b"][1].reshape(B_HEADS, QK_PAD, Q_LORA)[:, :B_NOPE + B_ROPE],
                   w_kv_b=_segments("w_kv_b", g_kv).astype(BF16))
    small = dict(norm1_g=dg1, lb_logits=jnp.stack([dl_f, dl_r]), hgrn_norm_g=dg_hn, q_a_norm_g=dg_qa, kv_a_norm_g=dg_kva,
                 mla_norm_g=dg_mla, norm2_g=dg2, final_norm_g=dg_f, loss=loss[0:1, 0])
    return dx.reshape(n_batch, seq, D_MODEL), small, {**{n: f for n, (f, _) in full.items()}, "w_kv_b": g_kv}, early16, \
        dict(zip(_LATE, recv_late))


_HBM = pl.BlockSpec(memory_space=pltpu.HBM)
_MESH = pl.DeviceIdType.MESH


def _place():
    x, y, c = lax.axis_index("x"), lax.axis_index("y"), lax.axis_index("c")
    other_chips = [(1 - x, y), (x, 1 - y), (1 - x, 1 - y)]
    return x, y, c, other_chips


def _gather_copies(w_ref, wall_ref, send_sems, recv_sems, local_sem, base=0):
    x, y, c, chips = _place()
    mine = 2 * x + y

    def mk(j, chip_index, to):
        return pltpu.make_async_remote_copy(src_ref=w_ref, dst_ref=wall_ref.at[chip_index], send_sem=send_sems.at[base + j],
                                            recv_sem=recv_sems.at[base + j], device_id=to, device_id_type=_MESH)

    local = pltpu.make_async_copy(w_ref, wall_ref.at[mine], local_sem)
    sends = [mk(j, mine, (*chip, c)) for j, chip in enumerate(chips)]
    recvs = [mk(j, 2 * px + py, (x, y, c)) for j, (px, py) in enumerate(chips)]
    return local, sends, recvs


def _gather_start(*refs):
    local, sends, _ = _gather_copies(*refs)
    local.start()
    for cp in sends:
        cp.start()


def _gather_wait(*refs):
    local, sends, recvs = _gather_copies(*refs)
    for cp in recvs:
        cp.wait_recv()
    for cp in sends:
        cp.wait_send()
    local.wait()


def _scatter_copies(g_ref, recv_ref, send_sems, recv_sems, base=0):
    x, y, c, chips = _place()

    def mk(j, src_index, to):
        return pltpu.make_async_remote_copy(src_ref=g_ref.at[src_index], dst_ref=recv_ref.at[j],
                                            send_sem=send_sems.at[base + j], recv_sem=recv_sems.at[base + j],
                                            device_id=to, device_id_type=_MESH)

    sends = [mk(j, 2 * px + py, (px, py, c)) for j, (px, py) in enumerate(chips)]
    recvs = [mk(j, 0, (x, y, c)) for j in range(3)]
    return sends, recvs


def _scatter_start(*refs):
    for cp in _scatter_copies(*refs)[0]:
        cp.start()


def _scatter_wait(*refs):
    sends, recvs = _scatter_copies(*refs)
    for cp in recvs:
        cp.wait_recv()
    for cp in sends:
        cp.wait_send()


def _half(ref, which, axis):
    n = ref.shape[axis] // 2
    idx = [slice(None)] * len(ref.shape)
    idx[axis] = pl.ds(which * n, n)
    return ref.at[tuple(idx)]


def _split_axis(a):
    return 1 if a.shape[1] % 256 == 0 else 0


def _gather_halves(src_ref, dst_ref, send_sems, recv_sems, local_sem, base, axis):
    x, y, c, chips = _place()
    mine = 2 * x + y

    def remote(src, dst, n, to):
        return pltpu.make_async_remote_copy(src_ref=src, dst_ref=dst, send_sem=send_sems.at[base + n],
                                            recv_sem=recv_sems.at[base + n], device_id=to, device_id_type=_MESH)

    def slot(chip_index, which):
        return _half(dst_ref.at[chip_index], which, axis)

    local = pltpu.make_async_copy(src_ref, dst_ref.at[mine], local_sem)
    first = [remote(_half(src_ref, c, axis), slot(mine, c), j, (*chip, c)) for j, chip in enumerate(chips)]
    landed = [remote(_half(src_ref, c, axis), slot(2 * px + py, c), j, (x, y, c)) for j, (px, py) in enumerate(chips)]
    passed = [remote(slot(2 * px + py, c), slot(2 * px + py, c), 3 + j, (x, y, 1 - c)) for j, (px, py) in enumerate(chips)]
    handed = [remote(_half(src_ref, c, axis), slot(2 * px + py, 1 - c), 3 + j, (x, y, c)) for j, (px, py) in enumerate(chips)]

    def start():
        local.start()
        for cp in first:
            cp.start()

    def pass_on():
        for arrived, onward in zip(landed, passed):
            arrived.wait_recv()
            onward.start()

    def finish():
        for cp in handed:
            cp.wait_recv()
        for cp in first + passed:
            cp.wait_send()
        local.wait()

    return start, pass_on, finish


def _gather_shards(srcs, small):
    n = len(srcs)

    def body(*refs):
        src_refs, s_ref, dst_refs, sall_ref = refs[:n], refs[n], refs[n + 1:2 * n + 1], refs[2 * n + 1]
        send_sems, recv_sems, local_sems = refs[2 * n + 2:]
        _gather_start(s_ref, sall_ref, send_sems, recv_sems, local_sems.at[n], 6 * n)
        steps = [_gather_halves(src_refs[i], dst_refs[i], send_sems, recv_sems, local_sems.at[i], 6 * i, _split_axis(srcs[i]))
                 for i in range(n)]
        for phase in range(3):
            for step in steps:
                step[phase]()
        _gather_wait(s_ref, sall_ref, send_sems, recv_sems, local_sems.at[n], 6 * n)

    return pl.pallas_call(
        body, name="gather_shards", in_specs=[_HBM] * (n + 1), out_specs=[_HBM] * (n + 1),
        out_shape=[jax.ShapeDtypeStruct((N_CHIPS,) + a.shape, a.dtype) for a in list(srcs) + [small]],
        scratch_shapes=[pltpu.SemaphoreType.DMA((6 * n + 3,)), pltpu.SemaphoreType.DMA((6 * n + 3,)),
                        pltpu.SemaphoreType.DMA((n + 1,))],
    )(*srcs, small)


_SEM = pl.BlockSpec(memory_space=pltpu.SEMAPHORE)
_SIDE_EFFECT = pltpu.SideEffectType.DATAFLOW_SIDE_EFFECTING


def _scatter_begin(gsegs):
    n = len(gsegs)
    def hbm(a):
        return pltpu.with_memory_space_constraint(a, pltpu.HBM)

    lands = [hbm(lax.empty((3,) + g.shape[1:], g.dtype)) for g in gsegs]

    def body(*refs):
        g_refs, land_refs, send_sems, recv_sems, token = refs[:n], refs[n:2 * n], refs[2 * n], refs[2 * n + 1], refs[-1]
        for i in range(n):
            _scatter_start(g_refs[i], land_refs[i], send_sems, recv_sems, 3 * i)
        token[...] = jnp.zeros_like(token)

    out = pl.pallas_call(
        body, name="scatter_begin", in_specs=[_HBM] * (2 * n),
        out_specs=(_SEM, _SEM) + (_HBM,) * (2 * n) + (pl.BlockSpec(memory_space=pltpu.VMEM),),
        out_shape=(pltpu.SemaphoreType.DMA((3 * n,)), pltpu.SemaphoreType.DMA((3 * n,)))
        + tuple(pltpu.HBM(a.shape, a.dtype) for a in list(gsegs) + lands) + (jax.ShapeDtypeStruct((8, 128), F32),),
        input_output_aliases={i: 2 + i for i in range(2 * n)},
        compiler_params=pltpu.CompilerParams(has_side_effects=_SIDE_EFFECT),
    )(*[hbm(g) for g in gsegs], *lands)
    return out[0], out[1], out[2:2 + n], out[2 + n:2 + 2 * n], out[-1]


def _scatter_end(send_sems, recv_sems, gsegs, lands, after):
    n = len(gsegs)

    def body(*refs):
        g_refs, land_refs, ssems, rsems = refs[:n], refs[n:2 * n], refs[2 * n], refs[2 * n + 1]
        for i in range(n):
            _scatter_wait(g_refs[i], land_refs[i], ssems, rsems, 3 * i)

    out = pl.pallas_call(
        body, name="scatter_end", in_specs=[_HBM] * (2 * n) + [_SEM, _SEM] + [pl.BlockSpec(memory_space=pl.ANY)] * len(after),
        out_specs=(_HBM,) * (2 * n), out_shape=tuple(pltpu.HBM(a.shape, a.dtype) for a in list(gsegs) + list(lands)),
        input_output_aliases={i: i for i in range(2 * n)},
        compiler_params=pltpu.CompilerParams(has_side_effects=_SIDE_EFFECT),
    )(*gsegs, *lands, send_sems, recv_sems, *after)
    return out[n:]


def _swap_sibling(parts, small=None):
    n = len(parts)
    extra = small is not None

    def body(*refs):
        ins, outs = refs[:n + extra], refs[n + extra:2 * (n + extra)]
        send_sems, recv_sems, local_sem = refs[2 * (n + extra):]
        x, y, c, _ = _place()
        cps = [pltpu.make_async_remote_copy(src_ref=ins[i], dst_ref=outs[i], send_sem=send_sems.at[i],
                                            recv_sem=recv_sems.at[i], device_id=(x, y, 1 - c), device_id_type=_MESH)
               for i in range(n)]
        for cp in cps:
            cp.start()
        if extra:
            s_ref, sall_ref = ins[n], outs[n]
            me = 4 * x + 2 * y + c
            flips = [(fx, fy, fc) for fx in (0, 1) for fy in (0, 1) for fc in (0, 1)][1:]

            def peer(f):
                return tuple((1 - a) if b else a for a, b in zip((x, y, c), f))

            def sm(r, index, to):
                return pltpu.make_async_remote_copy(src_ref=s_ref, dst_ref=sall_ref.at[index], send_sem=send_sems.at[n + r],
                                                    recv_sem=recv_sems.at[n + r], device_id=to, device_id_type=_MESH)

            local = pltpu.make_async_copy(s_ref, sall_ref.at[me], local_sem)
            local.start()
            sends = [sm(r, me, peer(f)) for r, f in enumerate(flips)]
            for cp in sends:
                cp.start()
            for r, f in enumerate(flips):
                px, py, pc = peer(f)
                sm(r, 4 * px + 2 * py + pc, (x, y, c)).wait_recv()
            for cp in sends:
                cp.wait_send()
            local.wait()
        for cp in cps:
            cp.wait()

    return pl.pallas_call(
        body, name="swap_small" if extra else "swap_sibling", in_specs=[_HBM] * (n + extra), out_specs=[_HBM] * (n + extra),
        out_shape=[jax.ShapeDtypeStruct(p.shape, p.dtype) for p in parts]
        + ([jax.ShapeDtypeStruct((8,) + small.shape, F32)] if extra else []),
        scratch_shapes=[pltpu.SemaphoreType.DMA((n + 7,)), pltpu.SemaphoreType.DMA((n + 7,)), pltpu.SemaphoreType.DMA(())],
    )(*parts, *([small] if extra else []))


def _sum_segments(chip, full, recv, tm, pitch, name, after=()):
    _, rows, cols = recv.shape

    def body(chip_ref, o_ref, r_ref, *rest):
        acc = o_ref[...]
        for j in range(3):
            acc = acc + r_ref[j].astype(F32)
        rest[-1][...] = acc

    return pl.pallas_call(
        body, name=name,
        grid_spec=pltpu.PrefetchScalarGridSpec(
            num_scalar_prefetch=1, grid=(rows // tm,),
            in_specs=[pl.BlockSpec((tm, cols), lambda i, c: (c[0] * pitch + i, 0)),
                      pl.BlockSpec((3, tm, cols), lambda i, c: (0, i, 0))]
            + [pl.BlockSpec(a.shape, lambda i, c, nd=a.ndim: (0,) * nd) for a in after],
            out_specs=pl.BlockSpec((tm, cols), lambda i, c: (i, 0))),
        out_shape=jax.ShapeDtypeStruct((rows, cols), F32), compiler_params=_params(1),
    )(chip, full, recv, *after)


def _sum_devices(sall):
    def body(s_ref, o_ref):
        acc = s_ref[0]
        for d in range(1, 8):
            acc = acc + s_ref[d]
        o_ref[...] = acc

    return pl.pallas_call(body, name="sum_devices", out_shape=jax.ShapeDtypeStruct(sall.shape[1:], F32))(sall)


def _adamw(w, m, v, ga, gb, name):
    rows, cols = w.shape
    tm = _row_tile(rows)
    two = gb is not None

    def body(*refs):
        w_ref, m_ref, v_ref, ga_ref = refs[:4]
        g_ref, d_ref, m2_ref, v2_ref = refs[-4:]
        g = ga_ref[...] + refs[4][...] if two else ga_ref[...]
        m2 = ADAM_B1 * m_ref[...] + (1.0 - ADAM_B1) * g
        v2 = ADAM_B2 * v_ref[...] + (1.0 - ADAM_B2) * (g * g)
        m_hat = m2 / (1.0 - ADAM_B1 ** ADAM_STEP)
        v_hat = v2 / (1.0 - ADAM_B2 ** ADAM_STEP)
        g_ref[...] = g
        d_ref[...] = -ADAM_LR * (m_hat / (jnp.sqrt(v_hat) + ADAM_EPS) + ADAM_WD * w_ref[...])
        m2_ref[...] = m2
        v2_ref[...] = v2

    blk = _rows(tm, cols)
    args = (w, m, v, ga) + ((gb,) if two else ())
    return pl.pallas_call(
        body, name=name, grid=(rows // tm,), in_specs=[blk] * len(args), out_specs=[blk] * 4,
        out_shape=[jax.ShapeDtypeStruct(w.shape, F32)] * 4, compiler_params=_params(1),
    )(*args)


_TRANSPOSED = ("w_in", "w_q_b", "w_gate", "w_up")
_COLUMN_SHARDED = ("w_kv_b",)
_FULL_SHAPES = dict(w_in=(D_IN, D_MODEL), w_q_b=(768, Q_LORA), w_kv_b=(KV_LORA, 1024), w_out=(D_MODEL, D_MODEL),
                    w_gate=(D_FF, D_MODEL), w_up=(D_FF, D_MODEL), w_down=(D_FF, D_MODEL))
_SMALL = (("norm1_g", 1024), ("lb_logits", 2048), ("hgrn_norm_g", 512), ("q_a_norm_g", 384), ("kv_a_norm_g", 256),
          ("mla_norm_g", 512), ("norm2_g", 1024), ("final_norm_g", 1024))
_UPDATE_ROWS = 48


def _pad_rows(a, rows):
    return jnp.pad(a, ((0, rows - a.shape[0]), (0, 0)))


_EARLY = ("w_in", "w_q_b", "w_kv_b")
_LATE = ("w_out", "w_gate", "w_up", "w_down")


def _segments(name, g):
    r, c = g.shape
    if name in _COLUMN_SHARDED:
        return g.reshape(r, N_CHIPS, c // N_CHIPS).transpose(1, 0, 2)
    return g.reshape(N_CHIPS, r // N_CHIPS, c)


def _own_segment(name, g, chip):
    r, c = g.shape
    if name in _COLUMN_SHARDED:
        return lax.dynamic_slice_in_dim(g, chip * (c // N_CHIPS), c // N_CHIPS, axis=1)
    return lax.dynamic_slice_in_dim(g, chip * (r // N_CHIPS), r // N_CHIPS, axis=0)


def _join_shards(name, seg):
    r, c = _FULL_SHAPES[name]
    if name in _COLUMN_SHARDED:
        return seg.transpose(1, 0, 2).reshape(r, c)
    return seg.reshape(r, c)


def kernel(x, norm1_g, w_in, lb_logits, hgrn_norm_g, q_a_norm_g, w_q_b, kv_a_norm_g, w_kv_b, mla_norm_g, w_out, norm2_g, w_gate, w_up, w_down, final_norm_g, loss_target, m_norm1_g, m_w_in, m_lb_logits, m_hgrn_norm_g, m_q_a_norm_g, m_w_q_b, m_kv_a_norm_g, m_w_kv_b, m_mla_norm_g, m_w_out, m_norm2_g, m_w_gate, m_w_up, m_w_down, m_final_norm_g, v_norm1_g, v_w_in, v_lb_logits, v_hgrn_norm_g, v_q_a_norm_g, v_w_q_b, v_kv_a_norm_g, v_w_kv_b, v_mla_norm_g, v_w_out, v_norm2_g, v_w_gate, v_w_up, v_w_down, v_final_norm_g):
    names = ("norm1_g", "w_in", "lb_logits", "hgrn_norm_g", "q_a_norm_g", "w_q_b", "kv_a_norm_g", "w_kv_b", "mla_norm_g",
             "w_out", "norm2_g", "w_gate", "w_up", "w_down", "final_norm_g")
    w = dict(zip(names, (norm1_g, w_in, lb_logits, hgrn_norm_g, q_a_norm_g, w_q_b, kv_a_norm_g, w_kv_b, mla_norm_g,
                         w_out, norm2_g, w_gate, w_up, w_down, final_norm_g)))
    m = dict(zip(names, (m_norm1_g, m_w_in, m_lb_logits, m_hgrn_norm_g, m_q_a_norm_g, m_w_q_b, m_kv_a_norm_g, m_w_kv_b,
                         m_mla_norm_g, m_w_out, m_norm2_g, m_w_gate, m_w_up, m_w_down, m_final_norm_g)))
    v = dict(zip(names, (v_norm1_g, v_w_in, v_lb_logits, v_hgrn_norm_g, v_q_a_norm_g, v_w_q_b, v_kv_a_norm_g, v_w_kv_b,
                         v_mla_norm_g, v_w_out, v_norm2_g, v_w_gate, v_w_up, v_w_down, v_final_norm_g)))
    matrices = _EARLY + _LATE
    chip = 2 * lax.axis_index("x") + lax.axis_index("y")

    def shard2d(a, n):
        return jnp.swapaxes(a[0], 0, 1) if n in _TRANSPOSED else a[0]

    def unshard2d(a, n):
        return (jnp.swapaxes(a, 0, 1) if n in _TRANSPOSED else a)[None]

    w16 = {n: shard2d(w[n], n).astype(BF16) for n in matrices}
    lb8 = _pad_rows(lb_logits.reshape(4, 128), 8)
    *early, lball = _gather_shards([w16[n] for n in _EARLY], lb8)
    lbl = lball[:, :4].reshape(N_CHIPS, 2, 2, 128).transpose(1, 2, 0, 3).reshape(2, 2, A_WIDTH)

    grad_x, g, full, early16, recv = _local_step(
        x, loss_target, lbl, norm1_g, hgrn_norm_g, q_a_norm_g, kv_a_norm_g, mla_norm_g, norm2_g, final_norm_g[None, :],
        *(_join_shards(n, a) for n, a in zip(_EARLY, early)), [w16[n] for n in _LATE])

    n_small = sum(size for _, size in _SMALL)
    small = jnp.concatenate([g[n].reshape(-1) for n, _ in _SMALL] + [g["loss"], jnp.zeros((SMALL_ROWS * 128 - n_small - 1,), F32)])
    chip1 = chip.reshape(1).astype(jnp.int32)
    out, raw = {}, {}

    def update(n, part, sib):
        raw[n] = _adamw(shard2d(w[n], n), shard2d(m[n], n), shard2d(v[n], n), part, sib, f"adamw_{n}")
        out[n] = tuple(unshard2d(r, n) for r in raw[n])

    send_sems, recv_sems, segs, lands, token = _scatter_begin([early16[n] for n in _EARLY])
    parts = [_sum_segments(chip1, full[n], recv[n], *_OWN_ROWS[n], f"sum_{n}", after=(token,)) for n in _LATE]
    for n, part, sib in zip(_LATE, parts, _swap_sibling(parts)):
        update(n, part, sib)
    recv_early = _scatter_end(send_sems, recv_sems, segs, lands, after=[raw[n][1] for n in _LATE])
    full["w_kv_b"] = _own_segment("w_kv_b", full["w_kv_b"], chip)
    parts = [_sum_segments(chip1, full[n], r, *_OWN_ROWS[n], f"sum_{n}") for n, r in zip(_EARLY, recv_early)]
    *sibs, small_all = _swap_sibling(parts, small.reshape(SMALL_ROWS, 128))
    for n, part, sib in zip(_EARLY, parts, sibs):
        update(n, part, sib)
    small_sum = _sum_devices(small_all).reshape(-1)
    loss = small_sum[n_small]

    small_g, off = {}, 0
    for n, size in _SMALL:
        small_g[n] = small_sum[off:off + size]
        off += size
    small_g["lb_logits"] = lax.dynamic_slice_in_dim(small_g["lb_logits"].reshape(2, 2, A_WIDTH), chip * 128, 128, axis=2)
    small_names = tuple(n for n, _ in _SMALL)

    def pack(d):
        return _pad_rows(jnp.concatenate([d[n].reshape(-1) for n in small_names]).reshape(-1, 128), _UPDATE_ROWS)

    res = _adamw(pack(w), pack(m), pack(v), pack(small_g), None, "adamw_small")
    off = 0
    flat = [r.reshape(-1) for r in res]
    for n in small_names:
        size = w[n].size
        out[n] = tuple(f[off:off + size].reshape(w[n].shape) for f in flat)
        off += size

    return (loss, grad_x) + tuple(out[n][i] for i in range(4) for n in names)
```

```python
import functools

import jax
import jax.numpy as jnp
from jax import lax
from jax.experimental import pallas as pl
from jax.experimental.pallas import tpu as pltpu

F32 = jnp.float32
BF16 = jnp.bfloat16

D_MODEL = 1024
A_WIDTH = 512
HEAD_PAIRS = 4
PAIRS_AT_ONCE = 2
CHUNK = 64
B_HEADS = 4
B_NOPE = 128
B_ROPE = 64
B_V = 128
QK_PAD = 256
Q_LORA = 384
KV_LORA = 256
D_FF = 2816
D_IN = 3264
D_IN_PAD = 3328
IN_WIDTHS = (512, 512, 512, 512, 512, Q_LORA, KV_LORA, 128)
ROPE_THETA = 10000.0
EPS = 1e-6
ATTN_SCALE = (B_NOPE + B_ROPE) ** -0.5
LOG2E = 1.4426950408889634
SCALE_LOG2E = ATTN_SCALE * LOG2E

ADAM_LR = 0.001
ADAM_B1 = 0.9
ADAM_B2 = 0.999
ADAM_EPS = 1e-08
ADAM_WD = 0.01
ADAM_STEP = 10

VMEM_LIMIT_BYTES = 60 * 1024 * 1024
N_CHIPS = 4
SMALL_ROWS = 56


def _params(n_axes):
    return pltpu.CompilerParams(dimension_semantics=("arbitrary",) * n_axes,
                                vmem_limit_bytes=VMEM_LIMIT_BYTES)


def _dot(a, b):
    return jnp.dot(a, b, preferred_element_type=F32)


def _dot_nt(a, b):
    return lax.dot_general(a, b, (((1,), (1,)), ((), ())), preferred_element_type=F32)


def _dot_tn(a, b):
    return lax.dot_general(a, b, (((0,), (0,)), ((), ())), preferred_element_type=F32)


def _split3(x):
    x1 = x.astype(BF16)
    r = x - x1.astype(F32)
    x2 = r.astype(BF16)
    x3 = (r - x2.astype(F32)).astype(BF16)
    return x1, x2, x3


def _exact_right(x, m16):
    x1, x2, x3 = _split3(x)
    return _dot(x1, m16) + _dot(x2, m16) + _dot(x3, m16)


def _iota2(shape, dim):
    return lax.broadcasted_iota(jnp.int32, shape, dim)


def _sigmoid(x):
    return jax.nn.sigmoid(x)


def _pick(dim, cap, mult=128):
    if dim <= cap:
        return dim
    best = None
    for d in range(mult, cap + 1, mult):
        if dim % d == 0:
            best = d
    assert best is not None, (dim, cap, mult)
    return best


def _row_tile(t, cap=256):
    return _pick(t, cap, 8)


ROW_GROUP = 256


def _row_groups(tm):
    size = min(tm, ROW_GROUP)
    return [slice(r, r + size) for r in range(0, tm, size)]


def _full(shape, single=False):
    if single:
        return pl.BlockSpec(shape, lambda *_: (0,) * len(shape), pipeline_mode=pl.Buffered(1))
    return pl.BlockSpec(shape, lambda *_: (0,) * len(shape))


def _rows(tm, width):
    return pl.BlockSpec((tm, width), lambda i: (i, 0))


def _acc_rows(ref, val, first):
    s = jnp.sum(val, axis=0, keepdims=True)

    @pl.when(first)
    def _():
        ref[...] = s

    @pl.when(jnp.logical_not(first))
    def _():
        ref[...] += s


def _in_fwd(x, g1, w_in_p):
    t = x.shape[0]
    tm = _row_tile(t, 2 * ROW_GROUP)

    def body(x_ref, g_ref, w_ref, h_ref, *outs):
        gain = g_ref[...]

        def group(xv):
            r = lax.rsqrt(jnp.mean(xv * xv, axis=-1, keepdims=True) + EPS)
            h = ((xv * r) * gain).astype(BF16)
            proj, off = [], 0
            for w in IN_WIDTHS:
                proj.append(_dot_nt(h, w_ref[off:off + w, :]))
                off += w
            return h, proj

        groups = _row_groups(tm)
        done = [group(xv) for xv in [x_ref[g, :] for g in groups]]
        for g, (h, proj) in zip(groups, done):
            h_ref[g, :] = h
            for o_ref, p in zip(outs, proj):
                o_ref[g, :] = p

    return pl.pallas_call(
        body, name="in_fwd", grid=(t // tm,),
        in_specs=[_rows(tm, D_MODEL), _full((1, D_MODEL)), _full((D_IN_PAD, D_MODEL))],
        out_specs=[_rows(tm, D_MODEL)] + [_rows(tm, w) for w in IN_WIDTHS],
        out_shape=[jax.ShapeDtypeStruct((t, D_MODEL), BF16)]
        + [jax.ShapeDtypeStruct((t, w), F32) for w in IN_WIDTHS],
        compiler_params=_params(1),
    )(x, g1, w_in_p)


def _in_bwd(dq_f, dq_r, dv_f, dv_r, dz_f, dz_r, dhg, hq, dcq, dckv, dkr, dx2, x, g1, w_in_p):
    t = x.shape[0]
    tm = _row_tile(t, 2 * ROW_GROUP)

    def body(dqf_ref, dqr_ref, dvf_ref, dvr_ref, dzf_ref, dzr_ref, dhg_ref, hq_ref, dcq_ref, dckv_ref,
             dkr_ref, dx2_ref, x_ref, g_ref, w_ref, dx_ref, dp_ref, dg_ref):
        gain = g_ref[...]

        def group(dqf, dqr, dvf, dvr, dzf, dzr, dhg, hqv, dcq, dckv, dkr, dx2, xv):
            sg = _sigmoid(hqv)
            dhq = (dqf + dqr) * (sg * (1.0 + hqv * (1.0 - sg)))
            pieces = [p.astype(BF16) for p in (dhq, dvf + dvr, dzf, dzr, dhg, dcq, dckv, dkr)]
            dh = _dot(jnp.concatenate(pieces, axis=1), w_ref[...])
            r = lax.rsqrt(jnp.mean(xv * xv, axis=-1, keepdims=True) + EPS)
            xh = xv * r
            dxh = dh * gain
            return dx2 + r * (dxh - xh * jnp.mean(dxh * xh, axis=-1, keepdims=True)), pieces, dh * xh

        ins = (dqf_ref, dqr_ref, dvf_ref, dvr_ref, dzf_ref, dzr_ref, dhg_ref, hq_ref, dcq_ref, dckv_ref, dkr_ref,
               dx2_ref, x_ref)
        groups = _row_groups(tm)
        done = [group(*vals) for vals in [[ref[g, :] for ref in ins] for g in groups]]
        for g, (dx, pieces, _) in zip(groups, done):
            dx_ref[g, :] = dx
            off = 0
            for p16, w in zip(pieces, IN_WIDTHS):
                dp_ref[g, off:off + w] = p16
                off += w
        _acc_rows(dg_ref, jnp.concatenate([d[2] for d in done], axis=0), pl.program_id(0) == 0)

    a512 = _rows(tm, A_WIDTH)
    return pl.pallas_call(
        body, name="in_bwd", grid=(t // tm,),
        in_specs=[a512] * 8 + [_rows(tm, Q_LORA), _rows(tm, KV_LORA), _rows(tm, 128), _rows(tm, D_MODEL),
                               _rows(tm, D_MODEL), _full((1, D_MODEL)), _full((D_IN_PAD, D_MODEL))],
        out_specs=[_rows(tm, D_MODEL), _rows(tm, D_IN_PAD), _full((1, D_MODEL))],
        out_shape=[jax.ShapeDtypeStruct((t, D_MODEL), F32), jax.ShapeDtypeStruct((t, D_IN_PAD), BF16),
                   jax.ShapeDtypeStruct((1, D_MODEL), F32)],
        compiler_params=_params(1),
    )(dq_f, dq_r, dv_f, dv_r, dz_f, dz_r, dhg, hq, dcq, dckv, dkr, dx2, x, g1, w_in_p)


def _lower_bound(lbl_ref, direction):
    l0 = lbl_ref[direction, 0:1, :]
    l1 = lbl_ref[direction, 1:2, :]
    m = jnp.maximum(l0, l1)
    e0 = jnp.exp(l0 - m)
    e1 = jnp.exp(l1 - m)
    return e0 / (e0 + e1)


def _hgrn_consts(rb, reverse):
    row = _iota2((rb, rb), 0)
    col = _iota2((rb, rb), 1)
    same = (row // CHUNK) == (col // CHUNK)
    tri = jnp.logical_and(same, (col >= row) if reverse else (col <= row))
    tri_t = jnp.logical_and(same, (col <= row) if reverse else (col >= row))
    r128 = _iota2((128, 128), 0)
    c128 = _iota2((128, 128), 1)
    bd = (r128 < 64) == (c128 < 64)
    lane = _iota2((1, 128), 1)
    m0 = (lane < 64).astype(F32)
    return tri, tri_t, bd, (m0, 1.0 - m0)


def _per_chunk(x, fn):
    n = x.shape[0] // CHUNK
    return jnp.concatenate([jnp.broadcast_to(fn(x[c * CHUNK:(c + 1) * CHUNK]), (CHUNK, x.shape[1])) for c in range(n)],
                           axis=0)


def _chunk_cumsum(x, reverse):
    rb = x.shape[0]
    pos = _iota2(x.shape, 0) % CHUNK
    step = 1
    while step < CHUNK:
        if reverse:
            x = x + jnp.where(pos < CHUNK - step, pltpu.roll(x, rb - step, 0), 0.0)
        else:
            x = x + jnp.where(pos >= step, pltpu.roll(x, step, 0), 0.0)
        step *= 2
    return x


def _hgrn_block(z, hqv, lb, reverse):
    sig = _sigmoid(z)
    sn = _sigmoid(-z)
    q = hqv * _sigmoid(hqv)
    f = lb + (1.0 - lb) * sig
    k = (1.0 - lb) * sn
    lf = jnp.log(f)
    cum = _chunk_cumsum(lf, reverse)
    last = _per_chunk(cum, (lambda a: a[0:1]) if reverse else (lambda a: a[CHUNK - 1:CHUNK]))
    e_neg = jnp.exp(-cum)
    e_end = jnp.exp(last - cum)
    a = jnp.exp(cum)
    return dict(sig=sig, sn=sn, q=q, f=f, k=k, a=a, e_neg=e_neg, e_end=e_end,
                q_dec=q * a, k_inv=k * e_neg, k_end=k * e_end, d=jnp.exp(last))


def _hgrn_dims(t, n_batch):
    s = t // n_batch
    rb = _pick(s, 256, CHUNK)
    return s, rb, s // rb, rb // CHUNK


def _hgrn_fwd(hq, hi, hf, lbl, *, n_batch, direction):
    t = hq.shape[0]
    reverse = direction == 1
    s, rb, nb, nc = _hgrn_dims(t, n_batch)

    def tmap(b, j):
        return (b * nb + ((nb - 1 - j) if reverse else j), 0)

    def smap(b, j):
        return (b * nb + ((nb - 1 - j) if reverse else j), 0, 0, 0)

    def body(hq_ref, hi_ref, hf_ref, lbl_ref, o_ref, st_ref, st_scr):
        @pl.when(pl.program_id(1) == 0)
        def _():
            st_scr[...] = jnp.zeros_like(st_scr)

        lb_all = _lower_bound(lbl_ref, direction)
        tri, _, bd, masks = _hgrn_consts(rb, reverse)
        order = range(nc - 1, -1, -1) if reverse else range(nc)

        pairs = [slice(p * 128, (p + 1) * 128) for p in range(HEAD_PAIRS)]
        chunks = [slice(c * CHUNK, (c + 1) * CHUNK) for c in range(nc)]
        w = _hgrn_block(hf_ref[...], hq_ref[...], lb_all, reverse)
        v16 = hi_ref[...].astype(BF16)
        qd16 = w["q_dec"].astype(BF16)
        ki16 = w["k_inv"].astype(BF16)
        ke16 = w["k_end"].astype(BF16)
        sc = [[_dot_nt(jnp.where(mh > 0.0, qd16[:, ls], 0.0).astype(BF16), ki16[:, ls]) for mh in masks] for ls in pairs]
        pv = [[_dot(jnp.where(tri, s_e, 0.0).astype(BF16), v16[:, ls]) for s_e in sc_p] for sc_p, ls in zip(sc, pairs)]
        o_intra = [jnp.where(masks[0] > 0.0, pv_p[0], pv_p[1]) for pv_p in pv]
        ut = [[jnp.where(bd, _dot_tn(v16[rs, ls], ke16[rs, ls]), 0.0) for ls in pairs] for rs in chunks]
        st = [st_scr[p] for p in range(HEAD_PAIRS)]
        for c in order:
            rs = chunks[c]
            inter = [_dot_nt(qd16[rs, ls], st[p].astype(BF16)) for p, ls in enumerate(pairs)]
            for p, ls in enumerate(pairs):
                o_ref[rs, ls] = o_intra[p][rs] + inter[p]
                st_ref[c, p] = st[p]
                st[p] = st[p] * w["d"][c * CHUNK:c * CHUNK + 1, ls] + ut[c][p]
        for p in range(HEAD_PAIRS):
            st_scr[p] = st[p]

    blk = pl.BlockSpec((rb, A_WIDTH), tmap)
    return pl.pallas_call(
        body, name=f"hgrn_fwd_{direction}", grid=(n_batch, nb),
        in_specs=[blk, blk, blk, _full((2, 2, A_WIDTH))],
        out_specs=[blk, pl.BlockSpec((nc, HEAD_PAIRS, 128, 128), smap)],
        out_shape=[jax.ShapeDtypeStruct((t, A_WIDTH), F32),
                   jax.ShapeDtypeStruct((t // CHUNK, HEAD_PAIRS, 128, 128), F32)],
        scratch_shapes=[pltpu.VMEM((HEAD_PAIRS, 128, 128), F32)],
        compiler_params=_params(2),
    )(hq, hi, hf, lbl)


def _hgrn_bwd(hq, hi, hf, do, st, lbl, *, n_batch, direction):
    t = hq.shape[0]
    reverse = direction == 1
    s, rb, nb, nc = _hgrn_dims(t, n_batch)

    def tmap(b, j):
        return (b * nb + (j if reverse else (nb - 1 - j)), 0)

    def smap(b, j):
        return (b * nb + (j if reverse else (nb - 1 - j)), 0, 0, 0)

    def body(hq_ref, hi_ref, hf_ref, do_ref, st_ref, lbl_ref, dq_ref, dv_ref, dz_ref, dl_ref, g_scr, dlb_scr):
        b = pl.program_id(0)
        j = pl.program_id(1)

        @pl.when(jnp.logical_and(b == 0, j == 0))
        def _():
            dlb_scr[...] = jnp.zeros_like(dlb_scr)

        @pl.when(j == 0)
        def _():
            g_scr[...] = jnp.zeros_like(g_scr)

        lb_all = _lower_bound(lbl_ref, direction)
        tri, tri_t, bd, masks = _hgrn_consts(rb, reverse)
        order = range(nc) if reverse else range(nc - 1, -1, -1)

        chunks = [slice(c * CHUNK, (c + 1) * CHUNK) for c in range(nc)]

        def lanes(per_pair):
            return jnp.concatenate(per_pair, axis=1)

        for first_pair in range(0, HEAD_PAIRS, PAIRS_AT_ONCE):
            here = slice(first_pair * 128, (first_pair + PAIRS_AT_ONCE) * 128)
            pairs = [slice(p * 128, (p + 1) * 128) for p in range(PAIRS_AT_ONCE)]
            lb = lb_all[:, here]
            w = _hgrn_block(hf_ref[:, here], hq_ref[:, here], lb, reverse)
            dov = do_ref[:, here]
            v16 = hi_ref[:, here].astype(BF16)
            do16 = dov.astype(BF16)
            qd16 = w["q_dec"].astype(BF16)
            ki16 = w["k_inv"].astype(BF16)
            ke16 = w["k_end"].astype(BF16)
            qm16 = [[jnp.where(mh > 0.0, qd16[:, ls], 0.0).astype(BF16) for mh in masks] for ls in pairs]
            dom16 = [[jnp.where(mh > 0.0, do16[:, ls], 0.0).astype(BF16) for mh in masks] for ls in pairs]
            heads = [(p, e) for p in range(PAIRS_AT_ONCE) for e in range(2)]
            dp = {(p, e): _dot_nt(dom16[p][e], v16[:, pairs[p]]) for p, e in heads}
            pm_t = {(p, e): _dot_nt(ki16[:, pairs[p]], qm16[p][e]) for p, e in heads}
            dp_t = {(p, e): _dot_nt(v16[:, pairs[p]], dom16[p][e]) for p, e in heads}
            dp = {h: jnp.where(tri, a, 0.0).astype(BF16) for h, a in dp.items()}
            pm_t = {h: jnp.where(tri_t, a, 0.0).astype(BF16) for h, a in pm_t.items()}
            dp_t = {h: jnp.where(tri_t, a, 0.0).astype(BF16) for h, a in dp_t.items()}
            dv_e = {(p, e): _dot(pm_t[p, e], do16[:, pairs[p]]) for p, e in heads}
            dq_e = {(p, e): _dot(dp[p, e], ki16[:, pairs[p]]) for p, e in heads}
            dk_e = {(p, e): _dot(dp_t[p, e], qd16[:, pairs[p]]) for p, e in heads}
            st = [[st_ref[c, first_pair + p] for p in range(PAIRS_AT_ONCE)] for c in range(nc)]
            dq_x = [[_dot(do16[rs, ls], st[c][p].astype(BF16)) for p, ls in enumerate(pairs)] for c, rs in enumerate(chunks)]
            gq = [[jnp.where(bd, _dot_tn(do16[rs, ls], qd16[rs, ls]), 0.0) for ls in pairs] for rs in chunks]
            g = [g_scr[first_pair + p] for p in range(PAIRS_AT_ONCE)]
            dk_end, dv_x, dd = [None] * nc, [None] * nc, [None] * nc
            for c in order:
                rs = chunks[c]
                g16 = [a.astype(BF16) for a in g]
                dk_end[c] = lanes([_dot(v16[rs, ls], g16[p]) for p, ls in enumerate(pairs)])
                dv_x[c] = lanes([_dot_nt(ke16[rs, ls], g16[p]) for p, ls in enumerate(pairs)])
                dd[c] = jnp.broadcast_to(lanes([jnp.sum(g[p] * st[c][p], axis=0, keepdims=True) for p in range(PAIRS_AT_ONCE)]),
                                         (CHUNK, PAIRS_AT_ONCE * 128))
                for p, ls in enumerate(pairs):
                    g[p] = g[p] * w["d"][c * CHUNK:c * CHUNK + 1, ls] + gq[c][p]
            for p in range(PAIRS_AT_ONCE):
                g_scr[first_pair + p] = g[p]

            def both_heads(d):
                return lanes([jnp.where(masks[0] > 0.0, d[p, 0], d[p, 1]) for p in range(PAIRS_AT_ONCE)])

            dq_dec = both_heads(dq_e) + jnp.concatenate([lanes(a) for a in dq_x], axis=0)
            dk_inv = both_heads(dk_e)
            dk_end = jnp.concatenate(dk_end, axis=0)
            dv = both_heads(dv_e) + jnp.concatenate(dv_x, axis=0)
            dd = jnp.concatenate(dd, axis=0)
            dke = dk_end * w["k_end"]
            dcum = dq_dec * w["q_dec"] - dk_inv * w["k_inv"] - dke
            dk = dk_inv * w["e_neg"] + dk_end * w["e_end"]
            dlast = _per_chunk(dke, lambda a: jnp.sum(a, axis=0, keepdims=True)) + dd * w["d"]
            dlf = _chunk_cumsum(dcum, not reverse) + dlast
            tt = dlf / w["f"] - dk
            dq_ref[:, here] = dq_dec * w["a"]
            dv_ref[:, here] = dv
            dz_ref[:, here] = ((1.0 - lb) * w["sig"] * w["sn"] * tt).astype(BF16)
            dlb_scr[:, here] += jnp.sum(w["sn"] * tt, axis=0, keepdims=True)

        @pl.when(jnp.logical_and(b == pl.num_programs(0) - 1, j == pl.num_programs(1) - 1))
        def _():
            d0 = dlb_scr[...] * lb_all * (1.0 - lb_all)
            dl_ref[0:1, :] = d0
            dl_ref[1:2, :] = -d0

    blk = pl.BlockSpec((rb, A_WIDTH), tmap)
    return pl.pallas_call(
        body, name=f"hgrn_bwd_{direction}", grid=(n_batch, nb),
        in_specs=[blk, blk, blk, blk, pl.BlockSpec((nc, HEAD_PAIRS, 128, 128), smap), _full((2, 2, A_WIDTH))],
        out_specs=[blk, blk, blk, _full((2, A_WIDTH))],
        out_shape=[jax.ShapeDtypeStruct((t, A_WIDTH), F32)] * 2
        + [jax.ShapeDtypeStruct((t, A_WIDTH), BF16), jax.ShapeDtypeStruct((2, A_WIDTH), F32)],
        scratch_shapes=[pltpu.VMEM((HEAD_PAIRS, 128, 128), F32), pltpu.VMEM((1, A_WIDTH), F32)],
        compiler_params=_params(2),
    )(hq, hi, hf, do, st, lbl)


def _swap_rope_halves(x):
    lane = _iota2(x.shape, 1)
    return jnp.where(lane < 32, pltpu.roll(x, 96, 1), pltpu.roll(x, 32, 1))


def _rms_fwd(xv, g):
    r = lax.rsqrt(jnp.mean(xv * xv, axis=-1, keepdims=True) + EPS)
    return (xv * r) * g


def _rms_bwd(dy, xv, g):
    r = lax.rsqrt(jnp.mean(xv * xv, axis=-1, keepdims=True) + EPS)
    xh = xv * r
    dxh = dy * g
    return r * (dxh - xh * jnp.mean(dxh * xh, axis=-1, keepdims=True)), dy * xh


def _mla_prep(cq, ckv, kr, g_qa, g_kva, w_q_p, w_kv_p, cosx, sinx, *, n_batch):
    t = cq.shape[0]
    s = t // n_batch
    tm = _row_tile(s, 1024)
    nt = s // tm

    def body(cq_ref, ckv_ref, kr_ref, gq_ref, gkv_ref, wq_ref, wkv_ref, cos_ref, sin_ref,
             q_ref, k_ref, v_ref, cqn_ref, ckvn_ref):
        cos, sin = cos_ref[...], sin_ref[...]
        cqn = _rms_fwd(cq_ref[...], gq_ref[...]).astype(BF16)
        ckvn = _rms_fwd(ckv_ref[...], gkv_ref[...]).astype(BF16)
        cqn_ref[...] = cqn
        ckvn_ref[...] = ckvn
        krv = kr_ref[...]
        kr_roped = (krv * cos + _swap_rope_halves(krv) * sin).astype(BF16)
        for h in range(B_HEADS):
            o = h * QK_PAD
            q_ref[:, o:o + 128] = _dot_nt(cqn, wq_ref[o:o + 128, :]).astype(BF16)
            qr = _dot_nt(cqn, wq_ref[o + 128:o + 256, :])
            q_ref[:, o + 128:o + 256] = (qr * cos + _swap_rope_halves(qr) * sin).astype(BF16)
            k_ref[:, o:o + 128] = _dot(ckvn, wkv_ref[:, h * 128:(h + 1) * 128]).astype(BF16)
            k_ref[:, o + 128:o + 256] = kr_roped
        v_ref[...] = _dot(ckvn, wkv_ref[:, 512:1024]).astype(BF16)

    tab = pl.BlockSpec((tm, 128), lambda i: (i % nt, 0))
    return pl.pallas_call(
        body, name="mla_prep", grid=(t // tm,),
        in_specs=[_rows(tm, Q_LORA), _rows(tm, KV_LORA), _rows(tm, 128), _full((1, Q_LORA)), _full((1, KV_LORA)),
                  _full((1024, Q_LORA)), _full((KV_LORA, 1024)), tab, tab],
        out_specs=[_rows(tm, 1024), _rows(tm, 1024), _rows(tm, 512), _rows(tm, Q_LORA), _rows(tm, KV_LORA)],
        out_shape=[jax.ShapeDtypeStruct((t, 1024), BF16), jax.ShapeDtypeStruct((t, 1024), BF16),
                   jax.ShapeDtypeStruct((t, 512), BF16), jax.ShapeDtypeStruct((t, Q_LORA), BF16),
                   jax.ShapeDtypeStruct((t, KV_LORA), BF16)],
        compiler_params=_params(1),
    )(cq, ckv, kr, g_qa, g_kva, w_q_p, w_kv_p, cosx, sinx)


def _mla_prep_bwd(dqt, dk, dv, cq, ckv, g_qa, g_kva, w_q_p, w_kv_p, cosx, sinx, *, n_batch):
    t = cq.shape[0]
    s = t // n_batch
    tm = _row_tile(s, 1024)
    nt = s // tm

    def body(dqt_ref, dk_ref, dv_ref, cq_ref, ckv_ref, gq_ref, gkv_ref, wq_ref, wkv_ref, cos_ref, sin_ref,
             dcq_ref, dckv_ref, dkr_ref, dqp_ref, dkvp_ref, dgq_ref, dgkv_ref):
        cos, sin = cos_ref[...], sin_ref[...]
        first = pl.program_id(0) == 0

        def unrope(d):
            return d * cos + _swap_rope_halves(d * sin)

        dcqn = None
        dkr = None
        dckvn = None
        for h in range(B_HEADS):
            o = h * QK_PAD
            dq_h = jnp.transpose(dqt_ref[o:o + QK_PAD, :])
            dqn16 = dq_h[:, 0:128].astype(BF16)
            dqr16 = unrope(dq_h[:, 128:256]).astype(BF16)
            dqp_ref[:, o:o + 128] = dqn16
            dqp_ref[:, o + 128:o + 256] = dqr16
            part = _dot(dqn16, wq_ref[o:o + 128, :]) + _dot(dqr16, wq_ref[o + 128:o + 256, :])
            dcqn = part if dcqn is None else dcqn + part
            dkn16 = dk_ref[:, o:o + 128].astype(BF16)
            dkvp_ref[:, h * 128:(h + 1) * 128] = dkn16
            part = _dot_nt(dkn16, wkv_ref[:, h * 128:(h + 1) * 128])
            dckvn = part if dckvn is None else dckvn + part
            kr_part = dk_ref[:, o + 128:o + 256]
            dkr = kr_part if dkr is None else dkr + kr_part
        dv16 = dv_ref[...].astype(BF16)
        dkvp_ref[:, 512:1024] = dv16
        dckvn = dckvn + _dot_nt(dv16, wkv_ref[:, 512:1024])
        dkr_ref[...] = unrope(dkr).astype(BF16)
        dcq, dgq = _rms_bwd(dcqn, cq_ref[...], gq_ref[...])
        dckv, dgkv = _rms_bwd(dckvn, ckv_ref[...], gkv_ref[...])
        dcq_ref[...] = dcq.astype(BF16)
        dckv_ref[...] = dckv.astype(BF16)
        _acc_rows(dgq_ref, dgq, first)
        _acc_rows(dgkv_ref, dgkv, first)

    tab = pl.BlockSpec((tm, 128), lambda i: (i % nt, 0))
    return pl.pallas_call(
        body, name="mla_prep_bwd", grid=(t // tm,),
        in_specs=[pl.BlockSpec((1024, tm), lambda i: (0, i)), _rows(tm, 1024), _rows(tm, 512), _rows(tm, Q_LORA),
                  _rows(tm, KV_LORA),
                  _full((1, Q_LORA)), _full((1, KV_LORA)), _full((1024, Q_LORA)), _full((KV_LORA, 1024)), tab, tab],
        out_specs=[_rows(tm, Q_LORA), _rows(tm, KV_LORA), _rows(tm, 128), _rows(tm, 1024), _rows(tm, 1024),
                   _full((1, Q_LORA)), _full((1, KV_LORA))],
        out_shape=[jax.ShapeDtypeStruct((t, Q_LORA), BF16), jax.ShapeDtypeStruct((t, KV_LORA), BF16),
                   jax.ShapeDtypeStruct((t, 128), BF16), jax.ShapeDtypeStruct((t, 1024), BF16),
                   jax.ShapeDtypeStruct((t, 1024), BF16), jax.ShapeDtypeStruct((1, Q_LORA), F32),
                   jax.ShapeDtypeStruct((1, KV_LORA), F32)],
        compiler_params=_params(1),
    )(dqt, dk, dv, cq, ckv, g_qa, g_kva, w_q_p, w_kv_p, cosx, sinx)


def _attn_dims(t, n_batch):
    s = t // n_batch
    tq = _pick(s, 1024, 128)
    return s, tq, s // tq


def _grid_ends(n_axes):
    ids = [pl.program_id(a) for a in range(n_axes)]
    first = functools.reduce(jnp.logical_and, [i == 0 for i in ids])
    last = functools.reduce(jnp.logical_and, [i == pl.num_programs(a) - 1 for a, i in enumerate(ids)])
    return first, last


def _attn_fwd(q, k, v, wsrcs, *, n_batch):
    t = q.shape[0]
    s, tq, nq = _attn_dims(t, n_batch)
    nw = len(wsrcs)

    def body(q_ref, k_ref, v_ref, *refs):
        w_refs, (o_ref, lse_ref), wall_refs = refs[:nw], refs[nw:nw + 2], refs[nw + 2:2 * nw + 2]
        send_sems, recv_sems, local_sems = refs[2 * nw + 2:]
        first, last = _grid_ends(3)

        @pl.when(first)
        def _():
            for i in range(nw):
                _gather_start(w_refs[i], wall_refs[i], send_sems, recv_sems, local_sems.at[i], 3 * i)

        @pl.when(last)
        def _():
            for i in range(nw):
                _gather_wait(w_refs[i], wall_refs[i], send_sems, recv_sems, local_sems.at[i], 3 * i)

        kv, vv = k_ref[...], v_ref[...]
        groups = _row_groups(tq)
        raw = [_dot_nt(q_ref[g, :], kv) for g in groups]
        m = [jnp.max(a, axis=-1, keepdims=True) for a in raw]
        p = [jnp.exp2((a - b) * SCALE_LOG2E) for a, b in zip(raw, m)]
        l = [jnp.sum(a, axis=-1, keepdims=True) for a in p]
        for g, pg, mg, lg in zip(groups, p, m, l):
            o_ref[g, :] = _dot(pg.astype(BF16), vv) / lg
            lse2 = mg * SCALE_LOG2E + jnp.log(lg) * LOG2E
            lse_ref[:, g] = jnp.transpose(jnp.broadcast_to(lse2, (lse2.shape[0], 128)))[0:1, :]

    return pl.pallas_call(
        body, name="attn_fwd", grid=(n_batch, B_HEADS, nq),
        in_specs=[pl.BlockSpec((tq, QK_PAD), lambda b, h, i: (b * nq + i, h)),
                  pl.BlockSpec((s, QK_PAD), lambda b, h, i: (b, h)),
                  pl.BlockSpec((s, B_V), lambda b, h, i: (b, h))] + [_HBM] * nw,
        out_specs=[pl.BlockSpec((tq, B_V), lambda b, h, i: (b * nq + i, h)),
                   pl.BlockSpec((None, 1, tq), lambda b, h, i: (h, 0, b * nq + i))] + [_HBM] * nw,
        out_shape=[jax.ShapeDtypeStruct((t, B_HEADS * B_V), F32), jax.ShapeDtypeStruct((B_HEADS, 1, t), F32)]
        + [jax.ShapeDtypeStruct((N_CHIPS,) + w.shape, w.dtype) for w in wsrcs],
        scratch_shapes=[pltpu.SemaphoreType.DMA((3 * nw,)), pltpu.SemaphoreType.DMA((3 * nw,)),
                        pltpu.SemaphoreType.DMA((nw,))],
        compiler_params=_params(3),
    )(q, k, v, *wsrcs)


def _attn_bwd(q, k, v, do16, lse, delta, gsegs, *, n_batch):
    t = q.shape[0]
    s = t // n_batch
    tk = _pick(s, 512, 128)
    nk = s // tk
    ng = len(gsegs)

    def body(q_ref, k_ref, v_ref, do_ref, lse_ref, dl_ref, *refs):
        g_refs, (dqt_ref, dk_ref, dv_ref), recv_refs = refs[:ng], refs[ng:ng + 3], refs[ng + 3:2 * ng + 3]
        send_sems, recv_sems = refs[2 * ng + 3:]
        first, last = _grid_ends(3)

        @pl.when(first)
        def _():
            for i in range(ng):
                _scatter_start(g_refs[i], recv_refs[i], send_sems, recv_sems, 3 * i)

        @pl.when(last)
        def _():
            for i in range(ng):
                _scatter_wait(g_refs[i], recv_refs[i], send_sems, recv_sems, 3 * i)

        j = pl.program_id(2)
        qv, kv, dov = q_ref[...], k_ref[...], do_ref[...]
        pt = jnp.exp2(_dot_nt(kv, qv) * SCALE_LOG2E - lse_ref[...])
        dv_ref[...] = _dot(pt.astype(BF16), dov)
        dpt = _dot_nt(v_ref[...], dov)
        dst = (pt * (dpt - dl_ref[...])).astype(BF16)
        dk_ref[...] = _dot(dst, qv) * ATTN_SCALE
        part = _dot_tn(kv, dst)

        @pl.when(j == 0)
        def _():
            dqt_ref[...] = part

        @pl.when(j != 0)
        def _():
            dqt_ref[...] += part

        @pl.when(j == nk - 1)
        def _():
            dqt_ref[...] = dqt_ref[...] * ATTN_SCALE

    row = pl.BlockSpec((None, 1, s), lambda b, h, j: (h, 0, b))
    return pl.pallas_call(
        body, name="attn_bwd", grid=(n_batch, B_HEADS, nk),
        in_specs=[pl.BlockSpec((s, QK_PAD), lambda b, h, j: (b, h)),
                  pl.BlockSpec((tk, QK_PAD), lambda b, h, j: (b * nk + j, h)),
                  pl.BlockSpec((tk, B_V), lambda b, h, j: (b * nk + j, h)),
                  pl.BlockSpec((s, B_V), lambda b, h, j: (b, h)), row, row] + [_HBM] * ng,
        out_specs=[pl.BlockSpec((QK_PAD, s), lambda b, h, j: (h, b)),
                   pl.BlockSpec((tk, QK_PAD), lambda b, h, j: (b * nk + j, h)),
                   pl.BlockSpec((tk, B_V), lambda b, h, j: (b * nk + j, h))] + [_HBM] * ng,
        out_shape=[jax.ShapeDtypeStruct((B_HEADS * QK_PAD, t), F32), jax.ShapeDtypeStruct((t, B_HEADS * QK_PAD), F32),
                   jax.ShapeDtypeStruct((t, B_HEADS * B_V), F32)]
        + [jax.ShapeDtypeStruct((3,) + g.shape[1:], g.dtype) for g in gsegs],
        scratch_shapes=[pltpu.SemaphoreType.DMA((3 * ng,)), pltpu.SemaphoreType.DMA((3 * ng,))],
        compiler_params=_params(3),
    )(q, k, v, do16, lse, delta, *gsegs)


def _group_ones16():
    r = _iota2((A_WIDTH, A_WIDTH), 0) // 64
    c = _iota2((A_WIDTH, A_WIDTH), 1) // 64
    return (r == c).astype(BF16)


def _head_rms(o, ones16):
    return lax.rsqrt(_exact_right(o * o, ones16) * (1.0 / 64.0) + EPS)


def _out_fwd(o_f, o_r, hg, o_attn, x, g_hn, g_mla, w_out, g2):
    t = x.shape[0]
    tm = _row_tile(t, 4 * ROW_GROUP)

    def body(of_ref, or_ref, hg_ref, oa_ref, x_ref, ghn_ref, gm_ref, w_ref, g2_ref, y_ref, x2_ref, h2_ref):
        ones16 = _group_ones16()
        ghn, gm, g2v = ghn_ref[...], gm_ref[...], g2_ref[...]

        def group(ofv, orv, hgv, oav, xv):
            o = ofv + orv
            ya16 = (((o * _head_rms(o, ones16)) * ghn) * (hgv * _sigmoid(hgv))).astype(BF16)
            yb16 = _rms_fwd(oav, gm).astype(BF16)
            x2 = xv + _dot(ya16, w_ref[0:A_WIDTH, :]) + _dot(yb16, w_ref[A_WIDTH:D_MODEL, :])
            return ya16, yb16, x2, _rms_fwd(x2, g2v).astype(BF16)

        ins = (of_ref, or_ref, hg_ref, oa_ref, x_ref)
        groups = _row_groups(tm)
        done = [group(*vals) for vals in [[ref[g, :] for ref in ins] for g in groups]]
        for g, (ya16, yb16, x2, h2) in zip(groups, done):
            y_ref[g, 0:A_WIDTH] = ya16
            y_ref[g, A_WIDTH:D_MODEL] = yb16
            x2_ref[g, :] = x2
            h2_ref[g, :] = h2

    a512 = _rows(tm, A_WIDTH)
    return pl.pallas_call(
        body, name="out_fwd", grid=(t // tm,),
        in_specs=[a512, a512, a512, a512, _rows(tm, D_MODEL), _full((1, A_WIDTH)), _full((1, A_WIDTH)),
                  _full((D_MODEL, D_MODEL)), _full((1, D_MODEL))],
        out_specs=[_rows(tm, D_MODEL)] * 3,
        out_shape=[jax.ShapeDtypeStruct((t, D_MODEL), BF16), jax.ShapeDtypeStruct((t, D_MODEL), F32),
                   jax.ShapeDtypeStruct((t, D_MODEL), BF16)],
        compiler_params=_params(1),
    )(o_f, o_r, hg, o_attn, x, g_hn, g_mla, w_out, g2)


def _out_bwd(dx2_16, o_f, o_r, hg, o_attn, g_hn, g_mla, w_out):
    t = dx2_16.shape[0]
    tm = _row_tile(t, 4 * ROW_GROUP)

    def body(dx_ref, of_ref, or_ref, hg_ref, oa_ref, ghn_ref, gm_ref, w_ref,
             do_ref, dhg_ref, doa_ref, dl_ref, dghn_ref, dgm_ref):
        first = pl.program_id(0) == 0
        ones16 = _group_ones16()
        sel16 = (_iota2((8, A_WIDTH), 0) == _iota2((8, A_WIDTH), 1) // B_V).astype(BF16)
        ghn, gm = ghn_ref[...], gm_ref[...]

        def group(dxv, ofv, orv, hgv, oav):
            dya = _dot_nt(dxv, w_ref[0:A_WIDTH, :])
            dyb = _dot_nt(dxv, w_ref[A_WIDTH:D_MODEL, :])
            o = ofv + orv
            rh = _head_rms(o, ones16)
            oh = o * rh
            sg = _sigmoid(hgv)
            sl = hgv * sg
            dhg = ((dya * (oh * ghn)) * (sg * (1.0 + hgv * (1.0 - sg)))).astype(BF16)
            doh = dya * sl * ghn
            do = rh * (doh - oh * (_exact_right(doh * oh, ones16) * (1.0 / 64.0)))
            doa, dgm = _rms_bwd(dyb, oav, gm)
            x1, x2, x3 = _split3(doa * oav)
            delta = _dot_nt(sel16, x1) + _dot_nt(sel16, x2) + _dot_nt(sel16, x3)
            return do, dhg, doa.astype(BF16), delta, dya * sl * oh, dgm

        groups = _row_groups(tm)
        loaded = [(dx_ref[g, :], of_ref[g, :], or_ref[g, :], hg_ref[g, :], oa_ref[g, :]) for g in groups]
        done = [group(*vals) for vals in loaded]
        for g, (do, dhg, doa16, delta, _, _) in zip(groups, done):
            do_ref[g, :] = do
            dhg_ref[g, :] = dhg
            doa_ref[g, :] = doa16
            for h in range(B_HEADS):
                dl_ref[h, :, g] = delta[h:h + 1, :]
        _acc_rows(dghn_ref, jnp.concatenate([d[4] for d in done], axis=0), first)
        _acc_rows(dgm_ref, jnp.concatenate([d[5] for d in done], axis=0), first)

    a512 = _rows(tm, A_WIDTH)
    return pl.pallas_call(
        body, name="out_bwd", grid=(t // tm,),
        in_specs=[_rows(tm, D_MODEL), a512, a512, a512, a512, _full((1, A_WIDTH)), _full((1, A_WIDTH)),
                  _full((D_MODEL, D_MODEL))],
        out_specs=[a512, a512, a512, pl.BlockSpec((B_HEADS, 1, tm), lambda i: (0, 0, i)),
                   _full((1, A_WIDTH)), _full((1, A_WIDTH))],
        out_shape=[jax.ShapeDtypeStruct((t, A_WIDTH), F32)] + [jax.ShapeDtypeStruct((t, A_WIDTH), BF16)] * 2
        + [jax.ShapeDtypeStruct((B_HEADS, 1, t), F32)]
        + [jax.ShapeDtypeStruct((1, A_WIDTH), F32)] * 2,
        compiler_params=_params(1),
    )(dx2_16, o_f, o_r, hg, o_attn, g_hn, g_mla, w_out)


def _ffn_fwd_bwd(h2, x2, target, w_gate, w_up, w_down, g_f, g2):
    t = x2.shape[0]
    tm = _row_tile(t)
    inv_d = 1.0 / D_MODEL

    def body(h2_ref, x2_ref, tg_ref, wg_ref, wu_ref, wd_ref, gf_ref, g2_ref,
             act_ref, dgate_ref, dup_ref, dx3_ref, dx2_ref, dx2h_ref, loss_ref, dgf_ref, dg2_ref):
        first = pl.program_id(0) == 0
        h2v = h2_ref[...]
        gate = _dot_nt(h2v, wg_ref[...])
        up = _dot_nt(h2v, wu_ref[...])
        sg = _sigmoid(gate)
        sl = gate * sg
        act16 = (sl * up).astype(BF16)
        act_ref[...] = act16
        x2v = x2_ref[...]
        x3 = x2v + _dot(act16, wd_ref[...])
        r3 = lax.rsqrt(jnp.mean(x3 * x3, axis=-1, keepdims=True) + EPS)
        x3h = x3 * r3
        gf = gf_ref[...]
        err = x3h * gf - tg_ref[...]
        part = 0.5 * jnp.sum(jnp.mean(err * err, axis=-1, keepdims=True), axis=0, keepdims=True)

        @pl.when(first)
        def _():
            loss_ref[...] = jnp.zeros_like(loss_ref)

        loss_ref[...] += part
        dy = err * inv_d
        _acc_rows(dgf_ref, dy * x3h, first)
        dx3h = dy * gf
        dx3 = r3 * (dx3h - x3h * jnp.mean(dx3h * x3h, axis=-1, keepdims=True))
        dx3_16 = dx3.astype(BF16)
        dx3_ref[...] = dx3_16
        da = _dot_nt(dx3_16, wd_ref[...])
        dup16 = (da * sl).astype(BF16)
        dgate16 = (da * up * (sg * (1.0 + gate * (1.0 - sg)))).astype(BF16)
        dup_ref[...] = dup16
        dgate_ref[...] = dgate16
        dh2 = _dot(dgate16, wg_ref[...]) + _dot(dup16, wu_ref[...])
        dx2n, dg2 = _rms_bwd(dh2, x2v, g2_ref[...])
        _acc_rows(dg2_ref, dg2, first)
        dx2 = dx3 + dx2n
        dx2_ref[...] = dx2
        dx2h_ref[...] = dx2.astype(BF16)

    return pl.pallas_call(
        body, name="ffn_fwd_bwd", grid=(t // tm,),
        in_specs=[_rows(tm, D_MODEL), _rows(tm, D_MODEL), _rows(tm, D_MODEL), _full((D_FF, D_MODEL), True),
                  _full((D_FF, D_MODEL), True), _full((D_FF, D_MODEL), True), _full((1, D_MODEL)), _full((1, D_MODEL))],
        out_specs=[_rows(tm, D_FF), _rows(tm, D_FF), _rows(tm, D_FF), _rows(tm, D_MODEL), _rows(tm, D_MODEL),
                   _rows(tm, D_MODEL), _full((8, 128)), _full((1, D_MODEL)), _full((1, D_MODEL))],
        out_shape=[jax.ShapeDtypeStruct((t, D_FF), BF16)] * 3
        + [jax.ShapeDtypeStruct((t, D_MODEL), BF16), jax.ShapeDtypeStruct((t, D_MODEL), F32),
           jax.ShapeDtypeStruct((t, D_MODEL), BF16), jax.ShapeDtypeStruct((8, 128), F32),
           jax.ShapeDtypeStruct((1, D_MODEL), F32), jax.ShapeDtypeStruct((1, D_MODEL), F32)],
        compiler_params=_params(1),
    )(h2, x2, target, w_gate, w_up, w_down, g_f, g2)


def _wgrad(a, b, name):
    t, m = a.shape
    n = b.shape[1]
    bm = _pick(m, 1664)
    bn = _pick(n, 1664)
    tk = _pick(t, 2048, 16)
    nk = t // tk

    def body(a_ref, b_ref, o_ref, o16_ref):
        k = pl.program_id(2)
        part = _dot_tn(a_ref[...], b_ref[...])

        @pl.when(k == 0)
        def _():
            o_ref[...] = part

        @pl.when(k != 0)
        def _():
            o_ref[...] += part

        @pl.when(k == nk - 1)
        def _():
            o16_ref[...] = o_ref[...].astype(BF16)

    out = pl.BlockSpec((bm, bn), lambda i, j, k: (i, j))
    return pl.pallas_call(
        body, name=name, grid=(m // bm, n // bn, nk),
        in_specs=[pl.BlockSpec((tk, bm), lambda i, j, k: (k, i)), pl.BlockSpec((tk, bn), lambda i, j, k: (k, j))],
        out_specs=[out, out],
        out_shape=[jax.ShapeDtypeStruct((m, n), F32), jax.ShapeDtypeStruct((m, n), BF16)],
        compiler_params=_params(3),
    )(a, b)


def _rope_tables(seq):
    inv = 1.0 / (ROPE_THETA ** (jnp.arange(0, B_ROPE, 2, dtype=F32) / B_ROPE))
    ang = jnp.arange(seq, dtype=F32)[:, None] * inv[None, :]
    cos, sin = jnp.cos(ang), jnp.sin(ang)
    zeros = jnp.zeros((seq, 64), F32)
    return jnp.concatenate([cos, cos, zeros], axis=1), jnp.concatenate([-sin, sin, zeros], axis=1)


def _pad_weights(w_in_t, w_q_t, w_kv_b):
    w_in_p = jnp.pad(w_in_t, ((0, D_IN_PAD - D_IN), (0, 0)))
    w_q_p = jnp.pad(w_q_t.reshape(B_HEADS, B_NOPE + B_ROPE, Q_LORA), ((0, 0), (0, 64), (0, 0))).reshape(1024, Q_LORA)
    kv = w_kv_b.reshape(KV_LORA, B_HEADS, B_NOPE + B_V)
    w_kv_p = jnp.concatenate([kv[:, :, :B_NOPE].reshape(KV_LORA, 512), kv[:, :, B_NOPE:].reshape(KV_LORA, 512)], axis=1)
    return w_in_p, w_q_p, w_kv_p


def _unpad_kv(g_kv_p):
    return jnp.concatenate([g_kv_p[:, :512].reshape(KV_LORA, B_HEADS, B_NOPE),
                            g_kv_p[:, 512:].reshape(KV_LORA, B_HEADS, B_V)], axis=2).reshape(KV_LORA, 1024)


_OWN_ROWS = dict(w_in=(272, 3), w_q_b=(64, 4), w_kv_b=(256, 0), w_out=(256, 1), w_gate=(176, 4), w_up=(176, 4), w_down=(176, 4))


def _local_step(x, target, lbl, g1, g_hn, g_qa, g_kva, g_mla, g2, g_f, w_in, w_q_b, w_kv_b, late_shards):
    n_batch, seq, _ = x.shape
    t = n_batch * seq
    x = x.reshape(t, D_MODEL)
    target = target.reshape(t, D_MODEL)
    w_in_p, w_q_p, w_kv_p = _pad_weights(w_in, w_q_b, w_kv_b)
    cosx, sinx = _rope_tables(seq)

    h1, hq, hi, hff, hfb, hg, cq, ckv, kr = _in_fwd(x, g1, w_in_p)
    o_f, st_f = _hgrn_fwd(hq, hi, hff, lbl, n_batch=n_batch, direction=0)
    o_r, st_r = _hgrn_fwd(hq, hi, hfb, lbl, n_batch=n_batch, direction=1)
    q, k, v, cqn, ckvn = _mla_prep(cq, ckv, kr, g_qa, g_kva, w_q_p, w_kv_p, cosx, sinx, n_batch=n_batch)
    o_attn, lse, *gathered = _attn_fwd(q, k, v, late_shards, n_batch=n_batch)
    w_out, w_gate, w_up, w_down = (_join_shards(n, a) for n, a in zip(_LATE, gathered))
    ycat, x2, h2 = _out_fwd(o_f, o_r, hg, o_attn, x, g_hn, g_mla, w_out, g2)
    act, dgate, dup, dx3_16, dx2, dx2_16, loss, dg_f, dg2 = _ffn_fwd_bwd(h2, x2, target, w_gate, w_up, w_down, g_f, g2)
    full = dict(w_out=_wgrad(ycat, dx2_16, "wgrad_out"), w_gate=_wgrad(dgate, h2, "wgrad_gate"),
                w_up=_wgrad(dup, h2, "wgrad_up"), w_down=_wgrad(act, dx3_16, "wgrad_down"))
    do_h, dhg, do_attn16, delta, dg_hn, dg_mla = _out_bwd(dx2_16, o_f, o_r, hg, o_attn, g_hn, g_mla, w_out)
    dqt, dk, dv, *recv_late = _attn_bwd(q, k, v, do_attn16, lse, delta,
                                        [_segments(n, full[n][1]) for n in _LATE], n_batch=n_batch)
    dcq, dckv, dkr, dqp16, dkvp16, dg_qa, dg_kva = _mla_prep_bwd(dqt, dk, dv, cq, ckv, g_qa, g_kva, w_q_p, w_kv_p,
                                                                  cosx, sinx, n_batch=n_batch)
    dq_f, dv_f, dz_f, dl_f = _hgrn_bwd(hq, hi, hff, do_h, st_f, lbl, n_batch=n_batch, direction=0)
    dq_r, dv_r, dz_r, dl_r = _hgrn_bwd(hq, hi, hfb, do_h, st_r, lbl, n_batch=n_batch, direction=1)
    dx, dproj16, dg1 = _in_bwd(dq_f, dq_r, dv_f, dv_r, dz_f, dz_r, dhg, hq, dcq, dckv, dkr, dx2, x, g1, w_in_p)

    full.update(w_in=_wgrad(dproj16, h1, "wgrad_in"), w_q_b=_wgrad(dqp16, cqn, "wgrad_q_b"))
    g_kv = _unpad_kv(_wgrad(ckvn, dkvp16, "wgrad_kv_b")[0])
    early16 = dict(w_in=full["w_in"][1][:D_IN].reshape(N_CHIPS, D_IN // N_CHIPS, D_MODEL),
                   w_q_b=full["w_q_b"][1].reshape(B_HEADS, QK_PAD, Q_LORA)[:, :B_NOPE + B_ROPE],
                   w_kv_b=_segments("w_kv_b", g_kv).astype(BF16))
    small = dict(norm1_g=dg1, lb_logits=jnp.stack([dl_f, dl_r]), hgrn_norm_g=dg_hn, q_a_norm_g=dg_qa, kv_a_norm_g=dg_kva,
                 mla_norm_g=dg_mla, norm2_g=dg2, final_norm_g=dg_f, loss=loss[0:1, 0])
    return dx.reshape(n_batch, seq, D_MODEL), small, {**{n: f for n, (f, _) in full.items()}, "w_kv_b": g_kv}, early16, \
        dict(zip(_LATE, recv_late))


_HBM = pl.BlockSpec(memory_space=pltpu.HBM)
_MESH = pl.DeviceIdType.MESH


def _place():
    x, y, c = lax.axis_index("x"), lax.axis_index("y"), lax.axis_index("c")
    other_chips = [(1 - x, y), (x, 1 - y), (1 - x, 1 - y)]
    return x, y, c, other_chips


def _gather_copies(w_ref, wall_ref, send_sems, recv_sems, local_sem, base=0):
    x, y, c, chips = _place()
    mine = 2 * x + y

    def mk(j, chip_index, to):
        return pltpu.make_async_remote_copy(src_ref=w_ref, dst_ref=wall_ref.at[chip_index], send_sem=send_sems.at[base + j],
                                            recv_sem=recv_sems.at[base + j], device_id=to, device_id_type=_MESH)

    local = pltpu.make_async_copy(w_ref, wall_ref.at[mine], local_sem)
    sends = [mk(j, mine, (*chip, c)) for j, chip in enumerate(chips)]
    recvs = [mk(j, 2 * px + py, (x, y, c)) for j, (px, py) in enumerate(chips)]
    return local, sends, recvs


def _gather_start(*refs):
    local, sends, _ = _gather_copies(*refs)
    local.start()
    for cp in sends:
        cp.start()


def _gather_wait(*refs):
    local, sends, recvs = _gather_copies(*refs)
    for cp in recvs:
        cp.wait_recv()
    for cp in sends:
        cp.wait_send()
    local.wait()


def _scatter_copies(g_ref, recv_ref, send_sems, recv_sems, base=0):
    x, y, c, chips = _place()

    def mk(j, src_index, to):
        return pltpu.make_async_remote_copy(src_ref=g_ref.at[src_index], dst_ref=recv_ref.at[j],
                                            send_sem=send_sems.at[base + j], recv_sem=recv_sems.at[base + j],
                                            device_id=to, device_id_type=_MESH)

    sends = [mk(j, 2 * px + py, (px, py, c)) for j, (px, py) in enumerate(chips)]
    recvs = [mk(j, 0, (x, y, c)) for j in range(3)]
    return sends, recvs


def _scatter_start(*refs):
    for cp in _scatter_copies(*refs)[0]:
        cp.start()


def _scatter_wait(*refs):
    sends, recvs = _scatter_copies(*refs)
    for cp in recvs:
        cp.wait_recv()
    for cp in sends:
        cp.wait_send()


def _half(ref, which, axis):
    n = ref.shape[axis] // 2
    idx = [slice(None)] * len(ref.shape)
    idx[axis] = pl.ds(which * n, n)
    return ref.at[tuple(idx)]


def _split_axis(a):
    return 1 if a.shape[1] % 256 == 0 else 0


def _gather_halves(src_ref, dst_ref, send_sems, recv_sems, local_sem, base, axis):
    x, y, c, chips = _place()
    mine = 2 * x + y

    def remote(src, dst, n, to):
        return pltpu.make_async_remote_copy(src_ref=src, dst_ref=dst, send_sem=send_sems.at[base + n],
                                            recv_sem=recv_sems.at[base + n], device_id=to, device_id_type=_MESH)

    def slot(chip_index, which):
        return _half(dst_ref.at[chip_index], which, axis)

    local = pltpu.make_async_copy(src_ref, dst_ref.at[mine], local_sem)
    first = [remote(_half(src_ref, c, axis), slot(mine, c), j, (*chip, c)) for j, chip in enumerate(chips)]
    landed = [remote(_half(src_ref, c, axis), slot(2 * px + py, c), j, (x, y, c)) for j, (px, py) in enumerate(chips)]
    passed = [remote(slot(2 * px + py, c), slot(2 * px + py, c), 3 + j, (x, y, 1 - c)) for j, (px, py) in enumerate(chips)]
    handed = [remote(_half(src_ref, c, axis), slot(2 * px + py, 1 - c), 3 + j, (x, y, c)) for j, (px, py) in enumerate(chips)]

    def start():
        local.start()
        for cp in first:
            cp.start()

    def pass_on():
        for arrived, onward in zip(landed, passed):
            arrived.wait_recv()
            onward.start()

    def finish():
        for cp in handed:
            cp.wait_recv()
        for cp in first + passed:
            cp.wait_send()
        local.wait()

    return start, pass_on, finish


def _gather_shards(srcs, small):
    n = len(srcs)

    def body(*refs):
        src_refs, s_ref, dst_refs, sall_ref = refs[:n], refs[n], refs[n + 1:2 * n + 1], refs[2 * n + 1]
        send_sems, recv_sems, local_sems = refs[2 * n + 2:]
        _gather_start(s_ref, sall_ref, send_sems, recv_sems, local_sems.at[n], 6 * n)
        steps = [_gather_halves(src_refs[i], dst_refs[i], send_sems, recv_sems, local_sems.at[i], 6 * i, _split_axis(srcs[i]))
                 for i in range(n)]
        for phase in range(3):
            for step in steps:
                step[phase]()
        _gather_wait(s_ref, sall_ref, send_sems, recv_sems, local_sems.at[n], 6 * n)

    return pl.pallas_call(
        body, name="gather_shards", in_specs=[_HBM] * (n + 1), out_specs=[_HBM] * (n + 1),
        out_shape=[jax.ShapeDtypeStruct((N_CHIPS,) + a.shape, a.dtype) for a in list(srcs) + [small]],
        scratch_shapes=[pltpu.SemaphoreType.DMA((6 * n + 3,)), pltpu.SemaphoreType.DMA((6 * n + 3,)),
                        pltpu.SemaphoreType.DMA((n + 1,))],
    )(*srcs, small)


_SEM = pl.BlockSpec(memory_space=pltpu.SEMAPHORE)
_SIDE_EFFECT = pltpu.SideEffectType.DATAFLOW_SIDE_EFFECTING


def _scatter_begin(gsegs):
    n = len(gsegs)
    def hbm(a):
        return pltpu.with_memory_space_constraint(a, pltpu.HBM)

    lands = [hbm(lax.empty((3,) + g.shape[1:], g.dtype)) for g in gsegs]

    def body(*refs):
        g_refs, land_refs, send_sems, recv_sems, token = refs[:n], refs[n:2 * n], refs[2 * n], refs[2 * n + 1], refs[-1]
        for i in range(n):
            _scatter_start(g_refs[i], land_refs[i], send_sems, recv_sems, 3 * i)
        token[...] = jnp.zeros_like(token)

    out = pl.pallas_call(
        body, name="scatter_begin", in_specs=[_HBM] * (2 * n),
        out_specs=(_SEM, _SEM) + (_HBM,) * (2 * n) + (pl.BlockSpec(memory_space=pltpu.VMEM),),
        out_shape=(pltpu.SemaphoreType.DMA((3 * n,)), pltpu.SemaphoreType.DMA((3 * n,)))
        + tuple(pltpu.HBM(a.shape, a.dtype) for a in list(gsegs) + lands) + (jax.ShapeDtypeStruct((8, 128), F32),),
        input_output_aliases={i: 2 + i for i in range(2 * n)},
        compiler_params=pltpu.CompilerParams(has_side_effects=_SIDE_EFFECT),
    )(*[hbm(g) for g in gsegs], *lands)
    return out[0], out[1], out[2:2 + n], out[2 + n:2 + 2 * n], out[-1]


def _scatter_end(send_sems, recv_sems, gsegs, lands, after):
    n = len(gsegs)

    def body(*refs):
        g_refs, land_refs, ssems, rsems = refs[:n], refs[n:2 * n], refs[2 * n], refs[2 * n + 1]
        for i in range(n):
            _scatter_wait(g_refs[i], land_refs[i], ssems, rsems, 3 * i)

    out = pl.pallas_call(
        body, name="scatter_end", in_specs=[_HBM] * (2 * n) + [_SEM, _SEM] + [pl.BlockSpec(memory_space=pl.ANY)] * len(after),
        out_specs=(_HBM,) * (2 * n), out_shape=tuple(pltpu.HBM(a.shape, a.dtype) for a in list(gsegs) + list(lands)),
        input_output_aliases={i: i for i in range(2 * n)},
        compiler_params=pltpu.CompilerParams(has_side_effects=_SIDE_EFFECT),
    )(*gsegs, *lands, send_sems, recv_sems, *after)
    return out[n:]


def _swap_sibling(parts, small=None):
    n = len(parts)
    extra = small is not None

    def body(*refs):
        ins, outs = refs[:n + extra], refs[n + extra:2 * (n + extra)]
        send_sems, recv_sems, local_sem = refs[2 * (n + extra):]
        x, y, c, _ = _place()
        cps = [pltpu.make_async_remote_copy(src_ref=ins[i], dst_ref=outs[i], send_sem=send_sems.at[i],
                                            recv_sem=recv_sems.at[i], device_id=(x, y, 1 - c), device_id_type=_MESH)
               for i in range(n)]
        for cp in cps:
            cp.start()
        if extra:
            s_ref, sall_ref = ins[n], outs[n]
            me = 4 * x + 2 * y + c
            flips = [(fx, fy, fc) for fx in (0, 1) for fy in (0, 1) for fc in (0, 1)][1:]

            def peer(f):
                return tuple((1 - a) if b else a for a, b in zip((x, y, c), f))

            def sm(r, index, to):
                return pltpu.make_async_remote_copy(src_ref=s_ref, dst_ref=sall_ref.at[index], send_sem=send_sems.at[n + r],
                                                    recv_sem=recv_sems.at[n + r], device_id=to, device_id_type=_MESH)

            local = pltpu.make_async_copy(s_ref, sall_ref.at[me], local_sem)
            local.start()
            sends = [sm(r, me, peer(f)) for r, f in enumerate(flips)]
            for cp in sends:
                cp.start()
            for r, f in enumerate(flips):
                px, py, pc = peer(f)
                sm(r, 4 * px + 2 * py + pc, (x, y, c)).wait_recv()
            for cp in sends:
                cp.wait_send()
            local.wait()
        for cp in cps:
            cp.wait()

    return pl.pallas_call(
        body, name="swap_small" if extra else "swap_sibling", in_specs=[_HBM] * (n + extra), out_specs=[_HBM] * (n + extra),
        out_shape=[jax.ShapeDtypeStruct(p.shape, p.dtype) for p in parts]
        + ([jax.ShapeDtypeStruct((8,) + small.shape, F32)] if extra else []),
        scratch_shapes=[pltpu.SemaphoreType.DMA((n + 7,)), pltpu.SemaphoreType.DMA((n + 7,)), pltpu.SemaphoreType.DMA(())],
    )(*parts, *([small] if extra else []))


def _sum_segments(chip, full, recv, tm, pitch, name, after=()):
    _, rows, cols = recv.shape

    def body(chip_ref, o_ref, r_ref, *rest):
        acc = o_ref[...]
        for j in range(3):
            acc = acc + r_ref[j].astype(F32)
        rest[-1][...] = acc

    return pl.pallas_call(
        body, name=name,
        grid_spec=pltpu.PrefetchScalarGridSpec(
            num_scalar_prefetch=1, grid=(rows // tm,),
            in_specs=[pl.BlockSpec((tm, cols), lambda i, c: (c[0] * pitch + i, 0)),
                      pl.BlockSpec((3, tm, cols), lambda i, c: (0, i, 0))]
            + [pl.BlockSpec(a.shape, lambda i, c, nd=a.ndim: (0,) * nd) for a in after],
            out_specs=pl.BlockSpec((tm, cols), lambda i, c: (i, 0))),
        out_shape=jax.ShapeDtypeStruct((rows, cols), F32), compiler_params=_params(1),
    )(chip, full, recv, *after)


def _sum_devices(sall):
    def body(s_ref, o_ref):
        acc = s_ref[0]
        for d in range(1, 8):
            acc = acc + s_ref[d]
        o_ref[...] = acc

    return pl.pallas_call(body, name="sum_devices", out_shape=jax.ShapeDtypeStruct(sall.shape[1:], F32))(sall)


def _adamw(w, m, v, ga, gb, name):
    rows, cols = w.shape
    tm = _row_tile(rows)
    two = gb is not None

    def body(*refs):
        w_ref, m_ref, v_ref, ga_ref = refs[:4]
        g_ref, d_ref, m2_ref, v2_ref = refs[-4:]
        g = ga_ref[...] + refs[4][...] if two else ga_ref[...]
        m2 = ADAM_B1 * m_ref[...] + (1.0 - ADAM_B1) * g
        v2 = ADAM_B2 * v_ref[...] + (1.0 - ADAM_B2) * (g * g)
        m_hat = m2 / (1.0 - ADAM_B1 ** ADAM_STEP)
        v_hat = v2 / (1.0 - ADAM_B2 ** ADAM_STEP)
        g_ref[...] = g
        d_ref[...] = -ADAM_LR * (m_hat / (jnp.sqrt(v_hat) + ADAM_EPS) + ADAM_WD * w_ref[...])
        m2_ref[...] = m2
        v2_ref[...] = v2

    blk = _rows(tm, cols)
    args = (w, m, v, ga) + ((gb,) if two else ())
    return pl.pallas_call(
        body, name=name, grid=(rows // tm,), in_specs=[blk] * len(args), out_specs=[blk] * 4,
        out_shape=[jax.ShapeDtypeStruct(w.shape, F32)] * 4, compiler_params=_params(1),
    )(*args)


_TRANSPOSED = ("w_in", "w_q_b", "w_gate", "w_up")
_COLUMN_SHARDED = ("w_kv_b",)
_FULL_SHAPES = dict(w_in=(D_IN, D_MODEL), w_q_b=(768, Q_LORA), w_kv_b=(KV_LORA, 1024), w_out=(D_MODEL, D_MODEL),
                    w_gate=(D_FF, D_MODEL), w_up=(D_FF, D_MODEL), w_down=(D_FF, D_MODEL))
_SMALL = (("norm1_g", 1024), ("lb_logits", 2048), ("hgrn_norm_g", 512), ("q_a_norm_g", 384), ("kv_a_norm_g", 256),
          ("mla_norm_g", 512), ("norm2_g", 1024), ("final_norm_g", 1024))
_UPDATE_ROWS = 48


def _pad_rows(a, rows):
    return jnp.pad(a, ((0, rows - a.shape[0]), (0, 0)))


_EARLY = ("w_in", "w_q_b", "w_kv_b")
_LATE = ("w_out", "w_gate", "w_up", "w_down")


def _segments(name, g):
    r, c = g.shape
    if name in _COLUMN_SHARDED:
        return g.reshape(r, N_CHIPS, c // N_CHIPS).transpose(1, 0, 2)
    return g.reshape(N_CHIPS, r // N_CHIPS, c)


def _own_segment(name, g, chip):
    r, c = g.shape
    if name in _COLUMN_SHARDED:
        return lax.dynamic_slice_in_dim(g, chip * (c // N_CHIPS), c // N_CHIPS, axis=1)
    return lax.dynamic_slice_in_dim(g, chip * (r // N_CHIPS), r // N_CHIPS, axis=0)


def _join_shards(name, seg):
    r, c = _FULL_SHAPES[name]
    if name in _COLUMN_SHARDED:
        return seg.transpose(1, 0, 2).reshape(r, c)
    return seg.reshape(r, c)


def kernel(x, norm1_g, w_in, lb_logits, hgrn_norm_g, q_a_norm_g, w_q_b, kv_a_norm_g, w_kv_b, mla_norm_g, w_out, norm2_g, w_gate, w_up, w_down, final_norm_g, loss_target, m_norm1_g, m_w_in, m_lb_logits, m_hgrn_norm_g, m_q_a_norm_g, m_w_q_b, m_kv_a_norm_g, m_w_kv_b, m_mla_norm_g, m_w_out, m_norm2_g, m_w_gate, m_w_up, m_w_down, m_final_norm_g, v_norm1_g, v_w_in, v_lb_logits, v_hgrn_norm_g, v_q_a_norm_g, v_w_q_b, v_kv_a_norm_g, v_w_kv_b, v_mla_norm_g, v_w_out, v_norm2_g, v_w_gate, v_w_up, v_w_down, v_final_norm_g):
    names = ("norm1_g", "w_in", "lb_logits", "hgrn_norm_g", "q_a_norm_g", "w_q_b", "kv_a_norm_g", "w_kv_b", "mla_norm_g",
             "w_out", "norm2_g", "w_gate", "w_up", "w_down", "final_norm_g")
    w = dict(zip(names, (norm1_g, w_in, lb_logits, hgrn_norm_g, q_a_norm_g, w_q_b, kv_a_norm_g, w_kv_b, mla_norm_g,
                         w_out, norm2_g, w_gate, w_up, w_down, final_norm_g)))
    m = dict(zip(names, (m_norm1_g, m_w_in, m_lb_logits, m_hgrn_norm_g, m_q_a_norm_g, m_w_q_b, m_kv_a_norm_g, m_w_kv_b,
                         m_mla_norm_g, m_w_out, m_norm2_g, m_w_gate, m_w_up, m_w_down, m_final_norm_g)))
    v = dict(zip(names, (v_norm1_g, v_w_in, v_lb_logits, v_hgrn_norm_g, v_q_a_norm_g, v_w_q_b, v_kv_a_norm_g, v_w_kv_b,
                         v_mla_norm_g, v_w_out, v_norm2_g, v_w_gate, v_w_up, v_w_down, v_final_norm_g)))
    matrices = _EARLY + _LATE
    chip = 2 * lax.axis_index("x") + lax.axis_index("y")

    def shard2d(a, n):
        return jnp.swapaxes(a[0], 0, 1) if n in _TRANSPOSED else a[0]

    def unshard2d(a, n):
        return (jnp.swapaxes(a, 0, 1) if n in _TRANSPOSED else a)[None]

    w16 = {n: shard2d(w[n], n).astype(BF16) for n in matrices}
    lb8 = _pad_rows(lb_logits.reshape(4, 128), 8)
    *early, lball = _gather_shards([w16[n] for n in _EARLY], lb8)
    lbl = lball[:, :4].reshape(N_CHIPS, 2, 2, 128).transpose(1, 2, 0, 3).reshape(2, 2, A_WIDTH)

    grad_x, g, full, early16, recv = _local_step(
        x, loss_target, lbl, norm1_g, hgrn_norm_g, q_a_norm_g, kv_a_norm_g, mla_norm_g, norm2_g, final_norm_g[None, :],
        *(_join_shards(n, a) for n, a in zip(_EARLY, early)), [w16[n] for n in _LATE])

    n_small = sum(size for _, size in _SMALL)
    small = jnp.concatenate([g[n].reshape(-1) for n, _ in _SMALL] + [g["loss"], jnp.zeros((SMALL_ROWS * 128 - n_small - 1,), F32)])
    chip1 = chip.reshape(1).astype(jnp.int32)
    out, raw = {}, {}

    def update(n, part, sib):
        raw[n] = _adamw(shard2d(w[n], n), shard2d(m[n], n), shard2d(v[n], n), part, sib, f"adamw_{n}")
        out[n] = tuple(unshard2d(r, n) for r in raw[n])

    send_sems, recv_sems, segs, lands, token = _scatter_begin([early16[n] for n in _EARLY])
    parts = [_sum_segments(chip1, full[n], recv[n], *_OWN_ROWS[n], f"sum_{n}", after=(token,)) for n in _LATE]
    for n, part, sib in zip(_LATE, parts, _swap_sibling(parts)):
        update(n, part, sib)
    recv_early = _scatter_end(send_sems, recv_sems, segs, lands, after=[raw[n][1] for n in _LATE])
    full["w_kv_b"] = _own_segment("w_kv_b", full["w_kv_b"], chip)
    parts = [_sum_segments(chip1, full[n], r, *_OWN_ROWS[n], f"sum_{n}") for n, r in zip(_EARLY, recv_early)]
    *sibs, small_all = _swap_sibling(parts, small.reshape(SMALL_ROWS, 128))
    for n, part, sib in zip(_EARLY, parts, sibs):
        update(n, part, sib)
    small_sum = _sum_devices(small_all).reshape(-1)
    loss = small_sum[n_small]

    small_g, off = {}, 0
    for n, size in _SMALL:
        small_g[n] = small_sum[off:off + size]
        off += size
    small_g["lb_logits"] = lax.dynamic_slice_in_dim(small_g["lb_logits"].reshape(2, 2, A_WIDTH), chip * 128, 128, axis=2)
    small_names = tuple(n for n, _ in _SMALL)

    def pack(d):
        return _pad_rows(jnp.concatenate([d[n].reshape(-1) for n in small_names]).reshape(-1, 128), _UPDATE_ROWS)

    res = _adamw(pack(w), pack(m), pack(v), pack(small_g), None, "adamw_small")
    off = 0
    flat = [r.reshape(-1) for r in res]
    for n in small_names:
        size = w[n].size
        out[n] = tuple(f[off:off + size].reshape(w[n].shape) for f in flat)
        off += size

    return (loss, grad_x) + tuple(out[n][i] for i in range(4) for n in names)
```

```python
import functools

import jax
import jax.numpy as jnp
from jax import lax
from jax.experimental import pallas as pl
from jax.experimental.pallas import tpu as pltpu

F32 = jnp.float32
BF16 = jnp.bfloat16

D_MODEL = 1024
A_WIDTH = 512
HEAD_PAIRS = 4
PAIRS_AT_ONCE = 2
CHUNK = 64
B_HEADS = 4
B_NOPE = 128
B_ROPE = 64
B_V = 128
QK_PAD = 256
Q_LORA = 384
KV_LORA = 256
D_FF = 2816
D_IN = 3264
D_IN_PAD = 3328
IN_WIDTHS = (512, 512, 512, 512, 512, Q_LORA, KV_LORA, 128)
ROPE_THETA = 10000.0
EPS = 1e-6
ATTN_SCALE = (B_NOPE + B_ROPE) ** -0.5
LOG2E = 1.4426950408889634
SCALE_LOG2E = ATTN_SCALE * LOG2E

ADAM_LR = 0.001
ADAM_B1 = 0.9
ADAM_B2 = 0.999
ADAM_EPS = 1e-08
ADAM_WD = 0.01
ADAM_STEP = 10

VMEM_LIMIT_BYTES = 60 * 1024 * 1024
N_CHIPS = 4
SMALL_ROWS = 56


def _params(n_axes):
    return pltpu.CompilerParams(dimension_semantics=("arbitrary",) * n_axes,
                                vmem_limit_bytes=VMEM_LIMIT_BYTES)


def _dot(a, b):
    return jnp.dot(a, b, preferred_element_type=F32)


def _dot_nt(a, b):
    return lax.dot_general(a, b, (((1,), (1,)), ((), ())), preferred_element_type=F32)


def _dot_tn(a, b):
    return lax.dot_general(a, b, (((0,), (0,)), ((), ())), preferred_element_type=F32)


def _split3(x):
    x1 = x.astype(BF16)
    r = x - x1.astype(F32)
    x2 = r.astype(BF16)
    x3 = (r - x2.astype(F32)).astype(BF16)
    return x1, x2, x3


def _exact_right(x, m16):
    x1, x2, x3 = _split3(x)
    return _dot(x1, m16) + _dot(x2, m16) + _dot(x3, m16)


def _iota2(shape, dim):
    return lax.broadcasted_iota(jnp.int32, shape, dim)


def _sigmoid(x):
    return jax.nn.sigmoid(x)


def _pick(dim, cap, mult=128):
    if dim <= cap:
        return dim
    best = None
    for d in range(mult, cap + 1, mult):
        if dim % d == 0:
            best = d
    assert best is not None, (dim, cap, mult)
    return best


def _row_tile(t, cap=256):
    return _pick(t, cap, 8)


ROW_GROUP = 256


def _row_groups(tm):
    size = min(tm, ROW_GROUP)
    return [slice(r, r + size) for r in range(0, tm, size)]


def _full(shape, single=False):
    if single:
        return pl.BlockSpec(shape, lambda *_: (0,) * len(shape), pipeline_mode=pl.Buffered(1))
    return pl.BlockSpec(shape, lambda *_: (0,) * len(shape))


def _rows(tm, width):
    return pl.BlockSpec((tm, width), lambda i: (i, 0))


def _acc_rows(ref, val, first):
    s = jnp.sum(val, axis=0, keepdims=True)

    @pl.when(first)
    def _():
        ref[...] = s

    @pl.when(jnp.logical_not(first))
    def _():
        ref[...] += s


def _in_fwd(x, g1, w_in_p):
    t = x.shape[0]
    tm = _row_tile(t, 4 * ROW_GROUP)

    def body(x_ref, g_ref, w_ref, h_ref, *outs):
        gain = g_ref[...]

        def group(xv):
            r = lax.rsqrt(jnp.mean(xv * xv, axis=-1, keepdims=True) + EPS)
            h = ((xv * r) * gain).astype(BF16)
            proj, off = [], 0
            for w in IN_WIDTHS:
                proj.append(_dot_nt(h, w_ref[off:off + w, :]))
                off += w
            return h, proj

        groups = _row_groups(tm)
        done = [group(xv) for xv in [x_ref[g, :] for g in groups]]
        for g, (h, proj) in zip(groups, done):
            h_ref[g, :] = h
            for o_ref, p in zip(outs, proj):
                o_ref[g, :] = p

    return pl.pallas_call(
        body, name="in_fwd", grid=(t // tm,),
        in_specs=[_rows(tm, D_MODEL), _full((1, D_MODEL)), _full((D_IN_PAD, D_MODEL))],
        out_specs=[_rows(tm, D_MODEL)] + [_rows(tm, w) for w in IN_WIDTHS],
        out_shape=[jax.ShapeDtypeStruct((t, D_MODEL), BF16)]
        + [jax.ShapeDtypeStruct((t, w), F32) for w in IN_WIDTHS],
        compiler_params=_params(1),
    )(x, g1, w_in_p)


def _in_bwd(dq_f, dq_r, dv_f, dv_r, dz_f, dz_r, dhg, hq, dcq, dckv, dkr, dx2, x, g1, w_in_p):
    t = x.shape[0]
    tm = _row_tile(t, 2 * ROW_GROUP)

    def body(dqf_ref, dqr_ref, dvf_ref, dvr_ref, dzf_ref, dzr_ref, dhg_ref, hq_ref, dcq_ref, dckv_ref,
             dkr_ref, dx2_ref, x_ref, g_ref, w_ref, dx_ref, dp_ref, dg_ref):
        gain = g_ref[...]

        def group(dqf, dqr, dvf, dvr, dzf, dzr, dhg, hqv, dcq, dckv, dkr, dx2, xv):
            sg = _sigmoid(hqv)
            dhq = (dqf + dqr) * (sg * (1.0 + hqv * (1.0 - sg)))
            pieces = [p.astype(BF16) for p in (dhq, dvf + dvr, dzf, dzr, dhg, dcq, dckv, dkr)]
            dh = _dot(jnp.concatenate(pieces, axis=1), w_ref[...])
            r = lax.rsqrt(jnp.mean(xv * xv, axis=-1, keepdims=True) + EPS)
            xh = xv * r
            dxh = dh * gain
            return dx2 + r * (dxh - xh * jnp.mean(dxh * xh, axis=-1, keepdims=True)), pieces, dh * xh

        ins = (dqf_ref, dqr_ref, dvf_ref, dvr_ref, dzf_ref, dzr_ref, dhg_ref, hq_ref, dcq_ref, dckv_ref, dkr_ref,
               dx2_ref, x_ref)
        groups = _row_groups(tm)
        done = [group(*vals) for vals in [[ref[g, :] for ref in ins] for g in groups]]
        for g, (dx, pieces, _) in zip(groups, done):
            dx_ref[g, :] = dx
            off = 0
            for p16, w in zip(pieces, IN_WIDTHS):
                dp_ref[g, off:off + w] = p16
                off += w
        _acc_rows(dg_ref, jnp.concatenate([d[2] for d in done], axis=0), pl.program_id(0) == 0)

    a512 = _rows(tm, A_WIDTH)
    return pl.pallas_call(
        body, name="in_bwd", grid=(t // tm,),
        in_specs=[a512] * 8 + [_rows(tm, Q_LORA), _rows(tm, KV_LORA), _rows(tm, 128), _rows(tm, D_MODEL),
                               _rows(tm, D_MODEL), _full((1, D_MODEL)), _full((D_IN_PAD, D_MODEL))],
        out_specs=[_rows(tm, D_MODEL), _rows(tm, D_IN_PAD), _full((1, D_MODEL))],
        out_shape=[jax.ShapeDtypeStruct((t, D_MODEL), F32), jax.ShapeDtypeStruct((t, D_IN_PAD), BF16),
                   jax.ShapeDtypeStruct((1, D_MODEL), F32)],
        compiler_params=_params(1),
    )(dq_f, dq_r, dv_f, dv_r, dz_f, dz_r, dhg, hq, dcq, dckv, dkr, dx2, x, g1, w_in_p)


def _lower_bound(lbl_ref, direction):
    l0 = lbl_ref[direction, 0:1, :]
    l1 = lbl_ref[direction, 1:2, :]
    m = jnp.maximum(l0, l1)
    e0 = jnp.exp(l0 - m)
    e1 = jnp.exp(l1 - m)
    return e0 / (e0 + e1)


def _hgrn_consts(rb, reverse):
    row = _iota2((rb, rb), 0)
    col = _iota2((rb, rb), 1)
    same = (row // CHUNK) == (col // CHUNK)
    tri = jnp.logical_and(same, (col >= row) if reverse else (col <= row))
    tri_t = jnp.logical_and(same, (col <= row) if reverse else (col >= row))
    r128 = _iota2((128, 128), 0)
    c128 = _iota2((128, 128), 1)
    bd = (r128 < 64) == (c128 < 64)
    lane = _iota2((1, 128), 1)
    m0 = (lane < 64).astype(F32)
    return tri, tri_t, bd, (m0, 1.0 - m0)


def _per_chunk(x, fn):
    n = x.shape[0] // CHUNK
    return jnp.concatenate([jnp.broadcast_to(fn(x[c * CHUNK:(c + 1) * CHUNK]), (CHUNK, x.shape[1])) for c in range(n)],
                           axis=0)


def _chunk_cumsum(x, reverse):
    rb = x.shape[0]
    pos = _iota2(x.shape, 0) % CHUNK
    step = 1
    while step < CHUNK:
        if reverse:
            x = x + jnp.where(pos < CHUNK - step, pltpu.roll(x, rb - step, 0), 0.0)
        else:
            x = x + jnp.where(pos >= step, pltpu.roll(x, step, 0), 0.0)
        step *= 2
    return x


def _hgrn_block(z, hqv, lb, reverse):
    sig = _sigmoid(z)
    sn = _sigmoid(-z)
    q = hqv * _sigmoid(hqv)
    f = lb + (1.0 - lb) * sig
    k = (1.0 - lb) * sn
    lf = jnp.log(f)
    cum = _chunk_cumsum(lf, reverse)
    last = _per_chunk(cum, (lambda a: a[0:1]) if reverse else (lambda a: a[CHUNK - 1:CHUNK]))
    e_neg = jnp.exp(-cum)
    e_end = jnp.exp(last - cum)
    a = jnp.exp(cum)
    return dict(sig=sig, sn=sn, q=q, f=f, k=k, a=a, e_neg=e_neg, e_end=e_end,
                q_dec=q * a, k_inv=k * e_neg, k_end=k * e_end, d=jnp.exp(last))


def _hgrn_dims(t, n_batch):
    s = t // n_batch
    rb = _pick(s, 256, CHUNK)
    return s, rb, s // rb, rb // CHUNK


def _hgrn_fwd(hq, hi, hf, lbl, *, n_batch, direction):
    t = hq.shape[0]
    reverse = direction == 1
    s, rb, nb, nc = _hgrn_dims(t, n_batch)

    def tmap(b, j):
        return (b * nb + ((nb - 1 - j) if reverse else j), 0)

    def smap(b, j):
        return (b * nb + ((nb - 1 - j) if reverse else j), 0, 0, 0)

    def body(hq_ref, hi_ref, hf_ref, lbl_ref, o_ref, st_ref, st_scr):
        @pl.when(pl.program_id(1) == 0)
        def _():
            st_scr[...] = jnp.zeros_like(st_scr)

        lb_all = _lower_bound(lbl_ref, direction)
        tri, _, bd, masks = _hgrn_consts(rb, reverse)
        order = range(nc - 1, -1, -1) if reverse else range(nc)

        pairs = [slice(p * 128, (p + 1) * 128) for p in range(HEAD_PAIRS)]
        chunks = [slice(c * CHUNK, (c + 1) * CHUNK) for c in range(nc)]
        w = _hgrn_block(hf_ref[...], hq_ref[...], lb_all, reverse)
        v16 = hi_ref[...].astype(BF16)
        qd16 = w["q_dec"].astype(BF16)
        ki16 = w["k_inv"].astype(BF16)
        ke16 = w["k_end"].astype(BF16)
        sc = [[_dot_nt(jnp.where(mh > 0.0, qd16[:, ls], 0.0).astype(BF16), ki16[:, ls]) for mh in masks] for ls in pairs]
        pv = [[_dot(jnp.where(tri, s_e, 0.0).astype(BF16), v16[:, ls]) for s_e in sc_p] for sc_p, ls in zip(sc, pairs)]
        o_intra = [jnp.where(masks[0] > 0.0, pv_p[0], pv_p[1]) for pv_p in pv]
        ut = [[jnp.where(bd, _dot_tn(v16[rs, ls], ke16[rs, ls]), 0.0) for ls in pairs] for rs in chunks]
        st = [st_scr[p] for p in range(HEAD_PAIRS)]
        for c in order:
            rs = chunks[c]
            inter = [_dot_nt(qd16[rs, ls], st[p].astype(BF16)) for p, ls in enumerate(pairs)]
            for p, ls in enumerate(pairs):
                o_ref[rs, ls] = o_intra[p][rs] + inter[p]
                st_ref[c, p] = st[p]
                st[p] = st[p] * w["d"][c * CHUNK:c * CHUNK + 1, ls] + ut[c][p]
        for p in range(HEAD_PAIRS):
            st_scr[p] = st[p]

    blk = pl.BlockSpec((rb, A_WIDTH), tmap)
    return pl.pallas_call(
        body, name=f"hgrn_fwd_{direction}", grid=(n_batch, nb),
        in_specs=[blk, blk, blk, _full((2, 2, A_WIDTH))],
        out_specs=[blk, pl.BlockSpec((nc, HEAD_PAIRS, 128, 128), smap)],
        out_shape=[jax.ShapeDtypeStruct((t, A_WIDTH), F32),
                   jax.ShapeDtypeStruct((t // CHUNK, HEAD_PAIRS, 128, 128), F32)],
        scratch_shapes=[pltpu.VMEM((HEAD_PAIRS, 128, 128), F32)],
        compiler_params=_params(2),
    )(hq, hi, hf, lbl)


def _hgrn_bwd(hq, hi, hf, do, st, lbl, *, n_batch, direction):
    t = hq.shape[0]
    reverse = direction == 1
    s, rb, nb, nc = _hgrn_dims(t, n_batch)

    def tmap(b, j):
        return (b * nb + (j if reverse else (nb - 1 - j)), 0)

    def smap(b, j):
        return (b * nb + (j if reverse else (nb - 1 - j)), 0, 0, 0)

    def body(hq_ref, hi_ref, hf_ref, do_ref, st_ref, lbl_ref, dq_ref, dv_ref, dz_ref, dl_ref, g_scr, dlb_scr):
        b = pl.program_id(0)
        j = pl.program_id(1)

        @pl.when(jnp.logical_and(b == 0, j == 0))
        def _():
            dlb_scr[...] = jnp.zeros_like(dlb_scr)

        @pl.when(j == 0)
        def _():
            g_scr[...] = jnp.zeros_like(g_scr)

        lb_all = _lower_bound(lbl_ref, direction)
        tri, tri_t, bd, masks = _hgrn_consts(rb, reverse)
        order = range(nc) if reverse else range(nc - 1, -1, -1)

        chunks = [slice(c * CHUNK, (c + 1) * CHUNK) for c in range(nc)]

        def lanes(per_pair):
            return jnp.concatenate(per_pair, axis=1)

        for first_pair in range(0, HEAD_PAIRS, PAIRS_AT_ONCE):
            here = slice(first_pair * 128, (first_pair + PAIRS_AT_ONCE) * 128)
            pairs = [slice(p * 128, (p + 1) * 128) for p in range(PAIRS_AT_ONCE)]
            lb = lb_all[:, here]
            w = _hgrn_block(hf_ref[:, here], hq_ref[:, here], lb, reverse)
            dov = do_ref[:, here]
            v16 = hi_ref[:, here].astype(BF16)
            do16 = dov.astype(BF16)
            qd16 = w["q_dec"].astype(BF16)
            ki16 = w["k_inv"].astype(BF16)
            ke16 = w["k_end"].astype(BF16)
            qm16 = [[jnp.where(mh > 0.0, qd16[:, ls], 0.0).astype(BF16) for mh in masks] for ls in pairs]
            dom16 = [[jnp.where(mh > 0.0, do16[:, ls], 0.0).astype(BF16) for mh in masks] for ls in pairs]
            heads = [(p, e) for p in range(PAIRS_AT_ONCE) for e in range(2)]
            dp = {(p, e): _dot_nt(dom16[p][e], v16[:, pairs[p]]) for p, e in heads}
            pm_t = {(p, e): _dot_nt(ki16[:, pairs[p]], qm16[p][e]) for p, e in heads}
            dp_t = {(p, e): _dot_nt(v16[:, pairs[p]], dom16[p][e]) for p, e in heads}
            dp = {h: jnp.where(tri, a, 0.0).astype(BF16) for h, a in dp.items()}
            pm_t = {h: jnp.where(tri_t, a, 0.0).astype(BF16) for h, a in pm_t.items()}
            dp_t = {h: jnp.where(tri_t, a, 0.0).astype(BF16) for h, a in dp_t.items()}
            dv_e = {(p, e): _dot(pm_t[p, e], do16[:, pairs[p]]) for p, e in heads}
            dq_e = {(p, e): _dot(dp[p, e], ki16[:, pairs[p]]) for p, e in heads}
            dk_e = {(p, e): _dot(dp_t[p, e], qd16[:, pairs[p]]) for p, e in heads}
            st = [[st_ref[c, first_pair + p] for p in range(PAIRS_AT_ONCE)] for c in range(nc)]
            dq_x = [[_dot(do16[rs, ls], st[c][p].astype(BF16)) for p, ls in enumerate(pairs)] for c, rs in enumerate(chunks)]
            gq = [[jnp.where(bd, _dot_tn(do16[rs, ls], qd16[rs, ls]), 0.0) for ls in pairs] for rs in chunks]
            g = [g_scr[first_pair + p] for p in range(PAIRS_AT_ONCE)]
            dk_end, dv_x, dd = [None] * nc, [None] * nc, [None] * nc
            for c in order:
                rs = chunks[c]
                g16 = [a.astype(BF16) for a in g]
                dk_end[c] = lanes([_dot(v16[rs, ls], g16[p]) for p, ls in enumerate(pairs)])
                dv_x[c] = lanes([_dot_nt(ke16[rs, ls], g16[p]) for p, ls in enumerate(pairs)])
                dd[c] = jnp.broadcast_to(lanes([jnp.sum(g[p] * st[c][p], axis=0, keepdims=True) for p in range(PAIRS_AT_ONCE)]),
                                         (CHUNK, PAIRS_AT_ONCE * 128))
                for p, ls in enumerate(pairs):
                    g[p] = g[p] * w["d"][c * CHUNK:c * CHUNK + 1, ls] + gq[c][p]
            for p in range(PAIRS_AT_ONCE):
                g_scr[first_pair + p] = g[p]

            def both_heads(d):
                return lanes([jnp.where(masks[0] > 0.0, d[p, 0], d[p, 1]) for p in range(PAIRS_AT_ONCE)])

            dq_dec = both_heads(dq_e) + jnp.concatenate([lanes(a) for a in dq_x], axis=0)
            dk_inv = both_heads(dk_e)
            dk_end = jnp.concatenate(dk_end, axis=0)
            dv = both_heads(dv_e) + jnp.concatenate(dv_x, axis=0)
            dd = jnp.concatenate(dd, axis=0)
            dke = dk_end * w["k_end"]
            dcum = dq_dec * w["q_dec"] - dk_inv * w["k_inv"] - dke
            dk = dk_inv * w["e_neg"] + dk_end * w["e_end"]
            dlast = _per_chunk(dke, lambda a: jnp.sum(a, axis=0, keepdims=True)) + dd * w["d"]
            dlf = _chunk_cumsum(dcum, not reverse) + dlast
            tt = dlf / w["f"] - dk
            dq_ref[:, here] = dq_dec * w["a"]
            dv_ref[:, here] = dv
            dz_ref[:, here] = ((1.0 - lb) * w["sig"] * w["sn"] * tt).astype(BF16)
            dlb_scr[:, here] += jnp.sum(w["sn"] * tt, axis=0, keepdims=True)

        @pl.when(jnp.logical_and(b == pl.num_programs(0) - 1, j == pl.num_programs(1) - 1))
        def _():
            d0 = dlb_scr[...] * lb_all * (1.0 - lb_all)
            dl_ref[0:1, :] = d0
            dl_ref[1:2, :] = -d0

    blk = pl.BlockSpec((rb, A_WIDTH), tmap)
    return pl.pallas_call(
        body, name=f"hgrn_bwd_{direction}", grid=(n_batch, nb),
        in_specs=[blk, blk, blk, blk, pl.BlockSpec((nc, HEAD_PAIRS, 128, 128), smap), _full((2, 2, A_WIDTH))],
        out_specs=[blk, blk, blk, _full((2, A_WIDTH))],
        out_shape=[jax.ShapeDtypeStruct((t, A_WIDTH), F32)] * 2
        + [jax.ShapeDtypeStruct((t, A_WIDTH), BF16), jax.ShapeDtypeStruct((2, A_WIDTH), F32)],
        scratch_shapes=[pltpu.VMEM((HEAD_PAIRS, 128, 128), F32), pltpu.VMEM((1, A_WIDTH), F32)],
        compiler_params=_params(2),
    )(hq, hi, hf, do, st, lbl)


def _swap_rope_halves(x):
    lane = _iota2(x.shape, 1)
    return jnp.where(lane < 32, pltpu.roll(x, 96, 1), pltpu.roll(x, 32, 1))


def _rms_fwd(xv, g):
    r = lax.rsqrt(jnp.mean(xv * xv, axis=-1, keepdims=True) + EPS)
    return (xv * r) * g


def _rms_bwd(dy, xv, g):
    r = lax.rsqrt(jnp.mean(xv * xv, axis=-1, keepdims=True) + EPS)
    xh = xv * r
    dxh = dy * g
    return r * (dxh - xh * jnp.mean(dxh * xh, axis=-1, keepdims=True)), dy * xh


def _mla_prep(cq, ckv, kr, g_qa, g_kva, w_q_p, w_kv_p, cosx, sinx, *, n_batch):
    t = cq.shape[0]
    s = t // n_batch
    tm = _row_tile(s, 1024)
    nt = s // tm

    def body(cq_ref, ckv_ref, kr_ref, gq_ref, gkv_ref, wq_ref, wkv_ref, cos_ref, sin_ref,
             q_ref, k_ref, v_ref, cqn_ref, ckvn_ref):
        cos, sin = cos_ref[...], sin_ref[...]
        cqn = _rms_fwd(cq_ref[...], gq_ref[...]).astype(BF16)
        ckvn = _rms_fwd(ckv_ref[...], gkv_ref[...]).astype(BF16)
        cqn_ref[...] = cqn
        ckvn_ref[...] = ckvn
        krv = kr_ref[...]
        kr_roped = (krv * cos + _swap_rope_halves(krv) * sin).astype(BF16)
        for h in range(B_HEADS):
            o = h * QK_PAD
            q_ref[:, o:o + 128] = _dot_nt(cqn, wq_ref[o:o + 128, :]).astype(BF16)
            qr = _dot_nt(cqn, wq_ref[o + 128:o + 256, :])
            q_ref[:, o + 128:o + 256] = (qr * cos + _swap_rope_halves(qr) * sin).astype(BF16)
            k_ref[:, o:o + 128] = _dot(ckvn, wkv_ref[:, h * 128:(h + 1) * 128]).astype(BF16)
            k_ref[:, o + 128:o + 256] = kr_roped
        v_ref[...] = _dot(ckvn, wkv_ref[:, 512:1024]).astype(BF16)

    tab = pl.BlockSpec((tm, 128), lambda i: (i % nt, 0))
    return pl.pallas_call(
        body, name="mla_prep", grid=(t // tm,),
        in_specs=[_rows(tm, Q_LORA), _rows(tm, KV_LORA), _rows(tm, 128), _full((1, Q_LORA)), _full((1, KV_LORA)),
                  _full((1024, Q_LORA)), _full((KV_LORA, 1024)), tab, tab],
        out_specs=[_rows(tm, 1024), _rows(tm, 1024), _rows(tm, 512), _rows(tm, Q_LORA), _rows(tm, KV_LORA)],
        out_shape=[jax.ShapeDtypeStruct((t, 1024), BF16), jax.ShapeDtypeStruct((t, 1024), BF16),
                   jax.ShapeDtypeStruct((t, 512), BF16), jax.ShapeDtypeStruct((t, Q_LORA), BF16),
                   jax.ShapeDtypeStruct((t, KV_LORA), BF16)],
        compiler_params=_params(1),
    )(cq, ckv, kr, g_qa, g_kva, w_q_p, w_kv_p, cosx, sinx)


def _mla_prep_bwd(dqt, dk, dv, cq, ckv, g_qa, g_kva, w_q_p, w_kv_p, cosx, sinx, *, n_batch):
    t = cq.shape[0]
    s = t // n_batch
    tm = _row_tile(s, 1024)
    nt = s // tm

    def body(dqt_ref, dk_ref, dv_ref, cq_ref, ckv_ref, gq_ref, gkv_ref, wq_ref, wkv_ref, cos_ref, sin_ref,
             dcq_ref, dckv_ref, dkr_ref, dqp_ref, dkvp_ref, dgq_ref, dgkv_ref):
        cos, sin = cos_ref[...], sin_ref[...]
        first = pl.program_id(0) == 0

        def unrope(d):
            return d * cos + _swap_rope_halves(d * sin)

        dcqn = None
        dkr = None
        dckvn = None
        for h in range(B_HEADS):
            o = h * QK_PAD
            dq_h = jnp.transpose(dqt_ref[o:o + QK_PAD, :])
            dqn16 = dq_h[:, 0:128].astype(BF16)
            dqr16 = unrope(dq_h[:, 128:256]).astype(BF16)
            dqp_ref[:, o:o + 128] = dqn16
            dqp_ref[:, o + 128:o + 256] = dqr16
            part = _dot(dqn16, wq_ref[o:o + 128, :]) + _dot(dqr16, wq_ref[o + 128:o + 256, :])
            dcqn = part if dcqn is None else dcqn + part
            dkn16 = dk_ref[:, o:o + 128].astype(BF16)
            dkvp_ref[:, h * 128:(h + 1) * 128] = dkn16
            part = _dot_nt(dkn16, wkv_ref[:, h * 128:(h + 1) * 128])
            dckvn = part if dckvn is None else dckvn + part
            kr_part = dk_ref[:, o + 128:o + 256]
            dkr = kr_part if dkr is None else dkr + kr_part
        dv16 = dv_ref[...].astype(BF16)
        dkvp_ref[:, 512:1024] = dv16
        dckvn = dckvn + _dot_nt(dv16, wkv_ref[:, 512:1024])
        dkr_ref[...] = unrope(dkr).astype(BF16)
        dcq, dgq = _rms_bwd(dcqn, cq_ref[...], gq_ref[...])
        dckv, dgkv = _rms_bwd(dckvn, ckv_ref[...], gkv_ref[...])
        dcq_ref[...] = dcq.astype(BF16)
        dckv_ref[...] = dckv.astype(BF16)
        _acc_rows(dgq_ref, dgq, first)
        _acc_rows(dgkv_ref, dgkv, first)

    tab = pl.BlockSpec((tm, 128), lambda i: (i % nt, 0))
    return pl.pallas_call(
        body, name="mla_prep_bwd", grid=(t // tm,),
        in_specs=[pl.BlockSpec((1024, tm), lambda i: (0, i)), _rows(tm, 1024), _rows(tm, 512), _rows(tm, Q_LORA),
                  _rows(tm, KV_LORA),
                  _full((1, Q_LORA)), _full((1, KV_LORA)), _full((1024, Q_LORA)), _full((KV_LORA, 1024)), tab, tab],
        out_specs=[_rows(tm, Q_LORA), _rows(tm, KV_LORA), _rows(tm, 128), _rows(tm, 1024), _rows(tm, 1024),
                   _full((1, Q_LORA)), _full((1, KV_LORA))],
        out_shape=[jax.ShapeDtypeStruct((t, Q_LORA), BF16), jax.ShapeDtypeStruct((t, KV_LORA), BF16),
                   jax.ShapeDtypeStruct((t, 128), BF16), jax.ShapeDtypeStruct((t, 1024), BF16),
                   jax.ShapeDtypeStruct((t, 1024), BF16), jax.ShapeDtypeStruct((1, Q_LORA), F32),
                   jax.ShapeDtypeStruct((1, KV_LORA), F32)],
        compiler_params=_params(1),
    )(dqt, dk, dv, cq, ckv, g_qa, g_kva, w_q_p, w_kv_p, cosx, sinx)


def _attn_dims(t, n_batch):
    s = t // n_batch
    tq = _pick(s, 2048, 128)
    return s, tq, s // tq


def _grid_ends(n_axes):
    ids = [pl.program_id(a) for a in range(n_axes)]
    first = functools.reduce(jnp.logical_and, [i == 0 for i in ids])
    last = functools.reduce(jnp.logical_and, [i == pl.num_programs(a) - 1 for a, i in enumerate(ids)])
    return first, last


def _attn_fwd(q, k, v, wsrcs, *, n_batch):
    t = q.shape[0]
    s, tq, nq = _attn_dims(t, n_batch)
    nw = len(wsrcs)

    def body(q_ref, k_ref, v_ref, *refs):
        w_refs, (o_ref, lse_ref), wall_refs = refs[:nw], refs[nw:nw + 2], refs[nw + 2:2 * nw + 2]
        send_sems, recv_sems, local_sems = refs[2 * nw + 2:]
        first, last = _grid_ends(3)

        @pl.when(first)
        def _():
            for i in range(nw):
                _gather_start(w_refs[i], wall_refs[i], send_sems, recv_sems, local_sems.at[i], 3 * i)

        @pl.when(last)
        def _():
            for i in range(nw):
                _gather_wait(w_refs[i], wall_refs[i], send_sems, recv_sems, local_sems.at[i], 3 * i)

        kv, vv = k_ref[...], v_ref[...]
        groups = _row_groups(tq)
        raw = [_dot_nt(q_ref[g, :], kv) for g in groups]
        m = [jnp.max(a, axis=-1, keepdims=True) for a in raw]
        p = [jnp.exp2((a - b) * SCALE_LOG2E) for a, b in zip(raw, m)]
        l = [jnp.sum(a, axis=-1, keepdims=True) for a in p]
        for g, pg, mg, lg in zip(groups, p, m, l):
            o_ref[g, :] = _dot(pg.astype(BF16), vv) / lg
            lse2 = mg * SCALE_LOG2E + jnp.log(lg) * LOG2E
            lse_ref[:, g] = jnp.transpose(jnp.broadcast_to(lse2, (lse2.shape[0], 128)))[0:1, :]

    return pl.pallas_call(
        body, name="attn_fwd", grid=(n_batch, B_HEADS, nq),
        in_specs=[pl.BlockSpec((tq, QK_PAD), lambda b, h, i: (b * nq + i, h)),
                  pl.BlockSpec((s, QK_PAD), lambda b, h, i: (b, h)),
                  pl.BlockSpec((s, B_V), lambda b, h, i: (b, h))] + [_HBM] * nw,
        out_specs=[pl.BlockSpec((tq, B_V), lambda b, h, i: (b * nq + i, h)),
                   pl.BlockSpec((None, 1, tq), lambda b, h, i: (h, 0, b * nq + i))] + [_HBM] * nw,
        out_shape=[jax.ShapeDtypeStruct((t, B_HEADS * B_V), F32), jax.ShapeDtypeStruct((B_HEADS, 1, t), F32)]
        + [jax.ShapeDtypeStruct((N_CHIPS,) + w.shape, w.dtype) for w in wsrcs],
        scratch_shapes=[pltpu.SemaphoreType.DMA((3 * nw,)), pltpu.SemaphoreType.DMA((3 * nw,)),
                        pltpu.SemaphoreType.DMA((nw,))],
        compiler_params=_params(3),
    )(q, k, v, *wsrcs)


def _attn_bwd(q, k, v, do16, lse, delta, gsegs, *, n_batch):
    t = q.shape[0]
    s = t // n_batch
    tk = _pick(s, 512, 128)
    nk = s // tk
    ng = len(gsegs)

    def body(q_ref, k_ref, v_ref, do_ref, lse_ref, dl_ref, *refs):
        g_refs, (dqt_ref, dk_ref, dv_ref), recv_refs = refs[:ng], refs[ng:ng + 3], refs[ng + 3:2 * ng + 3]
        send_sems, recv_sems = refs[2 * ng + 3:]
        first, last = _grid_ends(3)

        @pl.when(first)
        def _():
            for i in range(ng):
                _scatter_start(g_refs[i], recv_refs[i], send_sems, recv_sems, 3 * i)

        @pl.when(last)
        def _():
            for i in range(ng):
                _scatter_wait(g_refs[i], recv_refs[i], send_sems, recv_sems, 3 * i)

        j = pl.program_id(2)
        qv, kv, dov = q_ref[...], k_ref[...], do_ref[...]
        pt = jnp.exp2(_dot_nt(kv, qv) * SCALE_LOG2E - lse_ref[...])
        dv_ref[...] = _dot(pt.astype(BF16), dov)
        dpt = _dot_nt(v_ref[...], dov)
        dst = (pt * (dpt - dl_ref[...])).astype(BF16)
        dk_ref[...] = _dot(dst, qv) * ATTN_SCALE
        part = _dot_tn(kv, dst)

        @pl.when(j == 0)
        def _():
            dqt_ref[...] = part

        @pl.when(j != 0)
        def _():
            dqt_ref[...] += part

        @pl.when(j == nk - 1)
        def _():
            dqt_ref[...] = dqt_ref[...] * ATTN_SCALE

    row = pl.BlockSpec((None, 1, s), lambda b, h, j: (h, 0, b))
    return pl.pallas_call(
        body, name="attn_bwd", grid=(n_batch, B_HEADS, nk),
        in_specs=[pl.BlockSpec((s, QK_PAD), lambda b, h, j: (b, h)),
                  pl.BlockSpec((tk, QK_PAD), lambda b, h, j: (b * nk + j, h)),
                  pl.BlockSpec((tk, B_V), lambda b, h, j: (b * nk + j, h)),
                  pl.BlockSpec((s, B_V), lambda b, h, j: (b, h)), row, row] + [_HBM] * ng,
        out_specs=[pl.BlockSpec((QK_PAD, s), lambda b, h, j: (h, b)),
                   pl.BlockSpec((tk, QK_PAD), lambda b, h, j: (b * nk + j, h)),
                   pl.BlockSpec((tk, B_V), lambda b, h, j: (b * nk + j, h))] + [_HBM] * ng,
        out_shape=[jax.ShapeDtypeStruct((B_HEADS * QK_PAD, t), F32), jax.ShapeDtypeStruct((t, B_HEADS * QK_PAD), F32),
                   jax.ShapeDtypeStruct((t, B_HEADS * B_V), F32)]
        + [jax.ShapeDtypeStruct((3,) + g.shape[1:], g.dtype) for g in gsegs],
        scratch_shapes=[pltpu.SemaphoreType.DMA((3 * ng,)), pltpu.SemaphoreType.DMA((3 * ng,))],
        compiler_params=_params(3),
    )(q, k, v, do16, lse, delta, *gsegs)


def _group_ones16():
    r = _iota2((A_WIDTH, A_WIDTH), 0) // 64
    c = _iota2((A_WIDTH, A_WIDTH), 1) // 64
    return (r == c).astype(BF16)


def _head_rms(o, ones16):
    return lax.rsqrt(_exact_right(o * o, ones16) * (1.0 / 64.0) + EPS)


def _out_fwd(o_f, o_r, hg, o_attn, x, g_hn, g_mla, w_out, g2):
    t = x.shape[0]
    tm = _row_tile(t, 4 * ROW_GROUP)

    def body(of_ref, or_ref, hg_ref, oa_ref, x_ref, ghn_ref, gm_ref, w_ref, g2_ref, y_ref, x2_ref, h2_ref):
        ones16 = _group_ones16()
        ghn, gm, g2v = ghn_ref[...], gm_ref[...], g2_ref[...]

        def group(ofv, orv, hgv, oav, xv):
            o = ofv + orv
            ya16 = (((o * _head_rms(o, ones16)) * ghn) * (hgv * _sigmoid(hgv))).astype(BF16)
            yb16 = _rms_fwd(oav, gm).astype(BF16)
            x2 = xv + _dot(ya16, w_ref[0:A_WIDTH, :]) + _dot(yb16, w_ref[A_WIDTH:D_MODEL, :])
            return ya16, yb16, x2, _rms_fwd(x2, g2v).astype(BF16)

        ins = (of_ref, or_ref, hg_ref, oa_ref, x_ref)
        groups = _row_groups(tm)
        done = [group(*vals) for vals in [[ref[g, :] for ref in ins] for g in groups]]
        for g, (ya16, yb16, x2, h2) in zip(groups, done):
            y_ref[g, 0:A_WIDTH] = ya16
            y_ref[g, A_WIDTH:D_MODEL] = yb16
            x2_ref[g, :] = x2
            h2_ref[g, :] = h2

    a512 = _rows(tm, A_WIDTH)
    return pl.pallas_call(
        body, name="out_fwd", grid=(t // tm,),
        in_specs=[a512, a512, a512, a512, _rows(tm, D_MODEL), _full((1, A_WIDTH)), _full((1, A_WIDTH)),
                  _full((D_MODEL, D_MODEL)), _full((1, D_MODEL))],
        out_specs=[_rows(tm, D_MODEL)] * 3,
        out_shape=[jax.ShapeDtypeStruct((t, D_MODEL), BF16), jax.ShapeDtypeStruct((t, D_MODEL), F32),
                   jax.ShapeDtypeStruct((t, D_MODEL), BF16)],
        compiler_params=_params(1),
    )(o_f, o_r, hg, o_attn, x, g_hn, g_mla, w_out, g2)


def _out_bwd(dx2_16, o_f, o_r, hg, o_attn, g_hn, g_mla, w_out):
    t = dx2_16.shape[0]
    tm = _row_tile(t, 4 * ROW_GROUP)

    def body(dx_ref, of_ref, or_ref, hg_ref, oa_ref, ghn_ref, gm_ref, w_ref,
             do_ref, dhg_ref, doa_ref, dl_ref, dghn_ref, dgm_ref):
        first = pl.program_id(0) == 0
        ones16 = _group_ones16()
        sel16 = (_iota2((8, A_WIDTH), 0) == _iota2((8, A_WIDTH), 1) // B_V).astype(BF16)
        ghn, gm = ghn_ref[...], gm_ref[...]

        def group(dxv, ofv, orv, hgv, oav):
            dya = _dot_nt(dxv, w_ref[0:A_WIDTH, :])
            dyb = _dot_nt(dxv, w_ref[A_WIDTH:D_MODEL, :])
            o = ofv + orv
            rh = _head_rms(o, ones16)
            oh = o * rh
            sg = _sigmoid(hgv)
            sl = hgv * sg
            dhg = ((dya * (oh * ghn)) * (sg * (1.0 + hgv * (1.0 - sg)))).astype(BF16)
            doh = dya * sl * ghn
            do = rh * (doh - oh * (_exact_right(doh * oh, ones16) * (1.0 / 64.0)))
            doa, dgm = _rms_bwd(dyb, oav, gm)
            x1, x2, x3 = _split3(doa * oav)
            delta = _dot_nt(sel16, x1) + _dot_nt(sel16, x2) + _dot_nt(sel16, x3)
            return do, dhg, doa.astype(BF16), delta, dya * sl * oh, dgm

        groups = _row_groups(tm)
        loaded = [(dx_ref[g, :], of_ref[g, :], or_ref[g, :], hg_ref[g, :], oa_ref[g, :]) for g in groups]
        done = [group(*vals) for vals in loaded]
        for g, (do, dhg, doa16, delta, _, _) in zip(groups, done):
            do_ref[g, :] = do
            dhg_ref[g, :] = dhg
            doa_ref[g, :] = doa16
            for h in range(B_HEADS):
                dl_ref[h, :, g] = delta[h:h + 1, :]
        _acc_rows(dghn_ref, jnp.concatenate([d[4] for d in done], axis=0), first)
        _acc_rows(dgm_ref, jnp.concatenate([d[5] for d in done], axis=0), first)

    a512 = _rows(tm, A_WIDTH)
    return pl.pallas_call(
        body, name="out_bwd", grid=(t // tm,),
        in_specs=[_rows(tm, D_MODEL), a512, a512, a512, a512, _full((1, A_WIDTH)), _full((1, A_WIDTH)),
                  _full((D_MODEL, D_MODEL))],
        out_specs=[a512, a512, a512, pl.BlockSpec((B_HEADS, 1, tm), lambda i: (0, 0, i)),
                   _full((1, A_WIDTH)), _full((1, A_WIDTH))],
        out_shape=[jax.ShapeDtypeStruct((t, A_WIDTH), F32)] + [jax.ShapeDtypeStruct((t, A_WIDTH), BF16)] * 2
        + [jax.ShapeDtypeStruct((B_HEADS, 1, t), F32)]
        + [jax.ShapeDtypeStruct((1, A_WIDTH), F32)] * 2,
        compiler_params=_params(1),
    )(dx2_16, o_f, o_r, hg, o_attn, g_hn, g_mla, w_out)


def _ffn_fwd_bwd(h2, x2, target, w_gate, w_up, w_down, g_f, g2):
    t = x2.shape[0]
    tm = _row_tile(t)
    inv_d = 1.0 / D_MODEL

    def body(h2_ref, x2_ref, tg_ref, wg_ref, wu_ref, wd_ref, gf_ref, g2_ref,
             act_ref, dgate_ref, dup_ref, dx3_ref, dx2_ref, dx2h_ref, loss_ref, dgf_ref, dg2_ref):
        first = pl.program_id(0) == 0
        h2v = h2_ref[...]
        gate = _dot_nt(h2v, wg_ref[...])
        up = _dot_nt(h2v, wu_ref[...])
        sg = _sigmoid(gate)
        sl = gate * sg
        act16 = (sl * up).astype(BF16)
        act_ref[...] = act16
        x2v = x2_ref[...]
        x3 = x2v + _dot(act16, wd_ref[...])
        r3 = lax.rsqrt(jnp.mean(x3 * x3, axis=-1, keepdims=True) + EPS)
        x3h = x3 * r3
        gf = gf_ref[...]
        err = x3h * gf - tg_ref[...]
        part = 0.5 * jnp.sum(jnp.mean(err * err, axis=-1, keepdims=True), axis=0, keepdims=True)

        @pl.when(first)
        def _():
            loss_ref[...] = jnp.zeros_like(loss_ref)

        loss_ref[...] += part
        dy = err * inv_d
        _acc_rows(dgf_ref, dy * x3h, first)
        dx3h = dy * gf
        dx3 = r3 * (dx3h - x3h * jnp.mean(dx3h * x3h, axis=-1, keepdims=True))
        dx3_16 = dx3.astype(BF16)
        dx3_ref[...] = dx3_16
        da = _dot_nt(dx3_16, wd_ref[...])
        dup16 = (da * sl).astype(BF16)
        dgate16 = (da * up * (sg * (1.0 + gate * (1.0 - sg)))).astype(BF16)
        dup_ref[...] = dup16
        dgate_ref[...] = dgate16
        dh2 = _dot(dgate16, wg_ref[...]) + _dot(dup16, wu_ref[...])
        dx2n, dg2 = _rms_bwd(dh2, x2v, g2_ref[...])
        _acc_rows(dg2_ref, dg2, first)
        dx2 = dx3 + dx2n
        dx2_ref[...] = dx2
        dx2h_ref[...] = dx2.astype(BF16)

    return pl.pallas_call(
        body, name="ffn_fwd_bwd", grid=(t // tm,),
        in_specs=[_rows(tm, D_MODEL), _rows(tm, D_MODEL), _rows(tm, D_MODEL), _full((D_FF, D_MODEL), True),
                  _full((D_FF, D_MODEL), True), _full((D_FF, D_MODEL), True), _full((1, D_MODEL)), _full((1, D_MODEL))],
        out_specs=[_rows(tm, D_FF), _rows(tm, D_FF), _rows(tm, D_FF), _rows(tm, D_MODEL), _rows(tm, D_MODEL),
                   _rows(tm, D_MODEL), _full((8, 128)), _full((1, D_MODEL)), _full((1, D_MODEL))],
        out_shape=[jax.ShapeDtypeStruct((t, D_FF), BF16)] * 3
        + [jax.ShapeDtypeStruct((t, D_MODEL), BF16), jax.ShapeDtypeStruct((t, D_MODEL), F32),
           jax.ShapeDtypeStruct((t, D_MODEL), BF16), jax.ShapeDtypeStruct((8, 128), F32),
           jax.ShapeDtypeStruct((1, D_MODEL), F32), jax.ShapeDtypeStruct((1, D_MODEL), F32)],
        compiler_params=_params(1),
    )(h2, x2, target, w_gate, w_up, w_down, g_f, g2)


def _wgrad(a, b, name):
    t, m = a.shape
    n = b.shape[1]
    bm = _pick(m, 1664)
    bn = _pick(n, 1664)
    tk = _pick(t, 2048, 16)
    nk = t // tk

    def body(a_ref, b_ref, o_ref, o16_ref):
        k = pl.program_id(2)
        part = _dot_tn(a_ref[...], b_ref[...])

        @pl.when(k == 0)
        def _():
            o_ref[...] = part

        @pl.when(k != 0)
        def _():
            o_ref[...] += part

        @pl.when(k == nk - 1)
        def _():
            o16_ref[...] = o_ref[...].astype(BF16)

    out = pl.BlockSpec((bm, bn), lambda i, j, k: (i, j))
    return pl.pallas_call(
        body, name=name, grid=(m // bm, n // bn, nk),
        in_specs=[pl.BlockSpec((tk, bm), lambda i, j, k: (k, i)), pl.BlockSpec((tk, bn), lambda i, j, k: (k, j))],
        out_specs=[out, out],
        out_shape=[jax.ShapeDtypeStruct((m, n), F32), jax.ShapeDtypeStruct((m, n), BF16)],
        compiler_params=_params(3),
    )(a, b)


def _rope_tables(seq):
    inv = 1.0 / (ROPE_THETA ** (jnp.arange(0, B_ROPE, 2, dtype=F32) / B_ROPE))
    ang = jnp.arange(seq, dtype=F32)[:, None] * inv[None, :]
    cos, sin = jnp.cos(ang), jnp.sin(ang)
    zeros = jnp.zeros((seq, 64), F32)
    return jnp.concatenate([cos, cos, zeros], axis=1), jnp.concatenate([-sin, sin, zeros], axis=1)


def _pad_weights(w_in_t, w_q_t, w_kv_b):
    w_in_p = jnp.pad(w_in_t, ((0, D_IN_PAD - D_IN), (0, 0)))
    w_q_p = jnp.pad(w_q_t.reshape(B_HEADS, B_NOPE + B_ROPE, Q_LORA), ((0, 0), (0, 64), (0, 0))).reshape(1024, Q_LORA)
    kv = w_kv_b.reshape(KV_LORA, B_HEADS, B_NOPE + B_V)
    w_kv_p = jnp.concatenate([kv[:, :, :B_NOPE].reshape(KV_LORA, 512), kv[:, :, B_NOPE:].reshape(KV_LORA, 512)], axis=1)
    return w_in_p, w_q_p, w_kv_p


def _unpad_kv(g_kv_p):
    return jnp.concatenate([g_kv_p[:, :512].reshape(KV_LORA, B_HEADS, B_NOPE),
                            g_kv_p[:, 512:].reshape(KV_LORA, B_HEADS, B_V)], axis=2).reshape(KV_LORA, 1024)


_OWN_ROWS = dict(w_in=(272, 3), w_q_b=(64, 4), w_kv_b=(256, 0), w_out=(256, 1), w_gate=(176, 4), w_up=(176, 4), w_down=(176, 4))


def _local_step(x, target, lbl, g1, g_hn, g_qa, g_kva, g_mla, g2, g_f, w_in, w_q_b, w_kv_b, late_shards):
    n_batch, seq, _ = x.shape
    t = n_batch * seq
    x = x.reshape(t, D_MODEL)
    target = target.reshape(t, D_MODEL)
    w_in_p, w_q_p, w_kv_p = _pad_weights(w_in, w_q_b, w_kv_b)
    cosx, sinx = _rope_tables(seq)

    h1, hq, hi, hff, hfb, hg, cq, ckv, kr = _in_fwd(x, g1, w_in_p)
    o_f, st_f = _hgrn_fwd(hq, hi, hff, lbl, n_batch=n_batch, direction=0)
    o_r, st_r = _hgrn_fwd(hq, hi, hfb, lbl, n_batch=n_batch, direction=1)
    q, k, v, cqn, ckvn = _mla_prep(cq, ckv, kr, g_qa, g_kva, w_q_p, w_kv_p, cosx, sinx, n_batch=n_batch)
    o_attn, lse, *gathered = _attn_fwd(q, k, v, late_shards, n_batch=n_batch)
    w_out, w_gate, w_up, w_down = (_join_shards(n, a) for n, a in zip(_LATE, gathered))
    ycat, x2, h2 = _out_fwd(o_f, o_r, hg, o_attn, x, g_hn, g_mla, w_out, g2)
    act, dgate, dup, dx3_16, dx2, dx2_16, loss, dg_f, dg2 = _ffn_fwd_bwd(h2, x2, target, w_gate, w_up, w_down, g_f, g2)
    full = dict(w_out=_wgrad(ycat, dx2_16, "wgrad_out"), w_gate=_wgrad(dgate, h2, "wgrad_gate"),
                w_up=_wgrad(dup, h2, "wgrad_up"), w_down=_wgrad(act, dx3_16, "wgrad_down"))
    do_h, dhg, do_attn16, delta, dg_hn, dg_mla = _out_bwd(dx2_16, o_f, o_r, hg, o_attn, g_hn, g_mla, w_out)
    dqt, dk, dv, *recv_late = _attn_bwd(q, k, v, do_attn16, lse, delta,
                                        [_segments(n, full[n][1]) for n in _LATE], n_batch=n_batch)
    dcq, dckv, dkr, dqp16, dkvp16, dg_qa, dg_kva = _mla_prep_bwd(dqt, dk, dv, cq, ckv, g_qa, g_kva, w_q_p, w_kv_p,
                                                                  cosx, sinx, n_batch=n_batch)
    dq_f, dv_f, dz_f, dl_f = _hgrn_bwd(hq, hi, hff, do_h, st_f, lbl, n_batch=n_batch, direction=0)
    dq_r, dv_r, dz_r, dl_r = _hgrn_bwd(hq, hi, hfb, do_h, st_r, lbl, n_batch=n_batch, direction=1)
    dx, dproj16, dg1 = _in_bwd(dq_f, dq_r, dv_f, dv_r, dz_f, dz_r, dhg, hq, dcq, dckv, dkr, dx2, x, g1, w_in_p)

    full.update(w_in=_wgrad(dproj16, h1, "wgrad_in"), w_q_b=_wgrad(dqp16, cqn, "wgrad_q_b"))
    g_kv = _unpad_kv(_wgrad(ckvn, dkvp16, "wgrad_kv_b")[0])
    early16 = dict(w_in=full["w_in"][1][:D_IN].reshape(N_CHIPS, D_IN // N_CHIPS, D_MODEL),
                   w_q_b=full["w_q_b"][1].reshape(B_HEADS, QK_PAD, Q_LORA)[:, :B_NOPE + B_ROPE],
                   w_kv_b=_segments("w_kv_b", g_kv).astype(BF16))
    small = dict(norm1_g=dg1, lb_logits=jnp.stack([dl_f, dl_r]), hgrn_norm_g=dg_hn, q_a_norm_g=dg_qa, kv_a_norm_g=dg_kva,
                 mla_norm_g=dg_mla, norm2_g=dg2, final_norm_g=dg_f, loss=loss[0:1, 0])
    return dx.reshape(n_batch, seq, D_MODEL), small, {**{n: f for n, (f, _) in full.items()}, "w_kv_b": g_kv}, early16, \
        dict(zip(_LATE, recv_late))


_HBM = pl.BlockSpec(memory_space=pltpu.HBM)
_MESH = pl.DeviceIdType.MESH


def _place():
    x, y, c = lax.axis_index("x"), lax.axis_index("y"), lax.axis_index("c")
    other_chips = [(1 - x, y), (x, 1 - y), (1 - x, 1 - y)]
    return x, y, c, other_chips


def _gather_copies(w_ref, wall_ref, send_sems, recv_sems, local_sem, base=0):
    x, y, c, chips = _place()
    mine = 2 * x + y

    def mk(j, chip_index, to):
        return pltpu.make_async_remote_copy(src_ref=w_ref, dst_ref=wall_ref.at[chip_index], send_sem=send_sems.at[base + j],
                                            recv_sem=recv_sems.at[base + j], device_id=to, device_id_type=_MESH)

    local = pltpu.make_async_copy(w_ref, wall_ref.at[mine], local_sem)
    sends = [mk(j, mine, (*chip, c)) for j, chip in enumerate(chips)]
    recvs = [mk(j, 2 * px + py, (x, y, c)) for j, (px, py) in enumerate(chips)]
    return local, sends, recvs


def _gather_start(*refs):
    local, sends, _ = _gather_copies(*refs)
    local.start()
    for cp in sends:
        cp.start()


def _gather_wait(*refs):
    local, sends, recvs = _gather_copies(*refs)
    for cp in recvs:
        cp.wait_recv()
    for cp in sends:
        cp.wait_send()
    local.wait()


def _scatter_copies(g_ref, recv_ref, send_sems, recv_sems, base=0):
    x, y, c, chips = _place()

    def mk(j, src_index, to):
        return pltpu.make_async_remote_copy(src_ref=g_ref.at[src_index], dst_ref=recv_ref.at[j],
                                            send_sem=send_sems.at[base + j], recv_sem=recv_sems.at[base + j],
                                            device_id=to, device_id_type=_MESH)

    sends = [mk(j, 2 * px + py, (px, py, c)) for j, (px, py) in enumerate(chips)]
    recvs = [mk(j, 0, (x, y, c)) for j in range(3)]
    return sends, recvs


def _scatter_start(*refs):
    for cp in _scatter_copies(*refs)[0]:
        cp.start()


def _scatter_wait(*refs):
    sends, recvs = _scatter_copies(*refs)
    for cp in recvs:
        cp.wait_recv()
    for cp in sends:
        cp.wait_send()


def _half(ref, which, axis):
    n = ref.shape[axis] // 2
    idx = [slice(None)] * len(ref.shape)
    idx[axis] = pl.ds(which * n, n)
    return ref.at[tuple(idx)]


def _split_axis(a):
    return 1 if a.shape[1] % 256 == 0 else 0


def _gather_halves(src_ref, dst_ref, send_sems, recv_sems, local_sem, base, axis):
    x, y, c, chips = _place()
    mine = 2 * x + y

    def remote(src, dst, n, to):
        return pltpu.make_async_remote_copy(src_ref=src, dst_ref=dst, send_sem=send_sems.at[base + n],
                                            recv_sem=recv_sems.at[base + n], device_id=to, device_id_type=_MESH)

    def slot(chip_index, which):
        return _half(dst_ref.at[chip_index], which, axis)

    local = pltpu.make_async_copy(src_ref, dst_ref.at[mine], local_sem)
    first = [remote(_half(src_ref, c, axis), slot(mine, c), j, (*chip, c)) for j, chip in enumerate(chips)]
    landed = [remote(_half(src_ref, c, axis), slot(2 * px + py, c), j, (x, y, c)) for j, (px, py) in enumerate(chips)]
    passed = [remote(slot(2 * px + py, c), slot(2 * px + py, c), 3 + j, (x, y, 1 - c)) for j, (px, py) in enumerate(chips)]
    handed = [remote(_half(src_ref, c, axis), slot(2 * px + py, 1 - c), 3 + j, (x, y, c)) for j, (px, py) in enumerate(chips)]

    def start():
        local.start()
        for cp in first:
            cp.start()

    def pass_on():
        for arrived, onward in zip(landed, passed):
            arrived.wait_recv()
            onward.start()

    def finish():
        for cp in handed:
            cp.wait_recv()
        for cp in first + passed:
            cp.wait_send()
        local.wait()

    return start, pass_on, finish


def _gather_shards(srcs, small):
    n = len(srcs)

    def body(*refs):
        src_refs, s_ref, dst_refs, sall_ref = refs[:n], refs[n], refs[n + 1:2 * n + 1], refs[2 * n + 1]
        send_sems, recv_sems, local_sems = refs[2 * n + 2:]
        _gather_start(s_ref, sall_ref, send_sems, recv_sems, local_sems.at[n], 6 * n)
        steps = [_gather_halves(src_refs[i], dst_refs[i], send_sems, recv_sems, local_sems.at[i], 6 * i, _split_axis(srcs[i]))
                 for i in range(n)]
        for phase in range(3):
            for step in steps:
                step[phase]()
        _gather_wait(s_ref, sall_ref, send_sems, recv_sems, local_sems.at[n], 6 * n)

    return pl.pallas_call(
        body, name="gather_shards", in_specs=[_HBM] * (n + 1), out_specs=[_HBM] * (n + 1),
        out_shape=[jax.ShapeDtypeStruct((N_CHIPS,) + a.shape, a.dtype) for a in list(srcs) + [small]],
        scratch_shapes=[pltpu.SemaphoreType.DMA((6 * n + 3,)), pltpu.SemaphoreType.DMA((6 * n + 3,)),
                        pltpu.SemaphoreType.DMA((n + 1,))],
    )(*srcs, small)


_SEM = pl.BlockSpec(memory_space=pltpu.SEMAPHORE)
_SIDE_EFFECT = pltpu.SideEffectType.DATAFLOW_SIDE_EFFECTING


def _scatter_begin(gsegs):
    n = len(gsegs)
    def hbm(a):
        return pltpu.with_memory_space_constraint(a, pltpu.HBM)

    lands = [hbm(lax.empty((3,) + g.shape[1:], g.dtype)) for g in gsegs]

    def body(*refs):
        g_refs, land_refs, send_sems, recv_sems, token = refs[:n], refs[n:2 * n], refs[2 * n], refs[2 * n + 1], refs[-1]
        for i in range(n):
            _scatter_start(g_refs[i], land_refs[i], send_sems, recv_sems, 3 * i)
        token[...] = jnp.zeros_like(token)

    out = pl.pallas_call(
        body, name="scatter_begin", in_specs=[_HBM] * (2 * n),
        out_specs=(_SEM, _SEM) + (_HBM,) * (2 * n) + (pl.BlockSpec(memory_space=pltpu.VMEM),),
        out_shape=(pltpu.SemaphoreType.DMA((3 * n,)), pltpu.SemaphoreType.DMA((3 * n,)))
        + tuple(pltpu.HBM(a.shape, a.dtype) for a in list(gsegs) + lands) + (jax.ShapeDtypeStruct((8, 128), F32),),
        input_output_aliases={i: 2 + i for i in range(2 * n)},
        compiler_params=pltpu.CompilerParams(has_side_effects=_SIDE_EFFECT),
    )(*[hbm(g) for g in gsegs], *lands)
    return out[0], out[1], out[2:2 + n], out[2 + n:2 + 2 * n], out[-1]


def _scatter_end(send_sems, recv_sems, gsegs, lands, after):
    n = len(gsegs)

    def body(*refs):
        g_refs, land_refs, ssems, rsems = refs[:n], refs[n:2 * n], refs[2 * n], refs[2 * n + 1]
        for i in range(n):
            _scatter_wait(g_refs[i], land_refs[i], ssems, rsems, 3 * i)

    out = pl.pallas_call(
        body, name="scatter_end", in_specs=[_HBM] * (2 * n) + [_SEM, _SEM] + [pl.BlockSpec(memory_space=pl.ANY)] * len(after),
        out_specs=(_HBM,) * (2 * n), out_shape=tuple(pltpu.HBM(a.shape, a.dtype) for a in list(gsegs) + list(lands)),
        input_output_aliases={i: i for i in range(2 * n)},
        compiler_params=pltpu.CompilerParams(has_side_effects=_SIDE_EFFECT),
    )(*gsegs, *lands, send_sems, recv_sems, *after)
    return out[n:]


def _swap_sibling(parts, small=None):
    n = len(parts)
    extra = small is not None

    def body(*refs):
        ins, outs = refs[:n + extra], refs[n + extra:2 * (n + extra)]
        send_sems, recv_sems, local_sem = refs[2 * (n + extra):]
        x, y, c, _ = _place()
        cps = [pltpu.make_async_remote_copy(src_ref=ins[i], dst_ref=outs[i], send_sem=send_sems.at[i],
                                            recv_sem=recv_sems.at[i], device_id=(x, y, 1 - c), device_id_type=_MESH)
               for i in range(n)]
        for cp in cps:
            cp.start()
        if extra:
            s_ref, sall_ref = ins[n], outs[n]
            me = 4 * x + 2 * y + c
            flips = [(fx, fy, fc) for fx in (0, 1) for fy in (0, 1) for fc in (0, 1)][1:]

            def peer(f):
                return tuple((1 - a) if b else a for a, b in zip((x, y, c), f))

            def sm(r, index, to):
                return pltpu.make_async_remote_copy(src_ref=s_ref, dst_ref=sall_ref.at[index], send_sem=send_sems.at[n + r],
                                                    recv_sem=recv_sems.at[n + r], device_id=to, device_id_type=_MESH)

            local = pltpu.make_async_copy(s_ref, sall_ref.at[me], local_sem)
            local.start()
            sends = [sm(r, me, peer(f)) for r, f in enumerate(flips)]
            for cp in sends:
                cp.start()
            for r, f in enumerate(flips):
                px, py, pc = peer(f)
                sm(r, 4 * px + 2 * py + pc, (x, y, c)).wait_recv()
            for cp in sends:
                cp.wait_send()
            local.wait()
        for cp in cps:
            cp.wait()

    return pl.pallas_call(
        body, name="swap_small" if extra else "swap_sibling", in_specs=[_HBM] * (n + extra), out_specs=[_HBM] * (n + extra),
        out_shape=[jax.ShapeDtypeStruct(p.shape, p.dtype) for p in parts]
        + ([jax.ShapeDtypeStruct((8,) + small.shape, F32)] if extra else []),
        scratch_shapes=[pltpu.SemaphoreType.DMA((n + 7,)), pltpu.SemaphoreType.DMA((n + 7,)), pltpu.SemaphoreType.DMA(())],
    )(*parts, *([small] if extra else []))


def _sum_segments(chip, full, recv, tm, pitch, name, after=()):
    _, rows, cols = recv.shape

    def body(chip_ref, o_ref, r_ref, *rest):
        acc = o_ref[...]
        for j in range(3):
            acc = acc + r_ref[j].astype(F32)
        rest[-1][...] = acc

    return pl.pallas_call(
        body, name=name,
        grid_spec=pltpu.PrefetchScalarGridSpec(
            num_scalar_prefetch=1, grid=(rows // tm,),
            in_specs=[pl.BlockSpec((tm, cols), lambda i, c: (c[0] * pitch + i, 0)),
                      pl.BlockSpec((3, tm, cols), lambda i, c: (0, i, 0))]
            + [pl.BlockSpec(a.shape, lambda i, c, nd=a.ndim: (0,) * nd) for a in after],
            out_specs=pl.BlockSpec((tm, cols), lambda i, c: (i, 0))),
        out_shape=jax.ShapeDtypeStruct((rows, cols), F32), compiler_params=_params(1),
    )(chip, full, recv, *after)


def _sum_devices(sall):
    def body(s_ref, o_ref):
        acc = s_ref[0]
        for d in range(1, 8):
            acc = acc + s_ref[d]
        o_ref[...] = acc

    return pl.pallas_call(body, name="sum_devices", out_shape=jax.ShapeDtypeStruct(sall.shape[1:], F32))(sall)


def _adamw(w, m, v, ga, gb, name):
    rows, cols = w.shape
    tm = _row_tile(rows)
    two = gb is not None

    def body(*refs):
        w_ref, m_ref, v_ref, ga_ref = refs[:4]
        g_ref, d_ref, m2_ref, v2_ref = refs[-4:]
        g = ga_ref[...] + refs[4][...] if two else ga_ref[...]
        m2 = ADAM_B1 * m_ref[...] + (1.0 - ADAM_B1) * g
        v2 = ADAM_B2 * v_ref[...] + (1.0 - ADAM_B2) * (g * g)
        m_hat = m2 / (1.0 - ADAM_B1 ** ADAM_STEP)
        v_hat = v2 / (1.0 - ADAM_B2 ** ADAM_STEP)
        g_ref[...] = g
        d_ref[...] = -ADAM_LR * (m_hat / (jnp.sqrt(v_hat) + ADAM_EPS) + ADAM_WD * w_ref[...])
        m2_ref[...] = m2
        v2_ref[...] = v2

    blk = _rows(tm, cols)
    args = (w, m, v, ga) + ((gb,) if two else ())
    return pl.pallas_call(
        body, name=name, grid=(rows // tm,), in_specs=[blk] * len(args), out_specs=[blk] * 4,
        out_shape=[jax.ShapeDtypeStruct(w.shape, F32)] * 4, compiler_params=_params(1),
    )(*args)


_TRANSPOSED = ("w_in", "w_q_b", "w_gate", "w_up")
_COLUMN_SHARDED = ("w_kv_b",)
_FULL_SHAPES = dict(w_in=(D_IN, D_MODEL), w_q_b=(768, Q_LORA), w_kv_b=(KV_LORA, 1024), w_out=(D_MODEL, D_MODEL),
                    w_gate=(D_FF, D_MODEL), w_up=(D_FF, D_MODEL), w_down=(D_FF, D_MODEL))
_SMALL = (("norm1_g", 1024), ("lb_logits", 2048), ("hgrn_norm_g", 512), ("q_a_norm_g", 384), ("kv_a_norm_g", 256),
          ("mla_norm_g", 512), ("norm2_g", 1024), ("final_norm_g", 1024))
_UPDATE_ROWS = 48


def _pad_rows(a, rows):
    return jnp.pad(a, ((0, rows - a.shape[0]), (0, 0)))


_EARLY = ("w_in", "w_q_b", "w_kv_b")
_LATE = ("w_out", "w_gate", "w_up", "w_down")


def _segments(name, g):
    r, c = g.shape
    if name in _COLUMN_SHARDED:
        return g.reshape(r, N_CHIPS, c // N_CHIPS).transpose(1, 0, 2)
    return g.reshape(N_CHIPS, r // N_CHIPS, c)


def _own_segment(name, g, chip):
    r, c = g.shape
    if name in _COLUMN_SHARDED:
        return lax.dynamic_slice_in_dim(g, chip * (c // N_CHIPS), c // N_CHIPS, axis=1)
    return lax.dynamic_slice_in_dim(g, chip * (r // N_CHIPS), r // N_CHIPS, axis=0)


def _join_shards(name, seg):
    r, c = _FULL_SHAPES[name]
    if name in _COLUMN_SHARDED:
        return seg.transpose(1, 0, 2).reshape(r, c)
    return seg.reshape(r, c)


def kernel(x, norm1_g, w_in, lb_logits, hgrn_norm_g, q_a_norm_g, w_q_b, kv_a_norm_g, w_kv_b, mla_norm_g, w_out, norm2_g, w_gate, w_up, w_down, final_norm_g, loss_target, m_norm1_g, m_w_in, m_lb_logits, m_hgrn_norm_g, m_q_a_norm_g, m_w_q_b, m_kv_a_norm_g, m_w_kv_b, m_mla_norm_g, m_w_out, m_norm2_g, m_w_gate, m_w_up, m_w_down, m_final_norm_g, v_norm1_g, v_w_in, v_lb_logits, v_hgrn_norm_g, v_q_a_norm_g, v_w_q_b, v_kv_a_norm_g, v_w_kv_b, v_mla_norm_g, v_w_out, v_norm2_g, v_w_gate, v_w_up, v_w_down, v_final_norm_g):
    names = ("norm1_g", "w_in", "lb_logits", "hgrn_norm_g", "q_a_norm_g", "w_q_b", "kv_a_norm_g", "w_kv_b", "mla_norm_g",
             "w_out", "norm2_g", "w_gate", "w_up", "w_down", "final_norm_g")
    w = dict(zip(names, (norm1_g, w_in, lb_logits, hgrn_norm_g, q_a_norm_g, w_q_b, kv_a_norm_g, w_kv_b, mla_norm_g,
                         w_out, norm2_g, w_gate, w_up, w_down, final_norm_g)))
    m = dict(zip(names, (m_norm1_g, m_w_in, m_lb_logits, m_hgrn_norm_g, m_q_a_norm_g, m_w_q_b, m_kv_a_norm_g, m_w_kv_b,
                         m_mla_norm_g, m_w_out, m_norm2_g, m_w_gate, m_w_up, m_w_down, m_final_norm_g)))
    v = dict(zip(names, (v_norm1_g, v_w_in, v_lb_logits, v_hgrn_norm_g, v_q_a_norm_g, v_w_q_b, v_kv_a_norm_g, v_w_kv_b,
                         v_mla_norm_g, v_w_out, v_norm2_g, v_w_gate, v_w_up, v_w_down, v_final_norm_g)))
    matrices = _EARLY + _LATE
    chip = 2 * lax.axis_index("x") + lax.axis_index("y")

    def shard2d(a, n):
        return jnp.swapaxes(a[0], 0, 1) if n in _TRANSPOSED else a[0]

    def unshard2d(a, n):
        return (jnp.swapaxes(a, 0, 1) if n in _TRANSPOSED else a)[None]

    w16 = {n: shard2d(w[n], n).astype(BF16) for n in matrices}
    lb8 = _pad_rows(lb_logits.reshape(4, 128), 8)
    *early, lball = _gather_shards([w16[n] for n in _EARLY], lb8)
    lbl = lball[:, :4].reshape(N_CHIPS, 2, 2, 128).transpose(1, 2, 0, 3).reshape(2, 2, A_WIDTH)

    grad_x, g, full, early16, recv = _local_step(
        x, loss_target, lbl, norm1_g, hgrn_norm_g, q_a_norm_g, kv_a_norm_g, mla_norm_g, norm2_g, final_norm_g[None, :],
        *(_join_shards(n, a) for n, a in zip(_EARLY, early)), [w16[n] for n in _LATE])

    n_small = sum(size for _, size in _SMALL)
    small = jnp.concatenate([g[n].reshape(-1) for n, _ in _SMALL] + [g["loss"], jnp.zeros((SMALL_ROWS * 128 - n_small - 1,), F32)])
    chip1 = chip.reshape(1).astype(jnp.int32)
    out, raw = {}, {}

    def update(n, part, sib):
        raw[n] = _adamw(shard2d(w[n], n), shard2d(m[n], n), shard2d(v[n], n), part, sib, f"adamw_{n}")
        out[n] = tuple(unshard2d(r, n) for r in raw[n])

    send_sems, recv_sems, segs, lands, token = _scatter_begin([early16[n] for n in _EARLY])
    parts = [_sum_segments(chip1, full[n], recv[n], *_OWN_ROWS[n], f"sum_{n}", after=(token,)) for n in _LATE]
    for n, part, sib in zip(_LATE, parts, _swap_sibling(parts)):
        update(n, part, sib)
    recv_early = _scatter_end(send_sems, recv_sems, segs, lands, after=[raw[n][1] for n in _LATE])
    full["w_kv_b"] = _own_segment("w_kv_b", full["w_kv_b"], chip)
    parts = [_sum_segments(chip1, full[n], r, *_OWN_ROWS[n], f"sum_{n}") for n, r in zip(_EARLY, recv_early)]
    *sibs, small_all = _swap_sibling(parts, small.reshape(SMALL_ROWS, 128))
    for n, part, sib in zip(_EARLY, parts, sibs):
        update(n, part, sib)
    small_sum = _sum_devices(small_all).reshape(-1)
    loss = small_sum[n_small]

    small_g, off = {}, 0
    for n, size in _SMALL:
        small_g[n] = small_sum[off:off + size]
        off += size
    small_g["lb_logits"] = lax.dynamic_slice_in_dim(small_g["lb_logits"].reshape(2, 2, A_WIDTH), chip * 128, 128, axis=2)
    small_names = tuple(n for n, _ in _SMALL)

    def pack(d):
        return _pad_rows(jnp.concatenate([d[n].reshape(-1) for n in small_names]).reshape(-1, 128), _UPDATE_ROWS)

    res = _adamw(pack(w), pack(m), pack(v), pack(small_g), None, "adamw_small")
    off = 0
    flat = [r.reshape(-1) for r in res]
    for n in small_names:
        size = w[n].size
        out[n] = tuple(f[off:off + size].reshape(w[n].shape) for f in flat)
        off += size

    return (loss, grad_x) + tuple(out[n][i] for i in range(4) for n in names)
```

```python
import functools

import jax
import jax.numpy as jnp
from jax import lax
from jax.experimental import pallas as pl
from jax.experimental.pallas import tpu as pltpu

F32 = jnp.float32
BF16 = jnp.bfloat16

D_MODEL = 1024
A_WIDTH = 512
HEAD_PAIRS = 4
PAIRS_AT_ONCE = 2
CHUNK = 64
B_HEADS = 4
B_NOPE = 128
B_ROPE = 64
B_V = 128
QK_PAD = 256
Q_LORA = 384
KV_LORA = 256
D_FF = 2816
D_IN = 3264
D_IN_PAD = 3328
IN_WIDTHS = (512, 512, 512, 512, 512, Q_LORA, KV_LORA, 128)
ROPE_THETA = 10000.0
EPS = 1e-6
ATTN_SCALE = (B_NOPE + B_ROPE) ** -0.5
LOG2E = 1.4426950408889634
SCALE_LOG2E = ATTN_SCALE * LOG2E

ADAM_LR = 0.001
ADAM_B1 = 0.9
ADAM_B2 = 0.999
ADAM_EPS = 1e-08
ADAM_WD = 0.01
ADAM_STEP = 10

VMEM_LIMIT_BYTES = 60 * 1024 * 1024
N_CHIPS = 4
SMALL_ROWS = 56


def _params(n_axes):
    return pltpu.CompilerParams(dimension_semantics=("arbitrary",) * n_axes,
                                vmem_limit_bytes=VMEM_LIMIT_BYTES)


def _dot(a, b):
    return jnp.dot(a, b, preferred_element_type=F32)


def _dot_nt(a, b):
    return lax.dot_general(a, b, (((1,), (1,)), ((), ())), preferred_element_type=F32)


def _dot_tn(a, b):
    return lax.dot_general(a, b, (((0,), (0,)), ((), ())), preferred_element_type=F32)


def _split3(x):
    x1 = x.astype(BF16)
    r = x - x1.astype(F32)
    x2 = r.astype(BF16)
    x3 = (r - x2.astype(F32)).astype(BF16)
    return x1, x2, x3


def _exact_right(x, m16):
    x1, x2, x3 = _split3(x)
    return _dot(x1, m16) + _dot(x2, m16) + _dot(x3, m16)


def _iota2(shape, dim):
    return lax.broadcasted_iota(jnp.int32, shape, dim)


def _sigmoid(x):
    return jax.nn.sigmoid(x)


def _pick(dim, cap, mult=128):
    if dim <= cap:
        return dim
    best = None
    for d in range(mult, cap + 1, mult):
        if dim % d == 0:
            best = d
    assert best is not None, (dim, cap, mult)
    return best


def _row_tile(t, cap=256):
    return _pick(t, cap, 8)


ROW_GROUP = 256


def _row_groups(tm):
    size = min(tm, ROW_GROUP)
    return [slice(r, r + size) for r in range(0, tm, size)]


def _full(shape, single=False):
    if single:
        return pl.BlockSpec(shape, lambda *_: (0,) * len(shape), pipeline_mode=pl.Buffered(1))
    return pl.BlockSpec(shape, lambda *_: (0,) * len(shape))


def _rows(tm, width):
    return pl.BlockSpec((tm, width), lambda i: (i, 0))


def _acc_rows(ref, val, first):
    s = jnp.sum(val, axis=0, keepdims=True)

    @pl.when(first)
    def _():
        ref[...] = s

    @pl.when(jnp.logical_not(first))
    def _():
        ref[...] += s


def _in_fwd(x, g1, w_in_p):
    t = x.shape[0]
    tm = _row_tile(t, 2 * ROW_GROUP)

    def body(x_ref, g_ref, w_ref, h_ref, *outs):
        gain = g_ref[...]

        def group(xv):
            r = lax.rsqrt(jnp.mean(xv * xv, axis=-1, keepdims=True) + EPS)
            h = ((xv * r) * gain).astype(BF16)
            proj, off = [], 0
            for w in IN_WIDTHS:
                proj.append(_dot_nt(h, w_ref[off:off + w, :]))
                off += w
            return h, proj

        groups = _row_groups(tm)
        done = [group(xv) for xv in [x_ref[g, :] for g in groups]]
        for g, (h, proj) in zip(groups, done):
            h_ref[g, :] = h
            for o_ref, p in zip(outs, proj):
                o_ref[g, :] = p

    return pl.pallas_call(
        body, name="in_fwd", grid=(t // tm,),
        in_specs=[_rows(tm, D_MODEL), _full((1, D_MODEL)), _full((D_IN_PAD, D_MODEL))],
        out_specs=[_rows(tm, D_MODEL)] + [_rows(tm, w) for w in IN_WIDTHS],
        out_shape=[jax.ShapeDtypeStruct((t, D_MODEL), BF16)]
        + [jax.ShapeDtypeStruct((t, w), F32) for w in IN_WIDTHS],
        compiler_params=_params(1),
    )(x, g1, w_in_p)


def _in_bwd(dq_f, dq_r, dv_f, dv_r, dz_f, dz_r, dhg, hq, dcq, dckv, dkr, dx2, x, g1, w_in_p):
    t = x.shape[0]
    tm = _row_tile(t, 2 * ROW_GROUP)

    def body(dqf_ref, dqr_ref, dvf_ref, dvr_ref, dzf_ref, dzr_ref, dhg_ref, hq_ref, dcq_ref, dckv_ref,
             dkr_ref, dx2_ref, x_ref, g_ref, w_ref, dx_ref, dp_ref, dg_ref):
        gain = g_ref[...]

        def group(dqf, dqr, dvf, dvr, dzf, dzr, dhg, hqv, dcq, dckv, dkr, dx2, xv):
            sg = _sigmoid(hqv)
            dhq = (dqf + dqr) * (sg * (1.0 + hqv * (1.0 - sg)))
            pieces = [p.astype(BF16) for p in (dhq, dvf + dvr, dzf, dzr, dhg, dcq, dckv, dkr)]
            dh = _dot(jnp.concatenate(pieces, axis=1), w_ref[...])
            r = lax.rsqrt(jnp.mean(xv * xv, axis=-1, keepdims=True) + EPS)
            xh = xv * r
            dxh = dh * gain
            return dx2 + r * (dxh - xh * jnp.mean(dxh * xh, axis=-1, keepdims=True)), pieces, dh * xh

        ins = (dqf_ref, dqr_ref, dvf_ref, dvr_ref, dzf_ref, dzr_ref, dhg_ref, hq_ref, dcq_ref, dckv_ref, dkr_ref,
               dx2_ref, x_ref)
        groups = _row_groups(tm)
        done = [group(*vals) for vals in [[ref[g, :] for ref in ins] for g in groups]]
        for g, (dx, pieces, _) in zip(groups, done):
            dx_ref[g, :] = dx
            off = 0
            for p16, w in zip(pieces, IN_WIDTHS):
                dp_ref[g, off:off + w] = p16
                off += w
        _acc_rows(dg_ref, jnp.concatenate([d[2] for d in done], axis=0), pl.program_id(0) == 0)

    a512 = _rows(tm, A_WIDTH)
    return pl.pallas_call(
        body, name="in_bwd", grid=(t // tm,),
        in_specs=[a512] * 8 + [_rows(tm, Q_LORA), _rows(tm, KV_LORA), _rows(tm, 128), _rows(tm, D_MODEL),
                               _rows(tm, D_MODEL), _full((1, D_MODEL)), _full((D_IN_PAD, D_MODEL))],
        out_specs=[_rows(tm, D_MODEL), _rows(tm, D_IN_PAD), _full((1, D_MODEL))],
        out_shape=[jax.ShapeDtypeStruct((t, D_MODEL), F32), jax.ShapeDtypeStruct((t, D_IN_PAD), BF16),
                   jax.ShapeDtypeStruct((1, D_MODEL), F32)],
        compiler_params=_params(1),
    )(dq_f, dq_r, dv_f, dv_r, dz_f, dz_r, dhg, hq, dcq, dckv, dkr, dx2, x, g1, w_in_p)


def _lower_bound(lbl_ref, direction):
    l0 = lbl_ref[direction, 0:1, :]
    l1 = lbl_ref[direction, 1:2, :]
    m = jnp.maximum(l0, l1)
    e0 = jnp.exp(l0 - m)
    e1 = jnp.exp(l1 - m)
    return e0 / (e0 + e1)


def _hgrn_consts(rb, reverse):
    row = _iota2((rb, rb), 0)
    col = _iota2((rb, rb), 1)
    same = (row // CHUNK) == (col // CHUNK)
    tri = jnp.logical_and(same, (col >= row) if reverse else (col <= row))
    tri_t = jnp.logical_and(same, (col <= row) if reverse else (col >= row))
    r128 = _iota2((128, 128), 0)
    c128 = _iota2((128, 128), 1)
    bd = (r128 < 64) == (c128 < 64)
    lane = _iota2((1, 128), 1)
    m0 = (lane < 64).astype(F32)
    return tri, tri_t, bd, (m0, 1.0 - m0)


def _per_chunk(x, fn):
    n = x.shape[0] // CHUNK
    return jnp.concatenate([jnp.broadcast_to(fn(x[c * CHUNK:(c + 1) * CHUNK]), (CHUNK, x.shape[1])) for c in range(n)],
                           axis=0)


def _chunk_cumsum(x, reverse):
    rb = x.shape[0]
    pos = _iota2(x.shape, 0) % CHUNK
    step = 1
    while step < CHUNK:
        if reverse:
            x = x + jnp.where(pos < CHUNK - step, pltpu.roll(x, rb - step, 0), 0.0)
        else:
            x = x + jnp.where(pos >= step, pltpu.roll(x, step, 0), 0.0)
        step *= 2
    return x


def _hgrn_block(z, hqv, lb, reverse):
    sig = _sigmoid(z)
    sn = _sigmoid(-z)
    q = hqv * _sigmoid(hqv)
    f = lb + (1.0 - lb) * sig
    k = (1.0 - lb) * sn
    lf = jnp.log(f)
    cum = _chunk_cumsum(lf, reverse)
    last = _per_chunk(cum, (lambda a: a[0:1]) if reverse else (lambda a: a[CHUNK - 1:CHUNK]))
    e_neg = jnp.exp(-cum)
    e_end = jnp.exp(last - cum)
    a = jnp.exp(cum)
    return dict(sig=sig, sn=sn, q=q, f=f, k=k, a=a, e_neg=e_neg, e_end=e_end,
                q_dec=q * a, k_inv=k * e_neg, k_end=k * e_end, d=jnp.exp(last))


def _hgrn_dims(t, n_batch):
    s = t // n_batch
    rb = _pick(s, 256, CHUNK)
    return s, rb, s // rb, rb // CHUNK


def _hgrn_fwd(hq, hi, hf, lbl, *, n_batch, direction):
    t = hq.shape[0]
    reverse = direction == 1
    s, rb, nb, nc = _hgrn_dims(t, n_batch)

    def tmap(b, j):
        return (b * nb + ((nb - 1 - j) if reverse else j), 0)

    def smap(b, j):
        return (b * nb + ((nb - 1 - j) if reverse else j), 0, 0, 0)

    def body(hq_ref, hi_ref, hf_ref, lbl_ref, o_ref, st_ref, st_scr):
        @pl.when(pl.program_id(1) == 0)
        def _():
            st_scr[...] = jnp.zeros_like(st_scr)

        lb_all = _lower_bound(lbl_ref, direction)
        tri, _, bd, masks = _hgrn_consts(rb, reverse)
        order = range(nc - 1, -1, -1) if reverse else range(nc)

        pairs = [slice(p * 128, (p + 1) * 128) for p in range(HEAD_PAIRS)]
        chunks = [slice(c * CHUNK, (c + 1) * CHUNK) for c in range(nc)]
        w = _hgrn_block(hf_ref[...], hq_ref[...], lb_all, reverse)
        v16 = hi_ref[...].astype(BF16)
        qd16 = w["q_dec"].astype(BF16)
        ki16 = w["k_inv"].astype(BF16)
        ke16 = w["k_end"].astype(BF16)
        sc = [[_dot_nt(jnp.where(mh > 0.0, qd16[:, ls], 0.0).astype(BF16), ki16[:, ls]) for mh in masks] for ls in pairs]
        pv = [[_dot(jnp.where(tri, s_e, 0.0).astype(BF16), v16[:, ls]) for s_e in sc_p] for sc_p, ls in zip(sc, pairs)]
        o_intra = [jnp.where(masks[0] > 0.0, pv_p[0], pv_p[1]) for pv_p in pv]
        ut = [[jnp.where(bd, _dot_tn(v16[rs, ls], ke16[rs, ls]), 0.0) for ls in pairs] for rs in chunks]
        st = [st_scr[p] for p in range(HEAD_PAIRS)]
        for c in order:
            rs = chunks[c]
            inter = [_dot_nt(qd16[rs, ls], st[p].astype(BF16)) for p, ls in enumerate(pairs)]
            for p, ls in enumerate(pairs):
                o_ref[rs, ls] = o_intra[p][rs] + inter[p]
                st_ref[c, p] = st[p]
                st[p] = st[p] * w["d"][c * CHUNK:c * CHUNK + 1, ls] + ut[c][p]
        for p in range(HEAD_PAIRS):
            st_scr[p] = st[p]

    blk = pl.BlockSpec((rb, A_WIDTH), tmap)
    return pl.pallas_call(
        body, name=f"hgrn_fwd_{direction}", grid=(n_batch, nb),
        in_specs=[blk, blk, blk, _full((2, 2, A_WIDTH))],
        out_specs=[blk, pl.BlockSpec((nc, HEAD_PAIRS, 128, 128), smap)],
        out_shape=[jax.ShapeDtypeStruct((t, A_WIDTH), F32),
                   jax.ShapeDtypeStruct((t // CHUNK, HEAD_PAIRS, 128, 128), F32)],
        scratch_shapes=[pltpu.VMEM((HEAD_PAIRS, 128, 128), F32)],
        compiler_params=_params(2),
    )(hq, hi, hf, lbl)


def _hgrn_bwd(hq, hi, hf, do, st, lbl, *, n_batch, direction):
    t = hq.shape[0]
    reverse = direction == 1
    s, rb, nb, nc = _hgrn_dims(t, n_batch)

    def tmap(b, j):
        return (b * nb + (j if reverse else (nb - 1 - j)), 0)

    def smap(b, j):
        return (b * nb + (j if reverse else (nb - 1 - j)), 0, 0, 0)

    def body(hq_ref, hi_ref, hf_ref, do_ref, st_ref, lbl_ref, dq_ref, dv_ref, dz_ref, dl_ref, g_scr, dlb_scr):
        b = pl.program_id(0)
        j = pl.program_id(1)

        @pl.when(jnp.logical_and(b == 0, j == 0))
        def _():
            dlb_scr[...] = jnp.zeros_like(dlb_scr)

        @pl.when(j == 0)
        def _():
            g_scr[...] = jnp.zeros_like(g_scr)

        lb_all = _lower_bound(lbl_ref, direction)
        tri, tri_t, bd, masks = _hgrn_consts(rb, reverse)
        order = range(nc) if reverse else range(nc - 1, -1, -1)

        chunks = [slice(c * CHUNK, (c + 1) * CHUNK) for c in range(nc)]

        def lanes(per_pair):
            return jnp.concatenate(per_pair, axis=1)

        for first_pair in range(0, HEAD_PAIRS, PAIRS_AT_ONCE):
            here = slice(first_pair * 128, (first_pair + PAIRS_AT_ONCE) * 128)
            pairs = [slice(p * 128, (p + 1) * 128) for p in range(PAIRS_AT_ONCE)]
            lb = lb_all[:, here]
            w = _hgrn_block(hf_ref[:, here], hq_ref[:, here], lb, reverse)
            dov = do_ref[:, here]
            v16 = hi_ref[:, here].astype(BF16)
            do16 = dov.astype(BF16)
            qd16 = w["q_dec"].astype(BF16)
            ki16 = w["k_inv"].astype(BF16)
            ke16 = w["k_end"].astype(BF16)
            qm16 = [[jnp.where(mh > 0.0, qd16[:, ls], 0.0).astype(BF16) for mh in masks] for ls in pairs]
            dom16 = [[jnp.where(mh > 0.0, do16[:, ls], 0.0).astype(BF16) for mh in masks] for ls in pairs]
            heads = [(p, e) for p in range(PAIRS_AT_ONCE) for e in range(2)]
            dp = {(p, e): _dot_nt(dom16[p][e], v16[:, pairs[p]]) for p, e in heads}
            pm_t = {(p, e): _dot_nt(ki16[:, pairs[p]], qm16[p][e]) for p, e in heads}
            dp_t = {(p, e): _dot_nt(v16[:, pairs[p]], dom16[p][e]) for p, e in heads}
            dp = {h: jnp.where(tri, a, 0.0).astype(BF16) for h, a in dp.items()}
            pm_t = {h: jnp.where(tri_t, a, 0.0).astype(BF16) for h, a in pm_t.items()}
            dp_t = {h: jnp.where(tri_t, a, 0.0).astype(BF16) for h, a in dp_t.items()}
            dv_e = {(p, e): _dot(pm_t[p, e], do16[:, pairs[p]]) for p, e in heads}
            dq_e = {(p, e): _dot(dp[p, e], ki16[:, pairs[p]]) for p, e in heads}
            dk_e = {(p, e): _dot(dp_t[p, e], qd16[:, pairs[p]]) for p, e in heads}
            st = [[st_ref[c, first_pair + p] for p in range(PAIRS_AT_ONCE)] for c in range(nc)]
            dq_x = [[_dot(do16[rs, ls], st[c][p].astype(BF16)) for p, ls in enumerate(pairs)] for c, rs in enumerate(chunks)]
            gq = [[jnp.where(bd, _dot_tn(do16[rs, ls], qd16[rs, ls]), 0.0) for ls in pairs] for rs in chunks]
            g = [g_scr[first_pair + p] for p in range(PAIRS_AT_ONCE)]
            dk_end, dv_x, dd = [None] * nc, [None] * nc, [None] * nc
            for c in order:
                rs = chunks[c]
                g16 = [a.astype(BF16) for a in g]
                dk_end[c] = lanes([_dot(v16[rs, ls], g16[p]) for p, ls in enumerate(pairs)])
                dv_x[c] = lanes([_dot_nt(ke16[rs, ls], g16[p]) for p, ls in enumerate(pairs)])
                dd[c] = jnp.broadcast_to(lanes([jnp.sum(g[p] * st[c][p], axis=0, keepdims=True) for p in range(PAIRS_AT_ONCE)]),
                                         (CHUNK, PAIRS_AT_ONCE * 128))
                for p, ls in enumerate(pairs):
                    g[p] = g[p] * w["d"][c * CHUNK:c * CHUNK + 1, ls] + gq[c][p]
            for p in range(PAIRS_AT_ONCE):
                g_scr[first_pair + p] = g[p]

            def both_heads(d):
                return lanes([jnp.where(masks[0] > 0.0, d[p, 0], d[p, 1]) for p in range(PAIRS_AT_ONCE)])

            dq_dec = both_heads(dq_e) + jnp.concatenate([lanes(a) for a in dq_x], axis=0)
            dk_inv = both_heads(dk_e)
            dk_end = jnp.concatenate(dk_end, axis=0)
            dv = both_heads(dv_e) + jnp.concatenate(dv_x, axis=0)
            dd = jnp.concatenate(dd, axis=0)
            dke = dk_end * w["k_end"]
            dcum = dq_dec * w["q_dec"] - dk_inv * w["k_inv"] - dke
            dk = dk_inv * w["e_neg"] + dk_end * w["e_end"]
            dlast = _per_chunk(dke, lambda a: jnp.sum(a, axis=0, keepdims=True)) + dd * w["d"]
            dlf = _chunk_cumsum(dcum, not reverse) + dlast
            tt = dlf / w["f"] - dk
            dq_ref[:, here] = dq_dec * w["a"]
            dv_ref[:, here] = dv
            dz_ref[:, here] = ((1.0 - lb) * w["sig"] * w["sn"] * tt).astype(BF16)
            dlb_scr[:, here] += jnp.sum(w["sn"] * tt, axis=0, keepdims=True)

        @pl.when(jnp.logical_and(b == pl.num_programs(0) - 1, j == pl.num_programs(1) - 1))
        def _():
            d0 = dlb_scr[...] * lb_all * (1.0 - lb_all)
            dl_ref[0:1, :] = d0
            dl_ref[1:2, :] = -d0

    blk = pl.BlockSpec((rb, A_WIDTH), tmap)
    return pl.pallas_call(
        body, name=f"hgrn_bwd_{direction}", grid=(n_batch, nb),
        in_specs=[blk, blk, blk, blk, pl.BlockSpec((nc, HEAD_PAIRS, 128, 128), smap), _full((2, 2, A_WIDTH))],
        out_specs=[blk, blk, blk, _full((2, A_WIDTH))],
        out_shape=[jax.ShapeDtypeStruct((t, A_WIDTH), F32)] * 2
        + [jax.ShapeDtypeStruct((t, A_WIDTH), BF16), jax.ShapeDtypeStruct((2, A_WIDTH), F32)],
        scratch_shapes=[pltpu.VMEM((HEAD_PAIRS, 128, 128), F32), pltpu.VMEM((1, A_WIDTH), F32)],
        compiler_params=_params(2),
    )(hq, hi, hf, do, st, lbl)


def _swap_rope_halves(x):
    lane = _iota2(x.shape, 1)
    return jnp.where(lane < 32, pltpu.roll(x, 96, 1), pltpu.roll(x, 32, 1))


def _rms_fwd(xv, g):
    r = lax.rsqrt(jnp.mean(xv * xv, axis=-1, keepdims=True) + EPS)
    return (xv * r) * g


def _rms_bwd(dy, xv, g):
    r = lax.rsqrt(jnp.mean(xv * xv, axis=-1, keepdims=True) + EPS)
    xh = xv * r
    dxh = dy * g
    return r * (dxh - xh * jnp.mean(dxh * xh, axis=-1, keepdims=True)), dy * xh


def _mla_prep(cq, ckv, kr, g_qa, g_kva, w_q_p, w_kv_p, cosx, sinx, *, n_batch):
    t = cq.shape[0]
    s = t // n_batch
    tm = _row_tile(s, 1024)
    nt = s // tm

    def body(cq_ref, ckv_ref, kr_ref, gq_ref, gkv_ref, wq_ref, wkv_ref, cos_ref, sin_ref,
             q_ref, k_ref, v_ref, cqn_ref, ckvn_ref):
        cos, sin = cos_ref[...], sin_ref[...]
        cqn = _rms_fwd(cq_ref[...], gq_ref[...]).astype(BF16)
        ckvn = _rms_fwd(ckv_ref[...], gkv_ref[...]).astype(BF16)
        cqn_ref[...] = cqn
        ckvn_ref[...] = ckvn
        krv = kr_ref[...]
        kr_roped = (krv * cos + _swap_rope_halves(krv) * sin).astype(BF16)
        for h in range(B_HEADS):
            o = h * QK_PAD
            q_ref[:, o:o + 128] = _dot_nt(cqn, wq_ref[o:o + 128, :]).astype(BF16)
            qr = _dot_nt(cqn, wq_ref[o + 128:o + 256, :])
            q_ref[:, o + 128:o + 256] = (qr * cos + _swap_rope_halves(qr) * sin).astype(BF16)
            k_ref[:, o:o + 128] = _dot(ckvn, wkv_ref[:, h * 128:(h + 1) * 128]).astype(BF16)
            k_ref[:, o + 128:o + 256] = kr_roped
        v_ref[...] = _dot(ckvn, wkv_ref[:, 512:1024]).astype(BF16)

    tab = pl.BlockSpec((tm, 128), lambda i: (i % nt, 0))
    return pl.pallas_call(
        body, name="mla_prep", grid=(t // tm,),
        in_specs=[_rows(tm, Q_LORA), _rows(tm, KV_LORA), _rows(tm, 128), _full((1, Q_LORA)), _full((1, KV_LORA)),
                  _full((1024, Q_LORA)), _full((KV_LORA, 1024)), tab, tab],
        out_specs=[_rows(tm, 1024), _rows(tm, 1024), _rows(tm, 512), _rows(tm, Q_LORA), _rows(tm, KV_LORA)],
        out_shape=[jax.ShapeDtypeStruct((t, 1024), BF16), jax.ShapeDtypeStruct((t, 1024), BF16),
                   jax.ShapeDtypeStruct((t, 512), BF16), jax.ShapeDtypeStruct((t, Q_LORA), BF16),
                   jax.ShapeDtypeStruct((t, KV_LORA), BF16)],
        compiler_params=_params(1),
    )(cq, ckv, kr, g_qa, g_kva, w_q_p, w_kv_p, cosx, sinx)


def _mla_prep_bwd(dqt, dk, dv, cq, ckv, g_qa, g_kva, w_q_p, w_kv_p, cosx, sinx, *, n_batch):
    t = cq.shape[0]
    s = t // n_batch
    tm = _row_tile(s, 1024)
    nt = s // tm

    def body(dqt_ref, dk_ref, dv_ref, cq_ref, ckv_ref, gq_ref, gkv_ref, wq_ref, wkv_ref, cos_ref, sin_ref,
             dcq_ref, dckv_ref, dkr_ref, dqp_ref, dkvp_ref, dgq_ref, dgkv_ref):
        cos, sin = cos_ref[...], sin_ref[...]
        first = pl.program_id(0) == 0

        def unrope(d):
            return d * cos + _swap_rope_halves(d * sin)

        dcqn = None
        dkr = None
        dckvn = None
        for h in range(B_HEADS):
            o = h * QK_PAD
            dq_h = jnp.transpose(dqt_ref[o:o + QK_PAD, :])
            dqn16 = dq_h[:, 0:128].astype(BF16)
            dqr16 = unrope(dq_h[:, 128:256]).astype(BF16)
            dqp_ref[:, o:o + 128] = dqn16
            dqp_ref[:, o + 128:o + 256] = dqr16
            part = _dot(dqn16, wq_ref[o:o + 128, :]) + _dot(dqr16, wq_ref[o + 128:o + 256, :])
            dcqn = part if dcqn is None else dcqn + part
            dkn16 = dk_ref[:, o:o + 128].astype(BF16)
            dkvp_ref[:, h * 128:(h + 1) * 128] = dkn16
            part = _dot_nt(dkn16, wkv_ref[:, h * 128:(h + 1) * 128])
            dckvn = part if dckvn is None else dckvn + part
            kr_part = dk_ref[:, o + 128:o + 256]
            dkr = kr_part if dkr is None else dkr + kr_part
        dv16 = dv_ref[...].astype(BF16)
        dkvp_ref[:, 512:1024] = dv16
        dckvn = dckvn + _dot_nt(dv16, wkv_ref[:, 512:1024])
        dkr_ref[...] = unrope(dkr).astype(BF16)
        dcq, dgq = _rms_bwd(dcqn, cq_ref[...], gq_ref[...])
        dckv, dgkv = _rms_bwd(dckvn, ckv_ref[...], gkv_ref[...])
        dcq_ref[...] = dcq.astype(BF16)
        dckv_ref[...] = dckv.astype(BF16)
        _acc_rows(dgq_ref, dgq, first)
        _acc_rows(dgkv_ref, dgkv, first)

    tab = pl.BlockSpec((tm, 128), lambda i: (i % nt, 0))
    return pl.pallas_call(
        body, name="mla_prep_bwd", grid=(t // tm,),
        in_specs=[pl.BlockSpec((1024, tm), lambda i: (0, i)), _rows(tm, 1024), _rows(tm, 512), _rows(tm, Q_LORA),
                  _rows(tm, KV_LORA),
                  _full((1, Q_LORA)), _full((1, KV_LORA)), _full((1024, Q_LORA)), _full((KV_LORA, 1024)), tab, tab],
        out_specs=[_rows(tm, Q_LORA), _rows(tm, KV_LORA), _rows(tm, 128), _rows(tm, 1024), _rows(tm, 1024),
                   _full((1, Q_LORA)), _full((1, KV_LORA))],
        out_shape=[jax.ShapeDtypeStruct((t, Q_LORA), BF16), jax.ShapeDtypeStruct((t, KV_LORA), BF16),
                   jax.ShapeDtypeStruct((t, 128), BF16), jax.ShapeDtypeStruct((t, 1024), BF16),
                   jax.ShapeDtypeStruct((t, 1024), BF16), jax.ShapeDtypeStruct((1, Q_LORA), F32),
                   jax.ShapeDtypeStruct((1, KV_LORA), F32)],
        compiler_params=_params(1),
    )(dqt, dk, dv, cq, ckv, g_qa, g_kva, w_q_p, w_kv_p, cosx, sinx)


def _attn_dims(t, n_batch):
    s = t // n_batch
    tq = _pick(s, 1024, 128)
    return s, tq, s // tq


def _grid_ends(n_axes):
    ids = [pl.program_id(a) for a in range(n_axes)]
    first = functools.reduce(jnp.logical_and, [i == 0 for i in ids])
    last = functools.reduce(jnp.logical_and, [i == pl.num_programs(a) - 1 for a, i in enumerate(ids)])
    return first, last


def _attn_fwd(q, k, v, wsrcs, *, n_batch):
    t = q.shape[0]
    s, tq, nq = _attn_dims(t, n_batch)
    nw = len(wsrcs)

    def body(q_ref, k_ref, v_ref, *refs):
        w_refs, (o_ref, lse_ref), wall_refs = refs[:nw], refs[nw:nw + 2], refs[nw + 2:2 * nw + 2]
        send_sems, recv_sems, local_sems = refs[2 * nw + 2:]
        first, last = _grid_ends(3)

        @pl.when(first)
        def _():
            for i in range(nw):
                _gather_start(w_refs[i], wall_refs[i], send_sems, recv_sems, local_sems.at[i], 3 * i)

        @pl.when(last)
        def _():
            for i in range(nw):
                _gather_wait(w_refs[i], wall_refs[i], send_sems, recv_sems, local_sems.at[i], 3 * i)

        kv, vv = k_ref[...], v_ref[...]
        groups = _row_groups(tq)
        raw = [_dot_nt(q_ref[g, :], kv) for g in groups]
        m = [jnp.max(a, axis=-1, keepdims=True) for a in raw]
        p = [jnp.exp2((a - b) * SCALE_LOG2E) for a, b in zip(raw, m)]
        l = [jnp.sum(a, axis=-1, keepdims=True) for a in p]
        for g, pg, mg, lg in zip(groups, p, m, l):
            o_ref[g, :] = _dot(pg.astype(BF16), vv) / lg
            lse2 = mg * SCALE_LOG2E + jnp.log(lg) * LOG2E
            lse_ref[:, g] = jnp.transpose(jnp.broadcast_to(lse2, (lse2.shape[0], 128)))[0:1, :]

    return pl.pallas_call(
        body, name="attn_fwd", grid=(n_batch, B_HEADS, nq),
        in_specs=[pl.BlockSpec((tq, QK_PAD), lambda b, h, i: (b * nq + i, h)),
                  pl.BlockSpec((s, QK_PAD), lambda b, h, i: (b, h)),
                  pl.BlockSpec((s, B_V), lambda b, h, i: (b, h))] + [_HBM] * nw,
        out_specs=[pl.BlockSpec((tq, B_V), lambda b, h, i: (b * nq + i, h)),
                   pl.BlockSpec((None, 1, tq), lambda b, h, i: (h, 0, b * nq + i))] + [_HBM] * nw,
        out_shape=[jax.ShapeDtypeStruct((t, B_HEADS * B_V), F32), jax.ShapeDtypeStruct((B_HEADS, 1, t), F32)]
        + [jax.ShapeDtypeStruct((N_CHIPS,) + w.shape, w.dtype) for w in wsrcs],
        scratch_shapes=[pltpu.SemaphoreType.DMA((3 * nw,)), pltpu.SemaphoreType.DMA((3 * nw,)),
                        pltpu.SemaphoreType.DMA((nw,))],
        compiler_params=_params(3),
    )(q, k, v, *wsrcs)


KEY_PART = 512


def _attn_bwd(q, k, v, do16, lse, delta, gsegs, *, n_batch):
    t = q.shape[0]
    s = t // n_batch
    tk = _pick(s, 2 * KEY_PART, 128)
    nk = s // tk
    ng = len(gsegs)

    def body(q_ref, k_ref, v_ref, do_ref, lse_ref, dl_ref, *refs):
        g_refs, (dqt_ref, dk_ref, dv_ref), recv_refs = refs[:ng], refs[ng:ng + 3], refs[ng + 3:2 * ng + 3]
        send_sems, recv_sems = refs[2 * ng + 3:]
        first, last = _grid_ends(3)

        @pl.when(first)
        def _():
            for i in range(ng):
                _scatter_start(g_refs[i], recv_refs[i], send_sems, recv_sems, 3 * i)

        @pl.when(last)
        def _():
            for i in range(ng):
                _scatter_wait(g_refs[i], recv_refs[i], send_sems, recv_sems, 3 * i)

        j = pl.program_id(2)
        qv, dov = q_ref[...], do_ref[...]
        dst = []
        for g in [slice(r, r + min(tk, KEY_PART)) for r in range(0, tk, min(tk, KEY_PART))]:
            pt = jnp.exp2(_dot_nt(k_ref[g, :], qv) * SCALE_LOG2E - lse_ref[...])
            dv_ref[g, :] = _dot(pt.astype(BF16), dov)
            dpt = _dot_nt(v_ref[g, :], dov)
            dst.append((pt * (dpt - dl_ref[...])).astype(BF16))
            dk_ref[g, :] = _dot(dst[-1], qv) * ATTN_SCALE
        part = _dot_tn(k_ref[...], jnp.concatenate(dst, axis=0))

        @pl.when(j == 0)
        def _():
            dqt_ref[...] = part

        @pl.when(j != 0)
        def _():
            dqt_ref[...] += part

        @pl.when(j == nk - 1)
        def _():
            dqt_ref[...] = dqt_ref[...] * ATTN_SCALE

    row = pl.BlockSpec((None, 1, s), lambda b, h, j: (h, 0, b))
    return pl.pallas_call(
        body, name="attn_bwd", grid=(n_batch, B_HEADS, nk),
        in_specs=[pl.BlockSpec((s, QK_PAD), lambda b, h, j: (b, h)),
                  pl.BlockSpec((tk, QK_PAD), lambda b, h, j: (b * nk + j, h)),
                  pl.BlockSpec((tk, B_V), lambda b, h, j: (b * nk + j, h)),
                  pl.BlockSpec((s, B_V), lambda b, h, j: (b, h)), row, row] + [_HBM] * ng,
        out_specs=[pl.BlockSpec((QK_PAD, s), lambda b, h, j: (h, b)),
                   pl.BlockSpec((tk, QK_PAD), lambda b, h, j: (b * nk + j, h)),
                   pl.BlockSpec((tk, B_V), lambda b, h, j: (b * nk + j, h))] + [_HBM] * ng,
        out_shape=[jax.ShapeDtypeStruct((B_HEADS * QK_PAD, t), F32), jax.ShapeDtypeStruct((t, B_HEADS * QK_PAD), F32),
                   jax.ShapeDtypeStruct((t, B_HEADS * B_V), F32)]
        + [jax.ShapeDtypeStruct((3,) + g.shape[1:], g.dtype) for g in gsegs],
        scratch_shapes=[pltpu.SemaphoreType.DMA((3 * ng,)), pltpu.SemaphoreType.DMA((3 * ng,))],
        compiler_params=_params(3),
    )(q, k, v, do16, lse, delta, *gsegs)


def _group_ones16():
    r = _iota2((A_WIDTH, A_WIDTH), 0) // 64
    c = _iota2((A_WIDTH, A_WIDTH), 1) // 64
    return (r == c).astype(BF16)


def _head_rms(o, ones16):
    return lax.rsqrt(_exact_right(o * o, ones16) * (1.0 / 64.0) + EPS)


def _out_fwd(o_f, o_r, hg, o_attn, x, g_hn, g_mla, w_out, g2):
    t = x.shape[0]
    tm = _row_tile(t, 4 * ROW_GROUP)

    def body(of_ref, or_ref, hg_ref, oa_ref, x_ref, ghn_ref, gm_ref, w_ref, g2_ref, y_ref, x2_ref, h2_ref):
        ones16 = _group_ones16()
        ghn, gm, g2v = ghn_ref[...], gm_ref[...], g2_ref[...]

        def group(ofv, orv, hgv, oav, xv):
            o = ofv + orv
            ya16 = (((o * _head_rms(o, ones16)) * ghn) * (hgv * _sigmoid(hgv))).astype(BF16)
            yb16 = _rms_fwd(oav, gm).astype(BF16)
            x2 = xv + _dot(ya16, w_ref[0:A_WIDTH, :]) + _dot(yb16, w_ref[A_WIDTH:D_MODEL, :])
            return ya16, yb16, x2, _rms_fwd(x2, g2v).astype(BF16)

        ins = (of_ref, or_ref, hg_ref, oa_ref, x_ref)
        groups = _row_groups(tm)
        done = [group(*vals) for vals in [[ref[g, :] for ref in ins] for g in groups]]
        for g, (ya16, yb16, x2, h2) in zip(groups, done):
            y_ref[g, 0:A_WIDTH] = ya16
            y_ref[g, A_WIDTH:D_MODEL] = yb16
            x2_ref[g, :] = x2
            h2_ref[g, :] = h2

    a512 = _rows(tm, A_WIDTH)
    return pl.pallas_call(
        body, name="out_fwd", grid=(t // tm,),
        in_specs=[a512, a512, a512, a512, _rows(tm, D_MODEL), _full((1, A_WIDTH)), _full((1, A_WIDTH)),
                  _full((D_MODEL, D_MODEL)), _full((1, D_MODEL))],
        out_specs=[_rows(tm, D_MODEL)] * 3,
        out_shape=[jax.ShapeDtypeStruct((t, D_MODEL), BF16), jax.ShapeDtypeStruct((t, D_MODEL), F32),
                   jax.ShapeDtypeStruct((t, D_MODEL), BF16)],
        compiler_params=_params(1),
    )(o_f, o_r, hg, o_attn, x, g_hn, g_mla, w_out, g2)


def _out_bwd(dx2_16, o_f, o_r, hg, o_attn, g_hn, g_mla, w_out):
    t = dx2_16.shape[0]
    tm = _row_tile(t, 4 * ROW_GROUP)

    def body(dx_ref, of_ref, or_ref, hg_ref, oa_ref, ghn_ref, gm_ref, w_ref,
             do_ref, dhg_ref, doa_ref, dl_ref, dghn_ref, dgm_ref):
        first = pl.program_id(0) == 0
        ones16 = _group_ones16()
        sel16 = (_iota2((8, A_WIDTH), 0) == _iota2((8, A_WIDTH), 1) // B_V).astype(BF16)
        ghn, gm = ghn_ref[...], gm_ref[...]

        def group(dxv, ofv, orv, hgv, oav):
            dya = _dot_nt(dxv, w_ref[0:A_WIDTH, :])
            dyb = _dot_nt(dxv, w_ref[A_WIDTH:D_MODEL, :])
            o = ofv + orv
            rh = _head_rms(o, ones16)
            oh = o * rh
            sg = _sigmoid(hgv)
            sl = hgv * sg
            dhg = ((dya * (oh * ghn)) * (sg * (1.0 + hgv * (1.0 - sg)))).astype(BF16)
            doh = dya * sl * ghn
            do = rh * (doh - oh * (_exact_right(doh * oh, ones16) * (1.0 / 64.0)))
            doa, dgm = _rms_bwd(dyb, oav, gm)
            x1, x2, x3 = _split3(doa * oav)
            delta = _dot_nt(sel16, x1) + _dot_nt(sel16, x2) + _dot_nt(sel16, x3)
            return do, dhg, doa.astype(BF16), delta, dya * sl * oh, dgm

        groups = _row_groups(tm)
        loaded = [(dx_ref[g, :], of_ref[g, :], or_ref[g, :], hg_ref[g, :], oa_ref[g, :]) for g in groups]
        done = [group(*vals) for vals in loaded]
        for g, (do, dhg, doa16, delta, _, _) in zip(groups, done):
            do_ref[g, :] = do
            dhg_ref[g, :] = dhg
            doa_ref[g, :] = doa16
            for h in range(B_HEADS):
                dl_ref[h, :, g] = delta[h:h + 1, :]
        _acc_rows(dghn_ref, jnp.concatenate([d[4] for d in done], axis=0), first)
        _acc_rows(dgm_ref, jnp.concatenate([d[5] for d in done], axis=0), first)

    a512 = _rows(tm, A_WIDTH)
    return pl.pallas_call(
        body, name="out_bwd", grid=(t // tm,),
        in_specs=[_rows(tm, D_MODEL), a512, a512, a512, a512, _full((1, A_WIDTH)), _full((1, A_WIDTH)),
                  _full((D_MODEL, D_MODEL))],
        out_specs=[a512, a512, a512, pl.BlockSpec((B_HEADS, 1, tm), lambda i: (0, 0, i)),
                   _full((1, A_WIDTH)), _full((1, A_WIDTH))],
        out_shape=[jax.ShapeDtypeStruct((t, A_WIDTH), F32)] + [jax.ShapeDtypeStruct((t, A_WIDTH), BF16)] * 2
        + [jax.ShapeDtypeStruct((B_HEADS, 1, t), F32)]
        + [jax.ShapeDtypeStruct((1, A_WIDTH), F32)] * 2,
        compiler_params=_params(1),
    )(dx2_16, o_f, o_r, hg, o_attn, g_hn, g_mla, w_out)


def _ffn_fwd_bwd(h2, x2, target, w_gate, w_up, w_down, g_f, g2):
    t = x2.shape[0]
    tm = _row_tile(t)
    inv_d = 1.0 / D_MODEL

    def body(h2_ref, x2_ref, tg_ref, wg_ref, wu_ref, wd_ref, gf_ref, g2_ref,
             act_ref, dgate_ref, dup_ref, dx3_ref, dx2_ref, dx2h_ref, loss_ref, dgf_ref, dg2_ref):
        first = pl.program_id(0) == 0
        h2v = h2_ref[...]
        gate = _dot_nt(h2v, wg_ref[...])
        up = _dot_nt(h2v, wu_ref[...])
        sg = _sigmoid(gate)
        sl = gate * sg
        act16 = (sl * up).astype(BF16)
        act_ref[...] = act16
        x2v = x2_ref[...]
        x3 = x2v + _dot(act16, wd_ref[...])
        r3 = lax.rsqrt(jnp.mean(x3 * x3, axis=-1, keepdims=True) + EPS)
        x3h = x3 * r3
        gf = gf_ref[...]
        err = x3h * gf - tg_ref[...]
        part = 0.5 * jnp.sum(jnp.mean(err * err, axis=-1, keepdims=True), axis=0, keepdims=True)

        @pl.when(first)
        def _():
            loss_ref[...] = jnp.zeros_like(loss_ref)

        loss_ref[...] += part
        dy = err * inv_d
        _acc_rows(dgf_ref, dy * x3h, first)
        dx3h = dy * gf
        dx3 = r3 * (dx3h - x3h * jnp.mean(dx3h * x3h, axis=-1, keepdims=True))
        dx3_16 = dx3.astype(BF16)
        dx3_ref[...] = dx3_16
        da = _dot_nt(dx3_16, wd_ref[...])
        dup16 = (da * sl).astype(BF16)
        dgate16 = (da * up * (sg * (1.0 + gate * (1.0 - sg)))).astype(BF16)
        dup_ref[...] = dup16
        dgate_ref[...] = dgate16
        dh2 = _dot(dgate16, wg_ref[...]) + _dot(dup16, wu_ref[...])
        dx2n, dg2 = _rms_bwd(dh2, x2v, g2_ref[...])
        _acc_rows(dg2_ref, dg2, first)
        dx2 = dx3 + dx2n
        dx2_ref[...] = dx2
        dx2h_ref[...] = dx2.astype(BF16)

    return pl.pallas_call(
        body, name="ffn_fwd_bwd", grid=(t // tm,),
        in_specs=[_rows(tm, D_MODEL), _rows(tm, D_MODEL), _rows(tm, D_MODEL), _full((D_FF, D_MODEL), True),
                  _full((D_FF, D_MODEL), True), _full((D_FF, D_MODEL), True), _full((1, D_MODEL)), _full((1, D_MODEL))],
        out_specs=[_rows(tm, D_FF), _rows(tm, D_FF), _rows(tm, D_FF), _rows(tm, D_MODEL), _rows(tm, D_MODEL),
                   _rows(tm, D_MODEL), _full((8, 128)), _full((1, D_MODEL)), _full((1, D_MODEL))],
        out_shape=[jax.ShapeDtypeStruct((t, D_FF), BF16)] * 3
        + [jax.ShapeDtypeStruct((t, D_MODEL), BF16), jax.ShapeDtypeStruct((t, D_MODEL), F32),
           jax.ShapeDtypeStruct((t, D_MODEL), BF16), jax.ShapeDtypeStruct((8, 128), F32),
           jax.ShapeDtypeStruct((1, D_MODEL), F32), jax.ShapeDtypeStruct((1, D_MODEL), F32)],
        compiler_params=_params(1),
    )(h2, x2, target, w_gate, w_up, w_down, g_f, g2)


def _wgrad(a, b, name):
    t, m = a.shape
    n = b.shape[1]
    bm = _pick(m, 1664)
    bn = _pick(n, 1664)
    tk = _pick(t, 2048, 16)
    nk = t // tk

    def body(a_ref, b_ref, o_ref, o16_ref):
        k = pl.program_id(2)
        part = _dot_tn(a_ref[...], b_ref[...])

        @pl.when(k == 0)
        def _():
            o_ref[...] = part

        @pl.when(k != 0)
        def _():
            o_ref[...] += part

        @pl.when(k == nk - 1)
        def _():
            o16_ref[...] = o_ref[...].astype(BF16)

    out = pl.BlockSpec((bm, bn), lambda i, j, k: (i, j))
    return pl.pallas_call(
        body, name=name, grid=(m // bm, n // bn, nk),
        in_specs=[pl.BlockSpec((tk, bm), lambda i, j, k: (k, i)), pl.BlockSpec((tk, bn), lambda i, j, k: (k, j))],
        out_specs=[out, out],
        out_shape=[jax.ShapeDtypeStruct((m, n), F32), jax.ShapeDtypeStruct((m, n), BF16)],
        compiler_params=_params(3),
    )(a, b)


def _rope_tables(seq):
    inv = 1.0 / (ROPE_THETA ** (jnp.arange(0, B_ROPE, 2, dtype=F32) / B_ROPE))
    ang = jnp.arange(seq, dtype=F32)[:, None] * inv[None, :]
    cos, sin = jnp.cos(ang), jnp.sin(ang)
    zeros = jnp.zeros((seq, 64), F32)
    return jnp.concatenate([cos, cos, zeros], axis=1), jnp.concatenate([-sin, sin, zeros], axis=1)


def _pad_weights(w_in_t, w_q_t, w_kv_b):
    w_in_p = jnp.pad(w_in_t, ((0, D_IN_PAD - D_IN), (0, 0)))
    w_q_p = jnp.pad(w_q_t.reshape(B_HEADS, B_NOPE + B_ROPE, Q_LORA), ((0, 0), (0, 64), (0, 0))).reshape(1024, Q_LORA)
    kv = w_kv_b.reshape(KV_LORA, B_HEADS, B_NOPE + B_V)
    w_kv_p = jnp.concatenate([kv[:, :, :B_NOPE].reshape(KV_LORA, 512), kv[:, :, B_NOPE:].reshape(KV_LORA, 512)], axis=1)
    return w_in_p, w_q_p, w_kv_p


def _unpad_kv(g_kv_p):
    return jnp.concatenate([g_kv_p[:, :512].reshape(KV_LORA, B_HEADS, B_NOPE),
                            g_kv_p[:, 512:].reshape(KV_LORA, B_HEADS, B_V)], axis=2).reshape(KV_LORA, 1024)


_OWN_ROWS = dict(w_in=(272, 3), w_q_b=(64, 4), w_kv_b=(256, 0), w_out=(256, 1), w_gate=(176, 4), w_up=(176, 4), w_down=(176, 4))


def _local_step(x, target, lbl, g1, g_hn, g_qa, g_kva, g_mla, g2, g_f, w_in, w_q_b, w_kv_b, late_shards):
    n_batch, seq, _ = x.shape
    t = n_batch * seq
    x = x.reshape(t, D_MODEL)
    target = target.reshape(t, D_MODEL)
    w_in_p, w_q_p, w_kv_p = _pad_weights(w_in, w_q_b, w_kv_b)
    cosx, sinx = _rope_tables(seq)

    h1, hq, hi, hff, hfb, hg, cq, ckv, kr = _in_fwd(x, g1, w_in_p)
    o_f, st_f = _hgrn_fwd(hq, hi, hff, lbl, n_batch=n_batch, direction=0)
    o_r, st_r = _hgrn_fwd(hq, hi, hfb, lbl, n_batch=n_batch, direction=1)
    q, k, v, cqn, ckvn = _mla_prep(cq, ckv, kr, g_qa, g_kva, w_q_p, w_kv_p, cosx, sinx, n_batch=n_batch)
    o_attn, lse, *gathered = _attn_fwd(q, k, v, late_shards, n_batch=n_batch)
    w_out, w_gate, w_up, w_down = (_join_shards(n, a) for n, a in zip(_LATE, gathered))
    ycat, x2, h2 = _out_fwd(o_f, o_r, hg, o_attn, x, g_hn, g_mla, w_out, g2)
    act, dgate, dup, dx3_16, dx2, dx2_16, loss, dg_f, dg2 = _ffn_fwd_bwd(h2, x2, target, w_gate, w_up, w_down, g_f, g2)
    full = dict(w_out=_wgrad(ycat, dx2_16, "wgrad_out"), w_gate=_wgrad(dgate, h2, "wgrad_gate"),
                w_up=_wgrad(dup, h2, "wgrad_up"), w_down=_wgrad(act, dx3_16, "wgrad_down"))
    do_h, dhg, do_attn16, delta, dg_hn, dg_mla = _out_bwd(dx2_16, o_f, o_r, hg, o_attn, g_hn, g_mla, w_out)
    dqt, dk, dv, *recv_late = _attn_bwd(q, k, v, do_attn16, lse, delta,
                                        [_segments(n, full[n][1]) for n in _LATE], n_batch=n_batch)
    dcq, dckv, dkr, dqp16, dkvp16, dg_qa, dg_kva = _mla_prep_bwd(dqt, dk, dv, cq, ckv, g_qa, g_kva, w_q_p, w_kv_p,
                                                                  cosx, sinx, n_batch=n_batch)
    dq_f, dv_f, dz_f, dl_f = _hgrn_bwd(hq, hi, hff, do_h, st_f, lbl, n_batch=n_batch, direction=0)
    dq_r, dv_r, dz_r, dl_r = _hgrn_bwd(hq, hi, hfb, do_h, st_r, lbl, n_batch=n_batch, direction=1)
    dx, dproj16, dg1 = _in_bwd(dq_f, dq_r, dv_f, dv_r, dz_f, dz_r, dhg, hq, dcq, dckv, dkr, dx2, x, g1, w_in_p)

    full.update(w_in=_wgrad(dproj16, h1, "wgrad_in"), w_q_b=_wgrad(dqp16, cqn, "wgrad_q_b"))
    g_kv = _unpad_kv(_wgrad(ckvn, dkvp16, "wgrad_kv_b")[0])
    early16 = dict(w_in=full["w_in"][1][:D_IN].reshape(N_CHIPS, D_IN // N_CHIPS, D_MODEL),
                   w_q_b=full["w_q_b"][1].reshape(B_HEADS, QK_PAD, Q_LORA)[:, :B_NOPE + B_ROPE],
                   w_kv_b=_segments("w_kv_b", g_kv).astype(BF16))
    small = dict(norm1_g=dg1, lb_logits=jnp.stack([dl_f, dl_r]), hgrn_norm_g=dg_hn, q_a_norm_g=dg_qa, kv_a_norm_g=dg_kva,
                 mla_norm_g=dg_mla, norm2_g=dg2, final_norm_g=dg_f, loss=loss[0:1, 0])
    return dx.reshape(n_batch, seq, D_MODEL), small, {**{n: f for n, (f, _) in full.items()}, "w_kv_b": g_kv}, early16, \
        dict(zip(_LATE, recv_late))


_HBM = pl.BlockSpec(memory_space=pltpu.HBM)
_MESH = pl.DeviceIdType.MESH


def _place():
    x, y, c = lax.axis_index("x"), lax.axis_index("y"), lax.axis_index("c")
    other_chips = [(1 - x, y), (x, 1 - y), (1 - x, 1 - y)]
    return x, y, c, other_chips


def _gather_copies(w_ref, wall_ref, send_sems, recv_sems, local_sem, base=0):
    x, y, c, chips = _place()
    mine = 2 * x + y

    def mk(j, chip_index, to):
        return pltpu.make_async_remote_copy(src_ref=w_ref, dst_ref=wall_ref.at[chip_index], send_sem=send_sems.at[base + j],
                                            recv_sem=recv_sems.at[base + j], device_id=to, device_id_type=_MESH)

    local = pltpu.make_async_copy(w_ref, wall_ref.at[mine], local_sem)
    sends = [mk(j, mine, (*chip, c)) for j, chip in enumerate(chips)]
    recvs = [mk(j, 2 * px + py, (x, y, c)) for j, (px, py) in enumerate(chips)]
    return local, sends, recvs


def _gather_start(*refs):
    local, sends, _ = _gather_copies(*refs)
    local.start()
    for cp in sends:
        cp.start()


def _gather_wait(*refs):
    local, sends, recvs = _gather_copies(*refs)
    for cp in recvs:
        cp.wait_recv()
    for cp in sends:
        cp.wait_send()
    local.wait()


def _scatter_copies(g_ref, recv_ref, send_sems, recv_sems, base=0):
    x, y, c, chips = _place()

    def mk(j, src_index, to):
        return pltpu.make_async_remote_copy(src_ref=g_ref.at[src_index], dst_ref=recv_ref.at[j],
                                            send_sem=send_sems.at[base + j], recv_sem=recv_sems.at[base + j],
                                            device_id=to, device_id_type=_MESH)

    sends = [mk(j, 2 * px + py, (px, py, c)) for j, (px, py) in enumerate(chips)]
    recvs = [mk(j, 0, (x, y, c)) for j in range(3)]
    return sends, recvs


def _scatter_start(*refs):
    for cp in _scatter_copies(*refs)[0]:
        cp.start()


def _scatter_wait(*refs):
    sends, recvs = _scatter_copies(*refs)
    for cp in recvs:
        cp.wait_recv()
    for cp in sends:
        cp.wait_send()


def _half(ref, which, axis):
    n = ref.shape[axis] // 2
    idx = [slice(None)] * len(ref.shape)
    idx[axis] = pl.ds(which * n, n)
    return ref.at[tuple(idx)]


def _split_axis(a):
    return 1 if a.shape[1] % 256 == 0 else 0


def _gather_halves(src_ref, dst_ref, send_sems, recv_sems, local_sem, base, axis):
    x, y, c, chips = _place()
    mine = 2 * x + y

    def remote(src, dst, n, to):
        return pltpu.make_async_remote_copy(src_ref=src, dst_ref=dst, send_sem=send_sems.at[base + n],
                                            recv_sem=recv_sems.at[base + n], device_id=to, device_id_type=_MESH)

    def slot(chip_index, which):
        return _half(dst_ref.at[chip_index], which, axis)

    local = pltpu.make_async_copy(src_ref, dst_ref.at[mine], local_sem)
    first = [remote(_half(src_ref, c, axis), slot(mine, c), j, (*chip, c)) for j, chip in enumerate(chips)]
    landed = [remote(_half(src_ref, c, axis), slot(2 * px + py, c), j, (x, y, c)) for j, (px, py) in enumerate(chips)]
    passed = [remote(slot(2 * px + py, c), slot(2 * px + py, c), 3 + j, (x, y, 1 - c)) for j, (px, py) in enumerate(chips)]
    handed = [remote(_half(src_ref, c, axis), slot(2 * px + py, 1 - c), 3 + j, (x, y, c)) for j, (px, py) in enumerate(chips)]

    def start():
        local.start()
        for cp in first:
            cp.start()

    def pass_on():
        for arrived, onward in zip(landed, passed):
            arrived.wait_recv()
            onward.start()

    def finish():
        for cp in handed:
            cp.wait_recv()
        for cp in first + passed:
            cp.wait_send()
        local.wait()

    return start, pass_on, finish


def _gather_shards(srcs, small):
    n = len(srcs)

    def body(*refs):
        src_refs, s_ref, dst_refs, sall_ref = refs[:n], refs[n], refs[n + 1:2 * n + 1], refs[2 * n + 1]
        send_sems, recv_sems, local_sems = refs[2 * n + 2:]
        _gather_start(s_ref, sall_ref, send_sems, recv_sems, local_sems.at[n], 6 * n)
        steps = [_gather_halves(src_refs[i], dst_refs[i], send_sems, recv_sems, local_sems.at[i], 6 * i, _split_axis(srcs[i]))
                 for i in range(n)]
        for phase in range(3):
            for step in steps:
                step[phase]()
        _gather_wait(s_ref, sall_ref, send_sems, recv_sems, local_sems.at[n], 6 * n)

    return pl.pallas_call(
        body, name="gather_shards", in_specs=[_HBM] * (n + 1), out_specs=[_HBM] * (n + 1),
        out_shape=[jax.ShapeDtypeStruct((N_CHIPS,) + a.shape, a.dtype) for a in list(srcs) + [small]],
        scratch_shapes=[pltpu.SemaphoreType.DMA((6 * n + 3,)), pltpu.SemaphoreType.DMA((6 * n + 3,)),
                        pltpu.SemaphoreType.DMA((n + 1,))],
    )(*srcs, small)


_SEM = pl.BlockSpec(memory_space=pltpu.SEMAPHORE)
_SIDE_EFFECT = pltpu.SideEffectType.DATAFLOW_SIDE_EFFECTING


def _scatter_begin(gsegs):
    n = len(gsegs)
    def hbm(a):
        return pltpu.with_memory_space_constraint(a, pltpu.HBM)

    lands = [hbm(lax.empty((3,) + g.shape[1:], g.dtype)) for g in gsegs]

    def body(*refs):
        g_refs, land_refs, send_sems, recv_sems, token = refs[:n], refs[n:2 * n], refs[2 * n], refs[2 * n + 1], refs[-1]
        for i in range(n):
            _scatter_start(g_refs[i], land_refs[i], send_sems, recv_sems, 3 * i)
        token[...] = jnp.zeros_like(token)

    out = pl.pallas_call(
        body, name="scatter_begin", in_specs=[_HBM] * (2 * n),
        out_specs=(_SEM, _SEM) + (_HBM,) * (2 * n) + (pl.BlockSpec(memory_space=pltpu.VMEM),),
        out_shape=(pltpu.SemaphoreType.DMA((3 * n,)), pltpu.SemaphoreType.DMA((3 * n,)))
        + tuple(pltpu.HBM(a.shape, a.dtype) for a in list(gsegs) + lands) + (jax.ShapeDtypeStruct((8, 128), F32),),
        input_output_aliases={i: 2 + i for i in range(2 * n)},
        compiler_params=pltpu.CompilerParams(has_side_effects=_SIDE_EFFECT),
    )(*[hbm(g) for g in gsegs], *lands)
    return out[0], out[1], out[2:2 + n], out[2 + n:2 + 2 * n], out[-1]


def _scatter_end(send_sems, recv_sems, gsegs, lands, after):
    n = len(gsegs)

    def body(*refs):
        g_refs, land_refs, ssems, rsems = refs[:n], refs[n:2 * n], refs[2 * n], refs[2 * n + 1]
        for i in range(n):
            _scatter_wait(g_refs[i], land_refs[i], ssems, rsems, 3 * i)

    out = pl.pallas_call(
        body, name="scatter_end", in_specs=[_HBM] * (2 * n) + [_SEM, _SEM] + [pl.BlockSpec(memory_space=pl.ANY)] * len(after),
        out_specs=(_HBM,) * (2 * n), out_shape=tuple(pltpu.HBM(a.shape, a.dtype) for a in list(gsegs) + list(lands)),
        input_output_aliases={i: i for i in range(2 * n)},
        compiler_params=pltpu.CompilerParams(has_side_effects=_SIDE_EFFECT),
    )(*gsegs, *lands, send_sems, recv_sems, *after)
    return out[n:]


def _swap_sibling(parts, small=None):
    n = len(parts)
    extra = small is not None

    def body(*refs):
        ins, outs = refs[:n + extra], refs[n + extra:2 * (n + extra)]
        send_sems, recv_sems, local_sem = refs[2 * (n + extra):]
        x, y, c, _ = _place()
        cps = [pltpu.make_async_remote_copy(src_ref=ins[i], dst_ref=outs[i], send_sem=send_sems.at[i],
                                            recv_sem=recv_sems.at[i], device_id=(x, y, 1 - c), device_id_type=_MESH)
               for i in range(n)]
        for cp in cps:
            cp.start()
        if extra:
            s_ref, sall_ref = ins[n], outs[n]
            me = 4 * x + 2 * y + c
            flips = [(fx, fy, fc) for fx in (0, 1) for fy in (0, 1) for fc in (0, 1)][1:]

            def peer(f):
                return tuple((1 - a) if b else a for a, b in zip((x, y, c), f))

            def sm(r, index, to):
                return pltpu.make_async_remote_copy(src_ref=s_ref, dst_ref=sall_ref.at[index], send_sem=send_sems.at[n + r],
                                                    recv_sem=recv_sems.at[n + r], device_id=to, device_id_type=_MESH)

            local = pltpu.make_async_copy(s_ref, sall_ref.at[me], local_sem)
            local.start()
            sends = [sm(r, me, peer(f)) for r, f in enumerate(flips)]
            for cp in sends:
                cp.start()
            for r, f in enumerate(flips):
                px, py, pc = peer(f)
                sm(r, 4 * px + 2 * py + pc, (x, y, c)).wait_recv()
            for cp in sends:
                cp.wait_send()
            local.wait()
        for cp in cps:
            cp.wait()

    return pl.pallas_call(
        body, name="swap_small" if extra else "swap_sibling", in_specs=[_HBM] * (n + extra), out_specs=[_HBM] * (n + extra),
        out_shape=[jax.ShapeDtypeStruct(p.shape, p.dtype) for p in parts]
        + ([jax.ShapeDtypeStruct((8,) + small.shape, F32)] if extra else []),
        scratch_shapes=[pltpu.SemaphoreType.DMA((n + 7,)), pltpu.SemaphoreType.DMA((n + 7,)), pltpu.SemaphoreType.DMA(())],
    )(*parts, *([small] if extra else []))


def _sum_segments(chip, full, recv, tm, pitch, name, after=()):
    _, rows, cols = recv.shape

    def body(chip_ref, o_ref, r_ref, *rest):
        acc = o_ref[...]
        for j in range(3):
            acc = acc + r_ref[j].astype(F32)
        rest[-1][...] = acc

    return pl.pallas_call(
        body, name=name,
        grid_spec=pltpu.PrefetchScalarGridSpec(
            num_scalar_prefetch=1, grid=(rows // tm,),
            in_specs=[pl.BlockSpec((tm, cols), lambda i, c: (c[0] * pitch + i, 0)),
                      pl.BlockSpec((3, tm, cols), lambda i, c: (0, i, 0))]
            + [pl.BlockSpec(a.shape, lambda i, c, nd=a.ndim: (0,) * nd) for a in after],
            out_specs=pl.BlockSpec((tm, cols), lambda i, c: (i, 0))),
        out_shape=jax.ShapeDtypeStruct((rows, cols), F32), compiler_params=_params(1),
    )(chip, full, recv, *after)


def _sum_devices(sall):
    def body(s_ref, o_ref):
        acc = s_ref[0]
        for d in range(1, 8):
            acc = acc + s_ref[d]
        o_ref[...] = acc

    return pl.pallas_call(body, name="sum_devices", out_shape=jax.ShapeDtypeStruct(sall.shape[1:], F32))(sall)


def _adamw(w, m, v, ga, gb, name):
    rows, cols = w.shape
    tm = _row_tile(rows)
    two = gb is not None

    def body(*refs):
        w_ref, m_ref, v_ref, ga_ref = refs[:4]
        g_ref, d_ref, m2_ref, v2_ref = refs[-4:]
        g = ga_ref[...] + refs[4][...] if two else ga_ref[...]
        m2 = ADAM_B1 * m_ref[...] + (1.0 - ADAM_B1) * g
        v2 = ADAM_B2 * v_ref[...] + (1.0 - ADAM_B2) * (g * g)
        m_hat = m2 / (1.0 - ADAM_B1 ** ADAM_STEP)
        v_hat = v2 / (1.0 - ADAM_B2 ** ADAM_STEP)
        g_ref[...] = g
        d_ref[...] = -ADAM_LR * (m_hat / (jnp.sqrt(v_hat) + ADAM_EPS) + ADAM_WD * w_ref[...])
        m2_ref[...] = m2
        v2_ref[...] = v2

    blk = _rows(tm, cols)
    args = (w, m, v, ga) + ((gb,) if two else ())
    return pl.pallas_call(
        body, name=name, grid=(rows // tm,), in_specs=[blk] * len(args), out_specs=[blk] * 4,
        out_shape=[jax.ShapeDtypeStruct(w.shape, F32)] * 4, compiler_params=_params(1),
    )(*args)


_TRANSPOSED = ("w_in", "w_q_b", "w_gate", "w_up")
_COLUMN_SHARDED = ("w_kv_b",)
_FULL_SHAPES = dict(w_in=(D_IN, D_MODEL), w_q_b=(768, Q_LORA), w_kv_b=(KV_LORA, 1024), w_out=(D_MODEL, D_MODEL),
                    w_gate=(D_FF, D_MODEL), w_up=(D_FF, D_MODEL), w_down=(D_FF, D_MODEL))
_SMALL = (("norm1_g", 1024), ("lb_logits", 2048), ("hgrn_norm_g", 512), ("q_a_norm_g", 384), ("kv_a_norm_g", 256),
          ("mla_norm_g", 512), ("norm2_g", 1024), ("final_norm_g", 1024))
_UPDATE_ROWS = 48


def _pad_rows(a, rows):
    return jnp.pad(a, ((0, rows - a.shape[0]), (0, 0)))


_EARLY = ("w_in", "w_q_b", "w_kv_b")
_LATE = ("w_out", "w_gate", "w_up", "w_down")


def _segments(name, g):
    r, c = g.shape
    if name in _COLUMN_SHARDED:
        return g.reshape(r, N_CHIPS, c // N_CHIPS).transpose(1, 0, 2)
    return g.reshape(N_CHIPS, r // N_CHIPS, c)


def _own_segment(name, g, chip):
    r, c = g.shape
    if name in _COLUMN_SHARDED:
        return lax.dynamic_slice_in_dim(g, chip * (c // N_CHIPS), c // N_CHIPS, axis=1)
    return lax.dynamic_slice_in_dim(g, chip * (r // N_CHIPS), r // N_CHIPS, axis=0)


def _join_shards(name, seg):
    r, c = _FULL_SHAPES[name]
    if name in _COLUMN_SHARDED:
        return seg.transpose(1, 0, 2).reshape(r, c)
    return seg.reshape(r, c)


def kernel(x, norm1_g, w_in, lb_logits, hgrn_norm_g, q_a_norm_g, w_q_b, kv_a_norm_g, w_kv_b, mla_norm_g, w_out, norm2_g, w_gate, w_up, w_down, final_norm_g, loss_target, m_norm1_g, m_w_in, m_lb_logits, m_hgrn_norm_g, m_q_a_norm_g, m_w_q_b, m_kv_a_norm_g, m_w_kv_b, m_mla_norm_g, m_w_out, m_norm2_g, m_w_gate, m_w_up, m_w_down, m_final_norm_g, v_norm1_g, v_w_in, v_lb_logits, v_hgrn_norm_g, v_q_a_norm_g, v_w_q_b, v_kv_a_norm_g, v_w_kv_b, v_mla_norm_g, v_w_out, v_norm2_g, v_w_gate, v_w_up, v_w_down, v_final_norm_g):
    names = ("norm1_g", "w_in", "lb_logits", "hgrn_norm_g", "q_a_norm_g", "w_q_b", "kv_a_norm_g", "w_kv_b", "mla_norm_g",
             "w_out", "norm2_g", "w_gate", "w_up", "w_down", "final_norm_g")
    w = dict(zip(names, (norm1_g, w_in, lb_logits, hgrn_norm_g, q_a_norm_g, w_q_b, kv_a_norm_g, w_kv_b, mla_norm_g,
                         w_out, norm2_g, w_gate, w_up, w_down, final_norm_g)))
    m = dict(zip(names, (m_norm1_g, m_w_in, m_lb_logits, m_hgrn_norm_g, m_q_a_norm_g, m_w_q_b, m_kv_a_norm_g, m_w_kv_b,
                         m_mla_norm_g, m_w_out, m_norm2_g, m_w_gate, m_w_up, m_w_down, m_final_norm_g)))
    v = dict(zip(names, (v_norm1_g, v_w_in, v_lb_logits, v_hgrn_norm_g, v_q_a_norm_g, v_w_q_b, v_kv_a_norm_g, v_w_kv_b,
                         v_mla_norm_g, v_w_out, v_norm2_g, v_w_gate, v_w_up, v_w_down, v_final_norm_g)))
    matrices = _EARLY + _LATE
    chip = 2 * lax.axis_index("x") + lax.axis_index("y")

    def shard2d(a, n):
        return jnp.swapaxes(a[0], 0, 1) if n in _TRANSPOSED else a[0]

    def unshard2d(a, n):
        return (jnp.swapaxes(a, 0, 1) if n in _TRANSPOSED else a)[None]

    w16 = {n: shard2d(w[n], n).astype(BF16) for n in matrices}
    lb8 = _pad_rows(lb_logits.reshape(4, 128), 8)
    *early, lball = _gather_shards([w16[n] for n in _EARLY], lb8)
    lbl = lball[:, :4].reshape(N_CHIPS, 2, 2, 128).transpose(1, 2, 0, 3).reshape(2, 2, A_WIDTH)

    grad_x, g, full, early16, recv = _local_step(
        x, loss_target, lbl, norm1_g, hgrn_norm_g, q_a_norm_g, kv_a_norm_g, mla_norm_g, norm2_g, final_norm_g[None, :],
        *(_join_shards(n, a) for n, a in zip(_EARLY, early)), [w16[n] for n in _LATE])

    n_small = sum(size for _, size in _SMALL)
    small = jnp.concatenate([g[n].reshape(-1) for n, _ in _SMALL] + [g["loss"], jnp.zeros((SMALL_ROWS * 128 - n_small - 1,), F32)])
    chip1 = chip.reshape(1).astype(jnp.int32)
    out, raw = {}, {}

    def update(n, part, sib):
        raw[n] = _adamw(shard2d(w[n], n), shard2d(m[n], n), shard2d(v[n], n), part, sib, f"adamw_{n}")
        out[n] = tuple(unshard2d(r, n) for r in raw[n])

    send_sems, recv_sems, segs, lands, token = _scatter_begin([early16[n] for n in _EARLY])
    parts = [_sum_segments(chip1, full[n], recv[n], *_OWN_ROWS[n], f"sum_{n}", after=(token,)) for n in _LATE]
    for n, part, sib in zip(_LATE, parts, _swap_sibling(parts)):
        update(n, part, sib)
    recv_early = _scatter_end(send_sems, recv_sems, segs, lands, after=[raw[n][1] for n in _LATE])
    full["w_kv_b"] = _own_segment("w_kv_b", full["w_kv_b"], chip)
    parts = [_sum_segments(chip1, full[n], r, *_OWN_ROWS[n], f"sum_{n}") for n, r in zip(_EARLY, recv_early)]
    *sibs, small_all = _swap_sibling(parts, small.reshape(SMALL_ROWS, 128))
    for n, part, sib in zip(_EARLY, parts, sibs):
        update(n, part, sib)
    small_sum = _sum_devices(small_all).reshape(-1)
    loss = small_sum[n_small]

    small_g, off = {}, 0
    for n, size in _SMALL:
        small_g[n] = small_sum[off:off + size]
        off += size
    small_g["lb_logits"] = lax.dynamic_slice_in_dim(small_g["lb_logits"].reshape(2, 2, A_WIDTH), chip * 128, 128, axis=2)
    small_names = tuple(n for n, _ in _SMALL)

    def pack(d):
        return _pad_rows(jnp.concatenate([d[n].reshape(-1) for n in small_names]).reshape(-1, 128), _UPDATE_ROWS)

    res = _adamw(pack(w), pack(m), pack(v), pack(small_g), None, "adamw_small")
    off = 0
    flat = [r.reshape(-1) for r in res]
    for n in small_names:
        size = w[n].size
        out[n] = tuple(f[off:off + size].reshape(w[n].shape) for f in flat)
        off += size

    return (loss, grad_x) + tuple(out[n][i] for i in range(4) for n in names)
```

```python
import functools

import jax
import jax.numpy as jnp
from jax import lax
from jax.experimental import pallas as pl
from jax.experimental.pallas import tpu as pltpu

F32 = jnp.float32
BF16 = jnp.bfloat16

D_MODEL = 1024
A_WIDTH = 512
HEAD_PAIRS = 4
PAIRS_AT_ONCE = 2
CHUNK = 64
B_HEADS = 4
B_NOPE = 128
B_ROPE = 64
B_V = 128
QK_PAD = 256
Q_LORA = 384
KV_LORA = 256
D_FF = 2816
D_IN = 3264
D_IN_PAD = 3328
IN_WIDTHS = (512, 512, 512, 512, 512, Q_LORA, KV_LORA, 128)
ROPE_THETA = 10000.0
EPS = 1e-6
ATTN_SCALE = (B_NOPE + B_ROPE) ** -0.5
LOG2E = 1.4426950408889634
SCALE_LOG2E = ATTN_SCALE * LOG2E

ADAM_LR = 0.001
ADAM_B1 = 0.9
ADAM_B2 = 0.999
ADAM_EPS = 1e-08
ADAM_WD = 0.01
ADAM_STEP = 10

VMEM_LIMIT_BYTES = 60 * 1024 * 1024
N_CHIPS = 4
SMALL_ROWS = 56


def _params(n_axes):
    return pltpu.CompilerParams(dimension_semantics=("arbitrary",) * n_axes,
                                vmem_limit_bytes=VMEM_LIMIT_BYTES)


def _dot(a, b):
    return jnp.dot(a, b, preferred_element_type=F32)


def _dot_nt(a, b):
    return lax.dot_general(a, b, (((1,), (1,)), ((), ())), preferred_element_type=F32)


def _dot_tn(a, b):
    return lax.dot_general(a, b, (((0,), (0,)), ((), ())), preferred_element_type=F32)


def _split3(x):
    x1 = x.astype(BF16)
    r = x - x1.astype(F32)
    x2 = r.astype(BF16)
    x3 = (r - x2.astype(F32)).astype(BF16)
    return x1, x2, x3


def _exact_right(x, m16):
    x1, x2, x3 = _split3(x)
    return _dot(x1, m16) + _dot(x2, m16) + _dot(x3, m16)


def _iota2(shape, dim):
    return lax.broadcasted_iota(jnp.int32, shape, dim)


def _sigmoid(x):
    return jax.nn.sigmoid(x)


def _pick(dim, cap, mult=128):
    if dim <= cap:
        return dim
    best = None
    for d in range(mult, cap + 1, mult):
        if dim % d == 0:
            best = d
    assert best is not None, (dim, cap, mult)
    return best


def _row_tile(t, cap=256):
    return _pick(t, cap, 8)


ROW_GROUP = 256


def _row_groups(tm):
    size = min(tm, ROW_GROUP)
    return [slice(r, r + size) for r in range(0, tm, size)]


def _full(shape, single=False):
    if single:
        return pl.BlockSpec(shape, lambda *_: (0,) * len(shape), pipeline_mode=pl.Buffered(1))
    return pl.BlockSpec(shape, lambda *_: (0,) * len(shape))


def _rows(tm, width):
    return pl.BlockSpec((tm, width), lambda i: (i, 0))


def _acc_rows(ref, val, first):
    s = jnp.sum(val, axis=0, keepdims=True)

    @pl.when(first)
    def _():
        ref[...] = s

    @pl.when(jnp.logical_not(first))
    def _():
        ref[...] += s


def _in_fwd(x, g1, w_in_p):
    t = x.shape[0]
    tm = _row_tile(t, 2 * ROW_GROUP)

    def body(x_ref, g_ref, w_ref, h_ref, *outs):
        gain = g_ref[...]

        def group(xv):
            r = lax.rsqrt(jnp.mean(xv * xv, axis=-1, keepdims=True) + EPS)
            h = ((xv * r) * gain).astype(BF16)
            proj, off = [], 0
            for w in IN_WIDTHS:
                proj.append(_dot_nt(h, w_ref[off:off + w, :]))
                off += w
            return h, proj

        groups = _row_groups(tm)
        done = [group(xv) for xv in [x_ref[g, :] for g in groups]]
        for g, (h, proj) in zip(groups, done):
            h_ref[g, :] = h
            for o_ref, p in zip(outs, proj):
                o_ref[g, :] = p

    return pl.pallas_call(
        body, name="in_fwd", grid=(t // tm,),
        in_specs=[_rows(tm, D_MODEL), _full((1, D_MODEL)), _full((D_IN_PAD, D_MODEL))],
        out_specs=[_rows(tm, D_MODEL)] + [_rows(tm, w) for w in IN_WIDTHS],
        out_shape=[jax.ShapeDtypeStruct((t, D_MODEL), BF16)]
        + [jax.ShapeDtypeStruct((t, w), F32) for w in IN_WIDTHS],
        compiler_params=_params(1),
    )(x, g1, w_in_p)


def _in_bwd(dq_f, dq_r, dv_f, dv_r, dz_f, dz_r, dhg, hq, dcq, dckv, dkr, dx2, x, g1, w_in_p):
    t = x.shape[0]
    tm = _row_tile(t, 2 * ROW_GROUP)

    def body(dqf_ref, dqr_ref, dvf_ref, dvr_ref, dzf_ref, dzr_ref, dhg_ref, hq_ref, dcq_ref, dckv_ref,
             dkr_ref, dx2_ref, x_ref, g_ref, w_ref, dx_ref, dp_ref, dg_ref):
        gain = g_ref[...]

        def group(dqf, dqr, dvf, dvr, dzf, dzr, dhg, hqv, dcq, dckv, dkr, dx2, xv):
            sg = _sigmoid(hqv)
            dhq = (dqf + dqr) * (sg * (1.0 + hqv * (1.0 - sg)))
            pieces = [p.astype(BF16) for p in (dhq, dvf + dvr, dzf, dzr, dhg, dcq, dckv, dkr)]
            dh = _dot(jnp.concatenate(pieces, axis=1), w_ref[...])
            r = lax.rsqrt(jnp.mean(xv * xv, axis=-1, keepdims=True) + EPS)
            xh = xv * r
            dxh = dh * gain
            return dx2 + r * (dxh - xh * jnp.mean(dxh * xh, axis=-1, keepdims=True)), pieces, dh * xh

        ins = (dqf_ref, dqr_ref, dvf_ref, dvr_ref, dzf_ref, dzr_ref, dhg_ref, hq_ref, dcq_ref, dckv_ref, dkr_ref,
               dx2_ref, x_ref)
        groups = _row_groups(tm)
        done = [group(*vals) for vals in [[ref[g, :] for ref in ins] for g in groups]]
        for g, (dx, pieces, _) in zip(groups, done):
            dx_ref[g, :] = dx
            off = 0
            for p16, w in zip(pieces, IN_WIDTHS):
                dp_ref[g, off:off + w] = p16
                off += w
        _acc_rows(dg_ref, jnp.concatenate([d[2] for d in done], axis=0), pl.program_id(0) == 0)

    a512 = _rows(tm, A_WIDTH)
    return pl.pallas_call(
        body, name="in_bwd", grid=(t // tm,),
        in_specs=[a512] * 8 + [_rows(tm, Q_LORA), _rows(tm, KV_LORA), _rows(tm, 128), _rows(tm, D_MODEL),
                               _rows(tm, D_MODEL), _full((1, D_MODEL)), _full((D_IN_PAD, D_MODEL))],
        out_specs=[_rows(tm, D_MODEL), _rows(tm, D_IN_PAD), _full((1, D_MODEL))],
        out_shape=[jax.ShapeDtypeStruct((t, D_MODEL), F32), jax.ShapeDtypeStruct((t, D_IN_PAD), BF16),
                   jax.ShapeDtypeStruct((1, D_MODEL), F32)],
        compiler_params=_params(1),
    )(dq_f, dq_r, dv_f, dv_r, dz_f, dz_r, dhg, hq, dcq, dckv, dkr, dx2, x, g1, w_in_p)


def _lower_bound(lbl_ref, direction):
    l0 = lbl_ref[direction, 0:1, :]
    l1 = lbl_ref[direction, 1:2, :]
    m = jnp.maximum(l0, l1)
    e0 = jnp.exp(l0 - m)
    e1 = jnp.exp(l1 - m)
    return e0 / (e0 + e1)


def _hgrn_consts(rb, reverse):
    row = _iota2((rb, rb), 0)
    col = _iota2((rb, rb), 1)
    same = (row // CHUNK) == (col // CHUNK)
    tri = jnp.logical_and(same, (col >= row) if reverse else (col <= row))
    tri_t = jnp.logical_and(same, (col <= row) if reverse else (col >= row))
    r128 = _iota2((128, 128), 0)
    c128 = _iota2((128, 128), 1)
    bd = (r128 < 64) == (c128 < 64)
    lane = _iota2((1, 128), 1)
    m0 = (lane < 64).astype(F32)
    return tri, tri_t, bd, (m0, 1.0 - m0)


def _per_chunk(x, fn):
    n = x.shape[0] // CHUNK
    return jnp.concatenate([jnp.broadcast_to(fn(x[c * CHUNK:(c + 1) * CHUNK]), (CHUNK, x.shape[1])) for c in range(n)],
                           axis=0)


def _chunk_cumsum(x, reverse):
    rb = x.shape[0]
    pos = _iota2(x.shape, 0) % CHUNK
    step = 1
    while step < CHUNK:
        if reverse:
            x = x + jnp.where(pos < CHUNK - step, pltpu.roll(x, rb - step, 0), 0.0)
        else:
            x = x + jnp.where(pos >= step, pltpu.roll(x, step, 0), 0.0)
        step *= 2
    return x


def _hgrn_block(z, hqv, lb, reverse):
    sig = _sigmoid(z)
    sn = _sigmoid(-z)
    q = hqv * _sigmoid(hqv)
    f = lb + (1.0 - lb) * sig
    k = (1.0 - lb) * sn
    lf = jnp.log(f)
    cum = _chunk_cumsum(lf, reverse)
    last = _per_chunk(cum, (lambda a: a[0:1]) if reverse else (lambda a: a[CHUNK - 1:CHUNK]))
    e_neg = jnp.exp(-cum)
    e_end = jnp.exp(last - cum)
    a = jnp.exp(cum)
    return dict(sig=sig, sn=sn, q=q, f=f, k=k, a=a, e_neg=e_neg, e_end=e_end,
                q_dec=q * a, k_inv=k * e_neg, k_end=k * e_end, d=jnp.exp(last))


def _hgrn_dims(t, n_batch):
    s = t // n_batch
    rb = _pick(s, 256, CHUNK)
    return s, rb, s // rb, rb // CHUNK


def _hgrn_fwd(hq, hi, hf, lbl, *, n_batch, direction):
    t = hq.shape[0]
    reverse = direction == 1
    s, rb, nb, nc = _hgrn_dims(t, n_batch)

    def tmap(b, j):
        return (b * nb + ((nb - 1 - j) if reverse else j), 0)

    def smap(b, j):
        return (b * nb + ((nb - 1 - j) if reverse else j), 0, 0, 0)

    def body(hq_ref, hi_ref, hf_ref, lbl_ref, o_ref, st_ref, st_scr):
        @pl.when(pl.program_id(1) == 0)
        def _():
            st_scr[...] = jnp.zeros_like(st_scr)

        lb_all = _lower_bound(lbl_ref, direction)
        tri, _, bd, masks = _hgrn_consts(rb, reverse)
        order = range(nc - 1, -1, -1) if reverse else range(nc)

        pairs = [slice(p * 128, (p + 1) * 128) for p in range(HEAD_PAIRS)]
        chunks = [slice(c * CHUNK, (c + 1) * CHUNK) for c in range(nc)]
        w = _hgrn_block(hf_ref[...], hq_ref[...], lb_all, reverse)
        v16 = hi_ref[...].astype(BF16)
        qd16 = w["q_dec"].astype(BF16)
        ki16 = w["k_inv"].astype(BF16)
        ke16 = w["k_end"].astype(BF16)
        sc = [[_dot_nt(jnp.where(mh > 0.0, qd16[:, ls], 0.0).astype(BF16), ki16[:, ls]) for mh in masks] for ls in pairs]
        pv = [[_dot(jnp.where(tri, s_e, 0.0).astype(BF16), v16[:, ls]) for s_e in sc_p] for sc_p, ls in zip(sc, pairs)]
        o_intra = [jnp.where(masks[0] > 0.0, pv_p[0], pv_p[1]) for pv_p in pv]
        ut = [[jnp.where(bd, _dot_tn(v16[rs, ls], ke16[rs, ls]), 0.0) for ls in pairs] for rs in chunks]
        st = [st_scr[p] for p in range(HEAD_PAIRS)]
        for c in order:
            rs = chunks[c]
            inter = [_dot_nt(qd16[rs, ls], st[p].astype(BF16)) for p, ls in enumerate(pairs)]
            for p, ls in enumerate(pairs):
                o_ref[rs, ls] = o_intra[p][rs] + inter[p]
                st_ref[c, p] = st[p]
                st[p] = st[p] * w["d"][c * CHUNK:c * CHUNK + 1, ls] + ut[c][p]
        for p in range(HEAD_PAIRS):
            st_scr[p] = st[p]

    blk = pl.BlockSpec((rb, A_WIDTH), tmap)
    return pl.pallas_call(
        body, name=f"hgrn_fwd_{direction}", grid=(n_batch, nb),
        in_specs=[blk, blk, blk, _full((2, 2, A_WIDTH))],
        out_specs=[blk, pl.BlockSpec((nc, HEAD_PAIRS, 128, 128), smap)],
        out_shape=[jax.ShapeDtypeStruct((t, A_WIDTH), F32),
                   jax.ShapeDtypeStruct((t // CHUNK, HEAD_PAIRS, 128, 128), F32)],
        scratch_shapes=[pltpu.VMEM((HEAD_PAIRS, 128, 128), F32)],
        compiler_params=_params(2),
    )(hq, hi, hf, lbl)


def _hgrn_bwd(hq, hi, hf, do, st, lbl, *, n_batch, direction):
    t = hq.shape[0]
    reverse = direction == 1
    s, rb, nb, nc = _hgrn_dims(t, n_batch)

    def tmap(b, j):
        return (b * nb + (j if reverse else (nb - 1 - j)), 0)

    def smap(b, j):
        return (b * nb + (j if reverse else (nb - 1 - j)), 0, 0, 0)

    def body(hq_ref, hi_ref, hf_ref, do_ref, st_ref, lbl_ref, dq_ref, dv_ref, dz_ref, dl_ref, g_scr, dlb_scr):
        b = pl.program_id(0)
        j = pl.program_id(1)

        @pl.when(jnp.logical_and(b == 0, j == 0))
        def _():
            dlb_scr[...] = jnp.zeros_like(dlb_scr)

        @pl.when(j == 0)
        def _():
            g_scr[...] = jnp.zeros_like(g_scr)

        lb_all = _lower_bound(lbl_ref, direction)
        tri, tri_t, bd, masks = _hgrn_consts(rb, reverse)
        order = range(nc) if reverse else range(nc - 1, -1, -1)

        chunks = [slice(c * CHUNK, (c + 1) * CHUNK) for c in range(nc)]

        def lanes(per_pair):
            return jnp.concatenate(per_pair, axis=1)

        for first_pair in range(0, HEAD_PAIRS, PAIRS_AT_ONCE):
            here = slice(first_pair * 128, (first_pair + PAIRS_AT_ONCE) * 128)
            pairs = [slice(p * 128, (p + 1) * 128) for p in range(PAIRS_AT_ONCE)]
            lb = lb_all[:, here]
            w = _hgrn_block(hf_ref[:, here], hq_ref[:, here], lb, reverse)
            dov = do_ref[:, here]
            v16 = hi_ref[:, here].astype(BF16)
            do16 = dov.astype(BF16)
            qd16 = w["q_dec"].astype(BF16)
            ki16 = w["k_inv"].astype(BF16)
            ke16 = w["k_end"].astype(BF16)
            qm16 = [[jnp.where(mh > 0.0, qd16[:, ls], 0.0).astype(BF16) for mh in masks] for ls in pairs]
            dom16 = [[jnp.where(mh > 0.0, do16[:, ls], 0.0).astype(BF16) for mh in masks] for ls in pairs]
            heads = [(p, e) for p in range(PAIRS_AT_ONCE) for e in range(2)]
            dp = {(p, e): _dot_nt(dom16[p][e], v16[:, pairs[p]]) for p, e in heads}
            pm_t = {(p, e): _dot_nt(ki16[:, pairs[p]], qm16[p][e]) for p, e in heads}
            dp_t = {(p, e): _dot_nt(v16[:, pairs[p]], dom16[p][e]) for p, e in heads}
            dp = {h: jnp.where(tri, a, 0.0).astype(BF16) for h, a in dp.items()}
            pm_t = {h: jnp.where(tri_t, a, 0.0).astype(BF16) for h, a in pm_t.items()}
            dp_t = {h: jnp.where(tri_t, a, 0.0).astype(BF16) for h, a in dp_t.items()}
            dv_e = {(p, e): _dot(pm_t[p, e], do16[:, pairs[p]]) for p, e in heads}
            dq_e = {(p, e): _dot(dp[p, e], ki16[:, pairs[p]]) for p, e in heads}
            dk_e = {(p, e): _dot(dp_t[p, e], qd16[:, pairs[p]]) for p, e in heads}
            st = [[st_ref[c, first_pair + p] for p in range(PAIRS_AT_ONCE)] for c in range(nc)]
            dq_x = [[_dot(do16[rs, ls], st[c][p].astype(BF16)) for p, ls in enumerate(pairs)] for c, rs in enumerate(chunks)]
            gq = [[jnp.where(bd, _dot_tn(do16[rs, ls], qd16[rs, ls]), 0.0) for ls in pairs] for rs in chunks]
            g = [g_scr[first_pair + p] for p in range(PAIRS_AT_ONCE)]
            dk_end, dv_x, dd = [None] * nc, [None] * nc, [None] * nc
            for c in order:
                rs = chunks[c]
                g16 = [a.astype(BF16) for a in g]
                dk_end[c] = lanes([_dot(v16[rs, ls], g16[p]) for p, ls in enumerate(pairs)])
                dv_x[c] = lanes([_dot_nt(ke16[rs, ls], g16[p]) for p, ls in enumerate(pairs)])
                dd[c] = jnp.broadcast_to(lanes([jnp.sum(g[p] * st[c][p], axis=0, keepdims=True) for p in range(PAIRS_AT_ONCE)]),
                                         (CHUNK, PAIRS_AT_ONCE * 128))
                for p, ls in enumerate(pairs):
                    g[p] = g[p] * w["d"][c * CHUNK:c * CHUNK + 1, ls] + gq[c][p]
            for p in range(PAIRS_AT_ONCE):
                g_scr[first_pair + p] = g[p]

            def both_heads(d):
                return lanes([jnp.where(masks[0] > 0.0, d[p, 0], d[p, 1]) for p in range(PAIRS_AT_ONCE)])

            dq_dec = both_heads(dq_e) + jnp.concatenate([lanes(a) for a in dq_x], axis=0)
            dk_inv = both_heads(dk_e)
            dk_end = jnp.concatenate(dk_end, axis=0)
            dv = both_heads(dv_e) + jnp.concatenate(dv_x, axis=0)
            dd = jnp.concatenate(dd, axis=0)
            dke = dk_end * w["k_end"]
            dcum = dq_dec * w["q_dec"] - dk_inv * w["k_inv"] - dke
            dk = dk_inv * w["e_neg"] + dk_end * w["e_end"]
            dlast = _per_chunk(dke, lambda a: jnp.sum(a, axis=0, keepdims=True)) + dd * w["d"]
            dlf = _chunk_cumsum(dcum, not reverse) + dlast
            tt = dlf / w["f"] - dk
            dq_ref[:, here] = dq_dec * w["a"]
            dv_ref[:, here] = dv
            dz_ref[:, here] = ((1.0 - lb) * w["sig"] * w["sn"] * tt).astype(BF16)
            dlb_scr[:, here] += jnp.sum(w["sn"] * tt, axis=0, keepdims=True)

        @pl.when(jnp.logical_and(b == pl.num_programs(0) - 1, j == pl.num_programs(1) - 1))
        def _():
            d0 = dlb_scr[...] * lb_all * (1.0 - lb_all)
            dl_ref[0:1, :] = d0
            dl_ref[1:2, :] = -d0

    blk = pl.BlockSpec((rb, A_WIDTH), tmap)
    return pl.pallas_call(
        body, name=f"hgrn_bwd_{direction}", grid=(n_batch, nb),
        in_specs=[blk, blk, blk, blk, pl.BlockSpec((nc, HEAD_PAIRS, 128, 128), smap), _full((2, 2, A_WIDTH))],
        out_specs=[blk, blk, blk, _full((2, A_WIDTH))],
        out_shape=[jax.ShapeDtypeStruct((t, A_WIDTH), F32)] * 2
        + [jax.ShapeDtypeStruct((t, A_WIDTH), BF16), jax.ShapeDtypeStruct((2, A_WIDTH), F32)],
        scratch_shapes=[pltpu.VMEM((HEAD_PAIRS, 128, 128), F32), pltpu.VMEM((1, A_WIDTH), F32)],
        compiler_params=_params(2),
    )(hq, hi, hf, do, st, lbl)


def _swap_rope_halves(x):
    lane = _iota2(x.shape, 1)
    return jnp.where(lane < 32, pltpu.roll(x, 96, 1), pltpu.roll(x, 32, 1))


def _rms_fwd(xv, g):
    r = lax.rsqrt(jnp.mean(xv * xv, axis=-1, keepdims=True) + EPS)
    return (xv * r) * g


def _rms_bwd(dy, xv, g):
    r = lax.rsqrt(jnp.mean(xv * xv, axis=-1, keepdims=True) + EPS)
    xh = xv * r
    dxh = dy * g
    return r * (dxh - xh * jnp.mean(dxh * xh, axis=-1, keepdims=True)), dy * xh


def _mla_prep(cq, ckv, kr, g_qa, g_kva, w_q_p, w_kv_p, cosx, sinx, *, n_batch):
    t = cq.shape[0]
    s = t // n_batch
    tm = _row_tile(s, 1024)
    nt = s // tm

    def body(cq_ref, ckv_ref, kr_ref, gq_ref, gkv_ref, wq_ref, wkv_ref, cos_ref, sin_ref,
             q_ref, k_ref, v_ref, cqn_ref, ckvn_ref):
        cos, sin = cos_ref[...], sin_ref[...]
        cqn = _rms_fwd(cq_ref[...], gq_ref[...]).astype(BF16)
        ckvn = _rms_fwd(ckv_ref[...], gkv_ref[...]).astype(BF16)
        cqn_ref[...] = cqn
        ckvn_ref[...] = ckvn
        krv = kr_ref[...]
        kr_roped = (krv * cos + _swap_rope_halves(krv) * sin).astype(BF16)
        for h in range(B_HEADS):
            o = h * QK_PAD
            q_ref[:, o:o + 128] = _dot_nt(cqn, wq_ref[o:o + 128, :]).astype(BF16)
            qr = _dot_nt(cqn, wq_ref[o + 128:o + 256, :])
            q_ref[:, o + 128:o + 256] = (qr * cos + _swap_rope_halves(qr) * sin).astype(BF16)
            k_ref[:, o:o + 128] = _dot(ckvn, wkv_ref[:, h * 128:(h + 1) * 128]).astype(BF16)
            k_ref[:, o + 128:o + 256] = kr_roped
        v_ref[...] = _dot(ckvn, wkv_ref[:, 512:1024]).astype(BF16)

    tab = pl.BlockSpec((tm, 128), lambda i: (i % nt, 0))
    return pl.pallas_call(
        body, name="mla_prep", grid=(t // tm,),
        in_specs=[_rows(tm, Q_LORA), _rows(tm, KV_LORA), _rows(tm, 128), _full((1, Q_LORA)), _full((1, KV_LORA)),
                  _full((1024, Q_LORA)), _full((KV_LORA, 1024)), tab, tab],
        out_specs=[_rows(tm, 1024), _rows(tm, 1024), _rows(tm, 512), _rows(tm, Q_LORA), _rows(tm, KV_LORA)],
        out_shape=[jax.ShapeDtypeStruct((t, 1024), BF16), jax.ShapeDtypeStruct((t, 1024), BF16),
                   jax.ShapeDtypeStruct((t, 512), BF16), jax.ShapeDtypeStruct((t, Q_LORA), BF16),
                   jax.ShapeDtypeStruct((t, KV_LORA), BF16)],
        compiler_params=_params(1),
    )(cq, ckv, kr, g_qa, g_kva, w_q_p, w_kv_p, cosx, sinx)


def _mla_prep_bwd(dqt, dk, dv, cq, ckv, g_qa, g_kva, w_q_p, w_kv_p, cosx, sinx, *, n_batch):
    t = cq.shape[0]
    s = t // n_batch
    tm = _row_tile(s, 1024)
    nt = s // tm

    def body(dqt_ref, dk_ref, dv_ref, cq_ref, ckv_ref, gq_ref, gkv_ref, wq_ref, wkv_ref, cos_ref, sin_ref,
             dcq_ref, dckv_ref, dkr_ref, dqp_ref, dkvp_ref, dgq_ref, dgkv_ref):
        cos, sin = cos_ref[...], sin_ref[...]
        first = pl.program_id(0) == 0

        def unrope(d):
            return d * cos + _swap_rope_halves(d * sin)

        dcqn = None
        dkr = None
        dckvn = None
        for h in range(B_HEADS):
            o = h * QK_PAD
            dq_h = jnp.transpose(dqt_ref[o:o + QK_PAD, :])
            dqn16 = dq_h[:, 0:128].astype(BF16)
            dqr16 = unrope(dq_h[:, 128:256]).astype(BF16)
            dqp_ref[:, o:o + 128] = dqn16
            dqp_ref[:, o + 128:o + 256] = dqr16
            part = _dot(dqn16, wq_ref[o:o + 128, :]) + _dot(dqr16, wq_ref[o + 128:o + 256, :])
            dcqn = part if dcqn is None else dcqn + part
            dkn16 = dk_ref[:, o:o + 128].astype(BF16)
            dkvp_ref[:, h * 128:(h + 1) * 128] = dkn16
            part = _dot_nt(dkn16, wkv_ref[:, h * 128:(h + 1) * 128])
            dckvn = part if dckvn is None else dckvn + part
            kr_part = dk_ref[:, o + 128:o + 256]
            dkr = kr_part if dkr is None else dkr + kr_part
        dv16 = dv_ref[...].astype(BF16)
        dkvp_ref[:, 512:1024] = dv16
        dckvn = dckvn + _dot_nt(dv16, wkv_ref[:, 512:1024])
        dkr_ref[...] = unrope(dkr).astype(BF16)
        dcq, dgq = _rms_bwd(dcqn, cq_ref[...], gq_ref[...])
        dckv, dgkv = _rms_bwd(dckvn, ckv_ref[...], gkv_ref[...])
        dcq_ref[...] = dcq.astype(BF16)
        dckv_ref[...] = dckv.astype(BF16)
        _acc_rows(dgq_ref, dgq, first)
        _acc_rows(dgkv_ref, dgkv, first)

    tab = pl.BlockSpec((tm, 128), lambda i: (i % nt, 0))
    return pl.pallas_call(
        body, name="mla_prep_bwd", grid=(t // tm,),
        in_specs=[pl.BlockSpec((1024, tm), lambda i: (0, i)), _rows(tm, 1024), _rows(tm, 512), _rows(tm, Q_LORA),
                  _rows(tm, KV_LORA),
                  _full((1, Q_LORA)), _full((1, KV_LORA)), _full((1024, Q_LORA)), _full((KV_LORA, 1024)), tab, tab],
        out_specs=[_rows(tm, Q_LORA), _rows(tm, KV_LORA), _rows(tm, 128), _rows(tm, 1024), _rows(tm, 1024),
                   _full((1, Q_LORA)), _full((1, KV_LORA))],
        out_shape=[jax.ShapeDtypeStruct((t, Q_LORA), BF16), jax.ShapeDtypeStruct((t, KV_LORA), BF16),
                   jax.ShapeDtypeStruct((t, 128), BF16), jax.ShapeDtypeStruct((t, 1024), BF16),
                   jax.ShapeDtypeStruct((t, 1024), BF16), jax.ShapeDtypeStruct((1, Q_LORA), F32),
                   jax.ShapeDtypeStruct((1, KV_LORA), F32)],
        compiler_params=_params(1),
    )(dqt, dk, dv, cq, ckv, g_qa, g_kva, w_q_p, w_kv_p, cosx, sinx)


def _attn_dims(t, n_batch):
    s = t // n_batch
    tq = _pick(s, 1024, 128)
    return s, tq, s // tq


def _grid_ends(n_axes):
    ids = [pl.program_id(a) for a in range(n_axes)]
    first = functools.reduce(jnp.logical_and, [i == 0 for i in ids])
    last = functools.reduce(jnp.logical_and, [i == pl.num_programs(a) - 1 for a, i in enumerate(ids)])
    return first, last


def _attn_fwd(q, k, v, wsrcs, *, n_batch):
    t = q.shape[0]
    s, tq, nq = _attn_dims(t, n_batch)
    nw = len(wsrcs)

    def body(q_ref, k_ref, v_ref, *refs):
        w_refs, (o_ref, lse_ref), wall_refs = refs[:nw], refs[nw:nw + 2], refs[nw + 2:2 * nw + 2]
        send_sems, recv_sems, local_sems = refs[2 * nw + 2:]
        first, last = _grid_ends(3)

        @pl.when(first)
        def _():
            for i in range(nw):
                _gather_start(w_refs[i], wall_refs[i], send_sems, recv_sems, local_sems.at[i], 3 * i)

        @pl.when(last)
        def _():
            for i in range(nw):
                _gather_wait(w_refs[i], wall_refs[i], send_sems, recv_sems, local_sems.at[i], 3 * i)

        kv, vv = k_ref[...], v_ref[...]
        groups = _row_groups(tq)
        raw = [_dot_nt(q_ref[g, :], kv) for g in groups]
        m = [jnp.max(a, axis=-1, keepdims=True) for a in raw]
        p = [jnp.exp2((a - b) * SCALE_LOG2E) for a, b in zip(raw, m)]
        l = [jnp.sum(a, axis=-1, keepdims=True) for a in p]
        for g, pg, mg, lg in zip(groups, p, m, l):
            o_ref[g, :] = _dot(pg.astype(BF16), vv) / lg
            lse2 = mg * SCALE_LOG2E + jnp.log(lg) * LOG2E
            lse_ref[:, g] = jnp.transpose(jnp.broadcast_to(lse2, (lse2.shape[0], 128)))[0:1, :]

    return pl.pallas_call(
        body, name="attn_fwd", grid=(n_batch, B_HEADS, nq),
        in_specs=[pl.BlockSpec((tq, QK_PAD), lambda b, h, i: (b * nq + i, h)),
                  pl.BlockSpec((s, QK_PAD), lambda b, h, i: (b, h)),
                  pl.BlockSpec((s, B_V), lambda b, h, i: (b, h))] + [_HBM] * nw,
        out_specs=[pl.BlockSpec((tq, B_V), lambda b, h, i: (b * nq + i, h)),
                   pl.BlockSpec((None, 1, tq), lambda b, h, i: (h, 0, b * nq + i))] + [_HBM] * nw,
        out_shape=[jax.ShapeDtypeStruct((t, B_HEADS * B_V), F32), jax.ShapeDtypeStruct((B_HEADS, 1, t), F32)]
        + [jax.ShapeDtypeStruct((N_CHIPS,) + w.shape, w.dtype) for w in wsrcs],
        scratch_shapes=[pltpu.SemaphoreType.DMA((3 * nw,)), pltpu.SemaphoreType.DMA((3 * nw,)),
                        pltpu.SemaphoreType.DMA((nw,))],
        compiler_params=_params(3),
    )(q, k, v, *wsrcs)


KEY_PART = 512


def _attn_bwd(q, k, v, do16, lse, delta, gsegs, *, n_batch):
    t = q.shape[0]
    s = t // n_batch
    tk = _pick(s, 4 * KEY_PART, 128)
    nk = s // tk
    ng = len(gsegs)

    def body(q_ref, k_ref, v_ref, do_ref, lse_ref, dl_ref, *refs):
        g_refs, (dqt_ref, dk_ref, dv_ref), recv_refs = refs[:ng], refs[ng:ng + 3], refs[ng + 3:2 * ng + 3]
        send_sems, recv_sems = refs[2 * ng + 3:]
        first, last = _grid_ends(3)

        @pl.when(first)
        def _():
            for i in range(ng):
                _scatter_start(g_refs[i], recv_refs[i], send_sems, recv_sems, 3 * i)

        @pl.when(last)
        def _():
            for i in range(ng):
                _scatter_wait(g_refs[i], recv_refs[i], send_sems, recv_sems, 3 * i)

        j = pl.program_id(2)
        qv, dov = q_ref[...], do_ref[...]
        dst = []
        for g in [slice(r, r + min(tk, KEY_PART)) for r in range(0, tk, min(tk, KEY_PART))]:
            pt = jnp.exp2(_dot_nt(k_ref[g, :], qv) * SCALE_LOG2E - lse_ref[...])
            dv_ref[g, :] = _dot(pt.astype(BF16), dov)
            dpt = _dot_nt(v_ref[g, :], dov)
            dst.append((pt * (dpt - dl_ref[...])).astype(BF16))
            dk_ref[g, :] = _dot(dst[-1], qv) * ATTN_SCALE
        part = _dot_tn(k_ref[...], jnp.concatenate(dst, axis=0))

        @pl.when(j == 0)
        def _():
            dqt_ref[...] = part

        @pl.when(j != 0)
        def _():
            dqt_ref[...] += part

        @pl.when(j == nk - 1)
        def _():
            dqt_ref[...] = dqt_ref[...] * ATTN_SCALE

    row = pl.BlockSpec((None, 1, s), lambda b, h, j: (h, 0, b))
    return pl.pallas_call(
        body, name="attn_bwd", grid=(n_batch, B_HEADS, nk),
        in_specs=[pl.BlockSpec((s, QK_PAD), lambda b, h, j: (b, h)),
                  pl.BlockSpec((tk, QK_PAD), lambda b, h, j: (b * nk + j, h)),
                  pl.BlockSpec((tk, B_V), lambda b, h, j: (b * nk + j, h)),
                  pl.BlockSpec((s, B_V), lambda b, h, j: (b, h)), row, row] + [_HBM] * ng,
        out_specs=[pl.BlockSpec((QK_PAD, s), lambda b, h, j: (h, b)),
                   pl.BlockSpec((tk, QK_PAD), lambda b, h, j: (b * nk + j, h)),
                   pl.BlockSpec((tk, B_V), lambda b, h, j: (b * nk + j, h))] + [_HBM] * ng,
        out_shape=[jax.ShapeDtypeStruct((B_HEADS * QK_PAD, t), F32), jax.ShapeDtypeStruct((t, B_HEADS * QK_PAD), F32),
                   jax.ShapeDtypeStruct((t, B_HEADS * B_V), F32)]
        + [jax.ShapeDtypeStruct((3,) + g.shape[1:], g.dtype) for g in gsegs],
        scratch_shapes=[pltpu.SemaphoreType.DMA((3 * ng,)), pltpu.SemaphoreType.DMA((3 * ng,))],
        compiler_params=_params(3),
    )(q, k, v, do16, lse, delta, *gsegs)


def _group_ones16():
    r = _iota2((A_WIDTH, A_WIDTH), 0) // 64
    c = _iota2((A_WIDTH, A_WIDTH), 1) // 64
    return (r == c).astype(BF16)


def _head_rms(o, ones16):
    return lax.rsqrt(_exact_right(o * o, ones16) * (1.0 / 64.0) + EPS)


def _out_fwd(o_f, o_r, hg, o_attn, x, g_hn, g_mla, w_out, g2):
    t = x.shape[0]
    tm = _row_tile(t, 4 * ROW_GROUP)

    def body(of_ref, or_ref, hg_ref, oa_ref, x_ref, ghn_ref, gm_ref, w_ref, g2_ref, y_ref, x2_ref, h2_ref):
        ones16 = _group_ones16()
        ghn, gm, g2v = ghn_ref[...], gm_ref[...], g2_ref[...]

        def group(ofv, orv, hgv, oav, xv):
            o = ofv + orv
            ya16 = (((o * _head_rms(o, ones16)) * ghn) * (hgv * _sigmoid(hgv))).astype(BF16)
            yb16 = _rms_fwd(oav, gm).astype(BF16)
            x2 = xv + _dot(ya16, w_ref[0:A_WIDTH, :]) + _dot(yb16, w_ref[A_WIDTH:D_MODEL, :])
            return ya16, yb16, x2, _rms_fwd(x2, g2v).astype(BF16)

        ins = (of_ref, or_ref, hg_ref, oa_ref, x_ref)
        groups = _row_groups(tm)
        done = [group(*vals) for vals in [[ref[g, :] for ref in ins] for g in groups]]
        for g, (ya16, yb16, x2, h2) in zip(groups, done):
            y_ref[g, 0:A_WIDTH] = ya16
            y_ref[g, A_WIDTH:D_MODEL] = yb16
            x2_ref[g, :] = x2
            h2_ref[g, :] = h2

    a512 = _rows(tm, A_WIDTH)
    return pl.pallas_call(
        body, name="out_fwd", grid=(t // tm,),
        in_specs=[a512, a512, a512, a512, _rows(tm, D_MODEL), _full((1, A_WIDTH)), _full((1, A_WIDTH)),
                  _full((D_MODEL, D_MODEL)), _full((1, D_MODEL))],
        out_specs=[_rows(tm, D_MODEL)] * 3,
        out_shape=[jax.ShapeDtypeStruct((t, D_MODEL), BF16), jax.ShapeDtypeStruct((t, D_MODEL), F32),
                   jax.ShapeDtypeStruct((t, D_MODEL), BF16)],
        compiler_params=_params(1),
    )(o_f, o_r, hg, o_attn, x, g_hn, g_mla, w_out, g2)


def _out_bwd(dx2_16, o_f, o_r, hg, o_attn, g_hn, g_mla, w_out):
    t = dx2_16.shape[0]
    tm = _row_tile(t, 4 * ROW_GROUP)

    def body(dx_ref, of_ref, or_ref, hg_ref, oa_ref, ghn_ref, gm_ref, w_ref,
             do_ref, dhg_ref, doa_ref, dl_ref, dghn_ref, dgm_ref):
        first = pl.program_id(0) == 0
        ones16 = _group_ones16()
        sel16 = (_iota2((8, A_WIDTH), 0) == _iota2((8, A_WIDTH), 1) // B_V).astype(BF16)
        ghn, gm = ghn_ref[...], gm_ref[...]

        def group(dxv, ofv, orv, hgv, oav):
            dya = _dot_nt(dxv, w_ref[0:A_WIDTH, :])
            dyb = _dot_nt(dxv, w_ref[A_WIDTH:D_MODEL, :])
            o = ofv + orv
            rh = _head_rms(o, ones16)
            oh = o * rh
            sg = _sigmoid(hgv)
            sl = hgv * sg
            dhg = ((dya * (oh * ghn)) * (sg * (1.0 + hgv * (1.0 - sg)))).astype(BF16)
            doh = dya * sl * ghn
            do = rh * (doh - oh * (_exact_right(doh * oh, ones16) * (1.0 / 64.0)))
            doa, dgm = _rms_bwd(dyb, oav, gm)
            x1, x2, x3 = _split3(doa * oav)
            delta = _dot_nt(sel16, x1) + _dot_nt(sel16, x2) + _dot_nt(sel16, x3)
            return do, dhg, doa.astype(BF16), delta, dya * sl * oh, dgm

        groups = _row_groups(tm)
        loaded = [(dx_ref[g, :], of_ref[g, :], or_ref[g, :], hg_ref[g, :], oa_ref[g, :]) for g in groups]
        done = [group(*vals) for vals in loaded]
        for g, (do, dhg, doa16, delta, _, _) in zip(groups, done):
            do_ref[g, :] = do
            dhg_ref[g, :] = dhg
            doa_ref[g, :] = doa16
            for h in range(B_HEADS):
                dl_ref[h, :, g] = delta[h:h + 1, :]
        _acc_rows(dghn_ref, jnp.concatenate([d[4] for d in done], axis=0), first)
        _acc_rows(dgm_ref, jnp.concatenate([d[5] for d in done], axis=0), first)

    a512 = _rows(tm, A_WIDTH)
    return pl.pallas_call(
        body, name="out_bwd", grid=(t // tm,),
        in_specs=[_rows(tm, D_MODEL), a512, a512, a512, a512, _full((1, A_WIDTH)), _full((1, A_WIDTH)),
                  _full((D_MODEL, D_MODEL))],
        out_specs=[a512, a512, a512, pl.BlockSpec((B_HEADS, 1, tm), lambda i: (0, 0, i)),
                   _full((1, A_WIDTH)), _full((1, A_WIDTH))],
        out_shape=[jax.ShapeDtypeStruct((t, A_WIDTH), F32)] + [jax.ShapeDtypeStruct((t, A_WIDTH), BF16)] * 2
        + [jax.ShapeDtypeStruct((B_HEADS, 1, t), F32)]
        + [jax.ShapeDtypeStruct((1, A_WIDTH), F32)] * 2,
        compiler_params=_params(1),
    )(dx2_16, o_f, o_r, hg, o_attn, g_hn, g_mla, w_out)


def _ffn_fwd_bwd(h2, x2, target, w_gate, w_up, w_down, g_f, g2):
    t = x2.shape[0]
    tm = _row_tile(t)
    inv_d = 1.0 / D_MODEL

    def body(h2_ref, x2_ref, tg_ref, wg_ref, wu_ref, wd_ref, gf_ref, g2_ref,
             act_ref, dgate_ref, dup_ref, dx3_ref, dx2_ref, dx2h_ref, loss_ref, dgf_ref, dg2_ref):
        first = pl.program_id(0) == 0
        h2v = h2_ref[...]
        gate = _dot_nt(h2v, wg_ref[...])
        up = _dot_nt(h2v, wu_ref[...])
        sg = _sigmoid(gate)
        sl = gate * sg
        act16 = (sl * up).astype(BF16)
        act_ref[...] = act16
        x2v = x2_ref[...]
        x3 = x2v + _dot(act16, wd_ref[...])
        r3 = lax.rsqrt(jnp.mean(x3 * x3, axis=-1, keepdims=True) + EPS)
        x3h = x3 * r3
        gf = gf_ref[...]
        err = x3h * gf - tg_ref[...]
        part = 0.5 * jnp.sum(jnp.mean(err * err, axis=-1, keepdims=True), axis=0, keepdims=True)

        @pl.when(first)
        def _():
            loss_ref[...] = jnp.zeros_like(loss_ref)

        loss_ref[...] += part
        dy = err * inv_d
        _acc_rows(dgf_ref, dy * x3h, first)
        dx3h = dy * gf
        dx3 = r3 * (dx3h - x3h * jnp.mean(dx3h * x3h, axis=-1, keepdims=True))
        dx3_16 = dx3.astype(BF16)
        dx3_ref[...] = dx3_16
        da = _dot_nt(dx3_16, wd_ref[...])
        dup16 = (da * sl).astype(BF16)
        dgate16 = (da * up * (sg * (1.0 + gate * (1.0 - sg)))).astype(BF16)
        dup_ref[...] = dup16
        dgate_ref[...] = dgate16
        dh2 = _dot(dgate16, wg_ref[...]) + _dot(dup16, wu_ref[...])
        dx2n, dg2 = _rms_bwd(dh2, x2v, g2_ref[...])
        _acc_rows(dg2_ref, dg2, first)
        dx2 = dx3 + dx2n
        dx2_ref[...] = dx2
        dx2h_ref[...] = dx2.astype(BF16)

    return pl.pallas_call(
        body, name="ffn_fwd_bwd", grid=(t // tm,),
        in_specs=[_rows(tm, D_MODEL), _rows(tm, D_MODEL), _rows(tm, D_MODEL), _full((D_FF, D_MODEL), True),
                  _full((D_FF, D_MODEL), True), _full((D_FF, D_MODEL), True), _full((1, D_MODEL)), _full((1, D_MODEL))],
        out_specs=[_rows(tm, D_FF), _rows(tm, D_FF), _rows(tm, D_FF), _rows(tm, D_MODEL), _rows(tm, D_MODEL),
                   _rows(tm, D_MODEL), _full((8, 128)), _full((1, D_MODEL)), _full((1, D_MODEL))],
        out_shape=[jax.ShapeDtypeStruct((t, D_FF), BF16)] * 3
        + [jax.ShapeDtypeStruct((t, D_MODEL), BF16), jax.ShapeDtypeStruct((t, D_MODEL), F32),
           jax.ShapeDtypeStruct((t, D_MODEL), BF16), jax.ShapeDtypeStruct((8, 128), F32),
           jax.ShapeDtypeStruct((1, D_MODEL), F32), jax.ShapeDtypeStruct((1, D_MODEL), F32)],
        compiler_params=_params(1),
    )(h2, x2, target, w_gate, w_up, w_down, g_f, g2)


def _wgrad(a, b, name):
    t, m = a.shape
    n = b.shape[1]
    bm = _pick(m, 1664)
    bn = _pick(n, 1664)
    tk = _pick(t, 2048, 16)
    nk = t // tk

    def body(a_ref, b_ref, o_ref, o16_ref):
        k = pl.program_id(2)
        part = _dot_tn(a_ref[...], b_ref[...])

        @pl.when(k == 0)
        def _():
            o_ref[...] = part

        @pl.when(k != 0)
        def _():
            o_ref[...] += part

        @pl.when(k == nk - 1)
        def _():
            o16_ref[...] = o_ref[...].astype(BF16)

    out = pl.BlockSpec((bm, bn), lambda i, j, k: (i, j))
    return pl.pallas_call(
        body, name=name, grid=(m // bm, n // bn, nk),
        in_specs=[pl.BlockSpec((tk, bm), lambda i, j, k: (k, i)), pl.BlockSpec((tk, bn), lambda i, j, k: (k, j))],
        out_specs=[out, out],
        out_shape=[jax.ShapeDtypeStruct((m, n), F32), jax.ShapeDtypeStruct((m, n), BF16)],
        compiler_params=_params(3),
    )(a, b)


def _rope_tables(seq):
    inv = 1.0 / (ROPE_THETA ** (jnp.arange(0, B_ROPE, 2, dtype=F32) / B_ROPE))
    ang = jnp.arange(seq, dtype=F32)[:, None] * inv[None, :]
    cos, sin = jnp.cos(ang), jnp.sin(ang)
    zeros = jnp.zeros((seq, 64), F32)
    return jnp.concatenate([cos, cos, zeros], axis=1), jnp.concatenate([-sin, sin, zeros], axis=1)


def _pad_weights(w_in_t, w_q_t, w_kv_b):
    w_in_p = jnp.pad(w_in_t, ((0, D_IN_PAD - D_IN), (0, 0)))
    w_q_p = jnp.pad(w_q_t.reshape(B_HEADS, B_NOPE + B_ROPE, Q_LORA), ((0, 0), (0, 64), (0, 0))).reshape(1024, Q_LORA)
    kv = w_kv_b.reshape(KV_LORA, B_HEADS, B_NOPE + B_V)
    w_kv_p = jnp.concatenate([kv[:, :, :B_NOPE].reshape(KV_LORA, 512), kv[:, :, B_NOPE:].reshape(KV_LORA, 512)], axis=1)
    return w_in_p, w_q_p, w_kv_p


def _unpad_kv(g_kv_p):
    return jnp.concatenate([g_kv_p[:, :512].reshape(KV_LORA, B_HEADS, B_NOPE),
                            g_kv_p[:, 512:].reshape(KV_LORA, B_HEADS, B_V)], axis=2).reshape(KV_LORA, 1024)


_OWN_ROWS = dict(w_in=(272, 3), w_q_b=(64, 4), w_kv_b=(256, 0), w_out=(256, 1), w_gate=(176, 4), w_up=(176, 4), w_down=(176, 4))


def _local_step(x, target, lbl, g1, g_hn, g_qa, g_kva, g_mla, g2, g_f, w_in, w_q_b, w_kv_b, late_shards):
    n_batch, seq, _ = x.shape
    t = n_batch * seq
    x = x.reshape(t, D_MODEL)
    target = target.reshape(t, D_MODEL)
    w_in_p, w_q_p, w_kv_p = _pad_weights(w_in, w_q_b, w_kv_b)
    cosx, sinx = _rope_tables(seq)

    h1, hq, hi, hff, hfb, hg, cq, ckv, kr = _in_fwd(x, g1, w_in_p)
    o_f, st_f = _hgrn_fwd(hq, hi, hff, lbl, n_batch=n_batch, direction=0)
    o_r, st_r = _hgrn_fwd(hq, hi, hfb, lbl, n_batch=n_batch, direction=1)
    q, k, v, cqn, ckvn = _mla_prep(cq, ckv, kr, g_qa, g_kva, w_q_p, w_kv_p, cosx, sinx, n_batch=n_batch)
    o_attn, lse, *gathered = _attn_fwd(q, k, v, late_shards, n_batch=n_batch)
    w_out, w_gate, w_up, w_down = (_join_shards(n, a) for n, a in zip(_LATE, gathered))
    ycat, x2, h2 = _out_fwd(o_f, o_r, hg, o_attn, x, g_hn, g_mla, w_out, g2)
    act, dgate, dup, dx3_16, dx2, dx2_16, loss, dg_f, dg2 = _ffn_fwd_bwd(h2, x2, target, w_gate, w_up, w_down, g_f, g2)
    full = dict(w_out=_wgrad(ycat, dx2_16, "wgrad_out"), w_gate=_wgrad(dgate, h2, "wgrad_gate"),
                w_up=_wgrad(dup, h2, "wgrad_up"), w_down=_wgrad(act, dx3_16, "wgrad_down"))
    do_h, dhg, do_attn16, delta, dg_hn, dg_mla = _out_bwd(dx2_16, o_f, o_r, hg, o_attn, g_hn, g_mla, w_out)
    dqt, dk, dv, *recv_late = _attn_bwd(q, k, v, do_attn16, lse, delta,
                                        [_segments(n, full[n][1]) for n in _LATE], n_batch=n_batch)
    dcq, dckv, dkr, dqp16, dkvp16, dg_qa, dg_kva = _mla_prep_bwd(dqt, dk, dv, cq, ckv, g_qa, g_kva, w_q_p, w_kv_p,
                                                                  cosx, sinx, n_batch=n_batch)
    dq_f, dv_f, dz_f, dl_f = _hgrn_bwd(hq, hi, hff, do_h, st_f, lbl, n_batch=n_batch, direction=0)
    dq_r, dv_r, dz_r, dl_r = _hgrn_bwd(hq, hi, hfb, do_h, st_r, lbl, n_batch=n_batch, direction=1)
    dx, dproj16, dg1 = _in_bwd(dq_f, dq_r, dv_f, dv_r, dz_f, dz_r, dhg, hq, dcq, dckv, dkr, dx2, x, g1, w_in_p)

    full.update(w_in=_wgrad(dproj16, h1, "wgrad_in"), w_q_b=_wgrad(dqp16, cqn, "wgrad_q_b"))
    g_kv = _unpad_kv(_wgrad(ckvn, dkvp16, "wgrad_kv_b")[0])
    early16 = dict(w_in=full["w_in"][1][:D_IN].reshape(N_CHIPS, D_IN // N_CHIPS, D_MODEL),
                   w_q_b=full["w_q_b"][1].reshape(B_HEADS, QK_PAD, Q_LORA)[:, :B_NOPE + B_ROPE],
                   w_kv_b=_segments("w_kv_b", g_kv).astype(BF16))
    small = dict(norm1_g=dg1, lb_logits=jnp.stack([dl_f, dl_r]), hgrn_norm_g=dg_hn, q_a_norm_g=dg_qa, kv_a_norm_g=dg_kva,
                 mla_norm_g=dg_mla, norm2_g=dg2, final_norm_g=dg_f, loss=loss[0:1, 0])
    return dx.reshape(n_batch, seq, D_MODEL), small, {**{n: f for n, (f, _) in full.items()}, "w_kv_b": g_kv}, early16, \
        dict(zip(_LATE, recv_late))


_HBM = pl.BlockSpec(memory_space=pltpu.HBM)
_MESH = pl.DeviceIdType.MESH


def _place():
    x, y, c = lax.axis_index("x"), lax.axis_index("y"), lax.axis_index("c")
    other_chips = [(1 - x, y), (x, 1 - y), (1 - x, 1 - y)]
    return x, y, c, other_chips


def _gather_copies(w_ref, wall_ref, send_sems, recv_sems, local_sem, base=0):
    x, y, c, chips = _place()
    mine = 2 * x + y

    def mk(j, chip_index, to):
        return pltpu.make_async_remote_copy(src_ref=w_ref, dst_ref=wall_ref.at[chip_index], send_sem=send_sems.at[base + j],
                                            recv_sem=recv_sems.at[base + j], device_id=to, device_id_type=_MESH)

    local = pltpu.make_async_copy(w_ref, wall_ref.at[mine], local_sem)
    sends = [mk(j, mine, (*chip, c)) for j, chip in enumerate(chips)]
    recvs = [mk(j, 2 * px + py, (x, y, c)) for j, (px, py) in enumerate(chips)]
    return local, sends, recvs


def _gather_start(*refs):
    local, sends, _ = _gather_copies(*refs)
    local.start()
    for cp in sends:
        cp.start()


def _gather_wait(*refs):
    local, sends, recvs = _gather_copies(*refs)
    for cp in recvs:
        cp.wait_recv()
    for cp in sends:
        cp.wait_send()
    local.wait()


def _scatter_copies(g_ref, recv_ref, send_sems, recv_sems, base=0):
    x, y, c, chips = _place()

    def mk(j, src_index, to):
        return pltpu.make_async_remote_copy(src_ref=g_ref.at[src_index], dst_ref=recv_ref.at[j],
                                            send_sem=send_sems.at[base + j], recv_sem=recv_sems.at[base + j],
                                            device_id=to, device_id_type=_MESH)

    sends = [mk(j, 2 * px + py, (px, py, c)) for j, (px, py) in enumerate(chips)]
    recvs = [mk(j, 0, (x, y, c)) for j in range(3)]
    return sends, recvs


def _scatter_start(*refs):
    for cp in _scatter_copies(*refs)[0]:
        cp.start()


def _scatter_wait(*refs):
    sends, recvs = _scatter_copies(*refs)
    for cp in recvs:
        cp.wait_recv()
    for cp in sends:
        cp.wait_send()


def _half(ref, which, axis):
    n = ref.shape[axis] // 2
    idx = [slice(None)] * len(ref.shape)
    idx[axis] = pl.ds(which * n, n)
    return ref.at[tuple(idx)]


def _split_axis(a):
    return 1 if a.shape[1] % 256 == 0 else 0


def _gather_halves(src_ref, dst_ref, send_sems, recv_sems, local_sem, base, axis):
    x, y, c, chips = _place()
    mine = 2 * x + y

    def remote(src, dst, n, to):
        return pltpu.make_async_remote_copy(src_ref=src, dst_ref=dst, send_sem=send_sems.at[base + n],
                                            recv_sem=recv_sems.at[base + n], device_id=to, device_id_type=_MESH)

    def slot(chip_index, which):
        return _half(dst_ref.at[chip_index], which, axis)

    local = pltpu.make_async_copy(src_ref, dst_ref.at[mine], local_sem)
    first = [remote(_half(src_ref, c, axis), slot(mine, c), j, (*chip, c)) for j, chip in enumerate(chips)]
    landed = [remote(_half(src_ref, c, axis), slot(2 * px + py, c), j, (x, y, c)) for j, (px, py) in enumerate(chips)]
    passed = [remote(slot(2 * px + py, c), slot(2 * px + py, c), 3 + j, (x, y, 1 - c)) for j, (px, py) in enumerate(chips)]
    handed = [remote(_half(src_ref, c, axis), slot(2 * px + py, 1 - c), 3 + j, (x, y, c)) for j, (px, py) in enumerate(chips)]

    def start():
        local.start()
        for cp in first:
            cp.start()

    def pass_on():
        for arrived, onward in zip(landed, passed):
            arrived.wait_recv()
            onward.start()

    def finish():
        for cp in handed:
            cp.wait_recv()
        for cp in first + passed:
            cp.wait_send()
        local.wait()

    return start, pass_on, finish


def _gather_shards(srcs, small):
    n = len(srcs)

    def body(*refs):
        src_refs, s_ref, dst_refs, sall_ref = refs[:n], refs[n], refs[n + 1:2 * n + 1], refs[2 * n + 1]
        send_sems, recv_sems, local_sems = refs[2 * n + 2:]
        _gather_start(s_ref, sall_ref, send_sems, recv_sems, local_sems.at[n], 6 * n)
        steps = [_gather_halves(src_refs[i], dst_refs[i], send_sems, recv_sems, local_sems.at[i], 6 * i, _split_axis(srcs[i]))
                 for i in range(n)]
        for phase in range(3):
            for step in steps:
                step[phase]()
        _gather_wait(s_ref, sall_ref, send_sems, recv_sems, local_sems.at[n], 6 * n)

    return pl.pallas_call(
        body, name="gather_shards", in_specs=[_HBM] * (n + 1), out_specs=[_HBM] * (n + 1),
        out_shape=[jax.ShapeDtypeStruct((N_CHIPS,) + a.shape, a.dtype) for a in list(srcs) + [small]],
        scratch_shapes=[pltpu.SemaphoreType.DMA((6 * n + 3,)), pltpu.SemaphoreType.DMA((6 * n + 3,)),
                        pltpu.SemaphoreType.DMA((n + 1,))],
    )(*srcs, small)


_SEM = pl.BlockSpec(memory_space=pltpu.SEMAPHORE)
_SIDE_EFFECT = pltpu.SideEffectType.DATAFLOW_SIDE_EFFECTING


def _scatter_begin(gsegs):
    n = len(gsegs)
    def hbm(a):
        return pltpu.with_memory_space_constraint(a, pltpu.HBM)

    lands = [hbm(lax.empty((3,) + g.shape[1:], g.dtype)) for g in gsegs]

    def body(*refs):
        g_refs, land_refs, send_sems, recv_sems, token = refs[:n], refs[n:2 * n], refs[2 * n], refs[2 * n + 1], refs[-1]
        for i in range(n):
            _scatter_start(g_refs[i], land_refs[i], send_sems, recv_sems, 3 * i)
        token[...] = jnp.zeros_like(token)

    out = pl.pallas_call(
        body, name="scatter_begin", in_specs=[_HBM] * (2 * n),
        out_specs=(_SEM, _SEM) + (_HBM,) * (2 * n) + (pl.BlockSpec(memory_space=pltpu.VMEM),),
        out_shape=(pltpu.SemaphoreType.DMA((3 * n,)), pltpu.SemaphoreType.DMA((3 * n,)))
        + tuple(pltpu.HBM(a.shape, a.dtype) for a in list(gsegs) + lands) + (jax.ShapeDtypeStruct((8, 128), F32),),
        input_output_aliases={i: 2 + i for i in range(2 * n)},
        compiler_params=pltpu.CompilerParams(has_side_effects=_SIDE_EFFECT),
    )(*[hbm(g) for g in gsegs], *lands)
    return out[0], out[1], out[2:2 + n], out[2 + n:2 + 2 * n], out[-1]


def _scatter_end(send_sems, recv_sems, gsegs, lands, after):
    n = len(gsegs)

    def body(*refs):
        g_refs, land_refs, ssems, rsems = refs[:n], refs[n:2 * n], refs[2 * n], refs[2 * n + 1]
        for i in range(n):
            _scatter_wait(g_refs[i], land_refs[i], ssems, rsems, 3 * i)

    out = pl.pallas_call(
        body, name="scatter_end", in_specs=[_HBM] * (2 * n) + [_SEM, _SEM] + [pl.BlockSpec(memory_space=pl.ANY)] * len(after),
        out_specs=(_HBM,) * (2 * n), out_shape=tuple(pltpu.HBM(a.shape, a.dtype) for a in list(gsegs) + list(lands)),
        input_output_aliases={i: i for i in range(2 * n)},
        compiler_params=pltpu.CompilerParams(has_side_effects=_SIDE_EFFECT),
    )(*gsegs, *lands, send_sems, recv_sems, *after)
    return out[n:]


def _swap_sibling(parts, small=None):
    n = len(parts)
    extra = small is not None

    def body(*refs):
        ins, outs = refs[:n + extra], refs[n + extra:2 * (n + extra)]
        send_sems, recv_sems, local_sem = refs[2 * (n + extra):]
        x, y, c, _ = _place()
        cps = [pltpu.make_async_remote_copy(src_ref=ins[i], dst_ref=outs[i], send_sem=send_sems.at[i],
                                            recv_sem=recv_sems.at[i], device_id=(x, y, 1 - c), device_id_type=_MESH)
               for i in range(n)]
        for cp in cps:
            cp.start()
        if extra:
            s_ref, sall_ref = ins[n], outs[n]
            me = 4 * x + 2 * y + c
            flips = [(fx, fy, fc) for fx in (0, 1) for fy in (0, 1) for fc in (0, 1)][1:]

            def peer(f):
                return tuple((1 - a) if b else a for a, b in zip((x, y, c), f))

            def sm(r, index, to):
                return pltpu.make_async_remote_copy(src_ref=s_ref, dst_ref=sall_ref.at[index], send_sem=send_sems.at[n + r],
                                                    recv_sem=recv_sems.at[n + r], device_id=to, device_id_type=_MESH)

            local = pltpu.make_async_copy(s_ref, sall_ref.at[me], local_sem)
            local.start()
            sends = [sm(r, me, peer(f)) for r, f in enumerate(flips)]
            for cp in sends:
                cp.start()
            for r, f in enumerate(flips):
                px, py, pc = peer(f)
                sm(r, 4 * px + 2 * py + pc, (x, y, c)).wait_recv()
            for cp in sends:
                cp.wait_send()
            local.wait()
        for cp in cps:
            cp.wait()

    return pl.pallas_call(
        body, name="swap_small" if extra else "swap_sibling", in_specs=[_HBM] * (n + extra), out_specs=[_HBM] * (n + extra),
        out_shape=[jax.ShapeDtypeStruct(p.shape, p.dtype) for p in parts]
        + ([jax.ShapeDtypeStruct((8,) + small.shape, F32)] if extra else []),
        scratch_shapes=[pltpu.SemaphoreType.DMA((n + 7,)), pltpu.SemaphoreType.DMA((n + 7,)), pltpu.SemaphoreType.DMA(())],
    )(*parts, *([small] if extra else []))


def _sum_segments(chip, full, recv, tm, pitch, name, after=()):
    _, rows, cols = recv.shape

    def body(chip_ref, o_ref, r_ref, *rest):
        acc = o_ref[...]
        for j in range(3):
            acc = acc + r_ref[j].astype(F32)
        rest[-1][...] = acc

    return pl.pallas_call(
        body, name=name,
        grid_spec=pltpu.PrefetchScalarGridSpec(
            num_scalar_prefetch=1, grid=(rows // tm,),
            in_specs=[pl.BlockSpec((tm, cols), lambda i, c: (c[0] * pitch + i, 0)),
                      pl.BlockSpec((3, tm, cols), lambda i, c: (0, i, 0))]
            + [pl.BlockSpec(a.shape, lambda i, c, nd=a.ndim: (0,) * nd) for a in after],
            out_specs=pl.BlockSpec((tm, cols), lambda i, c: (i, 0))),
        out_shape=jax.ShapeDtypeStruct((rows, cols), F32), compiler_params=_params(1),
    )(chip, full, recv, *after)


def _sum_devices(sall):
    def body(s_ref, o_ref):
        acc = s_ref[0]
        for d in range(1, 8):
            acc = acc + s_ref[d]
        o_ref[...] = acc

    return pl.pallas_call(body, name="sum_devices", out_shape=jax.ShapeDtypeStruct(sall.shape[1:], F32))(sall)


def _adamw(w, m, v, ga, gb, name):
    rows, cols = w.shape
    tm = _row_tile(rows)
    two = gb is not None

    def body(*refs):
        w_ref, m_ref, v_ref, ga_ref = refs[:4]
        g_ref, d_ref, m2_ref, v2_ref = refs[-4:]
        g = ga_ref[...] + refs[4][...] if two else ga_ref[...]
        m2 = ADAM_B1 * m_ref[...] + (1.0 - ADAM_B1) * g
        v2 = ADAM_B2 * v_ref[...] + (1.0 - ADAM_B2) * (g * g)
        m_hat = m2 / (1.0 - ADAM_B1 ** ADAM_STEP)
        v_hat = v2 / (1.0 - ADAM_B2 ** ADAM_STEP)
        g_ref[...] = g
        d_ref[...] = -ADAM_LR * (m_hat / (jnp.sqrt(v_hat) + ADAM_EPS) + ADAM_WD * w_ref[...])
        m2_ref[...] = m2
        v2_ref[...] = v2

    blk = _rows(tm, cols)
    args = (w, m, v, ga) + ((gb,) if two else ())
    return pl.pallas_call(
        body, name=name, grid=(rows // tm,), in_specs=[blk] * len(args), out_specs=[blk] * 4,
        out_shape=[jax.ShapeDtypeStruct(w.shape, F32)] * 4, compiler_params=_params(1),
    )(*args)


_TRANSPOSED = ("w_in", "w_q_b", "w_gate", "w_up")
_COLUMN_SHARDED = ("w_kv_b",)
_FULL_SHAPES = dict(w_in=(D_IN, D_MODEL), w_q_b=(768, Q_LORA), w_kv_b=(KV_LORA, 1024), w_out=(D_MODEL, D_MODEL),
                    w_gate=(D_FF, D_MODEL), w_up=(D_FF, D_MODEL), w_down=(D_FF, D_MODEL))
_SMALL = (("norm1_g", 1024), ("lb_logits", 2048), ("hgrn_norm_g", 512), ("q_a_norm_g", 384), ("kv_a_norm_g", 256),
          ("mla_norm_g", 512), ("norm2_g", 1024), ("final_norm_g", 1024))
_UPDATE_ROWS = 48


def _pad_rows(a, rows):
    return jnp.pad(a, ((0, rows - a.shape[0]), (0, 0)))


_EARLY = ("w_in", "w_q_b", "w_kv_b")
_LATE = ("w_out", "w_gate", "w_up", "w_down")


def _segments(name, g):
    r, c = g.shape
    if name in _COLUMN_SHARDED:
        return g.reshape(r, N_CHIPS, c // N_CHIPS).transpose(1, 0, 2)
    return g.reshape(N_CHIPS, r // N_CHIPS, c)


def _own_segment(name, g, chip):
    r, c = g.shape
    if name in _COLUMN_SHARDED:
        return lax.dynamic_slice_in_dim(g, chip * (c // N_CHIPS), c // N_CHIPS, axis=1)
    return lax.dynamic_slice_in_dim(g, chip * (r // N_CHIPS), r // N_CHIPS, axis=0)


def _join_shards(name, seg):
    r, c = _FULL_SHAPES[name]
    if name in _COLUMN_SHARDED:
        return seg.transpose(1, 0, 2).reshape(r, c)
    return seg.reshape(r, c)


def kernel(x, norm1_g, w_in, lb_logits, hgrn_norm_g, q_a_norm_g, w_q_b, kv_a_norm_g, w_kv_b, mla_norm_g, w_out, norm2_g, w_gate, w_up, w_down, final_norm_g, loss_target, m_norm1_g, m_w_in, m_lb_logits, m_hgrn_norm_g, m_q_a_norm_g, m_w_q_b, m_kv_a_norm_g, m_w_kv_b, m_mla_norm_g, m_w_out, m_norm2_g, m_w_gate, m_w_up, m_w_down, m_final_norm_g, v_norm1_g, v_w_in, v_lb_logits, v_hgrn_norm_g, v_q_a_norm_g, v_w_q_b, v_kv_a_norm_g, v_w_kv_b, v_mla_norm_g, v_w_out, v_norm2_g, v_w_gate, v_w_up, v_w_down, v_final_norm_g):
    names = ("norm1_g", "w_in", "lb_logits", "hgrn_norm_g", "q_a_norm_g", "w_q_b", "kv_a_norm_g", "w_kv_b", "mla_norm_g",
             "w_out", "norm2_g", "w_gate", "w_up", "w_down", "final_norm_g")
    w = dict(zip(names, (norm1_g, w_in, lb_logits, hgrn_norm_g, q_a_norm_g, w_q_b, kv_a_norm_g, w_kv_b, mla_norm_g,
                         w_out, norm2_g, w_gate, w_up, w_down, final_norm_g)))
    m = dict(zip(names, (m_norm1_g, m_w_in, m_lb_logits, m_hgrn_norm_g, m_q_a_norm_g, m_w_q_b, m_kv_a_norm_g, m_w_kv_b,
                         m_mla_norm_g, m_w_out, m_norm2_g, m_w_gate, m_w_up, m_w_down, m_final_norm_g)))
    v = dict(zip(names, (v_norm1_g, v_w_in, v_lb_logits, v_hgrn_norm_g, v_q_a_norm_g, v_w_q_b, v_kv_a_norm_g, v_w_kv_b,
                         v_mla_norm_g, v_w_out, v_norm2_g, v_w_gate, v_w_up, v_w_down, v_final_norm_g)))
    matrices = _EARLY + _LATE
    chip = 2 * lax.axis_index("x") + lax.axis_index("y")

    def shard2d(a, n):
        return jnp.swapaxes(a[0], 0, 1) if n in _TRANSPOSED else a[0]

    def unshard2d(a, n):
        return (jnp.swapaxes(a, 0, 1) if n in _TRANSPOSED else a)[None]

    w16 = {n: shard2d(w[n], n).astype(BF16) for n in matrices}
    lb8 = _pad_rows(lb_logits.reshape(4, 128), 8)
    *early, lball = _gather_shards([w16[n] for n in _EARLY], lb8)
    lbl = lball[:, :4].reshape(N_CHIPS, 2, 2, 128).transpose(1, 2, 0, 3).reshape(2, 2, A_WIDTH)

    grad_x, g, full, early16, recv = _local_step(
        x, loss_target, lbl, norm1_g, hgrn_norm_g, q_a_norm_g, kv_a_norm_g, mla_norm_g, norm2_g, final_norm_g[None, :],
        *(_join_shards(n, a) for n, a in zip(_EARLY, early)), [w16[n] for n in _LATE])

    n_small = sum(size for _, size in _SMALL)
    small = jnp.concatenate([g[n].reshape(-1) for n, _ in _SMALL] + [g["loss"], jnp.zeros((SMALL_ROWS * 128 - n_small - 1,), F32)])
    chip1 = chip.reshape(1).astype(jnp.int32)
    out, raw = {}, {}

    def update(n, part, sib):
        raw[n] = _adamw(shard2d(w[n], n), shard2d(m[n], n), shard2d(v[n], n), part, sib, f"adamw_{n}")
        out[n] = tuple(unshard2d(r, n) for r in raw[n])

    send_sems, recv_sems, segs, lands, token = _scatter_begin([early16[n] for n in _EARLY])
    parts = [_sum_segments(chip1, full[n], recv[n], *_OWN_ROWS[n], f"sum_{n}", after=(token,)) for n in _LATE]
    for n, part, sib in zip(_LATE, parts, _swap_sibling(parts)):
        update(n, part, sib)
    recv_early = _scatter_end(send_sems, recv_sems, segs, lands, after=[raw[n][1] for n in _LATE])
    full["w_kv_b"] = _own_segment("w_kv_b", full["w_kv_b"], chip)
    parts = [_sum_segments(chip1, full[n], r, *_OWN_ROWS[n], f"sum_{n}") for n, r in zip(_EARLY, recv_early)]
    *sibs, small_all = _swap_sibling(parts, small.reshape(SMALL_ROWS, 128))
    for n, part, sib in zip(_EARLY, parts, sibs):
        update(n, part, sib)
    small_sum = _sum_devices(small_all).reshape(-1)
    loss = small_sum[n_small]

    small_g, off = {}, 0
    for n, size in _SMALL:
        small_g[n] = small_sum[off:off + size]
        off += size
    small_g["lb_logits"] = lax.dynamic_slice_in_dim(small_g["lb_logits"].reshape(2, 2, A_WIDTH), chip * 128, 128, axis=2)
    small_names = tuple(n for n, _ in _SMALL)

    def pack(d):
        return _pad_rows(jnp.concatenate([d[n].reshape(-1) for n in small_names]).reshape(-1, 128), _UPDATE_ROWS)

    res = _adamw(pack(w), pack(m), pack(v), pack(small_g), None, "adamw_small")
    off = 0
    flat = [r.reshape(-1) for r in res]
    for n in small_names:
        size = w[n].size
        out[n] = tuple(f[off:off + size].reshape(w[n].shape) for f in flat)
        off += size

    return (loss, grad_x) + tuple(out[n][i] for i in range(4) for n in names)
```

```python
import functools

import jax
import jax.numpy as jnp
from jax import lax
from jax.experimental import pallas as pl
from jax.experimental.pallas import tpu as pltpu

F32 = jnp.float32
BF16 = jnp.bfloat16

D_MODEL = 1024
A_WIDTH = 512
HEAD_PAIRS = 4
PAIRS_AT_ONCE = 2
CHUNK = 64
B_HEADS = 4
B_NOPE = 128
B_ROPE = 64
B_V = 128
QK_PAD = 256
Q_LORA = 384
KV_LORA = 256
D_FF = 2816
D_IN = 3264
D_IN_PAD = 3328
IN_WIDTHS = (512, 512, 512, 512, 512, Q_LORA, KV_LORA, 128)
ROPE_THETA = 10000.0
EPS = 1e-6
ATTN_SCALE = (B_NOPE + B_ROPE) ** -0.5
LOG2E = 1.4426950408889634
SCALE_LOG2E = ATTN_SCALE * LOG2E

ADAM_LR = 0.001
ADAM_B1 = 0.9
ADAM_B2 = 0.999
ADAM_EPS = 1e-08
ADAM_WD = 0.01
ADAM_STEP = 10

VMEM_LIMIT_BYTES = 60 * 1024 * 1024
N_CHIPS = 4
SMALL_ROWS = 56


def _params(n_axes):
    return pltpu.CompilerParams(dimension_semantics=("arbitrary",) * n_axes,
                                vmem_limit_bytes=VMEM_LIMIT_BYTES)


def _dot(a, b):
    return jnp.dot(a, b, preferred_element_type=F32)


def _dot_nt(a, b):
    return lax.dot_general(a, b, (((1,), (1,)), ((), ())), preferred_element_type=F32)


def _dot_tn(a, b):
    return lax.dot_general(a, b, (((0,), (0,)), ((), ())), preferred_element_type=F32)


def _split3(x):
    x1 = x.astype(BF16)
    r = x - x1.astype(F32)
    x2 = r.astype(BF16)
    x3 = (r - x2.astype(F32)).astype(BF16)
    return x1, x2, x3


def _exact_right(x, m16):
    x1, x2, x3 = _split3(x)
    return _dot(x1, m16) + _dot(x2, m16) + _dot(x3, m16)


def _iota2(shape, dim):
    return lax.broadcasted_iota(jnp.int32, shape, dim)


def _sigmoid(x):
    return jax.nn.sigmoid(x)


def _pick(dim, cap, mult=128):
    if dim <= cap:
        return dim
    best = None
    for d in range(mult, cap + 1, mult):
        if dim % d == 0:
            best = d
    assert best is not None, (dim, cap, mult)
    return best


def _row_tile(t, cap=256):
    return _pick(t, cap, 8)


ROW_GROUP = 256


def _row_groups(tm):
    size = min(tm, ROW_GROUP)
    return [slice(r, r + size) for r in range(0, tm, size)]


def _full(shape, single=False):
    if single:
        return pl.BlockSpec(shape, lambda *_: (0,) * len(shape), pipeline_mode=pl.Buffered(1))
    return pl.BlockSpec(shape, lambda *_: (0,) * len(shape))


def _rows(tm, width):
    return pl.BlockSpec((tm, width), lambda i: (i, 0))


def _acc_rows(ref, val, first):
    s = jnp.sum(val, axis=0, keepdims=True)

    @pl.when(first)
    def _():
        ref[...] = s

    @pl.when(jnp.logical_not(first))
    def _():
        ref[...] += s


def _in_fwd(x, g1, w_in_p):
    t = x.shape[0]
    tm = _row_tile(t, 2 * ROW_GROUP)

    def body(x_ref, g_ref, w_ref, h_ref, *outs):
        gain = g_ref[...]

        def group(xv):
            r = lax.rsqrt(jnp.mean(xv * xv, axis=-1, keepdims=True) + EPS)
            h = ((xv * r) * gain).astype(BF16)
            proj, off = [], 0
            for w in IN_WIDTHS:
                proj.append(_dot_nt(h, w_ref[off:off + w, :]))
                off += w
            return h, proj

        groups = _row_groups(tm)
        done = [group(xv) for xv in [x_ref[g, :] for g in groups]]
        for g, (h, proj) in zip(groups, done):
            h_ref[g, :] = h
            for o_ref, p in zip(outs, proj):
                o_ref[g, :] = p

    return pl.pallas_call(
        body, name="in_fwd", grid=(t // tm,),
        in_specs=[_rows(tm, D_MODEL), _full((1, D_MODEL)), _full((D_IN_PAD, D_MODEL))],
        out_specs=[_rows(tm, D_MODEL)] + [_rows(tm, w) for w in IN_WIDTHS],
        out_shape=[jax.ShapeDtypeStruct((t, D_MODEL), BF16)]
        + [jax.ShapeDtypeStruct((t, w), F32) for w in IN_WIDTHS],
        compiler_params=_params(1),
    )(x, g1, w_in_p)


def _in_bwd(dq_f, dq_r, dv_f, dv_r, dz_f, dz_r, dhg, hq, dcq, dckv, dkr, dx2, x, g1, w_in_p):
    t = x.shape[0]
    tm = _row_tile(t, 2 * ROW_GROUP)

    def body(dqf_ref, dqr_ref, dvf_ref, dvr_ref, dzf_ref, dzr_ref, dhg_ref, hq_ref, dcq_ref, dckv_ref,
             dkr_ref, dx2_ref, x_ref, g_ref, w_ref, dx_ref, dp_ref, dg_ref):
        gain = g_ref[...]

        def group(dqf, dqr, dvf, dvr, dzf, dzr, dhg, hqv, dcq, dckv, dkr, dx2, xv):
            sg = _sigmoid(hqv)
            dhq = (dqf + dqr) * (sg * (1.0 + hqv * (1.0 - sg)))
            pieces = [p.astype(BF16) for p in (dhq, dvf + dvr, dzf, dzr, dhg, dcq, dckv, dkr)]
            dh = _dot(jnp.concatenate(pieces, axis=1), w_ref[...])
            r = lax.rsqrt(jnp.mean(xv * xv, axis=-1, keepdims=True) + EPS)
            xh = xv * r
            dxh = dh * gain
            return dx2 + r * (dxh - xh * jnp.mean(dxh * xh, axis=-1, keepdims=True)), pieces, dh * xh

        ins = (dqf_ref, dqr_ref, dvf_ref, dvr_ref, dzf_ref, dzr_ref, dhg_ref, hq_ref, dcq_ref, dckv_ref, dkr_ref,
               dx2_ref, x_ref)
        groups = _row_groups(tm)
        done = [group(*vals) for vals in [[ref[g, :] for ref in ins] for g in groups]]
        for g, (dx, pieces, _) in zip(groups, done):
            dx_ref[g, :] = dx
            off = 0
            for p16, w in zip(pieces, IN_WIDTHS):
                dp_ref[g, off:off + w] = p16
                off += w
        _acc_rows(dg_ref, jnp.concatenate([d[2] for d in done], axis=0), pl.program_id(0) == 0)

    a512 = _rows(tm, A_WIDTH)
    return pl.pallas_call(
        body, name="in_bwd", grid=(t // tm,),
        in_specs=[a512] * 8 + [_rows(tm, Q_LORA), _rows(tm, KV_LORA), _rows(tm, 128), _rows(tm, D_MODEL),
                               _rows(tm, D_MODEL), _full((1, D_MODEL)), _full((D_IN_PAD, D_MODEL))],
        out_specs=[_rows(tm, D_MODEL), _rows(tm, D_IN_PAD), _full((1, D_MODEL))],
        out_shape=[jax.ShapeDtypeStruct((t, D_MODEL), F32), jax.ShapeDtypeStruct((t, D_IN_PAD), BF16),
                   jax.ShapeDtypeStruct((1, D_MODEL), F32)],
        compiler_params=_params(1),
    )(dq_f, dq_r, dv_f, dv_r, dz_f, dz_r, dhg, hq, dcq, dckv, dkr, dx2, x, g1, w_in_p)


def _lower_bound(lbl_ref, direction):
    l0 = lbl_ref[direction, 0:1, :]
    l1 = lbl_ref[direction, 1:2, :]
    m = jnp.maximum(l0, l1)
    e0 = jnp.exp(l0 - m)
    e1 = jnp.exp(l1 - m)
    return e0 / (e0 + e1)


def _hgrn_consts(rb, reverse):
    row = _iota2((rb, rb), 0)
    col = _iota2((rb, rb), 1)
    same = (row // CHUNK) == (col // CHUNK)
    tri = jnp.logical_and(same, (col >= row) if reverse else (col <= row))
    tri_t = jnp.logical_and(same, (col <= row) if reverse else (col >= row))
    r128 = _iota2((128, 128), 0)
    c128 = _iota2((128, 128), 1)
    bd = (r128 < 64) == (c128 < 64)
    lane = _iota2((1, 128), 1)
    m0 = (lane < 64).astype(F32)
    return tri, tri_t, bd, (m0, 1.0 - m0)


def _per_chunk(x, fn):
    n = x.shape[0] // CHUNK
    return jnp.concatenate([jnp.broadcast_to(fn(x[c * CHUNK:(c + 1) * CHUNK]), (CHUNK, x.shape[1])) for c in range(n)],
                           axis=0)


def _chunk_cumsum(x, reverse):
    rb = x.shape[0]
    pos = _iota2(x.shape, 0) % CHUNK
    step = 1
    while step < CHUNK:
        if reverse:
            x = x + jnp.where(pos < CHUNK - step, pltpu.roll(x, rb - step, 0), 0.0)
        else:
            x = x + jnp.where(pos >= step, pltpu.roll(x, step, 0), 0.0)
        step *= 2
    return x


def _hgrn_block(z, hqv, lb, reverse):
    sig = _sigmoid(z)
    sn = _sigmoid(-z)
    q = hqv * _sigmoid(hqv)
    f = lb + (1.0 - lb) * sig
    k = (1.0 - lb) * sn
    lf = jnp.log(f)
    cum = _chunk_cumsum(lf, reverse)
    last = _per_chunk(cum, (lambda a: a[0:1]) if reverse else (lambda a: a[CHUNK - 1:CHUNK]))
    e_neg = jnp.exp(-cum)
    e_end = jnp.exp(last - cum)
    a = jnp.exp(cum)
    return dict(sig=sig, sn=sn, q=q, f=f, k=k, a=a, e_neg=e_neg, e_end=e_end,
                q_dec=q * a, k_inv=k * e_neg, k_end=k * e_end, d=jnp.exp(last))


def _hgrn_dims(t, n_batch):
    s = t // n_batch
    rb = _pick(s, 256, CHUNK)
    return s, rb, s // rb, rb // CHUNK


def _hgrn_fwd(hq, hi, hff, hfb, lbl, *, n_batch):
    t = hq.shape[0]
    s, rb, nb, nc = _hgrn_dims(t, n_batch)

    def body(hqf_ref, hif_ref, hff_ref, hqr_ref, hir_ref, hfr_ref, lbl_ref, of_ref, stf_ref, or_ref, str_ref,
             stf_scr, str_scr):
        one(0, hqf_ref, hif_ref, hff_ref, lbl_ref, of_ref, stf_ref, stf_scr)
        one(1, hqr_ref, hir_ref, hfr_ref, lbl_ref, or_ref, str_ref, str_scr)

    def one(direction, hq_ref, hi_ref, hf_ref, lbl_ref, o_ref, st_ref, st_scr):
        reverse = direction == 1

        @pl.when(pl.program_id(1) == 0)
        def _():
            st_scr[...] = jnp.zeros_like(st_scr)

        lb_all = _lower_bound(lbl_ref, direction)
        tri, _, bd, masks = _hgrn_consts(rb, reverse)
        order = range(nc - 1, -1, -1) if reverse else range(nc)

        pairs = [slice(p * 128, (p + 1) * 128) for p in range(HEAD_PAIRS)]
        chunks = [slice(c * CHUNK, (c + 1) * CHUNK) for c in range(nc)]
        w = _hgrn_block(hf_ref[...], hq_ref[...], lb_all, reverse)
        v16 = hi_ref[...].astype(BF16)
        qd16 = w["q_dec"].astype(BF16)
        ki16 = w["k_inv"].astype(BF16)
        ke16 = w["k_end"].astype(BF16)
        sc = [[_dot_nt(jnp.where(mh > 0.0, qd16[:, ls], 0.0).astype(BF16), ki16[:, ls]) for mh in masks] for ls in pairs]
        pv = [[_dot(jnp.where(tri, s_e, 0.0).astype(BF16), v16[:, ls]) for s_e in sc_p] for sc_p, ls in zip(sc, pairs)]
        o_intra = [jnp.where(masks[0] > 0.0, pv_p[0], pv_p[1]) for pv_p in pv]
        ut = [[jnp.where(bd, _dot_tn(v16[rs, ls], ke16[rs, ls]), 0.0) for ls in pairs] for rs in chunks]
        st = [st_scr[p] for p in range(HEAD_PAIRS)]
        for c in order:
            rs = chunks[c]
            inter = [_dot_nt(qd16[rs, ls], st[p].astype(BF16)) for p, ls in enumerate(pairs)]
            for p, ls in enumerate(pairs):
                o_ref[rs, ls] = o_intra[p][rs] + inter[p]
                st_ref[c, p] = st[p]
                st[p] = st[p] * w["d"][c * CHUNK:c * CHUNK + 1, ls] + ut[c][p]
        for p in range(HEAD_PAIRS):
            st_scr[p] = st[p]

    fwd = pl.BlockSpec((rb, A_WIDTH), lambda b, j: (b * nb + j, 0))
    rev = pl.BlockSpec((rb, A_WIDTH), lambda b, j: (b * nb + nb - 1 - j, 0))
    st_fwd = pl.BlockSpec((nc, HEAD_PAIRS, 128, 128), lambda b, j: (b * nb + j, 0, 0, 0))
    st_rev = pl.BlockSpec((nc, HEAD_PAIRS, 128, 128), lambda b, j: (b * nb + nb - 1 - j, 0, 0, 0))
    o_shape = jax.ShapeDtypeStruct((t, A_WIDTH), F32)
    st_shape = jax.ShapeDtypeStruct((t // CHUNK, HEAD_PAIRS, 128, 128), F32)
    return pl.pallas_call(
        body, name="hgrn_fwd", grid=(n_batch, nb),
        in_specs=[fwd, fwd, fwd, rev, rev, rev, _full((2, 2, A_WIDTH))],
        out_specs=[fwd, st_fwd, rev, st_rev],
        out_shape=[o_shape, st_shape, o_shape, st_shape],
        scratch_shapes=[pltpu.VMEM((HEAD_PAIRS, 128, 128), F32), pltpu.VMEM((HEAD_PAIRS, 128, 128), F32)],
        compiler_params=_params(2),
    )(hq, hi, hff, hq, hi, hfb, lbl)


def _hgrn_bwd(hq, hi, hf, do, st, lbl, *, n_batch, direction):
    t = hq.shape[0]
    reverse = direction == 1
    s, rb, nb, nc = _hgrn_dims(t, n_batch)

    def tmap(b, j):
        return (b * nb + (j if reverse else (nb - 1 - j)), 0)

    def smap(b, j):
        return (b * nb + (j if reverse else (nb - 1 - j)), 0, 0, 0)

    def body(hq_ref, hi_ref, hf_ref, do_ref, st_ref, lbl_ref, dq_ref, dv_ref, dz_ref, dl_ref, g_scr, dlb_scr):
        b = pl.program_id(0)
        j = pl.program_id(1)

        @pl.when(jnp.logical_and(b == 0, j == 0))
        def _():
            dlb_scr[...] = jnp.zeros_like(dlb_scr)

        @pl.when(j == 0)
        def _():
            g_scr[...] = jnp.zeros_like(g_scr)

        lb_all = _lower_bound(lbl_ref, direction)
        tri, tri_t, bd, masks = _hgrn_consts(rb, reverse)
        order = range(nc) if reverse else range(nc - 1, -1, -1)

        chunks = [slice(c * CHUNK, (c + 1) * CHUNK) for c in range(nc)]

        def lanes(per_pair):
            return jnp.concatenate(per_pair, axis=1)

        for first_pair in range(0, HEAD_PAIRS, PAIRS_AT_ONCE):
            here = slice(first_pair * 128, (first_pair + PAIRS_AT_ONCE) * 128)
            pairs = [slice(p * 128, (p + 1) * 128) for p in range(PAIRS_AT_ONCE)]
            lb = lb_all[:, here]
            w = _hgrn_block(hf_ref[:, here], hq_ref[:, here], lb, reverse)
            dov = do_ref[:, here]
            v16 = hi_ref[:, here].astype(BF16)
            do16 = dov.astype(BF16)
            qd16 = w["q_dec"].astype(BF16)
            ki16 = w["k_inv"].astype(BF16)
            ke16 = w["k_end"].astype(BF16)
            qm16 = [[jnp.where(mh > 0.0, qd16[:, ls], 0.0).astype(BF16) for mh in masks] for ls in pairs]
            dom16 = [[jnp.where(mh > 0.0, do16[:, ls], 0.0).astype(BF16) for mh in masks] for ls in pairs]
            heads = [(p, e) for p in range(PAIRS_AT_ONCE) for e in range(2)]
            dp = {(p, e): _dot_nt(dom16[p][e], v16[:, pairs[p]]) for p, e in heads}
            pm_t = {(p, e): _dot_nt(ki16[:, pairs[p]], qm16[p][e]) for p, e in heads}
            dp_t = {(p, e): _dot_nt(v16[:, pairs[p]], dom16[p][e]) for p, e in heads}
            dp = {h: jnp.where(tri, a, 0.0).astype(BF16) for h, a in dp.items()}
            pm_t = {h: jnp.where(tri_t, a, 0.0).astype(BF16) for h, a in pm_t.items()}
            dp_t = {h: jnp.where(tri_t, a, 0.0).astype(BF16) for h, a in dp_t.items()}
            dv_e = {(p, e): _dot(pm_t[p, e], do16[:, pairs[p]]) for p, e in heads}
            dq_e = {(p, e): _dot(dp[p, e], ki16[:, pairs[p]]) for p, e in heads}
            dk_e = {(p, e): _dot(dp_t[p, e], qd16[:, pairs[p]]) for p, e in heads}
            st = [[st_ref[c, first_pair + p] for p in range(PAIRS_AT_ONCE)] for c in range(nc)]
            dq_x = [[_dot(do16[rs, ls], st[c][p].astype(BF16)) for p, ls in enumerate(pairs)] for c, rs in enumerate(chunks)]
            gq = [[jnp.where(bd, _dot_tn(do16[rs, ls], qd16[rs, ls]), 0.0) for ls in pairs] for rs in chunks]
            g = [g_scr[first_pair + p] for p in range(PAIRS_AT_ONCE)]
            dk_end, dv_x, dd = [None] * nc, [None] * nc, [None] * nc
            for c in order:
                rs = chunks[c]
                g16 = [a.astype(BF16) for a in g]
                dk_end[c] = lanes([_dot(v16[rs, ls], g16[p]) for p, ls in enumerate(pairs)])
                dv_x[c] = lanes([_dot_nt(ke16[rs, ls], g16[p]) for p, ls in enumerate(pairs)])
                dd[c] = jnp.broadcast_to(lanes([jnp.sum(g[p] * st[c][p], axis=0, keepdims=True) for p in range(PAIRS_AT_ONCE)]),
                                         (CHUNK, PAIRS_AT_ONCE * 128))
                for p, ls in enumerate(pairs):
                    g[p] = g[p] * w["d"][c * CHUNK:c * CHUNK + 1, ls] + gq[c][p]
            for p in range(PAIRS_AT_ONCE):
                g_scr[first_pair + p] = g[p]

            def both_heads(d):
                return lanes([jnp.where(masks[0] > 0.0, d[p, 0], d[p, 1]) for p in range(PAIRS_AT_ONCE)])

            dq_dec = both_heads(dq_e) + jnp.concatenate([lanes(a) for a in dq_x], axis=0)
            dk_inv = both_heads(dk_e)
            dk_end = jnp.concatenate(dk_end, axis=0)
            dv = both_heads(dv_e) + jnp.concatenate(dv_x, axis=0)
            dd = jnp.concatenate(dd, axis=0)
            dke = dk_end * w["k_end"]
            dcum = dq_dec * w["q_dec"] - dk_inv * w["k_inv"] - dke
            dk = dk_inv * w["e_neg"] + dk_end * w["e_end"]
            dlast = _per_chunk(dke, lambda a: jnp.sum(a, axis=0, keepdims=True)) + dd * w["d"]
            dlf = _chunk_cumsum(dcum, not reverse) + dlast
            tt = dlf / w["f"] - dk
            dq_ref[:, here] = dq_dec * w["a"]
            dv_ref[:, here] = dv
            dz_ref[:, here] = ((1.0 - lb) * w["sig"] * w["sn"] * tt).astype(BF16)
            dlb_scr[:, here] += jnp.sum(w["sn"] * tt, axis=0, keepdims=True)

        @pl.when(jnp.logical_and(b == pl.num_programs(0) - 1, j == pl.num_programs(1) - 1))
        def _():
            d0 = dlb_scr[...] * lb_all * (1.0 - lb_all)
            dl_ref[0:1, :] = d0
            dl_ref[1:2, :] = -d0

    blk = pl.BlockSpec((rb, A_WIDTH), tmap)
    return pl.pallas_call(
        body, name=f"hgrn_bwd_{direction}", grid=(n_batch, nb),
        in_specs=[blk, blk, blk, blk, pl.BlockSpec((nc, HEAD_PAIRS, 128, 128), smap), _full((2, 2, A_WIDTH))],
        out_specs=[blk, blk, blk, _full((2, A_WIDTH))],
        out_shape=[jax.ShapeDtypeStruct((t, A_WIDTH), F32)] * 2
        + [jax.ShapeDtypeStruct((t, A_WIDTH), BF16), jax.ShapeDtypeStruct((2, A_WIDTH), F32)],
        scratch_shapes=[pltpu.VMEM((HEAD_PAIRS, 128, 128), F32), pltpu.VMEM((1, A_WIDTH), F32)],
        compiler_params=_params(2),
    )(hq, hi, hf, do, st, lbl)


def _swap_rope_halves(x):
    lane = _iota2(x.shape, 1)
    return jnp.where(lane < 32, pltpu.roll(x, 96, 1), pltpu.roll(x, 32, 1))


def _rms_fwd(xv, g):
    r = lax.rsqrt(jnp.mean(xv * xv, axis=-1, keepdims=True) + EPS)
    return (xv * r) * g


def _rms_bwd(dy, xv, g):
    r = lax.rsqrt(jnp.mean(xv * xv, axis=-1, keepdims=True) + EPS)
    xh = xv * r
    dxh = dy * g
    return r * (dxh - xh * jnp.mean(dxh * xh, axis=-1, keepdims=True)), dy * xh


def _mla_prep(cq, ckv, kr, g_qa, g_kva, w_q_p, w_kv_p, cosx, sinx, *, n_batch):
    t = cq.shape[0]
    s = t // n_batch
    tm = _row_tile(s, 1024)
    nt = s // tm

    def body(cq_ref, ckv_ref, kr_ref, gq_ref, gkv_ref, wq_ref, wkv_ref, cos_ref, sin_ref,
             q_ref, k_ref, v_ref, cqn_ref, ckvn_ref):
        cos, sin = cos_ref[...], sin_ref[...]
        cqn = _rms_fwd(cq_ref[...], gq_ref[...]).astype(BF16)
        ckvn = _rms_fwd(ckv_ref[...], gkv_ref[...]).astype(BF16)
        cqn_ref[...] = cqn
        ckvn_ref[...] = ckvn
        krv = kr_ref[...]
        kr_roped = (krv * cos + _swap_rope_halves(krv) * sin).astype(BF16)
        for h in range(B_HEADS):
            o = h * QK_PAD
            q_ref[:, o:o + 128] = _dot_nt(cqn, wq_ref[o:o + 128, :]).astype(BF16)
            qr = _dot_nt(cqn, wq_ref[o + 128:o + 256, :])
            q_ref[:, o + 128:o + 256] = (qr * cos + _swap_rope_halves(qr) * sin).astype(BF16)
            k_ref[:, o:o + 128] = _dot(ckvn, wkv_ref[:, h * 128:(h + 1) * 128]).astype(BF16)
            k_ref[:, o + 128:o + 256] = kr_roped
        v_ref[...] = _dot(ckvn, wkv_ref[:, 512:1024]).astype(BF16)

    tab = pl.BlockSpec((tm, 128), lambda i: (i % nt, 0))
    return pl.pallas_call(
        body, name="mla_prep", grid=(t // tm,),
        in_specs=[_rows(tm, Q_LORA), _rows(tm, KV_LORA), _rows(tm, 128), _full((1, Q_LORA)), _full((1, KV_LORA)),
                  _full((1024, Q_LORA)), _full((KV_LORA, 1024)), tab, tab],
        out_specs=[_rows(tm, 1024), _rows(tm, 1024), _rows(tm, 512), _rows(tm, Q_LORA), _rows(tm, KV_LORA)],
        out_shape=[jax.ShapeDtypeStruct((t, 1024), BF16), jax.ShapeDtypeStruct((t, 1024), BF16),
                   jax.ShapeDtypeStruct((t, 512), BF16), jax.ShapeDtypeStruct((t, Q_LORA), BF16),
                   jax.ShapeDtypeStruct((t, KV_LORA), BF16)],
        compiler_params=_params(1),
    )(cq, ckv, kr, g_qa, g_kva, w_q_p, w_kv_p, cosx, sinx)


def _mla_prep_bwd(dqt, dk, dv, cq, ckv, g_qa, g_kva, w_q_p, w_kv_p, cosx, sinx, *, n_batch):
    t = cq.shape[0]
    s = t // n_batch
    tm = _row_tile(s, 1024)
    nt = s // tm

    def body(dqt_ref, dk_ref, dv_ref, cq_ref, ckv_ref, gq_ref, gkv_ref, wq_ref, wkv_ref, cos_ref, sin_ref,
             dcq_ref, dckv_ref, dkr_ref, dqp_ref, dkvp_ref, dgq_ref, dgkv_ref):
        cos, sin = cos_ref[...], sin_ref[...]
        first = pl.program_id(0) == 0

        def unrope(d):
            return d * cos + _swap_rope_halves(d * sin)

        dcqn = None
        dkr = None
        dckvn = None
        for h in range(B_HEADS):
            o = h * QK_PAD
            dq_h = jnp.transpose(dqt_ref[o:o + QK_PAD, :])
            dqn16 = dq_h[:, 0:128].astype(BF16)
            dqr16 = unrope(dq_h[:, 128:256]).astype(BF16)
            dqp_ref[:, o:o + 128] = dqn16
            dqp_ref[:, o + 128:o + 256] = dqr16
            part = _dot(dqn16, wq_ref[o:o + 128, :]) + _dot(dqr16, wq_ref[o + 128:o + 256, :])
            dcqn = part if dcqn is None else dcqn + part
            dkn16 = dk_ref[:, o:o + 128].astype(BF16)
            dkvp_ref[:, h * 128:(h + 1) * 128] = dkn16
            part = _dot_nt(dkn16, wkv_ref[:, h * 128:(h + 1) * 128])
            dckvn = part if dckvn is None else dckvn + part
            kr_part = dk_ref[:, o + 128:o + 256]
            dkr = kr_part if dkr is None else dkr + kr_part
        dv16 = dv_ref[...].astype(BF16)
        dkvp_ref[:, 512:1024] = dv16
        dckvn = dckvn + _dot_nt(dv16, wkv_ref[:, 512:1024])
        dkr_ref[...] = unrope(dkr).astype(BF16)
        dcq, dgq = _rms_bwd(dcqn, cq_ref[...], gq_ref[...])
        dckv, dgkv = _rms_bwd(dckvn, ckv_ref[...], gkv_ref[...])
        dcq_ref[...] = dcq.astype(BF16)
        dckv_ref[...] = dckv.astype(BF16)
        _acc_rows(dgq_ref, dgq, first)
        _acc_rows(dgkv_ref, dgkv, first)

    tab = pl.BlockSpec((tm, 128), lambda i: (i % nt, 0))
    return pl.pallas_call(
        body, name="mla_prep_bwd", grid=(t // tm,),
        in_specs=[pl.BlockSpec((1024, tm), lambda i: (0, i)), _rows(tm, 1024), _rows(tm, 512), _rows(tm, Q_LORA),
                  _rows(tm, KV_LORA),
                  _full((1, Q_LORA)), _full((1, KV_LORA)), _full((1024, Q_LORA)), _full((KV_LORA, 1024)), tab, tab],
        out_specs=[_rows(tm, Q_LORA), _rows(tm, KV_LORA), _rows(tm, 128), _rows(tm, 1024), _rows(tm, 1024),
                   _full((1, Q_LORA)), _full((1, KV_LORA))],
        out_shape=[jax.ShapeDtypeStruct((t, Q_LORA), BF16), jax.ShapeDtypeStruct((t, KV_LORA), BF16),
                   jax.ShapeDtypeStruct((t, 128), BF16), jax.ShapeDtypeStruct((t, 1024), BF16),
                   jax.ShapeDtypeStruct((t, 1024), BF16), jax.ShapeDtypeStruct((1, Q_LORA), F32),
                   jax.ShapeDtypeStruct((1, KV_LORA), F32)],
        compiler_params=_params(1),
    )(dqt, dk, dv, cq, ckv, g_qa, g_kva, w_q_p, w_kv_p, cosx, sinx)


def _attn_dims(t, n_batch):
    s = t // n_batch
    tq = _pick(s, 1024, 128)
    return s, tq, s // tq


def _grid_ends(n_axes):
    ids = [pl.program_id(a) for a in range(n_axes)]
    first = functools.reduce(jnp.logical_and, [i == 0 for i in ids])
    last = functools.reduce(jnp.logical_and, [i == pl.num_programs(a) - 1 for a, i in enumerate(ids)])
    return first, last


def _attn_fwd(q, k, v, wsrcs, *, n_batch):
    t = q.shape[0]
    s, tq, nq = _attn_dims(t, n_batch)
    nw = len(wsrcs)

    def body(q_ref, k_ref, v_ref, *refs):
        w_refs, (o_ref, lse_ref), wall_refs = refs[:nw], refs[nw:nw + 2], refs[nw + 2:2 * nw + 2]
        send_sems, recv_sems, local_sems = refs[2 * nw + 2:]
        first, last = _grid_ends(3)

        @pl.when(first)
        def _():
            for i in range(nw):
                _gather_start(w_refs[i], wall_refs[i], send_sems, recv_sems, local_sems.at[i], 3 * i)

        @pl.when(last)
        def _():
            for i in range(nw):
                _gather_wait(w_refs[i], wall_refs[i], send_sems, recv_sems, local_sems.at[i], 3 * i)

        kv, vv = k_ref[...], v_ref[...]
        groups = _row_groups(tq)
        raw = [_dot_nt(q_ref[g, :], kv) for g in groups]
        m = [jnp.max(a, axis=-1, keepdims=True) for a in raw]
        p = [jnp.exp2((a - b) * SCALE_LOG2E) for a, b in zip(raw, m)]
        l = [jnp.sum(a, axis=-1, keepdims=True) for a in p]
        for g, pg, mg, lg in zip(groups, p, m, l):
            o_ref[g, :] = _dot(pg.astype(BF16), vv) / lg
            lse2 = mg * SCALE_LOG2E + jnp.log(lg) * LOG2E
            lse_ref[:, g] = jnp.transpose(jnp.broadcast_to(lse2, (lse2.shape[0], 128)))[0:1, :]

    return pl.pallas_call(
        body, name="attn_fwd", grid=(n_batch, B_HEADS, nq),
        in_specs=[pl.BlockSpec((tq, QK_PAD), lambda b, h, i: (b * nq + i, h)),
                  pl.BlockSpec((s, QK_PAD), lambda b, h, i: (b, h)),
                  pl.BlockSpec((s, B_V), lambda b, h, i: (b, h))] + [_HBM] * nw,
        out_specs=[pl.BlockSpec((tq, B_V), lambda b, h, i: (b * nq + i, h)),
                   pl.BlockSpec((None, 1, tq), lambda b, h, i: (h, 0, b * nq + i))] + [_HBM] * nw,
        out_shape=[jax.ShapeDtypeStruct((t, B_HEADS * B_V), F32), jax.ShapeDtypeStruct((B_HEADS, 1, t), F32)]
        + [jax.ShapeDtypeStruct((N_CHIPS,) + w.shape, w.dtype) for w in wsrcs],
        scratch_shapes=[pltpu.SemaphoreType.DMA((3 * nw,)), pltpu.SemaphoreType.DMA((3 * nw,)),
                        pltpu.SemaphoreType.DMA((nw,))],
        compiler_params=_params(3),
    )(q, k, v, *wsrcs)


KEY_PART = 512


def _attn_bwd(q, k, v, do16, lse, delta, gsegs, *, n_batch):
    t = q.shape[0]
    s = t // n_batch
    tk = _pick(s, 4 * KEY_PART, 128)
    nk = s // tk
    ng = len(gsegs)

    def body(q_ref, k_ref, v_ref, do_ref, lse_ref, dl_ref, *refs):
        g_refs, (dqt_ref, dk_ref, dv_ref), recv_refs = refs[:ng], refs[ng:ng + 3], refs[ng + 3:2 * ng + 3]
        send_sems, recv_sems = refs[2 * ng + 3:]
        first, last = _grid_ends(3)

        @pl.when(first)
        def _():
            for i in range(ng):
                _scatter_start(g_refs[i], recv_refs[i], send_sems, recv_sems, 3 * i)

        @pl.when(last)
        def _():
            for i in range(ng):
                _scatter_wait(g_refs[i], recv_refs[i], send_sems, recv_sems, 3 * i)

        j = pl.program_id(2)
        qv, dov = q_ref[...], do_ref[...]
        dst = []
        for g in [slice(r, r + min(tk, KEY_PART)) for r in range(0, tk, min(tk, KEY_PART))]:
            pt = jnp.exp2(_dot_nt(k_ref[g, :], qv) * SCALE_LOG2E - lse_ref[...])
            dv_ref[g, :] = _dot(pt.astype(BF16), dov)
            dpt = _dot_nt(v_ref[g, :], dov)
            dst.append((pt * (dpt - dl_ref[...])).astype(BF16))
            dk_ref[g, :] = _dot(dst[-1], qv) * ATTN_SCALE
        part = _dot_tn(k_ref[...], jnp.concatenate(dst, axis=0))

        @pl.when(j == 0)
        def _():
            dqt_ref[...] = part

        @pl.when(j != 0)
        def _():
            dqt_ref[...] += part

        @pl.when(j == nk - 1)
        def _():
            dqt_ref[...] = dqt_ref[...] * ATTN_SCALE

    row = pl.BlockSpec((None, 1, s), lambda b, h, j: (h, 0, b))
    return pl.pallas_call(
        body, name="attn_bwd", grid=(n_batch, B_HEADS, nk),
        in_specs=[pl.BlockSpec((s, QK_PAD), lambda b, h, j: (b, h)),
                  pl.BlockSpec((tk, QK_PAD), lambda b, h, j: (b * nk + j, h)),
                  pl.BlockSpec((tk, B_V), lambda b, h, j: (b * nk + j, h)),
                  pl.BlockSpec((s, B_V), lambda b, h, j: (b, h)), row, row] + [_HBM] * ng,
        out_specs=[pl.BlockSpec((QK_PAD, s), lambda b, h, j: (h, b)),
                   pl.BlockSpec((tk, QK_PAD), lambda b, h, j: (b * nk + j, h)),
                   pl.BlockSpec((tk, B_V), lambda b, h, j: (b * nk + j, h))] + [_HBM] * ng,
        out_shape=[jax.ShapeDtypeStruct((B_HEADS * QK_PAD, t), F32), jax.ShapeDtypeStruct((t, B_HEADS * QK_PAD), F32),
                   jax.ShapeDtypeStruct((t, B_HEADS * B_V), F32)]
        + [jax.ShapeDtypeStruct((3,) + g.shape[1:], g.dtype) for g in gsegs],
        scratch_shapes=[pltpu.SemaphoreType.DMA((3 * ng,)), pltpu.SemaphoreType.DMA((3 * ng,))],
        compiler_params=_params(3),
    )(q, k, v, do16, lse, delta, *gsegs)


def _group_ones16():
    r = _iota2((A_WIDTH, A_WIDTH), 0) // 64
    c = _iota2((A_WIDTH, A_WIDTH), 1) // 64
    return (r == c).astype(BF16)


def _head_rms(o, ones16):
    return lax.rsqrt(_exact_right(o * o, ones16) * (1.0 / 64.0) + EPS)


def _out_fwd(o_f, o_r, hg, o_attn, x, g_hn, g_mla, w_out, g2):
    t = x.shape[0]
    tm = _row_tile(t, 4 * ROW_GROUP)

    def body(of_ref, or_ref, hg_ref, oa_ref, x_ref, ghn_ref, gm_ref, w_ref, g2_ref, y_ref, x2_ref, h2_ref):
        ones16 = _group_ones16()
        ghn, gm, g2v = ghn_ref[...], gm_ref[...], g2_ref[...]

        def group(ofv, orv, hgv, oav, xv):
            o = ofv + orv
            ya16 = (((o * _head_rms(o, ones16)) * ghn) * (hgv * _sigmoid(hgv))).astype(BF16)
            yb16 = _rms_fwd(oav, gm).astype(BF16)
            x2 = xv + _dot(ya16, w_ref[0:A_WIDTH, :]) + _dot(yb16, w_ref[A_WIDTH:D_MODEL, :])
            return ya16, yb16, x2, _rms_fwd(x2, g2v).astype(BF16)

        ins = (of_ref, or_ref, hg_ref, oa_ref, x_ref)
        groups = _row_groups(tm)
        done = [group(*vals) for vals in [[ref[g, :] for ref in ins] for g in groups]]
        for g, (ya16, yb16, x2, h2) in zip(groups, done):
            y_ref[g, 0:A_WIDTH] = ya16
            y_ref[g, A_WIDTH:D_MODEL] = yb16
            x2_ref[g, :] = x2
            h2_ref[g, :] = h2

    a512 = _rows(tm, A_WIDTH)
    return pl.pallas_call(
        body, name="out_fwd", grid=(t // tm,),
        in_specs=[a512, a512, a512, a512, _rows(tm, D_MODEL), _full((1, A_WIDTH)), _full((1, A_WIDTH)),
                  _full((D_MODEL, D_MODEL)), _full((1, D_MODEL))],
        out_specs=[_rows(tm, D_MODEL)] * 3,
        out_shape=[jax.ShapeDtypeStruct((t, D_MODEL), BF16), jax.ShapeDtypeStruct((t, D_MODEL), F32),
                   jax.ShapeDtypeStruct((t, D_MODEL), BF16)],
        compiler_params=_params(1),
    )(o_f, o_r, hg, o_attn, x, g_hn, g_mla, w_out, g2)


def _out_bwd(dx2_16, o_f, o_r, hg, o_attn, g_hn, g_mla, w_out):
    t = dx2_16.shape[0]
    tm = _row_tile(t, 4 * ROW_GROUP)

    def body(dx_ref, of_ref, or_ref, hg_ref, oa_ref, ghn_ref, gm_ref, w_ref,
             do_ref, dhg_ref, doa_ref, dl_ref, dghn_ref, dgm_ref):
        first = pl.program_id(0) == 0
        ones16 = _group_ones16()
        sel16 = (_iota2((8, A_WIDTH), 0) == _iota2((8, A_WIDTH), 1) // B_V).astype(BF16)
        ghn, gm = ghn_ref[...], gm_ref[...]

        def group(dxv, ofv, orv, hgv, oav):
            dya = _dot_nt(dxv, w_ref[0:A_WIDTH, :])
            dyb = _dot_nt(dxv, w_ref[A_WIDTH:D_MODEL, :])
            o = ofv + orv
            rh = _head_rms(o, ones16)
            oh = o * rh
            sg = _sigmoid(hgv)
            sl = hgv * sg
            dhg = ((dya * (oh * ghn)) * (sg * (1.0 + hgv * (1.0 - sg)))).astype(BF16)
            doh = dya * sl * ghn
            do = rh * (doh - oh * (_exact_right(doh * oh, ones16) * (1.0 / 64.0)))
            doa, dgm = _rms_bwd(dyb, oav, gm)
            x1, x2, x3 = _split3(doa * oav)
            delta = _dot_nt(sel16, x1) + _dot_nt(sel16, x2) + _dot_nt(sel16, x3)
            return do, dhg, doa.astype(BF16), delta, dya * sl * oh, dgm

        groups = _row_groups(tm)
        loaded = [(dx_ref[g, :], of_ref[g, :], or_ref[g, :], hg_ref[g, :], oa_ref[g, :]) for g in groups]
        done = [group(*vals) for vals in loaded]
        for g, (do, dhg, doa16, delta, _, _) in zip(groups, done):
            do_ref[g, :] = do
            dhg_ref[g, :] = dhg
            doa_ref[g, :] = doa16
            for h in range(B_HEADS):
                dl_ref[h, :, g] = delta[h:h + 1, :]
        _acc_rows(dghn_ref, jnp.concatenate([d[4] for d in done], axis=0), first)
        _acc_rows(dgm_ref, jnp.concatenate([d[5] for d in done], axis=0), first)

    a512 = _rows(tm, A_WIDTH)
    return pl.pallas_call(
        body, name="out_bwd", grid=(t // tm,),
        in_specs=[_rows(tm, D_MODEL), a512, a512, a512, a512, _full((1, A_WIDTH)), _full((1, A_WIDTH)),
                  _full((D_MODEL, D_MODEL))],
        out_specs=[a512, a512, a512, pl.BlockSpec((B_HEADS, 1, tm), lambda i: (0, 0, i)),
                   _full((1, A_WIDTH)), _full((1, A_WIDTH))],
        out_shape=[jax.ShapeDtypeStruct((t, A_WIDTH), F32)] + [jax.ShapeDtypeStruct((t, A_WIDTH), BF16)] * 2
        + [jax.ShapeDtypeStruct((B_HEADS, 1, t), F32)]
        + [jax.ShapeDtypeStruct((1, A_WIDTH), F32)] * 2,
        compiler_params=_params(1),
    )(dx2_16, o_f, o_r, hg, o_attn, g_hn, g_mla, w_out)


def _ffn_fwd_bwd(h2, x2, target, w_gate, w_up, w_down, g_f, g2):
    t = x2.shape[0]
    tm = _row_tile(t)
    inv_d = 1.0 / D_MODEL

    def body(h2_ref, x2_ref, tg_ref, wg_ref, wu_ref, wd_ref, gf_ref, g2_ref,
             act_ref, dgate_ref, dup_ref, dx3_ref, dx2_ref, dx2h_ref, loss_ref, dgf_ref, dg2_ref):
        first = pl.program_id(0) == 0
        h2v = h2_ref[...]
        gate = _dot_nt(h2v, wg_ref[...])
        up = _dot_nt(h2v, wu_ref[...])
        sg = _sigmoid(gate)
        sl = gate * sg
        act16 = (sl * up).astype(BF16)
        act_ref[...] = act16
        x2v = x2_ref[...]
        x3 = x2v + _dot(act16, wd_ref[...])
        r3 = lax.rsqrt(jnp.mean(x3 * x3, axis=-1, keepdims=True) + EPS)
        x3h = x3 * r3
        gf = gf_ref[...]
        err = x3h * gf - tg_ref[...]
        part = 0.5 * jnp.sum(jnp.mean(err * err, axis=-1, keepdims=True), axis=0, keepdims=True)

        @pl.when(first)
        def _():
            loss_ref[...] = jnp.zeros_like(loss_ref)

        loss_ref[...] += part
        dy = err * inv_d
        _acc_rows(dgf_ref, dy * x3h, first)
        dx3h = dy * gf
        dx3 = r3 * (dx3h - x3h * jnp.mean(dx3h * x3h, axis=-1, keepdims=True))
        dx3_16 = dx3.astype(BF16)
        dx3_ref[...] = dx3_16
        da = _dot_nt(dx3_16, wd_ref[...])
        dup16 = (da * sl).astype(BF16)
        dgate16 = (da * up * (sg * (1.0 + gate * (1.0 - sg)))).astype(BF16)
        dup_ref[...] = dup16
        dgate_ref[...] = dgate16
        dh2 = _dot(dgate16, wg_ref[...]) + _dot(dup16, wu_ref[...])
        dx2n, dg2 = _rms_bwd(dh2, x2v, g2_ref[...])
        _acc_rows(dg2_ref, dg2, first)
        dx2 = dx3 + dx2n
        dx2_ref[...] = dx2
        dx2h_ref[...] = dx2.astype(BF16)

    return pl.pallas_call(
        body, name="ffn_fwd_bwd", grid=(t // tm,),
        in_specs=[_rows(tm, D_MODEL), _rows(tm, D_MODEL), _rows(tm, D_MODEL), _full((D_FF, D_MODEL), True),
                  _full((D_FF, D_MODEL), True), _full((D_FF, D_MODEL), True), _full((1, D_MODEL)), _full((1, D_MODEL))],
        out_specs=[_rows(tm, D_FF), _rows(tm, D_FF), _rows(tm, D_FF), _rows(tm, D_MODEL), _rows(tm, D_MODEL),
                   _rows(tm, D_MODEL), _full((8, 128)), _full((1, D_MODEL)), _full((1, D_MODEL))],
        out_shape=[jax.ShapeDtypeStruct((t, D_FF), BF16)] * 3
        + [jax.ShapeDtypeStruct((t, D_MODEL), BF16), jax.ShapeDtypeStruct((t, D_MODEL), F32),
           jax.ShapeDtypeStruct((t, D_MODEL), BF16), jax.ShapeDtypeStruct((8, 128), F32),
           jax.ShapeDtypeStruct((1, D_MODEL), F32), jax.ShapeDtypeStruct((1, D_MODEL), F32)],
        compiler_params=_params(1),
    )(h2, x2, target, w_gate, w_up, w_down, g_f, g2)


def _wgrad(a, b, name):
    t, m = a.shape
    n = b.shape[1]
    bm = _pick(m, 1664)
    bn = _pick(n, 1664)
    tk = _pick(t, 2048, 16)
    nk = t // tk

    def body(a_ref, b_ref, o_ref, o16_ref):
        k = pl.program_id(2)
        part = _dot_tn(a_ref[...], b_ref[...])

        @pl.when(k == 0)
        def _():
            o_ref[...] = part

        @pl.when(k != 0)
        def _():
            o_ref[...] += part

        @pl.when(k == nk - 1)
        def _():
            o16_ref[...] = o_ref[...].astype(BF16)

    out = pl.BlockSpec((bm, bn), lambda i, j, k: (i, j))
    return pl.pallas_call(
        body, name=name, grid=(m // bm, n // bn, nk),
        in_specs=[pl.BlockSpec((tk, bm), lambda i, j, k: (k, i)), pl.BlockSpec((tk, bn), lambda i, j, k: (k, j))],
        out_specs=[out, out],
        out_shape=[jax.ShapeDtypeStruct((m, n), F32), jax.ShapeDtypeStruct((m, n), BF16)],
        compiler_params=_params(3),
    )(a, b)


def _rope_tables(seq):
    inv = 1.0 / (ROPE_THETA ** (jnp.arange(0, B_ROPE, 2, dtype=F32) / B_ROPE))
    ang = jnp.arange(seq, dtype=F32)[:, None] * inv[None, :]
    cos, sin = jnp.cos(ang), jnp.sin(ang)
    zeros = jnp.zeros((seq, 64), F32)
    return jnp.concatenate([cos, cos, zeros], axis=1), jnp.concatenate([-sin, sin, zeros], axis=1)


def _pad_weights(w_in_t, w_q_t, w_kv_b):
    w_in_p = jnp.pad(w_in_t, ((0, D_IN_PAD - D_IN), (0, 0)))
    w_q_p = jnp.pad(w_q_t.reshape(B_HEADS, B_NOPE + B_ROPE, Q_LORA), ((0, 0), (0, 64), (0, 0))).reshape(1024, Q_LORA)
    kv = w_kv_b.reshape(KV_LORA, B_HEADS, B_NOPE + B_V)
    w_kv_p = jnp.concatenate([kv[:, :, :B_NOPE].reshape(KV_LORA, 512), kv[:, :, B_NOPE:].reshape(KV_LORA, 512)], axis=1)
    return w_in_p, w_q_p, w_kv_p


def _unpad_kv(g_kv_p):
    return jnp.concatenate([g_kv_p[:, :512].reshape(KV_LORA, B_HEADS, B_NOPE),
                            g_kv_p[:, 512:].reshape(KV_LORA, B_HEADS, B_V)], axis=2).reshape(KV_LORA, 1024)


_OWN_ROWS = dict(w_in=(272, 3), w_q_b=(64, 4), w_kv_b=(256, 0), w_out=(256, 1), w_gate=(176, 4), w_up=(176, 4), w_down=(176, 4))


def _local_step(x, target, lbl, g1, g_hn, g_qa, g_kva, g_mla, g2, g_f, w_in, w_q_b, w_kv_b, late_shards):
    n_batch, seq, _ = x.shape
    t = n_batch * seq
    x = x.reshape(t, D_MODEL)
    target = target.reshape(t, D_MODEL)
    w_in_p, w_q_p, w_kv_p = _pad_weights(w_in, w_q_b, w_kv_b)
    cosx, sinx = _rope_tables(seq)

    h1, hq, hi, hff, hfb, hg, cq, ckv, kr = _in_fwd(x, g1, w_in_p)
    o_f, st_f, o_r, st_r = _hgrn_fwd(hq, hi, hff, hfb, lbl, n_batch=n_batch)
    q, k, v, cqn, ckvn = _mla_prep(cq, ckv, kr, g_qa, g_kva, w_q_p, w_kv_p, cosx, sinx, n_batch=n_batch)
    o_attn, lse, *gathered = _attn_fwd(q, k, v, late_shards, n_batch=n_batch)
    w_out, w_gate, w_up, w_down = (_join_shards(n, a) for n, a in zip(_LATE, gathered))
    ycat, x2, h2 = _out_fwd(o_f, o_r, hg, o_attn, x, g_hn, g_mla, w_out, g2)
    act, dgate, dup, dx3_16, dx2, dx2_16, loss, dg_f, dg2 = _ffn_fwd_bwd(h2, x2, target, w_gate, w_up, w_down, g_f, g2)
    full = dict(w_out=_wgrad(ycat, dx2_16, "wgrad_out"), w_gate=_wgrad(dgate, h2, "wgrad_gate"),
                w_up=_wgrad(dup, h2, "wgrad_up"), w_down=_wgrad(act, dx3_16, "wgrad_down"))
    do_h, dhg, do_attn16, delta, dg_hn, dg_mla = _out_bwd(dx2_16, o_f, o_r, hg, o_attn, g_hn, g_mla, w_out)
    dqt, dk, dv, *recv_late = _attn_bwd(q, k, v, do_attn16, lse, delta,
                                        [_segments(n, full[n][1]) for n in _LATE], n_batch=n_batch)
    dcq, dckv, dkr, dqp16, dkvp16, dg_qa, dg_kva = _mla_prep_bwd(dqt, dk, dv, cq, ckv, g_qa, g_kva, w_q_p, w_kv_p,
                                                                  cosx, sinx, n_batch=n_batch)
    dq_f, dv_f, dz_f, dl_f = _hgrn_bwd(hq, hi, hff, do_h, st_f, lbl, n_batch=n_batch, direction=0)
    dq_r, dv_r, dz_r, dl_r = _hgrn_bwd(hq, hi, hfb, do_h, st_r, lbl, n_batch=n_batch, direction=1)
    dx, dproj16, dg1 = _in_bwd(dq_f, dq_r, dv_f, dv_r, dz_f, dz_r, dhg, hq, dcq, dckv, dkr, dx2, x, g1, w_in_p)

    full.update(w_in=_wgrad(dproj16, h1, "wgrad_in"), w_q_b=_wgrad(dqp16, cqn, "wgrad_q_b"))
    g_kv = _unpad_kv(_wgrad(ckvn, dkvp16, "wgrad_kv_b")[0])
    early16 = dict(w_in=full["w_in"][1][:D_IN].reshape(N_CHIPS, D_IN // N_CHIPS, D_MODEL),
                   w_q_b=full["w_q_b"][1].reshape(B_HEADS, QK_PAD, Q_LORA)[:, :B_NOPE + B_ROPE],
                   w_kv_b=_segments("w_kv_b", g_kv).astype(BF16))
    small = dict(norm1_g=dg1, lb_logits=jnp.stack([dl_f, dl_r]), hgrn_norm_g=dg_hn, q_a_norm_g=dg_qa, kv_a_norm_g=dg_kva,
                 mla_norm_g=dg_mla, norm2_g=dg2, final_norm_g=dg_f, loss=loss[0:1, 0])
    return dx.reshape(n_batch, seq, D_MODEL), small, {**{n: f for n, (f, _) in full.items()}, "w_kv_b": g_kv}, early16, \
        dict(zip(_LATE, recv_late))


_HBM = pl.BlockSpec(memory_space=pltpu.HBM)
_MESH = pl.DeviceIdType.MESH


def _place():
    x, y, c = lax.axis_index("x"), lax.axis_index("y"), lax.axis_index("c")
    other_chips = [(1 - x, y), (x, 1 - y), (1 - x, 1 - y)]
    return x, y, c, other_chips


def _gather_copies(w_ref, wall_ref, send_sems, recv_sems, local_sem, base=0):
    x, y, c, chips = _place()
    mine = 2 * x + y

    def mk(j, chip_index, to):
        return pltpu.make_async_remote_copy(src_ref=w_ref, dst_ref=wall_ref.at[chip_index], send_sem=send_sems.at[base + j],
                                            recv_sem=recv_sems.at[base + j], device_id=to, device_id_type=_MESH)

    local = pltpu.make_async_copy(w_ref, wall_ref.at[mine], local_sem)
    sends = [mk(j, mine, (*chip, c)) for j, chip in enumerate(chips)]
    recvs = [mk(j, 2 * px + py, (x, y, c)) for j, (px, py) in enumerate(chips)]
    return local, sends, recvs


def _gather_start(*refs):
    local, sends, _ = _gather_copies(*refs)
    local.start()
    for cp in sends:
        cp.start()


def _gather_wait(*refs):
    local, sends, recvs = _gather_copies(*refs)
    for cp in recvs:
        cp.wait_recv()
    for cp in sends:
        cp.wait_send()
    local.wait()


def _scatter_copies(g_ref, recv_ref, send_sems, recv_sems, base=0):
    x, y, c, chips = _place()

    def mk(j, src_index, to):
        return pltpu.make_async_remote_copy(src_ref=g_ref.at[src_index], dst_ref=recv_ref.at[j],
                                            send_sem=send_sems.at[base + j], recv_sem=recv_sems.at[base + j],
                                            device_id=to, device_id_type=_MESH)

    sends = [mk(j, 2 * px + py, (px, py, c)) for j, (px, py) in enumerate(chips)]
    recvs = [mk(j, 0, (x, y, c)) for j in range(3)]
    return sends, recvs


def _scatter_start(*refs):
    for cp in _scatter_copies(*refs)[0]:
        cp.start()


def _scatter_wait(*refs):
    sends, recvs = _scatter_copies(*refs)
    for cp in recvs:
        cp.wait_recv()
    for cp in sends:
        cp.wait_send()


def _half(ref, which, axis):
    n = ref.shape[axis] // 2
    idx = [slice(None)] * len(ref.shape)
    idx[axis] = pl.ds(which * n, n)
    return ref.at[tuple(idx)]


def _split_axis(a):
    return 1 if a.shape[1] % 256 == 0 else 0


def _gather_halves(src_ref, dst_ref, send_sems, recv_sems, local_sem, base, axis):
    x, y, c, chips = _place()
    mine = 2 * x + y

    def remote(src, dst, n, to):
        return pltpu.make_async_remote_copy(src_ref=src, dst_ref=dst, send_sem=send_sems.at[base + n],
                                            recv_sem=recv_sems.at[base + n], device_id=to, device_id_type=_MESH)

    def slot(chip_index, which):
        return _half(dst_ref.at[chip_index], which, axis)

    local = pltpu.make_async_copy(src_ref, dst_ref.at[mine], local_sem)
    first = [remote(_half(src_ref, c, axis), slot(mine, c), j, (*chip, c)) for j, chip in enumerate(chips)]
    landed = [remote(_half(src_ref, c, axis), slot(2 * px + py, c), j, (x, y, c)) for j, (px, py) in enumerate(chips)]
    passed = [remote(slot(2 * px + py, c), slot(2 * px + py, c), 3 + j, (x, y, 1 - c)) for j, (px, py) in enumerate(chips)]
    handed = [remote(_half(src_ref, c, axis), slot(2 * px + py, 1 - c), 3 + j, (x, y, c)) for j, (px, py) in enumerate(chips)]

    def start():
        local.start()
        for cp in first:
            cp.start()

    def pass_on():
        for arrived, onward in zip(landed, passed):
            arrived.wait_recv()
            onward.start()

    def finish():
        for cp in handed:
            cp.wait_recv()
        for cp in first + passed:
            cp.wait_send()
        local.wait()

    return start, pass_on, finish


def _gather_shards(srcs, small):
    n = len(srcs)

    def body(*refs):
        src_refs, s_ref, dst_refs, sall_ref = refs[:n], refs[n], refs[n + 1:2 * n + 1], refs[2 * n + 1]
        send_sems, recv_sems, local_sems = refs[2 * n + 2:]
        _gather_start(s_ref, sall_ref, send_sems, recv_sems, local_sems.at[n], 6 * n)
        steps = [_gather_halves(src_refs[i], dst_refs[i], send_sems, recv_sems, local_sems.at[i], 6 * i, _split_axis(srcs[i]))
                 for i in range(n)]
        for phase in range(3):
            for step in steps:
                step[phase]()
        _gather_wait(s_ref, sall_ref, send_sems, recv_sems, local_sems.at[n], 6 * n)

    return pl.pallas_call(
        body, name="gather_shards", in_specs=[_HBM] * (n + 1), out_specs=[_HBM] * (n + 1),
        out_shape=[jax.ShapeDtypeStruct((N_CHIPS,) + a.shape, a.dtype) for a in list(srcs) + [small]],
        scratch_shapes=[pltpu.SemaphoreType.DMA((6 * n + 3,)), pltpu.SemaphoreType.DMA((6 * n + 3,)),
                        pltpu.SemaphoreType.DMA((n + 1,))],
    )(*srcs, small)


_SEM = pl.BlockSpec(memory_space=pltpu.SEMAPHORE)
_SIDE_EFFECT = pltpu.SideEffectType.DATAFLOW_SIDE_EFFECTING


def _scatter_begin(gsegs):
    n = len(gsegs)
    def hbm(a):
        return pltpu.with_memory_space_constraint(a, pltpu.HBM)

    lands = [hbm(lax.empty((3,) + g.shape[1:], g.dtype)) for g in gsegs]

    def body(*refs):
        g_refs, land_refs, send_sems, recv_sems, token = refs[:n], refs[n:2 * n], refs[2 * n], refs[2 * n + 1], refs[-1]
        for i in range(n):
            _scatter_start(g_refs[i], land_refs[i], send_sems, recv_sems, 3 * i)
        token[...] = jnp.zeros_like(token)

    out = pl.pallas_call(
        body, name="scatter_begin", in_specs=[_HBM] * (2 * n),
        out_specs=(_SEM, _SEM) + (_HBM,) * (2 * n) + (pl.BlockSpec(memory_space=pltpu.VMEM),),
        out_shape=(pltpu.SemaphoreType.DMA((3 * n,)), pltpu.SemaphoreType.DMA((3 * n,)))
        + tuple(pltpu.HBM(a.shape, a.dtype) for a in list(gsegs) + lands) + (jax.ShapeDtypeStruct((8, 128), F32),),
        input_output_aliases={i: 2 + i for i in range(2 * n)},
        compiler_params=pltpu.CompilerParams(has_side_effects=_SIDE_EFFECT),
    )(*[hbm(g) for g in gsegs], *lands)
    return out[0], out[1], out[2:2 + n], out[2 + n:2 + 2 * n], out[-1]


def _scatter_end(send_sems, recv_sems, gsegs, lands, after):
    n = len(gsegs)

    def body(*refs):
        g_refs, land_refs, ssems, rsems = refs[:n], refs[n:2 * n], refs[2 * n], refs[2 * n + 1]
        for i in range(n):
            _scatter_wait(g_refs[i], land_refs[i], ssems, rsems, 3 * i)

    out = pl.pallas_call(
        body, name="scatter_end", in_specs=[_HBM] * (2 * n) + [_SEM, _SEM] + [pl.BlockSpec(memory_space=pl.ANY)] * len(after),
        out_specs=(_HBM,) * (2 * n), out_shape=tuple(pltpu.HBM(a.shape, a.dtype) for a in list(gsegs) + list(lands)),
        input_output_aliases={i: i for i in range(2 * n)},
        compiler_params=pltpu.CompilerParams(has_side_effects=_SIDE_EFFECT),
    )(*gsegs, *lands, send_sems, recv_sems, *after)
    return out[n:]


def _swap_sibling(parts, small=None):
    n = len(parts)
    extra = small is not None

    def body(*refs):
        ins, outs = refs[:n + extra], refs[n + extra:2 * (n + extra)]
        send_sems, recv_sems, local_sem = refs[2 * (n + extra):]
        x, y, c, _ = _place()
        cps = [pltpu.make_async_remote_copy(src_ref=ins[i], dst_ref=outs[i], send_sem=send_sems.at[i],
                                            recv_sem=recv_sems.at[i], device_id=(x, y, 1 - c), device_id_type=_MESH)
               for i in range(n)]
        for cp in cps:
            cp.start()
        if extra:
            s_ref, sall_ref = ins[n], outs[n]
            me = 4 * x + 2 * y + c
            flips = [(fx, fy, fc) for fx in (0, 1) for fy in (0, 1) for fc in (0, 1)][1:]

            def peer(f):
                return tuple((1 - a) if b else a for a, b in zip((x, y, c), f))

            def sm(r, index, to):
                return pltpu.make_async_remote_copy(src_ref=s_ref, dst_ref=sall_ref.at[index], send_sem=send_sems.at[n + r],
                                                    recv_sem=recv_sems.at[n + r], device_id=to, device_id_type=_MESH)

            local = pltpu.make_async_copy(s_ref, sall_ref.at[me], local_sem)
            local.start()
            sends = [sm(r, me, peer(f)) for r, f in enumerate(flips)]
            for cp in sends:
                cp.start()
            for r, f in enumerate(flips):
                px, py, pc = peer(f)
                sm(r, 4 * px + 2 * py + pc, (x, y, c)).wait_recv()
            for cp in sends:
                cp.wait_send()
            local.wait()
        for cp in cps:
            cp.wait()

    return pl.pallas_call(
        body, name="swap_small" if extra else "swap_sibling", in_specs=[_HBM] * (n + extra), out_specs=[_HBM] * (n + extra),
        out_shape=[jax.ShapeDtypeStruct(p.shape, p.dtype) for p in parts]
        + ([jax.ShapeDtypeStruct((8,) + small.shape, F32)] if extra else []),
        scratch_shapes=[pltpu.SemaphoreType.DMA((n + 7,)), pltpu.SemaphoreType.DMA((n + 7,)), pltpu.SemaphoreType.DMA(())],
    )(*parts, *([small] if extra else []))


def _sum_segments(chip, full, recv, tm, pitch, name, after=()):
    _, rows, cols = recv.shape

    def body(chip_ref, o_ref, r_ref, *rest):
        acc = o_ref[...]
        for j in range(3):
            acc = acc + r_ref[j].astype(F32)
        rest[-1][...] = acc

    return pl.pallas_call(
        body, name=name,
        grid_spec=pltpu.PrefetchScalarGridSpec(
            num_scalar_prefetch=1, grid=(rows // tm,),
            in_specs=[pl.BlockSpec((tm, cols), lambda i, c: (c[0] * pitch + i, 0)),
                      pl.BlockSpec((3, tm, cols), lambda i, c: (0, i, 0))]
            + [pl.BlockSpec(a.shape, lambda i, c, nd=a.ndim: (0,) * nd) for a in after],
            out_specs=pl.BlockSpec((tm, cols), lambda i, c: (i, 0))),
        out_shape=jax.ShapeDtypeStruct((rows, cols), F32), compiler_params=_params(1),
    )(chip, full, recv, *after)


def _sum_devices(sall):
    def body(s_ref, o_ref):
        acc = s_ref[0]
        for d in range(1, 8):
            acc = acc + s_ref[d]
        o_ref[...] = acc

    return pl.pallas_call(body, name="sum_devices", out_shape=jax.ShapeDtypeStruct(sall.shape[1:], F32))(sall)


def _adamw(w, m, v, ga, gb, name):
    rows, cols = w.shape
    tm = _row_tile(rows)
    two = gb is not None

    def body(*refs):
        w_ref, m_ref, v_ref, ga_ref = refs[:4]
        g_ref, d_ref, m2_ref, v2_ref = refs[-4:]
        g = ga_ref[...] + refs[4][...] if two else ga_ref[...]
        m2 = ADAM_B1 * m_ref[...] + (1.0 - ADAM_B1) * g
        v2 = ADAM_B2 * v_ref[...] + (1.0 - ADAM_B2) * (g * g)
        m_hat = m2 / (1.0 - ADAM_B1 ** ADAM_STEP)
        v_hat = v2 / (1.0 - ADAM_B2 ** ADAM_STEP)
        g_ref[...] = g
        d_ref[...] = -ADAM_LR * (m_hat / (jnp.sqrt(v_hat) + ADAM_EPS) + ADAM_WD * w_ref[...])
        m2_ref[...] = m2
        v2_ref[...] = v2

    blk = _rows(tm, cols)
    args = (w, m, v, ga) + ((gb,) if two else ())
    return pl.pallas_call(
        body, name=name, grid=(rows // tm,), in_specs=[blk] * len(args), out_specs=[blk] * 4,
        out_shape=[jax.ShapeDtypeStruct(w.shape, F32)] * 4, compiler_params=_params(1),
    )(*args)


_TRANSPOSED = ("w_in", "w_q_b", "w_gate", "w_up")
_COLUMN_SHARDED = ("w_kv_b",)
_FULL_SHAPES = dict(w_in=(D_IN, D_MODEL), w_q_b=(768, Q_LORA), w_kv_b=(KV_LORA, 1024), w_out=(D_MODEL, D_MODEL),
                    w_gate=(D_FF, D_MODEL), w_up=(D_FF, D_MODEL), w_down=(D_FF, D_MODEL))
_SMALL = (("norm1_g", 1024), ("lb_logits", 2048), ("hgrn_norm_g", 512), ("q_a_norm_g", 384), ("kv_a_norm_g", 256),
          ("mla_norm_g", 512), ("norm2_g", 1024), ("final_norm_g", 1024))
_UPDATE_ROWS = 48


def _pad_rows(a, rows):
    return jnp.pad(a, ((0, rows - a.shape[0]), (0, 0)))


_EARLY = ("w_in", "w_q_b", "w_kv_b")
_LATE = ("w_out", "w_gate", "w_up", "w_down")


def _segments(name, g):
    r, c = g.shape
    if name in _COLUMN_SHARDED:
        return g.reshape(r, N_CHIPS, c // N_CHIPS).transpose(1, 0, 2)
    return g.reshape(N_CHIPS, r // N_CHIPS, c)


def _own_segment(name, g, chip):
    r, c = g.shape
    if name in _COLUMN_SHARDED:
        return lax.dynamic_slice_in_dim(g, chip * (c // N_CHIPS), c // N_CHIPS, axis=1)
    return lax.dynamic_slice_in_dim(g, chip * (r // N_CHIPS), r // N_CHIPS, axis=0)


def _join_shards(name, seg):
    r, c = _FULL_SHAPES[name]
    if name in _COLUMN_SHARDED:
        return seg.transpose(1, 0, 2).reshape(r, c)
    return seg.reshape(r, c)


def kernel(x, norm1_g, w_in, lb_logits, hgrn_norm_g, q_a_norm_g, w_q_b, kv_a_norm_g, w_kv_b, mla_norm_g, w_out, norm2_g, w_gate, w_up, w_down, final_norm_g, loss_target, m_norm1_g, m_w_in, m_lb_logits, m_hgrn_norm_g, m_q_a_norm_g, m_w_q_b, m_kv_a_norm_g, m_w_kv_b, m_mla_norm_g, m_w_out, m_norm2_g, m_w_gate, m_w_up, m_w_down, m_final_norm_g, v_norm1_g, v_w_in, v_lb_logits, v_hgrn_norm_g, v_q_a_norm_g, v_w_q_b, v_kv_a_norm_g, v_w_kv_b, v_mla_norm_g, v_w_out, v_norm2_g, v_w_gate, v_w_up, v_w_down, v_final_norm_g):
    names = ("norm1_g", "w_in", "lb_logits", "hgrn_norm_g", "q_a_norm_g", "w_q_b", "kv_a_norm_g", "w_kv_b", "mla_norm_g",
             "w_out", "norm2_g", "w_gate", "w_up", "w_down", "final_norm_g")
    w = dict(zip(names, (norm1_g, w_in, lb_logits, hgrn_norm_g, q_a_norm_g, w_q_b, kv_a_norm_g, w_kv_b, mla_norm_g,
                         w_out, norm2_g, w_gate, w_up, w_down, final_norm_g)))
    m = dict(zip(names, (m_norm1_g, m_w_in, m_lb_logits, m_hgrn_norm_g, m_q_a_norm_g, m_w_q_b, m_kv_a_norm_g, m_w_kv_b,
                         m_mla_norm_g, m_w_out, m_norm2_g, m_w_gate, m_w_up, m_w_down, m_final_norm_g)))
    v = dict(zip(names, (v_norm1_g, v_w_in, v_lb_logits, v_hgrn_norm_g, v_q_a_norm_g, v_w_q_b, v_kv_a_norm_g, v_w_kv_b,
                         v_mla_norm_g, v_w_out, v_norm2_g, v_w_gate, v_w_up, v_w_down, v_final_norm_g)))
    matrices = _EARLY + _LATE
    chip = 2 * lax.axis_index("x") + lax.axis_index("y")

    def shard2d(a, n):
        return jnp.swapaxes(a[0], 0, 1) if n in _TRANSPOSED else a[0]

    def unshard2d(a, n):
        return (jnp.swapaxes(a, 0, 1) if n in _TRANSPOSED else a)[None]

    w16 = {n: shard2d(w[n], n).astype(BF16) for n in matrices}
    lb8 = _pad_rows(lb_logits.reshape(4, 128), 8)
    *early, lball = _gather_shards([w16[n] for n in _EARLY], lb8)
    lbl = lball[:, :4].reshape(N_CHIPS, 2, 2, 128).transpose(1, 2, 0, 3).reshape(2, 2, A_WIDTH)

    grad_x, g, full, early16, recv = _local_step(
        x, loss_target, lbl, norm1_g, hgrn_norm_g, q_a_norm_g, kv_a_norm_g, mla_norm_g, norm2_g, final_norm_g[None, :],
        *(_join_shards(n, a) for n, a in zip(_EARLY, early)), [w16[n] for n in _LATE])

    n_small = sum(size for _, size in _SMALL)
    small = jnp.concatenate([g[n].reshape(-1) for n, _ in _SMALL] + [g["loss"], jnp.zeros((SMALL_ROWS * 128 - n_small - 1,), F32)])
    chip1 = chip.reshape(1).astype(jnp.int32)
    out, raw = {}, {}

    def update(n, part, sib):
        raw[n] = _adamw(shard2d(w[n], n), shard2d(m[n], n), shard2d(v[n], n), part, sib, f"adamw_{n}")
        out[n] = tuple(unshard2d(r, n) for r in raw[n])

    send_sems, recv_sems, segs, lands, token = _scatter_begin([early16[n] for n in _EARLY])
    parts = [_sum_segments(chip1, full[n], recv[n], *_OWN_ROWS[n], f"sum_{n}", after=(token,)) for n in _LATE]
    for n, part, sib in zip(_LATE, parts, _swap_sibling(parts)):
        update(n, part, sib)
    recv_early = _scatter_end(send_sems, recv_sems, segs, lands, after=[raw[n][1] for n in _LATE])
    full["w_kv_b"] = _own_segment("w_kv_b", full["w_kv_b"], chip)
    parts = [_sum_segments(chip1, full[n], r, *_OWN_ROWS[n], f"sum_{n}") for n, r in zip(_EARLY, recv_early)]
    *sibs, small_all = _swap_sibling(parts, small.reshape(SMALL_ROWS, 128))
    for n, part, sib in zip(_EARLY, parts, sibs):
        update(n, part, sib)
    small_sum = _sum_devices(small_all).reshape(-1)
    loss = small_sum[n_small]

    small_g, off = {}, 0
    for n, size in _SMALL:
        small_g[n] = small_sum[off:off + size]
        off += size
    small_g["lb_logits"] = lax.dynamic_slice_in_dim(small_g["lb_logits"].reshape(2, 2, A_WIDTH), chip * 128, 128, axis=2)
    small_names = tuple(n for n, _ in _SMALL)

    def pack(d):
        return _pad_rows(jnp.concatenate([d[n].reshape(-1) for n in small_names]).reshape(-1, 128), _UPDATE_ROWS)

    res = _adamw(pack(w), pack(m), pack(v), pack(small_g), None, "adamw_small")
    off = 0
    flat = [r.reshape(-1) for r in res]
    for n in small_names:
        size = w[n].size
        out[n] = tuple(f[off:off + size].reshape(w[n].shape) for f in flat)
        off += size

    return (loss, grad_x) + tuple(out[n][i] for i in range(4) for n in names)
```
